```python
import math
import numpy as np
import jax
import jax.numpy as jnp
from jax import lax

D_MODEL = 1024
BATCH = 32
SEQ = 256
DEPTH = 2
DEC_BATCH = 4
DEC_SEQ = 2048
PAST_LEN = 512

GRID_W = 64
HEAD_DIM = 64
NA_HEADS = 8
NA_WIN_H = 8
NA_WIN_W = 16
NA_QB_W = 16
NA_KSPAN = NA_QB_W + NA_WIN_W
GQA_Q_HEADS = 8
GQA_KV_HEADS = 2
GQA_GROUP = GQA_Q_HEADS // GQA_KV_HEADS
SWA_WINDOW = 128
SWA_BLOCK = 128
ROPE_THETA = 10000.0
S5_CH = 512
S5_GROUP_CH = 16
S5_GROUPS = S5_CH // S5_GROUP_CH
S5_STATE = 64
S5_DT_MIN = 1e-3
S5_DT_MAX = 1e-1
N_BRANCH = 3
NA_WIDTH = NA_HEADS * HEAD_DIM
GQA_Q_WIDTH = GQA_Q_HEADS * HEAD_DIM
GQA_KV_WIDTH = GQA_KV_HEADS * HEAD_DIM
IN_COLS = 3 * NA_WIDTH + GQA_Q_WIDTH + 2 * GQA_KV_WIDTH + S5_CH + N_BRANCH * D_MODEL
MOE_GROUPS = 4
MOE_EXPERTS_PER_GROUP = 4
MOE_EXPERTS = MOE_GROUPS * MOE_EXPERTS_PER_GROUP
MOE_TOP_K = 2
EXPERT_FF = 512
ATTN_BLOCK_Q = 128
EPS = 1e-6
NEG_INF = -1e30

kernel_name = "hybrid_diffusion_prefix_trunk_step"


def rmsnorm(x, g):
    xf = x.astype(jnp.float32)
    y = xf * lax.rsqrt(jnp.mean(xf * xf, axis=-1, keepdims=True) + EPS)
    return (y * g.astype(jnp.float32)).astype(x.dtype)


def ada_mod(cond, w_ada, b_ada):
    m = jax.nn.silu(cond) @ w_ada + b_ada
    return m.reshape(m.shape[:-1] + (6, D_MODEL))


def split_projection(z):
    widths = (NA_WIDTH, NA_WIDTH, NA_WIDTH, GQA_Q_WIDTH, GQA_KV_WIDTH, GQA_KV_WIDTH, S5_CH, N_BRANCH * D_MODEL)
    offsets = np.cumsum(widths)[:-1].tolist()
    return jnp.split(z, offsets, axis=-1)


def rope_2d(x):
    B, L, H, D = x.shape
    nf = D // 4
    t = jnp.arange(L)
    pos = jnp.stack([t // GRID_W, t % GRID_W], axis=-1).astype(jnp.float32)
    inv = ROPE_THETA ** (-jnp.arange(nf, dtype=jnp.float32) / nf)
    ang = pos[:, :, None] * inv
    cos = jnp.cos(ang)[None, :, None]
    sin = jnp.sin(ang)[None, :, None]
    xf = x.astype(jnp.float32).reshape(B, L, H, 2, 2, nf)
    x1, x2 = xf[..., 0, :], xf[..., 1, :]
    out = jnp.stack([x1 * cos - x2 * sin, x2 * cos + x1 * sin], axis=-2)
    return out.reshape(x.shape).astype(x.dtype)


def dense_attention(q, k, v, sink):
    B, Lq, KVH, G, D = q.shape
    nb = Lq // ATTN_BLOCK_Q
    lk = k.shape[1]
    scale = D ** -0.5
    qb = jnp.moveaxis(q.reshape(B, nb, ATTN_BLOCK_Q, KVH, G, D), 1, 0)

    def one_block(q_n):
        s = jnp.einsum('bqhgd,bkhd->bhgqk', q_n, k).astype(jnp.float32) * scale
        if sink is not None:
            s_sink = jnp.broadcast_to(sink.astype(jnp.float32)[None, :, :, None, None], s.shape[:-1] + (1,))
            s = jnp.concatenate([s, s_sink], axis=-1)
        p = jax.nn.softmax(s, axis=-1)[..., :lk]
        return jnp.einsum('bhgqk,bkhd->bqhgd', p.astype(v.dtype), v)

    o = lax.map(one_block, qb)
    return jnp.moveaxis(o, 0, 1).reshape(B, Lq, KVH * G * D)


def na_latent(q, k, v, ck, cv, rpb):
    B, L, H, D = q.shape
    rows = L // GRID_W
    kh = min(NA_WIN_H, rows)
    ncb = GRID_W // NA_QB_W
    nl = kh * NA_KSPAN
    scale = D ** -0.5
    r = jnp.arange(rows)
    ridx = jnp.clip(r - kh // 2, 0, rows - kh)[:, None] + jnp.arange(kh)
    dr_idx = ridx - r[:, None] + (NA_WIN_H - 1)
    j = jnp.arange(ncb)
    cidx = jnp.clip(j * NA_QB_W - NA_WIN_W // 2, 0, GRID_W - NA_KSPAN)[:, None] + jnp.arange(NA_KSPAN)
    qcol = j[:, None] * NA_QB_W + jnp.arange(NA_QB_W)
    cstart = jnp.clip(qcol - NA_WIN_W // 2, 0, GRID_W - NA_WIN_W)
    col_ok = (cidx[:, None, :] >= cstart[..., None]) & (cidx[:, None, :] < cstart[..., None] + NA_WIN_W)
    dc_idx = jnp.clip(cidx[:, None, :] - qcol[:, :, None] + (NA_WIN_W - 1), 0, 2 * NA_WIN_W - 2)
    rpb32 = rpb.astype(jnp.float32)
    kg = k.reshape(B, rows, GRID_W, H, D)
    vg = v.reshape(B, rows, GRID_W, H, D)
    qg = q.reshape(B, rows, ncb, NA_QB_W, H, D)

    def one_row(xs):
        q_r, ridx_r, dr_r = xs
        k_r = kg[:, ridx_r][:, :, cidx]
        v_r = vg[:, ridx_r][:, :, cidx]
        v_r = jnp.transpose(v_r, (0, 2, 1, 3, 4, 5)).reshape(B, ncb, nl, H, D)
        bias = rpb32[:, dr_r[None, None, :, None], dc_idx[:, :, None, :]]
        bias = jnp.where(col_ok[:, :, None, :], bias, NEG_INF)
        s_loc = jnp.einsum('bjqhd,bkjchd->bhjqkc', q_r, k_r).astype(jnp.float32) * scale + bias[None]
        s_loc = s_loc.reshape(B, H, ncb, NA_QB_W, nl)
        s_ctx = jnp.einsum('bjqhd,bshd->bhjqs', q_r, ck).astype(jnp.float32) * scale
        p = jax.nn.softmax(jnp.concatenate([s_loc, s_ctx], axis=-1), axis=-1).astype(v.dtype)
        return (jnp.einsum('bhjqn,bjnhd->bjqhd', p[..., :nl], v_r)
                + jnp.einsum('bhjqs,bshd->bjqhd', p[..., nl:], cv))

    o = lax.map(one_row, (jnp.moveaxis(qg, 1, 0), ridx, dr_idx))
    return jnp.moveaxis(o, 0, 1).reshape(B, L, H * D)


def swa_latent(q, k, v, ck, cv, sink):
    B, L, KVH, G, D = q.shape
    nb = L // SWA_BLOCK
    nl = 3 * SWA_BLOCK
    lc = ck.shape[1]
    scale = D ** -0.5

    def band(x):
        xb = x.reshape(B, nb, SWA_BLOCK, KVH, D)
        xp = jnp.pad(xb, ((0, 0), (1, 1), (0, 0), (0, 0), (0, 0)))
        w = jnp.concatenate([xp[:, :-2], xp[:, 1:-1], xp[:, 2:]], axis=2)
        return jnp.moveaxis(w, 1, 0)

    blk = jnp.arange(nb)
    qpos = blk[:, None] * SWA_BLOCK + jnp.arange(SWA_BLOCK)
    kpos = (blk[:, None] - 1) * SWA_BLOCK + jnp.arange(nl)
    ok = ((kpos[:, None, :] >= 0) & (kpos[:, None, :] < L)
          & (jnp.abs(qpos[:, :, None] - kpos[:, None, :]) <= SWA_WINDOW))
    sink_logit = sink.astype(jnp.float32)[None, :, :, None, None]

    def one_block(xs):
        q_n, k_n, v_n, ok_n = xs
        s_loc = jnp.einsum('bqhgd,bkhd->bhgqk', q_n, k_n).astype(jnp.float32) * scale
        s_loc = jnp.where(ok_n, s_loc, NEG_INF)
        s_ctx = jnp.einsum('bqhgd,bshd->bhgqs', q_n, ck).astype(jnp.float32) * scale
        s_sink = jnp.broadcast_to(sink_logit, (B, KVH, G, SWA_BLOCK, 1))
        p = jax.nn.softmax(jnp.concatenate([s_loc, s_ctx, s_sink], axis=-1), axis=-1).astype(v.dtype)
        return (jnp.einsum('bhgqk,bkhd->bqhgd', p[..., :nl], v_n)
                + jnp.einsum('bhgqs,bshd->bqhgd', p[..., nl:nl + lc], cv))

    qb = jnp.moveaxis(q.reshape(B, nb, SWA_BLOCK, KVH, G, D), 1, 0)
    o = lax.map(one_block, (qb, band(k), band(v), ok))
    return jnp.moveaxis(o, 0, 1).reshape(B, L, KVH * G * D)


def _linear_combine(e1, e2):
    a1, b1 = e1
    a2, b2 = e2
    return a1 * a2, a2 * b1 + b2


def s5_mixer(u, P, l, h0):
    B, L, _ = u.shape
    uf = u.astype(jnp.float32).reshape(B, L, S5_GROUPS, S5_GROUP_CH)
    uc = uf.astype(jnp.complex64)
    y = uf * P['s5_d'][l].astype(jnp.float32).reshape(S5_GROUPS, S5_GROUP_CH)
    hs = []
    for d in range(2):
        A = lax.complex(P['s5_a_re'][l, d].astype(jnp.float32), P['s5_a_im'][l, d].astype(jnp.float32))
        dt = jnp.exp(P['s5_log_dt'][l, d].astype(jnp.float32))[:, None]
        a_bar = jnp.exp(A * dt)
        b_mat = lax.complex(P['s5_b_re'][l, d].astype(jnp.float32), P['s5_b_im'][l, d].astype(jnp.float32))
        b_bar = ((a_bar - 1.0) / A)[..., None] * b_mat
        c_mat = lax.complex(P['s5_c_re'][l, d].astype(jnp.float32), P['s5_c_im'][l, d].astype(jnp.float32))
        bu = jnp.einsum('blgc,gnc->blgn', uc, b_bar)
        if d == 1:
            bu = jnp.flip(bu, axis=1)
        if h0 is not None:
            bu = bu.at[:, 0].add(a_bar * h0[d])
        _, hseq = lax.associative_scan(_linear_combine, (jnp.broadcast_to(a_bar, bu.shape), bu), axis=1)
        if d == 1:
            hseq = jnp.flip(hseq, axis=1)
        y = y + jnp.einsum('blgn,gcn->blgc', hseq, c_mat).real
        hs.append(hseq)
    y = jax.nn.gelu(y.reshape(B, L, S5_CH))
    y = y * jax.nn.sigmoid(y @ P['s5_w_glu'][l].astype(jnp.float32))
    return y.astype(u.dtype), hs[0], hs[1]


def hier_moe(h, P, l):
    B, L, _ = h.shape
    lg = (h @ P['moe_w_group'][l] + P['moe_b_group'][l]).astype(jnp.float32)
    gsel = jnp.argmax(lg, axis=-1)
    p_group = jnp.take_along_axis(jax.nn.softmax(lg, axis=-1), gsel[..., None], axis=-1)
    le = (h @ P['moe_w_expert'][l] + P['moe_b_expert'][l]).astype(jnp.float32)
    le = le.reshape(B, L, MOE_GROUPS, MOE_EXPERTS_PER_GROUP)
    le_sel = jnp.take_along_axis(le, gsel[..., None, None], axis=-2)[..., 0, :]
    top_v, top_i = lax.top_k(le_sel, MOE_TOP_K)
    w_top = jax.nn.softmax(top_v, axis=-1) * p_group
    expert_id = gsel[..., None] * MOE_EXPERTS_PER_GROUP + top_i
    combine = jnp.sum(jax.nn.one_hot(expert_id, MOE_EXPERTS, dtype=jnp.float32) * w_top[..., None], axis=-2)
    a = jnp.einsum('bld,edf->blef', h, P['moe_w_gate'][l])
    b = jnp.einsum('bld,edf->blef', h, P['moe_w_up'][l])
    act = jax.nn.silu(a) * b * combine.astype(h.dtype)[..., None]
    return jnp.einsum('blef,efd->bld', act, P['moe_w_down'][l])


def trunk_layer(x, mod, P, l, ctx):
    B, L, _ = x.shape
    h = rmsnorm(x, P['norm_g'][l, 0]) * (1.0 + mod[..., 1, :]) + mod[..., 0, :]
    na_q, na_k, na_v, g_q, g_k, g_v, u, gates = split_projection(h @ P['w_in'][l])
    na_q = na_q.reshape(B, L, NA_HEADS, HEAD_DIM)
    na_k = na_k.reshape(B, L, NA_HEADS, HEAD_DIM)
    na_v = na_v.reshape(B, L, NA_HEADS, HEAD_DIM)
    g_q = g_q.reshape(B, L, GQA_Q_HEADS, HEAD_DIM)
    g_k = g_k.reshape(B, L, GQA_KV_HEADS, HEAD_DIM)
    g_v = g_v.reshape(B, L, GQA_KV_HEADS, HEAD_DIM)
    sink = P['gqa_sink'][l].reshape(GQA_KV_HEADS, GQA_GROUP)
    if ctx is None:
        o_a = dense_attention(na_q[:, :, :, None], na_k, na_v, None)
        o_b = dense_attention(g_q.reshape(B, L, GQA_KV_HEADS, GQA_GROUP, HEAD_DIM), g_k, g_v, sink)
        o_c, hf, hb = s5_mixer(u, P, l, None)
        st = jnp.stack([jnp.stack([hf[:, -1].real, hf[:, -1].imag], axis=1),
                        jnp.stack([hb[:, 0].real, hb[:, 0].imag], axis=1)], axis=1).astype(x.dtype)
        new_ctx = (jnp.stack([na_k, na_v], axis=1), jnp.stack([g_k, g_v], axis=1), st)
    else:
        na_ckv, g_ckv, ssm = ctx
        o_a = na_latent(na_q, na_k, na_v, na_ckv[:, 0], na_ckv[:, 1], P['na_rpb'][l])
        g_q = rope_2d(g_q)
        g_k = rope_2d(g_k)
        o_b = swa_latent(g_q.reshape(B, L, GQA_KV_HEADS, GQA_GROUP, HEAD_DIM), g_k, g_v, g_ckv[:, 0], g_ckv[:, 1], sink)
        ssm32 = ssm.astype(jnp.float32)
        h0 = (lax.complex(ssm32[:, 0, 0], ssm32[:, 0, 1]), lax.complex(ssm32[:, 1, 0], ssm32[:, 1, 1]))
        o_c, _, _ = s5_mixer(u, P, l, h0)
        new_ctx = None
    g = jax.nn.sigmoid(gates.astype(jnp.float32)).astype(h.dtype)
    g_a, g_b, g_c = jnp.split(g, N_BRANCH, axis=-1)
    merged = (g_a * (o_a @ P['w_branch_a'][l]) + g_b * (o_b @ P['w_branch_b'][l])
              + g_c * (o_c @ P['w_branch_c'][l]))
    x = x + mod[..., 2, :] * (merged @ P['w_out'][l])
    h2 = rmsnorm(x, P['norm_g'][l, 1]) * (1.0 + mod[..., 4, :]) + mod[..., 3, :]
    x = x + mod[..., 5, :] * hier_moe(h2, P, l)
    return x, new_ctx


def setup_inputs(seed: int = 0) -> dict:
    key = jax.random.key(seed)
    keys = iter(list(jax.random.split(key, 40)))

    def nrm(shape, scale):
        return jax.random.normal(next(keys), shape, jnp.float32) * scale

    G, N, GC = S5_GROUPS, S5_STATE, S5_GROUP_CH
    a_im_init = math.pi * jnp.arange(N, dtype=jnp.float32)
    return {
        'x_prompt': nrm((BATCH, SEQ, D_MODEL), 1.0),
        'x_sample': nrm((DEC_BATCH, DEC_SEQ, D_MODEL), 1.0),
        'cache_na_kv': nrm((DEC_BATCH, DEPTH, 2, PAST_LEN, NA_HEADS, HEAD_DIM), 1.0),
        'cache_gqa_kv': nrm((DEC_BATCH, DEPTH, 2, PAST_LEN, GQA_KV_HEADS, HEAD_DIM), 1.0),
        'state_ssm': nrm((DEC_BATCH, DEPTH, 2, 2, G, N), 0.1),
        'c': nrm((DEC_BATCH, D_MODEL), 1.0),
        'c_ctx': nrm((D_MODEL,), 1.0),
        'norm_g': 1.0 + nrm((DEPTH, 2, D_MODEL), 0.02),
        'w_ada': nrm((DEPTH, D_MODEL, 6 * D_MODEL), 0.5 * D_MODEL ** -0.5),
        'b_ada': nrm((DEPTH, 6 * D_MODEL), 0.02),
        'w_in': nrm((DEPTH, D_MODEL, IN_COLS), D_MODEL ** -0.5),
        'na_rpb': nrm((DEPTH, NA_HEADS, 2 * NA_WIN_H - 1, 2 * NA_WIN_W - 1), 0.1),
        'gqa_sink': nrm((DEPTH, GQA_Q_HEADS), 0.5),
        's5_a_re': -0.5 + nrm((DEPTH, 2, G, N), 0.01),
        's5_a_im': a_im_init + nrm((DEPTH, 2, G, N), 0.01),
        's5_log_dt': jax.random.uniform(next(keys), (DEPTH, 2, G), jnp.float32,
                                        math.log(S5_DT_MIN), math.log(S5_DT_MAX)),
        's5_b_re': nrm((DEPTH, 2, G, N, GC), (2 * GC) ** -0.5),
        's5_b_im': nrm((DEPTH, 2, G, N, GC), (2 * GC) ** -0.5),
        's5_c_re': nrm((DEPTH, 2, G, GC, N), 0.5),
        's5_c_im': nrm((DEPTH, 2, G, GC, N), 0.5),
        's5_d': nrm((DEPTH, S5_CH), 1.0),
        's5_w_glu': nrm((DEPTH, S5_CH, S5_CH), S5_CH ** -0.5),
        'w_branch_a': nrm((DEPTH, NA_WIDTH, D_MODEL), NA_WIDTH ** -0.5),
        'w_branch_b': nrm((DEPTH, GQA_Q_WIDTH, D_MODEL), GQA_Q_WIDTH ** -0.5),
        'w_branch_c': nrm((DEPTH, S5_CH, D_MODEL), S5_CH ** -0.5),
        'w_out': nrm((DEPTH, D_MODEL, D_MODEL), D_MODEL ** -0.5),
        'moe_w_group': nrm((DEPTH, D_MODEL, MOE_GROUPS), D_MODEL ** -0.5),
        'moe_b_group': nrm((DEPTH, MOE_GROUPS), 0.01),
        'moe_w_expert': nrm((DEPTH, D_MODEL, MOE_EXPERTS), D_MODEL ** -0.5),
        'moe_b_expert': nrm((DEPTH, MOE_EXPERTS), 0.01),
        'moe_w_gate': nrm((DEPTH, MOE_EXPERTS, D_MODEL, EXPERT_FF), D_MODEL ** -0.5),
        'moe_w_up': nrm((DEPTH, MOE_EXPERTS, D_MODEL, EXPERT_FF), D_MODEL ** -0.5),
        'moe_w_down': nrm((DEPTH, MOE_EXPERTS, EXPERT_FF, D_MODEL), EXPERT_FF ** -0.5),
        'final_g': 1.0 + nrm((D_MODEL,), 0.02),
    }


def reference(x_prompt, x_sample, cache_na_kv, cache_gqa_kv, state_ssm, c, c_ctx, norm_g, w_ada, b_ada, w_in,
              na_rpb, gqa_sink, s5_a_re, s5_a_im, s5_log_dt, s5_b_re, s5_b_im, s5_c_re, s5_c_im, s5_d, s5_w_glu,
              w_branch_a, w_branch_b, w_branch_c, w_out, moe_w_group, moe_b_group, moe_w_expert, moe_b_expert,
              moe_w_gate, moe_w_up, moe_w_down, final_g):
    P = dict(norm_g=norm_g, w_in=w_in, na_rpb=na_rpb, gqa_sink=gqa_sink, s5_a_re=s5_a_re, s5_a_im=s5_a_im,
             s5_log_dt=s5_log_dt, s5_b_re=s5_b_re, s5_b_im=s5_b_im, s5_c_re=s5_c_re, s5_c_im=s5_c_im, s5_d=s5_d,
             s5_w_glu=s5_w_glu, w_branch_a=w_branch_a, w_branch_b=w_branch_b, w_branch_c=w_branch_c, w_out=w_out,
             moe_w_group=moe_w_group, moe_b_group=moe_b_group, moe_w_expert=moe_w_expert,
             moe_b_expert=moe_b_expert, moe_w_gate=moe_w_gate, moe_w_up=moe_w_up, moe_w_down=moe_w_down)

    xp = x_prompt
    na_list, gqa_list, ssm_list = [], [], []
    for l in range(DEPTH):
        mod_ctx = ada_mod(c_ctx, w_ada[l], b_ada[l])[None, None]
        xp, (na_kv, g_kv, st) = trunk_layer(xp, mod_ctx, P, l, None)
        na_list.append(na_kv)
        gqa_list.append(g_kv)
        ssm_list.append(st)
    y_prompt = rmsnorm(xp, final_g)
    new_cache_na_kv = jnp.stack(na_list, axis=1)
    new_cache_gqa_kv = jnp.stack(gqa_list, axis=1)
    new_state_ssm = jnp.stack(ssm_list, axis=1)

    xs = x_sample
    for l in range(DEPTH):
        mod_lat = ada_mod(c, w_ada[l], b_ada[l])[:, None]
        xs, _ = trunk_layer(xs, mod_lat, P, l, (cache_na_kv[:, l], cache_gqa_kv[:, l], state_ssm[:, l]))
    y_sample = rmsnorm(xs, final_g)
    return (y_prompt, y_sample, new_cache_na_kv, new_cache_gqa_kv, new_state_ssm)
```

```python
import functools
import math

import numpy as np
import jax
import jax.numpy as jnp
from jax import lax
from jax.experimental import pallas as pl
from jax.experimental.pallas import tpu as pltpu

F32 = jnp.float32
BF16 = jnp.bfloat16

D_MODEL = 1024
BATCH = 32
SEQ = 256
DEPTH = 2
DEC_BATCH = 4
DEC_SEQ = 2048
PAST_LEN = 512
GRID_W = 64
GRID_ROWS = DEC_SEQ // GRID_W
HEAD_DIM = 64
NA_HEADS = 8
NA_WIN_H = 8
NA_WIN_W = 16
GQA_Q_HEADS = 8
GQA_KV_HEADS = 2
GQA_GROUP = GQA_Q_HEADS // GQA_KV_HEADS
SWA_WINDOW = 128
SWA_BLOCK = 128
ROPE_THETA = 10000.0
S5_CH = 512
S5_GROUP_CH = 16
S5_GROUPS = S5_CH // S5_GROUP_CH
S5_STATE = 64
N_BRANCH = 3
NA_WIDTH = NA_HEADS * HEAD_DIM
GQA_Q_WIDTH = GQA_Q_HEADS * HEAD_DIM
GQA_KV_WIDTH = GQA_KV_HEADS * HEAD_DIM
MOE_GROUPS = 4
MOE_EXPERTS_PER_GROUP = 4
MOE_EXPERTS = MOE_GROUPS * MOE_EXPERTS_PER_GROUP
EXPERT_FF = 512
EPS = 1e-6
NEG_INF = -1e30

N_CTX_TOK = BATCH * SEQ
N_LAT_TOK = DEC_BATCH * DEC_SEQ
N_TOK = N_CTX_TOK + N_LAT_TOK
N_COND = 8

COL_NA_Q = 0
COL_NA_K = 512
COL_NA_V = 1024
COL_G_Q = 1536
COL_G_K = 2048
COL_G_V = 2176
COL_U = 2560
COL_GATES = 3072
Z_COLS = 6144

S5_T = 16
S5_PAIRS = S5_GROUPS // 2
S5_LAT_B = 8

NA_QROWS = 4
NA_KROWS = 12

VMEM_LIMIT = 56 * 1024 * 1024


def _dot(a, b):
    return jnp.dot(a, b, preferred_element_type=F32)


def _dot_nt(a, b):
    return lax.dot_general(a, b, (((1,), (1,)), ((), ())), preferred_element_type=F32)


def _split_bf16(x):
    hi = x.astype(BF16)
    lo = (x - hi.astype(F32)).astype(BF16)
    return hi, lo


def _rms(x, g):
    return x * lax.rsqrt(jnp.mean(x * x, axis=-1, keepdims=True) + EPS) * g


def _cond_index(i, tm):
    nctx = N_CTX_TOK // tm
    return jnp.where(i < nctx, 0, 1 + ((i - nctx) * tm) // DEC_SEQ)


def _ada_kernel(c_ref, w_ref, b_ref, o_ref):
    c = c_ref[...]
    s = c * jax.nn.sigmoid(c)
    s_hi, s_lo = _split_bf16(s)
    w_hi, w_lo = _split_bf16(w_ref[...])
    o_ref[...] = _dot(s_hi, w_hi) + _dot(s_lo, w_hi) + _dot(s_hi, w_lo) + b_ref[...]


def _ada_mod(cond, w_ada, b_ada):
    tn = 1536
    n = 6 * D_MODEL
    out = pl.pallas_call(
        _ada_kernel,
        out_shape=jax.ShapeDtypeStruct((DEPTH, N_COND, n), F32),
        grid=(DEPTH, n // tn),
        in_specs=[
            pl.BlockSpec((N_COND, D_MODEL), lambda l, j: (0, 0)),
            pl.BlockSpec((None, D_MODEL, tn), lambda l, j: (l, 0, j)),
            pl.BlockSpec((None, 1, tn), lambda l, j: (l, 0, j)),
        ],
        out_specs=pl.BlockSpec((None, N_COND, tn), lambda l, j: (l, 0, j)),
        compiler_params=pltpu.CompilerParams(
            dimension_semantics=("parallel", "parallel"), vmem_limit_bytes=VMEM_LIMIT),
        name="ada_mod",
    )(cond, w_ada, b_ada.reshape(DEPTH, 1, n))
    return out.reshape(DEPTH, N_COND, 6, D_MODEL)


def _inproj_kernel(x_ref, g_ref, mod_ref, w_ref, o_ref, h_sc):
    @pl.when(pl.program_id(1) == 0)
    def _():
        h = _rms(x_ref[...], g_ref[...]) * (1.0 + mod_ref[1:2, :]) + mod_ref[0:1, :]
        h_sc[...] = h.astype(BF16)

    o_ref[...] = _dot(h_sc[...], w_ref[...])


def _inproj(x, g, mod, w, l):
    tm, tn = 1024, 1536
    return pl.pallas_call(
        _inproj_kernel,
        out_shape=jax.ShapeDtypeStruct((N_TOK, Z_COLS), F32),
        grid=(N_TOK // tm, Z_COLS // tn),
        in_specs=[
            pl.BlockSpec((tm, D_MODEL), lambda i, j: (i, 0)),
            pl.BlockSpec((1, D_MODEL), lambda i, j: (0, 0)),
            pl.BlockSpec((None, None, 6, D_MODEL), lambda i, j: (l, _cond_index(i, tm), 0, 0)),
            pl.BlockSpec((D_MODEL, tn), lambda i, j: (0, j)),
        ],
        out_specs=pl.BlockSpec((tm, tn), lambda i, j: (i, j)),
        scratch_shapes=[pltpu.VMEM((tm, D_MODEL), BF16)],
        compiler_params=pltpu.CompilerParams(
            dimension_semantics=("parallel", "arbitrary"), vmem_limit_bytes=VMEM_LIMIT),
        name="inproj",
    )(x, g, mod, w)


def _ctx_attn_kernel(sink_ref, na_ref, gq_ref, gkv_ref, oa_ref, ob_ref):
    scale = HEAD_DIM ** -0.5
    na = na_ref[...].astype(BF16)
    gq = gq_ref[...].astype(BF16)
    gkv = gkv_ref[...].astype(BF16)
    for h in range(NA_HEADS):
        q = na[:, COL_NA_Q + h * HEAD_DIM:COL_NA_Q + (h + 1) * HEAD_DIM]
        k = na[:, COL_NA_K + h * HEAD_DIM:COL_NA_K + (h + 1) * HEAD_DIM]
        v = na[:, COL_NA_V + h * HEAD_DIM:COL_NA_V + (h + 1) * HEAD_DIM]
        s = _dot_nt(q, k) * scale
        m = jnp.max(s, axis=-1, keepdims=True)
        p = jnp.exp(s - m)
        d = jnp.sum(p, axis=-1, keepdims=True)
        o = _dot(p.astype(BF16), v) / d
        oa_ref[:, h * HEAD_DIM:(h + 1) * HEAD_DIM] = o.astype(BF16)
    for h in range(GQA_Q_HEADS):
        kh = h // GQA_GROUP
        q = gq[:, h * HEAD_DIM:(h + 1) * HEAD_DIM]
        k = gkv[:, kh * HEAD_DIM:(kh + 1) * HEAD_DIM]
        v = gkv[:, GQA_KV_WIDTH + kh * HEAD_DIM:GQA_KV_WIDTH + (kh + 1) * HEAD_DIM]
        sink = sink_ref[h]
        s = _dot_nt(q, k) * scale
        m = jnp.maximum(jnp.max(s, axis=-1, keepdims=True), sink)
        p = jnp.exp(s - m)
        d = jnp.sum(p, axis=-1, keepdims=True) + jnp.exp(sink - m)
        o = _dot(p.astype(BF16), v) / d
        ob_ref[:, h * HEAD_DIM:(h + 1) * HEAD_DIM] = o.astype(BF16)


def _ctx_attn(z, sink):
    out = jax.ShapeDtypeStruct((N_CTX_TOK, NA_WIDTH), BF16)
    return pl.pallas_call(
        _ctx_attn_kernel,
        out_shape=(out, out),
        grid=(BATCH,),
        in_specs=[
            pl.BlockSpec(memory_space=pltpu.SMEM),
            pl.BlockSpec((SEQ, 3 * NA_WIDTH), lambda b: (b, 0)),
            pl.BlockSpec((SEQ, GQA_Q_WIDTH), lambda b: (b, COL_G_Q // GQA_Q_WIDTH)),
            pl.BlockSpec((SEQ, 2 * GQA_KV_WIDTH), lambda b: (b, COL_G_K // (2 * GQA_KV_WIDTH))),
        ],
        out_specs=(pl.BlockSpec((SEQ, NA_WIDTH), lambda b: (b, 0)),
                   pl.BlockSpec((SEQ, GQA_Q_WIDTH), lambda b: (b, 0))),
        compiler_params=pltpu.CompilerParams(
            dimension_semantics=("parallel",), vmem_limit_bytes=VMEM_LIMIT),
        name="ctx_attn",
    )(sink, z, z, z)


def _na_bias_tables(rpb):
    kh = min(NA_WIN_H, GRID_ROWS)
    idx_r, idx_c, oks = [], [], []
    qc = np.arange(GRID_W)
    kc = np.arange(GRID_W)
    cstart = np.clip(qc - NA_WIN_W // 2, 0, GRID_W - NA_WIN_W)
    col_ok = (kc[None, :] >= cstart[:, None]) & (kc[None, :] < cstart[:, None] + NA_WIN_W)
    dc = np.clip(kc[None, :] - qc[:, None] + (NA_WIN_W - 1), 0, 2 * NA_WIN_W - 2)
    for r0, ks in ((0, 0), (NA_QROWS, 0), (GRID_ROWS - NA_QROWS, GRID_ROWS - NA_KROWS)):
        r = r0 + np.arange(NA_QROWS)
        st = np.clip(r - kh // 2, 0, GRID_ROWS - kh)
        krow = ks + np.arange(NA_KROWS)
        row_ok = (krow[None, :] >= st[:, None]) & (krow[None, :] < st[:, None] + kh)
        dr = np.clip(krow[None, :] - r[:, None] + (NA_WIN_H - 1), 0, 2 * NA_WIN_H - 2)
        shape = (NA_QROWS, GRID_W, NA_KROWS, GRID_W)
        ok = np.broadcast_to(row_ok[:, None, :, None] & col_ok[None, :, None, :], shape)
        idx_r.append(np.broadcast_to(dr[:, None, :, None], shape).reshape(NA_QROWS * GRID_W, NA_KROWS * GRID_W))
        idx_c.append(np.broadcast_to(dc[None, :, None, :], shape).reshape(NA_QROWS * GRID_W, NA_KROWS * GRID_W))
        oks.append(ok.reshape(NA_QROWS * GRID_W, NA_KROWS * GRID_W))
    idx_r, idx_c, oks = np.stack(idx_r), np.stack(idx_c), np.stack(oks)
    vals = rpb.astype(F32)[:, idx_r, idx_c]
    return jnp.where(oks[None], vals, NEG_INF).transpose(1, 0, 2, 3)


def _na_lat_kernel(q_ref, k_ref, v_ref, ck_ref, cv_ref, bias_ref, o_ref):
    i = pl.program_id(1)
    scale = HEAD_DIM ** -0.5
    nk = NA_KROWS * GRID_W
    ks = jnp.clip(i * NA_QROWS - NA_WIN_H // 2, 0, GRID_ROWS - NA_KROWS)
    start = pl.multiple_of(ks * GRID_W, GRID_W)
    q = q_ref[...].astype(BF16)
    kw = k_ref[pl.ds(start, nk), :].astype(BF16)
    vw = v_ref[pl.ds(start, nk), :].astype(BF16)
    ck = ck_ref[...].astype(BF16)
    cv = cv_ref[...].astype(BF16)
    for h in range(NA_HEADS):
        sl = slice(h * HEAD_DIM, (h + 1) * HEAD_DIM)
        qh = q[:, sl]
        s_loc = _dot_nt(qh, kw[:, sl]) * scale + bias_ref[h]
        s_ctx = _dot_nt(qh, ck[:, sl]) * scale
        m = jnp.maximum(jnp.max(s_loc, axis=-1, keepdims=True), jnp.max(s_ctx, axis=-1, keepdims=True))
        p_loc = jnp.exp(s_loc - m)
        p_ctx = jnp.exp(s_ctx - m)
        d = jnp.sum(p_loc, axis=-1, keepdims=True) + jnp.sum(p_ctx, axis=-1, keepdims=True)
        o = (_dot(p_loc.astype(BF16), vw[:, sl]) + _dot(p_ctx.astype(BF16), cv[:, sl])) / d
        o_ref[:, sl] = o.astype(BF16)


def _na_lat(z, cache_na, bias, l):
    tq = NA_QROWS * GRID_W
    nsteps = GRID_ROWS // NA_QROWS
    lat_blk = N_CTX_TOK // DEC_SEQ

    def pat(i):
        return jnp.where(i == 0, 0, jnp.where(i == nsteps - 1, 2, 1))

    return pl.pallas_call(
        _na_lat_kernel,
        out_shape=jax.ShapeDtypeStruct((N_LAT_TOK, NA_WIDTH), BF16),
        grid=(DEC_BATCH, nsteps),
        in_specs=[
            pl.BlockSpec((tq, NA_WIDTH), lambda b, i: (N_CTX_TOK // tq + b * nsteps + i, COL_NA_Q // NA_WIDTH)),
            pl.BlockSpec((DEC_SEQ, NA_WIDTH), lambda b, i: (lat_blk + b, COL_NA_K // NA_WIDTH)),
            pl.BlockSpec((DEC_SEQ, NA_WIDTH), lambda b, i: (lat_blk + b, COL_NA_V // NA_WIDTH)),
            pl.BlockSpec((None, None, None, PAST_LEN, NA_WIDTH), lambda b, i: (b, l, 0, 0, 0)),
            pl.BlockSpec((None, None, None, PAST_LEN, NA_WIDTH), lambda b, i: (b, l, 1, 0, 0)),
            pl.BlockSpec((None, NA_HEADS, tq, NA_KROWS * GRID_W), lambda b, i: (pat(i), 0, 0, 0)),
        ],
        out_specs=pl.BlockSpec((tq, NA_WIDTH), lambda b, i: (b * nsteps + i, 0)),
        compiler_params=pltpu.CompilerParams(
            dimension_semantics=("parallel", "arbitrary"), vmem_limit_bytes=VMEM_LIMIT),
        name="na_lat",
    )(z, z, z, cache_na, cache_na, bias)


def _rope_tables():
    nf = HEAD_DIM // 4
    t = jnp.arange(DEC_SEQ)
    pos = jnp.stack([t // GRID_W, t % GRID_W], axis=-1).astype(F32)
    inv = ROPE_THETA ** (-jnp.arange(nf, dtype=F32) / nf)
    ang = pos[:, :, None] * inv
    cos = jnp.cos(ang)
    sin = jnp.sin(ang)
    cos_d = jnp.stack([cos, cos], axis=2).reshape(DEC_SEQ, HEAD_DIM)
    sin_d = jnp.stack([-sin, sin], axis=2).reshape(DEC_SEQ, HEAD_DIM)
    return cos_d, sin_d


def _rope(x, cos, sin_signed):
    n = x.shape[-1]
    nf = HEAD_DIM // 4
    lane = lax.broadcasted_iota(jnp.int32, x.shape, 1)
    first_half = (lane // nf) % 2 == 0
    partner = jnp.where(first_half, pltpu.roll(x, n - nf, 1), pltpu.roll(x, nf, 1))
    return x * cos + partner * sin_signed


def _swa_lat_kernel(sink_ref, q_ref, kv_ref, cq_ref, sq_ref, ckt_ref, skt_ref, ck_ref, cv_ref, o_ref, k_sc, v_sc):
    n = pl.program_id(1)
    scale = HEAD_DIM ** -0.5
    nwin = 3 * SWA_BLOCK

    @pl.when(n == 0)
    def _():
        kv = kv_ref[...]
        k_sc[...] = _rope(kv[:, :GQA_KV_WIDTH], ckt_ref[...], skt_ref[...]).astype(BF16)
        v_sc[...] = kv[:, GQA_KV_WIDTH:].astype(BF16)

    q = _rope(q_ref[...], cq_ref[...], sq_ref[...]).astype(BF16)
    start = pl.multiple_of(jnp.clip((n - 1) * SWA_BLOCK, 0, DEC_SEQ - nwin), SWA_BLOCK)
    kw = k_sc[pl.ds(start, nwin), :]
    vw = v_sc[pl.ds(start, nwin), :]
    ck = ck_ref[...].astype(BF16)
    cv = cv_ref[...].astype(BF16)
    rows = GQA_GROUP * SWA_BLOCK
    row = lax.broadcasted_iota(jnp.int32, (rows, nwin), 0)
    col = lax.broadcasted_iota(jnp.int32, (rows, nwin), 1)
    qpos = n * SWA_BLOCK + row % SWA_BLOCK
    kpos = start + col
    ok = jnp.abs(qpos - kpos) <= SWA_WINDOW
    grp = lax.broadcasted_iota(jnp.int32, (rows, 1), 0) // SWA_BLOCK
    for kh in range(GQA_KV_HEADS):
        sl = slice(kh * HEAD_DIM, (kh + 1) * HEAD_DIM)
        q4 = jnp.concatenate(
            [q[:, (kh * GQA_GROUP + g) * HEAD_DIM:(kh * GQA_GROUP + g + 1) * HEAD_DIM] for g in range(GQA_GROUP)],
            axis=0)
        sink = jnp.zeros((rows, 1), F32)
        for g in range(GQA_GROUP):
            sink = jnp.where(grp == g, sink_ref[kh * GQA_GROUP + g], sink)
        s_loc = jnp.where(ok, _dot_nt(q4, kw[:, sl]) * scale, NEG_INF)
        s_ctx = _dot_nt(q4, ck[:, sl]) * scale
        m = jnp.maximum(jnp.maximum(jnp.max(s_loc, axis=-1, keepdims=True),
                                    jnp.max(s_ctx, axis=-1, keepdims=True)), sink)
        p_loc = jnp.exp(s_loc - m)
        p_ctx = jnp.exp(s_ctx - m)
        d = (jnp.sum(p_loc, axis=-1, keepdims=True) + jnp.sum(p_ctx, axis=-1, keepdims=True)
             + jnp.exp(sink - m))
        o4 = (_dot(p_loc.astype(BF16), vw[:, sl]) + _dot(p_ctx.astype(BF16), cv[:, sl])) / d
        for g in range(GQA_GROUP):
            h = kh * GQA_GROUP + g
            o_ref[:, h * HEAD_DIM:(h + 1) * HEAD_DIM] = o4[g * SWA_BLOCK:(g + 1) * SWA_BLOCK].astype(BF16)


def _swa_lat(z, cache_gqa, sink, cos_q, sin_q, cos_k, sin_k, l):
    nb = DEC_SEQ // SWA_BLOCK
    lat_blk = N_CTX_TOK // DEC_SEQ
    return pl.pallas_call(
        _swa_lat_kernel,
        out_shape=jax.ShapeDtypeStruct((N_LAT_TOK, GQA_Q_WIDTH), BF16),
        grid=(DEC_BATCH, nb),
        in_specs=[
            pl.BlockSpec(memory_space=pltpu.SMEM),
            pl.BlockSpec((SWA_BLOCK, GQA_Q_WIDTH),
                         lambda b, n: (N_CTX_TOK // SWA_BLOCK + b * nb + n, COL_G_Q // GQA_Q_WIDTH)),
            pl.BlockSpec((DEC_SEQ, 2 * GQA_KV_WIDTH), lambda b, n: (lat_blk + b, COL_G_K // (2 * GQA_KV_WIDTH))),
            pl.BlockSpec((SWA_BLOCK, GQA_Q_WIDTH), lambda b, n: (n, 0)),
            pl.BlockSpec((SWA_BLOCK, GQA_Q_WIDTH), lambda b, n: (n, 0)),
            pl.BlockSpec((DEC_SEQ, GQA_KV_WIDTH), lambda b, n: (0, 0)),
            pl.BlockSpec((DEC_SEQ, GQA_KV_WIDTH), lambda b, n: (0, 0)),
            pl.BlockSpec((None, None, None, PAST_LEN, GQA_KV_WIDTH), lambda b, n: (b, l, 0, 0, 0)),
            pl.BlockSpec((None, None, None, PAST_LEN, GQA_KV_WIDTH), lambda b, n: (b, l, 1, 0, 0)),
        ],
        out_specs=pl.BlockSpec((SWA_BLOCK, GQA_Q_WIDTH), lambda b, n: (b * nb + n, 0)),
        scratch_shapes=[pltpu.VMEM((DEC_SEQ, GQA_KV_WIDTH), BF16), pltpu.VMEM((DEC_SEQ, GQA_KV_WIDTH), BF16)],
        compiler_params=pltpu.CompilerParams(
            dimension_semantics=("parallel", "arbitrary"), vmem_limit_bytes=VMEM_LIMIT),
        name="swa_lat",
    )(sink, z, z, cos_q, sin_q, cos_k, sin_k, cache_gqa, cache_gqa)


def _s5_operators(a_re, a_im, log_dt, b_re, b_im, c_re, c_im):
    T, G, N, C = S5_T, S5_GROUPS, S5_STATE, S5_GROUP_CH
    tau = jnp.arange(T + 1, dtype=F32)
    out = {k: [] for k in ("bs", "m", "cre", "cim", "are", "aim")}
    jj, ss = np.meshgrid(np.arange(T), np.arange(T), indexing="ij")
    for d in range(2):
        A = lax.complex(a_re[d].astype(F32), a_im[d].astype(F32))
        dt = jnp.exp(log_dt[d].astype(F32))[:, None]
        a_bar = jnp.exp(A * dt)
        pw = jnp.exp((A * dt)[None] * tau[:, None, None])
        b_bar = ((a_bar - 1.0) / A)[..., None] * lax.complex(b_re[d].astype(F32), b_im[d].astype(F32))
        c_mat = lax.complex(c_re[d].astype(F32), c_im[d].astype(F32))
        kern = jnp.einsum("gon,tgn,gni->gtoi", c_mat, pw[:T], b_bar, precision=lax.Precision.HIGHEST).real
        if d == 0:
            lag, ok = jj - ss, jj >= ss
            p_in = pw[T - 1 - np.arange(T)]
            p_out = pw[1 + np.arange(T)]
        else:
            lag, ok = ss - jj, ss >= jj
            p_in = pw[np.arange(T)]
            p_out = pw[T - np.arange(T)]
        m = kern[:, np.clip(lag, 0, T - 1)]
        m = jnp.where(ok[None, :, :, None, None], m, 0.0).transpose(0, 2, 4, 1, 3).reshape(G, T * C, T * C)
        bs = (p_in[:, :, :, None] * b_bar[None]).transpose(1, 0, 3, 2).reshape(G, T * C, N)
        co = (c_mat[None] * p_out[:, :, None, :]).transpose(1, 3, 0, 2).reshape(G, N, T * C)

        def pair_diag(x):
            x = x.reshape(S5_PAIRS, 2, x.shape[1], x.shape[2])
            zero = jnp.zeros_like(x[:, 0])
            return jnp.concatenate([jnp.concatenate([x[:, 0], zero], axis=2),
                                    jnp.concatenate([zero, x[:, 1]], axis=2)], axis=1)

        out["bs"].append(jnp.concatenate([pair_diag(bs.real), pair_diag(bs.imag)], axis=2))
        out["m"].append(pair_diag(m))
        out["cre"].append(pair_diag(co.real))
        out["cim"].append(pair_diag(-co.imag))
        out["are"].append(pw[T].real.reshape(1, G * N))
        out["aim"].append(pw[T].imag.reshape(1, G * N))
    return {k: jnp.stack(v) for k, v in out.items()}


def _s5_kernel(x_ref, bs_ref, m_ref, cre_ref, cim_ref, are_ref, aim_ref, h0re_ref, h0im_ref,
               y_ref, fre_ref, fim_ref, pre_sc, pim_sc, hre_sc, him_sc, *, nb, nsteps):
    d = pl.program_id(1)
    half = 2 * S5_STATE
    x = x_ref[...]
    p = _dot(x, bs_ref[...])
    pre_sc[...] = p[:, :half]
    pim_sc[...] = p[:, half:]
    ar = jnp.broadcast_to(are_ref[...], (nb, half))
    ai = jnp.broadcast_to(aim_ref[...], (nb, half))

    def body(i, carry):
        hr, hi = carry
        k = jnp.where(d == 0, i, nsteps - 1 - i)
        r0 = pl.multiple_of(k * nb, nb)
        hre_sc[pl.ds(r0, nb), :] = hr
        him_sc[pl.ds(r0, nb), :] = hi
        nr = ar * hr - ai * hi + pre_sc[pl.ds(r0, nb), :]
        ni = ar * hi + ai * hr + pim_sc[pl.ds(r0, nb), :]
        return nr, ni

    hr, hi = lax.fori_loop(0, nsteps, body, (h0re_ref[...], h0im_ref[...]))
    fre_ref[...] = hr
    fim_ref[...] = hi
    y = (_dot(x, m_ref[...]) + _dot(hre_sc[...].astype(BF16), cre_ref[...])
         + _dot(him_sc[...].astype(BF16), cim_ref[...]))

    @pl.when(d == 0)
    def _():
        y_ref[...] = y

    @pl.when(d == 1)
    def _():
        y_ref[...] += y


def _s5_scan(xs, ops, h0re, h0im, nb, nsteps):
    rows = nsteps * nb
    pw = 2 * S5_T * S5_GROUP_CH
    half = 2 * S5_STATE
    st = jax.ShapeDtypeStruct((2, nb, S5_GROUPS * S5_STATE), F32)
    wspec = lambda r, c: pl.BlockSpec((None, None, r, c), lambda p, d: (d, p, 0, 0))
    aspec = pl.BlockSpec((None, 1, half), lambda p, d: (d, 0, p))
    hspec = pl.BlockSpec((None, nb, half), lambda p, d: (d, 0, p))
    return pl.pallas_call(
        functools.partial(_s5_kernel, nb=nb, nsteps=nsteps),
        out_shape=(jax.ShapeDtypeStruct((rows, S5_GROUPS * S5_T * S5_GROUP_CH), F32), st, st),
        grid=(S5_PAIRS, 2),
        in_specs=[
            pl.BlockSpec((rows, pw), lambda p, d: (0, p)),
            wspec(pw, 2 * half), wspec(pw, pw), wspec(half, pw), wspec(half, pw),
            aspec, aspec, hspec, hspec,
        ],
        out_specs=(pl.BlockSpec((rows, pw), lambda p, d: (0, p)), hspec, hspec),
        scratch_shapes=[pltpu.VMEM((rows, half), F32) for _ in range(4)],
        compiler_params=pltpu.CompilerParams(
            dimension_semantics=("parallel", "arbitrary"), vmem_limit_bytes=VMEM_LIMIT),
        name="s5_scan",
    )(xs, ops["bs"], ops["m"], ops["cre"], ops["cim"], ops["are"], ops["aim"], h0re, h0im)


def _to_superstep(u, nbatch, nsteps, pad_to):
    x = u.reshape(nbatch, nsteps, S5_T, S5_GROUPS, S5_GROUP_CH).transpose(1, 0, 3, 2, 4)
    if pad_to > nbatch:
        x = jnp.pad(x, ((0, 0), (0, pad_to - nbatch), (0, 0), (0, 0), (0, 0)))
    return x.reshape(nsteps * pad_to, S5_GROUPS * S5_T * S5_GROUP_CH)


def _from_superstep(y, nbatch, nsteps, padded):
    x = y.reshape(nsteps, padded, S5_GROUPS, S5_T, S5_GROUP_CH)[:, :nbatch]
    return x.transpose(1, 0, 3, 2, 4).reshape(nbatch * nsteps * S5_T, S5_CH)


def _route(logits):
    lane = lax.broadcasted_iota(jnp.int32, logits.shape, 1)
    big = jnp.int32(1 << 20)
    is_g = (lane >= MOE_EXPERTS) & (lane < MOE_EXPERTS + MOE_GROUPS)
    lg = jnp.where(is_g, logits, NEG_INF)
    gmax = jnp.max(lg, axis=-1, keepdims=True)
    gsel = jnp.min(jnp.where(is_g & (lg == gmax), lane, big), axis=-1, keepdims=True) - MOE_EXPERTS
    p_group = 1.0 / jnp.sum(jnp.where(is_g, jnp.exp(lg - gmax), 0.0), axis=-1, keepdims=True)
    in_grp = (lane < MOE_EXPERTS) & (lane // MOE_EXPERTS_PER_GROUP == gsel)
    le = jnp.where(in_grp, logits, NEG_INF)
    v1 = jnp.max(le, axis=-1, keepdims=True)
    i1 = jnp.min(jnp.where(in_grp & (le == v1), lane, big), axis=-1, keepdims=True)
    rest = in_grp & (lane != i1)
    le2 = jnp.where(rest, logits, NEG_INF)
    v2 = jnp.max(le2, axis=-1, keepdims=True)
    i2 = jnp.min(jnp.where(rest & (le2 == v2), lane, big), axis=-1, keepdims=True)
    e2 = jnp.exp(v2 - v1)
    w1 = 1.0 / (1.0 + e2)
    w2 = e2 / (1.0 + e2)
    return jnp.where(lane == i1, w1 * p_group, 0.0) + jnp.where(lane == i2, w2 * p_group, 0.0)


def _merge_kernel(x_ref, oa_ref, ob_ref, u_ref, yc_ref, gates_ref, mod_ref, d_ref, wglu_ref, wa_ref, wb_ref,
                  wc_ref, wout_ref, g2_ref, wrh_ref, wrl_ref, br_ref, x1_ref, h2_ref, comb_ref):
    y = u_ref[...] * d_ref[...] + yc_ref[...]
    y = y * (0.5 * (1.0 + jnp.tanh(math.sqrt(2.0 / math.pi) * (y + 0.044715 * (y * y * y)))))
    oc = y * jax.nn.sigmoid(_dot(y.astype(BF16), wglu_ref[...]))
    gate = jax.nn.sigmoid(gates_ref[...])
    merged = (gate[:, :D_MODEL] * _dot(oa_ref[...], wa_ref[...])
              + gate[:, D_MODEL:2 * D_MODEL] * _dot(ob_ref[...], wb_ref[...])
              + gate[:, 2 * D_MODEL:] * _dot(oc.astype(BF16), wc_ref[...]))
    x1 = x_ref[...] + mod_ref[2:3, :] * _dot(merged.astype(BF16), wout_ref[...])
    x1_ref[...] = x1
    h2 = _rms(x1, g2_ref[...]) * (1.0 + mod_ref[4:5, :]) + mod_ref[3:4, :]
    h2_ref[...] = h2.astype(BF16)
    h_hi, h_lo = _split_bf16(h2)
    logits = _dot(h_hi, wrh_ref[...]) + _dot(h_lo, wrh_ref[...]) + _dot(h_hi, wrl_ref[...]) + br_ref[...]
    comb_ref[...] = _route(logits)


def _merge(x, oa, ob, z, yc, mod, d_s5, wglu, wa, wb, wc, wout, g2, wr_hi, wr_lo, br, l):
    tm = 512
    full = lambda r, c: pl.BlockSpec((r, c), lambda i: (0, 0))
    return pl.pallas_call(
        _merge_kernel,
        out_shape=(jax.ShapeDtypeStruct((N_TOK, D_MODEL), F32),
                   jax.ShapeDtypeStruct((N_TOK, D_MODEL), BF16),
                   jax.ShapeDtypeStruct((N_TOK, 128), F32)),
        grid=(N_TOK // tm,),
        in_specs=[
            pl.BlockSpec((tm, D_MODEL), lambda i: (i, 0)),
            pl.BlockSpec((tm, NA_WIDTH), lambda i: (i, 0)),
            pl.BlockSpec((tm, GQA_Q_WIDTH), lambda i: (i, 0)),
            pl.BlockSpec((tm, S5_CH), lambda i: (i, COL_U // S5_CH)),
            pl.BlockSpec((tm, S5_CH), lambda i: (i, 0)),
            pl.BlockSpec((tm, N_BRANCH * D_MODEL), lambda i: (i, COL_GATES // (N_BRANCH * D_MODEL))),
            pl.BlockSpec((None, None, 6, D_MODEL), lambda i: (l, _cond_index(i, tm), 0, 0)),
            full(1, S5_CH), full(S5_CH, S5_CH), full(NA_WIDTH, D_MODEL), full(GQA_Q_WIDTH, D_MODEL),
            full(S5_CH, D_MODEL), full(D_MODEL, D_MODEL), full(1, D_MODEL),
            full(D_MODEL, 128), full(D_MODEL, 128), full(1, 128),
        ],
        out_specs=(pl.BlockSpec((tm, D_MODEL), lambda i: (i, 0)),
                   pl.BlockSpec((tm, D_MODEL), lambda i: (i, 0)),
                   pl.BlockSpec((tm, 128), lambda i: (i, 0))),
        compiler_params=pltpu.CompilerParams(
            dimension_semantics=("parallel",), vmem_limit_bytes=VMEM_LIMIT),
        name="merge",
    )(x, oa, ob, z, yc, z, mod, d_s5, wglu, wa, wb, wc, wout, g2, wr_hi, wr_lo, br)


def _moe_kernel(h_ref, x_ref, comb_ref, mod_ref, wg_ref, wu_ref, wd_ref, fg_ref, x2_ref, y_ref, acc_sc):
    e = pl.program_id(1)
    h = h_ref[...]
    a = _dot(h, wg_ref[...])
    b = _dot(h, wu_ref[...])
    comb = comb_ref[...]
    lane = lax.broadcasted_iota(jnp.int32, comb.shape, 1)
    ce = jnp.sum(jnp.where(lane == e, comb, 0.0), axis=-1, keepdims=True)
    act = (a * jax.nn.sigmoid(a)) * b * ce
    contrib = _dot(act.astype(BF16), wd_ref[...])

    @pl.when(e == 0)
    def _():
        acc_sc[...] = contrib

    @pl.when(e > 0)
    def _():
        acc_sc[...] += contrib

    @pl.when(e == MOE_EXPERTS - 1)
    def _():
        x2 = x_ref[...] + mod_ref[5:6, :] * acc_sc[...]
        x2_ref[...] = x2
        y_ref[...] = _rms(x2, fg_ref[...])


def _moe(h2, x1, comb, mod, wg, wu, wd, fg, l):
    tm = 1024
    tok = pl.BlockSpec((tm, D_MODEL), lambda i, e: (i, 0))
    out = jax.ShapeDtypeStruct((N_TOK, D_MODEL), F32)
    return pl.pallas_call(
        _moe_kernel,
        out_shape=(out, out),
        grid=(N_TOK // tm, MOE_EXPERTS),
        in_specs=[
            tok, tok,
            pl.BlockSpec((tm, 128), lambda i, e: (i, 0)),
            pl.BlockSpec((None, None, 6, D_MODEL), lambda i, e: (l, _cond_index(i, tm), 0, 0)),
            pl.BlockSpec((None, D_MODEL, EXPERT_FF), lambda i, e: (e, 0, 0)),
            pl.BlockSpec((None, D_MODEL, EXPERT_FF), lambda i, e: (e, 0, 0)),
            pl.BlockSpec((None, EXPERT_FF, D_MODEL), lambda i, e: (e, 0, 0)),
            pl.BlockSpec((1, D_MODEL), lambda i, e: (0, 0)),
        ],
        out_specs=(tok, tok),
        scratch_shapes=[pltpu.VMEM((tm, D_MODEL), F32)],
        compiler_params=pltpu.CompilerParams(
            dimension_semantics=("parallel", "arbitrary"), vmem_limit_bytes=VMEM_LIMIT),
        name="moe",
    )(h2, x1, comb, mod, wg, wu, wd, fg)


def _pack_w_in(w):
    qkv = w[:, :COL_G_V + GQA_KV_WIDTH]
    u = w[:, COL_G_V + GQA_KV_WIDTH:COL_G_V + GQA_KV_WIDTH + S5_CH]
    gates = w[:, COL_G_V + GQA_KV_WIDTH + S5_CH:]
    pad = jnp.zeros((D_MODEL, COL_U - (COL_G_V + GQA_KV_WIDTH)), w.dtype)
    return jnp.concatenate([qkv, pad, u, gates], axis=1).astype(BF16)


def kernel(x_prompt, x_sample, cache_na_kv, cache_gqa_kv, state_ssm, c, c_ctx, norm_g, w_ada, b_ada, w_in, na_rpb, gqa_sink, s5_a_re, s5_a_im, s5_log_dt, s5_b_re, s5_b_im, s5_c_re, s5_c_im, s5_d, s5_w_glu, w_branch_a, w_branch_b, w_branch_c, w_out, moe_w_group, moe_b_group, moe_w_expert, moe_b_expert, moe_w_gate, moe_w_up, moe_w_down, final_g):
    cond = jnp.zeros((N_COND, D_MODEL), F32).at[0].set(c_ctx.astype(F32)).at[1:1 + DEC_BATCH].set(c.astype(F32))
    mod = _ada_mod(cond, w_ada.astype(F32), b_ada.astype(F32))

    x = jnp.concatenate([x_prompt.reshape(N_CTX_TOK, D_MODEL), x_sample.reshape(N_LAT_TOK, D_MODEL)], axis=0)
    cache_na = cache_na_kv.reshape(DEC_BATCH, DEPTH, 2, PAST_LEN, NA_WIDTH)
    cache_gqa = cache_gqa_kv.reshape(DEC_BATCH, DEPTH, 2, PAST_LEN, GQA_KV_WIDTH)
    cos_d, sin_d = _rope_tables()
    cos_q, sin_q = jnp.tile(cos_d, (1, GQA_Q_HEADS)), jnp.tile(sin_d, (1, GQA_Q_HEADS))
    cos_k, sin_k = jnp.tile(cos_d, (1, GQA_KV_HEADS)), jnp.tile(sin_d, (1, GQA_KV_HEADS))
    ctx_steps, lat_steps = SEQ // S5_T, DEC_SEQ // S5_T
    gn = S5_GROUPS * S5_STATE
    zeros_h0 = jnp.zeros((2, BATCH, gn), F32)
    fg = final_g.astype(F32).reshape(1, D_MODEL)

    na_list, gqa_list, ssm_list = [], [], []
    y = None
    for l in range(DEPTH):
        z = _inproj(x, norm_g[l, 0].astype(F32).reshape(1, D_MODEL), mod, _pack_w_in(w_in[l]), l)

        zc = z[:N_CTX_TOK]
        na_list.append(jnp.stack([zc[:, COL_NA_K:COL_NA_K + NA_WIDTH].reshape(BATCH, SEQ, NA_HEADS, HEAD_DIM),
                                  zc[:, COL_NA_V:COL_NA_V + NA_WIDTH].reshape(BATCH, SEQ, NA_HEADS, HEAD_DIM)],
                                 axis=1))
        gqa_list.append(jnp.stack(
            [zc[:, COL_G_K:COL_G_K + GQA_KV_WIDTH].reshape(BATCH, SEQ, GQA_KV_HEADS, HEAD_DIM),
             zc[:, COL_G_V:COL_G_V + GQA_KV_WIDTH].reshape(BATCH, SEQ, GQA_KV_HEADS, HEAD_DIM)], axis=1))

        sink = gqa_sink[l].astype(F32)
        oa_c, ob_c = _ctx_attn(z, sink)
        oa_l = _na_lat(z, cache_na, _na_bias_tables(na_rpb[l]), l)
        ob_l = _swa_lat(z, cache_gqa, sink, cos_q, sin_q, cos_k, sin_k, l)

        ops = _s5_operators(s5_a_re[l], s5_a_im[l], s5_log_dt[l], s5_b_re[l], s5_b_im[l], s5_c_re[l], s5_c_im[l])
        ops = {k: (v.astype(BF16) if k in ("bs", "m", "cre", "cim") else v) for k, v in ops.items()}
        u = z[:, COL_U:COL_U + S5_CH].astype(BF16)
        yc_c, fre, fim = _s5_scan(_to_superstep(u[:N_CTX_TOK], BATCH, ctx_steps, BATCH), ops,
                                  zeros_h0, zeros_h0, BATCH, ctx_steps)
        st = state_ssm[:, l].astype(F32).reshape(DEC_BATCH, 2, 2, gn)
        h0 = jnp.pad(st.transpose(2, 1, 0, 3), ((0, 0), (0, 0), (0, S5_LAT_B - DEC_BATCH), (0, 0)))
        yc_l, _, _ = _s5_scan(_to_superstep(u[N_CTX_TOK:], DEC_BATCH, lat_steps, S5_LAT_B), ops,
                              h0[0], h0[1], S5_LAT_B, lat_steps)
        yc = jnp.concatenate([_from_superstep(yc_c, BATCH, ctx_steps, BATCH),
                              _from_superstep(yc_l, DEC_BATCH, lat_steps, S5_LAT_B)], axis=0)
        ssm_list.append(jnp.stack([fre, fim], axis=2).transpose(1, 0, 2, 3)
                        .reshape(BATCH, 2, 2, S5_GROUPS, S5_STATE).astype(x_prompt.dtype))

        oa = jnp.concatenate([oa_c, oa_l], axis=0)
        ob = jnp.concatenate([ob_c, ob_l], axis=0)
        wr = jnp.zeros((D_MODEL, 128), F32)
        wr = wr.at[:, :MOE_EXPERTS].set(moe_w_expert[l].astype(F32))
        wr = wr.at[:, MOE_EXPERTS:MOE_EXPERTS + MOE_GROUPS].set(moe_w_group[l].astype(F32))
        br = jnp.zeros((1, 128), F32)
        br = br.at[0, :MOE_EXPERTS].set(moe_b_expert[l].astype(F32))
        br = br.at[0, MOE_EXPERTS:MOE_EXPERTS + MOE_GROUPS].set(moe_b_group[l].astype(F32))
        wr_hi, wr_lo = _split_bf16(wr)
        x1, h2, comb = _merge(
            x, oa, ob, z, yc, mod, s5_d[l].astype(F32).reshape(1, S5_CH), s5_w_glu[l].astype(BF16),
            w_branch_a[l].astype(BF16), w_branch_b[l].astype(BF16), w_branch_c[l].astype(BF16),
            w_out[l].astype(BF16), norm_g[l, 1].astype(F32).reshape(1, D_MODEL), wr_hi, wr_lo, br, l)
        x, y = _moe(h2, x1, comb, mod, moe_w_gate[l].astype(BF16), moe_w_up[l].astype(BF16),
                    moe_w_down[l].astype(BF16), fg, l)

    y_prompt = y[:N_CTX_TOK].reshape(BATCH, SEQ, D_MODEL)
    y_sample = y[N_CTX_TOK:].reshape(DEC_BATCH, DEC_SEQ, D_MODEL)
    return (y_prompt, y_sample, jnp.stack(na_list, axis=1), jnp.stack(gqa_list, axis=1),
            jnp.stack(ssm_list, axis=1))
```

```python
import functools
import math

import numpy as np
import jax
import jax.numpy as jnp
from jax import lax
from jax.experimental import pallas as pl
from jax.experimental.pallas import tpu as pltpu

F32 = jnp.float32
BF16 = jnp.bfloat16

D_MODEL = 1024
BATCH = 32
SEQ = 256
DEPTH = 2
DEC_BATCH = 4
DEC_SEQ = 2048
PAST_LEN = 512
GRID_W = 64
GRID_ROWS = DEC_SEQ // GRID_W
HEAD_DIM = 64
NA_HEADS = 8
NA_WIN_H = 8
NA_WIN_W = 16
GQA_Q_HEADS = 8
GQA_KV_HEADS = 2
GQA_GROUP = GQA_Q_HEADS // GQA_KV_HEADS
SWA_WINDOW = 128
SWA_BLOCK = 128
ROPE_THETA = 10000.0
S5_CH = 512
S5_GROUP_CH = 16
S5_GROUPS = S5_CH // S5_GROUP_CH
S5_STATE = 64
N_BRANCH = 3
NA_WIDTH = NA_HEADS * HEAD_DIM
GQA_Q_WIDTH = GQA_Q_HEADS * HEAD_DIM
GQA_KV_WIDTH = GQA_KV_HEADS * HEAD_DIM
MOE_GROUPS = 4
MOE_EXPERTS_PER_GROUP = 4
MOE_EXPERTS = MOE_GROUPS * MOE_EXPERTS_PER_GROUP
EXPERT_FF = 512
EPS = 1e-6
NEG_INF = -1e30

N_CTX_TOK = BATCH * SEQ
N_LAT_TOK = DEC_BATCH * DEC_SEQ
N_TOK = N_CTX_TOK + N_LAT_TOK
N_COND = 8

COL_NA_Q = 0
COL_NA_K = 512
COL_NA_V = 1024
COL_G_Q = 1536
COL_G_K = 2048
COL_G_V = 2176
COL_U = 2560
COL_GATES = 3072
Z_COLS = 6144

S5_T = 16
S5_PAIRS = S5_GROUPS // 2
S5_LAT_B = 8

NA_QROWS = 4
NA_KROWS = 12

VMEM_LIMIT = 56 * 1024 * 1024


def _dot(a, b):
    return jnp.dot(a, b, preferred_element_type=F32)


def _dot_nt(a, b):
    return lax.dot_general(a, b, (((1,), (1,)), ((), ())), preferred_element_type=F32)


def _split_bf16(x):
    hi = x.astype(BF16)
    lo = (x - hi.astype(F32)).astype(BF16)
    return hi, lo


def _rms(x, g):
    return x * lax.rsqrt(jnp.mean(x * x, axis=-1, keepdims=True) + EPS) * g


def _cond_index(i, tm):
    nctx = N_CTX_TOK // tm
    return jnp.where(i < nctx, 0, 1 + ((i - nctx) * tm) // DEC_SEQ)


def _ada_kernel(c_ref, w_ref, b_ref, o_ref):
    c = c_ref[...]
    s = c * jax.nn.sigmoid(c)
    s_hi, s_lo = _split_bf16(s)
    w_hi, w_lo = _split_bf16(w_ref[...])
    o_ref[...] = _dot(s_hi, w_hi) + _dot(s_lo, w_hi) + _dot(s_hi, w_lo) + b_ref[...]


def _ada_mod(cond, w_ada, b_ada):
    tn = 1536
    n = 6 * D_MODEL
    out = pl.pallas_call(
        _ada_kernel,
        out_shape=jax.ShapeDtypeStruct((DEPTH, N_COND, n), F32),
        grid=(DEPTH, n // tn),
        in_specs=[
            pl.BlockSpec((N_COND, D_MODEL), lambda l, j: (0, 0)),
            pl.BlockSpec((None, D_MODEL, tn), lambda l, j: (l, 0, j)),
            pl.BlockSpec((None, 1, tn), lambda l, j: (l, 0, j)),
        ],
        out_specs=pl.BlockSpec((None, N_COND, tn), lambda l, j: (l, 0, j)),
        compiler_params=pltpu.CompilerParams(
            dimension_semantics=("parallel", "parallel"), vmem_limit_bytes=VMEM_LIMIT),
        name="ada_mod",
    )(cond, w_ada, b_ada.reshape(DEPTH, 1, n))
    return out.reshape(DEPTH, N_COND, 6, D_MODEL)


def _inproj_kernel(x_ref, g_ref, mod_ref, w_ref, o_ref, h_sc):
    @pl.when(pl.program_id(1) == 0)
    def _():
        h = _rms(x_ref[...], g_ref[...]) * (1.0 + mod_ref[1:2, :]) + mod_ref[0:1, :]
        h_sc[...] = h.astype(BF16)

    o_ref[...] = _dot(h_sc[...], w_ref[...])


def _inproj(x, g, mod, w, l):
    tm, tn = 1024, 1536
    return pl.pallas_call(
        _inproj_kernel,
        out_shape=jax.ShapeDtypeStruct((N_TOK, Z_COLS), F32),
        grid=(N_TOK // tm, Z_COLS // tn),
        in_specs=[
            pl.BlockSpec((tm, D_MODEL), lambda i, j: (i, 0)),
            pl.BlockSpec((1, D_MODEL), lambda i, j: (0, 0)),
            pl.BlockSpec((None, None, 6, D_MODEL), lambda i, j: (l, _cond_index(i, tm), 0, 0)),
            pl.BlockSpec((D_MODEL, tn), lambda i, j: (0, j)),
        ],
        out_specs=pl.BlockSpec((tm, tn), lambda i, j: (i, j)),
        scratch_shapes=[pltpu.VMEM((tm, D_MODEL), BF16)],
        compiler_params=pltpu.CompilerParams(
            dimension_semantics=("parallel", "arbitrary"), vmem_limit_bytes=VMEM_LIMIT),
        name="inproj",
    )(x, g, mod, w)


def _ctx_attn_kernel(sink_ref, na_ref, gq_ref, gkv_ref, oa_ref, ob_ref):
    scale = HEAD_DIM ** -0.5
    na = na_ref[...].astype(BF16)
    gq = gq_ref[...].astype(BF16)
    gkv = gkv_ref[...].astype(BF16)
    for h in range(NA_HEADS):
        q = na[:, COL_NA_Q + h * HEAD_DIM:COL_NA_Q + (h + 1) * HEAD_DIM]
        k = na[:, COL_NA_K + h * HEAD_DIM:COL_NA_K + (h + 1) * HEAD_DIM]
        v = na[:, COL_NA_V + h * HEAD_DIM:COL_NA_V + (h + 1) * HEAD_DIM]
        s = _dot_nt(q, k) * scale
        m = jnp.max(s, axis=-1, keepdims=True)
        p = jnp.exp(s - m)
        d = jnp.sum(p, axis=-1, keepdims=True)
        o = _dot(p.astype(BF16), v) / d
        oa_ref[:, h * HEAD_DIM:(h + 1) * HEAD_DIM] = o.astype(BF16)
    for h in range(GQA_Q_HEADS):
        kh = h // GQA_GROUP
        q = gq[:, h * HEAD_DIM:(h + 1) * HEAD_DIM]
        k = gkv[:, kh * HEAD_DIM:(kh + 1) * HEAD_DIM]
        v = gkv[:, GQA_KV_WIDTH + kh * HEAD_DIM:GQA_KV_WIDTH + (kh + 1) * HEAD_DIM]
        sink = sink_ref[h]
        s = _dot_nt(q, k) * scale
        m = jnp.maximum(jnp.max(s, axis=-1, keepdims=True), sink)
        p = jnp.exp(s - m)
        d = jnp.sum(p, axis=-1, keepdims=True) + jnp.exp(sink - m)
        o = _dot(p.astype(BF16), v) / d
        ob_ref[:, h * HEAD_DIM:(h + 1) * HEAD_DIM] = o.astype(BF16)


def _ctx_attn(z, sink):
    out = jax.ShapeDtypeStruct((N_CTX_TOK, NA_WIDTH), BF16)
    return pl.pallas_call(
        _ctx_attn_kernel,
        out_shape=(out, out),
        grid=(BATCH,),
        in_specs=[
            pl.BlockSpec(memory_space=pltpu.SMEM),
            pl.BlockSpec((SEQ, 3 * NA_WIDTH), lambda b: (b, 0)),
            pl.BlockSpec((SEQ, GQA_Q_WIDTH), lambda b: (b, COL_G_Q // GQA_Q_WIDTH)),
            pl.BlockSpec((SEQ, 2 * GQA_KV_WIDTH), lambda b: (b, COL_G_K // (2 * GQA_KV_WIDTH))),
        ],
        out_specs=(pl.BlockSpec((SEQ, NA_WIDTH), lambda b: (b, 0)),
                   pl.BlockSpec((SEQ, GQA_Q_WIDTH), lambda b: (b, 0))),
        compiler_params=pltpu.CompilerParams(
            dimension_semantics=("parallel",), vmem_limit_bytes=VMEM_LIMIT),
        name="ctx_attn",
    )(sink, z, z, z)


def _na_bias_tables(rpb):
    kh = min(NA_WIN_H, GRID_ROWS)
    n_dc = 2 * NA_WIN_W - 1
    qc = np.arange(GRID_W)
    kc = np.arange(GRID_W)
    cstart = np.clip(qc - NA_WIN_W // 2, 0, GRID_W - NA_WIN_W)
    col_ok = (kc[None, :] >= cstart[:, None]) & (kc[None, :] < cstart[:, None] + NA_WIN_W)
    dc = np.clip(kc[None, :] - qc[:, None] + (NA_WIN_W - 1), 0, n_dc - 1)
    onehot = ((np.arange(n_dc)[:, None, None] == dc[None]) & col_ok[None]).astype(np.float32)
    band = jnp.einsum("hrd,dqk->hrqk", rpb.astype(F32), onehot, precision=lax.Precision.HIGHEST)
    band = jnp.where(col_ok, band, NEG_INF)
    band = jnp.pad(band, ((0, 0), (NA_KROWS, NA_KROWS), (0, 0), (0, 0)))
    pats = []
    for r0, ks in ((0, 0), (NA_QROWS, 0), (GRID_ROWS - NA_QROWS, GRID_ROWS - NA_KROWS)):
        rows = []
        for qr in range(NA_QROWS):
            r = r0 + qr
            st = min(max(r - kh // 2, 0), GRID_ROWS - kh)
            krow = ks + np.arange(NA_KROWS)
            row_ok = (krow >= st) & (krow < st + kh)
            lo = ks - r + (NA_WIN_H - 1) + NA_KROWS
            rows.append(jnp.where(row_ok[None, :, None, None], band[:, lo:lo + NA_KROWS], NEG_INF))
        pats.append(jnp.stack(rows, axis=1))
    bias = jnp.stack(pats).transpose(0, 1, 2, 4, 3, 5)
    return bias.reshape(3, NA_HEADS, NA_QROWS * GRID_W, NA_KROWS * GRID_W)


def _na_lat_kernel(q_ref, k_ref, v_ref, ck_ref, cv_ref, bias_ref, o_ref):
    i = pl.program_id(1)
    scale = HEAD_DIM ** -0.5
    nk = NA_KROWS * GRID_W
    ks = jnp.clip(i * NA_QROWS - NA_WIN_H // 2, 0, GRID_ROWS - NA_KROWS)
    start = pl.multiple_of(ks * GRID_W, GRID_W)
    q = q_ref[...].astype(BF16)
    kw = k_ref[pl.ds(start, nk), :].astype(BF16)
    vw = v_ref[pl.ds(start, nk), :].astype(BF16)
    ck = ck_ref[...].astype(BF16)
    cv = cv_ref[...].astype(BF16)
    for h in range(NA_HEADS):
        sl = slice(h * HEAD_DIM, (h + 1) * HEAD_DIM)
        qh = q[:, sl]
        s_loc = _dot_nt(qh, kw[:, sl]) * scale + bias_ref[h]
        s_ctx = _dot_nt(qh, ck[:, sl]) * scale
        m = jnp.maximum(jnp.max(s_loc, axis=-1, keepdims=True), jnp.max(s_ctx, axis=-1, keepdims=True))
        p_loc = jnp.exp(s_loc - m)
        p_ctx = jnp.exp(s_ctx - m)
        d = jnp.sum(p_loc, axis=-1, keepdims=True) + jnp.sum(p_ctx, axis=-1, keepdims=True)
        o = (_dot(p_loc.astype(BF16), vw[:, sl]) + _dot(p_ctx.astype(BF16), cv[:, sl])) / d
        o_ref[:, sl] = o.astype(BF16)


def _na_lat(z, cache_na, bias, l):
    tq = NA_QROWS * GRID_W
    nsteps = GRID_ROWS // NA_QROWS
    lat_blk = N_CTX_TOK // DEC_SEQ

    def pat(i):
        return jnp.where(i == 0, 0, jnp.where(i == nsteps - 1, 2, 1))

    return pl.pallas_call(
        _na_lat_kernel,
        out_shape=jax.ShapeDtypeStruct((N_LAT_TOK, NA_WIDTH), BF16),
        grid=(DEC_BATCH, nsteps),
        in_specs=[
            pl.BlockSpec((tq, NA_WIDTH), lambda b, i: (N_CTX_TOK // tq + b * nsteps + i, COL_NA_Q // NA_WIDTH)),
            pl.BlockSpec((DEC_SEQ, NA_WIDTH), lambda b, i: (lat_blk + b, COL_NA_K // NA_WIDTH)),
            pl.BlockSpec((DEC_SEQ, NA_WIDTH), lambda b, i: (lat_blk + b, COL_NA_V // NA_WIDTH)),
            pl.BlockSpec((None, None, None, PAST_LEN, NA_WIDTH), lambda b, i: (b, l, 0, 0, 0)),
            pl.BlockSpec((None, None, None, PAST_LEN, NA_WIDTH), lambda b, i: (b, l, 1, 0, 0)),
            pl.BlockSpec((None, NA_HEADS, tq, NA_KROWS * GRID_W), lambda b, i: (pat(i), 0, 0, 0)),
        ],
        out_specs=pl.BlockSpec((tq, NA_WIDTH), lambda b, i: (b * nsteps + i, 0)),
        compiler_params=pltpu.CompilerParams(
            dimension_semantics=("parallel", "arbitrary"), vmem_limit_bytes=VMEM_LIMIT),
        name="na_lat",
    )(z, z, z, cache_na, cache_na, bias)


def _rope_tables():
    nf = HEAD_DIM // 4
    t = jnp.arange(DEC_SEQ)
    pos = jnp.stack([t // GRID_W, t % GRID_W], axis=-1).astype(F32)
    inv = ROPE_THETA ** (-jnp.arange(nf, dtype=F32) / nf)
    ang = pos[:, :, None] * inv
    cos = jnp.cos(ang)
    sin = jnp.sin(ang)
    cos_d = jnp.stack([cos, cos], axis=2).reshape(DEC_SEQ, HEAD_DIM)
    sin_d = jnp.stack([-sin, sin], axis=2).reshape(DEC_SEQ, HEAD_DIM)
    return cos_d, sin_d


def _rope(x, cos, sin_signed):
    n = x.shape[-1]
    nf = HEAD_DIM // 4
    lane = lax.broadcasted_iota(jnp.int32, x.shape, 1)
    first_half = (lane // nf) % 2 == 0
    partner = jnp.where(first_half, pltpu.roll(x, n - nf, 1), pltpu.roll(x, nf, 1))
    return x * cos + partner * sin_signed


def _swa_lat_kernel(sink_ref, q_ref, kv_ref, cq_ref, sq_ref, ckt_ref, skt_ref, ck_ref, cv_ref, o_ref, k_sc, v_sc):
    n = pl.program_id(1)
    scale = HEAD_DIM ** -0.5
    nwin = 3 * SWA_BLOCK

    @pl.when(n == 0)
    def _():
        kv = kv_ref[...]
        k_sc[...] = _rope(kv[:, :GQA_KV_WIDTH], ckt_ref[...], skt_ref[...]).astype(BF16)
        v_sc[...] = kv[:, GQA_KV_WIDTH:].astype(BF16)

    q = _rope(q_ref[...], cq_ref[...], sq_ref[...]).astype(BF16)
    start = pl.multiple_of(jnp.clip((n - 1) * SWA_BLOCK, 0, DEC_SEQ - nwin), SWA_BLOCK)
    kw = k_sc[pl.ds(start, nwin), :]
    vw = v_sc[pl.ds(start, nwin), :]
    ck = ck_ref[...].astype(BF16)
    cv = cv_ref[...].astype(BF16)
    rows = GQA_GROUP * SWA_BLOCK
    row = lax.broadcasted_iota(jnp.int32, (rows, nwin), 0)
    col = lax.broadcasted_iota(jnp.int32, (rows, nwin), 1)
    qpos = n * SWA_BLOCK + row % SWA_BLOCK
    kpos = start + col
    ok = jnp.abs(qpos - kpos) <= SWA_WINDOW
    grp = lax.broadcasted_iota(jnp.int32, (rows, 1), 0) // SWA_BLOCK
    for kh in range(GQA_KV_HEADS):
        sl = slice(kh * HEAD_DIM, (kh + 1) * HEAD_DIM)
        q4 = jnp.concatenate(
            [q[:, (kh * GQA_GROUP + g) * HEAD_DIM:(kh * GQA_GROUP + g + 1) * HEAD_DIM] for g in range(GQA_GROUP)],
            axis=0)
        sink = jnp.zeros((rows, 1), F32)
        for g in range(GQA_GROUP):
            sink = jnp.where(grp == g, sink_ref[kh * GQA_GROUP + g], sink)
        s_loc = jnp.where(ok, _dot_nt(q4, kw[:, sl]) * scale, NEG_INF)
        s_ctx = _dot_nt(q4, ck[:, sl]) * scale
        m = jnp.maximum(jnp.maximum(jnp.max(s_loc, axis=-1, keepdims=True),
                                    jnp.max(s_ctx, axis=-1, keepdims=True)), sink)
        p_loc = jnp.exp(s_loc - m)
        p_ctx = jnp.exp(s_ctx - m)
        d = (jnp.sum(p_loc, axis=-1, keepdims=True) + jnp.sum(p_ctx, axis=-1, keepdims=True)
             + jnp.exp(sink - m))
        o4 = (_dot(p_loc.astype(BF16), vw[:, sl]) + _dot(p_ctx.astype(BF16), cv[:, sl])) / d
        for g in range(GQA_GROUP):
            h = kh * GQA_GROUP + g
            o_ref[:, h * HEAD_DIM:(h + 1) * HEAD_DIM] = o4[g * SWA_BLOCK:(g + 1) * SWA_BLOCK].astype(BF16)


def _swa_lat(z, cache_gqa, sink, cos_q, sin_q, cos_k, sin_k, l):
    nb = DEC_SEQ // SWA_BLOCK
    lat_blk = N_CTX_TOK // DEC_SEQ
    return pl.pallas_call(
        _swa_lat_kernel,
        out_shape=jax.ShapeDtypeStruct((N_LAT_TOK, GQA_Q_WIDTH), BF16),
        grid=(DEC_BATCH, nb),
        in_specs=[
            pl.BlockSpec(memory_space=pltpu.SMEM),
            pl.BlockSpec((SWA_BLOCK, GQA_Q_WIDTH),
                         lambda b, n: (N_CTX_TOK // SWA_BLOCK + b * nb + n, COL_G_Q // GQA_Q_WIDTH)),
            pl.BlockSpec((DEC_SEQ, 2 * GQA_KV_WIDTH), lambda b, n: (lat_blk + b, COL_G_K // (2 * GQA_KV_WIDTH))),
            pl.BlockSpec((SWA_BLOCK, GQA_Q_WIDTH), lambda b, n: (n, 0)),
            pl.BlockSpec((SWA_BLOCK, GQA_Q_WIDTH), lambda b, n: (n, 0)),
            pl.BlockSpec((DEC_SEQ, GQA_KV_WIDTH), lambda b, n: (0, 0)),
            pl.BlockSpec((DEC_SEQ, GQA_KV_WIDTH), lambda b, n: (0, 0)),
            pl.BlockSpec((None, None, None, PAST_LEN, GQA_KV_WIDTH), lambda b, n: (b, l, 0, 0, 0)),
            pl.BlockSpec((None, None, None, PAST_LEN, GQA_KV_WIDTH), lambda b, n: (b, l, 1, 0, 0)),
        ],
        out_specs=pl.BlockSpec((SWA_BLOCK, GQA_Q_WIDTH), lambda b, n: (b * nb + n, 0)),
        scratch_shapes=[pltpu.VMEM((DEC_SEQ, GQA_KV_WIDTH), BF16), pltpu.VMEM((DEC_SEQ, GQA_KV_WIDTH), BF16)],
        compiler_params=pltpu.CompilerParams(
            dimension_semantics=("parallel", "arbitrary"), vmem_limit_bytes=VMEM_LIMIT),
        name="swa_lat",
    )(sink, z, z, cos_q, sin_q, cos_k, sin_k, cache_gqa, cache_gqa)


def _s5_operators(a_re, a_im, log_dt, b_re, b_im, c_re, c_im):
    T, G, N, C = S5_T, S5_GROUPS, S5_STATE, S5_GROUP_CH
    tau = jnp.arange(T + 1, dtype=F32)
    out = {k: [] for k in ("bs", "m", "cre", "cim", "are", "aim")}
    jj, ss = np.meshgrid(np.arange(T), np.arange(T), indexing="ij")
    for d in range(2):
        A = lax.complex(a_re[d].astype(F32), a_im[d].astype(F32))
        dt = jnp.exp(log_dt[d].astype(F32))[:, None]
        a_bar = jnp.exp(A * dt)
        pw = jnp.exp((A * dt)[None] * tau[:, None, None])
        b_bar = ((a_bar - 1.0) / A)[..., None] * lax.complex(b_re[d].astype(F32), b_im[d].astype(F32))
        c_mat = lax.complex(c_re[d].astype(F32), c_im[d].astype(F32))
        kern = jnp.einsum("gon,tgn,gni->gtoi", c_mat, pw[:T], b_bar, precision=lax.Precision.HIGHEST).real
        if d == 0:
            lag, ok = jj - ss, jj >= ss
            p_in = pw[T - 1 - np.arange(T)]
            p_out = pw[1 + np.arange(T)]
        else:
            lag, ok = ss - jj, ss >= jj
            p_in = pw[np.arange(T)]
            p_out = pw[T - np.arange(T)]
        m = kern[:, np.clip(lag, 0, T - 1)]
        m = jnp.where(ok[None, :, :, None, None], m, 0.0).transpose(0, 2, 4, 1, 3).reshape(G, T * C, T * C)
        bs = (p_in[:, :, :, None] * b_bar[None]).transpose(1, 0, 3, 2).reshape(G, T * C, N)
        co = (c_mat[None] * p_out[:, :, None, :]).transpose(1, 3, 0, 2).reshape(G, N, T * C)

        def pair_diag(x):
            x = x.reshape(S5_PAIRS, 2, x.shape[1], x.shape[2])
            zero = jnp.zeros_like(x[:, 0])
            return jnp.concatenate([jnp.concatenate([x[:, 0], zero], axis=2),
                                    jnp.concatenate([zero, x[:, 1]], axis=2)], axis=1)

        out["bs"].append(jnp.concatenate([pair_diag(bs.real), pair_diag(bs.imag)], axis=2))
        out["m"].append(pair_diag(m))
        out["cre"].append(pair_diag(co.real))
        out["cim"].append(pair_diag(-co.imag))
        out["are"].append(pw[T].real.reshape(1, G * N))
        out["aim"].append(pw[T].imag.reshape(1, G * N))
    return {k: jnp.stack(v) for k, v in out.items()}


def _s5_kernel(x_ref, bs_ref, m_ref, cre_ref, cim_ref, are_ref, aim_ref, h0re_ref, h0im_ref,
               y_ref, fre_ref, fim_ref, pre_sc, pim_sc, hre_sc, him_sc, *, nb, nsteps):
    d = pl.program_id(1)
    half = 2 * S5_STATE
    x = x_ref[...]
    p = _dot(x, bs_ref[...])
    pre_sc[...] = p[:, :half]
    pim_sc[...] = p[:, half:]
    ar = jnp.broadcast_to(are_ref[...], (nb, half))
    ai = jnp.broadcast_to(aim_ref[...], (nb, half))

    def body(i, carry):
        hr, hi = carry
        k = jnp.where(d == 0, i, nsteps - 1 - i)
        r0 = pl.multiple_of(k * nb, nb)
        hre_sc[pl.ds(r0, nb), :] = hr
        him_sc[pl.ds(r0, nb), :] = hi
        nr = ar * hr - ai * hi + pre_sc[pl.ds(r0, nb), :]
        ni = ar * hi + ai * hr + pim_sc[pl.ds(r0, nb), :]
        return nr, ni

    hr, hi = lax.fori_loop(0, nsteps, body, (h0re_ref[...], h0im_ref[...]))
    fre_ref[...] = hr
    fim_ref[...] = hi
    y = (_dot(x, m_ref[...]) + _dot(hre_sc[...].astype(BF16), cre_ref[...])
         + _dot(him_sc[...].astype(BF16), cim_ref[...]))

    @pl.when(d == 0)
    def _():
        y_ref[...] = y

    @pl.when(d == 1)
    def _():
        y_ref[...] += y


def _s5_scan(xs, ops, h0re, h0im, nb, nsteps):
    rows = nsteps * nb
    pw = 2 * S5_T * S5_GROUP_CH
    half = 2 * S5_STATE
    st = jax.ShapeDtypeStruct((2, nb, S5_GROUPS * S5_STATE), F32)
    wspec = lambda r, c: pl.BlockSpec((None, None, r, c), lambda p, d: (d, p, 0, 0))
    aspec = pl.BlockSpec((None, 1, half), lambda p, d: (d, 0, p))
    hspec = pl.BlockSpec((None, nb, half), lambda p, d: (d, 0, p))
    return pl.pallas_call(
        functools.partial(_s5_kernel, nb=nb, nsteps=nsteps),
        out_shape=(jax.ShapeDtypeStruct((rows, S5_GROUPS * S5_T * S5_GROUP_CH), F32), st, st),
        grid=(S5_PAIRS, 2),
        in_specs=[
            pl.BlockSpec((rows, pw), lambda p, d: (0, p)),
            wspec(pw, 2 * half), wspec(pw, pw), wspec(half, pw), wspec(half, pw),
            aspec, aspec, hspec, hspec,
        ],
        out_specs=(pl.BlockSpec((rows, pw), lambda p, d: (0, p)), hspec, hspec),
        scratch_shapes=[pltpu.VMEM((rows, half), F32) for _ in range(4)],
        compiler_params=pltpu.CompilerParams(
            dimension_semantics=("parallel", "arbitrary"), vmem_limit_bytes=VMEM_LIMIT),
        name="s5_scan",
    )(xs, ops["bs"], ops["m"], ops["cre"], ops["cim"], ops["are"], ops["aim"], h0re, h0im)


def _to_superstep(u, nbatch, nsteps, pad_to):
    x = u.reshape(nbatch, nsteps, S5_T, S5_GROUPS, S5_GROUP_CH).transpose(1, 0, 3, 2, 4)
    if pad_to > nbatch:
        x = jnp.pad(x, ((0, 0), (0, pad_to - nbatch), (0, 0), (0, 0), (0, 0)))
    return x.reshape(nsteps * pad_to, S5_GROUPS * S5_T * S5_GROUP_CH)


def _from_superstep(y, nbatch, nsteps, padded):
    x = y.reshape(nsteps, padded, S5_GROUPS, S5_T, S5_GROUP_CH)[:, :nbatch]
    return x.transpose(1, 0, 3, 2, 4).reshape(nbatch * nsteps * S5_T, S5_CH)


def _route(logits):
    lane = lax.broadcasted_iota(jnp.int32, logits.shape, 1)
    big = jnp.int32(1 << 20)
    is_g = (lane >= MOE_EXPERTS) & (lane < MOE_EXPERTS + MOE_GROUPS)
    lg = jnp.where(is_g, logits, NEG_INF)
    gmax = jnp.max(lg, axis=-1, keepdims=True)
    gsel = jnp.min(jnp.where(is_g & (lg == gmax), lane, big), axis=-1, keepdims=True) - MOE_EXPERTS
    p_group = 1.0 / jnp.sum(jnp.where(is_g, jnp.exp(lg - gmax), 0.0), axis=-1, keepdims=True)
    in_grp = (lane < MOE_EXPERTS) & (lane // MOE_EXPERTS_PER_GROUP == gsel)
    le = jnp.where(in_grp, logits, NEG_INF)
    v1 = jnp.max(le, axis=-1, keepdims=True)
    i1 = jnp.min(jnp.where(in_grp & (le == v1), lane, big), axis=-1, keepdims=True)
    rest = in_grp & (lane != i1)
    le2 = jnp.where(rest, logits, NEG_INF)
    v2 = jnp.max(le2, axis=-1, keepdims=True)
    i2 = jnp.min(jnp.where(rest & (le2 == v2), lane, big), axis=-1, keepdims=True)
    e2 = jnp.exp(v2 - v1)
    w1 = 1.0 / (1.0 + e2)
    w2 = e2 / (1.0 + e2)
    return jnp.where(lane == i1, w1 * p_group, 0.0) + jnp.where(lane == i2, w2 * p_group, 0.0)


def _merge_kernel(x_ref, oa_ref, ob_ref, u_ref, yc_ref, gates_ref, mod_ref, d_ref, wglu_ref, wa_ref, wb_ref,
                  wc_ref, wout_ref, g2_ref, wrh_ref, wrl_ref, br_ref, x1_ref, h2_ref, comb_ref):
    y = u_ref[...] * d_ref[...] + yc_ref[...]
    y = y * (0.5 * (1.0 + jnp.tanh(math.sqrt(2.0 / math.pi) * (y + 0.044715 * (y * y * y)))))
    oc = y * jax.nn.sigmoid(_dot(y.astype(BF16), wglu_ref[...]))
    gate = jax.nn.sigmoid(gates_ref[...])
    merged = (gate[:, :D_MODEL] * _dot(oa_ref[...], wa_ref[...])
              + gate[:, D_MODEL:2 * D_MODEL] * _dot(ob_ref[...], wb_ref[...])
              + gate[:, 2 * D_MODEL:] * _dot(oc.astype(BF16), wc_ref[...]))
    x1 = x_ref[...] + mod_ref[2:3, :] * _dot(merged.astype(BF16), wout_ref[...])
    x1_ref[...] = x1
    h2 = _rms(x1, g2_ref[...]) * (1.0 + mod_ref[4:5, :]) + mod_ref[3:4, :]
    h2_ref[...] = h2.astype(BF16)
    h_hi, h_lo = _split_bf16(h2)
    logits = _dot(h_hi, wrh_ref[...]) + _dot(h_lo, wrh_ref[...]) + _dot(h_hi, wrl_ref[...]) + br_ref[...]
    comb_ref[...] = _route(logits)


def _merge(x, oa, ob, z, yc, mod, d_s5, wglu, wa, wb, wc, wout, g2, wr_hi, wr_lo, br, l):
    tm = 512
    full = lambda r, c: pl.BlockSpec((r, c), lambda i: (0, 0))
    return pl.pallas_call(
        _merge_kernel,
        out_shape=(jax.ShapeDtypeStruct((N_TOK, D_MODEL), F32),
                   jax.ShapeDtypeStruct((N_TOK, D_MODEL), BF16),
                   jax.ShapeDtypeStruct((N_TOK, 128), F32)),
        grid=(N_TOK // tm,),
        in_specs=[
            pl.BlockSpec((tm, D_MODEL), lambda i: (i, 0)),
            pl.BlockSpec((tm, NA_WIDTH), lambda i: (i, 0)),
            pl.BlockSpec((tm, GQA_Q_WIDTH), lambda i: (i, 0)),
            pl.BlockSpec((tm, S5_CH), lambda i: (i, COL_U // S5_CH)),
            pl.BlockSpec((tm, S5_CH), lambda i: (i, 0)),
            pl.BlockSpec((tm, N_BRANCH * D_MODEL), lambda i: (i, COL_GATES // (N_BRANCH * D_MODEL))),
            pl.BlockSpec((None, None, 6, D_MODEL), lambda i: (l, _cond_index(i, tm), 0, 0)),
            full(1, S5_CH), full(S5_CH, S5_CH), full(NA_WIDTH, D_MODEL), full(GQA_Q_WIDTH, D_MODEL),
            full(S5_CH, D_MODEL), full(D_MODEL, D_MODEL), full(1, D_MODEL),
            full(D_MODEL, 128), full(D_MODEL, 128), full(1, 128),
        ],
        out_specs=(pl.BlockSpec((tm, D_MODEL), lambda i: (i, 0)),
                   pl.BlockSpec((tm, D_MODEL), lambda i: (i, 0)),
                   pl.BlockSpec((tm, 128), lambda i: (i, 0))),
        compiler_params=pltpu.CompilerParams(
            dimension_semantics=("parallel",), vmem_limit_bytes=VMEM_LIMIT),
        name="merge",
    )(x, oa, ob, z, yc, z, mod, d_s5, wglu, wa, wb, wc, wout, g2, wr_hi, wr_lo, br)


def _moe_kernel(h_ref, x_ref, comb_ref, mod_ref, wg_ref, wu_ref, wd_ref, fg_ref, x2_ref, y_ref, acc_sc):
    e = pl.program_id(1)
    h = h_ref[...]
    a = _dot(h, wg_ref[...])
    b = _dot(h, wu_ref[...])
    comb = comb_ref[...]
    lane = lax.broadcasted_iota(jnp.int32, comb.shape, 1)
    ce = jnp.sum(jnp.where(lane == e, comb, 0.0), axis=-1, keepdims=True)
    act = (a * jax.nn.sigmoid(a)) * b * ce
    contrib = _dot(act.astype(BF16), wd_ref[...])

    @pl.when(e == 0)
    def _():
        acc_sc[...] = contrib

    @pl.when(e > 0)
    def _():
        acc_sc[...] += contrib

    @pl.when(e == MOE_EXPERTS - 1)
    def _():
        x2 = x_ref[...] + mod_ref[5:6, :] * acc_sc[...]
        x2_ref[...] = x2
        y_ref[...] = _rms(x2, fg_ref[...])


def _moe(h2, x1, comb, mod, wg, wu, wd, fg, l):
    tm = 1024
    tok = pl.BlockSpec((tm, D_MODEL), lambda i, e: (i, 0))
    out = jax.ShapeDtypeStruct((N_TOK, D_MODEL), F32)
    return pl.pallas_call(
        _moe_kernel,
        out_shape=(out, out),
        grid=(N_TOK // tm, MOE_EXPERTS),
        in_specs=[
            tok, tok,
            pl.BlockSpec((tm, 128), lambda i, e: (i, 0)),
            pl.BlockSpec((None, None, 6, D_MODEL), lambda i, e: (l, _cond_index(i, tm), 0, 0)),
            pl.BlockSpec((None, D_MODEL, EXPERT_FF), lambda i, e: (e, 0, 0)),
            pl.BlockSpec((None, D_MODEL, EXPERT_FF), lambda i, e: (e, 0, 0)),
            pl.BlockSpec((None, EXPERT_FF, D_MODEL), lambda i, e: (e, 0, 0)),
            pl.BlockSpec((1, D_MODEL), lambda i, e: (0, 0)),
        ],
        out_specs=(tok, tok),
        scratch_shapes=[pltpu.VMEM((tm, D_MODEL), F32)],
        compiler_params=pltpu.CompilerParams(
            dimension_semantics=("parallel", "arbitrary"), vmem_limit_bytes=VMEM_LIMIT),
        name="moe",
    )(h2, x1, comb, mod, wg, wu, wd, fg)


def _pack_w_in(w):
    qkv = w[:, :COL_G_V + GQA_KV_WIDTH]
    u = w[:, COL_G_V + GQA_KV_WIDTH:COL_G_V + GQA_KV_WIDTH + S5_CH]
    gates = w[:, COL_G_V + GQA_KV_WIDTH + S5_CH:]
    pad = jnp.zeros((D_MODEL, COL_U - (COL_G_V + GQA_KV_WIDTH)), w.dtype)
    return jnp.concatenate([qkv, pad, u, gates], axis=1).astype(BF16)


def kernel(x_prompt, x_sample, cache_na_kv, cache_gqa_kv, state_ssm, c, c_ctx, norm_g, w_ada, b_ada, w_in, na_rpb, gqa_sink, s5_a_re, s5_a_im, s5_log_dt, s5_b_re, s5_b_im, s5_c_re, s5_c_im, s5_d, s5_w_glu, w_branch_a, w_branch_b, w_branch_c, w_out, moe_w_group, moe_b_group, moe_w_expert, moe_b_expert, moe_w_gate, moe_w_up, moe_w_down, final_g):
    cond = jnp.zeros((N_COND, D_MODEL), F32).at[0].set(c_ctx.astype(F32)).at[1:1 + DEC_BATCH].set(c.astype(F32))
    mod = _ada_mod(cond, w_ada.astype(F32), b_ada.astype(F32))

    x = jnp.concatenate([x_prompt.reshape(N_CTX_TOK, D_MODEL), x_sample.reshape(N_LAT_TOK, D_MODEL)], axis=0)
    cache_na = cache_na_kv.reshape(DEC_BATCH, DEPTH, 2, PAST_LEN, NA_WIDTH)
    cache_gqa = cache_gqa_kv.reshape(DEC_BATCH, DEPTH, 2, PAST_LEN, GQA_KV_WIDTH)
    cos_d, sin_d = _rope_tables()
    cos_q, sin_q = jnp.tile(cos_d, (1, GQA_Q_HEADS)), jnp.tile(sin_d, (1, GQA_Q_HEADS))
    cos_k, sin_k = jnp.tile(cos_d, (1, GQA_KV_HEADS)), jnp.tile(sin_d, (1, GQA_KV_HEADS))
    ctx_steps, lat_steps = SEQ // S5_T, DEC_SEQ // S5_T
    gn = S5_GROUPS * S5_STATE
    zeros_h0 = jnp.zeros((2, BATCH, gn), F32)
    fg = final_g.astype(F32).reshape(1, D_MODEL)

    na_list, gqa_list, ssm_list = [], [], []
    y = None
    for l in range(DEPTH):
        z = _inproj(x, norm_g[l, 0].astype(F32).reshape(1, D_MODEL), mod, _pack_w_in(w_in[l]), l)

        zc = z[:N_CTX_TOK]
        na_list.append(jnp.stack([zc[:, COL_NA_K:COL_NA_K + NA_WIDTH].reshape(BATCH, SEQ, NA_HEADS, HEAD_DIM),
                                  zc[:, COL_NA_V:COL_NA_V + NA_WIDTH].reshape(BATCH, SEQ, NA_HEADS, HEAD_DIM)],
                                 axis=1))
        gqa_list.append(jnp.stack(
            [zc[:, COL_G_K:COL_G_K + GQA_KV_WIDTH].reshape(BATCH, SEQ, GQA_KV_HEADS, HEAD_DIM),
             zc[:, COL_G_V:COL_G_V + GQA_KV_WIDTH].reshape(BATCH, SEQ, GQA_KV_HEADS, HEAD_DIM)], axis=1))

        sink = gqa_sink[l].astype(F32)
        oa_c, ob_c = _ctx_attn(z, sink)
        oa_l = _na_lat(z, cache_na, _na_bias_tables(na_rpb[l]), l)
        ob_l = _swa_lat(z, cache_gqa, sink, cos_q, sin_q, cos_k, sin_k, l)

        ops = _s5_operators(s5_a_re[l], s5_a_im[l], s5_log_dt[l], s5_b_re[l], s5_b_im[l], s5_c_re[l], s5_c_im[l])
        ops = {k: (v.astype(BF16) if k in ("bs", "m", "cre", "cim") else v) for k, v in ops.items()}
        u = z[:, COL_U:COL_U + S5_CH].astype(BF16)
        yc_c, fre, fim = _s5_scan(_to_superstep(u[:N_CTX_TOK], BATCH, ctx_steps, BATCH), ops,
                                  zeros_h0, zeros_h0, BATCH, ctx_steps)
        st = state_ssm[:, l].astype(F32).reshape(DEC_BATCH, 2, 2, gn)
        h0 = jnp.pad(st.transpose(2, 1, 0, 3), ((0, 0), (0, 0), (0, S5_LAT_B - DEC_BATCH), (0, 0)))
        yc_l, _, _ = _s5_scan(_to_superstep(u[N_CTX_TOK:], DEC_BATCH, lat_steps, S5_LAT_B), ops,
                              h0[0], h0[1], S5_LAT_B, lat_steps)
        yc = jnp.concatenate([_from_superstep(yc_c, BATCH, ctx_steps, BATCH),
                              _from_superstep(yc_l, DEC_BATCH, lat_steps, S5_LAT_B)], axis=0)
        ssm_list.append(jnp.stack([fre, fim], axis=2).transpose(1, 0, 2, 3)
                        .reshape(BATCH, 2, 2, S5_GROUPS, S5_STATE).astype(x_prompt.dtype))

        oa = jnp.concatenate([oa_c, oa_l], axis=0)
        ob = jnp.concatenate([ob_c, ob_l], axis=0)
        wr = jnp.zeros((D_MODEL, 128), F32)
        wr = wr.at[:, :MOE_EXPERTS].set(moe_w_expert[l].astype(F32))
        wr = wr.at[:, MOE_EXPERTS:MOE_EXPERTS + MOE_GROUPS].set(moe_w_group[l].astype(F32))
        br = jnp.zeros((1, 128), F32)
        br = br.at[0, :MOE_EXPERTS].set(moe_b_expert[l].astype(F32))
        br = br.at[0, MOE_EXPERTS:MOE_EXPERTS + MOE_GROUPS].set(moe_b_group[l].astype(F32))
        wr_hi, wr_lo = _split_bf16(wr)
        x1, h2, comb = _merge(
            x, oa, ob, z, yc, mod, s5_d[l].astype(F32).reshape(1, S5_CH), s5_w_glu[l].astype(BF16),
            w_branch_a[l].astype(BF16), w_branch_b[l].astype(BF16), w_branch_c[l].astype(BF16),
            w_out[l].astype(BF16), norm_g[l, 1].astype(F32).reshape(1, D_MODEL), wr_hi, wr_lo, br, l)
        x, y = _moe(h2, x1, comb, mod, moe_w_gate[l].astype(BF16), moe_w_up[l].astype(BF16),
                    moe_w_down[l].astype(BF16), fg, l)

    y_prompt = y[:N_CTX_TOK].reshape(BATCH, SEQ, D_MODEL)
    y_sample = y[N_CTX_TOK:].reshape(DEC_BATCH, DEC_SEQ, D_MODEL)
    return (y_prompt, y_sample, jnp.stack(na_list, axis=1), jnp.stack(gqa_list, axis=1),
            jnp.stack(ssm_list, axis=1))
```

```python
import functools
import math

import numpy as np
import jax
import jax.numpy as jnp
from jax import lax
from jax.experimental import pallas as pl
from jax.experimental.pallas import tpu as pltpu

F32 = jnp.float32
BF16 = jnp.bfloat16

D_MODEL = 1024
BATCH = 32
SEQ = 256
DEPTH = 2
DEC_BATCH = 4
DEC_SEQ = 2048
PAST_LEN = 512
GRID_W = 64
GRID_ROWS = DEC_SEQ // GRID_W
HEAD_DIM = 64
NA_HEADS = 8
NA_WIN_H = 8
NA_WIN_W = 16
GQA_Q_HEADS = 8
GQA_KV_HEADS = 2
GQA_GROUP = GQA_Q_HEADS // GQA_KV_HEADS
SWA_WINDOW = 128
SWA_BLOCK = 128
ROPE_THETA = 10000.0
S5_CH = 512
S5_GROUP_CH = 16
S5_GROUPS = S5_CH // S5_GROUP_CH
S5_STATE = 64
N_BRANCH = 3
NA_WIDTH = NA_HEADS * HEAD_DIM
GQA_Q_WIDTH = GQA_Q_HEADS * HEAD_DIM
GQA_KV_WIDTH = GQA_KV_HEADS * HEAD_DIM
MOE_GROUPS = 4
MOE_EXPERTS_PER_GROUP = 4
MOE_EXPERTS = MOE_GROUPS * MOE_EXPERTS_PER_GROUP
EXPERT_FF = 512
EPS = 1e-6
NEG_INF = -1e30

N_CTX_TOK = BATCH * SEQ
N_LAT_TOK = DEC_BATCH * DEC_SEQ
N_TOK = N_CTX_TOK + N_LAT_TOK
N_COND = 8

COL_NA_Q = 0
COL_NA_K = 512
COL_NA_V = 1024
COL_G_Q = 1536
COL_G_K = 2048
COL_G_V = 2176
COL_U = 2560
COL_GATES = 3072
Z_COLS = 6144

S5_T = 16
S5_PAIRS = S5_GROUPS // 2
S5_OCT_PAIRS = 4
S5_BLOCK_TOK = 2048
S5_ROWS = S5_BLOCK_TOK // S5_T
S5_NPOW = 7

NA_QROWS = 4
NA_KROWS = 12

VMEM_LIMIT = 56 * 1024 * 1024


def _dot(a, b):
    return jnp.dot(a, b, preferred_element_type=F32)


def _dot_nt(a, b):
    return lax.dot_general(a, b, (((1,), (1,)), ((), ())), preferred_element_type=F32)


def _split_bf16(x):
    hi = x.astype(BF16)
    lo = (x - hi.astype(F32)).astype(BF16)
    return hi, lo


def _rms(x, g):
    return x * lax.rsqrt(jnp.mean(x * x, axis=-1, keepdims=True) + EPS) * g


def _cond_index(i, tm):
    nctx = N_CTX_TOK // tm
    return jnp.where(i < nctx, 0, 1 + ((i - nctx) * tm) // DEC_SEQ)


def _ada_kernel(c_ref, w_ref, b_ref, o_ref):
    c = c_ref[...]
    s = c * jax.nn.sigmoid(c)
    s_hi, s_lo = _split_bf16(s)
    w_hi, w_lo = _split_bf16(w_ref[...])
    o_ref[...] = _dot(s_hi, w_hi) + _dot(s_lo, w_hi) + _dot(s_hi, w_lo) + b_ref[...]


def _ada_mod(cond, w_ada, b_ada):
    tn = 1536
    n = 6 * D_MODEL
    out = pl.pallas_call(
        _ada_kernel,
        out_shape=jax.ShapeDtypeStruct((DEPTH, N_COND, n), F32),
        grid=(DEPTH, n // tn),
        in_specs=[
            pl.BlockSpec((N_COND, D_MODEL), lambda l, j: (0, 0)),
            pl.BlockSpec((None, D_MODEL, tn), lambda l, j: (l, 0, j)),
            pl.BlockSpec((None, 1, tn), lambda l, j: (l, 0, j)),
        ],
        out_specs=pl.BlockSpec((None, N_COND, tn), lambda l, j: (l, 0, j)),
        compiler_params=pltpu.CompilerParams(
            dimension_semantics=("parallel", "parallel"), vmem_limit_bytes=VMEM_LIMIT),
        name="ada_mod",
    )(cond, w_ada, b_ada.reshape(DEPTH, 1, n))
    return out.reshape(DEPTH, N_COND, 6, D_MODEL)


def _inproj_kernel(x_ref, g_ref, mod_ref, w_ref, o_ref, h_sc):
    @pl.when(pl.program_id(1) == 0)
    def _():
        h = _rms(x_ref[...], g_ref[...]) * (1.0 + mod_ref[1:2, :]) + mod_ref[0:1, :]
        h_sc[...] = h.astype(BF16)

    o_ref[...] = _dot(h_sc[...], w_ref[...])


def _inproj(x, g, mod, w, l):
    tm, tn = 1024, 1536
    return pl.pallas_call(
        _inproj_kernel,
        out_shape=jax.ShapeDtypeStruct((N_TOK, Z_COLS), F32),
        grid=(N_TOK // tm, Z_COLS // tn),
        in_specs=[
            pl.BlockSpec((tm, D_MODEL), lambda i, j: (i, 0)),
            pl.BlockSpec((1, D_MODEL), lambda i, j: (0, 0)),
            pl.BlockSpec((None, None, 6, D_MODEL), lambda i, j: (l, _cond_index(i, tm), 0, 0)),
            pl.BlockSpec((D_MODEL, tn), lambda i, j: (0, j)),
        ],
        out_specs=pl.BlockSpec((tm, tn), lambda i, j: (i, j)),
        scratch_shapes=[pltpu.VMEM((tm, D_MODEL), BF16)],
        compiler_params=pltpu.CompilerParams(
            dimension_semantics=("parallel", "arbitrary"), vmem_limit_bytes=VMEM_LIMIT),
        name="inproj",
    )(x, g, mod, w)


def _ctx_attn_kernel(sink_ref, na_ref, gq_ref, gkv_ref, oa_ref, ob_ref):
    scale = HEAD_DIM ** -0.5
    na = na_ref[...].astype(BF16)
    gq = gq_ref[...].astype(BF16)
    gkv = gkv_ref[...].astype(BF16)
    for h in range(NA_HEADS):
        q = na[:, COL_NA_Q + h * HEAD_DIM:COL_NA_Q + (h + 1) * HEAD_DIM]
        k = na[:, COL_NA_K + h * HEAD_DIM:COL_NA_K + (h + 1) * HEAD_DIM]
        v = na[:, COL_NA_V + h * HEAD_DIM:COL_NA_V + (h + 1) * HEAD_DIM]
        s = _dot_nt(q, k) * scale
        m = jnp.max(s, axis=-1, keepdims=True)
        p = jnp.exp(s - m)
        d = jnp.sum(p, axis=-1, keepdims=True)
        o = _dot(p.astype(BF16), v) / d
        oa_ref[:, h * HEAD_DIM:(h + 1) * HEAD_DIM] = o.astype(BF16)
    for h in range(GQA_Q_HEADS):
        kh = h // GQA_GROUP
        q = gq[:, h * HEAD_DIM:(h + 1) * HEAD_DIM]
        k = gkv[:, kh * HEAD_DIM:(kh + 1) * HEAD_DIM]
        v = gkv[:, GQA_KV_WIDTH + kh * HEAD_DIM:GQA_KV_WIDTH + (kh + 1) * HEAD_DIM]
        sink = sink_ref[h]
        s = _dot_nt(q, k) * scale
        m = jnp.maximum(jnp.max(s, axis=-1, keepdims=True), sink)
        p = jnp.exp(s - m)
        d = jnp.sum(p, axis=-1, keepdims=True) + jnp.exp(sink - m)
        o = _dot(p.astype(BF16), v) / d
        ob_ref[:, h * HEAD_DIM:(h + 1) * HEAD_DIM] = o.astype(BF16)


def _ctx_attn(z, sink):
    out = jax.ShapeDtypeStruct((N_CTX_TOK, NA_WIDTH), BF16)
    return pl.pallas_call(
        _ctx_attn_kernel,
        out_shape=(out, out),
        grid=(BATCH,),
        in_specs=[
            pl.BlockSpec(memory_space=pltpu.SMEM),
            pl.BlockSpec((SEQ, 3 * NA_WIDTH), lambda b: (b, 0)),
            pl.BlockSpec((SEQ, GQA_Q_WIDTH), lambda b: (b, COL_G_Q // GQA_Q_WIDTH)),
            pl.BlockSpec((SEQ, 2 * GQA_KV_WIDTH), lambda b: (b, COL_G_K // (2 * GQA_KV_WIDTH))),
        ],
        out_specs=(pl.BlockSpec((SEQ, NA_WIDTH), lambda b: (b, 0)),
                   pl.BlockSpec((SEQ, GQA_Q_WIDTH), lambda b: (b, 0))),
        compiler_params=pltpu.CompilerParams(
            dimension_semantics=("parallel",), vmem_limit_bytes=VMEM_LIMIT),
        name="ctx_attn",
    )(sink, z, z, z)


def _na_bias_tables(rpb):
    kh = min(NA_WIN_H, GRID_ROWS)
    n_dc = 2 * NA_WIN_W - 1
    qc = np.arange(GRID_W)
    kc = np.arange(GRID_W)
    cstart = np.clip(qc - NA_WIN_W // 2, 0, GRID_W - NA_WIN_W)
    col_ok = (kc[None, :] >= cstart[:, None]) & (kc[None, :] < cstart[:, None] + NA_WIN_W)
    dc = np.clip(kc[None, :] - qc[:, None] + (NA_WIN_W - 1), 0, n_dc - 1)
    onehot = ((np.arange(n_dc)[:, None, None] == dc[None]) & col_ok[None]).astype(np.float32)
    band = jnp.einsum("hrd,dqk->hrqk", rpb.astype(F32), onehot, precision=lax.Precision.HIGHEST)
    band = jnp.where(col_ok, band, NEG_INF)
    band = jnp.pad(band, ((0, 0), (NA_KROWS, NA_KROWS), (0, 0), (0, 0)))
    pats = []
    for r0, ks in ((0, 0), (NA_QROWS, 0), (GRID_ROWS - NA_QROWS, GRID_ROWS - NA_KROWS)):
        rows = []
        for qr in range(NA_QROWS):
            r = r0 + qr
            st = min(max(r - kh // 2, 0), GRID_ROWS - kh)
            krow = ks + np.arange(NA_KROWS)
            row_ok = (krow >= st) & (krow < st + kh)
            lo = ks - r + (NA_WIN_H - 1) + NA_KROWS
            rows.append(jnp.where(row_ok[None, :, None, None], band[:, lo:lo + NA_KROWS], NEG_INF))
        pats.append(jnp.stack(rows, axis=1))
    bias = jnp.stack(pats).transpose(0, 1, 2, 4, 3, 5)
    return bias.reshape(3, NA_HEADS, NA_QROWS * GRID_W, NA_KROWS * GRID_W)


def _na_lat_kernel(q_ref, k_ref, v_ref, ck_ref, cv_ref, bias_ref, o_ref):
    i = pl.program_id(1)
    scale = HEAD_DIM ** -0.5
    nk = NA_KROWS * GRID_W
    ks = jnp.clip(i * NA_QROWS - NA_WIN_H // 2, 0, GRID_ROWS - NA_KROWS)
    start = pl.multiple_of(ks * GRID_W, GRID_W)
    q = q_ref[...].astype(BF16)
    kw = k_ref[pl.ds(start, nk), :].astype(BF16)
    vw = v_ref[pl.ds(start, nk), :].astype(BF16)
    ck = ck_ref[...].astype(BF16)
    cv = cv_ref[...].astype(BF16)
    for h in range(NA_HEADS):
        sl = slice(h * HEAD_DIM, (h + 1) * HEAD_DIM)
        qh = q[:, sl]
        s_loc = _dot_nt(qh, kw[:, sl]) * scale + bias_ref[h]
        s_ctx = _dot_nt(qh, ck[:, sl]) * scale
        m = jnp.maximum(jnp.max(s_loc, axis=-1, keepdims=True), jnp.max(s_ctx, axis=-1, keepdims=True))
        p_loc = jnp.exp(s_loc - m)
        p_ctx = jnp.exp(s_ctx - m)
        d = jnp.sum(p_loc, axis=-1, keepdims=True) + jnp.sum(p_ctx, axis=-1, keepdims=True)
        o = (_dot(p_loc.astype(BF16), vw[:, sl]) + _dot(p_ctx.astype(BF16), cv[:, sl])) / d
        o_ref[:, sl] = o.astype(BF16)


def _na_lat(z, cache_na, bias, l):
    tq = NA_QROWS * GRID_W
    nsteps = GRID_ROWS // NA_QROWS
    lat_blk = N_CTX_TOK // DEC_SEQ

    def pat(i):
        return jnp.where(i == 0, 0, jnp.where(i == nsteps - 1, 2, 1))

    return pl.pallas_call(
        _na_lat_kernel,
        out_shape=jax.ShapeDtypeStruct((N_LAT_TOK, NA_WIDTH), BF16),
        grid=(DEC_BATCH, nsteps),
        in_specs=[
            pl.BlockSpec((tq, NA_WIDTH), lambda b, i: (N_CTX_TOK // tq + b * nsteps + i, COL_NA_Q // NA_WIDTH)),
            pl.BlockSpec((DEC_SEQ, NA_WIDTH), lambda b, i: (lat_blk + b, COL_NA_K // NA_WIDTH)),
            pl.BlockSpec((DEC_SEQ, NA_WIDTH), lambda b, i: (lat_blk + b, COL_NA_V // NA_WIDTH)),
            pl.BlockSpec((None, None, None, PAST_LEN, NA_WIDTH), lambda b, i: (b, l, 0, 0, 0)),
            pl.BlockSpec((None, None, None, PAST_LEN, NA_WIDTH), lambda b, i: (b, l, 1, 0, 0)),
            pl.BlockSpec((None, NA_HEADS, tq, NA_KROWS * GRID_W), lambda b, i: (pat(i), 0, 0, 0)),
        ],
        out_specs=pl.BlockSpec((tq, NA_WIDTH), lambda b, i: (b * nsteps + i, 0)),
        compiler_params=pltpu.CompilerParams(
            dimension_semantics=("parallel", "arbitrary"), vmem_limit_bytes=VMEM_LIMIT),
        name="na_lat",
    )(z, z, z, cache_na, cache_na, bias)


def _rope_tables():
    nf = HEAD_DIM // 4
    t = jnp.arange(DEC_SEQ)
    pos = jnp.stack([t // GRID_W, t % GRID_W], axis=-1).astype(F32)
    inv = ROPE_THETA ** (-jnp.arange(nf, dtype=F32) / nf)
    ang = pos[:, :, None] * inv
    cos = jnp.cos(ang)
    sin = jnp.sin(ang)
    cos_d = jnp.stack([cos, cos], axis=2).reshape(DEC_SEQ, HEAD_DIM)
    sin_d = jnp.stack([-sin, sin], axis=2).reshape(DEC_SEQ, HEAD_DIM)
    return cos_d, sin_d


def _rope(x, cos, sin_signed):
    n = x.shape[-1]
    nf = HEAD_DIM // 4
    lane = lax.broadcasted_iota(jnp.int32, x.shape, 1)
    first_half = (lane // nf) % 2 == 0
    partner = jnp.where(first_half, pltpu.roll(x, n - nf, 1), pltpu.roll(x, nf, 1))
    return x * cos + partner * sin_signed


def _swa_lat_kernel(sink_ref, q_ref, kv_ref, cq_ref, sq_ref, ckt_ref, skt_ref, ck_ref, cv_ref, o_ref, k_sc, v_sc):
    n = pl.program_id(1)
    scale = HEAD_DIM ** -0.5
    nwin = 3 * SWA_BLOCK

    @pl.when(n == 0)
    def _():
        kv = kv_ref[...]
        k_sc[...] = _rope(kv[:, :GQA_KV_WIDTH], ckt_ref[...], skt_ref[...]).astype(BF16)
        v_sc[...] = kv[:, GQA_KV_WIDTH:].astype(BF16)

    q = _rope(q_ref[...], cq_ref[...], sq_ref[...]).astype(BF16)
    start = pl.multiple_of(jnp.clip((n - 1) * SWA_BLOCK, 0, DEC_SEQ - nwin), SWA_BLOCK)
    kw = k_sc[pl.ds(start, nwin), :]
    vw = v_sc[pl.ds(start, nwin), :]
    ck = ck_ref[...].astype(BF16)
    cv = cv_ref[...].astype(BF16)
    rows = GQA_GROUP * SWA_BLOCK
    row = lax.broadcasted_iota(jnp.int32, (rows, nwin), 0)
    col = lax.broadcasted_iota(jnp.int32, (rows, nwin), 1)
    qpos = n * SWA_BLOCK + row % SWA_BLOCK
    kpos = start + col
    ok = jnp.abs(qpos - kpos) <= SWA_WINDOW
    grp = lax.broadcasted_iota(jnp.int32, (rows, 1), 0) // SWA_BLOCK
    for kh in range(GQA_KV_HEADS):
        sl = slice(kh * HEAD_DIM, (kh + 1) * HEAD_DIM)
        q4 = jnp.concatenate(
            [q[:, (kh * GQA_GROUP + g) * HEAD_DIM:(kh * GQA_GROUP + g + 1) * HEAD_DIM] for g in range(GQA_GROUP)],
            axis=0)
        sink = jnp.zeros((rows, 1), F32)
        for g in range(GQA_GROUP):
            sink = jnp.where(grp == g, sink_ref[kh * GQA_GROUP + g], sink)
        s_loc = jnp.where(ok, _dot_nt(q4, kw[:, sl]) * scale, NEG_INF)
        s_ctx = _dot_nt(q4, ck[:, sl]) * scale
        m = jnp.maximum(jnp.maximum(jnp.max(s_loc, axis=-1, keepdims=True),
                                    jnp.max(s_ctx, axis=-1, keepdims=True)), sink)
        p_loc = jnp.exp(s_loc - m)
        p_ctx = jnp.exp(s_ctx - m)
        d = (jnp.sum(p_loc, axis=-1, keepdims=True) + jnp.sum(p_ctx, axis=-1, keepdims=True)
             + jnp.exp(sink - m))
        o4 = (_dot(p_loc.astype(BF16), vw[:, sl]) + _dot(p_ctx.astype(BF16), cv[:, sl])) / d
        for g in range(GQA_GROUP):
            h = kh * GQA_GROUP + g
            o_ref[:, h * HEAD_DIM:(h + 1) * HEAD_DIM] = o4[g * SWA_BLOCK:(g + 1) * SWA_BLOCK].astype(BF16)


def _swa_lat(z, cache_gqa, sink, cos_q, sin_q, cos_k, sin_k, l):
    nb = DEC_SEQ // SWA_BLOCK
    lat_blk = N_CTX_TOK // DEC_SEQ
    return pl.pallas_call(
        _swa_lat_kernel,
        out_shape=jax.ShapeDtypeStruct((N_LAT_TOK, GQA_Q_WIDTH), BF16),
        grid=(DEC_BATCH, nb),
        in_specs=[
            pl.BlockSpec(memory_space=pltpu.SMEM),
            pl.BlockSpec((SWA_BLOCK, GQA_Q_WIDTH),
                         lambda b, n: (N_CTX_TOK // SWA_BLOCK + b * nb + n, COL_G_Q // GQA_Q_WIDTH)),
            pl.BlockSpec((DEC_SEQ, 2 * GQA_KV_WIDTH), lambda b, n: (lat_blk + b, COL_G_K // (2 * GQA_KV_WIDTH))),
            pl.BlockSpec((SWA_BLOCK, GQA_Q_WIDTH), lambda b, n: (n, 0)),
            pl.BlockSpec((SWA_BLOCK, GQA_Q_WIDTH), lambda b, n: (n, 0)),
            pl.BlockSpec((DEC_SEQ, GQA_KV_WIDTH), lambda b, n: (0, 0)),
            pl.BlockSpec((DEC_SEQ, GQA_KV_WIDTH), lambda b, n: (0, 0)),
            pl.BlockSpec((None, None, None, PAST_LEN, GQA_KV_WIDTH), lambda b, n: (b, l, 0, 0, 0)),
            pl.BlockSpec((None, None, None, PAST_LEN, GQA_KV_WIDTH), lambda b, n: (b, l, 1, 0, 0)),
        ],
        out_specs=pl.BlockSpec((SWA_BLOCK, GQA_Q_WIDTH), lambda b, n: (b * nb + n, 0)),
        scratch_shapes=[pltpu.VMEM((DEC_SEQ, GQA_KV_WIDTH), BF16), pltpu.VMEM((DEC_SEQ, GQA_KV_WIDTH), BF16)],
        compiler_params=pltpu.CompilerParams(
            dimension_semantics=("parallel", "arbitrary"), vmem_limit_bytes=VMEM_LIMIT),
        name="swa_lat",
    )(sink, z, z, cos_q, sin_q, cos_k, sin_k, cache_gqa, cache_gqa)


def _s5_operators(a_re, a_im, log_dt, b_re, b_im, c_re, c_im):
    T, G, N, C, P = S5_T, S5_GROUPS, S5_STATE, S5_GROUP_CH, S5_PAIRS
    tau = jnp.arange(T + 1, dtype=F32)
    eye2 = jnp.eye(2, dtype=F32)
    out = {k: [] for k in ("bs", "m", "cre", "cim", "apr", "api")}
    jj, ss = np.meshgrid(np.arange(T), np.arange(T), indexing="ij")

    def pairs(x):
        return x.reshape((P, 2) + x.shape[1:])

    def diag(x, axis):
        e = eye2.reshape((1, 2) + (1,) * (axis - 1) + (2,) + (1,) * (x.ndim - axis - 1))
        return jnp.expand_dims(x, axis + 1) * e

    for d in range(2):
        A = lax.complex(a_re[d].astype(F32), a_im[d].astype(F32))
        dt = jnp.exp(log_dt[d].astype(F32))[:, None]
        a_bar = jnp.exp(A * dt)
        pw = jnp.exp((A * dt)[None] * tau[:, None, None])
        b_bar = ((a_bar - 1.0) / A)[..., None] * lax.complex(b_re[d].astype(F32), b_im[d].astype(F32))
        c_mat = lax.complex(c_re[d].astype(F32), c_im[d].astype(F32))
        kern = jnp.einsum("gon,tgn,gni->gtoi", c_mat, pw[:T], b_bar, precision=lax.Precision.HIGHEST).real
        if d == 0:
            lag, ok = jj - ss, jj >= ss
            p_in = pw[:T][::-1]
            p_out = pw[1:T + 1]
        else:
            lag, ok = ss - jj, ss >= jj
            p_in = pw[:T]
            p_out = pw[1:T + 1][::-1]
        sel = ((np.arange(T)[None, None, :] == lag[:, :, None]) & ok[:, :, None]).astype(np.float32)
        m = jnp.einsum("jst,gtoi->gsijo", sel, kern, precision=lax.Precision.HIGHEST)
        bs = pairs((p_in[:, :, :, None] * b_bar[None]).transpose(1, 0, 3, 2))
        co = pairs((c_mat[None] * p_out[:, :, None, :]).transpose(1, 3, 0, 2))
        bs = jnp.stack([diag(bs.real, 3), diag(bs.imag, 3)], axis=4)
        out["bs"].append(bs.transpose(0, 2, 1, 3, 4, 5, 6).reshape(P, 2 * T * C, 4 * N))
        m = diag(pairs(m), 4)
        out["m"].append(m.transpose(0, 2, 1, 3, 4, 5, 6).reshape(P, 2 * T * C, 2 * T * C))
        out["cre"].append(diag(co.real, 3).reshape(P, 2 * N, 2 * T * C))
        out["cim"].append(diag(-co.imag, 3).reshape(P, 2 * N, 2 * T * C))
        pw2 = jnp.exp((A * dt)[None] * (T * 2.0 ** jnp.arange(S5_NPOW, dtype=F32))[:, None, None])
        out["apr"].append(pw2.real.reshape(S5_NPOW, G * N))
        out["api"].append(pw2.imag.reshape(S5_NPOW, G * N))
    return {k: jnp.stack(v) for k, v in out.items()}


def _s5_kernel(u_ref, bs_ref, m_ref, cre_ref, cim_ref, apr_ref, api_ref, *rest, kseq, has_h0):
    if has_h0:
        h0_ref, _, y_ref = rest
    else:
        y_ref, fin_ref, fin_sc = rest
    t = pl.program_id(1)
    rows, half, pc = S5_ROWS, 2 * S5_STATE, 2 * S5_GROUP_CH
    k = lax.broadcasted_iota(jnp.int32, (rows, half), 0) % kseq
    shifts = [1 << i for i in range(kseq.bit_length() - 1)]
    xs = [u_ref[pl.ds(s, rows, stride=S5_T), :].astype(BF16) for s in range(S5_T)]
    ys = []
    for p in range(S5_OCT_PAIRS):
        lanes = slice(p * half, (p + 1) * half)
        xp = jnp.concatenate([x[:, p * pc:(p + 1) * pc] for x in xs], axis=1)
        acc = None
        for d in range(2):
            inc = _dot(xp, bs_ref[d, p])
            sr, si = inc[:, :half], inc[:, half:]
            if has_h0:
                h0r = h0_ref[d, 0, t, :, lanes]
                h0i = h0_ref[d, 1, t, :, lanes]
                ar, ai = apr_ref[d, 0:1, lanes], api_ref[d, 0:1, lanes]
                first = (k == 0) if d == 0 else (k == kseq - 1)
                sr = sr + jnp.where(first, ar * h0r - ai * h0i, 0.0)
                si = si + jnp.where(first, ar * h0i + ai * h0r, 0.0)
            else:
                h0r = h0i = 0.0
            for i, sh in enumerate(shifts):
                ar, ai = apr_ref[d, i:i + 1, lanes], api_ref[d, i:i + 1, lanes]
                ok = (k >= sh) if d == 0 else (k < kseq - sh)
                amt = sh if d == 0 else rows - sh
                rr = jnp.where(ok, pltpu.roll(sr, amt, 0), 0.0)
                ri = jnp.where(ok, pltpu.roll(si, amt, 0), 0.0)
                sr, si = sr + ar * rr - ai * ri, si + ar * ri + ai * rr
            inner = (k >= 1) if d == 0 else (k < kseq - 1)
            amt = 1 if d == 0 else rows - 1
            hr = jnp.where(inner, pltpu.roll(sr, amt, 0), h0r)
            hi = jnp.where(inner, pltpu.roll(si, amt, 0), h0i)
            yd = (_dot(xp, m_ref[d, p]) + _dot(hr.astype(BF16), cre_ref[d, p])
                  + _dot(hi.astype(BF16), cim_ref[d, p]))
            acc = yd if acc is None else acc + yd
            if not has_h0:
                last = kseq - 1 if d == 0 else 0
                fin_sc[0] = sr
                fin_sc[1] = si
                fin_ref[d, 0, :, lanes] = fin_sc[0, pl.ds(last, rows // kseq, stride=kseq), :]
                fin_ref[d, 1, :, lanes] = fin_sc[1, pl.ds(last, rows // kseq, stride=kseq), :]
        ys.append(acc)
    for j in range(S5_T):
        y_ref[pl.ds(j, rows, stride=S5_T), :] = jnp.concatenate([y[:, j * pc:(j + 1) * pc] for y in ys], axis=1)


def _s5_scan(z, ops, h0, yc_prev, tok0, ntok, kseq):
    nblk, tb0 = ntok // S5_BLOCK_TOK, tok0 // S5_BLOCK_TOK
    noct = S5_PAIRS // S5_OCT_PAIRS
    half, gn = 2 * S5_STATE, S5_GROUPS * S5_STATE
    pw = 2 * S5_T * S5_GROUP_CH
    wspec = lambda r, c: pl.BlockSpec((2, S5_OCT_PAIRS, r, c), lambda q, t: (0, q, 0, 0))
    aspec = pl.BlockSpec((2, S5_NPOW, S5_OCT_PAIRS * half), lambda q, t: (0, 0, q))
    in_specs = [
        pl.BlockSpec((S5_BLOCK_TOK, 128), lambda q, t: (tb0 + t, COL_U // 128 + q)),
        wspec(pw, 2 * half), wspec(pw, pw), wspec(half, pw), wspec(half, pw), aspec, aspec,
    ]
    args = [z, ops["bs"], ops["m"], ops["cre"], ops["cim"], ops["apr"], ops["api"]]
    y_shape = jax.ShapeDtypeStruct((N_TOK, S5_CH), F32)
    y_spec = pl.BlockSpec((S5_BLOCK_TOK, 128), lambda q, t: (tb0 + t, q))
    params = pltpu.CompilerParams(dimension_semantics=("parallel", "arbitrary"), vmem_limit_bytes=VMEM_LIMIT)
    kern = functools.partial(_s5_kernel, kseq=kseq, has_h0=h0 is not None)
    if h0 is None:
        nseq = S5_ROWS // kseq
        return pl.pallas_call(
            kern, grid=(noct, nblk), in_specs=in_specs,
            out_shape=(y_shape, jax.ShapeDtypeStruct((2, 2, nblk * nseq, gn), F32)),
            out_specs=(y_spec, pl.BlockSpec((2, 2, nseq, S5_OCT_PAIRS * half), lambda q, t: (0, 0, t, q))),
            scratch_shapes=[pltpu.VMEM((2, S5_ROWS, half), F32)],
            compiler_params=params, name="s5_ctx",
        )(*args)
    nseq = h0.shape[2]
    in_specs += [pl.BlockSpec((2, 2, nseq, 1, S5_OCT_PAIRS * half), lambda q, t: (0, 0, 0, 0, q)),
                 pl.BlockSpec(memory_space=pl.ANY)]
    return pl.pallas_call(
        kern, grid=(noct, nblk), in_specs=in_specs, out_shape=y_shape, out_specs=y_spec,
        input_output_aliases={len(args) + 1: 0},
        compiler_params=params, name="s5_lat",
    )(*args, h0.reshape(2, 2, nseq, 1, gn), yc_prev)


def _route(logits):
    lane = lax.broadcasted_iota(jnp.int32, logits.shape, 1)
    big = jnp.int32(1 << 20)
    is_g = (lane >= MOE_EXPERTS) & (lane < MOE_EXPERTS + MOE_GROUPS)
    lg = jnp.where(is_g, logits, NEG_INF)
    gmax = jnp.max(lg, axis=-1, keepdims=True)
    gsel = jnp.min(jnp.where(is_g & (lg == gmax), lane, big), axis=-1, keepdims=True) - MOE_EXPERTS
    p_group = 1.0 / jnp.sum(jnp.where(is_g, jnp.exp(lg - gmax), 0.0), axis=-1, keepdims=True)
    in_grp = (lane < MOE_EXPERTS) & (lane // MOE_EXPERTS_PER_GROUP == gsel)
    le = jnp.where(in_grp, logits, NEG_INF)
    v1 = jnp.max(le, axis=-1, keepdims=True)
    i1 = jnp.min(jnp.where(in_grp & (le == v1), lane, big), axis=-1, keepdims=True)
    rest = in_grp & (lane != i1)
    le2 = jnp.where(rest, logits, NEG_INF)
    v2 = jnp.max(le2, axis=-1, keepdims=True)
    i2 = jnp.min(jnp.where(rest & (le2 == v2), lane, big), axis=-1, keepdims=True)
    e2 = jnp.exp(v2 - v1)
    w1 = 1.0 / (1.0 + e2)
    w2 = e2 / (1.0 + e2)
    return jnp.where(lane == i1, w1 * p_group, 0.0) + jnp.where(lane == i2, w2 * p_group, 0.0)


def _merge_kernel(x_ref, oa_ref, ob_ref, u_ref, yc_ref, gates_ref, mod_ref, d_ref, wglu_ref, wa_ref, wb_ref,
                  wc_ref, wout_ref, g2_ref, wrh_ref, wrl_ref, br_ref, x1_ref, h2_ref, comb_ref):
    y = u_ref[...] * d_ref[...] + yc_ref[...]
    y = y * (0.5 * (1.0 + jnp.tanh(math.sqrt(2.0 / math.pi) * (y + 0.044715 * (y * y * y)))))
    oc = y * jax.nn.sigmoid(_dot(y.astype(BF16), wglu_ref[...]))
    gate = jax.nn.sigmoid(gates_ref[...])
    merged = (gate[:, :D_MODEL] * _dot(oa_ref[...], wa_ref[...])
              + gate[:, D_MODEL:2 * D_MODEL] * _dot(ob_ref[...], wb_ref[...])
              + gate[:, 2 * D_MODEL:] * _dot(oc.astype(BF16), wc_ref[...]))
    x1 = x_ref[...] + mod_ref[2:3, :] * _dot(merged.astype(BF16), wout_ref[...])
    x1_ref[...] = x1
    h2 = _rms(x1, g2_ref[...]) * (1.0 + mod_ref[4:5, :]) + mod_ref[3:4, :]
    h2_ref[...] = h2.astype(BF16)
    h_hi, h_lo = _split_bf16(h2)
    logits = _dot(h_hi, wrh_ref[...]) + _dot(h_lo, wrh_ref[...]) + _dot(h_hi, wrl_ref[...]) + br_ref[...]
    comb_ref[...] = _route(logits)


def _merge(x, oa, ob, z, yc, mod, d_s5, wglu, wa, wb, wc, wout, g2, wr_hi, wr_lo, br, l):
    tm = 512
    full = lambda r, c: pl.BlockSpec((r, c), lambda i: (0, 0))
    return pl.pallas_call(
        _merge_kernel,
        out_shape=(jax.ShapeDtypeStruct((N_TOK, D_MODEL), F32),
                   jax.ShapeDtypeStruct((N_TOK, D_MODEL), BF16),
                   jax.ShapeDtypeStruct((N_TOK, 128), F32)),
        grid=(N_TOK // tm,),
        in_specs=[
            pl.BlockSpec((tm, D_MODEL), lambda i: (i, 0)),
            pl.BlockSpec((tm, NA_WIDTH), lambda i: (i, 0)),
            pl.BlockSpec((tm, GQA_Q_WIDTH), lambda i: (i, 0)),
            pl.BlockSpec((tm, S5_CH), lambda i: (i, COL_U // S5_CH)),
            pl.BlockSpec((tm, S5_CH), lambda i: (i, 0)),
            pl.BlockSpec((tm, N_BRANCH * D_MODEL), lambda i: (i, COL_GATES // (N_BRANCH * D_MODEL))),
            pl.BlockSpec((None, None, 6, D_MODEL), lambda i: (l, _cond_index(i, tm), 0, 0)),
            full(1, S5_CH), full(S5_CH, S5_CH), full(NA_WIDTH, D_MODEL), full(GQA_Q_WIDTH, D_MODEL),
            full(S5_CH, D_MODEL), full(D_MODEL, D_MODEL), full(1, D_MODEL),
            full(D_MODEL, 128), full(D_MODEL, 128), full(1, 128),
        ],
        out_specs=(pl.BlockSpec((tm, D_MODEL), lambda i: (i, 0)),
                   pl.BlockSpec((tm, D_MODEL), lambda i: (i, 0)),
                   pl.BlockSpec((tm, 128), lambda i: (i, 0))),
        compiler_params=pltpu.CompilerParams(
            dimension_semantics=("parallel",), vmem_limit_bytes=VMEM_LIMIT),
        name="merge",
    )(x, oa, ob, z, yc, z, mod, d_s5, wglu, wa, wb, wc, wout, g2, wr_hi, wr_lo, br)


def _moe_kernel(h_ref, x_ref, comb_ref, mod_ref, wg_ref, wu_ref, wd_ref, fg_ref, x2_ref, y_ref, acc_sc):
    e = pl.program_id(1)
    h = h_ref[...]
    a = _dot(h, wg_ref[...])
    b = _dot(h, wu_ref[...])
    comb = comb_ref[...]
    lane = lax.broadcasted_iota(jnp.int32, comb.shape, 1)
    ce = jnp.sum(jnp.where(lane == e, comb, 0.0), axis=-1, keepdims=True)
    act = (a * jax.nn.sigmoid(a)) * b * ce
    contrib = _dot(act.astype(BF16), wd_ref[...])

    @pl.when(e == 0)
    def _():
        acc_sc[...] = contrib

    @pl.when(e > 0)
    def _():
        acc_sc[...] += contrib

    @pl.when(e == MOE_EXPERTS - 1)
    def _():
        x2 = x_ref[...] + mod_ref[5:6, :] * acc_sc[...]
        x2_ref[...] = x2
        y_ref[...] = _rms(x2, fg_ref[...])


def _moe(h2, x1, comb, mod, wg, wu, wd, fg, l):
    tm = 1024
    tok = pl.BlockSpec((tm, D_MODEL), lambda i, e: (i, 0))
    out = jax.ShapeDtypeStruct((N_TOK, D_MODEL), F32)
    return pl.pallas_call(
        _moe_kernel,
        out_shape=(out, out),
        grid=(N_TOK // tm, MOE_EXPERTS),
        in_specs=[
            tok, tok,
            pl.BlockSpec((tm, 128), lambda i, e: (i, 0)),
            pl.BlockSpec((None, None, 6, D_MODEL), lambda i, e: (l, _cond_index(i, tm), 0, 0)),
            pl.BlockSpec((None, D_MODEL, EXPERT_FF), lambda i, e: (e, 0, 0)),
            pl.BlockSpec((None, D_MODEL, EXPERT_FF), lambda i, e: (e, 0, 0)),
            pl.BlockSpec((None, EXPERT_FF, D_MODEL), lambda i, e: (e, 0, 0)),
            pl.BlockSpec((1, D_MODEL), lambda i, e: (0, 0)),
        ],
        out_specs=(tok, tok),
        scratch_shapes=[pltpu.VMEM((tm, D_MODEL), F32)],
        compiler_params=pltpu.CompilerParams(
            dimension_semantics=("parallel", "arbitrary"), vmem_limit_bytes=VMEM_LIMIT),
        name="moe",
    )(h2, x1, comb, mod, wg, wu, wd, fg)


def _pack_w_in(w):
    qkv = w[:, :COL_G_V + GQA_KV_WIDTH]
    u = w[:, COL_G_V + GQA_KV_WIDTH:COL_G_V + GQA_KV_WIDTH + S5_CH]
    gates = w[:, COL_G_V + GQA_KV_WIDTH + S5_CH:]
    pad = jnp.zeros((D_MODEL, COL_U - (COL_G_V + GQA_KV_WIDTH)), w.dtype)
    return jnp.concatenate([qkv, pad, u, gates], axis=1).astype(BF16)


def kernel(x_prompt, x_sample, cache_na_kv, cache_gqa_kv, state_ssm, c, c_ctx, norm_g, w_ada, b_ada, w_in, na_rpb, gqa_sink, s5_a_re, s5_a_im, s5_log_dt, s5_b_re, s5_b_im, s5_c_re, s5_c_im, s5_d, s5_w_glu, w_branch_a, w_branch_b, w_branch_c, w_out, moe_w_group, moe_b_group, moe_w_expert, moe_b_expert, moe_w_gate, moe_w_up, moe_w_down, final_g):
    cond = jnp.zeros((N_COND, D_MODEL), F32).at[0].set(c_ctx.astype(F32)).at[1:1 + DEC_BATCH].set(c.astype(F32))
    mod = _ada_mod(cond, w_ada.astype(F32), b_ada.astype(F32))

    x = jnp.concatenate([x_prompt.reshape(N_CTX_TOK, D_MODEL), x_sample.reshape(N_LAT_TOK, D_MODEL)], axis=0)
    cache_na = cache_na_kv.reshape(DEC_BATCH, DEPTH, 2, PAST_LEN, NA_WIDTH)
    cache_gqa = cache_gqa_kv.reshape(DEC_BATCH, DEPTH, 2, PAST_LEN, GQA_KV_WIDTH)
    cos_d, sin_d = _rope_tables()
    cos_q, sin_q = jnp.tile(cos_d, (1, GQA_Q_HEADS)), jnp.tile(sin_d, (1, GQA_Q_HEADS))
    cos_k, sin_k = jnp.tile(cos_d, (1, GQA_KV_HEADS)), jnp.tile(sin_d, (1, GQA_KV_HEADS))
    gn = S5_GROUPS * S5_STATE
    fg = final_g.astype(F32).reshape(1, D_MODEL)

    na_list, gqa_list, ssm_list = [], [], []
    y = None
    for l in range(DEPTH):
        z = _inproj(x, norm_g[l, 0].astype(F32).reshape(1, D_MODEL), mod, _pack_w_in(w_in[l]), l)

        zc = z[:N_CTX_TOK]
        na_list.append(jnp.stack([zc[:, COL_NA_K:COL_NA_K + NA_WIDTH].reshape(BATCH, SEQ, NA_HEADS, HEAD_DIM),
                                  zc[:, COL_NA_V:COL_NA_V + NA_WIDTH].reshape(BATCH, SEQ, NA_HEADS, HEAD_DIM)],
                                 axis=1))
        gqa_list.append(jnp.stack(
            [zc[:, COL_G_K:COL_G_K + GQA_KV_WIDTH].reshape(BATCH, SEQ, GQA_KV_HEADS, HEAD_DIM),
             zc[:, COL_G_V:COL_G_V + GQA_KV_WIDTH].reshape(BATCH, SEQ, GQA_KV_HEADS, HEAD_DIM)], axis=1))

        sink = gqa_sink[l].astype(F32)
        oa_c, ob_c = _ctx_attn(z, sink)
        oa_l = _na_lat(z, cache_na, _na_bias_tables(na_rpb[l]), l)
        ob_l = _swa_lat(z, cache_gqa, sink, cos_q, sin_q, cos_k, sin_k, l)

        ops = _s5_operators(s5_a_re[l], s5_a_im[l], s5_log_dt[l], s5_b_re[l], s5_b_im[l], s5_c_re[l], s5_c_im[l])
        ops = {k: (v.astype(BF16) if k in ("bs", "m", "cre", "cim") else v) for k, v in ops.items()}
        yc, fin = _s5_scan(z, ops, None, None, 0, N_CTX_TOK, SEQ // S5_T)
        h0 = state_ssm[:, l].astype(F32).reshape(DEC_BATCH, 2, 2, gn).transpose(1, 2, 0, 3)
        yc = _s5_scan(z, ops, h0, yc, N_CTX_TOK, N_LAT_TOK, DEC_SEQ // S5_T)
        ssm_list.append(fin.transpose(2, 0, 1, 3).reshape(BATCH, 2, 2, S5_GROUPS, S5_STATE).astype(x_prompt.dtype))

        oa = jnp.concatenate([oa_c, oa_l], axis=0)
        ob = jnp.concatenate([ob_c, ob_l], axis=0)
        wr = jnp.zeros((D_MODEL, 128), F32)
        wr = wr.at[:, :MOE_EXPERTS].set(moe_w_expert[l].astype(F32))
        wr = wr.at[:, MOE_EXPERTS:MOE_EXPERTS + MOE_GROUPS].set(moe_w_group[l].astype(F32))
        br = jnp.zeros((1, 128), F32)
        br = br.at[0, :MOE_EXPERTS].set(moe_b_expert[l].astype(F32))
        br = br.at[0, MOE_EXPERTS:MOE_EXPERTS + MOE_GROUPS].set(moe_b_group[l].astype(F32))
        wr_hi, wr_lo = _split_bf16(wr)
        x1, h2, comb = _merge(
            x, oa, ob, z, yc, mod, s5_d[l].astype(F32).reshape(1, S5_CH), s5_w_glu[l].astype(BF16),
            w_branch_a[l].astype(BF16), w_branch_b[l].astype(BF16), w_branch_c[l].astype(BF16),
            w_out[l].astype(BF16), norm_g[l, 1].astype(F32).reshape(1, D_MODEL), wr_hi, wr_lo, br, l)
        x, y = _moe(h2, x1, comb, mod, moe_w_gate[l].astype(BF16), moe_w_up[l].astype(BF16),
                    moe_w_down[l].astype(BF16), fg, l)

    y_prompt = y[:N_CTX_TOK].reshape(BATCH, SEQ, D_MODEL)
    y_sample = y[N_CTX_TOK:].reshape(DEC_BATCH, DEC_SEQ, D_MODEL)
    return (y_prompt, y_sample, jnp.stack(na_list, axis=1), jnp.stack(gqa_list, axis=1),
            jnp.stack(ssm_list, axis=1))
```

```python
import functools
import math

import numpy as np
import jax
import jax.numpy as jnp
from jax import lax
from jax.experimental import pallas as pl
from jax.experimental.pallas import tpu as pltpu

F32 = jnp.float32
BF16 = jnp.bfloat16

D_MODEL = 1024
BATCH = 32
SEQ = 256
DEPTH = 2
DEC_BATCH = 4
DEC_SEQ = 2048
PAST_LEN = 512
GRID_W = 64
GRID_ROWS = DEC_SEQ // GRID_W
HEAD_DIM = 64
NA_HEADS = 8
NA_WIN_H = 8
NA_WIN_W = 16
GQA_Q_HEADS = 8
GQA_KV_HEADS = 2
GQA_GROUP = GQA_Q_HEADS // GQA_KV_HEADS
SWA_WINDOW = 128
SWA_BLOCK = 128
ROPE_THETA = 10000.0
S5_CH = 512
S5_GROUP_CH = 16
S5_GROUPS = S5_CH // S5_GROUP_CH
S5_STATE = 64
N_BRANCH = 3
NA_WIDTH = NA_HEADS * HEAD_DIM
GQA_Q_WIDTH = GQA_Q_HEADS * HEAD_DIM
GQA_KV_WIDTH = GQA_KV_HEADS * HEAD_DIM
MOE_GROUPS = 4
MOE_EXPERTS_PER_GROUP = 4
MOE_EXPERTS = MOE_GROUPS * MOE_EXPERTS_PER_GROUP
EXPERT_FF = 512
EPS = 1e-6
NEG_INF = -1e30

N_CTX_TOK = BATCH * SEQ
N_LAT_TOK = DEC_BATCH * DEC_SEQ
N_TOK = N_CTX_TOK + N_LAT_TOK
N_COND = 8

COL_NA_Q = 0
COL_NA_K = 512
COL_NA_V = 1024
COL_G_Q = 1536
COL_G_K = 2048
COL_G_V = 2176
COL_U = 2560
COL_GATES = 3072
Z_COLS = 6144

S5_T = 16
S5_PAIRS = S5_GROUPS // 2
S5_OCT_PAIRS = 4
S5_BLOCK_TOK = 2048
S5_ROWS = S5_BLOCK_TOK // S5_T
S5_NPOW = 7

NA_QROWS = 4
NA_KROWS = 12

VMEM_LIMIT = 56 * 1024 * 1024


def _dot(a, b):
    return jnp.dot(a, b, preferred_element_type=F32)


def _dot_nt(a, b):
    return lax.dot_general(a, b, (((1,), (1,)), ((), ())), preferred_element_type=F32)


def _split_bf16(x):
    hi = x.astype(BF16)
    lo = (x - hi.astype(F32)).astype(BF16)
    return hi, lo


def _rms(x, g):
    return x * lax.rsqrt(jnp.mean(x * x, axis=-1, keepdims=True) + EPS) * g


def _cond_index(i, tm):
    nctx = N_CTX_TOK // tm
    return jnp.where(i < nctx, 0, 1 + ((i - nctx) * tm) // DEC_SEQ)


def _ada_kernel(c_ref, w_ref, b_ref, o_ref):
    c = c_ref[...]
    s = c * jax.nn.sigmoid(c)
    s_hi, s_lo = _split_bf16(s)
    w_hi, w_lo = _split_bf16(w_ref[...])
    o_ref[...] = _dot(s_hi, w_hi) + _dot(s_lo, w_hi) + _dot(s_hi, w_lo) + b_ref[...]


def _ada_mod(cond, w_ada, b_ada):
    tn = 1536
    n = 6 * D_MODEL
    out = pl.pallas_call(
        _ada_kernel,
        out_shape=jax.ShapeDtypeStruct((DEPTH, N_COND, n), F32),
        grid=(DEPTH, n // tn),
        in_specs=[
            pl.BlockSpec((N_COND, D_MODEL), lambda l, j: (0, 0)),
            pl.BlockSpec((None, D_MODEL, tn), lambda l, j: (l, 0, j)),
            pl.BlockSpec((None, 1, tn), lambda l, j: (l, 0, j)),
        ],
        out_specs=pl.BlockSpec((None, N_COND, tn), lambda l, j: (l, 0, j)),
        compiler_params=pltpu.CompilerParams(
            dimension_semantics=("parallel", "parallel"), vmem_limit_bytes=VMEM_LIMIT),
        name="ada_mod",
    )(cond, w_ada, b_ada.reshape(DEPTH, 1, n))
    return out.reshape(DEPTH, N_COND, 6, D_MODEL)


def _inproj_kernel(x_ref, g_ref, mod_ref, w_ref, o_ref, h_sc):
    @pl.when(pl.program_id(1) == 0)
    def _():
        h = _rms(x_ref[...], g_ref[...]) * (1.0 + mod_ref[1:2, :]) + mod_ref[0:1, :]
        h_sc[...] = h.astype(BF16)

    o_ref[...] = _dot(h_sc[...], w_ref[...])


def _inproj(x, g, mod, w, l):
    tm, tn = 1024, 1536
    return pl.pallas_call(
        _inproj_kernel,
        out_shape=jax.ShapeDtypeStruct((N_TOK, Z_COLS), F32),
        grid=(N_TOK // tm, Z_COLS // tn),
        in_specs=[
            pl.BlockSpec((tm, D_MODEL), lambda i, j: (i, 0)),
            pl.BlockSpec((1, D_MODEL), lambda i, j: (0, 0)),
            pl.BlockSpec((None, None, 6, D_MODEL), lambda i, j: (l, _cond_index(i, tm), 0, 0)),
            pl.BlockSpec((D_MODEL, tn), lambda i, j: (0, j)),
        ],
        out_specs=pl.BlockSpec((tm, tn), lambda i, j: (i, j)),
        scratch_shapes=[pltpu.VMEM((tm, D_MODEL), BF16)],
        compiler_params=pltpu.CompilerParams(
            dimension_semantics=("parallel", "arbitrary"), vmem_limit_bytes=VMEM_LIMIT),
        name="inproj",
    )(x, g, mod, w)


def _ctx_attn_kernel(sink_ref, na_ref, gq_ref, gkv_ref, oa_ref, ob_ref):
    scale = HEAD_DIM ** -0.5
    na = na_ref[...].astype(BF16)
    gq = gq_ref[...].astype(BF16)
    gkv = gkv_ref[...].astype(BF16)
    for h in range(NA_HEADS):
        q = na[:, COL_NA_Q + h * HEAD_DIM:COL_NA_Q + (h + 1) * HEAD_DIM]
        k = na[:, COL_NA_K + h * HEAD_DIM:COL_NA_K + (h + 1) * HEAD_DIM]
        v = na[:, COL_NA_V + h * HEAD_DIM:COL_NA_V + (h + 1) * HEAD_DIM]
        s = _dot_nt(q, k) * scale
        m = jnp.max(s, axis=-1, keepdims=True)
        p = jnp.exp(s - m)
        d = jnp.sum(p, axis=-1, keepdims=True)
        o = _dot(p.astype(BF16), v) / d
        oa_ref[:, h * HEAD_DIM:(h + 1) * HEAD_DIM] = o.astype(BF16)
    for h in range(GQA_Q_HEADS):
        kh = h // GQA_GROUP
        q = gq[:, h * HEAD_DIM:(h + 1) * HEAD_DIM]
        k = gkv[:, kh * HEAD_DIM:(kh + 1) * HEAD_DIM]
        v = gkv[:, GQA_KV_WIDTH + kh * HEAD_DIM:GQA_KV_WIDTH + (kh + 1) * HEAD_DIM]
        sink = sink_ref[h]
        s = _dot_nt(q, k) * scale
        m = jnp.maximum(jnp.max(s, axis=-1, keepdims=True), sink)
        p = jnp.exp(s - m)
        d = jnp.sum(p, axis=-1, keepdims=True) + jnp.exp(sink - m)
        o = _dot(p.astype(BF16), v) / d
        ob_ref[:, h * HEAD_DIM:(h + 1) * HEAD_DIM] = o.astype(BF16)


def _ctx_attn(z, sink):
    out = jax.ShapeDtypeStruct((N_CTX_TOK, NA_WIDTH), BF16)
    return pl.pallas_call(
        _ctx_attn_kernel,
        out_shape=(out, out),
        grid=(BATCH,),
        in_specs=[
            pl.BlockSpec(memory_space=pltpu.SMEM),
            pl.BlockSpec((SEQ, 3 * NA_WIDTH), lambda b: (b, 0)),
            pl.BlockSpec((SEQ, GQA_Q_WIDTH), lambda b: (b, COL_G_Q // GQA_Q_WIDTH)),
            pl.BlockSpec((SEQ, 2 * GQA_KV_WIDTH), lambda b: (b, COL_G_K // (2 * GQA_KV_WIDTH))),
        ],
        out_specs=(pl.BlockSpec((SEQ, NA_WIDTH), lambda b: (b, 0)),
                   pl.BlockSpec((SEQ, GQA_Q_WIDTH), lambda b: (b, 0))),
        compiler_params=pltpu.CompilerParams(
            dimension_semantics=("parallel",), vmem_limit_bytes=VMEM_LIMIT),
        name="ctx_attn",
    )(sink, z, z, z)


def _na_bias_tables(rpb):
    n_dc = 2 * NA_WIN_W - 1
    qc = np.arange(GRID_W)
    kc = np.arange(GRID_W)
    cstart = np.clip(qc - NA_WIN_W // 2, 0, GRID_W - NA_WIN_W)
    col_ok = (kc[None, :] >= cstart[:, None]) & (kc[None, :] < cstart[:, None] + NA_WIN_W)
    dc = np.clip(kc[None, :] - qc[:, None] + (NA_WIN_W - 1), 0, n_dc - 1)
    onehot = ((np.arange(n_dc)[:, None, None] == dc[None]) & col_ok[None]).astype(np.float32)
    band = jnp.einsum("hrd,dqk->hrqk", rpb.astype(F32), onehot, precision=lax.Precision.HIGHEST)
    band = jnp.where(col_ok, band, NEG_INF)
    band = jnp.pad(band, ((0, 0), (1, 1), (0, 0), (0, 0)), constant_values=NEG_INF)
    return jnp.concatenate([band[:, :-1], band[:, 1:]], axis=-1)


def _na_lat_kernel(q_ref, k_ref, v_ref, ck_ref, cv_ref, bias_ref, o_ref):
    i = pl.program_id(1)
    scale = HEAD_DIM ** -0.5
    nk = NA_KROWS * GRID_W
    kh = min(NA_WIN_H, GRID_ROWS)
    ks = jnp.clip(i * NA_QROWS - NA_WIN_H // 2, 0, GRID_ROWS - NA_KROWS)
    start = pl.multiple_of(ks * GRID_W, GRID_W)
    q = q_ref[...].astype(BF16)
    kw = k_ref[pl.ds(start, nk), :].astype(BF16)
    vw = v_ref[pl.ds(start, nk), :].astype(BF16)
    ck = ck_ref[...].astype(BF16)
    cv = cv_ref[...].astype(BF16)
    left = lax.broadcasted_iota(jnp.int32, (GRID_W, 2 * GRID_W), 1) < GRID_W
    pieces = []
    for qr in range(NA_QROWS):
        r = i * NA_QROWS + qr
        st = jnp.clip(r - kh // 2, 0, GRID_ROWS - kh)
        row = []
        for m in range(NA_KROWS // 2):
            k0 = ks + 2 * m
            plane = jnp.clip(k0 - r + NA_WIN_H, 0, 2 * NA_WIN_H - 1)
            ok0 = (k0 >= st) & (k0 < st + kh)
            ok1 = (k0 + 1 >= st) & (k0 + 1 < st + kh)
            row.append((plane, jnp.where(left, ok0.astype(jnp.int32), ok1.astype(jnp.int32)) > 0))
        pieces.append(row)
    for h in range(NA_HEADS):
        sl = slice(h * HEAD_DIM, (h + 1) * HEAD_DIM)
        qh = q[:, sl]
        bias = jnp.concatenate(
            [jnp.concatenate([jnp.where(ok, bias_ref[h, plane], NEG_INF) for plane, ok in row], axis=1)
             for row in pieces], axis=0)
        s_loc = _dot_nt(qh, kw[:, sl]) * scale + bias
        s_ctx = _dot_nt(qh, ck[:, sl]) * scale
        m = jnp.maximum(jnp.max(s_loc, axis=-1, keepdims=True), jnp.max(s_ctx, axis=-1, keepdims=True))
        p_loc = jnp.exp(s_loc - m)
        p_ctx = jnp.exp(s_ctx - m)
        d = jnp.sum(p_loc, axis=-1, keepdims=True) + jnp.sum(p_ctx, axis=-1, keepdims=True)
        o = (_dot(p_loc.astype(BF16), vw[:, sl]) + _dot(p_ctx.astype(BF16), cv[:, sl])) / d
        o_ref[:, sl] = o.astype(BF16)


def _na_lat(z, cache_na, bias, l):
    tq = NA_QROWS * GRID_W
    nsteps = GRID_ROWS // NA_QROWS
    lat_blk = N_CTX_TOK // DEC_SEQ
    return pl.pallas_call(
        _na_lat_kernel,
        out_shape=jax.ShapeDtypeStruct((N_LAT_TOK, NA_WIDTH), BF16),
        grid=(DEC_BATCH, nsteps),
        in_specs=[
            pl.BlockSpec((tq, NA_WIDTH), lambda b, i: (N_CTX_TOK // tq + b * nsteps + i, COL_NA_Q // NA_WIDTH)),
            pl.BlockSpec((DEC_SEQ, NA_WIDTH), lambda b, i: (lat_blk + b, COL_NA_K // NA_WIDTH)),
            pl.BlockSpec((DEC_SEQ, NA_WIDTH), lambda b, i: (lat_blk + b, COL_NA_V // NA_WIDTH)),
            pl.BlockSpec((None, None, None, PAST_LEN, NA_WIDTH), lambda b, i: (b, l, 0, 0, 0)),
            pl.BlockSpec((None, None, None, PAST_LEN, NA_WIDTH), lambda b, i: (b, l, 1, 0, 0)),
            pl.BlockSpec((NA_HEADS, 2 * NA_WIN_H, GRID_W, 2 * GRID_W), lambda b, i: (0, 0, 0, 0)),
        ],
        out_specs=pl.BlockSpec((tq, NA_WIDTH), lambda b, i: (b * nsteps + i, 0)),
        compiler_params=pltpu.CompilerParams(
            dimension_semantics=("parallel", "arbitrary"), vmem_limit_bytes=VMEM_LIMIT),
        name="na_lat",
    )(z, z, z, cache_na, cache_na, bias)


def _rope_tables():
    nf = HEAD_DIM // 4
    t = jnp.arange(DEC_SEQ)
    pos = jnp.stack([t // GRID_W, t % GRID_W], axis=-1).astype(F32)
    inv = ROPE_THETA ** (-jnp.arange(nf, dtype=F32) / nf)
    ang = pos[:, :, None] * inv
    cos = jnp.cos(ang)
    sin = jnp.sin(ang)
    cos_d = jnp.stack([cos, cos], axis=2).reshape(DEC_SEQ, HEAD_DIM)
    sin_d = jnp.stack([-sin, sin], axis=2).reshape(DEC_SEQ, HEAD_DIM)
    return cos_d, sin_d


def _rope(x, cos, sin_signed):
    n = x.shape[-1]
    nf = HEAD_DIM // 4
    lane = lax.broadcasted_iota(jnp.int32, x.shape, 1)
    first_half = (lane // nf) % 2 == 0
    partner = jnp.where(first_half, pltpu.roll(x, n - nf, 1), pltpu.roll(x, nf, 1))
    return x * cos + partner * sin_signed


def _swa_lat_kernel(sink_ref, q_ref, kv_ref, cq_ref, sq_ref, ckt_ref, skt_ref, ck_ref, cv_ref, o_ref, k_sc, v_sc):
    n = pl.program_id(1)
    scale = HEAD_DIM ** -0.5
    nwin = 3 * SWA_BLOCK

    @pl.when(n == 0)
    def _():
        kv = kv_ref[...]
        k_sc[...] = _rope(kv[:, :GQA_KV_WIDTH], ckt_ref[...], skt_ref[...]).astype(BF16)
        v_sc[...] = kv[:, GQA_KV_WIDTH:].astype(BF16)

    q = _rope(q_ref[...], cq_ref[...], sq_ref[...]).astype(BF16)
    start = pl.multiple_of(jnp.clip((n - 1) * SWA_BLOCK, 0, DEC_SEQ - nwin), SWA_BLOCK)
    kw = k_sc[pl.ds(start, nwin), :]
    vw = v_sc[pl.ds(start, nwin), :]
    ck = ck_ref[...].astype(BF16)
    cv = cv_ref[...].astype(BF16)
    rows = GQA_GROUP * SWA_BLOCK
    row = lax.broadcasted_iota(jnp.int32, (rows, nwin), 0)
    col = lax.broadcasted_iota(jnp.int32, (rows, nwin), 1)
    qpos = n * SWA_BLOCK + row % SWA_BLOCK
    kpos = start + col
    ok = jnp.abs(qpos - kpos) <= SWA_WINDOW
    grp = lax.broadcasted_iota(jnp.int32, (rows, 1), 0) // SWA_BLOCK
    for kh in range(GQA_KV_HEADS):
        sl = slice(kh * HEAD_DIM, (kh + 1) * HEAD_DIM)
        q4 = jnp.concatenate(
            [q[:, (kh * GQA_GROUP + g) * HEAD_DIM:(kh * GQA_GROUP + g + 1) * HEAD_DIM] for g in range(GQA_GROUP)],
            axis=0)
        sink = jnp.zeros((rows, 1), F32)
        for g in range(GQA_GROUP):
            sink = jnp.where(grp == g, sink_ref[kh * GQA_GROUP + g], sink)
        s_loc = jnp.where(ok, _dot_nt(q4, kw[:, sl]) * scale, NEG_INF)
        s_ctx = _dot_nt(q4, ck[:, sl]) * scale
        m = jnp.maximum(jnp.maximum(jnp.max(s_loc, axis=-1, keepdims=True),
                                    jnp.max(s_ctx, axis=-1, keepdims=True)), sink)
        p_loc = jnp.exp(s_loc - m)
        p_ctx = jnp.exp(s_ctx - m)
        d = (jnp.sum(p_loc, axis=-1, keepdims=True) + jnp.sum(p_ctx, axis=-1, keepdims=True)
             + jnp.exp(sink - m))
        o4 = (_dot(p_loc.astype(BF16), vw[:, sl]) + _dot(p_ctx.astype(BF16), cv[:, sl])) / d
        for g in range(GQA_GROUP):
            h = kh * GQA_GROUP + g
            o_ref[:, h * HEAD_DIM:(h + 1) * HEAD_DIM] = o4[g * SWA_BLOCK:(g + 1) * SWA_BLOCK].astype(BF16)


def _swa_lat(z, cache_gqa, sink, cos_q, sin_q, cos_k, sin_k, l):
    nb = DEC_SEQ // SWA_BLOCK
    lat_blk = N_CTX_TOK // DEC_SEQ
    return pl.pallas_call(
        _swa_lat_kernel,
        out_shape=jax.ShapeDtypeStruct((N_LAT_TOK, GQA_Q_WIDTH), BF16),
        grid=(DEC_BATCH, nb),
        in_specs=[
            pl.BlockSpec(memory_space=pltpu.SMEM),
            pl.BlockSpec((SWA_BLOCK, GQA_Q_WIDTH),
                         lambda b, n: (N_CTX_TOK // SWA_BLOCK + b * nb + n, COL_G_Q // GQA_Q_WIDTH)),
            pl.BlockSpec((DEC_SEQ, 2 * GQA_KV_WIDTH), lambda b, n: (lat_blk + b, COL_G_K // (2 * GQA_KV_WIDTH))),
            pl.BlockSpec((SWA_BLOCK, GQA_Q_WIDTH), lambda b, n: (n, 0)),
            pl.BlockSpec((SWA_BLOCK, GQA_Q_WIDTH), lambda b, n: (n, 0)),
            pl.BlockSpec((DEC_SEQ, GQA_KV_WIDTH), lambda b, n: (0, 0)),
            pl.BlockSpec((DEC_SEQ, GQA_KV_WIDTH), lambda b, n: (0, 0)),
            pl.BlockSpec((None, None, None, PAST_LEN, GQA_KV_WIDTH), lambda b, n: (b, l, 0, 0, 0)),
            pl.BlockSpec((None, None, None, PAST_LEN, GQA_KV_WIDTH), lambda b, n: (b, l, 1, 0, 0)),
        ],
        out_specs=pl.BlockSpec((SWA_BLOCK, GQA_Q_WIDTH), lambda b, n: (b * nb + n, 0)),
        scratch_shapes=[pltpu.VMEM((DEC_SEQ, GQA_KV_WIDTH), BF16), pltpu.VMEM((DEC_SEQ, GQA_KV_WIDTH), BF16)],
        compiler_params=pltpu.CompilerParams(
            dimension_semantics=("parallel", "arbitrary"), vmem_limit_bytes=VMEM_LIMIT),
        name="swa_lat",
    )(sink, z, z, cos_q, sin_q, cos_k, sin_k, cache_gqa, cache_gqa)


def _s5_operators(a_re, a_im, log_dt, b_re, b_im, c_re, c_im):
    T, G, N, C, P = S5_T, S5_GROUPS, S5_STATE, S5_GROUP_CH, S5_PAIRS
    tau = jnp.arange(T + 1, dtype=F32)
    eye2 = jnp.eye(2, dtype=F32)
    out = {k: [] for k in ("bs", "m", "cre", "cim", "apr", "api")}

    def pairs(x):
        return x.reshape((P, 2) + x.shape[1:])

    def diag(x, axis):
        e = eye2.reshape((1, 2) + (1,) * (axis - 1) + (2,) + (1,) * (x.ndim - axis - 1))
        return jnp.expand_dims(x, axis + 1) * e

    for d in range(2):
        A = lax.complex(a_re[d].astype(F32), a_im[d].astype(F32))
        dt = jnp.exp(log_dt[d].astype(F32))[:, None]
        a_bar = jnp.exp(A * dt)
        pw = jnp.exp((A * dt)[None] * tau[:, None, None])
        b_bar = ((a_bar - 1.0) / A)[..., None] * lax.complex(b_re[d].astype(F32), b_im[d].astype(F32))
        c_mat = lax.complex(c_re[d].astype(F32), c_im[d].astype(F32))
        kern = jnp.einsum("gon,tgn,gni->gtoi", c_mat, pw[:T], b_bar, precision=lax.Precision.HIGHEST).real
        k0 = diag(pairs(kern).transpose(0, 1, 4, 2, 3), 3).reshape(P, 2 * C, T * 2 * C)
        zero = jnp.zeros_like(k0)
        if d == 0:
            p_in = pw[:T][::-1]
            p_out = pw[1:T + 1]
            kp = jnp.concatenate([zero, k0], axis=2)
            m = [kp[:, :, (T - s) * 2 * C:(2 * T - s) * 2 * C] for s in range(T)]
        else:
            p_in = pw[:T]
            p_out = pw[1:T + 1][::-1]
            kp = jnp.concatenate([k0.reshape(P, 2 * C, T, 2 * C)[:, :, ::-1].reshape(k0.shape), zero], axis=2)
            m = [kp[:, :, (T - 1 - s) * 2 * C:(2 * T - 1 - s) * 2 * C] for s in range(T)]
        out["m"].append(jnp.concatenate(m, axis=1))
        pin = p_in.reshape(T, P, 2 * N).transpose(1, 0, 2)[:, :, None, :]
        bx = diag(pairs(b_bar.transpose(0, 2, 1)), 2).reshape(P, 1, 2 * C, 2 * N)
        bs = pin * bx
        out["bs"].append(jnp.concatenate([bs.real, bs.imag], axis=-1).reshape(P, 2 * T * C, 4 * N))
        cx = diag(pairs(c_mat.transpose(0, 2, 1)), 2).reshape(P, 2 * N, 2 * C)
        pout = p_out.reshape(T, P, 2 * N).transpose(1, 2, 0)
        co = jnp.tile(cx, (1, 1, T)) * jnp.repeat(pout, 2 * C, axis=2)
        out["cre"].append(co.real)
        out["cim"].append(-co.imag)
        pw2 = jnp.exp((A * dt)[None] * (T * 2.0 ** jnp.arange(S5_NPOW, dtype=F32))[:, None, None])
        out["apr"].append(pw2.real.reshape(S5_NPOW, G * N))
        out["api"].append(pw2.imag.reshape(S5_NPOW, G * N))
    return {k: jnp.stack(v) for k, v in out.items()}


def _s5_kernel(u_ref, bs_ref, m_ref, cre_ref, cim_ref, apr_ref, api_ref, *rest, kseq, has_h0):
    if has_h0:
        h0_ref, _, y_ref = rest
    else:
        y_ref, fin_ref, fin_sc = rest
    t = pl.program_id(1)
    rows, half, pc = S5_ROWS, 2 * S5_STATE, 2 * S5_GROUP_CH
    k = lax.broadcasted_iota(jnp.int32, (rows, half), 0) % kseq
    shifts = [1 << i for i in range(kseq.bit_length() - 1)]
    xs = [u_ref[pl.ds(s, rows, stride=S5_T), :].astype(BF16) for s in range(S5_T)]
    ys = []
    for p in range(S5_OCT_PAIRS):
        lanes = slice(p * half, (p + 1) * half)
        xp = jnp.concatenate([x[:, p * pc:(p + 1) * pc] for x in xs], axis=1)
        acc = None
        for d in range(2):
            inc = _dot(xp, bs_ref[d, p])
            sr, si = inc[:, :half], inc[:, half:]
            if has_h0:
                h0r = h0_ref[d, 0, t, :, lanes]
                h0i = h0_ref[d, 1, t, :, lanes]
                ar, ai = apr_ref[d, 0:1, lanes], api_ref[d, 0:1, lanes]
                first = (k == 0) if d == 0 else (k == kseq - 1)
                sr = sr + jnp.where(first, ar * h0r - ai * h0i, 0.0)
                si = si + jnp.where(first, ar * h0i + ai * h0r, 0.0)
            else:
                h0r = h0i = 0.0
            for i, sh in enumerate(shifts):
                ar, ai = apr_ref[d, i:i + 1, lanes], api_ref[d, i:i + 1, lanes]
                ok = (k >= sh) if d == 0 else (k < kseq - sh)
                amt = sh if d == 0 else rows - sh
                rr = jnp.where(ok, pltpu.roll(sr, amt, 0), 0.0)
                ri = jnp.where(ok, pltpu.roll(si, amt, 0), 0.0)
                sr, si = sr + ar * rr - ai * ri, si + ar * ri + ai * rr
            inner = (k >= 1) if d == 0 else (k < kseq - 1)
            amt = 1 if d == 0 else rows - 1
            hr = jnp.where(inner, pltpu.roll(sr, amt, 0), h0r)
            hi = jnp.where(inner, pltpu.roll(si, amt, 0), h0i)
            yd = (_dot(xp, m_ref[d, p]) + _dot(hr.astype(BF16), cre_ref[d, p])
                  + _dot(hi.astype(BF16), cim_ref[d, p]))
            acc = yd if acc is None else acc + yd
            if not has_h0:
                last = kseq - 1 if d == 0 else 0
                fin_sc[0] = sr
                fin_sc[1] = si
                fin_ref[d, 0, :, lanes] = fin_sc[0, pl.ds(last, rows // kseq, stride=kseq), :]
                fin_ref[d, 1, :, lanes] = fin_sc[1, pl.ds(last, rows // kseq, stride=kseq), :]
        ys.append(acc)
    for j in range(S5_T):
        y_ref[pl.ds(j, rows, stride=S5_T), :] = jnp.concatenate([y[:, j * pc:(j + 1) * pc] for y in ys], axis=1)


def _s5_scan(z, ops, h0, yc_prev, tok0, ntok, kseq):
    nblk, tb0 = ntok // S5_BLOCK_TOK, tok0 // S5_BLOCK_TOK
    noct = S5_PAIRS // S5_OCT_PAIRS
    half, gn = 2 * S5_STATE, S5_GROUPS * S5_STATE
    pw = 2 * S5_T * S5_GROUP_CH
    wspec = lambda r, c: pl.BlockSpec((2, S5_OCT_PAIRS, r, c), lambda q, t: (0, q, 0, 0))
    aspec = pl.BlockSpec((2, S5_NPOW, S5_OCT_PAIRS * half), lambda q, t: (0, 0, q))
    in_specs = [
        pl.BlockSpec((S5_BLOCK_TOK, 128), lambda q, t: (tb0 + t, COL_U // 128 + q)),
        wspec(pw, 2 * half), wspec(pw, pw), wspec(half, pw), wspec(half, pw), aspec, aspec,
    ]
    args = [z, ops["bs"], ops["m"], ops["cre"], ops["cim"], ops["apr"], ops["api"]]
    y_shape = jax.ShapeDtypeStruct((N_TOK, S5_CH), F32)
    y_spec = pl.BlockSpec((S5_BLOCK_TOK, 128), lambda q, t: (tb0 + t, q))
    params = pltpu.CompilerParams(dimension_semantics=("parallel", "arbitrary"), vmem_limit_bytes=VMEM_LIMIT)
    kern = functools.partial(_s5_kernel, kseq=kseq, has_h0=h0 is not None)
    if h0 is None:
        nseq = S5_ROWS // kseq
        return pl.pallas_call(
            kern, grid=(noct, nblk), in_specs=in_specs,
            out_shape=(y_shape, jax.ShapeDtypeStruct((2, 2, nblk * nseq, gn), F32)),
            out_specs=(y_spec, pl.BlockSpec((2, 2, nseq, S5_OCT_PAIRS * half), lambda q, t: (0, 0, t, q))),
            scratch_shapes=[pltpu.VMEM((2, S5_ROWS, half), F32)],
            compiler_params=params, name="s5_ctx",
        )(*args)
    nseq = h0.shape[2]
    in_specs += [pl.BlockSpec((2, 2, nseq, 1, S5_OCT_PAIRS * half), lambda q, t: (0, 0, 0, 0, q)),
                 pl.BlockSpec(memory_space=pl.ANY)]
    return pl.pallas_call(
        kern, grid=(noct, nblk), in_specs=in_specs, out_shape=y_shape, out_specs=y_spec,
        input_output_aliases={len(args) + 1: 0},
        compiler_params=params, name="s5_lat",
    )(*args, h0.reshape(2, 2, nseq, 1, gn), yc_prev)


def _route(logits):
    lane = lax.broadcasted_iota(jnp.int32, logits.shape, 1)
    big = jnp.int32(1 << 20)
    is_g = (lane >= MOE_EXPERTS) & (lane < MOE_EXPERTS + MOE_GROUPS)
    lg = jnp.where(is_g, logits, NEG_INF)
    gmax = jnp.max(lg, axis=-1, keepdims=True)
    gsel = jnp.min(jnp.where(is_g & (lg == gmax), lane, big), axis=-1, keepdims=True) - MOE_EXPERTS
    p_group = 1.0 / jnp.sum(jnp.where(is_g, jnp.exp(lg - gmax), 0.0), axis=-1, keepdims=True)
    in_grp = (lane < MOE_EXPERTS) & (lane // MOE_EXPERTS_PER_GROUP == gsel)
    le = jnp.where(in_grp, logits, NEG_INF)
    v1 = jnp.max(le, axis=-1, keepdims=True)
    i1 = jnp.min(jnp.where(in_grp & (le == v1), lane, big), axis=-1, keepdims=True)
    rest = in_grp & (lane != i1)
    le2 = jnp.where(rest, logits, NEG_INF)
    v2 = jnp.max(le2, axis=-1, keepdims=True)
    i2 = jnp.min(jnp.where(rest & (le2 == v2), lane, big), axis=-1, keepdims=True)
    e2 = jnp.exp(v2 - v1)
    w1 = 1.0 / (1.0 + e2)
    w2 = e2 / (1.0 + e2)
    return jnp.where(lane == i1, w1 * p_group, 0.0) + jnp.where(lane == i2, w2 * p_group, 0.0)


def _merge_kernel(x_ref, oa_ref, ob_ref, u_ref, yc_ref, gates_ref, mod_ref, d_ref, wglu_ref, wa_ref, wb_ref,
                  wc_ref, wout_ref, g2_ref, wrh_ref, wrl_ref, br_ref, x1_ref, h2_ref, comb_ref):
    y = u_ref[...] * d_ref[...] + yc_ref[...]
    y = y * (0.5 * (1.0 + jnp.tanh(math.sqrt(2.0 / math.pi) * (y + 0.044715 * (y * y * y)))))
    oc = y * jax.nn.sigmoid(_dot(y.astype(BF16), wglu_ref[...]))
    gate = jax.nn.sigmoid(gates_ref[...])
    merged = (gate[:, :D_MODEL] * _dot(oa_ref[...], wa_ref[...])
              + gate[:, D_MODEL:2 * D_MODEL] * _dot(ob_ref[...], wb_ref[...])
              + gate[:, 2 * D_MODEL:] * _dot(oc.astype(BF16), wc_ref[...]))
    x1 = x_ref[...] + mod_ref[2:3, :] * _dot(merged.astype(BF16), wout_ref[...])
    x1_ref[...] = x1
    h2 = _rms(x1, g2_ref[...]) * (1.0 + mod_ref[4:5, :]) + mod_ref[3:4, :]
    h2_ref[...] = h2.astype(BF16)
    h_hi, h_lo = _split_bf16(h2)
    logits = _dot(h_hi, wrh_ref[...]) + _dot(h_lo, wrh_ref[...]) + _dot(h_hi, wrl_ref[...]) + br_ref[...]
    comb_ref[...] = _route(logits)


def _merge(x, oa, ob, z, yc, mod, d_s5, wglu, wa, wb, wc, wout, g2, wr_hi, wr_lo, br, l):
    tm = 512
    full = lambda r, c: pl.BlockSpec((r, c), lambda i: (0, 0))
    return pl.pallas_call(
        _merge_kernel,
        out_shape=(jax.ShapeDtypeStruct((N_TOK, D_MODEL), F32),
                   jax.ShapeDtypeStruct((N_TOK, D_MODEL), BF16),
                   jax.ShapeDtypeStruct((N_TOK, 128), F32)),
        grid=(N_TOK // tm,),
        in_specs=[
            pl.BlockSpec((tm, D_MODEL), lambda i: (i, 0)),
            pl.BlockSpec((tm, NA_WIDTH), lambda i: (i, 0)),
            pl.BlockSpec((tm, GQA_Q_WIDTH), lambda i: (i, 0)),
            pl.BlockSpec((tm, S5_CH), lambda i: (i, COL_U // S5_CH)),
            pl.BlockSpec((tm, S5_CH), lambda i: (i, 0)),
            pl.BlockSpec((tm, N_BRANCH * D_MODEL), lambda i: (i, COL_GATES // (N_BRANCH * D_MODEL))),
            pl.BlockSpec((None, None, 6, D_MODEL), lambda i: (l, _cond_index(i, tm), 0, 0)),
            full(1, S5_CH), full(S5_CH, S5_CH), full(NA_WIDTH, D_MODEL), full(GQA_Q_WIDTH, D_MODEL),
            full(S5_CH, D_MODEL), full(D_MODEL, D_MODEL), full(1, D_MODEL),
            full(D_MODEL, 128), full(D_MODEL, 128), full(1, 128),
        ],
        out_specs=(pl.BlockSpec((tm, D_MODEL), lambda i: (i, 0)),
                   pl.BlockSpec((tm, D_MODEL), lambda i: (i, 0)),
                   pl.BlockSpec((tm, 128), lambda i: (i, 0))),
        compiler_params=pltpu.CompilerParams(
            dimension_semantics=("parallel",), vmem_limit_bytes=VMEM_LIMIT),
        name="merge",
    )(x, oa, ob, z, yc, z, mod, d_s5, wglu, wa, wb, wc, wout, g2, wr_hi, wr_lo, br)


def _moe_kernel(h_ref, x_ref, comb_ref, mod_ref, wg_ref, wu_ref, wd_ref, fg_ref, x2_ref, y_ref, acc_sc):
    e = pl.program_id(1)
    h = h_ref[...]
    a = _dot(h, wg_ref[...])
    b = _dot(h, wu_ref[...])
    comb = comb_ref[...]
    lane = lax.broadcasted_iota(jnp.int32, comb.shape, 1)
    ce = jnp.sum(jnp.where(lane == e, comb, 0.0), axis=-1, keepdims=True)
    act = (a * jax.nn.sigmoid(a)) * b * ce
    contrib = _dot(act.astype(BF16), wd_ref[...])

    @pl.when(e == 0)
    def _():
        acc_sc[...] = contrib

    @pl.when(e > 0)
    def _():
        acc_sc[...] += contrib

    @pl.when(e == MOE_EXPERTS - 1)
    def _():
        x2 = x_ref[...] + mod_ref[5:6, :] * acc_sc[...]
        x2_ref[...] = x2
        y_ref[...] = _rms(x2, fg_ref[...])


def _moe(h2, x1, comb, mod, wg, wu, wd, fg, l):
    tm = 1024
    tok = pl.BlockSpec((tm, D_MODEL), lambda i, e: (i, 0))
    out = jax.ShapeDtypeStruct((N_TOK, D_MODEL), F32)
    return pl.pallas_call(
        _moe_kernel,
        out_shape=(out, out),
        grid=(N_TOK // tm, MOE_EXPERTS),
        in_specs=[
            tok, tok,
            pl.BlockSpec((tm, 128), lambda i, e: (i, 0)),
            pl.BlockSpec((None, None, 6, D_MODEL), lambda i, e: (l, _cond_index(i, tm), 0, 0)),
            pl.BlockSpec((None, D_MODEL, EXPERT_FF), lambda i, e: (e, 0, 0)),
            pl.BlockSpec((None, D_MODEL, EXPERT_FF), lambda i, e: (e, 0, 0)),
            pl.BlockSpec((None, EXPERT_FF, D_MODEL), lambda i, e: (e, 0, 0)),
            pl.BlockSpec((1, D_MODEL), lambda i, e: (0, 0)),
        ],
        out_specs=(tok, tok),
        scratch_shapes=[pltpu.VMEM((tm, D_MODEL), F32)],
        compiler_params=pltpu.CompilerParams(
            dimension_semantics=("parallel", "arbitrary"), vmem_limit_bytes=VMEM_LIMIT),
        name="moe",
    )(h2, x1, comb, mod, wg, wu, wd, fg)


def _pack_w_in(w):
    qkv = w[:, :COL_G_V + GQA_KV_WIDTH]
    u = w[:, COL_G_V + GQA_KV_WIDTH:COL_G_V + GQA_KV_WIDTH + S5_CH]
    gates = w[:, COL_G_V + GQA_KV_WIDTH + S5_CH:]
    pad = jnp.zeros((D_MODEL, COL_U - (COL_G_V + GQA_KV_WIDTH)), w.dtype)
    return jnp.concatenate([qkv, pad, u, gates], axis=1).astype(BF16)


def kernel(x_prompt, x_sample, cache_na_kv, cache_gqa_kv, state_ssm, c, c_ctx, norm_g, w_ada, b_ada, w_in, na_rpb, gqa_sink, s5_a_re, s5_a_im, s5_log_dt, s5_b_re, s5_b_im, s5_c_re, s5_c_im, s5_d, s5_w_glu, w_branch_a, w_branch_b, w_branch_c, w_out, moe_w_group, moe_b_group, moe_w_expert, moe_b_expert, moe_w_gate, moe_w_up, moe_w_down, final_g):
    cond = jnp.zeros((N_COND, D_MODEL), F32).at[0].set(c_ctx.astype(F32)).at[1:1 + DEC_BATCH].set(c.astype(F32))
    mod = _ada_mod(cond, w_ada.astype(F32), b_ada.astype(F32))

    x = jnp.concatenate([x_prompt.reshape(N_CTX_TOK, D_MODEL), x_sample.reshape(N_LAT_TOK, D_MODEL)], axis=0)
    cache_na = cache_na_kv.reshape(DEC_BATCH, DEPTH, 2, PAST_LEN, NA_WIDTH)
    cache_gqa = cache_gqa_kv.reshape(DEC_BATCH, DEPTH, 2, PAST_LEN, GQA_KV_WIDTH)
    cos_d, sin_d = _rope_tables()
    cos_q, sin_q = jnp.tile(cos_d, (1, GQA_Q_HEADS)), jnp.tile(sin_d, (1, GQA_Q_HEADS))
    cos_k, sin_k = jnp.tile(cos_d, (1, GQA_KV_HEADS)), jnp.tile(sin_d, (1, GQA_KV_HEADS))
    gn = S5_GROUPS * S5_STATE
    fg = final_g.astype(F32).reshape(1, D_MODEL)

    na_list, gqa_list, ssm_list = [], [], []
    y = None
    for l in range(DEPTH):
        z = _inproj(x, norm_g[l, 0].astype(F32).reshape(1, D_MODEL), mod, _pack_w_in(w_in[l]), l)

        zc = z[:N_CTX_TOK]
        na_list.append(jnp.stack([zc[:, COL_NA_K:COL_NA_K + NA_WIDTH].reshape(BATCH, SEQ, NA_HEADS, HEAD_DIM),
                                  zc[:, COL_NA_V:COL_NA_V + NA_WIDTH].reshape(BATCH, SEQ, NA_HEADS, HEAD_DIM)],
                                 axis=1))
        gqa_list.append(jnp.stack(
            [zc[:, COL_G_K:COL_G_K + GQA_KV_WIDTH].reshape(BATCH, SEQ, GQA_KV_HEADS, HEAD_DIM),
             zc[:, COL_G_V:COL_G_V + GQA_KV_WIDTH].reshape(BATCH, SEQ, GQA_KV_HEADS, HEAD_DIM)], axis=1))

        sink = gqa_sink[l].astype(F32)
        oa_c, ob_c = _ctx_attn(z, sink)
        oa_l = _na_lat(z, cache_na, _na_bias_tables(na_rpb[l]), l)
        ob_l = _swa_lat(z, cache_gqa, sink, cos_q, sin_q, cos_k, sin_k, l)

        ops = _s5_operators(s5_a_re[l], s5_a_im[l], s5_log_dt[l], s5_b_re[l], s5_b_im[l], s5_c_re[l], s5_c_im[l])
        ops = {k: (v.astype(BF16) if k in ("bs", "m", "cre", "cim") else v) for k, v in ops.items()}
        yc, fin = _s5_scan(z, ops, None, None, 0, N_CTX_TOK, SEQ // S5_T)
        h0 = state_ssm[:, l].astype(F32).reshape(DEC_BATCH, 2, 2, gn).transpose(1, 2, 0, 3)
        yc = _s5_scan(z, ops, h0, yc, N_CTX_TOK, N_LAT_TOK, DEC_SEQ // S5_T)
        ssm_list.append(fin.transpose(2, 0, 1, 3).reshape(BATCH, 2, 2, S5_GROUPS, S5_STATE).astype(x_prompt.dtype))

        oa = jnp.concatenate([oa_c, oa_l], axis=0)
        ob = jnp.concatenate([ob_c, ob_l], axis=0)
        wr = jnp.zeros((D_MODEL, 128), F32)
        wr = wr.at[:, :MOE_EXPERTS].set(moe_w_expert[l].astype(F32))
        wr = wr.at[:, MOE_EXPERTS:MOE_EXPERTS + MOE_GROUPS].set(moe_w_group[l].astype(F32))
        br = jnp.zeros((1, 128), F32)
        br = br.at[0, :MOE_EXPERTS].set(moe_b_expert[l].astype(F32))
        br = br.at[0, MOE_EXPERTS:MOE_EXPERTS + MOE_GROUPS].set(moe_b_group[l].astype(F32))
        wr_hi, wr_lo = _split_bf16(wr)
        x1, h2, comb = _merge(
            x, oa, ob, z, yc, mod, s5_d[l].astype(F32).reshape(1, S5_CH), s5_w_glu[l].astype(BF16),
            w_branch_a[l].astype(BF16), w_branch_b[l].astype(BF16), w_branch_c[l].astype(BF16),
            w_out[l].astype(BF16), norm_g[l, 1].astype(F32).reshape(1, D_MODEL), wr_hi, wr_lo, br, l)
        x, y = _moe(h2, x1, comb, mod, moe_w_gate[l].astype(BF16), moe_w_up[l].astype(BF16),
                    moe_w_down[l].astype(BF16), fg, l)

    y_prompt = y[:N_CTX_TOK].reshape(BATCH, SEQ, D_MODEL)
    y_sample = y[N_CTX_TOK:].reshape(DEC_BATCH, DEC_SEQ, D_MODEL)
    return (y_prompt, y_sample, jnp.stack(na_list, axis=1), jnp.stack(gqa_list, axis=1),
            jnp.stack(ssm_list, axis=1))
```

```python
import functools
import math

import numpy as np
import jax
import jax.numpy as jnp
from jax import lax
from jax.experimental import pallas as pl
from jax.experimental.pallas import tpu as pltpu

F32 = jnp.float32
BF16 = jnp.bfloat16

D_MODEL = 1024
BATCH = 32
SEQ = 256
DEPTH = 2
DEC_BATCH = 4
DEC_SEQ = 2048
PAST_LEN = 512
GRID_W = 64
GRID_ROWS = DEC_SEQ // GRID_W
HEAD_DIM = 64
NA_HEADS = 8
NA_WIN_H = 8
NA_WIN_W = 16
GQA_Q_HEADS = 8
GQA_KV_HEADS = 2
GQA_GROUP = GQA_Q_HEADS // GQA_KV_HEADS
SWA_WINDOW = 128
SWA_BLOCK = 128
ROPE_THETA = 10000.0
S5_CH = 512
S5_GROUP_CH = 16
S5_GROUPS = S5_CH // S5_GROUP_CH
S5_STATE = 64
N_BRANCH = 3
NA_WIDTH = NA_HEADS * HEAD_DIM
GQA_Q_WIDTH = GQA_Q_HEADS * HEAD_DIM
GQA_KV_WIDTH = GQA_KV_HEADS * HEAD_DIM
MOE_GROUPS = 4
MOE_EXPERTS_PER_GROUP = 4
MOE_EXPERTS = MOE_GROUPS * MOE_EXPERTS_PER_GROUP
EXPERT_FF = 512
EPS = 1e-6
NEG_INF = -1e30

N_CTX_TOK = BATCH * SEQ
N_LAT_TOK = DEC_BATCH * DEC_SEQ
N_TOK = N_CTX_TOK + N_LAT_TOK
N_COND = 8

COL_NA_Q = 0
COL_NA_K = 512
COL_NA_V = 1024
COL_G_Q = 1536
COL_G_K = 2048
COL_G_V = 2176
COL_U = 2560
COL_GATES = 3072
Z_COLS = 6144

S5_T = 16
S5_PAIRS = S5_GROUPS // 2
S5_OCT_PAIRS = 4
S5_BLOCK_TOK = 2048
S5_ROWS = S5_BLOCK_TOK // S5_T
S5_NPOW = 7

MOE_ROW_WORDS = D_MODEL // 2 + 128
MOE_TILE = 512
MOE_ROWS = N_TOK + MOE_GROUPS * MOE_TILE
MOE_TOK_BLOCK = 512

NA_QROWS = 4
NA_KROWS = 12

VMEM_LIMIT = 56 * 1024 * 1024


def _dot(a, b):
    return jnp.dot(a, b, preferred_element_type=F32)


def _dot_nt(a, b):
    return lax.dot_general(a, b, (((1,), (1,)), ((), ())), preferred_element_type=F32)


def _split_bf16(x):
    hi = x.astype(BF16)
    lo = (x - hi.astype(F32)).astype(BF16)
    return hi, lo


def _rms(x, g):
    return x * lax.rsqrt(jnp.mean(x * x, axis=-1, keepdims=True) + EPS) * g


def _cond_index(i, tm):
    nctx = N_CTX_TOK // tm
    return jnp.where(i < nctx, 0, 1 + ((i - nctx) * tm) // DEC_SEQ)


def _ada_kernel(c_ref, w_ref, b_ref, o_ref):
    c = c_ref[...]
    s = c * jax.nn.sigmoid(c)
    s_hi, s_lo = _split_bf16(s)
    w_hi, w_lo = _split_bf16(w_ref[...])
    o_ref[...] = _dot(s_hi, w_hi) + _dot(s_lo, w_hi) + _dot(s_hi, w_lo) + b_ref[...]


def _ada_mod(cond, w_ada, b_ada):
    tn = 1536
    n = 6 * D_MODEL
    out = pl.pallas_call(
        _ada_kernel,
        out_shape=jax.ShapeDtypeStruct((DEPTH, N_COND, n), F32),
        grid=(DEPTH, n // tn),
        in_specs=[
            pl.BlockSpec((N_COND, D_MODEL), lambda l, j: (0, 0)),
            pl.BlockSpec((None, D_MODEL, tn), lambda l, j: (l, 0, j)),
            pl.BlockSpec((None, 1, tn), lambda l, j: (l, 0, j)),
        ],
        out_specs=pl.BlockSpec((None, N_COND, tn), lambda l, j: (l, 0, j)),
        compiler_params=pltpu.CompilerParams(
            dimension_semantics=("parallel", "parallel"), vmem_limit_bytes=VMEM_LIMIT),
        name="ada_mod",
    )(cond, w_ada, b_ada.reshape(DEPTH, 1, n))
    return out.reshape(DEPTH, N_COND, 6, D_MODEL)


def _inproj_kernel(x_ref, g_ref, mod_ref, w_ref, o_ref, h_sc):
    @pl.when(pl.program_id(1) == 0)
    def _():
        h = _rms(x_ref[...], g_ref[...]) * (1.0 + mod_ref[1:2, :]) + mod_ref[0:1, :]
        h_sc[...] = h.astype(BF16)

    o_ref[...] = _dot(h_sc[...], w_ref[...])


def _inproj(x, g, mod, w, l):
    tm, tn = 1024, 1536
    return pl.pallas_call(
        _inproj_kernel,
        out_shape=jax.ShapeDtypeStruct((N_TOK, Z_COLS), F32),
        grid=(N_TOK // tm, Z_COLS // tn),
        in_specs=[
            pl.BlockSpec((tm, D_MODEL), lambda i, j: (i, 0)),
            pl.BlockSpec((1, D_MODEL), lambda i, j: (0, 0)),
            pl.BlockSpec((None, None, 6, D_MODEL), lambda i, j: (l, _cond_index(i, tm), 0, 0)),
            pl.BlockSpec((D_MODEL, tn), lambda i, j: (0, j)),
        ],
        out_specs=pl.BlockSpec((tm, tn), lambda i, j: (i, j)),
        scratch_shapes=[pltpu.VMEM((tm, D_MODEL), BF16)],
        compiler_params=pltpu.CompilerParams(
            dimension_semantics=("parallel", "arbitrary"), vmem_limit_bytes=VMEM_LIMIT),
        name="inproj",
    )(x, g, mod, w)


def _ctx_attn_kernel(sink_ref, na_ref, gq_ref, gkv_ref, oa_ref, ob_ref):
    scale = HEAD_DIM ** -0.5
    na = na_ref[...].astype(BF16)
    gq = gq_ref[...].astype(BF16)
    gkv = gkv_ref[...].astype(BF16)
    for h in range(NA_HEADS):
        q = na[:, COL_NA_Q + h * HEAD_DIM:COL_NA_Q + (h + 1) * HEAD_DIM]
        k = na[:, COL_NA_K + h * HEAD_DIM:COL_NA_K + (h + 1) * HEAD_DIM]
        v = na[:, COL_NA_V + h * HEAD_DIM:COL_NA_V + (h + 1) * HEAD_DIM]
        s = _dot_nt(q, k) * scale
        m = jnp.max(s, axis=-1, keepdims=True)
        p = jnp.exp(s - m)
        d = jnp.sum(p, axis=-1, keepdims=True)
        o = _dot(p.astype(BF16), v) / d
        oa_ref[:, h * HEAD_DIM:(h + 1) * HEAD_DIM] = o.astype(BF16)
    for h in range(GQA_Q_HEADS):
        kh = h // GQA_GROUP
        q = gq[:, h * HEAD_DIM:(h + 1) * HEAD_DIM]
        k = gkv[:, kh * HEAD_DIM:(kh + 1) * HEAD_DIM]
        v = gkv[:, GQA_KV_WIDTH + kh * HEAD_DIM:GQA_KV_WIDTH + (kh + 1) * HEAD_DIM]
        sink = sink_ref[h]
        s = _dot_nt(q, k) * scale
        m = jnp.maximum(jnp.max(s, axis=-1, keepdims=True), sink)
        p = jnp.exp(s - m)
        d = jnp.sum(p, axis=-1, keepdims=True) + jnp.exp(sink - m)
        o = _dot(p.astype(BF16), v) / d
        ob_ref[:, h * HEAD_DIM:(h + 1) * HEAD_DIM] = o.astype(BF16)


def _ctx_attn(z, sink):
    out = jax.ShapeDtypeStruct((N_CTX_TOK, NA_WIDTH), BF16)
    return pl.pallas_call(
        _ctx_attn_kernel,
        out_shape=(out, out),
        grid=(BATCH,),
        in_specs=[
            pl.BlockSpec(memory_space=pltpu.SMEM),
            pl.BlockSpec((SEQ, 3 * NA_WIDTH), lambda b: (b, 0)),
            pl.BlockSpec((SEQ, GQA_Q_WIDTH), lambda b: (b, COL_G_Q // GQA_Q_WIDTH)),
            pl.BlockSpec((SEQ, 2 * GQA_KV_WIDTH), lambda b: (b, COL_G_K // (2 * GQA_KV_WIDTH))),
        ],
        out_specs=(pl.BlockSpec((SEQ, NA_WIDTH), lambda b: (b, 0)),
                   pl.BlockSpec((SEQ, GQA_Q_WIDTH), lambda b: (b, 0))),
        compiler_params=pltpu.CompilerParams(
            dimension_semantics=("parallel",), vmem_limit_bytes=VMEM_LIMIT),
        name="ctx_attn",
    )(sink, z, z, z)


def _na_bias_tables(rpb):
    n_dc = 2 * NA_WIN_W - 1
    qc = np.arange(GRID_W)
    kc = np.arange(GRID_W)
    cstart = np.clip(qc - NA_WIN_W // 2, 0, GRID_W - NA_WIN_W)
    col_ok = (kc[None, :] >= cstart[:, None]) & (kc[None, :] < cstart[:, None] + NA_WIN_W)
    dc = np.clip(kc[None, :] - qc[:, None] + (NA_WIN_W - 1), 0, n_dc - 1)
    onehot = ((np.arange(n_dc)[:, None, None] == dc[None]) & col_ok[None]).astype(np.float32)
    band = jnp.einsum("hrd,dqk->hrqk", rpb.astype(F32), onehot, precision=lax.Precision.HIGHEST)
    band = jnp.where(col_ok, band, NEG_INF)
    band = jnp.pad(band, ((0, 0), (1, 1), (0, 0), (0, 0)), constant_values=NEG_INF)
    return jnp.concatenate([band[:, :-1], band[:, 1:]], axis=-1)


def _na_lat_kernel(q_ref, k_ref, v_ref, ck_ref, cv_ref, bias_ref, o_ref):
    i = pl.program_id(1)
    scale = HEAD_DIM ** -0.5
    nk = NA_KROWS * GRID_W
    kh = min(NA_WIN_H, GRID_ROWS)
    ks = jnp.clip(i * NA_QROWS - NA_WIN_H // 2, 0, GRID_ROWS - NA_KROWS)
    start = pl.multiple_of(ks * GRID_W, GRID_W)
    q = q_ref[...].astype(BF16)
    kw = k_ref[pl.ds(start, nk), :].astype(BF16)
    vw = v_ref[pl.ds(start, nk), :].astype(BF16)
    ck = ck_ref[...].astype(BF16)
    cv = cv_ref[...].astype(BF16)
    left = lax.broadcasted_iota(jnp.int32, (GRID_W, 2 * GRID_W), 1) < GRID_W
    pieces = []
    for qr in range(NA_QROWS):
        r = i * NA_QROWS + qr
        st = jnp.clip(r - kh // 2, 0, GRID_ROWS - kh)
        row = []
        for m in range(NA_KROWS // 2):
            k0 = ks + 2 * m
            plane = jnp.clip(k0 - r + NA_WIN_H, 0, 2 * NA_WIN_H - 1)
            ok0 = (k0 >= st) & (k0 < st + kh)
            ok1 = (k0 + 1 >= st) & (k0 + 1 < st + kh)
            row.append((plane, jnp.where(left, ok0.astype(jnp.int32), ok1.astype(jnp.int32)) > 0))
        pieces.append(row)
    for h in range(NA_HEADS):
        sl = slice(h * HEAD_DIM, (h + 1) * HEAD_DIM)
        qh = q[:, sl]
        bias = jnp.concatenate(
            [jnp.concatenate([jnp.where(ok, bias_ref[h, plane], NEG_INF) for plane, ok in row], axis=1)
             for row in pieces], axis=0)
        s_loc = _dot_nt(qh, kw[:, sl]) * scale + bias
        s_ctx = _dot_nt(qh, ck[:, sl]) * scale
        m = jnp.maximum(jnp.max(s_loc, axis=-1, keepdims=True), jnp.max(s_ctx, axis=-1, keepdims=True))
        p_loc = jnp.exp(s_loc - m)
        p_ctx = jnp.exp(s_ctx - m)
        d = jnp.sum(p_loc, axis=-1, keepdims=True) + jnp.sum(p_ctx, axis=-1, keepdims=True)
        o = (_dot(p_loc.astype(BF16), vw[:, sl]) + _dot(p_ctx.astype(BF16), cv[:, sl])) / d
        o_ref[:, sl] = o.astype(BF16)


def _na_lat(z, cache_na, bias, l):
    tq = NA_QROWS * GRID_W
    nsteps = GRID_ROWS // NA_QROWS
    lat_blk = N_CTX_TOK // DEC_SEQ
    return pl.pallas_call(
        _na_lat_kernel,
        out_shape=jax.ShapeDtypeStruct((N_LAT_TOK, NA_WIDTH), BF16),
        grid=(DEC_BATCH, nsteps),
        in_specs=[
            pl.BlockSpec((tq, NA_WIDTH), lambda b, i: (N_CTX_TOK // tq + b * nsteps + i, COL_NA_Q // NA_WIDTH)),
            pl.BlockSpec((DEC_SEQ, NA_WIDTH), lambda b, i: (lat_blk + b, COL_NA_K // NA_WIDTH)),
            pl.BlockSpec((DEC_SEQ, NA_WIDTH), lambda b, i: (lat_blk + b, COL_NA_V // NA_WIDTH)),
            pl.BlockSpec((None, None, None, PAST_LEN, NA_WIDTH), lambda b, i: (b, l, 0, 0, 0)),
            pl.BlockSpec((None, None, None, PAST_LEN, NA_WIDTH), lambda b, i: (b, l, 1, 0, 0)),
            pl.BlockSpec((NA_HEADS, 2 * NA_WIN_H, GRID_W, 2 * GRID_W), lambda b, i: (0, 0, 0, 0)),
        ],
        out_specs=pl.BlockSpec((tq, NA_WIDTH), lambda b, i: (b * nsteps + i, 0)),
        compiler_params=pltpu.CompilerParams(
            dimension_semantics=("parallel", "arbitrary"), vmem_limit_bytes=VMEM_LIMIT),
        name="na_lat",
    )(z, z, z, cache_na, cache_na, bias)


def _rope_tables():
    nf = HEAD_DIM // 4
    t = jnp.arange(DEC_SEQ)
    pos = jnp.stack([t // GRID_W, t % GRID_W], axis=-1).astype(F32)
    inv = ROPE_THETA ** (-jnp.arange(nf, dtype=F32) / nf)
    ang = pos[:, :, None] * inv
    cos = jnp.cos(ang)
    sin = jnp.sin(ang)
    cos_d = jnp.stack([cos, cos], axis=2).reshape(DEC_SEQ, HEAD_DIM)
    sin_d = jnp.stack([-sin, sin], axis=2).reshape(DEC_SEQ, HEAD_DIM)
    return cos_d, sin_d


def _rope(x, cos, sin_signed):
    n = x.shape[-1]
    nf = HEAD_DIM // 4
    lane = lax.broadcasted_iota(jnp.int32, x.shape, 1)
    first_half = (lane // nf) % 2 == 0
    partner = jnp.where(first_half, pltpu.roll(x, n - nf, 1), pltpu.roll(x, nf, 1))
    return x * cos + partner * sin_signed


def _swa_lat_kernel(sink_ref, q_ref, kv_ref, cq_ref, sq_ref, ckt_ref, skt_ref, ck_ref, cv_ref, o_ref, k_sc, v_sc):
    n = pl.program_id(1)
    scale = HEAD_DIM ** -0.5
    nwin = 3 * SWA_BLOCK

    @pl.when(n == 0)
    def _():
        kv = kv_ref[...]
        k_sc[...] = _rope(kv[:, :GQA_KV_WIDTH], ckt_ref[...], skt_ref[...]).astype(BF16)
        v_sc[...] = kv[:, GQA_KV_WIDTH:].astype(BF16)

    q = _rope(q_ref[...], cq_ref[...], sq_ref[...]).astype(BF16)
    start = pl.multiple_of(jnp.clip((n - 1) * SWA_BLOCK, 0, DEC_SEQ - nwin), SWA_BLOCK)
    kw = k_sc[pl.ds(start, nwin), :]
    vw = v_sc[pl.ds(start, nwin), :]
    ck = ck_ref[...].astype(BF16)
    cv = cv_ref[...].astype(BF16)
    rows = GQA_GROUP * SWA_BLOCK
    row = lax.broadcasted_iota(jnp.int32, (rows, nwin), 0)
    col = lax.broadcasted_iota(jnp.int32, (rows, nwin), 1)
    qpos = n * SWA_BLOCK + row % SWA_BLOCK
    kpos = start + col
    ok = jnp.abs(qpos - kpos) <= SWA_WINDOW
    grp = lax.broadcasted_iota(jnp.int32, (rows, 1), 0) // SWA_BLOCK
    for kh in range(GQA_KV_HEADS):
        sl = slice(kh * HEAD_DIM, (kh + 1) * HEAD_DIM)
        q4 = jnp.concatenate(
            [q[:, (kh * GQA_GROUP + g) * HEAD_DIM:(kh * GQA_GROUP + g + 1) * HEAD_DIM] for g in range(GQA_GROUP)],
            axis=0)
        sink = jnp.zeros((rows, 1), F32)
        for g in range(GQA_GROUP):
            sink = jnp.where(grp == g, sink_ref[kh * GQA_GROUP + g], sink)
        s_loc = jnp.where(ok, _dot_nt(q4, kw[:, sl]) * scale, NEG_INF)
        s_ctx = _dot_nt(q4, ck[:, sl]) * scale
        m = jnp.maximum(jnp.maximum(jnp.max(s_loc, axis=-1, keepdims=True),
                                    jnp.max(s_ctx, axis=-1, keepdims=True)), sink)
        p_loc = jnp.exp(s_loc - m)
        p_ctx = jnp.exp(s_ctx - m)
        d = (jnp.sum(p_loc, axis=-1, keepdims=True) + jnp.sum(p_ctx, axis=-1, keepdims=True)
             + jnp.exp(sink - m))
        o4 = (_dot(p_loc.astype(BF16), vw[:, sl]) + _dot(p_ctx.astype(BF16), cv[:, sl])) / d
        for g in range(GQA_GROUP):
            h = kh * GQA_GROUP + g
            o_ref[:, h * HEAD_DIM:(h + 1) * HEAD_DIM] = o4[g * SWA_BLOCK:(g + 1) * SWA_BLOCK].astype(BF16)


def _swa_lat(z, cache_gqa, sink, cos_q, sin_q, cos_k, sin_k, l):
    nb = DEC_SEQ // SWA_BLOCK
    lat_blk = N_CTX_TOK // DEC_SEQ
    return pl.pallas_call(
        _swa_lat_kernel,
        out_shape=jax.ShapeDtypeStruct((N_LAT_TOK, GQA_Q_WIDTH), BF16),
        grid=(DEC_BATCH, nb),
        in_specs=[
            pl.BlockSpec(memory_space=pltpu.SMEM),
            pl.BlockSpec((SWA_BLOCK, GQA_Q_WIDTH),
                         lambda b, n: (N_CTX_TOK // SWA_BLOCK + b * nb + n, COL_G_Q // GQA_Q_WIDTH)),
            pl.BlockSpec((DEC_SEQ, 2 * GQA_KV_WIDTH), lambda b, n: (lat_blk + b, COL_G_K // (2 * GQA_KV_WIDTH))),
            pl.BlockSpec((SWA_BLOCK, GQA_Q_WIDTH), lambda b, n: (n, 0)),
            pl.BlockSpec((SWA_BLOCK, GQA_Q_WIDTH), lambda b, n: (n, 0)),
            pl.BlockSpec((DEC_SEQ, GQA_KV_WIDTH), lambda b, n: (0, 0)),
            pl.BlockSpec((DEC_SEQ, GQA_KV_WIDTH), lambda b, n: (0, 0)),
            pl.BlockSpec((None, None, None, PAST_LEN, GQA_KV_WIDTH), lambda b, n: (b, l, 0, 0, 0)),
            pl.BlockSpec((None, None, None, PAST_LEN, GQA_KV_WIDTH), lambda b, n: (b, l, 1, 0, 0)),
        ],
        out_specs=pl.BlockSpec((SWA_BLOCK, GQA_Q_WIDTH), lambda b, n: (b * nb + n, 0)),
        scratch_shapes=[pltpu.VMEM((DEC_SEQ, GQA_KV_WIDTH), BF16), pltpu.VMEM((DEC_SEQ, GQA_KV_WIDTH), BF16)],
        compiler_params=pltpu.CompilerParams(
            dimension_semantics=("parallel", "arbitrary"), vmem_limit_bytes=VMEM_LIMIT),
        name="swa_lat",
    )(sink, z, z, cos_q, sin_q, cos_k, sin_k, cache_gqa, cache_gqa)


def _s5_operators(a_re, a_im, log_dt, b_re, b_im, c_re, c_im):
    T, G, N, C, P = S5_T, S5_GROUPS, S5_STATE, S5_GROUP_CH, S5_PAIRS
    tau = jnp.arange(T + 1, dtype=F32)
    eye2 = jnp.eye(2, dtype=F32)
    out = {k: [] for k in ("bs", "m", "cre", "cim", "apr", "api")}

    def pairs(x):
        return x.reshape((P, 2) + x.shape[1:])

    def diag(x, axis):
        e = eye2.reshape((1, 2) + (1,) * (axis - 1) + (2,) + (1,) * (x.ndim - axis - 1))
        return jnp.expand_dims(x, axis + 1) * e

    for d in range(2):
        A = lax.complex(a_re[d].astype(F32), a_im[d].astype(F32))
        dt = jnp.exp(log_dt[d].astype(F32))[:, None]
        a_bar = jnp.exp(A * dt)
        pw = jnp.exp((A * dt)[None] * tau[:, None, None])
        b_bar = ((a_bar - 1.0) / A)[..., None] * lax.complex(b_re[d].astype(F32), b_im[d].astype(F32))
        c_mat = lax.complex(c_re[d].astype(F32), c_im[d].astype(F32))
        kern = jnp.einsum("gon,tgn,gni->gtoi", c_mat, pw[:T], b_bar, precision=lax.Precision.HIGHEST).real
        k0 = diag(pairs(kern).transpose(0, 1, 4, 2, 3), 3).reshape(P, 2 * C, T * 2 * C)
        zero = jnp.zeros_like(k0)
        if d == 0:
            p_in = pw[:T][::-1]
            p_out = pw[1:T + 1]
            kp = jnp.concatenate([zero, k0], axis=2)
            m = [kp[:, :, (T - s) * 2 * C:(2 * T - s) * 2 * C] for s in range(T)]
        else:
            p_in = pw[:T]
            p_out = pw[1:T + 1][::-1]
            kp = jnp.concatenate([k0.reshape(P, 2 * C, T, 2 * C)[:, :, ::-1].reshape(k0.shape), zero], axis=2)
            m = [kp[:, :, (T - 1 - s) * 2 * C:(2 * T - 1 - s) * 2 * C] for s in range(T)]
        out["m"].append(jnp.concatenate(m, axis=1))
        pin = p_in.reshape(T, P, 2 * N).transpose(1, 0, 2)[:, :, None, :]
        bx = diag(pairs(b_bar.transpose(0, 2, 1)), 2).reshape(P, 1, 2 * C, 2 * N)
        bs = pin * bx
        out["bs"].append(jnp.concatenate([bs.real, bs.imag], axis=-1).reshape(P, 2 * T * C, 4 * N))
        cx = diag(pairs(c_mat.transpose(0, 2, 1)), 2).reshape(P, 2 * N, 2 * C)
        pout = p_out.reshape(T, P, 2 * N).transpose(1, 2, 0)
        co = jnp.tile(cx, (1, 1, T)) * jnp.repeat(pout, 2 * C, axis=2)
        out["cre"].append(co.real)
        out["cim"].append(-co.imag)
        pw2 = jnp.exp((A * dt)[None] * (T * 2.0 ** jnp.arange(S5_NPOW, dtype=F32))[:, None, None])
        out["apr"].append(pw2.real.reshape(S5_NPOW, G * N))
        out["api"].append(pw2.imag.reshape(S5_NPOW, G * N))
    return {k: jnp.stack(v) for k, v in out.items()}


def _s5_kernel(u_ref, bs_ref, m_ref, cre_ref, cim_ref, apr_ref, api_ref, h0_ref, y_ref, fin_ref, fin_sc):
    t = pl.program_id(1)
    nctx = N_CTX_TOK // S5_BLOCK_TOK
    refs = (u_ref, bs_ref, m_ref, cre_ref, cim_ref, apr_ref, api_ref, h0_ref, y_ref, fin_ref, fin_sc)

    @pl.when(t < nctx)
    def _():
        _s5_block(*refs, t, kseq=SEQ // S5_T, has_h0=False)

    @pl.when(t >= nctx)
    def _():
        _s5_block(*refs, t - nctx, kseq=DEC_SEQ // S5_T, has_h0=True)


def _s5_block(u_ref, bs_ref, m_ref, cre_ref, cim_ref, apr_ref, api_ref, h0_ref, y_ref, fin_ref, fin_sc, t, *,
              kseq, has_h0):
    rows, half, pc = S5_ROWS, 2 * S5_STATE, 2 * S5_GROUP_CH
    k = lax.broadcasted_iota(jnp.int32, (rows, half), 0) % kseq
    shifts = [1 << i for i in range(kseq.bit_length() - 1)]
    xs = [u_ref[pl.ds(s, rows, stride=S5_T), :].astype(BF16) for s in range(S5_T)]
    ys = []
    for p in range(S5_OCT_PAIRS):
        lanes = slice(p * half, (p + 1) * half)
        xp = jnp.concatenate([x[:, p * pc:(p + 1) * pc] for x in xs], axis=1)
        acc = None
        for d in range(2):
            inc = _dot(xp, bs_ref[d, p])
            sr, si = inc[:, :half], inc[:, half:]
            if has_h0:
                h0r = h0_ref[d, 0, t, :, lanes]
                h0i = h0_ref[d, 1, t, :, lanes]
                ar, ai = apr_ref[d, 0:1, lanes], api_ref[d, 0:1, lanes]
                first = (k == 0) if d == 0 else (k == kseq - 1)
                sr = sr + jnp.where(first, ar * h0r - ai * h0i, 0.0)
                si = si + jnp.where(first, ar * h0i + ai * h0r, 0.0)
            else:
                h0r = h0i = 0.0
            for i, sh in enumerate(shifts):
                ar, ai = apr_ref[d, i:i + 1, lanes], api_ref[d, i:i + 1, lanes]
                ok = (k >= sh) if d == 0 else (k < kseq - sh)
                amt = sh if d == 0 else rows - sh
                rr = jnp.where(ok, pltpu.roll(sr, amt, 0), 0.0)
                ri = jnp.where(ok, pltpu.roll(si, amt, 0), 0.0)
                sr, si = sr + ar * rr - ai * ri, si + ar * ri + ai * rr
            inner = (k >= 1) if d == 0 else (k < kseq - 1)
            amt = 1 if d == 0 else rows - 1
            hr = jnp.where(inner, pltpu.roll(sr, amt, 0), h0r)
            hi = jnp.where(inner, pltpu.roll(si, amt, 0), h0i)
            yd = (_dot(xp, m_ref[d, p]) + _dot(hr.astype(BF16), cre_ref[d, p])
                  + _dot(hi.astype(BF16), cim_ref[d, p]))
            acc = yd if acc is None else acc + yd
            if not has_h0:
                last = kseq - 1 if d == 0 else 0
                fin_sc[0] = sr
                fin_sc[1] = si
                fin_ref[d, 0, :, lanes] = fin_sc[0, pl.ds(last, rows // kseq, stride=kseq), :]
                fin_ref[d, 1, :, lanes] = fin_sc[1, pl.ds(last, rows // kseq, stride=kseq), :]
        ys.append(acc)
    for j in range(S5_T):
        y_ref[pl.ds(j, rows, stride=S5_T), :] = jnp.concatenate([y[:, j * pc:(j + 1) * pc] for y in ys], axis=1)


def _s5_scan(z, ops, h0):
    nblk, nctx = N_TOK // S5_BLOCK_TOK, N_CTX_TOK // S5_BLOCK_TOK
    noct = S5_PAIRS // S5_OCT_PAIRS
    half, gn = 2 * S5_STATE, S5_GROUPS * S5_STATE
    pw = 2 * S5_T * S5_GROUP_CH
    nseq = S5_BLOCK_TOK // SEQ
    wspec = lambda r, c: pl.BlockSpec((2, S5_OCT_PAIRS, r, c), lambda q, t: (0, q, 0, 0))
    aspec = pl.BlockSpec((2, S5_NPOW, S5_OCT_PAIRS * half), lambda q, t: (0, 0, q))
    return pl.pallas_call(
        _s5_kernel,
        grid=(noct, nblk),
        in_specs=[
            pl.BlockSpec((S5_BLOCK_TOK, 128), lambda q, t: (t, COL_U // 128 + q)),
            wspec(pw, 2 * half), wspec(pw, pw), wspec(half, pw), wspec(half, pw), aspec, aspec,
            pl.BlockSpec((2, 2, DEC_BATCH, 1, S5_OCT_PAIRS * half), lambda q, t: (0, 0, 0, 0, q)),
        ],
        out_shape=(jax.ShapeDtypeStruct((N_TOK, S5_CH), F32), jax.ShapeDtypeStruct((2, 2, BATCH, gn), F32)),
        out_specs=(pl.BlockSpec((S5_BLOCK_TOK, 128), lambda q, t: (t, q)),
                   pl.BlockSpec((2, 2, nseq, S5_OCT_PAIRS * half),
                                lambda q, t: (0, 0, jnp.minimum(t, nctx - 1), q))),
        scratch_shapes=[pltpu.VMEM((2, S5_ROWS, half), F32)],
        compiler_params=pltpu.CompilerParams(
            dimension_semantics=("parallel", "arbitrary"), vmem_limit_bytes=VMEM_LIMIT),
        name="s5_scan",
    )(z, ops["bs"], ops["m"], ops["cre"], ops["cim"], ops["apr"], ops["api"],
      h0.reshape(2, 2, DEC_BATCH, 1, gn))


def _route(logits):
    lane = lax.broadcasted_iota(jnp.int32, logits.shape, 1)
    big = jnp.int32(1 << 20)
    is_g = (lane >= MOE_EXPERTS) & (lane < MOE_EXPERTS + MOE_GROUPS)
    lg = jnp.where(is_g, logits, NEG_INF)
    gmax = jnp.max(lg, axis=-1, keepdims=True)
    gsel = jnp.min(jnp.where(is_g & (lg == gmax), lane, big), axis=-1, keepdims=True) - MOE_EXPERTS
    p_group = 1.0 / jnp.sum(jnp.where(is_g, jnp.exp(lg - gmax), 0.0), axis=-1, keepdims=True)
    in_grp = (lane < MOE_EXPERTS) & (lane // MOE_EXPERTS_PER_GROUP == gsel)
    le = jnp.where(in_grp, logits, NEG_INF)
    v1 = jnp.max(le, axis=-1, keepdims=True)
    i1 = jnp.min(jnp.where(in_grp & (le == v1), lane, big), axis=-1, keepdims=True)
    rest = in_grp & (lane != i1)
    le2 = jnp.where(rest, logits, NEG_INF)
    v2 = jnp.max(le2, axis=-1, keepdims=True)
    i2 = jnp.min(jnp.where(rest & (le2 == v2), lane, big), axis=-1, keepdims=True)
    e2 = jnp.exp(v2 - v1)
    w1 = 1.0 / (1.0 + e2)
    w2 = e2 / (1.0 + e2)
    comb = jnp.where(lane == i1, w1 * p_group, 0.0) + jnp.where(lane == i2, w2 * p_group, 0.0)
    return jnp.where(lane == MOE_EXPERTS, gsel.astype(F32), comb)


def _merge_kernel(x_ref, oa_ref, ob_ref, u_ref, yc_ref, gates_ref, mod_ref, d_ref, wglu_ref, wa_ref, wb_ref,
                  wc_ref, wout_ref, g2_ref, wrh_ref, wrl_ref, br_ref, x1_ref, hx_ref):
    y = u_ref[...] * d_ref[...] + yc_ref[...]
    y = y * (0.5 * (1.0 + jnp.tanh(math.sqrt(2.0 / math.pi) * (y + 0.044715 * (y * y * y)))))
    oc = y * jax.nn.sigmoid(_dot(y.astype(BF16), wglu_ref[...]))
    gate = jax.nn.sigmoid(gates_ref[...])
    merged = (gate[:, :D_MODEL] * _dot(oa_ref[...], wa_ref[...])
              + gate[:, D_MODEL:2 * D_MODEL] * _dot(ob_ref[...], wb_ref[...])
              + gate[:, 2 * D_MODEL:] * _dot(oc.astype(BF16), wc_ref[...]))
    x1 = x_ref[...] + mod_ref[2:3, :] * _dot(merged.astype(BF16), wout_ref[...])
    x1_ref[...] = x1
    h2 = _rms(x1, g2_ref[...]) * (1.0 + mod_ref[4:5, :]) + mod_ref[3:4, :]
    h_hi, h_lo = _split_bf16(h2)
    logits = _dot(h_hi, wrh_ref[...]) + _dot(h_lo, wrh_ref[...]) + _dot(h_hi, wrl_ref[...]) + br_ref[...]
    half = D_MODEL // 2
    hb = h_hi.astype(F32)
    word = (pltpu.bitcast(hb[:, :half], jnp.uint32) & jnp.uint32(0xFFFF0000)) | (
        pltpu.bitcast(hb[:, half:], jnp.uint32) >> 16)
    hx_ref[:, :half] = word
    hx_ref[:, half:] = pltpu.bitcast(_route(logits), jnp.uint32)


def _merge(x, oa, ob, z, yc, mod, d_s5, wglu, wa, wb, wc, wout, g2, wr_hi, wr_lo, br, l):
    tm = 512
    full = lambda r, c: pl.BlockSpec((r, c), lambda i: (0, 0))
    return pl.pallas_call(
        _merge_kernel,
        out_shape=(jax.ShapeDtypeStruct((N_TOK, D_MODEL), F32),
                   jax.ShapeDtypeStruct((N_TOK, MOE_ROW_WORDS), jnp.uint32)),
        grid=(N_TOK // tm,),
        in_specs=[
            pl.BlockSpec((tm, D_MODEL), lambda i: (i, 0)),
            pl.BlockSpec((tm, NA_WIDTH), lambda i: (i, 0)),
            pl.BlockSpec((tm, GQA_Q_WIDTH), lambda i: (i, 0)),
            pl.BlockSpec((tm, S5_CH), lambda i: (i, COL_U // S5_CH)),
            pl.BlockSpec((tm, S5_CH), lambda i: (i, 0)),
            pl.BlockSpec((tm, N_BRANCH * D_MODEL), lambda i: (i, COL_GATES // (N_BRANCH * D_MODEL))),
            pl.BlockSpec((None, None, 6, D_MODEL), lambda i: (l, _cond_index(i, tm), 0, 0)),
            full(1, S5_CH), full(S5_CH, S5_CH), full(NA_WIDTH, D_MODEL), full(GQA_Q_WIDTH, D_MODEL),
            full(S5_CH, D_MODEL), full(D_MODEL, D_MODEL), full(1, D_MODEL),
            full(D_MODEL, 128), full(D_MODEL, 128), full(1, 128),
        ],
        out_specs=(pl.BlockSpec((tm, D_MODEL), lambda i: (i, 0)),
                   pl.BlockSpec((tm, MOE_ROW_WORDS), lambda i: (i, 0))),
        compiler_params=pltpu.CompilerParams(
            dimension_semantics=("parallel",), vmem_limit_bytes=VMEM_LIMIT),
        name="merge",
    )(x, oa, ob, z, yc, z, mod, d_s5, wglu, wa, wb, wc, wout, g2, wr_hi, wr_lo, br)


def _moe_plan(gsel):
    onehot = (gsel[:, None] == jnp.arange(MOE_GROUPS)[None, :]).astype(jnp.int32)
    rank = jnp.sum((jnp.cumsum(onehot, axis=0) - onehot) * onehot, axis=1)
    count = jnp.sum(onehot, axis=0)
    padded = (count + MOE_TILE - 1) // MOE_TILE * MOE_TILE
    end = jnp.cumsum(padded)
    pos = jnp.sum(onehot * (end - padded)[None, :], axis=1) + rank
    tile_row = jnp.arange(MOE_ROWS // MOE_TILE) * MOE_TILE
    tile_group = jnp.minimum(jnp.sum((tile_row[:, None] >= end[None, :]).astype(jnp.int32), axis=1),
                             MOE_GROUPS - 1)
    return pos.astype(jnp.int32), tile_group.astype(jnp.int32), (end[-1:] // MOE_TILE).astype(jnp.int32)


def _dispatch_kernel(pos_ref, hx_ref, init_ref, out_ref, sem):
    del init_ref
    base = pl.program_id(0) * MOE_TOK_BLOCK

    def row_copy(r):
        return pltpu.make_async_copy(hx_ref.at[pl.ds(r, 1), :], out_ref.at[pl.ds(pos_ref[base + r], 1), :], sem)

    def start(r, c):
        row_copy(r).start()
        return c

    def wait(r, c):
        row_copy(r).wait()
        return c

    lax.fori_loop(0, MOE_TOK_BLOCK, start, 0)
    lax.fori_loop(0, MOE_TOK_BLOCK, wait, 0)


def _dispatch(hx, pos):
    return pl.pallas_call(
        _dispatch_kernel,
        grid_spec=pltpu.PrefetchScalarGridSpec(
            num_scalar_prefetch=1,
            grid=(N_TOK // MOE_TOK_BLOCK,),
            in_specs=[pl.BlockSpec((MOE_TOK_BLOCK, MOE_ROW_WORDS), lambda i, pos: (i, 0)),
                      pl.BlockSpec(memory_space=pl.ANY)],
            out_specs=pl.BlockSpec(memory_space=pl.ANY),
            scratch_shapes=[pltpu.SemaphoreType.DMA]),
        out_shape=jax.ShapeDtypeStruct((MOE_ROWS, MOE_ROW_WORDS), jnp.uint32),
        input_output_aliases={2: 0},
        compiler_params=pltpu.CompilerParams(dimension_semantics=("arbitrary",)),
        name="moe_dispatch",
    )(pos, hx, jnp.zeros((MOE_ROWS, MOE_ROW_WORDS), jnp.uint32))


def _experts_kernel(tg_ref, nu_ref, hx_ref, wg_ref, wu_ref, wd_ref, y_ref):
    t = pl.program_id(0)

    @pl.when(t >= nu_ref[0])
    def _():
        y_ref[...] = jnp.zeros_like(y_ref)

    @pl.when(t < nu_ref[0])
    def _():
        half = D_MODEL // 2
        word = hx_ref[:, :half]
        h = jnp.concatenate([pltpu.bitcast(word & jnp.uint32(0xFFFF0000), F32).astype(BF16),
                             pltpu.bitcast(word << 16, F32).astype(BF16)], axis=1)
        comb = pltpu.bitcast(hx_ref[:, half:], F32)
        lane = lax.broadcasted_iota(jnp.int32, comb.shape, 1)
        first = tg_ref[t] * MOE_EXPERTS_PER_GROUP
        acc = None
        for e in range(MOE_EXPERTS_PER_GROUP):
            a = _dot(h, wg_ref[e])
            b = _dot(h, wu_ref[e])
            ce = jnp.sum(jnp.where(lane == first + e, comb, 0.0), axis=-1, keepdims=True)
            act = (a * jax.nn.sigmoid(a)) * b * ce
            c = _dot(act.astype(BF16), wd_ref[e])
            acc = c if acc is None else acc + c
        y_ref[...] = acc


def _experts(hs, tile_group, n_used, wg, wu, wd):
    wspec = lambda r, c: pl.BlockSpec((None, MOE_EXPERTS_PER_GROUP, r, c), lambda t, tg, nu: (tg[t], 0, 0, 0))
    return pl.pallas_call(
        _experts_kernel,
        grid_spec=pltpu.PrefetchScalarGridSpec(
            num_scalar_prefetch=2,
            grid=(MOE_ROWS // MOE_TILE,),
            in_specs=[pl.BlockSpec((MOE_TILE, MOE_ROW_WORDS), lambda t, tg, nu: (t, 0)),
                      wspec(D_MODEL, EXPERT_FF), wspec(D_MODEL, EXPERT_FF), wspec(EXPERT_FF, D_MODEL)],
            out_specs=pl.BlockSpec((MOE_TILE, D_MODEL), lambda t, tg, nu: (t, 0))),
        out_shape=jax.ShapeDtypeStruct((MOE_ROWS, D_MODEL), F32),
        compiler_params=pltpu.CompilerParams(
            dimension_semantics=("arbitrary",), vmem_limit_bytes=VMEM_LIMIT),
        name="moe_experts",
    )(tile_group, n_used, hs, wg, wu, wd)


def _combine_kernel(pos_ref, y_ref, x_ref, mod_ref, fg_ref, x2_ref, *rest, final):
    buf, sem = rest[-2:]
    base = pl.program_id(0) * MOE_TOK_BLOCK

    def row_copy(r):
        return pltpu.make_async_copy(y_ref.at[pl.ds(pos_ref[base + r], 1), :], buf.at[pl.ds(r, 1), :], sem)

    def start(r, c):
        row_copy(r).start()
        return c

    def wait(r, c):
        row_copy(r).wait()
        return c

    lax.fori_loop(0, MOE_TOK_BLOCK, start, 0)
    lax.fori_loop(0, MOE_TOK_BLOCK, wait, 0)
    x2 = x_ref[...] + mod_ref[5:6, :] * buf[...]
    x2_ref[...] = x2
    if final:
        rest[0][...] = _rms(x2, fg_ref[...])


def _combine(ys, pos, x1, mod, fg, l, final):
    tm = MOE_TOK_BLOCK
    tok = pl.BlockSpec((tm, D_MODEL), lambda i, pos: (i, 0))
    out = jax.ShapeDtypeStruct((N_TOK, D_MODEL), F32)
    return pl.pallas_call(
        functools.partial(_combine_kernel, final=final),
        grid_spec=pltpu.PrefetchScalarGridSpec(
            num_scalar_prefetch=1,
            grid=(N_TOK // tm,),
            in_specs=[pl.BlockSpec(memory_space=pl.ANY), tok,
                      pl.BlockSpec((None, None, 6, D_MODEL), lambda i, pos: (l, _cond_index(i, tm), 0, 0)),
                      pl.BlockSpec((1, D_MODEL), lambda i, pos: (0, 0))],
            out_specs=(tok, tok) if final else tok,
            scratch_shapes=[pltpu.VMEM((tm, D_MODEL), F32), pltpu.SemaphoreType.DMA]),
        out_shape=(out, out) if final else out,
        compiler_params=pltpu.CompilerParams(
            dimension_semantics=("arbitrary",), vmem_limit_bytes=VMEM_LIMIT),
        name="moe_combine",
    )(pos, ys, x1, mod, fg)


def _pack_w_in(w):
    qkv = w[:, :COL_G_V + GQA_KV_WIDTH]
    u = w[:, COL_G_V + GQA_KV_WIDTH:COL_G_V + GQA_KV_WIDTH + S5_CH]
    gates = w[:, COL_G_V + GQA_KV_WIDTH + S5_CH:]
    pad = jnp.zeros((D_MODEL, COL_U - (COL_G_V + GQA_KV_WIDTH)), w.dtype)
    return jnp.concatenate([qkv, pad, u, gates], axis=1).astype(BF16)


def kernel(x_prompt, x_sample, cache_na_kv, cache_gqa_kv, state_ssm, c, c_ctx, norm_g, w_ada, b_ada, w_in, na_rpb, gqa_sink, s5_a_re, s5_a_im, s5_log_dt, s5_b_re, s5_b_im, s5_c_re, s5_c_im, s5_d, s5_w_glu, w_branch_a, w_branch_b, w_branch_c, w_out, moe_w_group, moe_b_group, moe_w_expert, moe_b_expert, moe_w_gate, moe_w_up, moe_w_down, final_g):
    cond = jnp.zeros((N_COND, D_MODEL), F32).at[0].set(c_ctx.astype(F32)).at[1:1 + DEC_BATCH].set(c.astype(F32))
    mod = _ada_mod(cond, w_ada.astype(F32), b_ada.astype(F32))

    x = jnp.concatenate([x_prompt.reshape(N_CTX_TOK, D_MODEL), x_sample.reshape(N_LAT_TOK, D_MODEL)], axis=0)
    cache_na = cache_na_kv.reshape(DEC_BATCH, DEPTH, 2, PAST_LEN, NA_WIDTH)
    cache_gqa = cache_gqa_kv.reshape(DEC_BATCH, DEPTH, 2, PAST_LEN, GQA_KV_WIDTH)
    cos_d, sin_d = _rope_tables()
    cos_q, sin_q = jnp.tile(cos_d, (1, GQA_Q_HEADS)), jnp.tile(sin_d, (1, GQA_Q_HEADS))
    cos_k, sin_k = jnp.tile(cos_d, (1, GQA_KV_HEADS)), jnp.tile(sin_d, (1, GQA_KV_HEADS))
    gn = S5_GROUPS * S5_STATE
    fg = final_g.astype(F32).reshape(1, D_MODEL)

    na_list, gqa_list, ssm_list = [], [], []
    y = None
    for l in range(DEPTH):
        z = _inproj(x, norm_g[l, 0].astype(F32).reshape(1, D_MODEL), mod, _pack_w_in(w_in[l]), l)

        zc = z[:N_CTX_TOK]
        na_list.append(jnp.stack([zc[:, COL_NA_K:COL_NA_K + NA_WIDTH].reshape(BATCH, SEQ, NA_HEADS, HEAD_DIM),
                                  zc[:, COL_NA_V:COL_NA_V + NA_WIDTH].reshape(BATCH, SEQ, NA_HEADS, HEAD_DIM)],
                                 axis=1))
        gqa_list.append(jnp.stack(
            [zc[:, COL_G_K:COL_G_K + GQA_KV_WIDTH].reshape(BATCH, SEQ, GQA_KV_HEADS, HEAD_DIM),
             zc[:, COL_G_V:COL_G_V + GQA_KV_WIDTH].reshape(BATCH, SEQ, GQA_KV_HEADS, HEAD_DIM)], axis=1))

        sink = gqa_sink[l].astype(F32)
        oa_c, ob_c = _ctx_attn(z, sink)
        oa_l = _na_lat(z, cache_na, _na_bias_tables(na_rpb[l]), l)
        ob_l = _swa_lat(z, cache_gqa, sink, cos_q, sin_q, cos_k, sin_k, l)

        ops = _s5_operators(s5_a_re[l], s5_a_im[l], s5_log_dt[l], s5_b_re[l], s5_b_im[l], s5_c_re[l], s5_c_im[l])
        ops = {k: (v.astype(BF16) if k in ("bs", "m", "cre", "cim") else v) for k, v in ops.items()}
        h0 = state_ssm[:, l].astype(F32).reshape(DEC_BATCH, 2, 2, gn).transpose(1, 2, 0, 3)
        yc, fin = _s5_scan(z, ops, h0)
        ssm_list.append(fin.transpose(2, 0, 1, 3).reshape(BATCH, 2, 2, S5_GROUPS, S5_STATE).astype(x_prompt.dtype))

        oa = jnp.concatenate([oa_c, oa_l], axis=0)
        ob = jnp.concatenate([ob_c, ob_l], axis=0)
        wr = jnp.zeros((D_MODEL, 128), F32)
        wr = wr.at[:, :MOE_EXPERTS].set(moe_w_expert[l].astype(F32))
        wr = wr.at[:, MOE_EXPERTS:MOE_EXPERTS + MOE_GROUPS].set(moe_w_group[l].astype(F32))
        br = jnp.zeros((1, 128), F32)
        br = br.at[0, :MOE_EXPERTS].set(moe_b_expert[l].astype(F32))
        br = br.at[0, MOE_EXPERTS:MOE_EXPERTS + MOE_GROUPS].set(moe_b_group[l].astype(F32))
        wr_hi, wr_lo = _split_bf16(wr)
        x1, hx = _merge(
            x, oa, ob, z, yc, mod, s5_d[l].astype(F32).reshape(1, S5_CH), s5_w_glu[l].astype(BF16),
            w_branch_a[l].astype(BF16), w_branch_b[l].astype(BF16), w_branch_c[l].astype(BF16),
            w_out[l].astype(BF16), norm_g[l, 1].astype(F32).reshape(1, D_MODEL), wr_hi, wr_lo, br, l)
        gsel = lax.bitcast_convert_type(hx[:, D_MODEL // 2 + MOE_EXPERTS], F32).astype(jnp.int32)
        pos, tile_group, n_used = _moe_plan(gsel)
        grouped = lambda w: w.astype(BF16).reshape((MOE_GROUPS, MOE_EXPERTS_PER_GROUP) + w.shape[1:])
        ys = _experts(_dispatch(hx, pos), tile_group, n_used, grouped(moe_w_gate[l]), grouped(moe_w_up[l]),
                      grouped(moe_w_down[l]))
        if l < DEPTH - 1:
            x = _combine(ys, pos, x1, mod, fg, l, False)
        else:
            x, y = _combine(ys, pos, x1, mod, fg, l, True)

    y_prompt = y[:N_CTX_TOK].reshape(BATCH, SEQ, D_MODEL)
    y_sample = y[N_CTX_TOK:].reshape(DEC_BATCH, DEC_SEQ, D_MODEL)
    return (y_prompt, y_sample, jnp.stack(na_list, axis=1), jnp.stack(gqa_list, axis=1),
            jnp.stack(ssm_list, axis=1))
```

```python
import functools
import math

import numpy as np
import jax
import jax.numpy as jnp
from jax import lax
from jax.experimental import pallas as pl
from jax.experimental.pallas import tpu as pltpu

F32 = jnp.float32
BF16 = jnp.bfloat16

D_MODEL = 1024
BATCH = 32
SEQ = 256
DEPTH = 2
DEC_BATCH = 4
DEC_SEQ = 2048
PAST_LEN = 512
GRID_W = 64
GRID_ROWS = DEC_SEQ // GRID_W
HEAD_DIM = 64
NA_HEADS = 8
NA_WIN_H = 8
NA_WIN_W = 16
GQA_Q_HEADS = 8
GQA_KV_HEADS = 2
GQA_GROUP = GQA_Q_HEADS // GQA_KV_HEADS
SWA_WINDOW = 128
SWA_BLOCK = 128
ROPE_THETA = 10000.0
S5_CH = 512
S5_GROUP_CH = 16
S5_GROUPS = S5_CH // S5_GROUP_CH
S5_STATE = 64
N_BRANCH = 3
NA_WIDTH = NA_HEADS * HEAD_DIM
GQA_Q_WIDTH = GQA_Q_HEADS * HEAD_DIM
GQA_KV_WIDTH = GQA_KV_HEADS * HEAD_DIM
MOE_GROUPS = 4
MOE_EXPERTS_PER_GROUP = 4
MOE_EXPERTS = MOE_GROUPS * MOE_EXPERTS_PER_GROUP
EXPERT_FF = 512
EPS = 1e-6
NEG_INF = -1e30

N_CTX_TOK = BATCH * SEQ
N_LAT_TOK = DEC_BATCH * DEC_SEQ
N_TOK = N_CTX_TOK + N_LAT_TOK
N_COND = 8

COL_NA_Q = 0
COL_NA_K = 512
COL_NA_V = 1024
COL_G_Q = 1536
COL_G_K = 2048
COL_G_V = 2176
COL_U = 2560
COL_GATES = 3072
Z_COLS = 6144

S5_T = 16
S5_PAIRS = S5_GROUPS // 2
S5_OCT_PAIRS = 4
S5_BLOCK_TOK = 2048
S5_ROWS = S5_BLOCK_TOK // S5_T
S5_NPOW = 7

MOE_ROW_WORDS = D_MODEL // 2 + 128
MOE_TILE = 512
MOE_ROWS = N_TOK + MOE_GROUPS * MOE_TILE
MOE_TOK_BLOCK = 512

NA_QROWS = 4
NA_KROWS = 12

VMEM_LIMIT = 56 * 1024 * 1024


def _dot(a, b):
    return jnp.dot(a, b, preferred_element_type=F32)


def _dot_nt(a, b):
    return lax.dot_general(a, b, (((1,), (1,)), ((), ())), preferred_element_type=F32)


def _split_bf16(x):
    hi = x.astype(BF16)
    lo = (x - hi.astype(F32)).astype(BF16)
    return hi, lo


def _rms(x, g):
    return x * lax.rsqrt(jnp.mean(x * x, axis=-1, keepdims=True) + EPS) * g


def _cond_index(i, tm):
    nctx = N_CTX_TOK // tm
    return jnp.where(i < nctx, 0, 1 + ((i - nctx) * tm) // DEC_SEQ)


def _ada_kernel(c_ref, w_ref, b_ref, o_ref):
    c = c_ref[...]
    s = c * jax.nn.sigmoid(c)
    s_hi, s_lo = _split_bf16(s)
    w_hi, w_lo = _split_bf16(w_ref[...])
    o_ref[...] = _dot(s_hi, w_hi) + _dot(s_lo, w_hi) + _dot(s_hi, w_lo) + b_ref[...]


def _ada_mod(cond, w_ada, b_ada):
    tn = 1536
    n = 6 * D_MODEL
    out = pl.pallas_call(
        _ada_kernel,
        out_shape=jax.ShapeDtypeStruct((DEPTH, N_COND, n), F32),
        grid=(DEPTH, n // tn),
        in_specs=[
            pl.BlockSpec((N_COND, D_MODEL), lambda l, j: (0, 0)),
            pl.BlockSpec((None, D_MODEL, tn), lambda l, j: (l, 0, j)),
            pl.BlockSpec((None, 1, tn), lambda l, j: (l, 0, j)),
        ],
        out_specs=pl.BlockSpec((None, N_COND, tn), lambda l, j: (l, 0, j)),
        compiler_params=pltpu.CompilerParams(
            dimension_semantics=("parallel", "parallel"), vmem_limit_bytes=VMEM_LIMIT),
        name="ada_mod",
    )(cond, w_ada, b_ada.reshape(DEPTH, 1, n))
    return out.reshape(DEPTH, N_COND, 6, D_MODEL)


INPROJ_TM = 1024
INPROJ_TN = 1536


def _inproj_kernel(x_ref, g_ref, mod_ref, w_ref, z_ref, u_ref, cna_ref, cgq_ref, h_sc):
    i, j = pl.program_id(0), pl.program_id(1)
    is_ctx = i < N_CTX_TOK // INPROJ_TM

    @pl.when(j == 0)
    def _():
        h = _rms(x_ref[...], g_ref[...]) * (1.0 + mod_ref[1:2, :]) + mod_ref[0:1, :]
        h_sc[...] = h.astype(BF16)

    acc = _dot(h_sc[...], w_ref[...])
    z_ref[...] = acc.astype(BF16)

    def rows(b):
        return slice(b * SEQ, (b + 1) * SEQ)

    def cols(c, width):
        return slice(c % INPROJ_TN, c % INPROJ_TN + width)

    @pl.when(is_ctx & (j == COL_NA_K // INPROJ_TN))
    def _():
        for b in range(INPROJ_TM // SEQ):
            cna_ref[b, 0] = acc[rows(b), cols(COL_NA_K, NA_WIDTH)]
            cna_ref[b, 1] = acc[rows(b), cols(COL_NA_V, NA_WIDTH)]

    @pl.when(is_ctx & (j == COL_G_K // INPROJ_TN))
    def _():
        for b in range(INPROJ_TM // SEQ):
            cgq_ref[b, 0] = acc[rows(b), cols(COL_G_K, GQA_KV_WIDTH)]
            cgq_ref[b, 1] = acc[rows(b), cols(COL_G_V, GQA_KV_WIDTH)]

    @pl.when(j == COL_U // INPROJ_TN)
    def _():
        u_ref[...] = acc[:, cols(COL_U, S5_CH)]


def _inproj(x, g, mod, w, l):
    tm, tn = INPROJ_TM, INPROJ_TN
    nb = tm // SEQ
    last_ctx = N_CTX_TOK // tm - 1
    return pl.pallas_call(
        _inproj_kernel,
        out_shape=(jax.ShapeDtypeStruct((N_TOK, Z_COLS), BF16),
                   jax.ShapeDtypeStruct((N_TOK, S5_CH), F32),
                   jax.ShapeDtypeStruct((BATCH, 2, SEQ, NA_WIDTH), F32),
                   jax.ShapeDtypeStruct((BATCH, 2, SEQ, GQA_KV_WIDTH), F32)),
        grid=(N_TOK // tm, Z_COLS // tn),
        in_specs=[
            pl.BlockSpec((tm, D_MODEL), lambda i, j: (i, 0)),
            pl.BlockSpec((1, D_MODEL), lambda i, j: (0, 0)),
            pl.BlockSpec((None, None, 6, D_MODEL), lambda i, j: (l, _cond_index(i, tm), 0, 0)),
            pl.BlockSpec((None, D_MODEL, tn), lambda i, j: (l, 0, j)),
        ],
        out_specs=(pl.BlockSpec((tm, tn), lambda i, j: (i, j)),
                   pl.BlockSpec((tm, S5_CH), lambda i, j: (i, 0)),
                   pl.BlockSpec((nb, 2, SEQ, NA_WIDTH), lambda i, j: (jnp.minimum(i, last_ctx), 0, 0, 0)),
                   pl.BlockSpec((nb, 2, SEQ, GQA_KV_WIDTH), lambda i, j: (jnp.minimum(i, last_ctx), 0, 0, 0))),
        scratch_shapes=[pltpu.VMEM((tm, D_MODEL), BF16)],
        compiler_params=pltpu.CompilerParams(
            dimension_semantics=("arbitrary", "arbitrary"), vmem_limit_bytes=VMEM_LIMIT),
        name="inproj",
    )(x, g, mod, w)


def _ctx_attn_kernel(sink_ref, na_ref, gq_ref, gkv_ref, oa_ref, ob_ref):
    scale = HEAD_DIM ** -0.5
    na = na_ref[...].astype(BF16)
    gq = gq_ref[...].astype(BF16)
    gkv = gkv_ref[...].astype(BF16)
    for h in range(NA_HEADS):
        q = na[:, COL_NA_Q + h * HEAD_DIM:COL_NA_Q + (h + 1) * HEAD_DIM]
        k = na[:, COL_NA_K + h * HEAD_DIM:COL_NA_K + (h + 1) * HEAD_DIM]
        v = na[:, COL_NA_V + h * HEAD_DIM:COL_NA_V + (h + 1) * HEAD_DIM]
        s = _dot_nt(q, k) * scale
        m = jnp.max(s, axis=-1, keepdims=True)
        p = jnp.exp(s - m)
        d = jnp.sum(p, axis=-1, keepdims=True)
        o = _dot(p.astype(BF16), v) / d
        oa_ref[:, h * HEAD_DIM:(h + 1) * HEAD_DIM] = o.astype(BF16)
    for h in range(GQA_Q_HEADS):
        kh = h // GQA_GROUP
        q = gq[:, h * HEAD_DIM:(h + 1) * HEAD_DIM]
        k = gkv[:, kh * HEAD_DIM:(kh + 1) * HEAD_DIM]
        v = gkv[:, GQA_KV_WIDTH + kh * HEAD_DIM:GQA_KV_WIDTH + (kh + 1) * HEAD_DIM]
        sink = sink_ref[h]
        s = _dot_nt(q, k) * scale
        m = jnp.maximum(jnp.max(s, axis=-1, keepdims=True), sink)
        p = jnp.exp(s - m)
        d = jnp.sum(p, axis=-1, keepdims=True) + jnp.exp(sink - m)
        o = _dot(p.astype(BF16), v) / d
        ob_ref[:, h * HEAD_DIM:(h + 1) * HEAD_DIM] = o.astype(BF16)


def _ctx_attn(z, sink):
    out = jax.ShapeDtypeStruct((N_CTX_TOK, NA_WIDTH), BF16)
    return pl.pallas_call(
        _ctx_attn_kernel,
        out_shape=(out, out),
        grid=(BATCH,),
        in_specs=[
            pl.BlockSpec(memory_space=pltpu.SMEM),
            pl.BlockSpec((SEQ, 3 * NA_WIDTH), lambda b: (b, 0)),
            pl.BlockSpec((SEQ, GQA_Q_WIDTH), lambda b: (b, COL_G_Q // GQA_Q_WIDTH)),
            pl.BlockSpec((SEQ, 2 * GQA_KV_WIDTH), lambda b: (b, COL_G_K // (2 * GQA_KV_WIDTH))),
        ],
        out_specs=(pl.BlockSpec((SEQ, NA_WIDTH), lambda b: (b, 0)),
                   pl.BlockSpec((SEQ, GQA_Q_WIDTH), lambda b: (b, 0))),
        compiler_params=pltpu.CompilerParams(
            dimension_semantics=("parallel",), vmem_limit_bytes=VMEM_LIMIT),
        name="ctx_attn",
    )(sink, z, z, z)


def _na_bias_tables(rpb):
    n_dc = 2 * NA_WIN_W - 1
    qc = np.arange(GRID_W)
    kc = np.arange(GRID_W)
    cstart = np.clip(qc - NA_WIN_W // 2, 0, GRID_W - NA_WIN_W)
    col_ok = (kc[None, :] >= cstart[:, None]) & (kc[None, :] < cstart[:, None] + NA_WIN_W)
    dc = np.clip(kc[None, :] - qc[:, None] + (NA_WIN_W - 1), 0, n_dc - 1)
    onehot = ((np.arange(n_dc)[:, None, None] == dc[None]) & col_ok[None]).astype(np.float32)
    band = jnp.einsum("hrd,dqk->hrqk", rpb.astype(F32), onehot, precision=lax.Precision.HIGHEST)
    band = jnp.where(col_ok, band, NEG_INF)
    band = jnp.pad(band, ((0, 0), (1, 1), (0, 0), (0, 0)), constant_values=NEG_INF)
    return jnp.concatenate([band[:, :-1], band[:, 1:]], axis=-1)


def _na_lat_kernel(q_ref, k_ref, v_ref, ck_ref, cv_ref, bias_ref, o_ref):
    i = pl.program_id(1)
    scale = HEAD_DIM ** -0.5
    nk = NA_KROWS * GRID_W
    kh = min(NA_WIN_H, GRID_ROWS)
    ks = jnp.clip(i * NA_QROWS - NA_WIN_H // 2, 0, GRID_ROWS - NA_KROWS)
    start = pl.multiple_of(ks * GRID_W, GRID_W)
    q = q_ref[...].astype(BF16)
    kw = k_ref[pl.ds(start, nk), :].astype(BF16)
    vw = v_ref[pl.ds(start, nk), :].astype(BF16)
    ck = ck_ref[...].astype(BF16)
    cv = cv_ref[...].astype(BF16)
    left = lax.broadcasted_iota(jnp.int32, (GRID_W, 2 * GRID_W), 1) < GRID_W
    pieces = []
    for qr in range(NA_QROWS):
        r = i * NA_QROWS + qr
        st = jnp.clip(r - kh // 2, 0, GRID_ROWS - kh)
        row = []
        for m in range(NA_KROWS // 2):
            k0 = ks + 2 * m
            plane = jnp.clip(k0 - r + NA_WIN_H, 0, 2 * NA_WIN_H - 1)
            ok0 = (k0 >= st) & (k0 < st + kh)
            ok1 = (k0 + 1 >= st) & (k0 + 1 < st + kh)
            row.append((plane, jnp.where(left, ok0.astype(jnp.int32), ok1.astype(jnp.int32)) > 0))
        pieces.append(row)
    for h in range(NA_HEADS):
        sl = slice(h * HEAD_DIM, (h + 1) * HEAD_DIM)
        qh = q[:, sl]
        bias = jnp.concatenate(
            [jnp.concatenate([jnp.where(ok, bias_ref[h, plane], NEG_INF) for plane, ok in row], axis=1)
             for row in pieces], axis=0)
        s_loc = _dot_nt(qh, kw[:, sl]) * scale + bias
        s_ctx = _dot_nt(qh, ck[:, sl]) * scale
        m = jnp.maximum(jnp.max(s_loc, axis=-1, keepdims=True), jnp.max(s_ctx, axis=-1, keepdims=True))
        p_loc = jnp.exp(s_loc - m)
        p_ctx = jnp.exp(s_ctx - m)
        d = jnp.sum(p_loc, axis=-1, keepdims=True) + jnp.sum(p_ctx, axis=-1, keepdims=True)
        o = (_dot(p_loc.astype(BF16), vw[:, sl]) + _dot(p_ctx.astype(BF16), cv[:, sl])) / d
        o_ref[:, sl] = o.astype(BF16)


def _na_lat(z, cache_na, bias, l):
    tq = NA_QROWS * GRID_W
    nsteps = GRID_ROWS // NA_QROWS
    lat_blk = N_CTX_TOK // DEC_SEQ
    return pl.pallas_call(
        _na_lat_kernel,
        out_shape=jax.ShapeDtypeStruct((N_LAT_TOK, NA_WIDTH), BF16),
        grid=(DEC_BATCH, nsteps),
        in_specs=[
            pl.BlockSpec((tq, NA_WIDTH), lambda b, i: (N_CTX_TOK // tq + b * nsteps + i, COL_NA_Q // NA_WIDTH)),
            pl.BlockSpec((DEC_SEQ, NA_WIDTH), lambda b, i: (lat_blk + b, COL_NA_K // NA_WIDTH)),
            pl.BlockSpec((DEC_SEQ, NA_WIDTH), lambda b, i: (lat_blk + b, COL_NA_V // NA_WIDTH)),
            pl.BlockSpec((None, None, None, PAST_LEN, NA_WIDTH), lambda b, i: (b, l, 0, 0, 0)),
            pl.BlockSpec((None, None, None, PAST_LEN, NA_WIDTH), lambda b, i: (b, l, 1, 0, 0)),
            pl.BlockSpec((NA_HEADS, 2 * NA_WIN_H, GRID_W, 2 * GRID_W), lambda b, i: (0, 0, 0, 0)),
        ],
        out_specs=pl.BlockSpec((tq, NA_WIDTH), lambda b, i: (b * nsteps + i, 0)),
        compiler_params=pltpu.CompilerParams(
            dimension_semantics=("parallel", "arbitrary"), vmem_limit_bytes=VMEM_LIMIT),
        name="na_lat",
    )(z, z, z, cache_na, cache_na, bias)


def _rope_tables():
    nf = HEAD_DIM // 4
    t = jnp.arange(DEC_SEQ)
    pos = jnp.stack([t // GRID_W, t % GRID_W], axis=-1).astype(F32)
    inv = ROPE_THETA ** (-jnp.arange(nf, dtype=F32) / nf)
    ang = pos[:, :, None] * inv
    cos = jnp.cos(ang)
    sin = jnp.sin(ang)
    cos_d = jnp.stack([cos, cos], axis=2).reshape(DEC_SEQ, HEAD_DIM)
    sin_d = jnp.stack([-sin, sin], axis=2).reshape(DEC_SEQ, HEAD_DIM)
    return cos_d, sin_d


def _rope(x, cos, sin_signed):
    n = x.shape[-1]
    nf = HEAD_DIM // 4
    lane = lax.broadcasted_iota(jnp.int32, x.shape, 1)
    first_half = (lane // nf) % 2 == 0
    partner = jnp.where(first_half, pltpu.roll(x, n - nf, 1), pltpu.roll(x, nf, 1))
    return x * cos + partner * sin_signed


def _swa_lat_kernel(sink_ref, q_ref, kv_ref, cq_ref, sq_ref, ckt_ref, skt_ref, ck_ref, cv_ref, o_ref, k_sc, v_sc):
    n = pl.program_id(1)
    scale = HEAD_DIM ** -0.5
    nwin = 3 * SWA_BLOCK

    @pl.when(n == 0)
    def _():
        kv = kv_ref[...]
        k_sc[...] = _rope(kv[:, :GQA_KV_WIDTH].astype(F32), ckt_ref[...], skt_ref[...]).astype(BF16)
        v_sc[...] = kv[:, GQA_KV_WIDTH:].astype(BF16)

    q = _rope(q_ref[...].astype(F32), cq_ref[...], sq_ref[...]).astype(BF16)
    start = pl.multiple_of(jnp.clip((n - 1) * SWA_BLOCK, 0, DEC_SEQ - nwin), SWA_BLOCK)
    kw = k_sc[pl.ds(start, nwin), :]
    vw = v_sc[pl.ds(start, nwin), :]
    ck = ck_ref[...].astype(BF16)
    cv = cv_ref[...].astype(BF16)
    rows = GQA_GROUP * SWA_BLOCK
    row = lax.broadcasted_iota(jnp.int32, (rows, nwin), 0)
    col = lax.broadcasted_iota(jnp.int32, (rows, nwin), 1)
    qpos = n * SWA_BLOCK + row % SWA_BLOCK
    kpos = start + col
    ok = jnp.abs(qpos - kpos) <= SWA_WINDOW
    grp = lax.broadcasted_iota(jnp.int32, (rows, 1), 0) // SWA_BLOCK
    for kh in range(GQA_KV_HEADS):
        sl = slice(kh * HEAD_DIM, (kh + 1) * HEAD_DIM)
        q4 = jnp.concatenate(
            [q[:, (kh * GQA_GROUP + g) * HEAD_DIM:(kh * GQA_GROUP + g + 1) * HEAD_DIM] for g in range(GQA_GROUP)],
            axis=0)
        sink = jnp.zeros((rows, 1), F32)
        for g in range(GQA_GROUP):
            sink = jnp.where(grp == g, sink_ref[kh * GQA_GROUP + g], sink)
        s_loc = jnp.where(ok, _dot_nt(q4, kw[:, sl]) * scale, NEG_INF)
        s_ctx = _dot_nt(q4, ck[:, sl]) * scale
        m = jnp.maximum(jnp.maximum(jnp.max(s_loc, axis=-1, keepdims=True),
                                    jnp.max(s_ctx, axis=-1, keepdims=True)), sink)
        p_loc = jnp.exp(s_loc - m)
        p_ctx = jnp.exp(s_ctx - m)
        d = (jnp.sum(p_loc, axis=-1, keepdims=True) + jnp.sum(p_ctx, axis=-1, keepdims=True)
             + jnp.exp(sink - m))
        o4 = (_dot(p_loc.astype(BF16), vw[:, sl]) + _dot(p_ctx.astype(BF16), cv[:, sl])) / d
        for g in range(GQA_GROUP):
            h = kh * GQA_GROUP + g
            o_ref[:, h * HEAD_DIM:(h + 1) * HEAD_DIM] = o4[g * SWA_BLOCK:(g + 1) * SWA_BLOCK].astype(BF16)


def _swa_lat(z, cache_gqa, sink, cos_q, sin_q, cos_k, sin_k, l):
    nb = DEC_SEQ // SWA_BLOCK
    lat_blk = N_CTX_TOK // DEC_SEQ
    return pl.pallas_call(
        _swa_lat_kernel,
        out_shape=jax.ShapeDtypeStruct((N_LAT_TOK, GQA_Q_WIDTH), BF16),
        grid=(DEC_BATCH, nb),
        in_specs=[
            pl.BlockSpec(memory_space=pltpu.SMEM),
            pl.BlockSpec((SWA_BLOCK, GQA_Q_WIDTH),
                         lambda b, n: (N_CTX_TOK // SWA_BLOCK + b * nb + n, COL_G_Q // GQA_Q_WIDTH)),
            pl.BlockSpec((DEC_SEQ, 2 * GQA_KV_WIDTH), lambda b, n: (lat_blk + b, COL_G_K // (2 * GQA_KV_WIDTH))),
            pl.BlockSpec((SWA_BLOCK, GQA_Q_WIDTH), lambda b, n: (n, 0)),
            pl.BlockSpec((SWA_BLOCK, GQA_Q_WIDTH), lambda b, n: (n, 0)),
            pl.BlockSpec((DEC_SEQ, GQA_KV_WIDTH), lambda b, n: (0, 0)),
            pl.BlockSpec((DEC_SEQ, GQA_KV_WIDTH), lambda b, n: (0, 0)),
            pl.BlockSpec((None, None, None, PAST_LEN, GQA_KV_WIDTH), lambda b, n: (b, l, 0, 0, 0)),
            pl.BlockSpec((None, None, None, PAST_LEN, GQA_KV_WIDTH), lambda b, n: (b, l, 1, 0, 0)),
        ],
        out_specs=pl.BlockSpec((SWA_BLOCK, GQA_Q_WIDTH), lambda b, n: (b * nb + n, 0)),
        scratch_shapes=[pltpu.VMEM((DEC_SEQ, GQA_KV_WIDTH), BF16), pltpu.VMEM((DEC_SEQ, GQA_KV_WIDTH), BF16)],
        compiler_params=pltpu.CompilerParams(
            dimension_semantics=("parallel", "arbitrary"), vmem_limit_bytes=VMEM_LIMIT),
        name="swa_lat",
    )(sink, z, z, cos_q, sin_q, cos_k, sin_k, cache_gqa, cache_gqa)


def _s5_operators(a_re, a_im, log_dt, b_re, b_im, c_re, c_im):
    T, G, N, C, P = S5_T, S5_GROUPS, S5_STATE, S5_GROUP_CH, S5_PAIRS
    tau = jnp.arange(T + 1, dtype=F32)
    eye2 = jnp.eye(2, dtype=F32)
    out = {k: [] for k in ("bs", "m", "cre", "cim", "apr", "api")}

    def pairs(x):
        return x.reshape((P, 2) + x.shape[1:])

    def diag(x, axis):
        e = eye2.reshape((1, 2) + (1,) * (axis - 1) + (2,) + (1,) * (x.ndim - axis - 1))
        return jnp.expand_dims(x, axis + 1) * e

    for d in range(2):
        A = lax.complex(a_re[d].astype(F32), a_im[d].astype(F32))
        dt = jnp.exp(log_dt[d].astype(F32))[:, None]
        a_bar = jnp.exp(A * dt)
        pw = jnp.exp((A * dt)[None] * tau[:, None, None])
        b_bar = ((a_bar - 1.0) / A)[..., None] * lax.complex(b_re[d].astype(F32), b_im[d].astype(F32))
        c_mat = lax.complex(c_re[d].astype(F32), c_im[d].astype(F32))
        kern = jnp.einsum("gon,tgn,gni->gtoi", c_mat, pw[:T], b_bar, precision=lax.Precision.HIGHEST).real
        k0 = diag(pairs(kern).transpose(0, 1, 4, 2, 3), 3).reshape(P, 2 * C, T * 2 * C)
        zero = jnp.zeros_like(k0)
        if d == 0:
            p_in = pw[:T][::-1]
            p_out = pw[1:T + 1]
            kp = jnp.concatenate([zero, k0], axis=2)
            m = [kp[:, :, (T - s) * 2 * C:(2 * T - s) * 2 * C] for s in range(T)]
        else:
            p_in = pw[:T]
            p_out = pw[1:T + 1][::-1]
            kp = jnp.concatenate([k0.reshape(P, 2 * C, T, 2 * C)[:, :, ::-1].reshape(k0.shape), zero], axis=2)
            m = [kp[:, :, (T - 1 - s) * 2 * C:(2 * T - 1 - s) * 2 * C] for s in range(T)]
        out["m"].append(jnp.concatenate(m, axis=1))
        pin = p_in.reshape(T, P, 2 * N).transpose(1, 0, 2)[:, :, None, :]
        bx = diag(pairs(b_bar.transpose(0, 2, 1)), 2).reshape(P, 1, 2 * C, 2 * N)
        bs = pin * bx
        out["bs"].append(jnp.concatenate([bs.real, bs.imag], axis=-1).reshape(P, 2 * T * C, 4 * N))
        cx = diag(pairs(c_mat.transpose(0, 2, 1)), 2).reshape(P, 2 * N, 2 * C)
        pout = p_out.reshape(T, P, 2 * N).transpose(1, 2, 0)
        co = jnp.tile(cx, (1, 1, T)) * jnp.repeat(pout, 2 * C, axis=2)
        out["cre"].append(co.real)
        out["cim"].append(-co.imag)
        pw2 = jnp.exp((A * dt)[None] * (T * 2.0 ** jnp.arange(S5_NPOW, dtype=F32))[:, None, None])
        out["apr"].append(pw2.real.reshape(S5_NPOW, G * N))
        out["api"].append(pw2.imag.reshape(S5_NPOW, G * N))
    return {k: jnp.stack(v) for k, v in out.items()}


def _s5_kernel(u_ref, bs_ref, m_ref, cre_ref, cim_ref, apr_ref, api_ref, h0_ref, y_ref, fin_ref, fin_sc):
    t = pl.program_id(1)
    nctx = N_CTX_TOK // S5_BLOCK_TOK
    refs = (u_ref, bs_ref, m_ref, cre_ref, cim_ref, apr_ref, api_ref, h0_ref, y_ref, fin_ref, fin_sc)

    @pl.when(t < nctx)
    def _():
        _s5_block(*refs, t, kseq=SEQ // S5_T, has_h0=False)

    @pl.when(t >= nctx)
    def _():
        _s5_block(*refs, t - nctx, kseq=DEC_SEQ // S5_T, has_h0=True)


def _s5_block(u_ref, bs_ref, m_ref, cre_ref, cim_ref, apr_ref, api_ref, h0_ref, y_ref, fin_ref, fin_sc, t, *,
              kseq, has_h0):
    rows, half, pc = S5_ROWS, 2 * S5_STATE, 2 * S5_GROUP_CH
    k = lax.broadcasted_iota(jnp.int32, (rows, half), 0) % kseq
    shifts = [1 << i for i in range(kseq.bit_length() - 1)]
    xs = [u_ref[pl.ds(s, rows, stride=S5_T), :].astype(BF16) for s in range(S5_T)]
    ys = []
    for p in range(S5_OCT_PAIRS):
        lanes = slice(p * half, (p + 1) * half)
        xp = jnp.concatenate([x[:, p * pc:(p + 1) * pc] for x in xs], axis=1)
        acc = None
        for d in range(2):
            inc = _dot(xp, bs_ref[d, p])
            sr, si = inc[:, :half], inc[:, half:]
            if has_h0:
                h0r = h0_ref[d, 0, t, :, lanes]
                h0i = h0_ref[d, 1, t, :, lanes]
                ar, ai = apr_ref[d, 0:1, lanes], api_ref[d, 0:1, lanes]
                first = (k == 0) if d == 0 else (k == kseq - 1)
                sr = sr + jnp.where(first, ar * h0r - ai * h0i, 0.0)
                si = si + jnp.where(first, ar * h0i + ai * h0r, 0.0)
            else:
                h0r = h0i = 0.0
            for i, sh in enumerate(shifts):
                ar, ai = apr_ref[d, i:i + 1, lanes], api_ref[d, i:i + 1, lanes]
                ok = (k >= sh) if d == 0 else (k < kseq - sh)
                amt = sh if d == 0 else rows - sh
                rr = jnp.where(ok, pltpu.roll(sr, amt, 0), 0.0)
                ri = jnp.where(ok, pltpu.roll(si, amt, 0), 0.0)
                sr, si = sr + ar * rr - ai * ri, si + ar * ri + ai * rr
            inner = (k >= 1) if d == 0 else (k < kseq - 1)
            amt = 1 if d == 0 else rows - 1
            hr = jnp.where(inner, pltpu.roll(sr, amt, 0), h0r)
            hi = jnp.where(inner, pltpu.roll(si, amt, 0), h0i)
            yd = (_dot(xp, m_ref[d, p]) + _dot(hr.astype(BF16), cre_ref[d, p])
                  + _dot(hi.astype(BF16), cim_ref[d, p]))
            acc = yd if acc is None else acc + yd
            if not has_h0:
                last = kseq - 1 if d == 0 else 0
                fin_sc[0] = sr
                fin_sc[1] = si
                fin_ref[d, 0, :, lanes] = fin_sc[0, pl.ds(last, rows // kseq, stride=kseq), :]
                fin_ref[d, 1, :, lanes] = fin_sc[1, pl.ds(last, rows // kseq, stride=kseq), :]
        ys.append(acc)
    for j in range(S5_T):
        y_ref[pl.ds(j, rows, stride=S5_T), :] = jnp.concatenate([y[:, j * pc:(j + 1) * pc] for y in ys], axis=1)


def _s5_scan(u, ops, h0):
    nblk, nctx = N_TOK // S5_BLOCK_TOK, N_CTX_TOK // S5_BLOCK_TOK
    noct = S5_PAIRS // S5_OCT_PAIRS
    half, gn = 2 * S5_STATE, S5_GROUPS * S5_STATE
    pw = 2 * S5_T * S5_GROUP_CH
    nseq = S5_BLOCK_TOK // SEQ
    wspec = lambda r, c: pl.BlockSpec((2, S5_OCT_PAIRS, r, c), lambda q, t: (0, q, 0, 0))
    aspec = pl.BlockSpec((2, S5_NPOW, S5_OCT_PAIRS * half), lambda q, t: (0, 0, q))
    return pl.pallas_call(
        _s5_kernel,
        grid=(noct, nblk),
        in_specs=[
            pl.BlockSpec((S5_BLOCK_TOK, 128), lambda q, t: (t, q)),
            wspec(pw, 2 * half), wspec(pw, pw), wspec(half, pw), wspec(half, pw), aspec, aspec,
            pl.BlockSpec((2, 2, DEC_BATCH, 1, S5_OCT_PAIRS * half), lambda q, t: (0, 0, 0, 0, q)),
        ],
        out_shape=(jax.ShapeDtypeStruct((N_TOK, S5_CH), F32), jax.ShapeDtypeStruct((2, 2, BATCH, gn), F32)),
        out_specs=(pl.BlockSpec((S5_BLOCK_TOK, 128), lambda q, t: (t, q)),
                   pl.BlockSpec((2, 2, nseq, S5_OCT_PAIRS * half),
                                lambda q, t: (0, 0, jnp.minimum(t, nctx - 1), q))),
        scratch_shapes=[pltpu.VMEM((2, S5_ROWS, half), F32)],
        compiler_params=pltpu.CompilerParams(
            dimension_semantics=("parallel", "arbitrary"), vmem_limit_bytes=VMEM_LIMIT),
        name="s5_scan",
    )(u, ops["bs"], ops["m"], ops["cre"], ops["cim"], ops["apr"], ops["api"],
      h0.reshape(2, 2, DEC_BATCH, 1, gn))


def _route(logits):
    lane = lax.broadcasted_iota(jnp.int32, logits.shape, 1)
    big = jnp.int32(1 << 20)
    is_g = (lane >= MOE_EXPERTS) & (lane < MOE_EXPERTS + MOE_GROUPS)
    lg = jnp.where(is_g, logits, NEG_INF)
    gmax = jnp.max(lg, axis=-1, keepdims=True)
    gsel = jnp.min(jnp.where(is_g & (lg == gmax), lane, big), axis=-1, keepdims=True) - MOE_EXPERTS
    p_group = 1.0 / jnp.sum(jnp.where(is_g, jnp.exp(lg - gmax), 0.0), axis=-1, keepdims=True)
    in_grp = (lane < MOE_EXPERTS) & (lane // MOE_EXPERTS_PER_GROUP == gsel)
    le = jnp.where(in_grp, logits, NEG_INF)
    v1 = jnp.max(le, axis=-1, keepdims=True)
    i1 = jnp.min(jnp.where(in_grp & (le == v1), lane, big), axis=-1, keepdims=True)
    rest = in_grp & (lane != i1)
    le2 = jnp.where(rest, logits, NEG_INF)
    v2 = jnp.max(le2, axis=-1, keepdims=True)
    i2 = jnp.min(jnp.where(rest & (le2 == v2), lane, big), axis=-1, keepdims=True)
    e2 = jnp.exp(v2 - v1)
    w1 = 1.0 / (1.0 + e2)
    w2 = e2 / (1.0 + e2)
    comb = jnp.where(lane == i1, w1 * p_group, 0.0) + jnp.where(lane == i2, w2 * p_group, 0.0)
    return jnp.where(lane == MOE_EXPERTS, gsel.astype(F32), comb)


def _merge_kernel(x_ref, oac_ref, oal_ref, obc_ref, obl_ref, u_ref, yc_ref, gates_ref, mod_ref, d_ref, wglu_ref,
                  wa_ref, wb_ref, wc_ref, wout_ref, g2_ref, wrh_ref, wrl_ref, br_ref, x1_ref, hx_ref, *, tm):
    is_ctx = pl.program_id(0) < N_CTX_TOK // tm
    oa = jnp.where(is_ctx, oac_ref[...], oal_ref[...])
    ob = jnp.where(is_ctx, obc_ref[...], obl_ref[...])
    y = u_ref[...] * d_ref[...] + yc_ref[...]
    y = y * (0.5 * (1.0 + jnp.tanh(math.sqrt(2.0 / math.pi) * (y + 0.044715 * (y * y * y)))))
    oc = y * jax.nn.sigmoid(_dot(y.astype(BF16), wglu_ref[...]))
    gate = jax.nn.sigmoid(gates_ref[...].astype(F32))
    merged = (gate[:, :D_MODEL] * _dot(oa, wa_ref[...])
              + gate[:, D_MODEL:2 * D_MODEL] * _dot(ob, wb_ref[...])
              + gate[:, 2 * D_MODEL:] * _dot(oc.astype(BF16), wc_ref[...]))
    x1 = x_ref[...] + mod_ref[2:3, :] * _dot(merged.astype(BF16), wout_ref[...])
    x1_ref[...] = x1
    h2 = _rms(x1, g2_ref[...]) * (1.0 + mod_ref[4:5, :]) + mod_ref[3:4, :]
    h_hi, h_lo = _split_bf16(h2)
    logits = _dot(h_hi, wrh_ref[...]) + _dot(h_lo, wrh_ref[...]) + _dot(h_hi, wrl_ref[...]) + br_ref[...]
    half = D_MODEL // 2
    hb = h_hi.astype(F32)
    word = (pltpu.bitcast(hb[:, :half], jnp.uint32) & jnp.uint32(0xFFFF0000)) | (
        pltpu.bitcast(hb[:, half:], jnp.uint32) >> 16)
    hx_ref[:, :half] = word
    hx_ref[:, half:] = pltpu.bitcast(_route(logits), jnp.uint32)


def _merge(x, oa_c, oa_l, ob_c, ob_l, u, yc, z, mod, d_s5, wglu, wa, wb, wc, wout, g2, wr_hi, wr_lo, br, l):
    tm = 512
    nctx = N_CTX_TOK // tm
    full = lambda r, c: pl.BlockSpec((r, c), lambda i: (0, 0))
    layer = lambda r, c: pl.BlockSpec((None, r, c), lambda i: (l, 0, 0))
    ctx_blk = lambda w: pl.BlockSpec((tm, w), lambda i: (jnp.minimum(i, nctx - 1), 0))
    lat_blk = lambda w: pl.BlockSpec((tm, w), lambda i: (jnp.maximum(i - nctx, 0), 0))
    return pl.pallas_call(
        functools.partial(_merge_kernel, tm=tm),
        out_shape=(jax.ShapeDtypeStruct((N_TOK, D_MODEL), F32),
                   jax.ShapeDtypeStruct((N_TOK, MOE_ROW_WORDS), jnp.uint32)),
        grid=(N_TOK // tm,),
        in_specs=[
            pl.BlockSpec((tm, D_MODEL), lambda i: (i, 0)),
            ctx_blk(NA_WIDTH), lat_blk(NA_WIDTH), ctx_blk(GQA_Q_WIDTH), lat_blk(GQA_Q_WIDTH),
            pl.BlockSpec((tm, S5_CH), lambda i: (i, 0)),
            pl.BlockSpec((tm, S5_CH), lambda i: (i, 0)),
            pl.BlockSpec((tm, N_BRANCH * D_MODEL), lambda i: (i, COL_GATES // (N_BRANCH * D_MODEL))),
            pl.BlockSpec((None, None, 6, D_MODEL), lambda i: (l, _cond_index(i, tm), 0, 0)),
            full(1, S5_CH), layer(S5_CH, S5_CH), layer(NA_WIDTH, D_MODEL), layer(GQA_Q_WIDTH, D_MODEL),
            layer(S5_CH, D_MODEL), layer(D_MODEL, D_MODEL), full(1, D_MODEL),
            full(D_MODEL, 128), full(D_MODEL, 128), full(1, 128),
        ],
        out_specs=(pl.BlockSpec((tm, D_MODEL), lambda i: (i, 0)),
                   pl.BlockSpec((tm, MOE_ROW_WORDS), lambda i: (i, 0))),
        compiler_params=pltpu.CompilerParams(
            dimension_semantics=("parallel",), vmem_limit_bytes=VMEM_LIMIT),
        name="merge",
    )(x, oa_c, oa_l, ob_c, ob_l, u, yc, z, mod, d_s5, wglu, wa, wb, wc, wout, g2, wr_hi, wr_lo, br)


def _moe_plan(gsel):
    onehot = (gsel[:, None] == jnp.arange(MOE_GROUPS)[None, :]).astype(jnp.int32)
    rank = jnp.sum((jnp.cumsum(onehot, axis=0) - onehot) * onehot, axis=1)
    count = jnp.sum(onehot, axis=0)
    padded = (count + MOE_TILE - 1) // MOE_TILE * MOE_TILE
    end = jnp.cumsum(padded)
    pos = jnp.sum(onehot * (end - padded)[None, :], axis=1) + rank
    tile_row = jnp.arange(MOE_ROWS // MOE_TILE) * MOE_TILE
    tile_group = jnp.minimum(jnp.sum((tile_row[:, None] >= end[None, :]).astype(jnp.int32), axis=1),
                             MOE_GROUPS - 1)
    return pos.astype(jnp.int32), tile_group.astype(jnp.int32), (end[-1:] // MOE_TILE).astype(jnp.int32)


def _dispatch_kernel(pos_ref, hx_ref, init_ref, out_ref, sem):
    del init_ref
    base = pl.program_id(0) * MOE_TOK_BLOCK

    def row_copy(r):
        return pltpu.make_async_copy(hx_ref.at[pl.ds(r, 1), :], out_ref.at[pl.ds(pos_ref[base + r], 1), :], sem)

    def start(r, c):
        row_copy(r).start()
        return c

    def wait(r, c):
        row_copy(r).wait()
        return c

    lax.fori_loop(0, MOE_TOK_BLOCK, start, 0, unroll=16)
    lax.fori_loop(0, MOE_TOK_BLOCK, wait, 0, unroll=True)


def _dispatch(hx, pos):
    return pl.pallas_call(
        _dispatch_kernel,
        grid_spec=pltpu.PrefetchScalarGridSpec(
            num_scalar_prefetch=1,
            grid=(N_TOK // MOE_TOK_BLOCK,),
            in_specs=[pl.BlockSpec((MOE_TOK_BLOCK, MOE_ROW_WORDS), lambda i, pos: (i, 0)),
                      pl.BlockSpec(memory_space=pl.ANY)],
            out_specs=pl.BlockSpec(memory_space=pl.ANY),
            scratch_shapes=[pltpu.SemaphoreType.DMA]),
        out_shape=jax.ShapeDtypeStruct((MOE_ROWS, MOE_ROW_WORDS), jnp.uint32),
        input_output_aliases={2: 0},
        compiler_params=pltpu.CompilerParams(dimension_semantics=("arbitrary",)),
        name="moe_dispatch",
    )(pos, hx, jnp.zeros((MOE_ROWS, MOE_ROW_WORDS), jnp.uint32))


def _experts_kernel(tg_ref, nu_ref, hx_ref, wg_ref, wu_ref, wd_ref, y_ref):
    t = pl.program_id(0)

    @pl.when(t >= nu_ref[0])
    def _():
        y_ref[...] = jnp.zeros_like(y_ref)

    @pl.when(t < nu_ref[0])
    def _():
        half = D_MODEL // 2
        word = hx_ref[:, :half]
        h = jnp.concatenate([pltpu.bitcast(word & jnp.uint32(0xFFFF0000), F32).astype(BF16),
                             pltpu.bitcast(word << 16, F32).astype(BF16)], axis=1)
        comb = pltpu.bitcast(hx_ref[:, half:], F32)
        lane = lax.broadcasted_iota(jnp.int32, comb.shape, 1)
        first = tg_ref[t] * MOE_EXPERTS_PER_GROUP
        acc = None
        for e in range(MOE_EXPERTS_PER_GROUP):
            a = _dot(h, wg_ref[e])
            b = _dot(h, wu_ref[e])
            ce = jnp.sum(jnp.where(lane == first + e, comb, 0.0), axis=-1, keepdims=True)
            act = (a * jax.nn.sigmoid(a)) * b * ce
            c = _dot(act.astype(BF16), wd_ref[e])
            acc = c if acc is None else acc + c
        y_ref[...] = acc


def _experts(hs, tile_group, n_used, wg, wu, wd, l):
    wspec = lambda r, c: pl.BlockSpec((None, None, MOE_EXPERTS_PER_GROUP, r, c),
                                      lambda t, tg, nu: (l, tg[t], 0, 0, 0))
    return pl.pallas_call(
        _experts_kernel,
        grid_spec=pltpu.PrefetchScalarGridSpec(
            num_scalar_prefetch=2,
            grid=(MOE_ROWS // MOE_TILE,),
            in_specs=[pl.BlockSpec((MOE_TILE, MOE_ROW_WORDS), lambda t, tg, nu: (t, 0)),
                      wspec(D_MODEL, EXPERT_FF), wspec(D_MODEL, EXPERT_FF), wspec(EXPERT_FF, D_MODEL)],
            out_specs=pl.BlockSpec((MOE_TILE, D_MODEL), lambda t, tg, nu: (t, 0))),
        out_shape=jax.ShapeDtypeStruct((MOE_ROWS, D_MODEL), F32),
        compiler_params=pltpu.CompilerParams(
            dimension_semantics=("arbitrary",), vmem_limit_bytes=VMEM_LIMIT),
        name="moe_experts",
    )(tile_group, n_used, hs, wg, wu, wd)


def _combine_kernel(pos_ref, y_ref, x_ref, mod_ref, fg_ref, x2_ref, *rest, final):
    buf, sem = rest[-2:]
    base = pl.program_id(0) * MOE_TOK_BLOCK

    def row_copy(r):
        return pltpu.make_async_copy(y_ref.at[pl.ds(pos_ref[base + r], 1), :], buf.at[pl.ds(r, 1), :], sem)

    def start(r, c):
        row_copy(r).start()
        return c

    def wait(r, c):
        row_copy(r).wait()
        return c

    lax.fori_loop(0, MOE_TOK_BLOCK, start, 0, unroll=16)
    lax.fori_loop(0, MOE_TOK_BLOCK, wait, 0, unroll=True)
    x2 = x_ref[...] + mod_ref[5:6, :] * buf[...]
    x2_ref[...] = x2
    if final:
        rest[0][...] = _rms(x2, fg_ref[...])


def _combine(ys, pos, x1, mod, fg, l, final):
    tm = MOE_TOK_BLOCK
    tok = pl.BlockSpec((tm, D_MODEL), lambda i, pos: (i, 0))
    out = jax.ShapeDtypeStruct((N_TOK, D_MODEL), F32)
    return pl.pallas_call(
        functools.partial(_combine_kernel, final=final),
        grid_spec=pltpu.PrefetchScalarGridSpec(
            num_scalar_prefetch=1,
            grid=(N_TOK // tm,),
            in_specs=[pl.BlockSpec(memory_space=pl.ANY), tok,
                      pl.BlockSpec((None, None, 6, D_MODEL), lambda i, pos: (l, _cond_index(i, tm), 0, 0)),
                      pl.BlockSpec((1, D_MODEL), lambda i, pos: (0, 0))],
            out_specs=(tok, tok) if final else tok,
            scratch_shapes=[pltpu.VMEM((tm, D_MODEL), F32), pltpu.SemaphoreType.DMA]),
        out_shape=(out, out) if final else out,
        compiler_params=pltpu.CompilerParams(
            dimension_semantics=("arbitrary",), vmem_limit_bytes=VMEM_LIMIT),
        name="moe_combine",
    )(pos, ys, x1, mod, fg)


def _pack_w_in(w):
    qkv = w[..., :COL_G_V + GQA_KV_WIDTH]
    u = w[..., COL_G_V + GQA_KV_WIDTH:COL_G_V + GQA_KV_WIDTH + S5_CH]
    gates = w[..., COL_G_V + GQA_KV_WIDTH + S5_CH:]
    pad = jnp.zeros(w.shape[:-1] + (COL_U - (COL_G_V + GQA_KV_WIDTH),), w.dtype)
    return jnp.concatenate([qkv, pad, u, gates], axis=-1).astype(BF16)


def kernel(x_prompt, x_sample, cache_na_kv, cache_gqa_kv, state_ssm, c, c_ctx, norm_g, w_ada, b_ada, w_in, na_rpb, gqa_sink, s5_a_re, s5_a_im, s5_log_dt, s5_b_re, s5_b_im, s5_c_re, s5_c_im, s5_d, s5_w_glu, w_branch_a, w_branch_b, w_branch_c, w_out, moe_w_group, moe_b_group, moe_w_expert, moe_b_expert, moe_w_gate, moe_w_up, moe_w_down, final_g):
    cond = jnp.zeros((N_COND, D_MODEL), F32).at[0].set(c_ctx.astype(F32)).at[1:1 + DEC_BATCH].set(c.astype(F32))
    mod = _ada_mod(cond, w_ada.astype(F32), b_ada.astype(F32))

    x = jnp.concatenate([x_prompt.reshape(N_CTX_TOK, D_MODEL), x_sample.reshape(N_LAT_TOK, D_MODEL)], axis=0)
    cache_na = cache_na_kv.reshape(DEC_BATCH, DEPTH, 2, PAST_LEN, NA_WIDTH)
    cache_gqa = cache_gqa_kv.reshape(DEC_BATCH, DEPTH, 2, PAST_LEN, GQA_KV_WIDTH)
    cos_d, sin_d = _rope_tables()
    cos_q, sin_q = jnp.tile(cos_d, (1, GQA_Q_HEADS)), jnp.tile(sin_d, (1, GQA_Q_HEADS))
    cos_k, sin_k = jnp.tile(cos_d, (1, GQA_KV_HEADS)), jnp.tile(sin_d, (1, GQA_KV_HEADS))
    gn = S5_GROUPS * S5_STATE
    fg = final_g.astype(F32).reshape(1, D_MODEL)

    w_in_packed = _pack_w_in(w_in)
    bf16_weights = [w.astype(BF16) for w in (s5_w_glu, w_branch_a, w_branch_b, w_branch_c, w_out)]
    grouped = lambda w: w.astype(BF16).reshape((DEPTH, MOE_GROUPS, MOE_EXPERTS_PER_GROUP) + w.shape[2:])
    moe_weights = [grouped(w) for w in (moe_w_gate, moe_w_up, moe_w_down)]

    na_list, gqa_list, ssm_list = [], [], []
    y = None
    for l in range(DEPTH):
        z, u, new_na, new_gqa = _inproj(x, norm_g[l, 0].astype(F32).reshape(1, D_MODEL), mod, w_in_packed, l)
        na_list.append(new_na.reshape(BATCH, 2, SEQ, NA_HEADS, HEAD_DIM))
        gqa_list.append(new_gqa.reshape(BATCH, 2, SEQ, GQA_KV_HEADS, HEAD_DIM))

        sink = gqa_sink[l].astype(F32)
        oa_c, ob_c = _ctx_attn(z, sink)
        oa_l = _na_lat(z, cache_na, _na_bias_tables(na_rpb[l]), l)
        ob_l = _swa_lat(z, cache_gqa, sink, cos_q, sin_q, cos_k, sin_k, l)

        ops = _s5_operators(s5_a_re[l], s5_a_im[l], s5_log_dt[l], s5_b_re[l], s5_b_im[l], s5_c_re[l], s5_c_im[l])
        ops = {k: (v.astype(BF16) if k in ("bs", "m", "cre", "cim") else v) for k, v in ops.items()}
        h0 = state_ssm[:, l].astype(F32).reshape(DEC_BATCH, 2, 2, gn).transpose(1, 2, 0, 3)
        yc, fin = _s5_scan(u, ops, h0)
        ssm_list.append(fin.transpose(2, 0, 1, 3).reshape(BATCH, 2, 2, S5_GROUPS, S5_STATE).astype(x_prompt.dtype))

        wr = jnp.zeros((D_MODEL, 128), F32)
        wr = wr.at[:, :MOE_EXPERTS].set(moe_w_expert[l].astype(F32))
        wr = wr.at[:, MOE_EXPERTS:MOE_EXPERTS + MOE_GROUPS].set(moe_w_group[l].astype(F32))
        br = jnp.zeros((1, 128), F32)
        br = br.at[0, :MOE_EXPERTS].set(moe_b_expert[l].astype(F32))
        br = br.at[0, MOE_EXPERTS:MOE_EXPERTS + MOE_GROUPS].set(moe_b_group[l].astype(F32))
        wr_hi, wr_lo = _split_bf16(wr)
        x1, hx = _merge(
            x, oa_c, oa_l, ob_c, ob_l, u, yc, z, mod, s5_d[l].astype(F32).reshape(1, S5_CH), *bf16_weights,
            norm_g[l, 1].astype(F32).reshape(1, D_MODEL), wr_hi, wr_lo, br, l)
        gsel = lax.bitcast_convert_type(hx[:, D_MODEL // 2 + MOE_EXPERTS], F32).astype(jnp.int32)
        pos, tile_group, n_used = _moe_plan(gsel)
        ys = _experts(_dispatch(hx, pos), tile_group, n_used, *moe_weights, l)
        if l < DEPTH - 1:
            x = _combine(ys, pos, x1, mod, fg, l, False)
        else:
            x, y = _combine(ys, pos, x1, mod, fg, l, True)

    y_prompt = y[:N_CTX_TOK].reshape(BATCH, SEQ, D_MODEL)
    y_sample = y[N_CTX_TOK:].reshape(DEC_BATCH, DEC_SEQ, D_MODEL)
    return (y_prompt, y_sample, jnp.stack(na_list, axis=1), jnp.stack(gqa_list, axis=1),
            jnp.stack(ssm_list, axis=1))
```

```python
import functools
import math

import numpy as np
import jax
import jax.numpy as jnp
from jax import lax
from jax.experimental import pallas as pl
from jax.experimental.pallas import tpu as pltpu

F32 = jnp.float32
BF16 = jnp.bfloat16

D_MODEL = 1024
BATCH = 32
SEQ = 256
DEPTH = 2
DEC_BATCH = 4
DEC_SEQ = 2048
PAST_LEN = 512
GRID_W = 64
GRID_ROWS = DEC_SEQ // GRID_W
HEAD_DIM = 64
NA_HEADS = 8
NA_WIN_H = 8
NA_WIN_W = 16
GQA_Q_HEADS = 8
GQA_KV_HEADS = 2
GQA_GROUP = GQA_Q_HEADS // GQA_KV_HEADS
SWA_WINDOW = 128
SWA_BLOCK = 128
ROPE_THETA = 10000.0
S5_CH = 512
S5_GROUP_CH = 16
S5_GROUPS = S5_CH // S5_GROUP_CH
S5_STATE = 64
N_BRANCH = 3
NA_WIDTH = NA_HEADS * HEAD_DIM
GQA_Q_WIDTH = GQA_Q_HEADS * HEAD_DIM
GQA_KV_WIDTH = GQA_KV_HEADS * HEAD_DIM
MOE_GROUPS = 4
MOE_EXPERTS_PER_GROUP = 4
MOE_EXPERTS = MOE_GROUPS * MOE_EXPERTS_PER_GROUP
EXPERT_FF = 512
EPS = 1e-6
NEG_INF = -1e30

N_CTX_TOK = BATCH * SEQ
N_LAT_TOK = DEC_BATCH * DEC_SEQ
N_TOK = N_CTX_TOK + N_LAT_TOK
N_COND = 8

COL_NA_Q = 0
COL_NA_K = 512
COL_NA_V = 1024
COL_G_Q = 1536
COL_G_K = 2048
COL_G_V = 2176
COL_U = 2560
COL_GATES = 3072
Z_COLS = 6144

S5_T = 16
S5_PAIRS = S5_GROUPS // 2
S5_OCT_PAIRS = 4
S5_BLOCK_TOK = 2048
S5_ROWS = S5_BLOCK_TOK // S5_T
S5_NPOW = 7

MOE_ROW_WORDS = D_MODEL // 2 + 128
MOE_TILE = 512
MOE_ROWS = N_TOK + MOE_GROUPS * MOE_TILE
MOE_TOK_BLOCK = 512

CTX_HEADS_PER_DOT = 4
NA_HEADS_PER_DOT = 2
NA_QROWS = 4
NA_KROWS = 12

VMEM_LIMIT = 56 * 1024 * 1024


def _dot(a, b):
    return jnp.dot(a, b, preferred_element_type=F32)


def _dot_nt(a, b):
    return lax.dot_general(a, b, (((1,), (1,)), ((), ())), preferred_element_type=F32)


def _split_bf16(x):
    hi = x.astype(BF16)
    lo = (x - hi.astype(F32)).astype(BF16)
    return hi, lo


def _rms(x, g):
    return x * lax.rsqrt(jnp.mean(x * x, axis=-1, keepdims=True) + EPS) * g


def _cond_index(i, tm):
    nctx = N_CTX_TOK // tm
    return jnp.where(i < nctx, 0, 1 + ((i - nctx) * tm) // DEC_SEQ)


def _ada_kernel(c_ref, w_ref, b_ref, o_ref):
    c = c_ref[...]
    s = c * jax.nn.sigmoid(c)
    s_hi, s_lo = _split_bf16(s)
    w_hi, w_lo = _split_bf16(w_ref[...])
    o_ref[...] = _dot(s_hi, w_hi) + _dot(s_lo, w_hi) + _dot(s_hi, w_lo) + b_ref[...]


def _ada_mod(cond, w_ada, b_ada):
    tn = 1536
    n = 6 * D_MODEL
    out = pl.pallas_call(
        _ada_kernel,
        out_shape=jax.ShapeDtypeStruct((DEPTH, N_COND, n), F32),
        grid=(DEPTH, n // tn),
        in_specs=[
            pl.BlockSpec((N_COND, D_MODEL), lambda l, j: (0, 0)),
            pl.BlockSpec((None, D_MODEL, tn), lambda l, j: (l, 0, j)),
            pl.BlockSpec((None, 1, tn), lambda l, j: (l, 0, j)),
        ],
        out_specs=pl.BlockSpec((None, N_COND, tn), lambda l, j: (l, 0, j)),
        compiler_params=pltpu.CompilerParams(
            dimension_semantics=("parallel", "parallel"), vmem_limit_bytes=VMEM_LIMIT),
        name="ada_mod",
    )(cond, w_ada, b_ada.reshape(DEPTH, 1, n))
    return out.reshape(DEPTH, N_COND, 6, D_MODEL)


INPROJ_TM = 1024
INPROJ_TN = 1536


def _inproj_kernel(xc_ref, xl_ref, g_ref, mod_ref, w_ref, z_ref, u_ref, cna_ref, cgq_ref, h_sc):
    i, j = pl.program_id(0), pl.program_id(1)
    is_ctx = i < N_CTX_TOK // INPROJ_TM

    def norm_mod(x_ref):
        h = _rms(x_ref[...], g_ref[...]) * (1.0 + mod_ref[1:2, :]) + mod_ref[0:1, :]
        h_sc[...] = h.astype(BF16)

    @pl.when((j == 0) & is_ctx)
    def _():
        norm_mod(xc_ref)

    @pl.when((j == 0) & jnp.logical_not(is_ctx))
    def _():
        norm_mod(xl_ref)

    acc = _dot(h_sc[...], w_ref[...])
    z_ref[...] = acc.astype(BF16)

    def rows(b):
        return slice(b * SEQ, (b + 1) * SEQ)

    def cols(c, width):
        return slice(c % INPROJ_TN, c % INPROJ_TN + width)

    @pl.when(is_ctx & (j == COL_NA_K // INPROJ_TN))
    def _():
        for b in range(INPROJ_TM // SEQ):
            cna_ref[b, 0] = acc[rows(b), cols(COL_NA_K, NA_WIDTH)]
            cna_ref[b, 1] = acc[rows(b), cols(COL_NA_V, NA_WIDTH)]

    @pl.when(is_ctx & (j == COL_G_K // INPROJ_TN))
    def _():
        for b in range(INPROJ_TM // SEQ):
            cgq_ref[b, 0] = acc[rows(b), cols(COL_G_K, GQA_KV_WIDTH)]
            cgq_ref[b, 1] = acc[rows(b), cols(COL_G_V, GQA_KV_WIDTH)]

    @pl.when(j == COL_U // INPROJ_TN)
    def _():
        u_ref[...] = acc[:, cols(COL_U, S5_CH)]


def _token_arrays(x):
    return (x, x, N_CTX_TOK) if not isinstance(x, tuple) else (x[0], x[1], 0)


def _inproj(x, g, mod, w, l):
    tm, tn = INPROJ_TM, INPROJ_TN
    nb = tm // SEQ
    last_ctx = N_CTX_TOK // tm - 1
    x_ctx, x_lat, lat0 = _token_arrays(x)
    return pl.pallas_call(
        _inproj_kernel,
        out_shape=(jax.ShapeDtypeStruct((N_TOK, Z_COLS), BF16),
                   jax.ShapeDtypeStruct((N_TOK, S5_CH), F32),
                   jax.ShapeDtypeStruct((BATCH, 2, SEQ, NA_WIDTH), F32),
                   jax.ShapeDtypeStruct((BATCH, 2, SEQ, GQA_KV_WIDTH), F32)),
        grid=(N_TOK // tm, Z_COLS // tn),
        in_specs=[
            pl.BlockSpec((tm, D_MODEL), lambda i, j: (jnp.minimum(i, last_ctx), 0)),
            pl.BlockSpec((tm, D_MODEL), lambda i, j: (jnp.maximum(i - last_ctx - 1, 0) + lat0 // tm, 0)),
            pl.BlockSpec((1, D_MODEL), lambda i, j: (0, 0)),
            pl.BlockSpec((None, None, 6, D_MODEL), lambda i, j: (l, _cond_index(i, tm), 0, 0)),
            pl.BlockSpec((None, D_MODEL, tn), lambda i, j: (l, 0, j)),
        ],
        out_specs=(pl.BlockSpec((tm, tn), lambda i, j: (i, j)),
                   pl.BlockSpec((tm, S5_CH), lambda i, j: (i, 0)),
                   pl.BlockSpec((nb, 2, SEQ, NA_WIDTH), lambda i, j: (jnp.minimum(i, last_ctx), 0, 0, 0)),
                   pl.BlockSpec((nb, 2, SEQ, GQA_KV_WIDTH), lambda i, j: (jnp.minimum(i, last_ctx), 0, 0, 0))),
        scratch_shapes=[pltpu.VMEM((tm, D_MODEL), BF16)],
        compiler_params=pltpu.CompilerParams(
            dimension_semantics=("arbitrary", "arbitrary"), vmem_limit_bytes=VMEM_LIMIT),
        name="inproj",
    )(x_ctx, x_lat, g, mod, w)


def _block_diag_heads(x, nrep):
    t = x.shape[0]
    rows = lax.broadcasted_iota(jnp.int32, (nrep * t, nrep * HEAD_DIM), 0) // t
    lanes = lax.broadcasted_iota(jnp.int32, (nrep * t, nrep * HEAD_DIM), 1) // HEAD_DIM
    return jnp.where(rows == lanes, jnp.concatenate([x] * nrep, axis=0), jnp.zeros((), x.dtype))


def _per_head(cols, shape):
    head = lax.broadcasted_iota(jnp.int32, shape, 1) // HEAD_DIM
    out = cols[-1]
    for h in range(len(cols) - 2, -1, -1):
        out = jnp.where(head == h, cols[h], out)
    return out


def _attend_heads(q, k, v, sinks):
    nh = q.shape[1] // HEAD_DIM
    t = k.shape[0]
    s = _dot_nt(q * HEAD_DIM ** -0.5, _block_diag_heads(k, nh))
    ps, ds = [], []
    for h in range(nh):
        sh = s[:, h * t:(h + 1) * t]
        m = jnp.max(sh, axis=-1, keepdims=True)
        if sinks is not None:
            m = jnp.maximum(m, sinks[h])
        p = jnp.exp(sh - m)
        d = jnp.sum(p, axis=-1, keepdims=True)
        if sinks is not None:
            d = d + jnp.exp(sinks[h] - m)
        ps.append(p.astype(BF16))
        ds.append(d)
    o = _dot(jnp.concatenate(ps, axis=1), _block_diag_heads(v, nh))
    return o / _per_head(ds, o.shape)


def _ctx_attn_kernel(sink_ref, na_ref, gq_ref, gkv_ref, oa_ref, ob_ref):
    hb = CTX_HEADS_PER_DOT
    w = hb * HEAD_DIM
    na = na_ref[...]
    gq = gq_ref[...]
    gkv = gkv_ref[...]
    for g in range(NA_HEADS // hb):
        o = _attend_heads(na[:, COL_NA_Q + g * w:COL_NA_Q + (g + 1) * w],
                          na[:, COL_NA_K + g * w:COL_NA_K + (g + 1) * w],
                          na[:, COL_NA_V + g * w:COL_NA_V + (g + 1) * w], None)
        oa_ref[:, g * w:(g + 1) * w] = o.astype(BF16)
    for g in range(GQA_Q_HEADS // hb):
        kv_heads = [(g * hb + h) // GQA_GROUP for h in range(hb)]
        k = jnp.concatenate([gkv[:, kh * HEAD_DIM:(kh + 1) * HEAD_DIM] for kh in kv_heads], axis=1)
        v = jnp.concatenate([gkv[:, GQA_KV_WIDTH + kh * HEAD_DIM:GQA_KV_WIDTH + (kh + 1) * HEAD_DIM]
                             for kh in kv_heads], axis=1)
        o = _attend_heads(gq[:, g * w:(g + 1) * w], k, v, [sink_ref[g * hb + h] for h in range(hb)])
        ob_ref[:, g * w:(g + 1) * w] = o.astype(BF16)


def _ctx_attn(z, sink):
    out = jax.ShapeDtypeStruct((N_CTX_TOK, NA_WIDTH), BF16)
    return pl.pallas_call(
        _ctx_attn_kernel,
        out_shape=(out, out),
        grid=(BATCH,),
        in_specs=[
            pl.BlockSpec(memory_space=pltpu.SMEM),
            pl.BlockSpec((SEQ, 3 * NA_WIDTH), lambda b: (b, 0)),
            pl.BlockSpec((SEQ, GQA_Q_WIDTH), lambda b: (b, COL_G_Q // GQA_Q_WIDTH)),
            pl.BlockSpec((SEQ, 2 * GQA_KV_WIDTH), lambda b: (b, COL_G_K // (2 * GQA_KV_WIDTH))),
        ],
        out_specs=(pl.BlockSpec((SEQ, NA_WIDTH), lambda b: (b, 0)),
                   pl.BlockSpec((SEQ, GQA_Q_WIDTH), lambda b: (b, 0))),
        compiler_params=pltpu.CompilerParams(
            dimension_semantics=("parallel",), vmem_limit_bytes=VMEM_LIMIT),
        name="ctx_attn",
    )(sink, z, z, z)


def _na_bias_tables(rpb):
    n_dc = 2 * NA_WIN_W - 1
    qc = np.arange(GRID_W)
    kc = np.arange(GRID_W)
    cstart = np.clip(qc - NA_WIN_W // 2, 0, GRID_W - NA_WIN_W)
    col_ok = (kc[None, :] >= cstart[:, None]) & (kc[None, :] < cstart[:, None] + NA_WIN_W)
    dc = np.clip(kc[None, :] - qc[:, None] + (NA_WIN_W - 1), 0, n_dc - 1)
    onehot = ((np.arange(n_dc)[:, None, None] == dc[None]) & col_ok[None]).astype(np.float32)
    band = jnp.einsum("hrd,dqk->hrqk", rpb.astype(F32), onehot, precision=lax.Precision.HIGHEST)
    band = jnp.where(col_ok, band, NEG_INF)
    band = jnp.pad(band, ((0, 0), (1, 1), (0, 0), (0, 0)), constant_values=NEG_INF)
    return jnp.concatenate([band[:, :-1], band[:, 1:]], axis=-1)


def _na_lat_kernel(q_ref, k_ref, v_ref, ck_ref, cv_ref, bias_ref, o_ref):
    i = pl.program_id(1)
    scale = HEAD_DIM ** -0.5
    nk = NA_KROWS * GRID_W
    kh = min(NA_WIN_H, GRID_ROWS)
    ks = jnp.clip(i * NA_QROWS - NA_WIN_H // 2, 0, GRID_ROWS - NA_KROWS)
    start = pl.multiple_of(ks * GRID_W, GRID_W)
    q = q_ref[...].astype(BF16)
    kw = k_ref[pl.ds(start, nk), :].astype(BF16)
    vw = v_ref[pl.ds(start, nk), :].astype(BF16)
    ck = ck_ref[...].astype(BF16)
    cv = cv_ref[...].astype(BF16)
    left = lax.broadcasted_iota(jnp.int32, (GRID_W, 2 * GRID_W), 1) < GRID_W
    pieces = []
    for qr in range(NA_QROWS):
        r = i * NA_QROWS + qr
        st = jnp.clip(r - kh // 2, 0, GRID_ROWS - kh)
        row = []
        for m in range(NA_KROWS // 2):
            k0 = ks + 2 * m
            plane = jnp.clip(k0 - r + NA_WIN_H, 0, 2 * NA_WIN_H - 1)
            ok0 = (k0 >= st) & (k0 < st + kh)
            ok1 = (k0 + 1 >= st) & (k0 + 1 < st + kh)
            row.append((plane, jnp.where(left, ok0.astype(jnp.int32), ok1.astype(jnp.int32)) > 0))
        pieces.append(row)
    hb = NA_HEADS_PER_DOT
    w = hb * HEAD_DIM
    for g in range(NA_HEADS // hb):
        lanes = slice(g * w, (g + 1) * w)
        qg = q[:, lanes] * scale
        s_loc = _dot_nt(qg, _block_diag_heads(kw[:, lanes], hb))
        s_ctx = _dot_nt(qg, _block_diag_heads(ck[:, lanes], hb))
        p_locs, p_ctxs, ds = [], [], []
        for hh in range(hb):
            bias = jnp.concatenate(
                [jnp.concatenate([jnp.where(ok, bias_ref[g * hb + hh, plane], NEG_INF) for plane, ok in row], axis=1)
                 for row in pieces], axis=0)
            sl = s_loc[:, hh * nk:(hh + 1) * nk] + bias
            sc = s_ctx[:, hh * PAST_LEN:(hh + 1) * PAST_LEN]
            m = jnp.maximum(jnp.max(sl, axis=-1, keepdims=True), jnp.max(sc, axis=-1, keepdims=True))
            p_loc = jnp.exp(sl - m)
            p_ctx = jnp.exp(sc - m)
            ds.append(jnp.sum(p_loc, axis=-1, keepdims=True) + jnp.sum(p_ctx, axis=-1, keepdims=True))
            p_locs.append(p_loc.astype(BF16))
            p_ctxs.append(p_ctx.astype(BF16))
        o = (_dot(jnp.concatenate(p_locs, axis=1), _block_diag_heads(vw[:, lanes], hb))
             + _dot(jnp.concatenate(p_ctxs, axis=1), _block_diag_heads(cv[:, lanes], hb)))
        o_ref[:, lanes] = (o / _per_head(ds, o.shape)).astype(BF16)


def _na_lat(z, cache_na, bias, l):
    tq = NA_QROWS * GRID_W
    nsteps = GRID_ROWS // NA_QROWS
    lat_blk = N_CTX_TOK // DEC_SEQ
    return pl.pallas_call(
        _na_lat_kernel,
        out_shape=jax.ShapeDtypeStruct((N_LAT_TOK, NA_WIDTH), BF16),
        grid=(DEC_BATCH, nsteps),
        in_specs=[
            pl.BlockSpec((tq, NA_WIDTH), lambda b, i: (N_CTX_TOK // tq + b * nsteps + i, COL_NA_Q // NA_WIDTH)),
            pl.BlockSpec((DEC_SEQ, NA_WIDTH), lambda b, i: (lat_blk + b, COL_NA_K // NA_WIDTH)),
            pl.BlockSpec((DEC_SEQ, NA_WIDTH), lambda b, i: (lat_blk + b, COL_NA_V // NA_WIDTH)),
            pl.BlockSpec((None, None, None, PAST_LEN, NA_WIDTH), lambda b, i: (b, l, 0, 0, 0)),
            pl.BlockSpec((None, None, None, PAST_LEN, NA_WIDTH), lambda b, i: (b, l, 1, 0, 0)),
            pl.BlockSpec((NA_HEADS, 2 * NA_WIN_H, GRID_W, 2 * GRID_W), lambda b, i: (0, 0, 0, 0)),
        ],
        out_specs=pl.BlockSpec((tq, NA_WIDTH), lambda b, i: (b * nsteps + i, 0)),
        compiler_params=pltpu.CompilerParams(
            dimension_semantics=("parallel", "arbitrary"), vmem_limit_bytes=VMEM_LIMIT),
        name="na_lat",
    )(z, z, z, cache_na, cache_na, bias)


def _rope_tables():
    nf = HEAD_DIM // 4
    t = jnp.arange(DEC_SEQ)
    pos = jnp.stack([t // GRID_W, t % GRID_W], axis=-1).astype(F32)
    inv = ROPE_THETA ** (-jnp.arange(nf, dtype=F32) / nf)
    ang = pos[:, :, None] * inv
    cos = jnp.cos(ang)
    sin = jnp.sin(ang)
    cos_d = jnp.stack([cos, cos], axis=2).reshape(DEC_SEQ, HEAD_DIM)
    sin_d = jnp.stack([-sin, sin], axis=2).reshape(DEC_SEQ, HEAD_DIM)
    return cos_d, sin_d


def _rope(x, cos, sin_signed):
    n = x.shape[-1]
    nf = HEAD_DIM // 4
    lane = lax.broadcasted_iota(jnp.int32, x.shape, 1)
    first_half = (lane // nf) % 2 == 0
    partner = jnp.where(first_half, pltpu.roll(x, n - nf, 1), pltpu.roll(x, nf, 1))
    return x * cos + partner * sin_signed


def _swa_lat_kernel(sink_ref, q_ref, kv_ref, cq_ref, sq_ref, ckt_ref, skt_ref, ck_ref, cv_ref, o_ref, k_sc, v_sc):
    n = pl.program_id(1)
    scale = HEAD_DIM ** -0.5
    nwin = 3 * SWA_BLOCK

    @pl.when(n == 0)
    def _():
        kv = kv_ref[...]
        k_sc[...] = _rope(kv[:, :GQA_KV_WIDTH].astype(F32), ckt_ref[...], skt_ref[...]).astype(BF16)
        v_sc[...] = kv[:, GQA_KV_WIDTH:].astype(BF16)

    q = (_rope(q_ref[...].astype(F32), cq_ref[...], sq_ref[...]) * scale).astype(BF16)
    start = pl.multiple_of(jnp.clip((n - 1) * SWA_BLOCK, 0, DEC_SEQ - nwin), SWA_BLOCK)
    kw = k_sc[pl.ds(start, nwin), :]
    vw = v_sc[pl.ds(start, nwin), :]
    ck = ck_ref[...].astype(BF16)
    cv = cv_ref[...].astype(BF16)
    rows = GQA_GROUP * SWA_BLOCK
    row = lax.broadcasted_iota(jnp.int32, (rows, nwin), 0)
    col = lax.broadcasted_iota(jnp.int32, (rows, nwin), 1)
    qpos = n * SWA_BLOCK + row % SWA_BLOCK
    kpos = start + col
    ok = jnp.abs(qpos - kpos) <= SWA_WINDOW
    grp = lax.broadcasted_iota(jnp.int32, (rows, 1), 0) // SWA_BLOCK
    for kh in range(GQA_KV_HEADS):
        sl = slice(kh * HEAD_DIM, (kh + 1) * HEAD_DIM)
        q4 = jnp.concatenate(
            [q[:, (kh * GQA_GROUP + g) * HEAD_DIM:(kh * GQA_GROUP + g + 1) * HEAD_DIM] for g in range(GQA_GROUP)],
            axis=0)
        sink = jnp.zeros((rows, 1), F32)
        for g in range(GQA_GROUP):
            sink = jnp.where(grp == g, sink_ref[kh * GQA_GROUP + g], sink)
        s_loc = jnp.where(ok, _dot_nt(q4, kw[:, sl]), NEG_INF)
        s_ctx = _dot_nt(q4, ck[:, sl])
        m = jnp.maximum(jnp.maximum(jnp.max(s_loc, axis=-1, keepdims=True),
                                    jnp.max(s_ctx, axis=-1, keepdims=True)), sink)
        p_loc = jnp.exp(s_loc - m)
        p_ctx = jnp.exp(s_ctx - m)
        d = (jnp.sum(p_loc, axis=-1, keepdims=True) + jnp.sum(p_ctx, axis=-1, keepdims=True)
             + jnp.exp(sink - m))
        o4 = (_dot(p_loc.astype(BF16), vw[:, sl]) + _dot(p_ctx.astype(BF16), cv[:, sl])) / d
        for g in range(GQA_GROUP):
            h = kh * GQA_GROUP + g
            o_ref[:, h * HEAD_DIM:(h + 1) * HEAD_DIM] = o4[g * SWA_BLOCK:(g + 1) * SWA_BLOCK].astype(BF16)


def _swa_lat(z, cache_gqa, sink, cos_q, sin_q, cos_k, sin_k, l):
    nb = DEC_SEQ // SWA_BLOCK
    lat_blk = N_CTX_TOK // DEC_SEQ
    return pl.pallas_call(
        _swa_lat_kernel,
        out_shape=jax.ShapeDtypeStruct((N_LAT_TOK, GQA_Q_WIDTH), BF16),
        grid=(DEC_BATCH, nb),
        in_specs=[
            pl.BlockSpec(memory_space=pltpu.SMEM),
            pl.BlockSpec((SWA_BLOCK, GQA_Q_WIDTH),
                         lambda b, n: (N_CTX_TOK // SWA_BLOCK + b * nb + n, COL_G_Q // GQA_Q_WIDTH)),
            pl.BlockSpec((DEC_SEQ, 2 * GQA_KV_WIDTH), lambda b, n: (lat_blk + b, COL_G_K // (2 * GQA_KV_WIDTH))),
            pl.BlockSpec((SWA_BLOCK, GQA_Q_WIDTH), lambda b, n: (n, 0)),
            pl.BlockSpec((SWA_BLOCK, GQA_Q_WIDTH), lambda b, n: (n, 0)),
            pl.BlockSpec((DEC_SEQ, GQA_KV_WIDTH), lambda b, n: (0, 0)),
            pl.BlockSpec((DEC_SEQ, GQA_KV_WIDTH), lambda b, n: (0, 0)),
            pl.BlockSpec((None, None, None, PAST_LEN, GQA_KV_WIDTH), lambda b, n: (b, l, 0, 0, 0)),
            pl.BlockSpec((None, None, None, PAST_LEN, GQA_KV_WIDTH), lambda b, n: (b, l, 1, 0, 0)),
        ],
        out_specs=pl.BlockSpec((SWA_BLOCK, GQA_Q_WIDTH), lambda b, n: (b * nb + n, 0)),
        scratch_shapes=[pltpu.VMEM((DEC_SEQ, GQA_KV_WIDTH), BF16), pltpu.VMEM((DEC_SEQ, GQA_KV_WIDTH), BF16)],
        compiler_params=pltpu.CompilerParams(
            dimension_semantics=("parallel", "arbitrary"), vmem_limit_bytes=VMEM_LIMIT),
        name="swa_lat",
    )(sink, z, z, cos_q, sin_q, cos_k, sin_k, cache_gqa, cache_gqa)


def _s5_operators(a_re, a_im, log_dt, b_re, b_im, c_re, c_im):
    T, G, N, C, P = S5_T, S5_GROUPS, S5_STATE, S5_GROUP_CH, S5_PAIRS
    tau = jnp.arange(T + 1, dtype=F32)
    eye2 = jnp.eye(2, dtype=F32)
    out = {k: [] for k in ("bs", "m", "cre", "cim", "apr", "api")}

    def pairs(x):
        return x.reshape((P, 2) + x.shape[1:])

    def diag(x, axis):
        e = eye2.reshape((1, 2) + (1,) * (axis - 1) + (2,) + (1,) * (x.ndim - axis - 1))
        return jnp.expand_dims(x, axis + 1) * e

    for d in range(2):
        A = lax.complex(a_re[d].astype(F32), a_im[d].astype(F32))
        dt = jnp.exp(log_dt[d].astype(F32))[:, None]
        a_bar = jnp.exp(A * dt)
        pw = jnp.exp((A * dt)[None] * tau[:, None, None])
        b_bar = ((a_bar - 1.0) / A)[..., None] * lax.complex(b_re[d].astype(F32), b_im[d].astype(F32))
        c_mat = lax.complex(c_re[d].astype(F32), c_im[d].astype(F32))
        kern = jnp.einsum("gon,tgn,gni->gtoi", c_mat, pw[:T], b_bar, precision=lax.Precision.HIGHEST).real
        k0 = diag(pairs(kern).transpose(0, 1, 4, 2, 3), 3).reshape(P, 2 * C, T * 2 * C)
        zero = jnp.zeros_like(k0)
        if d == 0:
            p_in = pw[:T][::-1]
            p_out = pw[1:T + 1]
            kp = jnp.concatenate([zero, k0], axis=2)
            m = [kp[:, :, (T - s) * 2 * C:(2 * T - s) * 2 * C] for s in range(T)]
        else:
            p_in = pw[:T]
            p_out = pw[1:T + 1][::-1]
            kp = jnp.concatenate([k0.reshape(P, 2 * C, T, 2 * C)[:, :, ::-1].reshape(k0.shape), zero], axis=2)
            m = [kp[:, :, (T - 1 - s) * 2 * C:(2 * T - 1 - s) * 2 * C] for s in range(T)]
        out["m"].append(jnp.concatenate(m, axis=1))
        pin = p_in.reshape(T, P, 2 * N).transpose(1, 0, 2)[:, :, None, :]
        bx = diag(pairs(b_bar.transpose(0, 2, 1)), 2).reshape(P, 1, 2 * C, 2 * N)
        bs = pin * bx
        out["bs"].append(jnp.concatenate([bs.real, bs.imag], axis=-1).reshape(P, 2 * T * C, 4 * N))
        cx = diag(pairs(c_mat.transpose(0, 2, 1)), 2).reshape(P, 2 * N, 2 * C)
        pout = p_out.reshape(T, P, 2 * N).transpose(1, 2, 0)
        co = jnp.tile(cx, (1, 1, T)) * jnp.repeat(pout, 2 * C, axis=2)
        out["cre"].append(co.real)
        out["cim"].append(-co.imag)
        pw2 = jnp.exp((A * dt)[None] * (T * 2.0 ** jnp.arange(S5_NPOW, dtype=F32))[:, None, None])
        out["apr"].append(pw2.real.reshape(S5_NPOW, G * N))
        out["api"].append(pw2.imag.reshape(S5_NPOW, G * N))
    return {k: jnp.stack(v) for k, v in out.items()}


def _s5_kernel(u_ref, bs_ref, m_ref, cre_ref, cim_ref, apr_ref, api_ref, h0_ref, y_ref, fin_ref, fin_sc):
    t = pl.program_id(1)
    nctx = N_CTX_TOK // S5_BLOCK_TOK
    refs = (u_ref, bs_ref, m_ref, cre_ref, cim_ref, apr_ref, api_ref, h0_ref, y_ref, fin_ref, fin_sc)

    @pl.when(t < nctx)
    def _():
        _s5_block(*refs, t, kseq=SEQ // S5_T, has_h0=False)

    @pl.when(t >= nctx)
    def _():
        _s5_block(*refs, t - nctx, kseq=DEC_SEQ // S5_T, has_h0=True)


def _s5_block(u_ref, bs_ref, m_ref, cre_ref, cim_ref, apr_ref, api_ref, h0_ref, y_ref, fin_ref, fin_sc, t, *,
              kseq, has_h0):
    rows, half, pc = S5_ROWS, 2 * S5_STATE, 2 * S5_GROUP_CH
    k = lax.broadcasted_iota(jnp.int32, (rows, half), 0) % kseq
    shifts = [1 << i for i in range(kseq.bit_length() - 1)]
    xs = [u_ref[pl.ds(s, rows, stride=S5_T), :].astype(BF16) for s in range(S5_T)]
    ys = []
    for p in range(S5_OCT_PAIRS):
        lanes = slice(p * half, (p + 1) * half)
        xp = jnp.concatenate([x[:, p * pc:(p + 1) * pc] for x in xs], axis=1)
        acc = None
        for d in range(2):
            inc = _dot(xp, bs_ref[d, p])
            sr, si = inc[:, :half], inc[:, half:]
            if has_h0:
                h0r = h0_ref[d, 0, t, :, lanes]
                h0i = h0_ref[d, 1, t, :, lanes]
                ar, ai = apr_ref[d, 0:1, lanes], api_ref[d, 0:1, lanes]
                first = (k == 0) if d == 0 else (k == kseq - 1)
                sr = sr + jnp.where(first, ar * h0r - ai * h0i, 0.0)
                si = si + jnp.where(first, ar * h0i + ai * h0r, 0.0)
            else:
                h0r = h0i = 0.0
            for i, sh in enumerate(shifts):
                ar, ai = apr_ref[d, i:i + 1, lanes], api_ref[d, i:i + 1, lanes]
                ok = (k >= sh) if d == 0 else (k < kseq - sh)
                amt = sh if d == 0 else rows - sh
                rr = jnp.where(ok, pltpu.roll(sr, amt, 0), 0.0)
                ri = jnp.where(ok, pltpu.roll(si, amt, 0), 0.0)
                sr, si = sr + ar * rr - ai * ri, si + ar * ri + ai * rr
            inner = (k >= 1) if d == 0 else (k < kseq - 1)
            amt = 1 if d == 0 else rows - 1
            hr = jnp.where(inner, pltpu.roll(sr, amt, 0), h0r)
            hi = jnp.where(inner, pltpu.roll(si, amt, 0), h0i)
            yd = (_dot(xp, m_ref[d, p]) + _dot(hr.astype(BF16), cre_ref[d, p])
                  + _dot(hi.astype(BF16), cim_ref[d, p]))
            acc = yd if acc is None else acc + yd
            if not has_h0:
                last = kseq - 1 if d == 0 else 0
                fin_sc[0] = sr
                fin_sc[1] = si
                fin_ref[d, 0, :, lanes] = fin_sc[0, pl.ds(last, rows // kseq, stride=kseq), :]
                fin_ref[d, 1, :, lanes] = fin_sc[1, pl.ds(last, rows // kseq, stride=kseq), :]
        ys.append(acc)
    for j in range(S5_T):
        y_ref[pl.ds(j, rows, stride=S5_T), :] = jnp.concatenate([y[:, j * pc:(j + 1) * pc] for y in ys], axis=1)


def _s5_scan(u, ops, h0, l):
    nblk, nctx = N_TOK // S5_BLOCK_TOK, N_CTX_TOK // S5_BLOCK_TOK
    noct = S5_PAIRS // S5_OCT_PAIRS
    half, gn = 2 * S5_STATE, S5_GROUPS * S5_STATE
    pw = 2 * S5_T * S5_GROUP_CH
    nseq = S5_BLOCK_TOK // SEQ
    wspec = lambda r, c: pl.BlockSpec((None, 2, S5_OCT_PAIRS, r, c), lambda q, t: (l, 0, q, 0, 0))
    aspec = pl.BlockSpec((None, 2, S5_NPOW, S5_OCT_PAIRS * half), lambda q, t: (l, 0, 0, q))
    return pl.pallas_call(
        _s5_kernel,
        grid=(noct, nblk),
        in_specs=[
            pl.BlockSpec((S5_BLOCK_TOK, 128), lambda q, t: (t, q)),
            wspec(pw, 2 * half), wspec(pw, pw), wspec(half, pw), wspec(half, pw), aspec, aspec,
            pl.BlockSpec((2, 2, DEC_BATCH, 1, S5_OCT_PAIRS * half), lambda q, t: (0, 0, 0, 0, q)),
        ],
        out_shape=(jax.ShapeDtypeStruct((N_TOK, S5_CH), F32), jax.ShapeDtypeStruct((2, 2, BATCH, gn), F32)),
        out_specs=(pl.BlockSpec((S5_BLOCK_TOK, 128), lambda q, t: (t, q)),
                   pl.BlockSpec((2, 2, nseq, S5_OCT_PAIRS * half),
                                lambda q, t: (0, 0, jnp.minimum(t, nctx - 1), q))),
        scratch_shapes=[pltpu.VMEM((2, S5_ROWS, half), F32)],
        compiler_params=pltpu.CompilerParams(
            dimension_semantics=("parallel", "arbitrary"), vmem_limit_bytes=VMEM_LIMIT),
        name="s5_scan",
    )(u, ops["bs"], ops["m"], ops["cre"], ops["cim"], ops["apr"], ops["api"],
      h0.reshape(2, 2, DEC_BATCH, 1, gn))


def _route(logits):
    lane = lax.broadcasted_iota(jnp.int32, logits.shape, 1)
    big = jnp.int32(1 << 20)
    is_g = (lane >= MOE_EXPERTS) & (lane < MOE_EXPERTS + MOE_GROUPS)
    lg = jnp.where(is_g, logits, NEG_INF)
    gmax = jnp.max(lg, axis=-1, keepdims=True)
    gsel = jnp.min(jnp.where(is_g & (lg == gmax), lane, big), axis=-1, keepdims=True) - MOE_EXPERTS
    p_group = 1.0 / jnp.sum(jnp.where(is_g, jnp.exp(lg - gmax), 0.0), axis=-1, keepdims=True)
    in_grp = (lane < MOE_EXPERTS) & (lane // MOE_EXPERTS_PER_GROUP == gsel)
    le = jnp.where(in_grp, logits, NEG_INF)
    v1 = jnp.max(le, axis=-1, keepdims=True)
    i1 = jnp.min(jnp.where(in_grp & (le == v1), lane, big), axis=-1, keepdims=True)
    rest = in_grp & (lane != i1)
    le2 = jnp.where(rest, logits, NEG_INF)
    v2 = jnp.max(le2, axis=-1, keepdims=True)
    i2 = jnp.min(jnp.where(rest & (le2 == v2), lane, big), axis=-1, keepdims=True)
    e2 = jnp.exp(v2 - v1)
    w1 = 1.0 / (1.0 + e2)
    w2 = e2 / (1.0 + e2)
    comb = jnp.where(lane == i1, w1 * p_group, 0.0) + jnp.where(lane == i2, w2 * p_group, 0.0)
    return jnp.where(lane == MOE_EXPERTS, gsel.astype(F32), comb)


def _merge_kernel(xc_ref, xl_ref, oac_ref, oal_ref, obc_ref, obl_ref, u_ref, yc_ref, gates_ref, mod_ref, d_ref,
                  wglu_ref, wa_ref, wb_ref, wc_ref, wout_ref, g2_ref, wrh_ref, wrl_ref, br_ref, x1_ref, hx_ref,
                  *, tm):
    is_ctx = pl.program_id(0) < N_CTX_TOK // tm
    x = jnp.where(is_ctx, xc_ref[...], xl_ref[...])
    oa = jnp.where(is_ctx, oac_ref[...], oal_ref[...])
    ob = jnp.where(is_ctx, obc_ref[...], obl_ref[...])
    y = u_ref[...] * d_ref[...] + yc_ref[...]
    y = y * (0.5 * (1.0 + jnp.tanh(math.sqrt(2.0 / math.pi) * (y + 0.044715 * (y * y * y)))))
    oc = y * jax.nn.sigmoid(_dot(y.astype(BF16), wglu_ref[...]))
    gate = jax.nn.sigmoid(gates_ref[...].astype(F32))
    merged = (gate[:, :D_MODEL] * _dot(oa, wa_ref[...])
              + gate[:, D_MODEL:2 * D_MODEL] * _dot(ob, wb_ref[...])
              + gate[:, 2 * D_MODEL:] * _dot(oc.astype(BF16), wc_ref[...]))
    x1 = x + mod_ref[2:3, :] * _dot(merged.astype(BF16), wout_ref[...])
    x1_ref[...] = x1
    h2 = _rms(x1, g2_ref[...]) * (1.0 + mod_ref[4:5, :]) + mod_ref[3:4, :]
    h_hi, h_lo = _split_bf16(h2)
    logits = _dot(h_hi, wrh_ref[...]) + _dot(h_lo, wrh_ref[...]) + _dot(h_hi, wrl_ref[...]) + br_ref[...]
    half = D_MODEL // 2
    hb = h_hi.astype(F32)
    word = (pltpu.bitcast(hb[:, :half], jnp.uint32) & jnp.uint32(0xFFFF0000)) | (
        pltpu.bitcast(hb[:, half:], jnp.uint32) >> 16)
    hx_ref[:, :half] = word
    hx_ref[:, half:] = pltpu.bitcast(_route(logits), jnp.uint32)


def _merge(x, oa_c, oa_l, ob_c, ob_l, u, yc, z, mod, d_s5, wglu, wa, wb, wc, wout, g2, wr_hi, wr_lo, br, l):
    tm = 512
    nctx = N_CTX_TOK // tm
    full = lambda r, c: pl.BlockSpec((r, c), lambda i: (0, 0))
    layer = lambda r, c: pl.BlockSpec((None, r, c), lambda i: (l, 0, 0))
    ctx_blk = lambda w: pl.BlockSpec((tm, w), lambda i: (jnp.minimum(i, nctx - 1), 0))
    lat_blk = lambda w, first=0: pl.BlockSpec((tm, w), lambda i: (jnp.maximum(i - nctx, 0) + first // tm, 0))
    x_ctx, x_lat, lat0 = _token_arrays(x)
    return pl.pallas_call(
        functools.partial(_merge_kernel, tm=tm),
        out_shape=(jax.ShapeDtypeStruct((N_TOK, D_MODEL), F32),
                   jax.ShapeDtypeStruct((N_TOK, MOE_ROW_WORDS), jnp.uint32)),
        grid=(N_TOK // tm,),
        in_specs=[
            ctx_blk(D_MODEL), lat_blk(D_MODEL, lat0),
            ctx_blk(NA_WIDTH), lat_blk(NA_WIDTH), ctx_blk(GQA_Q_WIDTH), lat_blk(GQA_Q_WIDTH),
            pl.BlockSpec((tm, S5_CH), lambda i: (i, 0)),
            pl.BlockSpec((tm, S5_CH), lambda i: (i, 0)),
            pl.BlockSpec((tm, N_BRANCH * D_MODEL), lambda i: (i, COL_GATES // (N_BRANCH * D_MODEL))),
            pl.BlockSpec((None, None, 6, D_MODEL), lambda i: (l, _cond_index(i, tm), 0, 0)),
            full(1, S5_CH), layer(S5_CH, S5_CH), layer(NA_WIDTH, D_MODEL), layer(GQA_Q_WIDTH, D_MODEL),
            layer(S5_CH, D_MODEL), layer(D_MODEL, D_MODEL), full(1, D_MODEL),
            full(D_MODEL, 128), full(D_MODEL, 128), full(1, 128),
        ],
        out_specs=(pl.BlockSpec((tm, D_MODEL), lambda i: (i, 0)),
                   pl.BlockSpec((tm, MOE_ROW_WORDS), lambda i: (i, 0))),
        compiler_params=pltpu.CompilerParams(
            dimension_semantics=("parallel",), vmem_limit_bytes=VMEM_LIMIT),
        name="merge",
    )(x_ctx, x_lat, oa_c, oa_l, ob_c, ob_l, u, yc, z, mod, d_s5, wglu, wa, wb, wc, wout, g2, wr_hi, wr_lo, br)


def _moe_plan(gsel):
    onehot = (gsel[:, None] == jnp.arange(MOE_GROUPS)[None, :]).astype(jnp.int32)
    rank = jnp.sum((jnp.cumsum(onehot, axis=0) - onehot) * onehot, axis=1)
    count = jnp.sum(onehot, axis=0)
    padded = (count + MOE_TILE - 1) // MOE_TILE * MOE_TILE
    end = jnp.cumsum(padded)
    pos = jnp.sum(onehot * (end - padded)[None, :], axis=1) + rank
    tile_row = jnp.arange(MOE_ROWS // MOE_TILE) * MOE_TILE
    tile_group = jnp.minimum(jnp.sum((tile_row[:, None] >= end[None, :]).astype(jnp.int32), axis=1),
                             MOE_GROUPS - 1)
    return pos.astype(jnp.int32), tile_group.astype(jnp.int32), (end[-1:] // MOE_TILE).astype(jnp.int32)


def _dispatch_kernel(pos_ref, hx_ref, init_ref, out_ref, sem):
    del init_ref
    base = pl.program_id(0) * MOE_TOK_BLOCK

    def row_copy(r):
        return pltpu.make_async_copy(hx_ref.at[pl.ds(r, 1), :], out_ref.at[pl.ds(pos_ref[base + r], 1), :], sem)

    def start(r, c):
        row_copy(r).start()
        return c

    def wait(r, c):
        row_copy(r).wait()
        return c

    lax.fori_loop(0, MOE_TOK_BLOCK, start, 0, unroll=16)
    lax.fori_loop(0, MOE_TOK_BLOCK, wait, 0, unroll=True)


def _dispatch(hx, pos):
    return pl.pallas_call(
        _dispatch_kernel,
        grid_spec=pltpu.PrefetchScalarGridSpec(
            num_scalar_prefetch=1,
            grid=(N_TOK // MOE_TOK_BLOCK,),
            in_specs=[pl.BlockSpec((MOE_TOK_BLOCK, MOE_ROW_WORDS), lambda i, pos: (i, 0)),
                      pl.BlockSpec(memory_space=pl.ANY)],
            out_specs=pl.BlockSpec(memory_space=pl.ANY),
            scratch_shapes=[pltpu.SemaphoreType.DMA]),
        out_shape=jax.ShapeDtypeStruct((MOE_ROWS, MOE_ROW_WORDS), jnp.uint32),
        input_output_aliases={2: 0},
        compiler_params=pltpu.CompilerParams(dimension_semantics=("arbitrary",)),
        name="moe_dispatch",
    )(pos, hx, jnp.zeros((MOE_ROWS, MOE_ROW_WORDS), jnp.uint32))


def _experts_kernel(tg_ref, nu_ref, hx_ref, wg_ref, wu_ref, wd_ref, y_ref):
    t = pl.program_id(0)

    @pl.when(t >= nu_ref[0])
    def _():
        y_ref[...] = jnp.zeros_like(y_ref)

    @pl.when(t < nu_ref[0])
    def _():
        half = D_MODEL // 2
        word = hx_ref[:, :half]
        h = jnp.concatenate([pltpu.bitcast(word & jnp.uint32(0xFFFF0000), F32).astype(BF16),
                             pltpu.bitcast(word << 16, F32).astype(BF16)], axis=1)
        comb = pltpu.bitcast(hx_ref[:, half:], F32)
        lane = lax.broadcasted_iota(jnp.int32, comb.shape, 1)
        first = tg_ref[t] * MOE_EXPERTS_PER_GROUP
        acc = None
        for e in range(MOE_EXPERTS_PER_GROUP):
            a = _dot(h, wg_ref[e])
            b = _dot(h, wu_ref[e])
            ce = jnp.sum(jnp.where(lane == first + e, comb, 0.0), axis=-1, keepdims=True)
            act = (a * jax.nn.sigmoid(a)) * b * ce
            c = _dot(act.astype(BF16), wd_ref[e])
            acc = c if acc is None else acc + c
        y_ref[...] = acc


def _experts(hs, tile_group, n_used, wg, wu, wd, l):
    wspec = lambda r, c: pl.BlockSpec((None, None, MOE_EXPERTS_PER_GROUP, r, c),
                                      lambda t, tg, nu: (l, tg[t], 0, 0, 0))
    return pl.pallas_call(
        _experts_kernel,
        grid_spec=pltpu.PrefetchScalarGridSpec(
            num_scalar_prefetch=2,
            grid=(MOE_ROWS // MOE_TILE,),
            in_specs=[pl.BlockSpec((MOE_TILE, MOE_ROW_WORDS), lambda t, tg, nu: (t, 0)),
                      wspec(D_MODEL, EXPERT_FF), wspec(D_MODEL, EXPERT_FF), wspec(EXPERT_FF, D_MODEL)],
            out_specs=pl.BlockSpec((MOE_TILE, D_MODEL), lambda t, tg, nu: (t, 0))),
        out_shape=jax.ShapeDtypeStruct((MOE_ROWS, D_MODEL), F32),
        compiler_params=pltpu.CompilerParams(
            dimension_semantics=("arbitrary",), vmem_limit_bytes=VMEM_LIMIT),
        name="moe_experts",
    )(tile_group, n_used, hs, wg, wu, wd)


def _combine_kernel(pos_ref, y_ref, x_ref, mod_ref, fg_ref, *rest, final):
    buf, sem = rest[-2:]
    i = pl.program_id(0)
    base = i * MOE_TOK_BLOCK

    def row_copy(r):
        return pltpu.make_async_copy(y_ref.at[pl.ds(pos_ref[base + r], 1), :], buf.at[pl.ds(r, 1), :], sem)

    def start(r, c):
        row_copy(r).start()
        return c

    def wait(r, c):
        row_copy(r).wait()
        return c

    lax.fori_loop(0, MOE_TOK_BLOCK, start, 0, unroll=16)
    lax.fori_loop(0, MOE_TOK_BLOCK, wait, 0, unroll=True)
    x2 = x_ref[...] + mod_ref[5:6, :] * buf[...]
    if not final:
        rest[0][...] = x2
    else:
        y = _rms(x2, fg_ref[...])
        is_ctx = i < N_CTX_TOK // MOE_TOK_BLOCK

        @pl.when(is_ctx)
        def _():
            rest[0][...] = y

        @pl.when(jnp.logical_not(is_ctx))
        def _():
            rest[1][...] = y


def _combine(ys, pos, x1, mod, fg, l, final):
    tm = MOE_TOK_BLOCK
    nctx = N_CTX_TOK // tm
    tok = pl.BlockSpec((tm, D_MODEL), lambda i, pos: (i, 0))
    out = jax.ShapeDtypeStruct((N_TOK, D_MODEL), F32)
    if final:
        out = (jax.ShapeDtypeStruct((N_CTX_TOK, D_MODEL), F32), jax.ShapeDtypeStruct((N_LAT_TOK, D_MODEL), F32))
        out_specs = (pl.BlockSpec((tm, D_MODEL), lambda i, pos: (jnp.minimum(i, nctx - 1), 0)),
                     pl.BlockSpec((tm, D_MODEL), lambda i, pos: (jnp.maximum(i - nctx, 0), 0)))
    else:
        out_specs = tok
    return pl.pallas_call(
        functools.partial(_combine_kernel, final=final),
        grid_spec=pltpu.PrefetchScalarGridSpec(
            num_scalar_prefetch=1,
            grid=(N_TOK // tm,),
            in_specs=[pl.BlockSpec(memory_space=pl.ANY), tok,
                      pl.BlockSpec((None, None, 6, D_MODEL), lambda i, pos: (l, _cond_index(i, tm), 0, 0)),
                      pl.BlockSpec((1, D_MODEL), lambda i, pos: (0, 0))],
            out_specs=out_specs,
            scratch_shapes=[pltpu.VMEM((tm, D_MODEL), F32), pltpu.SemaphoreType.DMA]),
        out_shape=out,
        compiler_params=pltpu.CompilerParams(
            dimension_semantics=("arbitrary",), vmem_limit_bytes=VMEM_LIMIT),
        name="moe_combine",
    )(pos, ys, x1, mod, fg)


def _pack_w_in(w):
    qkv = w[..., :COL_G_V + GQA_KV_WIDTH]
    u = w[..., COL_G_V + GQA_KV_WIDTH:COL_G_V + GQA_KV_WIDTH + S5_CH]
    gates = w[..., COL_G_V + GQA_KV_WIDTH + S5_CH:]
    pad = jnp.zeros(w.shape[:-1] + (COL_U - (COL_G_V + GQA_KV_WIDTH),), w.dtype)
    return jnp.concatenate([qkv, pad, u, gates], axis=-1).astype(BF16)


def kernel(x_prompt, x_sample, cache_na_kv, cache_gqa_kv, state_ssm, c, c_ctx, norm_g, w_ada, b_ada, w_in, na_rpb, gqa_sink, s5_a_re, s5_a_im, s5_log_dt, s5_b_re, s5_b_im, s5_c_re, s5_c_im, s5_d, s5_w_glu, w_branch_a, w_branch_b, w_branch_c, w_out, moe_w_group, moe_b_group, moe_w_expert, moe_b_expert, moe_w_gate, moe_w_up, moe_w_down, final_g):
    cond = jnp.zeros((N_COND, D_MODEL), F32).at[0].set(c_ctx.astype(F32)).at[1:1 + DEC_BATCH].set(c.astype(F32))
    mod = _ada_mod(cond, w_ada.astype(F32), b_ada.astype(F32))

    x = (x_prompt.astype(F32).reshape(N_CTX_TOK, D_MODEL), x_sample.astype(F32).reshape(N_LAT_TOK, D_MODEL))
    cache_na = cache_na_kv.reshape(DEC_BATCH, DEPTH, 2, PAST_LEN, NA_WIDTH)
    cache_gqa = cache_gqa_kv.reshape(DEC_BATCH, DEPTH, 2, PAST_LEN, GQA_KV_WIDTH)
    cos_d, sin_d = _rope_tables()
    cos_q, sin_q = jnp.tile(cos_d, (1, GQA_Q_HEADS)), jnp.tile(sin_d, (1, GQA_Q_HEADS))
    cos_k, sin_k = jnp.tile(cos_d, (1, GQA_KV_HEADS)), jnp.tile(sin_d, (1, GQA_KV_HEADS))
    gn = S5_GROUPS * S5_STATE
    fg = final_g.astype(F32).reshape(1, D_MODEL)

    w_in_packed = _pack_w_in(w_in)
    bf16_weights = [w.astype(BF16) for w in (s5_w_glu, w_branch_a, w_branch_b, w_branch_c, w_out)]
    grouped = lambda w: w.astype(BF16).reshape((DEPTH, MOE_GROUPS, MOE_EXPERTS_PER_GROUP) + w.shape[2:])
    moe_weights = [grouped(w) for w in (moe_w_gate, moe_w_up, moe_w_down)]
    ops = jax.vmap(_s5_operators)(s5_a_re, s5_a_im, s5_log_dt, s5_b_re, s5_b_im, s5_c_re, s5_c_im)
    ops = {k: (v.astype(BF16) if k in ("bs", "m", "cre", "cim") else v) for k, v in ops.items()}

    na_list, gqa_list, ssm_list = [], [], []
    y = None
    for l in range(DEPTH):
        z, u, new_na, new_gqa = _inproj(x, norm_g[l, 0].astype(F32).reshape(1, D_MODEL), mod, w_in_packed, l)
        na_list.append(new_na.reshape(BATCH, 2, SEQ, NA_HEADS, HEAD_DIM))
        gqa_list.append(new_gqa.reshape(BATCH, 2, SEQ, GQA_KV_HEADS, HEAD_DIM))

        sink = gqa_sink[l].astype(F32)
        oa_c, ob_c = _ctx_attn(z, sink)
        oa_l = _na_lat(z, cache_na, _na_bias_tables(na_rpb[l]), l)
        ob_l = _swa_lat(z, cache_gqa, sink, cos_q, sin_q, cos_k, sin_k, l)

        h0 = state_ssm[:, l].astype(F32).reshape(DEC_BATCH, 2, 2, gn).transpose(1, 2, 0, 3)
        yc, fin = _s5_scan(u, ops, h0, l)
        ssm_list.append(fin.transpose(2, 0, 1, 3).reshape(BATCH, 2, 2, S5_GROUPS, S5_STATE).astype(x_prompt.dtype))

        wr = jnp.zeros((D_MODEL, 128), F32)
        wr = wr.at[:, :MOE_EXPERTS].set(moe_w_expert[l].astype(F32))
        wr = wr.at[:, MOE_EXPERTS:MOE_EXPERTS + MOE_GROUPS].set(moe_w_group[l].astype(F32))
        br = jnp.zeros((1, 128), F32)
        br = br.at[0, :MOE_EXPERTS].set(moe_b_expert[l].astype(F32))
        br = br.at[0, MOE_EXPERTS:MOE_EXPERTS + MOE_GROUPS].set(moe_b_group[l].astype(F32))
        wr_hi, wr_lo = _split_bf16(wr)
        x1, hx = _merge(
            x, oa_c, oa_l, ob_c, ob_l, u, yc, z, mod, s5_d[l].astype(F32).reshape(1, S5_CH), *bf16_weights,
            norm_g[l, 1].astype(F32).reshape(1, D_MODEL), wr_hi, wr_lo, br, l)
        gsel = lax.bitcast_convert_type(hx[:, D_MODEL // 2 + MOE_EXPERTS], F32).astype(jnp.int32)
        pos, tile_group, n_used = _moe_plan(gsel)
        ys = _experts(_dispatch(hx, pos), tile_group, n_used, *moe_weights, l)
        if l < DEPTH - 1:
            x = _combine(ys, pos, x1, mod, fg, l, False)
        else:
            y_ctx, y_lat = _combine(ys, pos, x1, mod, fg, l, True)

    return (y_ctx.reshape(BATCH, SEQ, D_MODEL), y_lat.reshape(DEC_BATCH, DEC_SEQ, D_MODEL),
            jnp.stack(na_list, axis=1), jnp.stack(gqa_list, axis=1), jnp.stack(ssm_list, axis=1))
```

```python
import functools
import math

import numpy as np
import jax
import jax.numpy as jnp
from jax import lax
from jax.experimental import pallas as pl
from jax.experimental.pallas import tpu as pltpu

F32 = jnp.float32
BF16 = jnp.bfloat16

D_MODEL = 1024
BATCH = 32
SEQ = 256
DEPTH = 2
DEC_BATCH = 4
DEC_SEQ = 2048
PAST_LEN = 512
GRID_W = 64
GRID_ROWS = DEC_SEQ // GRID_W
HEAD_DIM = 64
NA_HEADS = 8
NA_WIN_H = 8
NA_WIN_W = 16
GQA_Q_HEADS = 8
GQA_KV_HEADS = 2
GQA_GROUP = GQA_Q_HEADS // GQA_KV_HEADS
SWA_WINDOW = 128
SWA_BLOCK = 128
ROPE_THETA = 10000.0
S5_CH = 512
S5_GROUP_CH = 16
S5_GROUPS = S5_CH // S5_GROUP_CH
S5_STATE = 64
N_BRANCH = 3
NA_WIDTH = NA_HEADS * HEAD_DIM
GQA_Q_WIDTH = GQA_Q_HEADS * HEAD_DIM
GQA_KV_WIDTH = GQA_KV_HEADS * HEAD_DIM
MOE_GROUPS = 4
MOE_EXPERTS_PER_GROUP = 4
MOE_EXPERTS = MOE_GROUPS * MOE_EXPERTS_PER_GROUP
EXPERT_FF = 512
EPS = 1e-6
NEG_INF = -1e30

N_CTX_TOK = BATCH * SEQ
N_LAT_TOK = DEC_BATCH * DEC_SEQ
N_TOK = N_CTX_TOK + N_LAT_TOK
N_COND = 8

COL_NA_Q = 0
COL_NA_K = 512
COL_NA_V = 1024
COL_G_Q = 1536
COL_G_K = 2048
COL_G_V = 2176
COL_U = 2560
COL_GATES = 3072
Z_COLS = 6144

S5_T = 16
S5_PAIRS = S5_GROUPS // 2
S5_OCT_PAIRS = 4
S5_BLOCK_TOK = 2048
S5_ROWS = S5_BLOCK_TOK // S5_T
S5_NPOW = 7

MOE_ROW_WORDS = D_MODEL // 2 + 128
MOE_TILE = 512
MOE_ROWS = N_TOK + MOE_GROUPS * MOE_TILE
MOE_TOK_BLOCK = 512

CTX_HEADS_PER_DOT = 4
NA_HEADS_PER_DOT = 2
NA_QROWS = 4
NA_KROWS = 12

VMEM_LIMIT = 56 * 1024 * 1024


def _dot(a, b):
    return jnp.dot(a, b, preferred_element_type=F32)


def _dot_nt(a, b):
    return lax.dot_general(a, b, (((1,), (1,)), ((), ())), preferred_element_type=F32)


def _split_bf16(x):
    hi = x.astype(BF16)
    lo = (x - hi.astype(F32)).astype(BF16)
    return hi, lo


def _pack_pairs(x):
    w = x.shape[1] // 2
    xb = x.astype(BF16).astype(F32)
    return (pltpu.bitcast(xb[:, :w], jnp.uint32) & jnp.uint32(0xFFFF0000)) | (
        pltpu.bitcast(xb[:, w:], jnp.uint32) >> 16)


def _unpack_pairs(word):
    return jnp.concatenate([pltpu.bitcast(word & jnp.uint32(0xFFFF0000), F32),
                            pltpu.bitcast(word << 16, F32)], axis=1)


def _rms(x, g):
    return x * lax.rsqrt(jnp.mean(x * x, axis=-1, keepdims=True) + EPS) * g


def _cond_index(i, tm):
    nctx = N_CTX_TOK // tm
    return jnp.where(i < nctx, 0, 1 + ((i - nctx) * tm) // DEC_SEQ)


def _ada_kernel(c_ref, w_ref, b_ref, o_ref):
    c = c_ref[...]
    s = c * jax.nn.sigmoid(c)
    s_hi, s_lo = _split_bf16(s)
    w_hi, w_lo = _split_bf16(w_ref[...])
    o_ref[...] = _dot(s_hi, w_hi) + _dot(s_lo, w_hi) + _dot(s_hi, w_lo) + b_ref[...]


def _ada_mod(cond, w_ada, b_ada):
    tn = 1536
    n = 6 * D_MODEL
    out = pl.pallas_call(
        _ada_kernel,
        out_shape=jax.ShapeDtypeStruct((DEPTH, N_COND, n), F32),
        grid=(DEPTH, n // tn),
        in_specs=[
            pl.BlockSpec((N_COND, D_MODEL), lambda l, j: (0, 0)),
            pl.BlockSpec((None, D_MODEL, tn), lambda l, j: (l, 0, j)),
            pl.BlockSpec((None, 1, tn), lambda l, j: (l, 0, j)),
        ],
        out_specs=pl.BlockSpec((None, N_COND, tn), lambda l, j: (l, 0, j)),
        compiler_params=pltpu.CompilerParams(
            dimension_semantics=("parallel", "parallel"), vmem_limit_bytes=VMEM_LIMIT),
        name="ada_mod",
    )(cond, w_ada, b_ada.reshape(DEPTH, 1, n))
    return out.reshape(DEPTH, N_COND, 6, D_MODEL)


INPROJ_TM = 1024
INPROJ_TN = 1536


def _inproj_kernel(xc_ref, xl_ref, g_ref, mod_ref, w_ref, z_ref, u_ref, cna_ref, cgq_ref, h_sc):
    i, j = pl.program_id(0), pl.program_id(1)
    is_ctx = i < N_CTX_TOK // INPROJ_TM

    def norm_mod(x_ref):
        h = _rms(x_ref[...], g_ref[...]) * (1.0 + mod_ref[1:2, :]) + mod_ref[0:1, :]
        h_sc[...] = h.astype(BF16)

    @pl.when((j == 0) & is_ctx)
    def _():
        norm_mod(xc_ref)

    @pl.when((j == 0) & jnp.logical_not(is_ctx))
    def _():
        norm_mod(xl_ref)

    acc = _dot(h_sc[...], w_ref[...])
    z_ref[...] = acc.astype(BF16)

    def rows(b):
        return slice(b * SEQ, (b + 1) * SEQ)

    def cols(c, width):
        return slice(c % INPROJ_TN, c % INPROJ_TN + width)

    @pl.when(is_ctx & (j == COL_NA_K // INPROJ_TN))
    def _():
        for b in range(INPROJ_TM // SEQ):
            cna_ref[b, 0] = acc[rows(b), cols(COL_NA_K, NA_WIDTH)]
            cna_ref[b, 1] = acc[rows(b), cols(COL_NA_V, NA_WIDTH)]

    @pl.when(is_ctx & (j == COL_G_K // INPROJ_TN))
    def _():
        for b in range(INPROJ_TM // SEQ):
            cgq_ref[b, 0] = acc[rows(b), cols(COL_G_K, GQA_KV_WIDTH)]
            cgq_ref[b, 1] = acc[rows(b), cols(COL_G_V, GQA_KV_WIDTH)]

    @pl.when(j == COL_U // INPROJ_TN)
    def _():
        u_ref[...] = acc[:, cols(COL_U, S5_CH)]


def _token_arrays(x):
    return (x, x, N_CTX_TOK) if not isinstance(x, tuple) else (x[0], x[1], 0)


def _inproj(x, g, mod, w, l):
    tm, tn = INPROJ_TM, INPROJ_TN
    nb = tm // SEQ
    last_ctx = N_CTX_TOK // tm - 1
    x_ctx, x_lat, lat0 = _token_arrays(x)
    return pl.pallas_call(
        _inproj_kernel,
        out_shape=(jax.ShapeDtypeStruct((N_TOK, Z_COLS), BF16),
                   jax.ShapeDtypeStruct((N_TOK, S5_CH), F32),
                   jax.ShapeDtypeStruct((BATCH, 2, SEQ, NA_WIDTH), F32),
                   jax.ShapeDtypeStruct((BATCH, 2, SEQ, GQA_KV_WIDTH), F32)),
        grid=(N_TOK // tm, Z_COLS // tn),
        in_specs=[
            pl.BlockSpec((tm, D_MODEL), lambda i, j: (jnp.minimum(i, last_ctx), 0)),
            pl.BlockSpec((tm, D_MODEL), lambda i, j: (jnp.maximum(i - last_ctx - 1, 0) + lat0 // tm, 0)),
            pl.BlockSpec((1, D_MODEL), lambda i, j: (0, 0)),
            pl.BlockSpec((None, None, 6, D_MODEL), lambda i, j: (l, _cond_index(i, tm), 0, 0)),
            pl.BlockSpec((None, D_MODEL, tn), lambda i, j: (l, 0, j)),
        ],
        out_specs=(pl.BlockSpec((tm, tn), lambda i, j: (i, j)),
                   pl.BlockSpec((tm, S5_CH), lambda i, j: (i, 0)),
                   pl.BlockSpec((nb, 2, SEQ, NA_WIDTH), lambda i, j: (jnp.minimum(i, last_ctx), 0, 0, 0)),
                   pl.BlockSpec((nb, 2, SEQ, GQA_KV_WIDTH), lambda i, j: (jnp.minimum(i, last_ctx), 0, 0, 0))),
        scratch_shapes=[pltpu.VMEM((tm, D_MODEL), BF16)],
        compiler_params=pltpu.CompilerParams(
            dimension_semantics=("arbitrary", "arbitrary"), vmem_limit_bytes=VMEM_LIMIT),
        name="inproj",
    )(x_ctx, x_lat, g, mod, w)


def _block_diag_heads(x, nrep):
    t = x.shape[0]
    rows = lax.broadcasted_iota(jnp.int32, (nrep * t, nrep * HEAD_DIM), 0) // t
    lanes = lax.broadcasted_iota(jnp.int32, (nrep * t, nrep * HEAD_DIM), 1) // HEAD_DIM
    return jnp.where(rows == lanes, jnp.concatenate([x] * nrep, axis=0), jnp.zeros((), x.dtype))


def _per_head(cols, shape):
    head = lax.broadcasted_iota(jnp.int32, shape, 1) // HEAD_DIM
    out = cols[-1]
    for h in range(len(cols) - 2, -1, -1):
        out = jnp.where(head == h, cols[h], out)
    return out


def _attend_heads(q, k, v, sinks):
    nh = q.shape[1] // HEAD_DIM
    t = k.shape[0]
    s = _dot_nt(q * HEAD_DIM ** -0.5, _block_diag_heads(k, nh))
    ps, ds = [], []
    for h in range(nh):
        sh = s[:, h * t:(h + 1) * t]
        m = jnp.max(sh, axis=-1, keepdims=True)
        if sinks is not None:
            m = jnp.maximum(m, sinks[h])
        p = jnp.exp(sh - m)
        d = jnp.sum(p, axis=-1, keepdims=True)
        if sinks is not None:
            d = d + jnp.exp(sinks[h] - m)
        ps.append(p.astype(BF16))
        ds.append(d)
    o = _dot(jnp.concatenate(ps, axis=1), _block_diag_heads(v, nh))
    return o / _per_head(ds, o.shape)


def _ctx_attn_kernel(sink_ref, na_ref, gq_ref, gkv_ref, oa_ref, ob_ref):
    hb = CTX_HEADS_PER_DOT
    w = hb * HEAD_DIM
    na = na_ref[...]
    gq = gq_ref[...]
    gkv = gkv_ref[...]
    for g in range(NA_HEADS // hb):
        o = _attend_heads(na[:, COL_NA_Q + g * w:COL_NA_Q + (g + 1) * w],
                          na[:, COL_NA_K + g * w:COL_NA_K + (g + 1) * w],
                          na[:, COL_NA_V + g * w:COL_NA_V + (g + 1) * w], None)
        oa_ref[:, g * w:(g + 1) * w] = o.astype(BF16)
    for g in range(GQA_Q_HEADS // hb):
        kv_heads = [(g * hb + h) // GQA_GROUP for h in range(hb)]
        k = jnp.concatenate([gkv[:, kh * HEAD_DIM:(kh + 1) * HEAD_DIM] for kh in kv_heads], axis=1)
        v = jnp.concatenate([gkv[:, GQA_KV_WIDTH + kh * HEAD_DIM:GQA_KV_WIDTH + (kh + 1) * HEAD_DIM]
                             for kh in kv_heads], axis=1)
        o = _attend_heads(gq[:, g * w:(g + 1) * w], k, v, [sink_ref[g * hb + h] for h in range(hb)])
        ob_ref[:, g * w:(g + 1) * w] = o.astype(BF16)


def _ctx_attn(z, sink):
    out = jax.ShapeDtypeStruct((N_CTX_TOK, NA_WIDTH), BF16)
    return pl.pallas_call(
        _ctx_attn_kernel,
        out_shape=(out, out),
        grid=(BATCH,),
        in_specs=[
            pl.BlockSpec(memory_space=pltpu.SMEM),
            pl.BlockSpec((SEQ, 3 * NA_WIDTH), lambda b: (b, 0)),
            pl.BlockSpec((SEQ, GQA_Q_WIDTH), lambda b: (b, COL_G_Q // GQA_Q_WIDTH)),
            pl.BlockSpec((SEQ, 2 * GQA_KV_WIDTH), lambda b: (b, COL_G_K // (2 * GQA_KV_WIDTH))),
        ],
        out_specs=(pl.BlockSpec((SEQ, NA_WIDTH), lambda b: (b, 0)),
                   pl.BlockSpec((SEQ, GQA_Q_WIDTH), lambda b: (b, 0))),
        compiler_params=pltpu.CompilerParams(
            dimension_semantics=("parallel",), vmem_limit_bytes=VMEM_LIMIT),
        name="ctx_attn",
    )(sink, z, z, z)


def _na_bias_tables(rpb):
    n_dc = 2 * NA_WIN_W - 1
    qc = np.arange(GRID_W)
    kc = np.arange(GRID_W)
    cstart = np.clip(qc - NA_WIN_W // 2, 0, GRID_W - NA_WIN_W)
    col_ok = (kc[None, :] >= cstart[:, None]) & (kc[None, :] < cstart[:, None] + NA_WIN_W)
    dc = np.clip(kc[None, :] - qc[:, None] + (NA_WIN_W - 1), 0, n_dc - 1)
    onehot = ((np.arange(n_dc)[:, None, None] == dc[None]) & col_ok[None]).astype(np.float32)
    band = jnp.einsum("hrd,dqk->hrqk", rpb.astype(F32), onehot, precision=lax.Precision.HIGHEST)
    band = jnp.where(col_ok, band, NEG_INF)
    band = jnp.pad(band, ((0, 0), (1, 1), (0, 0), (0, 0)), constant_values=NEG_INF)
    return jnp.concatenate([band[:, :-1], band[:, 1:]], axis=-1)


def _na_lat_kernel(q_ref, k_ref, v_ref, ck_ref, cv_ref, bias_ref, o_ref):
    i = pl.program_id(1)
    scale = HEAD_DIM ** -0.5
    nk = NA_KROWS * GRID_W
    kh = min(NA_WIN_H, GRID_ROWS)
    ks = jnp.clip(i * NA_QROWS - NA_WIN_H // 2, 0, GRID_ROWS - NA_KROWS)
    start = pl.multiple_of(ks * GRID_W, GRID_W)
    q = q_ref[...].astype(BF16)
    kw = k_ref[pl.ds(start, nk), :].astype(BF16)
    vw = v_ref[pl.ds(start, nk), :].astype(BF16)
    ck = ck_ref[...].astype(BF16)
    cv = cv_ref[...].astype(BF16)
    left = lax.broadcasted_iota(jnp.int32, (GRID_W, 2 * GRID_W), 1) < GRID_W
    pieces = []
    for qr in range(NA_QROWS):
        r = i * NA_QROWS + qr
        st = jnp.clip(r - kh // 2, 0, GRID_ROWS - kh)
        row = []
        for m in range(NA_KROWS // 2):
            k0 = ks + 2 * m
            plane = jnp.clip(k0 - r + NA_WIN_H, 0, 2 * NA_WIN_H - 1)
            ok0 = (k0 >= st) & (k0 < st + kh)
            ok1 = (k0 + 1 >= st) & (k0 + 1 < st + kh)
            row.append((plane, jnp.where(left, ok0.astype(jnp.int32), ok1.astype(jnp.int32)) > 0))
        pieces.append(row)
    hb = NA_HEADS_PER_DOT
    w = hb * HEAD_DIM
    for g in range(NA_HEADS // hb):
        lanes = slice(g * w, (g + 1) * w)
        qg = q[:, lanes] * scale
        s_loc = _dot_nt(qg, _block_diag_heads(kw[:, lanes], hb))
        s_ctx = _dot_nt(qg, _block_diag_heads(ck[:, lanes], hb))
        p_locs, p_ctxs, ds = [], [], []
        for hh in range(hb):
            bias = jnp.concatenate(
                [jnp.concatenate([jnp.where(ok, bias_ref[g * hb + hh, plane], NEG_INF) for plane, ok in row], axis=1)
                 for row in pieces], axis=0)
            sl = s_loc[:, hh * nk:(hh + 1) * nk] + bias
            sc = s_ctx[:, hh * PAST_LEN:(hh + 1) * PAST_LEN]
            m = jnp.maximum(jnp.max(sl, axis=-1, keepdims=True), jnp.max(sc, axis=-1, keepdims=True))
            p_loc = jnp.exp(sl - m)
            p_ctx = jnp.exp(sc - m)
            ds.append(jnp.sum(p_loc, axis=-1, keepdims=True) + jnp.sum(p_ctx, axis=-1, keepdims=True))
            p_locs.append(p_loc.astype(BF16))
            p_ctxs.append(p_ctx.astype(BF16))
        o = (_dot(jnp.concatenate(p_locs, axis=1), _block_diag_heads(vw[:, lanes], hb))
             + _dot(jnp.concatenate(p_ctxs, axis=1), _block_diag_heads(cv[:, lanes], hb)))
        o_ref[:, lanes] = (o / _per_head(ds, o.shape)).astype(BF16)


def _na_lat(z, cache_na, bias, l):
    tq = NA_QROWS * GRID_W
    nsteps = GRID_ROWS // NA_QROWS
    lat_blk = N_CTX_TOK // DEC_SEQ
    return pl.pallas_call(
        _na_lat_kernel,
        out_shape=jax.ShapeDtypeStruct((N_LAT_TOK, NA_WIDTH), BF16),
        grid=(DEC_BATCH, nsteps),
        in_specs=[
            pl.BlockSpec((tq, NA_WIDTH), lambda b, i: (N_CTX_TOK // tq + b * nsteps + i, COL_NA_Q // NA_WIDTH)),
            pl.BlockSpec((DEC_SEQ, NA_WIDTH), lambda b, i: (lat_blk + b, COL_NA_K // NA_WIDTH)),
            pl.BlockSpec((DEC_SEQ, NA_WIDTH), lambda b, i: (lat_blk + b, COL_NA_V // NA_WIDTH)),
            pl.BlockSpec((None, None, None, PAST_LEN, NA_WIDTH), lambda b, i: (b, l, 0, 0, 0)),
            pl.BlockSpec((None, None, None, PAST_LEN, NA_WIDTH), lambda b, i: (b, l, 1, 0, 0)),
            pl.BlockSpec((NA_HEADS, 2 * NA_WIN_H, GRID_W, 2 * GRID_W), lambda b, i: (0, 0, 0, 0)),
        ],
        out_specs=pl.BlockSpec((tq, NA_WIDTH), lambda b, i: (b * nsteps + i, 0)),
        compiler_params=pltpu.CompilerParams(
            dimension_semantics=("parallel", "arbitrary"), vmem_limit_bytes=VMEM_LIMIT),
        name="na_lat",
    )(z, z, z, cache_na, cache_na, bias)


def _rope_tables():
    nf = HEAD_DIM // 4
    t = jnp.arange(DEC_SEQ)
    pos = jnp.stack([t // GRID_W, t % GRID_W], axis=-1).astype(F32)
    inv = ROPE_THETA ** (-jnp.arange(nf, dtype=F32) / nf)
    ang = pos[:, :, None] * inv
    cos = jnp.cos(ang)
    sin = jnp.sin(ang)
    cos_d = jnp.stack([cos, cos], axis=2).reshape(DEC_SEQ, HEAD_DIM)
    sin_d = jnp.stack([-sin, sin], axis=2).reshape(DEC_SEQ, HEAD_DIM)
    return cos_d, sin_d


def _rope(x, cos, sin_signed):
    n = x.shape[-1]
    nf = HEAD_DIM // 4
    lane = lax.broadcasted_iota(jnp.int32, x.shape, 1)
    first_half = (lane // nf) % 2 == 0
    partner = jnp.where(first_half, pltpu.roll(x, n - nf, 1), pltpu.roll(x, nf, 1))
    return x * cos + partner * sin_signed


def _swa_lat_kernel(sink_ref, q_ref, kv_ref, cq_ref, sq_ref, ckt_ref, skt_ref, ck_ref, cv_ref, o_ref, k_sc, v_sc):
    n = pl.program_id(1)
    scale = HEAD_DIM ** -0.5
    nwin = 3 * SWA_BLOCK

    @pl.when(n == 0)
    def _():
        kv = kv_ref[...]
        k_sc[...] = _rope(kv[:, :GQA_KV_WIDTH].astype(F32), ckt_ref[...], skt_ref[...]).astype(BF16)
        v_sc[...] = kv[:, GQA_KV_WIDTH:].astype(BF16)

    q = (_rope(q_ref[...].astype(F32), cq_ref[...], sq_ref[...]) * scale).astype(BF16)
    start = pl.multiple_of(jnp.clip((n - 1) * SWA_BLOCK, 0, DEC_SEQ - nwin), SWA_BLOCK)
    kw = k_sc[pl.ds(start, nwin), :]
    vw = v_sc[pl.ds(start, nwin), :]
    ck = ck_ref[...].astype(BF16)
    cv = cv_ref[...].astype(BF16)
    rows = GQA_GROUP * SWA_BLOCK
    row = lax.broadcasted_iota(jnp.int32, (rows, nwin), 0)
    col = lax.broadcasted_iota(jnp.int32, (rows, nwin), 1)
    qpos = n * SWA_BLOCK + row % SWA_BLOCK
    kpos = start + col
    ok = jnp.abs(qpos - kpos) <= SWA_WINDOW
    grp = lax.broadcasted_iota(jnp.int32, (rows, 1), 0) // SWA_BLOCK
    for kh in range(GQA_KV_HEADS):
        sl = slice(kh * HEAD_DIM, (kh + 1) * HEAD_DIM)
        q4 = jnp.concatenate(
            [q[:, (kh * GQA_GROUP + g) * HEAD_DIM:(kh * GQA_GROUP + g + 1) * HEAD_DIM] for g in range(GQA_GROUP)],
            axis=0)
        sink = jnp.zeros((rows, 1), F32)
        for g in range(GQA_GROUP):
            sink = jnp.where(grp == g, sink_ref[kh * GQA_GROUP + g], sink)
        s_loc = jnp.where(ok, _dot_nt(q4, kw[:, sl]), NEG_INF)
        s_ctx = _dot_nt(q4, ck[:, sl])
        m = jnp.maximum(jnp.maximum(jnp.max(s_loc, axis=-1, keepdims=True),
                                    jnp.max(s_ctx, axis=-1, keepdims=True)), sink)
        p_loc = jnp.exp(s_loc - m)
        p_ctx = jnp.exp(s_ctx - m)
        d = (jnp.sum(p_loc, axis=-1, keepdims=True) + jnp.sum(p_ctx, axis=-1, keepdims=True)
             + jnp.exp(sink - m))
        o4 = (_dot(p_loc.astype(BF16), vw[:, sl]) + _dot(p_ctx.astype(BF16), cv[:, sl])) / d
        for g in range(GQA_GROUP):
            h = kh * GQA_GROUP + g
            o_ref[:, h * HEAD_DIM:(h + 1) * HEAD_DIM] = o4[g * SWA_BLOCK:(g + 1) * SWA_BLOCK].astype(BF16)


def _swa_lat(z, cache_gqa, sink, cos_q, sin_q, cos_k, sin_k, l):
    nb = DEC_SEQ // SWA_BLOCK
    lat_blk = N_CTX_TOK // DEC_SEQ
    return pl.pallas_call(
        _swa_lat_kernel,
        out_shape=jax.ShapeDtypeStruct((N_LAT_TOK, GQA_Q_WIDTH), BF16),
        grid=(DEC_BATCH, nb),
        in_specs=[
            pl.BlockSpec(memory_space=pltpu.SMEM),
            pl.BlockSpec((SWA_BLOCK, GQA_Q_WIDTH),
                         lambda b, n: (N_CTX_TOK // SWA_BLOCK + b * nb + n, COL_G_Q // GQA_Q_WIDTH)),
            pl.BlockSpec((DEC_SEQ, 2 * GQA_KV_WIDTH), lambda b, n: (lat_blk + b, COL_G_K // (2 * GQA_KV_WIDTH))),
            pl.BlockSpec((SWA_BLOCK, GQA_Q_WIDTH), lambda b, n: (n, 0)),
            pl.BlockSpec((SWA_BLOCK, GQA_Q_WIDTH), lambda b, n: (n, 0)),
            pl.BlockSpec((DEC_SEQ, GQA_KV_WIDTH), lambda b, n: (0, 0)),
            pl.BlockSpec((DEC_SEQ, GQA_KV_WIDTH), lambda b, n: (0, 0)),
            pl.BlockSpec((None, None, None, PAST_LEN, GQA_KV_WIDTH), lambda b, n: (b, l, 0, 0, 0)),
            pl.BlockSpec((None, None, None, PAST_LEN, GQA_KV_WIDTH), lambda b, n: (b, l, 1, 0, 0)),
        ],
        out_specs=pl.BlockSpec((SWA_BLOCK, GQA_Q_WIDTH), lambda b, n: (b * nb + n, 0)),
        scratch_shapes=[pltpu.VMEM((DEC_SEQ, GQA_KV_WIDTH), BF16), pltpu.VMEM((DEC_SEQ, GQA_KV_WIDTH), BF16)],
        compiler_params=pltpu.CompilerParams(
            dimension_semantics=("parallel", "arbitrary"), vmem_limit_bytes=VMEM_LIMIT),
        name="swa_lat",
    )(sink, z, z, cos_q, sin_q, cos_k, sin_k, cache_gqa, cache_gqa)


def _s5_operators(a_re, a_im, log_dt, b_re, b_im, c_re, c_im):
    T, G, N, C, P = S5_T, S5_GROUPS, S5_STATE, S5_GROUP_CH, S5_PAIRS
    tau = jnp.arange(T + 1, dtype=F32)
    eye2 = jnp.eye(2, dtype=F32)
    out = {k: [] for k in ("bs", "m", "cre", "cim", "apr", "api")}
    spread = np.zeros((2, T, C, T, 2, C), np.float32)
    for g in range(2):
        spread[g, np.arange(T)[:, None], np.arange(C)[None, :], np.arange(T)[:, None], g, np.arange(C)[None, :]] = 1.0
    spread = spread.reshape(2, T * C, T * 2 * C)

    def pairs(x):
        return x.reshape((P, 2) + x.shape[1:])

    def diag(x, axis):
        e = eye2.reshape((1, 2) + (1,) * (axis - 1) + (2,) + (1,) * (x.ndim - axis - 1))
        return jnp.expand_dims(x, axis + 1) * e

    for d in range(2):
        A = lax.complex(a_re[d].astype(F32), a_im[d].astype(F32))
        dt = jnp.exp(log_dt[d].astype(F32))[:, None]
        a_bar = jnp.exp(A * dt)
        pw = jnp.exp((A * dt)[None] * tau[:, None, None])
        b_bar = ((a_bar - 1.0) / A)[..., None] * lax.complex(b_re[d].astype(F32), b_im[d].astype(F32))
        c_mat = lax.complex(c_re[d].astype(F32), c_im[d].astype(F32))
        kern = jnp.einsum("gon,tgn,gni->gito", c_mat, pw[:T], b_bar, precision=lax.Precision.HIGHEST).real
        k0 = jnp.einsum("pgik,gkn->pgin", kern.reshape(P, 2, C, T * C), spread, precision=lax.Precision.HIGHEST)
        k0 = k0.reshape(P, 2 * C, T * 2 * C)
        zero = jnp.zeros_like(k0)
        if d == 0:
            p_in = pw[:T][::-1]
            p_out = pw[1:T + 1]
            kp = jnp.concatenate([zero, k0], axis=2)
            m = [kp[:, :, (T - s) * 2 * C:(2 * T - s) * 2 * C] for s in range(T)]
        else:
            p_in = pw[:T]
            p_out = pw[1:T + 1][::-1]
            kp = jnp.concatenate([k0.reshape(P, 2 * C, T, 2 * C)[:, :, ::-1].reshape(k0.shape), zero], axis=2)
            m = [kp[:, :, (T - 1 - s) * 2 * C:(2 * T - 1 - s) * 2 * C] for s in range(T)]
        out["m"].append(jnp.concatenate(m, axis=1))
        pin = p_in.reshape(T, P, 2 * N).transpose(1, 0, 2)[:, :, None, :]
        bx = diag(pairs(b_bar.transpose(0, 2, 1)), 2).reshape(P, 1, 2 * C, 2 * N)
        bs = pin * bx
        out["bs"].append(jnp.concatenate([bs.real, bs.imag], axis=-1).reshape(P, 2 * T * C, 4 * N))
        cx = diag(pairs(c_mat.transpose(0, 2, 1)), 2).reshape(P, 2 * N, 2 * C)
        pout = p_out.reshape(T, P, 2 * N).transpose(1, 2, 0)
        co = jnp.tile(cx, (1, 1, T)) * jnp.repeat(pout, 2 * C, axis=2)
        out["cre"].append(co.real)
        out["cim"].append(-co.imag)
        pw2 = jnp.exp((A * dt)[None] * (T * 2.0 ** jnp.arange(S5_NPOW, dtype=F32))[:, None, None])
        out["apr"].append(pw2.real.reshape(S5_NPOW, G * N))
        out["api"].append(pw2.imag.reshape(S5_NPOW, G * N))
    return {k: jnp.stack(v) for k, v in out.items()}


def _s5_kernel(u_ref, bs_ref, m_ref, cre_ref, cim_ref, apr_ref, api_ref, h0_ref, y_ref, fin_ref, fin_sc):
    t = pl.program_id(1)
    nctx = N_CTX_TOK // S5_BLOCK_TOK
    refs = (u_ref, bs_ref, m_ref, cre_ref, cim_ref, apr_ref, api_ref, h0_ref, y_ref, fin_ref, fin_sc)

    @pl.when(t < nctx)
    def _():
        _s5_block(*refs, t, kseq=SEQ // S5_T, has_h0=False)

    @pl.when(t >= nctx)
    def _():
        _s5_block(*refs, t - nctx, kseq=DEC_SEQ // S5_T, has_h0=True)


def _s5_block(u_ref, bs_ref, m_ref, cre_ref, cim_ref, apr_ref, api_ref, h0_ref, y_ref, fin_ref, fin_sc, t, *,
              kseq, has_h0):
    rows, half, pc = S5_ROWS, 2 * S5_STATE, 2 * S5_GROUP_CH
    k = lax.broadcasted_iota(jnp.int32, (rows, half), 0) % kseq
    shifts = [1 << i for i in range(kseq.bit_length() - 1)]
    xs = [u_ref[pl.ds(s, rows, stride=S5_T), :].astype(BF16) for s in range(S5_T)]
    ys = []
    for p in range(S5_OCT_PAIRS):
        lanes = slice(p * half, (p + 1) * half)
        xp = jnp.concatenate([x[:, p * pc:(p + 1) * pc] for x in xs], axis=1)
        acc = None
        for d in range(2):
            inc = _dot(xp, bs_ref[d, p])
            sr, si = inc[:, :half], inc[:, half:]
            if has_h0:
                h0r = h0_ref[d, 0, t, :, lanes]
                h0i = h0_ref[d, 1, t, :, lanes]
                ar, ai = apr_ref[d, 0:1, lanes], api_ref[d, 0:1, lanes]
                first = (k == 0) if d == 0 else (k == kseq - 1)
                sr = sr + jnp.where(first, ar * h0r - ai * h0i, 0.0)
                si = si + jnp.where(first, ar * h0i + ai * h0r, 0.0)
            else:
                h0r = h0i = 0.0
            for i, sh in enumerate(shifts):
                ar, ai = apr_ref[d, i:i + 1, lanes], api_ref[d, i:i + 1, lanes]
                ok = (k >= sh) if d == 0 else (k < kseq - sh)
                amt = sh if d == 0 else rows - sh
                rr = jnp.where(ok, pltpu.roll(sr, amt, 0), 0.0)
                ri = jnp.where(ok, pltpu.roll(si, amt, 0), 0.0)
                sr, si = sr + ar * rr - ai * ri, si + ar * ri + ai * rr
            inner = (k >= 1) if d == 0 else (k < kseq - 1)
            amt = 1 if d == 0 else rows - 1
            hr = jnp.where(inner, pltpu.roll(sr, amt, 0), h0r)
            hi = jnp.where(inner, pltpu.roll(si, amt, 0), h0i)
            yd = (_dot(xp, m_ref[d, p]) + _dot(hr.astype(BF16), cre_ref[d, p])
                  + _dot(hi.astype(BF16), cim_ref[d, p]))
            acc = yd if acc is None else acc + yd
            if not has_h0:
                last = kseq - 1 if d == 0 else 0
                fin_sc[0] = sr
                fin_sc[1] = si
                fin_ref[d, 0, :, lanes] = fin_sc[0, pl.ds(last, rows // kseq, stride=kseq), :]
                fin_ref[d, 1, :, lanes] = fin_sc[1, pl.ds(last, rows // kseq, stride=kseq), :]
        ys.append(acc)
    for j in range(S5_T):
        y_ref[pl.ds(j, rows, stride=S5_T), :] = jnp.concatenate([y[:, j * pc:(j + 1) * pc] for y in ys], axis=1)


def _s5_scan(u, ops, h0, l):
    nblk, nctx = N_TOK // S5_BLOCK_TOK, N_CTX_TOK // S5_BLOCK_TOK
    noct = S5_PAIRS // S5_OCT_PAIRS
    half, gn = 2 * S5_STATE, S5_GROUPS * S5_STATE
    pw = 2 * S5_T * S5_GROUP_CH
    nseq = S5_BLOCK_TOK // SEQ
    wspec = lambda r, c: pl.BlockSpec((None, 2, S5_OCT_PAIRS, r, c), lambda q, t: (l, 0, q, 0, 0))
    aspec = pl.BlockSpec((None, 2, S5_NPOW, S5_OCT_PAIRS * half), lambda q, t: (l, 0, 0, q))
    return pl.pallas_call(
        _s5_kernel,
        grid=(noct, nblk),
        in_specs=[
            pl.BlockSpec((S5_BLOCK_TOK, 128), lambda q, t: (t, q)),
            wspec(pw, 2 * half), wspec(pw, pw), wspec(half, pw), wspec(half, pw), aspec, aspec,
            pl.BlockSpec((2, 2, DEC_BATCH, 1, S5_OCT_PAIRS * half), lambda q, t: (0, 0, 0, 0, q)),
        ],
        out_shape=(jax.ShapeDtypeStruct((N_TOK, S5_CH), F32), jax.ShapeDtypeStruct((2, 2, BATCH, gn), F32)),
        out_specs=(pl.BlockSpec((S5_BLOCK_TOK, 128), lambda q, t: (t, q)),
                   pl.BlockSpec((2, 2, nseq, S5_OCT_PAIRS * half),
                                lambda q, t: (0, 0, jnp.minimum(t, nctx - 1), q))),
        scratch_shapes=[pltpu.VMEM((2, S5_ROWS, half), F32)],
        compiler_params=pltpu.CompilerParams(
            dimension_semantics=("parallel", "arbitrary"), vmem_limit_bytes=VMEM_LIMIT),
        name="s5_scan",
    )(u, ops["bs"], ops["m"], ops["cre"], ops["cim"], ops["apr"], ops["api"],
      h0.reshape(2, 2, DEC_BATCH, 1, gn))


def _route(logits):
    lane = lax.broadcasted_iota(jnp.int32, logits.shape, 1)
    big = jnp.int32(1 << 20)
    is_g = (lane >= MOE_EXPERTS) & (lane < MOE_EXPERTS + MOE_GROUPS)
    lg = jnp.where(is_g, logits, NEG_INF)
    gmax = jnp.max(lg, axis=-1, keepdims=True)
    gsel = jnp.min(jnp.where(is_g & (lg == gmax), lane, big), axis=-1, keepdims=True) - MOE_EXPERTS
    p_group = 1.0 / jnp.sum(jnp.where(is_g, jnp.exp(lg - gmax), 0.0), axis=-1, keepdims=True)
    in_grp = (lane < MOE_EXPERTS) & (lane // MOE_EXPERTS_PER_GROUP == gsel)
    le = jnp.where(in_grp, logits, NEG_INF)
    v1 = jnp.max(le, axis=-1, keepdims=True)
    i1 = jnp.min(jnp.where(in_grp & (le == v1), lane, big), axis=-1, keepdims=True)
    rest = in_grp & (lane != i1)
    le2 = jnp.where(rest, logits, NEG_INF)
    v2 = jnp.max(le2, axis=-1, keepdims=True)
    i2 = jnp.min(jnp.where(rest & (le2 == v2), lane, big), axis=-1, keepdims=True)
    e2 = jnp.exp(v2 - v1)
    w1 = 1.0 / (1.0 + e2)
    w2 = e2 / (1.0 + e2)
    comb = jnp.where(lane == i1, w1 * p_group, 0.0) + jnp.where(lane == i2, w2 * p_group, 0.0)
    return jnp.where(lane == MOE_EXPERTS, gsel.astype(F32), comb)


def _merge_kernel(xc_ref, xl_ref, oac_ref, oal_ref, obc_ref, obl_ref, u_ref, yc_ref, gates_ref, mod_ref, d_ref,
                  wglu_ref, wa_ref, wb_ref, wc_ref, wout_ref, g2_ref, wrh_ref, wrl_ref, br_ref, x1_ref, hx_ref,
                  *, tm):
    is_ctx = pl.program_id(0) < N_CTX_TOK // tm
    x = jnp.where(is_ctx, xc_ref[...], xl_ref[...])
    oa = jnp.where(is_ctx, oac_ref[...], oal_ref[...])
    ob = jnp.where(is_ctx, obc_ref[...], obl_ref[...])
    y = u_ref[...] * d_ref[...] + yc_ref[...]
    y = y * (0.5 * (1.0 + jnp.tanh(math.sqrt(2.0 / math.pi) * (y + 0.044715 * (y * y * y)))))
    oc = y * jax.nn.sigmoid(_dot(y.astype(BF16), wglu_ref[...]))
    gate = jax.nn.sigmoid(gates_ref[...].astype(F32))
    merged = (gate[:, :D_MODEL] * _dot(oa, wa_ref[...])
              + gate[:, D_MODEL:2 * D_MODEL] * _dot(ob, wb_ref[...])
              + gate[:, 2 * D_MODEL:] * _dot(oc.astype(BF16), wc_ref[...]))
    x1 = x + mod_ref[2:3, :] * _dot(merged.astype(BF16), wout_ref[...])
    x1_ref[...] = x1
    h2 = _rms(x1, g2_ref[...]) * (1.0 + mod_ref[4:5, :]) + mod_ref[3:4, :]
    h_hi, h_lo = _split_bf16(h2)
    logits = _dot(h_hi, wrh_ref[...]) + _dot(h_lo, wrh_ref[...]) + _dot(h_hi, wrl_ref[...]) + br_ref[...]
    half = D_MODEL // 2
    hx_ref[:, :half] = _pack_pairs(h2)
    hx_ref[:, half:] = pltpu.bitcast(_route(logits), jnp.uint32)


def _merge(x, oa_c, oa_l, ob_c, ob_l, u, yc, z, mod, d_s5, wglu, wa, wb, wc, wout, g2, wr_hi, wr_lo, br, l):
    tm = 512
    nctx = N_CTX_TOK // tm
    full = lambda r, c: pl.BlockSpec((r, c), lambda i: (0, 0))
    layer = lambda r, c: pl.BlockSpec((None, r, c), lambda i: (l, 0, 0))
    ctx_blk = lambda w: pl.BlockSpec((tm, w), lambda i: (jnp.minimum(i, nctx - 1), 0))
    lat_blk = lambda w, first=0: pl.BlockSpec((tm, w), lambda i: (jnp.maximum(i - nctx, 0) + first // tm, 0))
    x_ctx, x_lat, lat0 = _token_arrays(x)
    return pl.pallas_call(
        functools.partial(_merge_kernel, tm=tm),
        out_shape=(jax.ShapeDtypeStruct((N_TOK, D_MODEL), F32),
                   jax.ShapeDtypeStruct((N_TOK, MOE_ROW_WORDS), jnp.uint32)),
        grid=(N_TOK // tm,),
        in_specs=[
            ctx_blk(D_MODEL), lat_blk(D_MODEL, lat0),
            ctx_blk(NA_WIDTH), lat_blk(NA_WIDTH), ctx_blk(GQA_Q_WIDTH), lat_blk(GQA_Q_WIDTH),
            pl.BlockSpec((tm, S5_CH), lambda i: (i, 0)),
            pl.BlockSpec((tm, S5_CH), lambda i: (i, 0)),
            pl.BlockSpec((tm, N_BRANCH * D_MODEL), lambda i: (i, COL_GATES // (N_BRANCH * D_MODEL))),
            pl.BlockSpec((None, None, 6, D_MODEL), lambda i: (l, _cond_index(i, tm), 0, 0)),
            full(1, S5_CH), layer(S5_CH, S5_CH), layer(NA_WIDTH, D_MODEL), layer(GQA_Q_WIDTH, D_MODEL),
            layer(S5_CH, D_MODEL), layer(D_MODEL, D_MODEL), full(1, D_MODEL),
            full(D_MODEL, 128), full(D_MODEL, 128), full(1, 128),
        ],
        out_specs=(pl.BlockSpec((tm, D_MODEL), lambda i: (i, 0)),
                   pl.BlockSpec((tm, MOE_ROW_WORDS), lambda i: (i, 0))),
        compiler_params=pltpu.CompilerParams(
            dimension_semantics=("parallel",), vmem_limit_bytes=VMEM_LIMIT),
        name="merge",
    )(x_ctx, x_lat, oa_c, oa_l, ob_c, ob_l, u, yc, z, mod, d_s5, wglu, wa, wb, wc, wout, g2, wr_hi, wr_lo, br)


def _moe_plan(gsel):
    onehot = (gsel[:, None] == jnp.arange(MOE_GROUPS)[None, :]).astype(jnp.int32)
    rank = jnp.sum((jnp.cumsum(onehot, axis=0) - onehot) * onehot, axis=1)
    count = jnp.sum(onehot, axis=0)
    padded = (count + MOE_TILE - 1) // MOE_TILE * MOE_TILE
    end = jnp.cumsum(padded)
    pos = jnp.sum(onehot * (end - padded)[None, :], axis=1) + rank
    tile_row = jnp.arange(MOE_ROWS // MOE_TILE) * MOE_TILE
    tile_group = jnp.minimum(jnp.sum((tile_row[:, None] >= end[None, :]).astype(jnp.int32), axis=1),
                             MOE_GROUPS - 1)
    return pos.astype(jnp.int32), tile_group.astype(jnp.int32), (end[-1:] // MOE_TILE).astype(jnp.int32)


def _dispatch_kernel(pos_ref, hx_ref, init_ref, out_ref, sem):
    del init_ref
    base = pl.program_id(0) * MOE_TOK_BLOCK

    def row_copy(r):
        return pltpu.make_async_copy(hx_ref.at[pl.ds(r, 1), :], out_ref.at[pl.ds(pos_ref[base + r], 1), :], sem)

    def start(r, c):
        row_copy(r).start()
        return c

    def wait(r, c):
        row_copy(r).wait()
        return c

    lax.fori_loop(0, MOE_TOK_BLOCK, start, 0, unroll=16)
    lax.fori_loop(0, MOE_TOK_BLOCK, wait, 0, unroll=True)


def _dispatch(hx, pos):
    return pl.pallas_call(
        _dispatch_kernel,
        grid_spec=pltpu.PrefetchScalarGridSpec(
            num_scalar_prefetch=1,
            grid=(N_TOK // MOE_TOK_BLOCK,),
            in_specs=[pl.BlockSpec((MOE_TOK_BLOCK, MOE_ROW_WORDS), lambda i, pos: (i, 0)),
                      pl.BlockSpec(memory_space=pl.ANY)],
            out_specs=pl.BlockSpec(memory_space=pl.ANY),
            scratch_shapes=[pltpu.SemaphoreType.DMA]),
        out_shape=jax.ShapeDtypeStruct((MOE_ROWS, MOE_ROW_WORDS), jnp.uint32),
        input_output_aliases={2: 0},
        compiler_params=pltpu.CompilerParams(dimension_semantics=("arbitrary",)),
        name="moe_dispatch",
    )(pos, hx, jnp.zeros((MOE_ROWS, MOE_ROW_WORDS), jnp.uint32))


def _experts_kernel(tg_ref, nu_ref, hx_ref, wg_ref, wu_ref, wd_ref, y_ref):
    t = pl.program_id(0)

    @pl.when(t >= nu_ref[0])
    def _():
        y_ref[...] = jnp.zeros_like(y_ref)

    @pl.when(t < nu_ref[0])
    def _():
        half = D_MODEL // 2
        h = _unpack_pairs(hx_ref[:, :half]).astype(BF16)
        comb = pltpu.bitcast(hx_ref[:, half:], F32)
        lane = lax.broadcasted_iota(jnp.int32, comb.shape, 1)
        first = tg_ref[t] * MOE_EXPERTS_PER_GROUP
        acc = None
        for e in range(MOE_EXPERTS_PER_GROUP):
            a = _dot(h, wg_ref[e])
            b = _dot(h, wu_ref[e])
            ce = jnp.sum(jnp.where(lane == first + e, comb, 0.0), axis=-1, keepdims=True)
            act = (a * jax.nn.sigmoid(a)) * b * ce
            c = _dot(act.astype(BF16), wd_ref[e])
            acc = c if acc is None else acc + c
        y_ref[...] = _pack_pairs(acc)


def _experts(hs, tile_group, n_used, wg, wu, wd, l):
    wspec = lambda r, c: pl.BlockSpec((None, None, MOE_EXPERTS_PER_GROUP, r, c),
                                      lambda t, tg, nu: (l, tg[t], 0, 0, 0))
    return pl.pallas_call(
        _experts_kernel,
        grid_spec=pltpu.PrefetchScalarGridSpec(
            num_scalar_prefetch=2,
            grid=(MOE_ROWS // MOE_TILE,),
            in_specs=[pl.BlockSpec((MOE_TILE, MOE_ROW_WORDS), lambda t, tg, nu: (t, 0)),
                      wspec(D_MODEL, EXPERT_FF), wspec(D_MODEL, EXPERT_FF), wspec(EXPERT_FF, D_MODEL)],
            out_specs=pl.BlockSpec((MOE_TILE, D_MODEL // 2), lambda t, tg, nu: (t, 0))),
        out_shape=jax.ShapeDtypeStruct((MOE_ROWS, D_MODEL // 2), jnp.uint32),
        compiler_params=pltpu.CompilerParams(
            dimension_semantics=("arbitrary",), vmem_limit_bytes=VMEM_LIMIT),
        name="moe_experts",
    )(tile_group, n_used, hs, wg, wu, wd)


def _combine_kernel(pos_ref, y_ref, x_ref, mod_ref, fg_ref, *rest, final):
    buf, sem = rest[-2:]
    i = pl.program_id(0)
    base = i * MOE_TOK_BLOCK

    def row_copy(r):
        return pltpu.make_async_copy(y_ref.at[pl.ds(pos_ref[base + r], 1), :], buf.at[pl.ds(r, 1), :], sem)

    def start(r, c):
        row_copy(r).start()
        return c

    def wait(r, c):
        row_copy(r).wait()
        return c

    lax.fori_loop(0, MOE_TOK_BLOCK, start, 0, unroll=16)
    lax.fori_loop(0, MOE_TOK_BLOCK, wait, 0, unroll=True)
    x2 = x_ref[...] + mod_ref[5:6, :] * _unpack_pairs(buf[...])
    if not final:
        rest[0][...] = x2
    else:
        y = _rms(x2, fg_ref[...])
        is_ctx = i < N_CTX_TOK // MOE_TOK_BLOCK

        @pl.when(is_ctx)
        def _():
            rest[0][...] = y

        @pl.when(jnp.logical_not(is_ctx))
        def _():
            rest[1][...] = y


def _combine(ys, pos, x1, mod, fg, l, final):
    tm = MOE_TOK_BLOCK
    nctx = N_CTX_TOK // tm
    tok = pl.BlockSpec((tm, D_MODEL), lambda i, pos: (i, 0))
    out = jax.ShapeDtypeStruct((N_TOK, D_MODEL), F32)
    if final:
        out = (jax.ShapeDtypeStruct((N_CTX_TOK, D_MODEL), F32), jax.ShapeDtypeStruct((N_LAT_TOK, D_MODEL), F32))
        out_specs = (pl.BlockSpec((tm, D_MODEL), lambda i, pos: (jnp.minimum(i, nctx - 1), 0)),
                     pl.BlockSpec((tm, D_MODEL), lambda i, pos: (jnp.maximum(i - nctx, 0), 0)))
    else:
        out_specs = tok
    return pl.pallas_call(
        functools.partial(_combine_kernel, final=final),
        grid_spec=pltpu.PrefetchScalarGridSpec(
            num_scalar_prefetch=1,
            grid=(N_TOK // tm,),
            in_specs=[pl.BlockSpec(memory_space=pl.ANY), tok,
                      pl.BlockSpec((None, None, 6, D_MODEL), lambda i, pos: (l, _cond_index(i, tm), 0, 0)),
                      pl.BlockSpec((1, D_MODEL), lambda i, pos: (0, 0))],
            out_specs=out_specs,
            scratch_shapes=[pltpu.VMEM((tm, D_MODEL // 2), jnp.uint32), pltpu.SemaphoreType.DMA]),
        out_shape=out,
        compiler_params=pltpu.CompilerParams(
            dimension_semantics=("arbitrary",), vmem_limit_bytes=VMEM_LIMIT),
        name="moe_combine",
    )(pos, ys, x1, mod, fg)


def _pack_w_in(w):
    qkv = w[..., :COL_G_V + GQA_KV_WIDTH]
    u = w[..., COL_G_V + GQA_KV_WIDTH:COL_G_V + GQA_KV_WIDTH + S5_CH]
    gates = w[..., COL_G_V + GQA_KV_WIDTH + S5_CH:]
    pad = jnp.zeros(w.shape[:-1] + (COL_U - (COL_G_V + GQA_KV_WIDTH),), w.dtype)
    return jnp.concatenate([qkv, pad, u, gates], axis=-1).astype(BF16)


def kernel(x_prompt, x_sample, cache_na_kv, cache_gqa_kv, state_ssm, c, c_ctx, norm_g, w_ada, b_ada, w_in, na_rpb, gqa_sink, s5_a_re, s5_a_im, s5_log_dt, s5_b_re, s5_b_im, s5_c_re, s5_c_im, s5_d, s5_w_glu, w_branch_a, w_branch_b, w_branch_c, w_out, moe_w_group, moe_b_group, moe_w_expert, moe_b_expert, moe_w_gate, moe_w_up, moe_w_down, final_g):
    cond = jnp.zeros((N_COND, D_MODEL), F32).at[0].set(c_ctx.astype(F32)).at[1:1 + DEC_BATCH].set(c.astype(F32))
    mod = _ada_mod(cond, w_ada.astype(F32), b_ada.astype(F32))

    x = (x_prompt.astype(F32).reshape(N_CTX_TOK, D_MODEL), x_sample.astype(F32).reshape(N_LAT_TOK, D_MODEL))
    cache_na = cache_na_kv.reshape(DEC_BATCH, DEPTH, 2, PAST_LEN, NA_WIDTH)
    cache_gqa = cache_gqa_kv.reshape(DEC_BATCH, DEPTH, 2, PAST_LEN, GQA_KV_WIDTH)
    cos_d, sin_d = _rope_tables()
    cos_q, sin_q = jnp.tile(cos_d, (1, GQA_Q_HEADS)), jnp.tile(sin_d, (1, GQA_Q_HEADS))
    cos_k, sin_k = jnp.tile(cos_d, (1, GQA_KV_HEADS)), jnp.tile(sin_d, (1, GQA_KV_HEADS))
    gn = S5_GROUPS * S5_STATE
    fg = final_g.astype(F32).reshape(1, D_MODEL)

    w_in_packed = _pack_w_in(w_in)
    bf16_weights = [w.astype(BF16) for w in (s5_w_glu, w_branch_a, w_branch_b, w_branch_c, w_out)]
    grouped = lambda w: w.astype(BF16).reshape((DEPTH, MOE_GROUPS, MOE_EXPERTS_PER_GROUP) + w.shape[2:])
    moe_weights = [grouped(w) for w in (moe_w_gate, moe_w_up, moe_w_down)]
    ops = jax.vmap(_s5_operators)(s5_a_re, s5_a_im, s5_log_dt, s5_b_re, s5_b_im, s5_c_re, s5_c_im)
    ops = {k: (v.astype(BF16) if k in ("bs", "m", "cre", "cim") else v) for k, v in ops.items()}

    na_list, gqa_list, ssm_list = [], [], []
    y = None
    for l in range(DEPTH):
        z, u, new_na, new_gqa = _inproj(x, norm_g[l, 0].astype(F32).reshape(1, D_MODEL), mod, w_in_packed, l)
        na_list.append(new_na.reshape(BATCH, 2, SEQ, NA_HEADS, HEAD_DIM))
        gqa_list.append(new_gqa.reshape(BATCH, 2, SEQ, GQA_KV_HEADS, HEAD_DIM))

        sink = gqa_sink[l].astype(F32)
        oa_c, ob_c = _ctx_attn(z, sink)
        oa_l = _na_lat(z, cache_na, _na_bias_tables(na_rpb[l]), l)
        ob_l = _swa_lat(z, cache_gqa, sink, cos_q, sin_q, cos_k, sin_k, l)

        h0 = state_ssm[:, l].astype(F32).reshape(DEC_BATCH, 2, 2, gn).transpose(1, 2, 0, 3)
        yc, fin = _s5_scan(u, ops, h0, l)
        ssm_list.append(fin.transpose(2, 0, 1, 3).reshape(BATCH, 2, 2, S5_GROUPS, S5_STATE).astype(x_prompt.dtype))

        wr = jnp.zeros((D_MODEL, 128), F32)
        wr = wr.at[:, :MOE_EXPERTS].set(moe_w_expert[l].astype(F32))
        wr = wr.at[:, MOE_EXPERTS:MOE_EXPERTS + MOE_GROUPS].set(moe_w_group[l].astype(F32))
        br = jnp.zeros((1, 128), F32)
        br = br.at[0, :MOE_EXPERTS].set(moe_b_expert[l].astype(F32))
        br = br.at[0, MOE_EXPERTS:MOE_EXPERTS + MOE_GROUPS].set(moe_b_group[l].astype(F32))
        wr_hi, wr_lo = _split_bf16(wr)
        x1, hx = _merge(
            x, oa_c, oa_l, ob_c, ob_l, u, yc, z, mod, s5_d[l].astype(F32).reshape(1, S5_CH), *bf16_weights,
            norm_g[l, 1].astype(F32).reshape(1, D_MODEL), wr_hi, wr_lo, br, l)
        gsel = lax.bitcast_convert_type(hx[:, D_MODEL // 2 + MOE_EXPERTS], F32).astype(jnp.int32)
        pos, tile_group, n_used = _moe_plan(gsel)
        ys = _experts(_dispatch(hx, pos), tile_group, n_used, *moe_weights, l)
        if l < DEPTH - 1:
            x = _combine(ys, pos, x1, mod, fg, l, False)
        else:
            y_ctx, y_lat = _combine(ys, pos, x1, mod, fg, l, True)

    return (y_ctx.reshape(BATCH, SEQ, D_MODEL), y_lat.reshape(DEC_BATCH, DEC_SEQ, D_MODEL),
            jnp.stack(na_list, axis=1), jnp.stack(gqa_list, axis=1), jnp.stack(ssm_list, axis=1))
```

```python
import functools
import math

import numpy as np
import jax
import jax.numpy as jnp
from jax import lax
from jax.experimental import pallas as pl
from jax.experimental.pallas import tpu as pltpu

F32 = jnp.float32
BF16 = jnp.bfloat16

D_MODEL = 1024
BATCH = 32
SEQ = 256
DEPTH = 2
DEC_BATCH = 4
DEC_SEQ = 2048
PAST_LEN = 512
GRID_W = 64
GRID_ROWS = DEC_SEQ // GRID_W
HEAD_DIM = 64
NA_HEADS = 8
NA_WIN_H = 8
NA_WIN_W = 16
GQA_Q_HEADS = 8
GQA_KV_HEADS = 2
GQA_GROUP = GQA_Q_HEADS // GQA_KV_HEADS
SWA_WINDOW = 128
SWA_BLOCK = 128
ROPE_THETA = 10000.0
S5_CH = 512
S5_GROUP_CH = 16
S5_GROUPS = S5_CH // S5_GROUP_CH
S5_STATE = 64
N_BRANCH = 3
NA_WIDTH = NA_HEADS * HEAD_DIM
GQA_Q_WIDTH = GQA_Q_HEADS * HEAD_DIM
GQA_KV_WIDTH = GQA_KV_HEADS * HEAD_DIM
MOE_GROUPS = 4
MOE_EXPERTS_PER_GROUP = 4
MOE_EXPERTS = MOE_GROUPS * MOE_EXPERTS_PER_GROUP
EXPERT_FF = 512
EPS = 1e-6
NEG_INF = -1e30

N_CTX_TOK = BATCH * SEQ
N_LAT_TOK = DEC_BATCH * DEC_SEQ
N_TOK = N_CTX_TOK + N_LAT_TOK
N_COND = 8

COL_NA_Q = 0
COL_NA_K = 512
COL_NA_V = 1024
COL_G_Q = 1536
COL_G_K = 2048
COL_G_V = 2176
COL_U = 2560
COL_GATES = 3072
Z_COLS = 6144

S5_T = 16
S5_PAIRS = S5_GROUPS // 2
S5_OCT_PAIRS = 4
S5_BLOCK_TOK = 2048
S5_ROWS = S5_BLOCK_TOK // S5_T
S5_NPOW = 7

MOE_ROW_WORDS = D_MODEL // 2 + 128
MOE_TILE = 512
MOE_ROWS = N_TOK + MOE_GROUPS * MOE_TILE
MOE_TOK_BLOCK = 512
MOE_PAIR_MEMBERS = ((0, 1), (0, 2), (0, 3), (1, 3), (1, 2), (2, 3))
MOE_PAIRS = len(MOE_PAIR_MEMBERS)

CTX_HEADS_PER_DOT = 4
NA_HEADS_PER_DOT = 2
NA_QROWS = 4
NA_KROWS = 12

VMEM_LIMIT = 56 * 1024 * 1024


def _dot(a, b):
    return jnp.dot(a, b, preferred_element_type=F32)


def _dot_nt(a, b):
    return lax.dot_general(a, b, (((1,), (1,)), ((), ())), preferred_element_type=F32)


def _split_bf16(x):
    hi = x.astype(BF16)
    lo = (x - hi.astype(F32)).astype(BF16)
    return hi, lo


def _pack_pairs(x):
    w = x.shape[1] // 2
    xb = x.astype(BF16).astype(F32)
    return (pltpu.bitcast(xb[:, :w], jnp.uint32) & jnp.uint32(0xFFFF0000)) | (
        pltpu.bitcast(xb[:, w:], jnp.uint32) >> 16)


def _unpack_pairs(word):
    return jnp.concatenate([pltpu.bitcast(word & jnp.uint32(0xFFFF0000), F32),
                            pltpu.bitcast(word << 16, F32)], axis=1)


def _rms(x, g):
    return x * lax.rsqrt(jnp.mean(x * x, axis=-1, keepdims=True) + EPS) * g


def _cond_index(i, tm):
    nctx = N_CTX_TOK // tm
    return jnp.where(i < nctx, 0, 1 + ((i - nctx) * tm) // DEC_SEQ)


def _ada_kernel(c_ref, w_ref, b_ref, o_ref):
    c = c_ref[...]
    s = c * jax.nn.sigmoid(c)
    s_hi, s_lo = _split_bf16(s)
    w_hi, w_lo = _split_bf16(w_ref[...])
    o_ref[...] = _dot(s_hi, w_hi) + _dot(s_lo, w_hi) + _dot(s_hi, w_lo) + b_ref[...]


def _ada_mod(cond, w_ada, b_ada):
    tn = 1536
    n = 6 * D_MODEL
    out = pl.pallas_call(
        _ada_kernel,
        out_shape=jax.ShapeDtypeStruct((DEPTH, N_COND, n), F32),
        grid=(DEPTH, n // tn),
        in_specs=[
            pl.BlockSpec((N_COND, D_MODEL), lambda l, j: (0, 0)),
            pl.BlockSpec((None, D_MODEL, tn), lambda l, j: (l, 0, j)),
            pl.BlockSpec((None, 1, tn), lambda l, j: (l, 0, j)),
        ],
        out_specs=pl.BlockSpec((None, N_COND, tn), lambda l, j: (l, 0, j)),
        compiler_params=pltpu.CompilerParams(
            dimension_semantics=("parallel", "parallel"), vmem_limit_bytes=VMEM_LIMIT),
        name="ada_mod",
    )(cond, w_ada, b_ada.reshape(DEPTH, 1, n))
    return out.reshape(DEPTH, N_COND, 6, D_MODEL)


INPROJ_TM = 1024
INPROJ_TN = 1536


def _inproj_kernel(xc_ref, xl_ref, g_ref, mod_ref, w_ref, z_ref, u_ref, cna_ref, cgq_ref, h_sc):
    i, j = pl.program_id(0), pl.program_id(1)
    is_ctx = i < N_CTX_TOK // INPROJ_TM

    def norm_mod(x_ref):
        h = _rms(x_ref[...], g_ref[...]) * (1.0 + mod_ref[1:2, :]) + mod_ref[0:1, :]
        h_sc[...] = h.astype(BF16)

    @pl.when((j == 0) & is_ctx)
    def _():
        norm_mod(xc_ref)

    @pl.when((j == 0) & jnp.logical_not(is_ctx))
    def _():
        norm_mod(xl_ref)

    acc = _dot(h_sc[...], w_ref[...])
    z_ref[...] = acc.astype(BF16)

    def rows(b):
        return slice(b * SEQ, (b + 1) * SEQ)

    def cols(c, width):
        return slice(c % INPROJ_TN, c % INPROJ_TN + width)

    @pl.when(is_ctx & (j == COL_NA_K // INPROJ_TN))
    def _():
        for b in range(INPROJ_TM // SEQ):
            cna_ref[b, 0] = acc[rows(b), cols(COL_NA_K, NA_WIDTH)]
            cna_ref[b, 1] = acc[rows(b), cols(COL_NA_V, NA_WIDTH)]

    @pl.when(is_ctx & (j == COL_G_K // INPROJ_TN))
    def _():
        for b in range(INPROJ_TM // SEQ):
            cgq_ref[b, 0] = acc[rows(b), cols(COL_G_K, GQA_KV_WIDTH)]
            cgq_ref[b, 1] = acc[rows(b), cols(COL_G_V, GQA_KV_WIDTH)]

    @pl.when(j == COL_U // INPROJ_TN)
    def _():
        u_ref[...] = acc[:, cols(COL_U, S5_CH)]


def _token_arrays(x):
    return (x, x, N_CTX_TOK) if not isinstance(x, tuple) else (x[0], x[1], 0)


def _inproj(x, g, mod, w, l):
    tm, tn = INPROJ_TM, INPROJ_TN
    nb = tm // SEQ
    last_ctx = N_CTX_TOK // tm - 1
    x_ctx, x_lat, lat0 = _token_arrays(x)
    return pl.pallas_call(
        _inproj_kernel,
        out_shape=(jax.ShapeDtypeStruct((N_TOK, Z_COLS), BF16),
                   jax.ShapeDtypeStruct((N_TOK, S5_CH), F32),
                   jax.ShapeDtypeStruct((BATCH, 2, SEQ, NA_WIDTH), F32),
                   jax.ShapeDtypeStruct((BATCH, 2, SEQ, GQA_KV_WIDTH), F32)),
        grid=(N_TOK // tm, Z_COLS // tn),
        in_specs=[
            pl.BlockSpec((tm, D_MODEL), lambda i, j: (jnp.minimum(i, last_ctx), 0)),
            pl.BlockSpec((tm, D_MODEL), lambda i, j: (jnp.maximum(i - last_ctx - 1, 0) + lat0 // tm, 0)),
            pl.BlockSpec((1, D_MODEL), lambda i, j: (0, 0)),
            pl.BlockSpec((None, None, 6, D_MODEL), lambda i, j: (l, _cond_index(i, tm), 0, 0)),
            pl.BlockSpec((None, D_MODEL, tn), lambda i, j: (l, 0, j)),
        ],
        out_specs=(pl.BlockSpec((tm, tn), lambda i, j: (i, j)),
                   pl.BlockSpec((tm, S5_CH), lambda i, j: (i, 0)),
                   pl.BlockSpec((nb, 2, SEQ, NA_WIDTH), lambda i, j: (jnp.minimum(i, last_ctx), 0, 0, 0)),
                   pl.BlockSpec((nb, 2, SEQ, GQA_KV_WIDTH), lambda i, j: (jnp.minimum(i, last_ctx), 0, 0, 0))),
        scratch_shapes=[pltpu.VMEM((tm, D_MODEL), BF16)],
        compiler_params=pltpu.CompilerParams(
            dimension_semantics=("arbitrary", "arbitrary"), vmem_limit_bytes=VMEM_LIMIT),
        name="inproj",
    )(x_ctx, x_lat, g, mod, w)


def _block_diag_heads(x, nrep):
    t = x.shape[0]
    rows = lax.broadcasted_iota(jnp.int32, (nrep * t, nrep * HEAD_DIM), 0) // t
    lanes = lax.broadcasted_iota(jnp.int32, (nrep * t, nrep * HEAD_DIM), 1) // HEAD_DIM
    return jnp.where(rows == lanes, jnp.concatenate([x] * nrep, axis=0), jnp.zeros((), x.dtype))


def _per_head(cols, shape):
    head = lax.broadcasted_iota(jnp.int32, shape, 1) // HEAD_DIM
    out = cols[-1]
    for h in range(len(cols) - 2, -1, -1):
        out = jnp.where(head == h, cols[h], out)
    return out


def _attend_heads(q, k, v, sinks):
    nh = q.shape[1] // HEAD_DIM
    t = k.shape[0]
    s = _dot_nt(q * HEAD_DIM ** -0.5, _block_diag_heads(k, nh))
    ps, ds = [], []
    for h in range(nh):
        sh = s[:, h * t:(h + 1) * t]
        m = jnp.max(sh, axis=-1, keepdims=True)
        if sinks is not None:
            m = jnp.maximum(m, sinks[h])
        p = jnp.exp(sh - m)
        d = jnp.sum(p, axis=-1, keepdims=True)
        if sinks is not None:
            d = d + jnp.exp(sinks[h] - m)
        ps.append(p.astype(BF16))
        ds.append(d)
    o = _dot(jnp.concatenate(ps, axis=1), _block_diag_heads(v, nh))
    return o / _per_head(ds, o.shape)


def _ctx_attn_kernel(sink_ref, na_ref, gq_ref, gkv_ref, oa_ref, ob_ref):
    hb = CTX_HEADS_PER_DOT
    w = hb * HEAD_DIM
    na = na_ref[...]
    gq = gq_ref[...]
    gkv = gkv_ref[...]
    for g in range(NA_HEADS // hb):
        o = _attend_heads(na[:, COL_NA_Q + g * w:COL_NA_Q + (g + 1) * w],
                          na[:, COL_NA_K + g * w:COL_NA_K + (g + 1) * w],
                          na[:, COL_NA_V + g * w:COL_NA_V + (g + 1) * w], None)
        oa_ref[:, g * w:(g + 1) * w] = o.astype(BF16)
    for g in range(GQA_Q_HEADS // hb):
        kv_heads = [(g * hb + h) // GQA_GROUP for h in range(hb)]
        k = jnp.concatenate([gkv[:, kh * HEAD_DIM:(kh + 1) * HEAD_DIM] for kh in kv_heads], axis=1)
        v = jnp.concatenate([gkv[:, GQA_KV_WIDTH + kh * HEAD_DIM:GQA_KV_WIDTH + (kh + 1) * HEAD_DIM]
                             for kh in kv_heads], axis=1)
        o = _attend_heads(gq[:, g * w:(g + 1) * w], k, v, [sink_ref[g * hb + h] for h in range(hb)])
        ob_ref[:, g * w:(g + 1) * w] = o.astype(BF16)


def _ctx_attn(z, sink):
    out = jax.ShapeDtypeStruct((N_CTX_TOK, NA_WIDTH), BF16)
    return pl.pallas_call(
        _ctx_attn_kernel,
        out_shape=(out, out),
        grid=(BATCH,),
        in_specs=[
            pl.BlockSpec(memory_space=pltpu.SMEM),
            pl.BlockSpec((SEQ, 3 * NA_WIDTH), lambda b: (b, 0)),
            pl.BlockSpec((SEQ, GQA_Q_WIDTH), lambda b: (b, COL_G_Q // GQA_Q_WIDTH)),
            pl.BlockSpec((SEQ, 2 * GQA_KV_WIDTH), lambda b: (b, COL_G_K // (2 * GQA_KV_WIDTH))),
        ],
        out_specs=(pl.BlockSpec((SEQ, NA_WIDTH), lambda b: (b, 0)),
                   pl.BlockSpec((SEQ, GQA_Q_WIDTH), lambda b: (b, 0))),
        compiler_params=pltpu.CompilerParams(
            dimension_semantics=("parallel",), vmem_limit_bytes=VMEM_LIMIT),
        name="ctx_attn",
    )(sink, z, z, z)


def _na_bias_tables(rpb):
    n_dc = 2 * NA_WIN_W - 1
    qc = np.arange(GRID_W)
    kc = np.arange(GRID_W)
    cstart = np.clip(qc - NA_WIN_W // 2, 0, GRID_W - NA_WIN_W)
    col_ok = (kc[None, :] >= cstart[:, None]) & (kc[None, :] < cstart[:, None] + NA_WIN_W)
    dc = np.clip(kc[None, :] - qc[:, None] + (NA_WIN_W - 1), 0, n_dc - 1)
    onehot = ((np.arange(n_dc)[:, None, None] == dc[None]) & col_ok[None]).astype(np.float32)
    band = jnp.einsum("hrd,dqk->hrqk", rpb.astype(F32), onehot, precision=lax.Precision.HIGHEST)
    band = jnp.where(col_ok, band, NEG_INF)
    band = jnp.pad(band, ((0, 0), (1, 1), (0, 0), (0, 0)), constant_values=NEG_INF)
    return jnp.concatenate([band[:, :-1], band[:, 1:]], axis=-1)


def _na_lat_kernel(q_ref, k_ref, v_ref, ck_ref, cv_ref, bias_ref, o_ref):
    i = pl.program_id(1)
    scale = HEAD_DIM ** -0.5
    nk = NA_KROWS * GRID_W
    kh = min(NA_WIN_H, GRID_ROWS)
    ks = jnp.clip(i * NA_QROWS - NA_WIN_H // 2, 0, GRID_ROWS - NA_KROWS)
    start = pl.multiple_of(ks * GRID_W, GRID_W)
    q = q_ref[...].astype(BF16)
    kw = k_ref[pl.ds(start, nk), :].astype(BF16)
    vw = v_ref[pl.ds(start, nk), :].astype(BF16)
    ck = ck_ref[...].astype(BF16)
    cv = cv_ref[...].astype(BF16)
    left = lax.broadcasted_iota(jnp.int32, (GRID_W, 2 * GRID_W), 1) < GRID_W
    pieces = []
    for qr in range(NA_QROWS):
        r = i * NA_QROWS + qr
        st = jnp.clip(r - kh // 2, 0, GRID_ROWS - kh)
        row = []
        for m in range(NA_KROWS // 2):
            k0 = ks + 2 * m
            plane = jnp.clip(k0 - r + NA_WIN_H, 0, 2 * NA_WIN_H - 1)
            ok0 = (k0 >= st) & (k0 < st + kh)
            ok1 = (k0 + 1 >= st) & (k0 + 1 < st + kh)
            row.append((plane, jnp.where(left, ok0.astype(jnp.int32), ok1.astype(jnp.int32)) > 0))
        pieces.append(row)
    hb = NA_HEADS_PER_DOT
    w = hb * HEAD_DIM
    for g in range(NA_HEADS // hb):
        lanes = slice(g * w, (g + 1) * w)
        qg = q[:, lanes] * scale
        s_loc = _dot_nt(qg, _block_diag_heads(kw[:, lanes], hb))
        s_ctx = _dot_nt(qg, _block_diag_heads(ck[:, lanes], hb))
        p_locs, p_ctxs, ds = [], [], []
        for hh in range(hb):
            bias = jnp.concatenate(
                [jnp.concatenate([jnp.where(ok, bias_ref[g * hb + hh, plane], NEG_INF) for plane, ok in row], axis=1)
                 for row in pieces], axis=0)
            sl = s_loc[:, hh * nk:(hh + 1) * nk] + bias
            sc = s_ctx[:, hh * PAST_LEN:(hh + 1) * PAST_LEN]
            m = jnp.maximum(jnp.max(sl, axis=-1, keepdims=True), jnp.max(sc, axis=-1, keepdims=True))
            p_loc = jnp.exp(sl - m)
            p_ctx = jnp.exp(sc - m)
            ds.append(jnp.sum(p_loc, axis=-1, keepdims=True) + jnp.sum(p_ctx, axis=-1, keepdims=True))
            p_locs.append(p_loc.astype(BF16))
            p_ctxs.append(p_ctx.astype(BF16))
        o = (_dot(jnp.concatenate(p_locs, axis=1), _block_diag_heads(vw[:, lanes], hb))
             + _dot(jnp.concatenate(p_ctxs, axis=1), _block_diag_heads(cv[:, lanes], hb)))
        o_ref[:, lanes] = (o / _per_head(ds, o.shape)).astype(BF16)


def _na_lat(z, cache_na, bias, l):
    tq = NA_QROWS * GRID_W
    nsteps = GRID_ROWS // NA_QROWS
    lat_blk = N_CTX_TOK // DEC_SEQ
    return pl.pallas_call(
        _na_lat_kernel,
        out_shape=jax.ShapeDtypeStruct((N_LAT_TOK, NA_WIDTH), BF16),
        grid=(DEC_BATCH, nsteps),
        in_specs=[
            pl.BlockSpec((tq, NA_WIDTH), lambda b, i: (N_CTX_TOK // tq + b * nsteps + i, COL_NA_Q // NA_WIDTH)),
            pl.BlockSpec((DEC_SEQ, NA_WIDTH), lambda b, i: (lat_blk + b, COL_NA_K // NA_WIDTH)),
            pl.BlockSpec((DEC_SEQ, NA_WIDTH), lambda b, i: (lat_blk + b, COL_NA_V // NA_WIDTH)),
            pl.BlockSpec((None, None, None, PAST_LEN, NA_WIDTH), lambda b, i: (b, l, 0, 0, 0)),
            pl.BlockSpec((None, None, None, PAST_LEN, NA_WIDTH), lambda b, i: (b, l, 1, 0, 0)),
            pl.BlockSpec((NA_HEADS, 2 * NA_WIN_H, GRID_W, 2 * GRID_W), lambda b, i: (0, 0, 0, 0)),
        ],
        out_specs=pl.BlockSpec((tq, NA_WIDTH), lambda b, i: (b * nsteps + i, 0)),
        compiler_params=pltpu.CompilerParams(
            dimension_semantics=("parallel", "arbitrary"), vmem_limit_bytes=VMEM_LIMIT),
        name="na_lat",
    )(z, z, z, cache_na, cache_na, bias)


def _rope_tables():
    nf = HEAD_DIM // 4
    t = jnp.arange(DEC_SEQ)
    pos = jnp.stack([t // GRID_W, t % GRID_W], axis=-1).astype(F32)
    inv = ROPE_THETA ** (-jnp.arange(nf, dtype=F32) / nf)
    ang = pos[:, :, None] * inv
    cos = jnp.cos(ang)
    sin = jnp.sin(ang)
    cos_d = jnp.stack([cos, cos], axis=2).reshape(DEC_SEQ, HEAD_DIM)
    sin_d = jnp.stack([-sin, sin], axis=2).reshape(DEC_SEQ, HEAD_DIM)
    return cos_d, sin_d


def _rope(x, cos, sin_signed):
    n = x.shape[-1]
    nf = HEAD_DIM // 4
    lane = lax.broadcasted_iota(jnp.int32, x.shape, 1)
    first_half = (lane // nf) % 2 == 0
    partner = jnp.where(first_half, pltpu.roll(x, n - nf, 1), pltpu.roll(x, nf, 1))
    return x * cos + partner * sin_signed


def _swa_lat_kernel(sink_ref, q_ref, kv_ref, cq_ref, sq_ref, ckt_ref, skt_ref, ck_ref, cv_ref, o_ref, k_sc, v_sc):
    n = pl.program_id(1)
    scale = HEAD_DIM ** -0.5
    nwin = 3 * SWA_BLOCK

    @pl.when(n == 0)
    def _():
        kv = kv_ref[...]
        k_sc[...] = _rope(kv[:, :GQA_KV_WIDTH].astype(F32), ckt_ref[...], skt_ref[...]).astype(BF16)
        v_sc[...] = kv[:, GQA_KV_WIDTH:].astype(BF16)

    q = (_rope(q_ref[...].astype(F32), cq_ref[...], sq_ref[...]) * scale).astype(BF16)
    start = pl.multiple_of(jnp.clip((n - 1) * SWA_BLOCK, 0, DEC_SEQ - nwin), SWA_BLOCK)
    kw = k_sc[pl.ds(start, nwin), :]
    vw = v_sc[pl.ds(start, nwin), :]
    ck = ck_ref[...].astype(BF16)
    cv = cv_ref[...].astype(BF16)
    rows = GQA_GROUP * SWA_BLOCK
    row = lax.broadcasted_iota(jnp.int32, (rows, nwin), 0)
    col = lax.broadcasted_iota(jnp.int32, (rows, nwin), 1)
    qpos = n * SWA_BLOCK + row % SWA_BLOCK
    kpos = start + col
    ok = jnp.abs(qpos - kpos) <= SWA_WINDOW
    grp = lax.broadcasted_iota(jnp.int32, (rows, 1), 0) // SWA_BLOCK
    for kh in range(GQA_KV_HEADS):
        sl = slice(kh * HEAD_DIM, (kh + 1) * HEAD_DIM)
        q4 = jnp.concatenate(
            [q[:, (kh * GQA_GROUP + g) * HEAD_DIM:(kh * GQA_GROUP + g + 1) * HEAD_DIM] for g in range(GQA_GROUP)],
            axis=0)
        sink = jnp.zeros((rows, 1), F32)
        for g in range(GQA_GROUP):
            sink = jnp.where(grp == g, sink_ref[kh * GQA_GROUP + g], sink)
        s_loc = jnp.where(ok, _dot_nt(q4, kw[:, sl]), NEG_INF)
        s_ctx = _dot_nt(q4, ck[:, sl])
        m = jnp.maximum(jnp.maximum(jnp.max(s_loc, axis=-1, keepdims=True),
                                    jnp.max(s_ctx, axis=-1, keepdims=True)), sink)
        p_loc = jnp.exp(s_loc - m)
        p_ctx = jnp.exp(s_ctx - m)
        d = (jnp.sum(p_loc, axis=-1, keepdims=True) + jnp.sum(p_ctx, axis=-1, keepdims=True)
             + jnp.exp(sink - m))
        o4 = (_dot(p_loc.astype(BF16), vw[:, sl]) + _dot(p_ctx.astype(BF16), cv[:, sl])) / d
        for g in range(GQA_GROUP):
            h = kh * GQA_GROUP + g
            o_ref[:, h * HEAD_DIM:(h + 1) * HEAD_DIM] = o4[g * SWA_BLOCK:(g + 1) * SWA_BLOCK].astype(BF16)


def _swa_lat(z, cache_gqa, sink, cos_q, sin_q, cos_k, sin_k, l):
    nb = DEC_SEQ // SWA_BLOCK
    lat_blk = N_CTX_TOK // DEC_SEQ
    return pl.pallas_call(
        _swa_lat_kernel,
        out_shape=jax.ShapeDtypeStruct((N_LAT_TOK, GQA_Q_WIDTH), BF16),
        grid=(DEC_BATCH, nb),
        in_specs=[
            pl.BlockSpec(memory_space=pltpu.SMEM),
            pl.BlockSpec((SWA_BLOCK, GQA_Q_WIDTH),
                         lambda b, n: (N_CTX_TOK // SWA_BLOCK + b * nb + n, COL_G_Q // GQA_Q_WIDTH)),
            pl.BlockSpec((DEC_SEQ, 2 * GQA_KV_WIDTH), lambda b, n: (lat_blk + b, COL_G_K // (2 * GQA_KV_WIDTH))),
            pl.BlockSpec((SWA_BLOCK, GQA_Q_WIDTH), lambda b, n: (n, 0)),
            pl.BlockSpec((SWA_BLOCK, GQA_Q_WIDTH), lambda b, n: (n, 0)),
            pl.BlockSpec((DEC_SEQ, GQA_KV_WIDTH), lambda b, n: (0, 0)),
            pl.BlockSpec((DEC_SEQ, GQA_KV_WIDTH), lambda b, n: (0, 0)),
            pl.BlockSpec((None, None, None, PAST_LEN, GQA_KV_WIDTH), lambda b, n: (b, l, 0, 0, 0)),
            pl.BlockSpec((None, None, None, PAST_LEN, GQA_KV_WIDTH), lambda b, n: (b, l, 1, 0, 0)),
        ],
        out_specs=pl.BlockSpec((SWA_BLOCK, GQA_Q_WIDTH), lambda b, n: (b * nb + n, 0)),
        scratch_shapes=[pltpu.VMEM((DEC_SEQ, GQA_KV_WIDTH), BF16), pltpu.VMEM((DEC_SEQ, GQA_KV_WIDTH), BF16)],
        compiler_params=pltpu.CompilerParams(
            dimension_semantics=("parallel", "arbitrary"), vmem_limit_bytes=VMEM_LIMIT),
        name="swa_lat",
    )(sink, z, z, cos_q, sin_q, cos_k, sin_k, cache_gqa, cache_gqa)


def _s5_operators(a_re, a_im, log_dt, b_re, b_im, c_re, c_im):
    T, G, N, C, P = S5_T, S5_GROUPS, S5_STATE, S5_GROUP_CH, S5_PAIRS
    tau = jnp.arange(T + 1, dtype=F32)
    eye2 = jnp.eye(2, dtype=F32)
    out = {k: [] for k in ("bs", "m", "cre", "cim", "apr", "api")}

    def pairs(x):
        return x.reshape((P, 2) + x.shape[1:])

    def diag(x, axis):
        e = eye2.reshape((1, 2) + (1,) * (axis - 1) + (2,) + (1,) * (x.ndim - axis - 1))
        return jnp.expand_dims(x, axis + 1) * e

    for d in range(2):
        A = lax.complex(a_re[d].astype(F32), a_im[d].astype(F32))
        dt = jnp.exp(log_dt[d].astype(F32))[:, None]
        a_bar = jnp.exp(A * dt)
        pw = jnp.exp((A * dt)[None] * tau[:, None, None])
        b_bar = ((a_bar - 1.0) / A)[..., None] * lax.complex(b_re[d].astype(F32), b_im[d].astype(F32))
        c_mat = lax.complex(c_re[d].astype(F32), c_im[d].astype(F32))
        kern = jnp.einsum("gon,tgn,gni->gtoi", c_mat, pw[:T], b_bar, precision=lax.Precision.HIGHEST).real
        k0 = diag(pairs(kern).transpose(0, 1, 4, 2, 3), 3).reshape(P, 2 * C, T * 2 * C)
        zero = jnp.zeros_like(k0)
        if d == 0:
            p_in = pw[:T][::-1]
            p_out = pw[1:T + 1]
            kp = jnp.concatenate([zero, k0], axis=2)
            m = [kp[:, :, (T - s) * 2 * C:(2 * T - s) * 2 * C] for s in range(T)]
        else:
            p_in = pw[:T]
            p_out = pw[1:T + 1][::-1]
            kp = jnp.concatenate([k0.reshape(P, 2 * C, T, 2 * C)[:, :, ::-1].reshape(k0.shape), zero], axis=2)
            m = [kp[:, :, (T - 1 - s) * 2 * C:(2 * T - 1 - s) * 2 * C] for s in range(T)]
        out["m"].append(jnp.concatenate(m, axis=1))
        pin = p_in.reshape(T, P, 2 * N).transpose(1, 0, 2)[:, :, None, :]
        bx = diag(pairs(b_bar.transpose(0, 2, 1)), 2).reshape(P, 1, 2 * C, 2 * N)
        bs = pin * bx
        out["bs"].append(jnp.concatenate([bs.real, bs.imag], axis=-1).reshape(P, 2 * T * C, 4 * N))
        cx = diag(pairs(c_mat.transpose(0, 2, 1)), 2).reshape(P, 2 * N, 2 * C)
        pout = p_out.reshape(T, P, 2 * N).transpose(1, 2, 0)
        co = jnp.tile(cx, (1, 1, T)) * jnp.repeat(pout, 2 * C, axis=2)
        out["cre"].append(co.real)
        out["cim"].append(-co.imag)
        pw2 = jnp.exp((A * dt)[None] * (T * 2.0 ** jnp.arange(S5_NPOW, dtype=F32))[:, None, None])
        out["apr"].append(pw2.real.reshape(S5_NPOW, G * N))
        out["api"].append(pw2.imag.reshape(S5_NPOW, G * N))
    return {k: jnp.stack(v) for k, v in out.items()}


def _s5_kernel(u_ref, bs_ref, m_ref, cre_ref, cim_ref, apr_ref, api_ref, h0_ref, y_ref, fin_ref, fin_sc):
    t = pl.program_id(1)
    nctx = N_CTX_TOK // S5_BLOCK_TOK
    refs = (u_ref, bs_ref, m_ref, cre_ref, cim_ref, apr_ref, api_ref, h0_ref, y_ref, fin_ref, fin_sc)

    @pl.when(t < nctx)
    def _():
        _s5_block(*refs, t, kseq=SEQ // S5_T, has_h0=False)

    @pl.when(t >= nctx)
    def _():
        _s5_block(*refs, t - nctx, kseq=DEC_SEQ // S5_T, has_h0=True)


def _s5_block(u_ref, bs_ref, m_ref, cre_ref, cim_ref, apr_ref, api_ref, h0_ref, y_ref, fin_ref, fin_sc, t, *,
              kseq, has_h0):
    rows, half, pc = S5_ROWS, 2 * S5_STATE, 2 * S5_GROUP_CH
    k = lax.broadcasted_iota(jnp.int32, (rows, half), 0) % kseq
    shifts = [1 << i for i in range(kseq.bit_length() - 1)]
    xs = [u_ref[pl.ds(s, rows, stride=S5_T), :].astype(BF16) for s in range(S5_T)]
    ys = []
    for p in range(S5_OCT_PAIRS):
        lanes = slice(p * half, (p + 1) * half)
        xp = jnp.concatenate([x[:, p * pc:(p + 1) * pc] for x in xs], axis=1)
        acc = None
        for d in range(2):
            inc = _dot(xp, bs_ref[d, p])
            sr, si = inc[:, :half], inc[:, half:]
            if has_h0:
                h0r = h0_ref[d, 0, t, :, lanes]
                h0i = h0_ref[d, 1, t, :, lanes]
                ar, ai = apr_ref[d, 0:1, lanes], api_ref[d, 0:1, lanes]
                first = (k == 0) if d == 0 else (k == kseq - 1)
                sr = sr + jnp.where(first, ar * h0r - ai * h0i, 0.0)
                si = si + jnp.where(first, ar * h0i + ai * h0r, 0.0)
            else:
                h0r = h0i = 0.0
            for i, sh in enumerate(shifts):
                ar, ai = apr_ref[d, i:i + 1, lanes], api_ref[d, i:i + 1, lanes]
                ok = (k >= sh) if d == 0 else (k < kseq - sh)
                amt = sh if d == 0 else rows - sh
                rr = jnp.where(ok, pltpu.roll(sr, amt, 0), 0.0)
                ri = jnp.where(ok, pltpu.roll(si, amt, 0), 0.0)
                sr, si = sr + ar * rr - ai * ri, si + ar * ri + ai * rr
            inner = (k >= 1) if d == 0 else (k < kseq - 1)
            amt = 1 if d == 0 else rows - 1
            hr = jnp.where(inner, pltpu.roll(sr, amt, 0), h0r)
            hi = jnp.where(inner, pltpu.roll(si, amt, 0), h0i)
            yd = (_dot(xp, m_ref[d, p]) + _dot(hr.astype(BF16), cre_ref[d, p])
                  + _dot(hi.astype(BF16), cim_ref[d, p]))
            acc = yd if acc is None else acc + yd
            if not has_h0:
                last = kseq - 1 if d == 0 else 0
                fin_sc[0] = sr
                fin_sc[1] = si
                fin_ref[d, 0, :, lanes] = fin_sc[0, pl.ds(last, rows // kseq, stride=kseq), :]
                fin_ref[d, 1, :, lanes] = fin_sc[1, pl.ds(last, rows // kseq, stride=kseq), :]
        ys.append(acc)
    for j in range(S5_T):
        y_ref[pl.ds(j, rows, stride=S5_T), :] = jnp.concatenate([y[:, j * pc:(j + 1) * pc] for y in ys], axis=1)


def _s5_scan(u, ops, h0, l):
    nblk, nctx = N_TOK // S5_BLOCK_TOK, N_CTX_TOK // S5_BLOCK_TOK
    noct = S5_PAIRS // S5_OCT_PAIRS
    half, gn = 2 * S5_STATE, S5_GROUPS * S5_STATE
    pw = 2 * S5_T * S5_GROUP_CH
    nseq = S5_BLOCK_TOK // SEQ
    wspec = lambda r, c: pl.BlockSpec((None, 2, S5_OCT_PAIRS, r, c), lambda q, t: (l, 0, q, 0, 0))
    aspec = pl.BlockSpec((None, 2, S5_NPOW, S5_OCT_PAIRS * half), lambda q, t: (l, 0, 0, q))
    return pl.pallas_call(
        _s5_kernel,
        grid=(noct, nblk),
        in_specs=[
            pl.BlockSpec((S5_BLOCK_TOK, 128), lambda q, t: (t, q)),
            wspec(pw, 2 * half), wspec(pw, pw), wspec(half, pw), wspec(half, pw), aspec, aspec,
            pl.BlockSpec((2, 2, DEC_BATCH, 1, S5_OCT_PAIRS * half), lambda q, t: (0, 0, 0, 0, q)),
        ],
        out_shape=(jax.ShapeDtypeStruct((N_TOK, S5_CH), F32), jax.ShapeDtypeStruct((2, 2, BATCH, gn), F32)),
        out_specs=(pl.BlockSpec((S5_BLOCK_TOK, 128), lambda q, t: (t, q)),
                   pl.BlockSpec((2, 2, nseq, S5_OCT_PAIRS * half),
                                lambda q, t: (0, 0, jnp.minimum(t, nctx - 1), q))),
        scratch_shapes=[pltpu.VMEM((2, S5_ROWS, half), F32)],
        compiler_params=pltpu.CompilerParams(
            dimension_semantics=("parallel", "arbitrary"), vmem_limit_bytes=VMEM_LIMIT),
        name="s5_scan",
    )(u, ops["bs"], ops["m"], ops["cre"], ops["cim"], ops["apr"], ops["api"],
      h0.reshape(2, 2, DEC_BATCH, 1, gn))


def _route(logits):
    lane = lax.broadcasted_iota(jnp.int32, logits.shape, 1)
    big = jnp.int32(1 << 20)
    is_g = (lane >= MOE_EXPERTS) & (lane < MOE_EXPERTS + MOE_GROUPS)
    lg = jnp.where(is_g, logits, NEG_INF)
    gmax = jnp.max(lg, axis=-1, keepdims=True)
    gsel = jnp.min(jnp.where(is_g & (lg == gmax), lane, big), axis=-1, keepdims=True) - MOE_EXPERTS
    p_group = 1.0 / jnp.sum(jnp.where(is_g, jnp.exp(lg - gmax), 0.0), axis=-1, keepdims=True)
    in_grp = (lane < MOE_EXPERTS) & (lane // MOE_EXPERTS_PER_GROUP == gsel)
    le = jnp.where(in_grp, logits, NEG_INF)
    v1 = jnp.max(le, axis=-1, keepdims=True)
    i1 = jnp.min(jnp.where(in_grp & (le == v1), lane, big), axis=-1, keepdims=True)
    rest = in_grp & (lane != i1)
    le2 = jnp.where(rest, logits, NEG_INF)
    v2 = jnp.max(le2, axis=-1, keepdims=True)
    i2 = jnp.min(jnp.where(rest & (le2 == v2), lane, big), axis=-1, keepdims=True)
    e2 = jnp.exp(v2 - v1)
    w1 = 1.0 / (1.0 + e2)
    w2 = e2 / (1.0 + e2)
    comb = jnp.where(lane == i1, w1 * p_group, 0.0) + jnp.where(lane == i2, w2 * p_group, 0.0)
    a = jnp.minimum(i1, i2) - gsel * MOE_EXPERTS_PER_GROUP
    b = jnp.maximum(i1, i2) - gsel * MOE_EXPERTS_PER_GROUP
    pair = jnp.where(a == 0, b - 1, jnp.where(a == 1, jnp.where(b == 3, 3, 4), 5))
    cls = gsel * MOE_PAIRS + pair
    return jnp.where(lane == MOE_EXPERTS, cls.astype(F32), comb)


def _merge_kernel(xc_ref, xl_ref, oac_ref, oal_ref, obc_ref, obl_ref, u_ref, yc_ref, gates_ref, mod_ref, d_ref,
                  wglu_ref, wa_ref, wb_ref, wc_ref, wout_ref, g2_ref, wrh_ref, wrl_ref, br_ref, x1_ref, hx_ref,
                  *, tm):
    is_ctx = pl.program_id(0) < N_CTX_TOK // tm
    x = jnp.where(is_ctx, xc_ref[...], xl_ref[...])
    oa = jnp.where(is_ctx, oac_ref[...], oal_ref[...])
    ob = jnp.where(is_ctx, obc_ref[...], obl_ref[...])
    y = u_ref[...] * d_ref[...] + yc_ref[...]
    y = y * (0.5 * (1.0 + jnp.tanh(math.sqrt(2.0 / math.pi) * (y + 0.044715 * (y * y * y)))))
    oc = y * jax.nn.sigmoid(_dot(y.astype(BF16), wglu_ref[...]))
    gate = jax.nn.sigmoid(gates_ref[...].astype(F32))
    merged = (gate[:, :D_MODEL] * _dot(oa, wa_ref[...])
              + gate[:, D_MODEL:2 * D_MODEL] * _dot(ob, wb_ref[...])
              + gate[:, 2 * D_MODEL:] * _dot(oc.astype(BF16), wc_ref[...]))
    x1 = x + mod_ref[2:3, :] * _dot(merged.astype(BF16), wout_ref[...])
    x1_ref[...] = x1
    h2 = _rms(x1, g2_ref[...]) * (1.0 + mod_ref[4:5, :]) + mod_ref[3:4, :]
    h_hi, h_lo = _split_bf16(h2)
    logits = _dot(h_hi, wrh_ref[...]) + _dot(h_lo, wrh_ref[...]) + _dot(h_hi, wrl_ref[...]) + br_ref[...]
    half = D_MODEL // 2
    hx_ref[:, :half] = _pack_pairs(h2)
    hx_ref[:, half:] = pltpu.bitcast(_route(logits), jnp.uint32)


def _merge(x, oa_c, oa_l, ob_c, ob_l, u, yc, z, mod, d_s5, wglu, wa, wb, wc, wout, g2, wr_hi, wr_lo, br, l):
    tm = 512
    nctx = N_CTX_TOK // tm
    full = lambda r, c: pl.BlockSpec((r, c), lambda i: (0, 0))
    layer = lambda r, c: pl.BlockSpec((None, r, c), lambda i: (l, 0, 0))
    ctx_blk = lambda w: pl.BlockSpec((tm, w), lambda i: (jnp.minimum(i, nctx - 1), 0))
    lat_blk = lambda w, first=0: pl.BlockSpec((tm, w), lambda i: (jnp.maximum(i - nctx, 0) + first // tm, 0))
    x_ctx, x_lat, lat0 = _token_arrays(x)
    return pl.pallas_call(
        functools.partial(_merge_kernel, tm=tm),
        out_shape=(jax.ShapeDtypeStruct((N_TOK, D_MODEL), F32),
                   jax.ShapeDtypeStruct((N_TOK, MOE_ROW_WORDS), jnp.uint32)),
        grid=(N_TOK // tm,),
        in_specs=[
            ctx_blk(D_MODEL), lat_blk(D_MODEL, lat0),
            ctx_blk(NA_WIDTH), lat_blk(NA_WIDTH), ctx_blk(GQA_Q_WIDTH), lat_blk(GQA_Q_WIDTH),
            pl.BlockSpec((tm, S5_CH), lambda i: (i, 0)),
            pl.BlockSpec((tm, S5_CH), lambda i: (i, 0)),
            pl.BlockSpec((tm, N_BRANCH * D_MODEL), lambda i: (i, COL_GATES // (N_BRANCH * D_MODEL))),
            pl.BlockSpec((None, None, 6, D_MODEL), lambda i: (l, _cond_index(i, tm), 0, 0)),
            full(1, S5_CH), layer(S5_CH, S5_CH), layer(NA_WIDTH, D_MODEL), layer(GQA_Q_WIDTH, D_MODEL),
            layer(S5_CH, D_MODEL), layer(D_MODEL, D_MODEL), full(1, D_MODEL),
            full(D_MODEL, 128), full(D_MODEL, 128), full(1, 128),
        ],
        out_specs=(pl.BlockSpec((tm, D_MODEL), lambda i: (i, 0)),
                   pl.BlockSpec((tm, MOE_ROW_WORDS), lambda i: (i, 0))),
        compiler_params=pltpu.CompilerParams(
            dimension_semantics=("parallel",), vmem_limit_bytes=VMEM_LIMIT),
        name="merge",
    )(x_ctx, x_lat, oa_c, oa_l, ob_c, ob_l, u, yc, z, mod, d_s5, wglu, wa, wb, wc, wout, g2, wr_hi, wr_lo, br)


def _moe_plan(cls):
    ncls = MOE_GROUPS * MOE_PAIRS
    ntiles = MOE_ROWS // MOE_TILE
    onehot = (cls[:, None] == jnp.arange(ncls)[None, :]).astype(jnp.int32)
    rank = jnp.sum((jnp.cumsum(onehot, axis=0) - onehot) * onehot, axis=1)
    ccount = jnp.sum(onehot, axis=0).reshape(MOE_GROUPS, MOE_PAIRS)
    gcount = jnp.sum(ccount, axis=1)
    gpadded = (gcount + MOE_TILE - 1) // MOE_TILE * MOE_TILE
    gend = jnp.cumsum(gpadded)
    cstart = ((gend - gpadded)[:, None] + jnp.cumsum(ccount, axis=1) - ccount).reshape(ncls)
    cend = cstart + ccount.reshape(ncls)
    pos = jnp.sum(onehot * cstart[None, :], axis=1) + rank
    tile_row = jnp.arange(ntiles) * MOE_TILE
    tile_group = jnp.minimum(jnp.sum((tile_row[:, None] >= gend[None, :]).astype(jnp.int32), axis=1),
                             MOE_GROUPS - 1)
    member = np.zeros((ncls, MOE_GROUPS, MOE_EXPERTS_PER_GROUP), np.float32)
    for g in range(MOE_GROUPS):
        for p, pair in enumerate(MOE_PAIR_MEMBERS):
            member[g * MOE_PAIRS + p, g, list(pair)] = 1.0
    overlap = ((cstart[None, :] < tile_row[:, None] + MOE_TILE) & (cend[None, :] > tile_row[:, None])
               & (cend > cstart)[None, :]).astype(F32)
    need = jnp.einsum("tc,cge->tge", overlap, member)
    need = jnp.sum(need * (tile_group[:, None, None] == jnp.arange(MOE_GROUPS)[None, :, None]), axis=1)
    return (pos.astype(jnp.int32), tile_group.astype(jnp.int32), (need > 0).astype(jnp.int32).reshape(-1),
            (gend[-1:] // MOE_TILE).astype(jnp.int32))


def _dispatch_kernel(pos_ref, hx_ref, init_ref, out_ref, sem):
    del init_ref
    base = pl.program_id(0) * MOE_TOK_BLOCK

    def row_copy(r):
        return pltpu.make_async_copy(hx_ref.at[pl.ds(r, 1), :], out_ref.at[pl.ds(pos_ref[base + r], 1), :], sem)

    def start(r, c):
        row_copy(r).start()
        return c

    def wait(r, c):
        row_copy(r).wait()
        return c

    lax.fori_loop(0, MOE_TOK_BLOCK, start, 0, unroll=16)
    lax.fori_loop(0, MOE_TOK_BLOCK, wait, 0, unroll=True)


def _dispatch(hx, pos):
    return pl.pallas_call(
        _dispatch_kernel,
        grid_spec=pltpu.PrefetchScalarGridSpec(
            num_scalar_prefetch=1,
            grid=(N_TOK // MOE_TOK_BLOCK,),
            in_specs=[pl.BlockSpec((MOE_TOK_BLOCK, MOE_ROW_WORDS), lambda i, pos: (i, 0)),
                      pl.BlockSpec(memory_space=pl.ANY)],
            out_specs=pl.BlockSpec(memory_space=pl.ANY),
            scratch_shapes=[pltpu.SemaphoreType.DMA]),
        out_shape=jax.ShapeDtypeStruct((MOE_ROWS, MOE_ROW_WORDS), jnp.uint32),
        input_output_aliases={2: 0},
        compiler_params=pltpu.CompilerParams(dimension_semantics=("arbitrary",)),
        name="moe_dispatch",
    )(pos, hx, jnp.zeros((MOE_ROWS, MOE_ROW_WORDS), jnp.uint32))


def _experts_kernel(tg_ref, need_ref, nu_ref, hx_ref, wg_ref, wu_ref, wd_ref, y_ref):
    t = pl.program_id(0)
    y_ref[...] = jnp.zeros_like(y_ref)
    half = D_MODEL // 2
    for e in range(MOE_EXPERTS_PER_GROUP):
        @pl.when((t < nu_ref[0]) & (need_ref[t * MOE_EXPERTS_PER_GROUP + e] > 0))
        def _():
            h = _unpack_pairs(hx_ref[:, :half]).astype(BF16)
            comb = pltpu.bitcast(hx_ref[:, half:], F32)
            lane = lax.broadcasted_iota(jnp.int32, comb.shape, 1)
            ce = jnp.sum(jnp.where(lane == tg_ref[t] * MOE_EXPERTS_PER_GROUP + e, comb, 0.0), axis=-1,
                         keepdims=True)
            a = _dot(h, wg_ref[e])
            b = _dot(h, wu_ref[e])
            act = (a * jax.nn.sigmoid(a)) * b * ce
            y_ref[...] += _dot(act.astype(BF16), wd_ref[e])


def _experts(hs, tile_group, need, n_used, wg, wu, wd, l):
    wspec = lambda r, c: pl.BlockSpec((None, None, MOE_EXPERTS_PER_GROUP, r, c),
                                      lambda t, tg, need, nu: (l, tg[t], 0, 0, 0))
    return pl.pallas_call(
        _experts_kernel,
        grid_spec=pltpu.PrefetchScalarGridSpec(
            num_scalar_prefetch=3,
            grid=(MOE_ROWS // MOE_TILE,),
            in_specs=[pl.BlockSpec((MOE_TILE, MOE_ROW_WORDS), lambda t, tg, need, nu: (t, 0)),
                      wspec(D_MODEL, EXPERT_FF), wspec(D_MODEL, EXPERT_FF), wspec(EXPERT_FF, D_MODEL)],
            out_specs=pl.BlockSpec((MOE_TILE, D_MODEL), lambda t, tg, need, nu: (t, 0))),
        out_shape=jax.ShapeDtypeStruct((MOE_ROWS, D_MODEL), F32),
        compiler_params=pltpu.CompilerParams(
            dimension_semantics=("arbitrary",), vmem_limit_bytes=VMEM_LIMIT),
        name="moe_experts",
    )(tile_group, need, n_used, hs, wg, wu, wd)


def _combine_kernel(pos_ref, y_ref, x_ref, mod_ref, fg_ref, *rest, final):
    buf, sem = rest[-2:]
    i = pl.program_id(0)
    base = i * MOE_TOK_BLOCK

    def row_copy(r):
        return pltpu.make_async_copy(y_ref.at[pl.ds(pos_ref[base + r], 1), :], buf.at[pl.ds(r, 1), :], sem)

    def start(r, c):
        row_copy(r).start()
        return c

    def wait(r, c):
        row_copy(r).wait()
        return c

    lax.fori_loop(0, MOE_TOK_BLOCK, start, 0, unroll=16)
    lax.fori_loop(0, MOE_TOK_BLOCK, wait, 0, unroll=True)
    x2 = x_ref[...] + mod_ref[5:6, :] * buf[...]
    if not final:
        rest[0][...] = x2
    else:
        y = _rms(x2, fg_ref[...])
        is_ctx = i < N_CTX_TOK // MOE_TOK_BLOCK

        @pl.when(is_ctx)
        def _():
            rest[0][...] = y

        @pl.when(jnp.logical_not(is_ctx))
        def _():
            rest[1][...] = y


def _combine(ys, pos, x1, mod, fg, l, final):
    tm = MOE_TOK_BLOCK
    nctx = N_CTX_TOK // tm
    tok = pl.BlockSpec((tm, D_MODEL), lambda i, pos: (i, 0))
    out = jax.ShapeDtypeStruct((N_TOK, D_MODEL), F32)
    if final:
        out = (jax.ShapeDtypeStruct((N_CTX_TOK, D_MODEL), F32), jax.ShapeDtypeStruct((N_LAT_TOK, D_MODEL), F32))
        out_specs = (pl.BlockSpec((tm, D_MODEL), lambda i, pos: (jnp.minimum(i, nctx - 1), 0)),
                     pl.BlockSpec((tm, D_MODEL), lambda i, pos: (jnp.maximum(i - nctx, 0), 0)))
    else:
        out_specs = tok
    return pl.pallas_call(
        functools.partial(_combine_kernel, final=final),
        grid_spec=pltpu.PrefetchScalarGridSpec(
            num_scalar_prefetch=1,
            grid=(N_TOK // tm,),
            in_specs=[pl.BlockSpec(memory_space=pl.ANY), tok,
                      pl.BlockSpec((None, None, 6, D_MODEL), lambda i, pos: (l, _cond_index(i, tm), 0, 0)),
                      pl.BlockSpec((1, D_MODEL), lambda i, pos: (0, 0))],
            out_specs=out_specs,
            scratch_shapes=[pltpu.VMEM((tm, D_MODEL), F32), pltpu.SemaphoreType.DMA]),
        out_shape=out,
        compiler_params=pltpu.CompilerParams(
            dimension_semantics=("arbitrary",), vmem_limit_bytes=VMEM_LIMIT),
        name="moe_combine",
    )(pos, ys, x1, mod, fg)


def _pack_w_in(w):
    qkv = w[..., :COL_G_V + GQA_KV_WIDTH]
    u = w[..., COL_G_V + GQA_KV_WIDTH:COL_G_V + GQA_KV_WIDTH + S5_CH]
    gates = w[..., COL_G_V + GQA_KV_WIDTH + S5_CH:]
    pad = jnp.zeros(w.shape[:-1] + (COL_U - (COL_G_V + GQA_KV_WIDTH),), w.dtype)
    return jnp.concatenate([qkv, pad, u, gates], axis=-1).astype(BF16)


def kernel(x_prompt, x_sample, cache_na_kv, cache_gqa_kv, state_ssm, c, c_ctx, norm_g, w_ada, b_ada, w_in, na_rpb, gqa_sink, s5_a_re, s5_a_im, s5_log_dt, s5_b_re, s5_b_im, s5_c_re, s5_c_im, s5_d, s5_w_glu, w_branch_a, w_branch_b, w_branch_c, w_out, moe_w_group, moe_b_group, moe_w_expert, moe_b_expert, moe_w_gate, moe_w_up, moe_w_down, final_g):
    cond = jnp.zeros((N_COND, D_MODEL), F32).at[0].set(c_ctx.astype(F32)).at[1:1 + DEC_BATCH].set(c.astype(F32))
    mod = _ada_mod(cond, w_ada.astype(F32), b_ada.astype(F32))

    x = (x_prompt.astype(F32).reshape(N_CTX_TOK, D_MODEL), x_sample.astype(F32).reshape(N_LAT_TOK, D_MODEL))
    cache_na = cache_na_kv.reshape(DEC_BATCH, DEPTH, 2, PAST_LEN, NA_WIDTH)
    cache_gqa = cache_gqa_kv.reshape(DEC_BATCH, DEPTH, 2, PAST_LEN, GQA_KV_WIDTH)
    cos_d, sin_d = _rope_tables()
    cos_q, sin_q = jnp.tile(cos_d, (1, GQA_Q_HEADS)), jnp.tile(sin_d, (1, GQA_Q_HEADS))
    cos_k, sin_k = jnp.tile(cos_d, (1, GQA_KV_HEADS)), jnp.tile(sin_d, (1, GQA_KV_HEADS))
    gn = S5_GROUPS * S5_STATE
    fg = final_g.astype(F32).reshape(1, D_MODEL)

    w_in_packed = _pack_w_in(w_in)
    bf16_weights = [w.astype(BF16) for w in (s5_w_glu, w_branch_a, w_branch_b, w_branch_c, w_out)]
    grouped = lambda w: w.astype(BF16).reshape((DEPTH, MOE_GROUPS, MOE_EXPERTS_PER_GROUP) + w.shape[2:])
    moe_weights = [grouped(w) for w in (moe_w_gate, moe_w_up, moe_w_down)]
    ops = jax.vmap(_s5_operators)(s5_a_re, s5_a_im, s5_log_dt, s5_b_re, s5_b_im, s5_c_re, s5_c_im)
    ops = {k: (v.astype(BF16) if k in ("bs", "m", "cre", "cim") else v) for k, v in ops.items()}

    na_list, gqa_list, ssm_list = [], [], []
    y = None
    for l in range(DEPTH):
        z, u, new_na, new_gqa = _inproj(x, norm_g[l, 0].astype(F32).reshape(1, D_MODEL), mod, w_in_packed, l)
        na_list.append(new_na.reshape(BATCH, 2, SEQ, NA_HEADS, HEAD_DIM))
        gqa_list.append(new_gqa.reshape(BATCH, 2, SEQ, GQA_KV_HEADS, HEAD_DIM))

        sink = gqa_sink[l].astype(F32)
        oa_c, ob_c = _ctx_attn(z, sink)
        oa_l = _na_lat(z, cache_na, _na_bias_tables(na_rpb[l]), l)
        ob_l = _swa_lat(z, cache_gqa, sink, cos_q, sin_q, cos_k, sin_k, l)

        h0 = state_ssm[:, l].astype(F32).reshape(DEC_BATCH, 2, 2, gn).transpose(1, 2, 0, 3)
        yc, fin = _s5_scan(u, ops, h0, l)
        ssm_list.append(fin.transpose(2, 0, 1, 3).reshape(BATCH, 2, 2, S5_GROUPS, S5_STATE).astype(x_prompt.dtype))

        wr = jnp.zeros((D_MODEL, 128), F32)
        wr = wr.at[:, :MOE_EXPERTS].set(moe_w_expert[l].astype(F32))
        wr = wr.at[:, MOE_EXPERTS:MOE_EXPERTS + MOE_GROUPS].set(moe_w_group[l].astype(F32))
        br = jnp.zeros((1, 128), F32)
        br = br.at[0, :MOE_EXPERTS].set(moe_b_expert[l].astype(F32))
        br = br.at[0, MOE_EXPERTS:MOE_EXPERTS + MOE_GROUPS].set(moe_b_group[l].astype(F32))
        wr_hi, wr_lo = _split_bf16(wr)
        x1, hx = _merge(
            x, oa_c, oa_l, ob_c, ob_l, u, yc, z, mod, s5_d[l].astype(F32).reshape(1, S5_CH), *bf16_weights,
            norm_g[l, 1].astype(F32).reshape(1, D_MODEL), wr_hi, wr_lo, br, l)
        cls = lax.bitcast_convert_type(hx[:, D_MODEL // 2 + MOE_EXPERTS], F32).astype(jnp.int32)
        pos, tile_group, need, n_used = _moe_plan(cls)
        ys = _experts(_dispatch(hx, pos), tile_group, need, n_used, *moe_weights, l)
        if l < DEPTH - 1:
            x = _combine(ys, pos, x1, mod, fg, l, False)
        else:
            y_ctx, y_lat = _combine(ys, pos, x1, mod, fg, l, True)

    return (y_ctx.reshape(BATCH, SEQ, D_MODEL), y_lat.reshape(DEC_BATCH, DEC_SEQ, D_MODEL),
            jnp.stack(na_list, axis=1), jnp.stack(gqa_list, axis=1), jnp.stack(ssm_list, axis=1))
```

```python
import functools
import math

import numpy as np
import jax
import jax.numpy as jnp
from jax import lax
from jax.experimental import pallas as pl
from jax.experimental.pallas import tpu as pltpu

F32 = jnp.float32
BF16 = jnp.bfloat16

D_MODEL = 1024
BATCH = 32
SEQ = 256
DEPTH = 2
DEC_BATCH = 4
DEC_SEQ = 2048
PAST_LEN = 512
GRID_W = 64
GRID_ROWS = DEC_SEQ // GRID_W
HEAD_DIM = 64
NA_HEADS = 8
NA_WIN_H = 8
NA_WIN_W = 16
GQA_Q_HEADS = 8
GQA_KV_HEADS = 2
GQA_GROUP = GQA_Q_HEADS // GQA_KV_HEADS
SWA_WINDOW = 128
SWA_BLOCK = 128
ROPE_THETA = 10000.0
S5_CH = 512
S5_GROUP_CH = 16
S5_GROUPS = S5_CH // S5_GROUP_CH
S5_STATE = 64
N_BRANCH = 3
NA_WIDTH = NA_HEADS * HEAD_DIM
GQA_Q_WIDTH = GQA_Q_HEADS * HEAD_DIM
GQA_KV_WIDTH = GQA_KV_HEADS * HEAD_DIM
MOE_GROUPS = 4
MOE_EXPERTS_PER_GROUP = 4
MOE_EXPERTS = MOE_GROUPS * MOE_EXPERTS_PER_GROUP
EXPERT_FF = 512
EPS = 1e-6
NEG_INF = -1e30

N_CTX_TOK = BATCH * SEQ
N_LAT_TOK = DEC_BATCH * DEC_SEQ
N_TOK = N_CTX_TOK + N_LAT_TOK
N_COND = 8

COL_NA_Q = 0
COL_NA_K = 512
COL_NA_V = 1024
COL_G_Q = 1536
COL_G_K = 2048
COL_G_V = 2176
COL_U = 2560
COL_GATES = 3072
Z_COLS = 6144

S5_T = 16
S5_PAIRS = S5_GROUPS // 2
S5_OCT_PAIRS = 4
S5_BLOCK_TOK = 2048
S5_ROWS = S5_BLOCK_TOK // S5_T
S5_NPOW = 7

MOE_ROW_WORDS = D_MODEL // 2 + 128
MOE_TILE = 512
MOE_ROWS = N_TOK + MOE_GROUPS * MOE_TILE
MOE_TOK_BLOCK = 512
MOE_PAIR_MEMBERS = ((0, 1), (0, 2), (0, 3), (1, 3), (1, 2), (2, 3))
MOE_PAIRS = len(MOE_PAIR_MEMBERS)

CTX_HEADS_PER_DOT = 4
NA_HEADS_PER_DOT = 2
NA_QROWS = 4
NA_KROWS = 12

VMEM_LIMIT = 56 * 1024 * 1024


def _dot(a, b):
    return jnp.dot(a, b, preferred_element_type=F32)


def _dot_nt(a, b):
    return lax.dot_general(a, b, (((1,), (1,)), ((), ())), preferred_element_type=F32)


def _split_bf16(x):
    hi = x.astype(BF16)
    lo = (x - hi.astype(F32)).astype(BF16)
    return hi, lo


def _pack_pairs(x):
    w = x.shape[1] // 2
    xb = x.astype(BF16).astype(F32)
    return (pltpu.bitcast(xb[:, :w], jnp.uint32) & jnp.uint32(0xFFFF0000)) | (
        pltpu.bitcast(xb[:, w:], jnp.uint32) >> 16)


def _unpack_pairs(word):
    return jnp.concatenate([pltpu.bitcast(word & jnp.uint32(0xFFFF0000), F32),
                            pltpu.bitcast(word << 16, F32)], axis=1)


def _rms(x, g):
    return x * lax.rsqrt(jnp.mean(x * x, axis=-1, keepdims=True) + EPS) * g


def _cond_index(i, tm):
    nctx = N_CTX_TOK // tm
    return jnp.where(i < nctx, 0, 1 + ((i - nctx) * tm) // DEC_SEQ)


def _ada_kernel(c_ref, w_ref, b_ref, o_ref):
    c = c_ref[...]
    s = c * jax.nn.sigmoid(c)
    s_hi, s_lo = _split_bf16(s)
    w_hi, w_lo = _split_bf16(w_ref[...])
    o_ref[...] = _dot(s_hi, w_hi) + _dot(s_lo, w_hi) + _dot(s_hi, w_lo) + b_ref[...]


def _ada_mod(cond, w_ada, b_ada):
    tn = 1536
    n = 6 * D_MODEL
    out = pl.pallas_call(
        _ada_kernel,
        out_shape=jax.ShapeDtypeStruct((DEPTH, N_COND, n), F32),
        grid=(DEPTH, n // tn),
        in_specs=[
            pl.BlockSpec((N_COND, D_MODEL), lambda l, j: (0, 0)),
            pl.BlockSpec((None, D_MODEL, tn), lambda l, j: (l, 0, j)),
            pl.BlockSpec((None, 1, tn), lambda l, j: (l, 0, j)),
        ],
        out_specs=pl.BlockSpec((None, N_COND, tn), lambda l, j: (l, 0, j)),
        compiler_params=pltpu.CompilerParams(
            dimension_semantics=("parallel", "parallel"), vmem_limit_bytes=VMEM_LIMIT),
        name="ada_mod",
    )(cond, w_ada, b_ada.reshape(DEPTH, 1, n))
    return out.reshape(DEPTH, N_COND, 6, D_MODEL)


INPROJ_TM = 1024
INPROJ_TN = 1536


def _inproj_kernel(xc_ref, xl_ref, g_ref, mod_ref, w_ref, z_ref, u_ref, cna_ref, cgq_ref, h_sc):
    i, j = pl.program_id(0), pl.program_id(1)
    is_ctx = i < N_CTX_TOK // INPROJ_TM

    def norm_mod(x_ref):
        h = _rms(x_ref[...], g_ref[...]) * (1.0 + mod_ref[1:2, :]) + mod_ref[0:1, :]
        h_sc[...] = h.astype(BF16)

    @pl.when((j == 0) & is_ctx)
    def _():
        norm_mod(xc_ref)

    @pl.when((j == 0) & jnp.logical_not(is_ctx))
    def _():
        norm_mod(xl_ref)

    acc = _dot(h_sc[...], w_ref[...])
    z_ref[...] = acc.astype(BF16)

    def rows(b):
        return slice(b * SEQ, (b + 1) * SEQ)

    def cols(c, width):
        return slice(c % INPROJ_TN, c % INPROJ_TN + width)

    @pl.when(is_ctx & (j == COL_NA_K // INPROJ_TN))
    def _():
        for b in range(INPROJ_TM // SEQ):
            cna_ref[b, 0] = acc[rows(b), cols(COL_NA_K, NA_WIDTH)]
            cna_ref[b, 1] = acc[rows(b), cols(COL_NA_V, NA_WIDTH)]

    @pl.when(is_ctx & (j == COL_G_K // INPROJ_TN))
    def _():
        for b in range(INPROJ_TM // SEQ):
            cgq_ref[b, 0] = acc[rows(b), cols(COL_G_K, GQA_KV_WIDTH)]
            cgq_ref[b, 1] = acc[rows(b), cols(COL_G_V, GQA_KV_WIDTH)]

    @pl.when(j == COL_U // INPROJ_TN)
    def _():
        u_ref[...] = acc[:, cols(COL_U, S5_CH)]


def _token_arrays(x):
    return (x, x, N_CTX_TOK) if not isinstance(x, tuple) else (x[0], x[1], 0)


def _inproj(x, g, mod, w, l):
    tm, tn = INPROJ_TM, INPROJ_TN
    nb = tm // SEQ
    last_ctx = N_CTX_TOK // tm - 1
    x_ctx, x_lat, lat0 = _token_arrays(x)
    return pl.pallas_call(
        _inproj_kernel,
        out_shape=(jax.ShapeDtypeStruct((N_TOK, Z_COLS), BF16),
                   jax.ShapeDtypeStruct((N_TOK, S5_CH), F32),
                   jax.ShapeDtypeStruct((BATCH, 2, SEQ, NA_WIDTH), F32),
                   jax.ShapeDtypeStruct((BATCH, 2, SEQ, GQA_KV_WIDTH), F32)),
        grid=(N_TOK // tm, Z_COLS // tn),
        in_specs=[
            pl.BlockSpec((tm, D_MODEL), lambda i, j: (jnp.minimum(i, last_ctx), 0)),
            pl.BlockSpec((tm, D_MODEL), lambda i, j: (jnp.maximum(i - last_ctx - 1, 0) + lat0 // tm, 0)),
            pl.BlockSpec((1, D_MODEL), lambda i, j: (0, 0)),
            pl.BlockSpec((None, None, 6, D_MODEL), lambda i, j: (l, _cond_index(i, tm), 0, 0)),
            pl.BlockSpec((None, D_MODEL, tn), lambda i, j: (l, 0, j)),
        ],
        out_specs=(pl.BlockSpec((tm, tn), lambda i, j: (i, j)),
                   pl.BlockSpec((tm, S5_CH), lambda i, j: (i, 0)),
                   pl.BlockSpec((nb, 2, SEQ, NA_WIDTH), lambda i, j: (jnp.minimum(i, last_ctx), 0, 0, 0)),
                   pl.BlockSpec((nb, 2, SEQ, GQA_KV_WIDTH), lambda i, j: (jnp.minimum(i, last_ctx), 0, 0, 0))),
        scratch_shapes=[pltpu.VMEM((tm, D_MODEL), BF16)],
        compiler_params=pltpu.CompilerParams(
            dimension_semantics=("arbitrary", "arbitrary"), vmem_limit_bytes=VMEM_LIMIT),
        name="inproj",
    )(x_ctx, x_lat, g, mod, w)


def _block_diag_heads(x, nrep):
    t = x.shape[0]
    rows = lax.broadcasted_iota(jnp.int32, (nrep * t, nrep * HEAD_DIM), 0) // t
    lanes = lax.broadcasted_iota(jnp.int32, (nrep * t, nrep * HEAD_DIM), 1) // HEAD_DIM
    return jnp.where(rows == lanes, jnp.concatenate([x] * nrep, axis=0), jnp.zeros((), x.dtype))


def _per_head(cols, shape):
    head = lax.broadcasted_iota(jnp.int32, shape, 1) // HEAD_DIM
    out = cols[-1]
    for h in range(len(cols) - 2, -1, -1):
        out = jnp.where(head == h, cols[h], out)
    return out


def _attend_heads(q, k, v, sinks):
    nh = q.shape[1] // HEAD_DIM
    t = k.shape[0]
    s = _dot_nt(q * HEAD_DIM ** -0.5, _block_diag_heads(k, nh))
    ps, ds = [], []
    for h in range(nh):
        sh = s[:, h * t:(h + 1) * t]
        m = jnp.max(sh, axis=-1, keepdims=True)
        if sinks is not None:
            m = jnp.maximum(m, sinks[h])
        p = jnp.exp(sh - m)
        d = jnp.sum(p, axis=-1, keepdims=True)
        if sinks is not None:
            d = d + jnp.exp(sinks[h] - m)
        ps.append(p.astype(BF16))
        ds.append(d)
    o = _dot(jnp.concatenate(ps, axis=1), _block_diag_heads(v, nh))
    return o / _per_head(ds, o.shape)


def _ctx_attn_kernel(sink_ref, na_ref, gq_ref, gkv_ref, oa_ref, ob_ref):
    hb = CTX_HEADS_PER_DOT
    w = hb * HEAD_DIM
    na = na_ref[...]
    gq = gq_ref[...]
    gkv = gkv_ref[...]
    for g in range(NA_HEADS // hb):
        o = _attend_heads(na[:, COL_NA_Q + g * w:COL_NA_Q + (g + 1) * w],
                          na[:, COL_NA_K + g * w:COL_NA_K + (g + 1) * w],
                          na[:, COL_NA_V + g * w:COL_NA_V + (g + 1) * w], None)
        oa_ref[:, g * w:(g + 1) * w] = o.astype(BF16)
    for g in range(GQA_Q_HEADS // hb):
        kv_heads = [(g * hb + h) // GQA_GROUP for h in range(hb)]
        k = jnp.concatenate([gkv[:, kh * HEAD_DIM:(kh + 1) * HEAD_DIM] for kh in kv_heads], axis=1)
        v = jnp.concatenate([gkv[:, GQA_KV_WIDTH + kh * HEAD_DIM:GQA_KV_WIDTH + (kh + 1) * HEAD_DIM]
                             for kh in kv_heads], axis=1)
        o = _attend_heads(gq[:, g * w:(g + 1) * w], k, v, [sink_ref[g * hb + h] for h in range(hb)])
        ob_ref[:, g * w:(g + 1) * w] = o.astype(BF16)


def _ctx_attn(z, sink):
    out = jax.ShapeDtypeStruct((N_CTX_TOK, NA_WIDTH), BF16)
    return pl.pallas_call(
        _ctx_attn_kernel,
        out_shape=(out, out),
        grid=(BATCH,),
        in_specs=[
            pl.BlockSpec(memory_space=pltpu.SMEM),
            pl.BlockSpec((SEQ, 3 * NA_WIDTH), lambda b: (b, 0)),
            pl.BlockSpec((SEQ, GQA_Q_WIDTH), lambda b: (b, COL_G_Q // GQA_Q_WIDTH)),
            pl.BlockSpec((SEQ, 2 * GQA_KV_WIDTH), lambda b: (b, COL_G_K // (2 * GQA_KV_WIDTH))),
        ],
        out_specs=(pl.BlockSpec((SEQ, NA_WIDTH), lambda b: (b, 0)),
                   pl.BlockSpec((SEQ, GQA_Q_WIDTH), lambda b: (b, 0))),
        compiler_params=pltpu.CompilerParams(
            dimension_semantics=("parallel",), vmem_limit_bytes=VMEM_LIMIT),
        name="ctx_attn",
    )(sink, z, z, z)


def _na_bias_tables(rpb):
    n_dc = 2 * NA_WIN_W - 1
    qc = np.arange(GRID_W)
    kc = np.arange(GRID_W)
    cstart = np.clip(qc - NA_WIN_W // 2, 0, GRID_W - NA_WIN_W)
    col_ok = (kc[None, :] >= cstart[:, None]) & (kc[None, :] < cstart[:, None] + NA_WIN_W)
    dc = np.clip(kc[None, :] - qc[:, None] + (NA_WIN_W - 1), 0, n_dc - 1)
    onehot = ((np.arange(n_dc)[:, None, None] == dc[None]) & col_ok[None]).astype(np.float32)
    band = jnp.einsum("hrd,dqk->hrqk", rpb.astype(F32), onehot, precision=lax.Precision.HIGHEST)
    band = jnp.where(col_ok, band, NEG_INF)
    band = jnp.pad(band, ((0, 0), (1, 1), (0, 0), (0, 0)), constant_values=NEG_INF)
    return jnp.concatenate([band[:, :-1], band[:, 1:]], axis=-1)


def _na_lat_kernel(q_ref, k_ref, v_ref, ck_ref, cv_ref, bias_ref, o_ref):
    i = pl.program_id(1)
    scale = HEAD_DIM ** -0.5
    nk = NA_KROWS * GRID_W
    kh = min(NA_WIN_H, GRID_ROWS)
    ks = jnp.clip(i * NA_QROWS - NA_WIN_H // 2, 0, GRID_ROWS - NA_KROWS)
    start = pl.multiple_of(ks * GRID_W, GRID_W)
    q = q_ref[...].astype(BF16)
    kw = k_ref[pl.ds(start, nk), :].astype(BF16)
    vw = v_ref[pl.ds(start, nk), :].astype(BF16)
    ck = ck_ref[...].astype(BF16)
    cv = cv_ref[...].astype(BF16)
    left = lax.broadcasted_iota(jnp.int32, (GRID_W, 2 * GRID_W), 1) < GRID_W
    pieces = []
    for qr in range(NA_QROWS):
        r = i * NA_QROWS + qr
        st = jnp.clip(r - kh // 2, 0, GRID_ROWS - kh)
        row = []
        for m in range(NA_KROWS // 2):
            k0 = ks + 2 * m
            plane = jnp.clip(k0 - r + NA_WIN_H, 0, 2 * NA_WIN_H - 1)
            ok0 = (k0 >= st) & (k0 < st + kh)
            ok1 = (k0 + 1 >= st) & (k0 + 1 < st + kh)
            row.append((plane, jnp.where(left, ok0.astype(jnp.int32), ok1.astype(jnp.int32)) > 0))
        pieces.append(row)
    hb = NA_HEADS_PER_DOT
    w = hb * HEAD_DIM
    for g in range(NA_HEADS // hb):
        lanes = slice(g * w, (g + 1) * w)
        qg = q[:, lanes] * scale
        s_loc = _dot_nt(qg, _block_diag_heads(kw[:, lanes], hb))
        s_ctx = _dot_nt(qg, _block_diag_heads(ck[:, lanes], hb))
        p_locs, p_ctxs, ds = [], [], []
        for hh in range(hb):
            bias = jnp.concatenate(
                [jnp.concatenate([jnp.where(ok, bias_ref[g * hb + hh, plane], NEG_INF) for plane, ok in row], axis=1)
                 for row in pieces], axis=0)
            sl = s_loc[:, hh * nk:(hh + 1) * nk] + bias
            sc = s_ctx[:, hh * PAST_LEN:(hh + 1) * PAST_LEN]
            m = jnp.maximum(jnp.max(sl, axis=-1, keepdims=True), jnp.max(sc, axis=-1, keepdims=True))
            p_loc = jnp.exp(sl - m)
            p_ctx = jnp.exp(sc - m)
            ds.append(jnp.sum(p_loc, axis=-1, keepdims=True) + jnp.sum(p_ctx, axis=-1, keepdims=True))
            p_locs.append(p_loc.astype(BF16))
            p_ctxs.append(p_ctx.astype(BF16))
        o = (_dot(jnp.concatenate(p_locs, axis=1), _block_diag_heads(vw[:, lanes], hb))
             + _dot(jnp.concatenate(p_ctxs, axis=1), _block_diag_heads(cv[:, lanes], hb)))
        o_ref[:, lanes] = (o / _per_head(ds, o.shape)).astype(BF16)


def _na_lat(z, cache_na, bias, l):
    tq = NA_QROWS * GRID_W
    nsteps = GRID_ROWS // NA_QROWS
    lat_blk = N_CTX_TOK // DEC_SEQ
    return pl.pallas_call(
        _na_lat_kernel,
        out_shape=jax.ShapeDtypeStruct((N_LAT_TOK, NA_WIDTH), BF16),
        grid=(DEC_BATCH, nsteps),
        in_specs=[
            pl.BlockSpec((tq, NA_WIDTH), lambda b, i: (N_CTX_TOK // tq + b * nsteps + i, COL_NA_Q // NA_WIDTH)),
            pl.BlockSpec((DEC_SEQ, NA_WIDTH), lambda b, i: (lat_blk + b, COL_NA_K // NA_WIDTH)),
            pl.BlockSpec((DEC_SEQ, NA_WIDTH), lambda b, i: (lat_blk + b, COL_NA_V // NA_WIDTH)),
            pl.BlockSpec((None, None, None, PAST_LEN, NA_WIDTH), lambda b, i: (b, l, 0, 0, 0)),
            pl.BlockSpec((None, None, None, PAST_LEN, NA_WIDTH), lambda b, i: (b, l, 1, 0, 0)),
            pl.BlockSpec((NA_HEADS, 2 * NA_WIN_H, GRID_W, 2 * GRID_W), lambda b, i: (0, 0, 0, 0)),
        ],
        out_specs=pl.BlockSpec((tq, NA_WIDTH), lambda b, i: (b * nsteps + i, 0)),
        compiler_params=pltpu.CompilerParams(
            dimension_semantics=("parallel", "arbitrary"), vmem_limit_bytes=VMEM_LIMIT),
        name="na_lat",
    )(z, z, z, cache_na, cache_na, bias)


def _rope_tables():
    nf = HEAD_DIM // 4
    t = jnp.arange(DEC_SEQ)
    pos = jnp.stack([t // GRID_W, t % GRID_W], axis=-1).astype(F32)
    inv = ROPE_THETA ** (-jnp.arange(nf, dtype=F32) / nf)
    ang = pos[:, :, None] * inv
    cos = jnp.cos(ang)
    sin = jnp.sin(ang)
    cos_d = jnp.stack([cos, cos], axis=2).reshape(DEC_SEQ, HEAD_DIM)
    sin_d = jnp.stack([-sin, sin], axis=2).reshape(DEC_SEQ, HEAD_DIM)
    return cos_d, sin_d


def _rope(x, cos, sin_signed):
    n = x.shape[-1]
    nf = HEAD_DIM // 4
    lane = lax.broadcasted_iota(jnp.int32, x.shape, 1)
    first_half = (lane // nf) % 2 == 0
    partner = jnp.where(first_half, pltpu.roll(x, n - nf, 1), pltpu.roll(x, nf, 1))
    return x * cos + partner * sin_signed


def _swa_lat_kernel(sink_ref, q_ref, kv_ref, cq_ref, sq_ref, ckt_ref, skt_ref, ck_ref, cv_ref, o_ref, k_sc, v_sc):
    n = pl.program_id(1)
    scale = HEAD_DIM ** -0.5
    nwin = 3 * SWA_BLOCK

    @pl.when(n == 0)
    def _():
        kv = kv_ref[...]
        k_sc[...] = _rope(kv[:, :GQA_KV_WIDTH].astype(F32), ckt_ref[...], skt_ref[...]).astype(BF16)
        v_sc[...] = kv[:, GQA_KV_WIDTH:].astype(BF16)

    q = (_rope(q_ref[...].astype(F32), cq_ref[...], sq_ref[...]) * scale).astype(BF16)
    start = pl.multiple_of(jnp.clip((n - 1) * SWA_BLOCK, 0, DEC_SEQ - nwin), SWA_BLOCK)
    kw = k_sc[pl.ds(start, nwin), :]
    vw = v_sc[pl.ds(start, nwin), :]
    ck = ck_ref[...].astype(BF16)
    cv = cv_ref[...].astype(BF16)
    rows = GQA_GROUP * SWA_BLOCK
    row = lax.broadcasted_iota(jnp.int32, (rows, nwin), 0)
    col = lax.broadcasted_iota(jnp.int32, (rows, nwin), 1)
    qpos = n * SWA_BLOCK + row % SWA_BLOCK
    kpos = start + col
    ok = jnp.abs(qpos - kpos) <= SWA_WINDOW
    grp = lax.broadcasted_iota(jnp.int32, (rows, 1), 0) // SWA_BLOCK
    for kh in range(GQA_KV_HEADS):
        sl = slice(kh * HEAD_DIM, (kh + 1) * HEAD_DIM)
        q4 = jnp.concatenate(
            [q[:, (kh * GQA_GROUP + g) * HEAD_DIM:(kh * GQA_GROUP + g + 1) * HEAD_DIM] for g in range(GQA_GROUP)],
            axis=0)
        sink = jnp.zeros((rows, 1), F32)
        for g in range(GQA_GROUP):
            sink = jnp.where(grp == g, sink_ref[kh * GQA_GROUP + g], sink)
        s_loc = jnp.where(ok, _dot_nt(q4, kw[:, sl]), NEG_INF)
        s_ctx = _dot_nt(q4, ck[:, sl])
        m = jnp.maximum(jnp.maximum(jnp.max(s_loc, axis=-1, keepdims=True),
                                    jnp.max(s_ctx, axis=-1, keepdims=True)), sink)
        p_loc = jnp.exp(s_loc - m)
        p_ctx = jnp.exp(s_ctx - m)
        d = (jnp.sum(p_loc, axis=-1, keepdims=True) + jnp.sum(p_ctx, axis=-1, keepdims=True)
             + jnp.exp(sink - m))
        o4 = (_dot(p_loc.astype(BF16), vw[:, sl]) + _dot(p_ctx.astype(BF16), cv[:, sl])) / d
        for g in range(GQA_GROUP):
            h = kh * GQA_GROUP + g
            o_ref[:, h * HEAD_DIM:(h + 1) * HEAD_DIM] = o4[g * SWA_BLOCK:(g + 1) * SWA_BLOCK].astype(BF16)


def _swa_lat(z, cache_gqa, sink, cos_q, sin_q, cos_k, sin_k, l):
    nb = DEC_SEQ // SWA_BLOCK
    lat_blk = N_CTX_TOK // DEC_SEQ
    return pl.pallas_call(
        _swa_lat_kernel,
        out_shape=jax.ShapeDtypeStruct((N_LAT_TOK, GQA_Q_WIDTH), BF16),
        grid=(DEC_BATCH, nb),
        in_specs=[
            pl.BlockSpec(memory_space=pltpu.SMEM),
            pl.BlockSpec((SWA_BLOCK, GQA_Q_WIDTH),
                         lambda b, n: (N_CTX_TOK // SWA_BLOCK + b * nb + n, COL_G_Q // GQA_Q_WIDTH)),
            pl.BlockSpec((DEC_SEQ, 2 * GQA_KV_WIDTH), lambda b, n: (lat_blk + b, COL_G_K // (2 * GQA_KV_WIDTH))),
            pl.BlockSpec((SWA_BLOCK, GQA_Q_WIDTH), lambda b, n: (n, 0)),
            pl.BlockSpec((SWA_BLOCK, GQA_Q_WIDTH), lambda b, n: (n, 0)),
            pl.BlockSpec((DEC_SEQ, GQA_KV_WIDTH), lambda b, n: (0, 0)),
            pl.BlockSpec((DEC_SEQ, GQA_KV_WIDTH), lambda b, n: (0, 0)),
            pl.BlockSpec((None, None, None, PAST_LEN, GQA_KV_WIDTH), lambda b, n: (b, l, 0, 0, 0)),
            pl.BlockSpec((None, None, None, PAST_LEN, GQA_KV_WIDTH), lambda b, n: (b, l, 1, 0, 0)),
        ],
        out_specs=pl.BlockSpec((SWA_BLOCK, GQA_Q_WIDTH), lambda b, n: (b * nb + n, 0)),
        scratch_shapes=[pltpu.VMEM((DEC_SEQ, GQA_KV_WIDTH), BF16), pltpu.VMEM((DEC_SEQ, GQA_KV_WIDTH), BF16)],
        compiler_params=pltpu.CompilerParams(
            dimension_semantics=("parallel", "arbitrary"), vmem_limit_bytes=VMEM_LIMIT),
        name="swa_lat",
    )(sink, z, z, cos_q, sin_q, cos_k, sin_k, cache_gqa, cache_gqa)


S5_OP_KEYS = ("kt", "bt_re", "bt_im", "ct_re", "ct_im", "pin_re", "pin_im", "po_re", "po_im", "apr", "api")


def _s5_operators(a_re, a_im, log_dt, b_re, b_im, c_re, c_im):
    T, G, N, C = S5_T, S5_GROUPS, S5_STATE, S5_GROUP_CH
    tau = jnp.arange(T + 1, dtype=F32)
    out = {k: [] for k in S5_OP_KEYS}
    for d in range(2):
        A = lax.complex(a_re[d].astype(F32), a_im[d].astype(F32))
        dt = jnp.exp(log_dt[d].astype(F32))[:, None]
        a_bar = jnp.exp(A * dt)
        pw = jnp.exp((A * dt)[None] * tau[:, None, None])
        b_bar = ((a_bar - 1.0) / A)[..., None] * lax.complex(b_re[d].astype(F32), b_im[d].astype(F32))
        c_mat = lax.complex(c_re[d].astype(F32), c_im[d].astype(F32))
        kern = jnp.einsum("gon,tgn,gni->gito", c_mat, pw[:T], b_bar, precision=lax.Precision.HIGHEST).real
        if d == 0:
            p_in = pw[:T][::-1]
            p_out = pw[1:T + 1]
        else:
            p_in = pw[:T]
            p_out = pw[1:T + 1][::-1]
            kern = kern[:, :, ::-1]
        bt, ct, po = b_bar.transpose(0, 2, 1), c_mat.transpose(0, 2, 1), p_out.transpose(1, 2, 0)
        pw2 = jnp.exp((A * dt)[None] * (T * 2.0 ** jnp.arange(S5_NPOW, dtype=F32))[:, None, None])
        vals = (kern.reshape(G, C, T * C), bt.real, bt.imag, ct.real, ct.imag,
                p_in.real.reshape(T, S5_PAIRS, 1, 2 * N), p_in.imag.reshape(T, S5_PAIRS, 1, 2 * N), po.real, po.imag,
                pw2.real.reshape(S5_NPOW, G * N), pw2.imag.reshape(S5_NPOW, G * N))
        for k, v in zip(S5_OP_KEYS, vals):
            out[k].append(v)
    return {k: jnp.stack(v) for k, v in out.items()}


def _s5_expand(kt_ref, btr_ref, bti_ref, ctr_ref, cti_ref, pir_ref, pii_ref, por_ref, poi_ref,
               bs_sc, m_sc, cre_sc, cim_sc):
    T, C, N = S5_T, S5_GROUP_CH, S5_STATE
    pc, half, width = 2 * C, 2 * N, 2 * T * C
    lane = lax.broadcasted_iota(jnp.int32, (pc, width), 1)
    zc, zn, zo = jnp.zeros((C, C), F32), jnp.zeros((C, N), F32), jnp.zeros((N, C), F32)

    def own_half(x, z, g):
        return [x, z] if g == 0 else [z, x]

    def expand_pair(p, d):
        k0, bx_re, bx_im, co_re, co_im = [], [], [], [], []
        for g in range(2):
            gl = 2 * p + g
            ktg = kt_ref[d, gl]
            k0.append(jnp.concatenate(
                [blk for lag in range(T) for blk in own_half(ktg[:, lag * C:(lag + 1) * C], zc, g)], axis=1))
            bx_re.append(jnp.concatenate(own_half(btr_ref[d, gl], zn, g), axis=1))
            bx_im.append(jnp.concatenate(own_half(bti_ref[d, gl], zn, g), axis=1))
            cr, ci = ctr_ref[d, gl], cti_ref[d, gl]
            pr_all, pi_all = por_ref[d, gl], poi_ref[d, gl]
            re_p, im_p = [], []
            for j in range(T):
                pr, pi = pr_all[:, j:j + 1], pi_all[:, j:j + 1]
                re_p += own_half(cr * pr - ci * pi, zo, g)
                im_p += own_half(cr * pi + ci * pr, zo, g)
            co_re.append(jnp.concatenate(re_p, axis=1))
            co_im.append(jnp.concatenate(im_p, axis=1))
        k0 = jnp.concatenate(k0, axis=0)
        bx_re, bx_im = jnp.concatenate(bx_re, axis=0), jnp.concatenate(bx_im, axis=0)
        for s in range(T):
            if d == 0:
                sh = pc * s
                blk = jnp.where(lane >= sh, pltpu.roll(k0, sh, 1), 0.0) if sh else k0
            else:
                sh = pc * (T - 1 - s)
                blk = jnp.where(lane < width - sh, pltpu.roll(k0, width - sh, 1), 0.0) if sh else k0
            m_sc[d, p, s * pc:(s + 1) * pc, :] = blk.astype(BF16)
            pr, pi = pir_ref[d, s, p], pii_ref[d, s, p]
            bs_sc[d, p, s * pc:(s + 1) * pc, :] = jnp.concatenate(
                [pr * bx_re - pi * bx_im, pr * bx_im + pi * bx_re], axis=1).astype(BF16)
        cre_sc[d, p] = jnp.concatenate(co_re, axis=0).astype(BF16)
        cim_sc[d, p] = (-jnp.concatenate(co_im, axis=0)).astype(BF16)

    for d in range(2):
        lax.fori_loop(0, S5_OCT_PAIRS, lambda p, c, d=d: (expand_pair(p, d), c)[1], 0)


def _s5_kernel(u_ref, kt_ref, btr_ref, bti_ref, ctr_ref, cti_ref, pir_ref, pii_ref, por_ref, poi_ref, apr_ref,
               api_ref, h0_ref, y_ref, fin_ref, fin_sc, bs_sc, m_sc, cre_sc, cim_sc):
    t = pl.program_id(1)
    nctx = N_CTX_TOK // S5_BLOCK_TOK

    @pl.when(t == 0)
    def _():
        _s5_expand(kt_ref, btr_ref, bti_ref, ctr_ref, cti_ref, pir_ref, pii_ref, por_ref, poi_ref,
                   bs_sc, m_sc, cre_sc, cim_sc)

    refs = (u_ref, bs_sc, m_sc, cre_sc, cim_sc, apr_ref, api_ref, h0_ref, y_ref, fin_ref, fin_sc)

    @pl.when(t < nctx)
    def _():
        _s5_block(*refs, t, kseq=SEQ // S5_T, has_h0=False)

    @pl.when(t >= nctx)
    def _():
        _s5_block(*refs, t - nctx, kseq=DEC_SEQ // S5_T, has_h0=True)


def _s5_block(u_ref, bs_ref, m_ref, cre_ref, cim_ref, apr_ref, api_ref, h0_ref, y_ref, fin_ref, fin_sc, t, *,
              kseq, has_h0):
    rows, half, pc = S5_ROWS, 2 * S5_STATE, 2 * S5_GROUP_CH
    k = lax.broadcasted_iota(jnp.int32, (rows, half), 0) % kseq
    shifts = [1 << i for i in range(kseq.bit_length() - 1)]
    xs = [u_ref[pl.ds(s, rows, stride=S5_T), :].astype(BF16) for s in range(S5_T)]
    ys = []
    for p in range(S5_OCT_PAIRS):
        lanes = slice(p * half, (p + 1) * half)
        xp = jnp.concatenate([x[:, p * pc:(p + 1) * pc] for x in xs], axis=1)
        acc = None
        for d in range(2):
            inc = _dot(xp, bs_ref[d, p])
            sr, si = inc[:, :half], inc[:, half:]
            if has_h0:
                h0r = h0_ref[d, 0, t, :, lanes]
                h0i = h0_ref[d, 1, t, :, lanes]
                ar, ai = apr_ref[d, 0:1, lanes], api_ref[d, 0:1, lanes]
                first = (k == 0) if d == 0 else (k == kseq - 1)
                sr = sr + jnp.where(first, ar * h0r - ai * h0i, 0.0)
                si = si + jnp.where(first, ar * h0i + ai * h0r, 0.0)
            else:
                h0r = h0i = 0.0
            for i, sh in enumerate(shifts):
                ar, ai = apr_ref[d, i:i + 1, lanes], api_ref[d, i:i + 1, lanes]
                ok = (k >= sh) if d == 0 else (k < kseq - sh)
                amt = sh if d == 0 else rows - sh
                rr = jnp.where(ok, pltpu.roll(sr, amt, 0), 0.0)
                ri = jnp.where(ok, pltpu.roll(si, amt, 0), 0.0)
                sr, si = sr + ar * rr - ai * ri, si + ar * ri + ai * rr
            inner = (k >= 1) if d == 0 else (k < kseq - 1)
            amt = 1 if d == 0 else rows - 1
            hr = jnp.where(inner, pltpu.roll(sr, amt, 0), h0r)
            hi = jnp.where(inner, pltpu.roll(si, amt, 0), h0i)
            yd = (_dot(xp, m_ref[d, p]) + _dot(hr.astype(BF16), cre_ref[d, p])
                  + _dot(hi.astype(BF16), cim_ref[d, p]))
            acc = yd if acc is None else acc + yd
            if not has_h0:
                last = kseq - 1 if d == 0 else 0
                fin_sc[0] = sr
                fin_sc[1] = si
                fin_ref[d, 0, :, lanes] = fin_sc[0, pl.ds(last, rows // kseq, stride=kseq), :]
                fin_ref[d, 1, :, lanes] = fin_sc[1, pl.ds(last, rows // kseq, stride=kseq), :]
        ys.append(acc)
    for j in range(S5_T):
        y_ref[pl.ds(j, rows, stride=S5_T), :] = jnp.concatenate([y[:, j * pc:(j + 1) * pc] for y in ys], axis=1)


def _s5_scan(u, ops, h0, l):
    nblk, nctx = N_TOK // S5_BLOCK_TOK, N_CTX_TOK // S5_BLOCK_TOK
    noct = S5_PAIRS // S5_OCT_PAIRS
    half, gn = 2 * S5_STATE, S5_GROUPS * S5_STATE
    pw = 2 * S5_T * S5_GROUP_CH
    nseq = S5_BLOCK_TOK // SEQ
    ngrp = 2 * S5_OCT_PAIRS
    C, N, T = S5_GROUP_CH, S5_STATE, S5_T
    gspec = lambda r, c: pl.BlockSpec((None, 2, ngrp, r, c), lambda q, t: (l, 0, q, 0, 0))
    lspec = lambda r: pl.BlockSpec((None, 2, r, S5_OCT_PAIRS * half), lambda q, t: (l, 0, 0, q))
    pspec = pl.BlockSpec((None, 2, T, S5_OCT_PAIRS, 1, half), lambda q, t: (l, 0, 0, q, 0, 0))
    return pl.pallas_call(
        _s5_kernel,
        grid=(noct, nblk),
        in_specs=[
            pl.BlockSpec((S5_BLOCK_TOK, 128), lambda q, t: (t, q)),
            gspec(C, T * C), gspec(C, N), gspec(C, N), gspec(N, C), gspec(N, C), pspec, pspec,
            gspec(N, T), gspec(N, T), lspec(S5_NPOW), lspec(S5_NPOW),
            pl.BlockSpec((2, 2, DEC_BATCH, 1, S5_OCT_PAIRS * half), lambda q, t: (0, 0, 0, 0, q)),
        ],
        out_shape=(jax.ShapeDtypeStruct((N_TOK, S5_CH), F32), jax.ShapeDtypeStruct((2, 2, BATCH, gn), F32)),
        out_specs=(pl.BlockSpec((S5_BLOCK_TOK, 128), lambda q, t: (t, q)),
                   pl.BlockSpec((2, 2, nseq, S5_OCT_PAIRS * half),
                                lambda q, t: (0, 0, jnp.minimum(t, nctx - 1), q))),
        scratch_shapes=[pltpu.VMEM((2, S5_ROWS, half), F32),
                        pltpu.VMEM((2, S5_OCT_PAIRS, pw, 2 * half), BF16), pltpu.VMEM((2, S5_OCT_PAIRS, pw, pw), BF16),
                        pltpu.VMEM((2, S5_OCT_PAIRS, half, pw), BF16), pltpu.VMEM((2, S5_OCT_PAIRS, half, pw), BF16)],
        compiler_params=pltpu.CompilerParams(
            dimension_semantics=("parallel", "arbitrary"), vmem_limit_bytes=VMEM_LIMIT),
        name="s5_scan",
    )(u, *[ops[k] for k in S5_OP_KEYS], h0.reshape(2, 2, DEC_BATCH, 1, gn))


def _route(logits):
    lane = lax.broadcasted_iota(jnp.int32, logits.shape, 1)
    big = jnp.int32(1 << 20)
    is_g = (lane >= MOE_EXPERTS) & (lane < MOE_EXPERTS + MOE_GROUPS)
    lg = jnp.where(is_g, logits, NEG_INF)
    gmax = jnp.max(lg, axis=-1, keepdims=True)
    gsel = jnp.min(jnp.where(is_g & (lg == gmax), lane, big), axis=-1, keepdims=True) - MOE_EXPERTS
    p_group = 1.0 / jnp.sum(jnp.where(is_g, jnp.exp(lg - gmax), 0.0), axis=-1, keepdims=True)
    in_grp = (lane < MOE_EXPERTS) & (lane // MOE_EXPERTS_PER_GROUP == gsel)
    le = jnp.where(in_grp, logits, NEG_INF)
    v1 = jnp.max(le, axis=-1, keepdims=True)
    i1 = jnp.min(jnp.where(in_grp & (le == v1), lane, big), axis=-1, keepdims=True)
    rest = in_grp & (lane != i1)
    le2 = jnp.where(rest, logits, NEG_INF)
    v2 = jnp.max(le2, axis=-1, keepdims=True)
    i2 = jnp.min(jnp.where(rest & (le2 == v2), lane, big), axis=-1, keepdims=True)
    e2 = jnp.exp(v2 - v1)
    w1 = 1.0 / (1.0 + e2)
    w2 = e2 / (1.0 + e2)
    comb = jnp.where(lane == i1, w1 * p_group, 0.0) + jnp.where(lane == i2, w2 * p_group, 0.0)
    a = jnp.minimum(i1, i2) - gsel * MOE_EXPERTS_PER_GROUP
    b = jnp.maximum(i1, i2) - gsel * MOE_EXPERTS_PER_GROUP
    pair = jnp.where(a == 0, b - 1, jnp.where(a == 1, jnp.where(b == 3, 3, 4), 5))
    cls = gsel * MOE_PAIRS + pair
    return jnp.where(lane == MOE_EXPERTS, cls.astype(F32), comb)


def _merge_kernel(xc_ref, xl_ref, oac_ref, oal_ref, obc_ref, obl_ref, u_ref, yc_ref, gates_ref, mod_ref, d_ref,
                  wglu_ref, wa_ref, wb_ref, wc_ref, wout_ref, g2_ref, wrh_ref, wrl_ref, br_ref, x1_ref, hx_ref,
                  *, tm):
    is_ctx = pl.program_id(0) < N_CTX_TOK // tm
    x = jnp.where(is_ctx, xc_ref[...], xl_ref[...])
    oa = jnp.where(is_ctx, oac_ref[...], oal_ref[...])
    ob = jnp.where(is_ctx, obc_ref[...], obl_ref[...])
    y = u_ref[...] * d_ref[...] + yc_ref[...]
    y = y * (0.5 * (1.0 + jnp.tanh(math.sqrt(2.0 / math.pi) * (y + 0.044715 * (y * y * y)))))
    oc = y * jax.nn.sigmoid(_dot(y.astype(BF16), wglu_ref[...]))
    gate = jax.nn.sigmoid(gates_ref[...].astype(F32))
    merged = (gate[:, :D_MODEL] * _dot(oa, wa_ref[...])
              + gate[:, D_MODEL:2 * D_MODEL] * _dot(ob, wb_ref[...])
              + gate[:, 2 * D_MODEL:] * _dot(oc.astype(BF16), wc_ref[...]))
    x1 = x + mod_ref[2:3, :] * _dot(merged.astype(BF16), wout_ref[...])
    x1_ref[...] = x1
    h2 = _rms(x1, g2_ref[...]) * (1.0 + mod_ref[4:5, :]) + mod_ref[3:4, :]
    h_hi, h_lo = _split_bf16(h2)
    logits = _dot(h_hi, wrh_ref[...]) + _dot(h_lo, wrh_ref[...]) + _dot(h_hi, wrl_ref[...]) + br_ref[...]
    half = D_MODEL // 2
    hx_ref[:, :half] = _pack_pairs(h2)
    hx_ref[:, half:] = pltpu.bitcast(_route(logits), jnp.uint32)


def _merge(x, oa_c, oa_l, ob_c, ob_l, u, yc, z, mod, d_s5, wglu, wa, wb, wc, wout, g2, wr_hi, wr_lo, br, l):
    tm = 512
    nctx = N_CTX_TOK // tm
    full = lambda r, c: pl.BlockSpec((r, c), lambda i: (0, 0))
    layer = lambda r, c: pl.BlockSpec((None, r, c), lambda i: (l, 0, 0))
    ctx_blk = lambda w: pl.BlockSpec((tm, w), lambda i: (jnp.minimum(i, nctx - 1), 0))
    lat_blk = lambda w, first=0: pl.BlockSpec((tm, w), lambda i: (jnp.maximum(i - nctx, 0) + first // tm, 0))
    x_ctx, x_lat, lat0 = _token_arrays(x)
    return pl.pallas_call(
        functools.partial(_merge_kernel, tm=tm),
        out_shape=(jax.ShapeDtypeStruct((N_TOK, D_MODEL), F32),
                   jax.ShapeDtypeStruct((N_TOK, MOE_ROW_WORDS), jnp.uint32)),
        grid=(N_TOK // tm,),
        in_specs=[
            ctx_blk(D_MODEL), lat_blk(D_MODEL, lat0),
            ctx_blk(NA_WIDTH), lat_blk(NA_WIDTH), ctx_blk(GQA_Q_WIDTH), lat_blk(GQA_Q_WIDTH),
            pl.BlockSpec((tm, S5_CH), lambda i: (i, 0)),
            pl.BlockSpec((tm, S5_CH), lambda i: (i, 0)),
            pl.BlockSpec((tm, N_BRANCH * D_MODEL), lambda i: (i, COL_GATES // (N_BRANCH * D_MODEL))),
            pl.BlockSpec((None, None, 6, D_MODEL), lambda i: (l, _cond_index(i, tm), 0, 0)),
            full(1, S5_CH), layer(S5_CH, S5_CH), layer(NA_WIDTH, D_MODEL), layer(GQA_Q_WIDTH, D_MODEL),
            layer(S5_CH, D_MODEL), layer(D_MODEL, D_MODEL), full(1, D_MODEL),
            full(D_MODEL, 128), full(D_MODEL, 128), full(1, 128),
        ],
        out_specs=(pl.BlockSpec((tm, D_MODEL), lambda i: (i, 0)),
                   pl.BlockSpec((tm, MOE_ROW_WORDS), lambda i: (i, 0))),
        compiler_params=pltpu.CompilerParams(
            dimension_semantics=("parallel",), vmem_limit_bytes=VMEM_LIMIT),
        name="merge",
    )(x_ctx, x_lat, oa_c, oa_l, ob_c, ob_l, u, yc, z, mod, d_s5, wglu, wa, wb, wc, wout, g2, wr_hi, wr_lo, br)


def _moe_plan(cls):
    ncls = MOE_GROUPS * MOE_PAIRS
    ntiles = MOE_ROWS // MOE_TILE
    onehot = (cls[:, None] == jnp.arange(ncls)[None, :]).astype(jnp.int32)
    rank = jnp.sum((jnp.cumsum(onehot, axis=0) - onehot) * onehot, axis=1)
    ccount = jnp.sum(onehot, axis=0).reshape(MOE_GROUPS, MOE_PAIRS)
    gcount = jnp.sum(ccount, axis=1)
    gpadded = (gcount + MOE_TILE - 1) // MOE_TILE * MOE_TILE
    gend = jnp.cumsum(gpadded)
    cstart = ((gend - gpadded)[:, None] + jnp.cumsum(ccount, axis=1) - ccount).reshape(ncls)
    cend = cstart + ccount.reshape(ncls)
    pos = jnp.sum(onehot * cstart[None, :], axis=1) + rank
    tile_row = jnp.arange(ntiles) * MOE_TILE
    tile_group = jnp.minimum(jnp.sum((tile_row[:, None] >= gend[None, :]).astype(jnp.int32), axis=1),
                             MOE_GROUPS - 1)
    member = np.zeros((ncls, MOE_GROUPS, MOE_EXPERTS_PER_GROUP), np.float32)
    for g in range(MOE_GROUPS):
        for p, pair in enumerate(MOE_PAIR_MEMBERS):
            member[g * MOE_PAIRS + p, g, list(pair)] = 1.0
    overlap = ((cstart[None, :] < tile_row[:, None] + MOE_TILE) & (cend[None, :] > tile_row[:, None])
               & (cend > cstart)[None, :]).astype(F32)
    need = jnp.einsum("tc,cge->tge", overlap, member)
    need = jnp.sum(need * (tile_group[:, None, None] == jnp.arange(MOE_GROUPS)[None, :, None]), axis=1)
    return (pos.astype(jnp.int32), tile_group.astype(jnp.int32), (need > 0).astype(jnp.int32).reshape(-1),
            (gend[-1:] // MOE_TILE).astype(jnp.int32))


def _dispatch_kernel(pos_ref, hx_ref, init_ref, out_ref, sem):
    del init_ref
    base = pl.program_id(0) * MOE_TOK_BLOCK

    def row_copy(r):
        return pltpu.make_async_copy(hx_ref.at[pl.ds(r, 1), :], out_ref.at[pl.ds(pos_ref[base + r], 1), :], sem)

    def start(r, c):
        row_copy(r).start()
        return c

    def wait(r, c):
        row_copy(r).wait()
        return c

    lax.fori_loop(0, MOE_TOK_BLOCK, start, 0, unroll=16)
    lax.fori_loop(0, MOE_TOK_BLOCK, wait, 0, unroll=True)


def _dispatch(hx, pos):
    return pl.pallas_call(
        _dispatch_kernel,
        grid_spec=pltpu.PrefetchScalarGridSpec(
            num_scalar_prefetch=1,
            grid=(N_TOK // MOE_TOK_BLOCK,),
            in_specs=[pl.BlockSpec((MOE_TOK_BLOCK, MOE_ROW_WORDS), lambda i, pos: (i, 0)),
                      pl.BlockSpec(memory_space=pl.ANY)],
            out_specs=pl.BlockSpec(memory_space=pl.ANY),
            scratch_shapes=[pltpu.SemaphoreType.DMA]),
        out_shape=jax.ShapeDtypeStruct((MOE_ROWS, MOE_ROW_WORDS), jnp.uint32),
        input_output_aliases={2: 0},
        compiler_params=pltpu.CompilerParams(dimension_semantics=("arbitrary",)),
        name="moe_dispatch",
    )(pos, hx, jnp.zeros((MOE_ROWS, MOE_ROW_WORDS), jnp.uint32))


def _experts_kernel(tg_ref, need_ref, nu_ref, hx_ref, wg_ref, wu_ref, wd_ref, y_ref):
    t = pl.program_id(0)
    y_ref[...] = jnp.zeros_like(y_ref)
    half = D_MODEL // 2
    for e in range(MOE_EXPERTS_PER_GROUP):
        @pl.when((t < nu_ref[0]) & (need_ref[t * MOE_EXPERTS_PER_GROUP + e] > 0))
        def _():
            h = _unpack_pairs(hx_ref[:, :half]).astype(BF16)
            comb = pltpu.bitcast(hx_ref[:, half:], F32)
            lane = lax.broadcasted_iota(jnp.int32, comb.shape, 1)
            ce = jnp.sum(jnp.where(lane == tg_ref[t] * MOE_EXPERTS_PER_GROUP + e, comb, 0.0), axis=-1,
                         keepdims=True)
            a = _dot(h, wg_ref[e])
            b = _dot(h, wu_ref[e])
            act = (a * jax.nn.sigmoid(a)) * b * ce
            y_ref[...] += _dot(act.astype(BF16), wd_ref[e])


def _experts(hs, tile_group, need, n_used, wg, wu, wd, l):
    wspec = lambda r, c: pl.BlockSpec((None, None, MOE_EXPERTS_PER_GROUP, r, c),
                                      lambda t, tg, need, nu: (l, tg[t], 0, 0, 0))
    return pl.pallas_call(
        _experts_kernel,
        grid_spec=pltpu.PrefetchScalarGridSpec(
            num_scalar_prefetch=3,
            grid=(MOE_ROWS // MOE_TILE,),
            in_specs=[pl.BlockSpec((MOE_TILE, MOE_ROW_WORDS), lambda t, tg, need, nu: (t, 0)),
                      wspec(D_MODEL, EXPERT_FF), wspec(D_MODEL, EXPERT_FF), wspec(EXPERT_FF, D_MODEL)],
            out_specs=pl.BlockSpec((MOE_TILE, D_MODEL), lambda t, tg, need, nu: (t, 0))),
        out_shape=jax.ShapeDtypeStruct((MOE_ROWS, D_MODEL), F32),
        compiler_params=pltpu.CompilerParams(
            dimension_semantics=("arbitrary",), vmem_limit_bytes=VMEM_LIMIT),
        name="moe_experts",
    )(tile_group, need, n_used, hs, wg, wu, wd)


def _combine_kernel(pos_ref, y_ref, x_ref, mod_ref, fg_ref, *rest, final):
    buf, sem = rest[-2:]
    i = pl.program_id(0)
    base = i * MOE_TOK_BLOCK

    def row_copy(r):
        return pltpu.make_async_copy(y_ref.at[pl.ds(pos_ref[base + r], 1), :], buf.at[pl.ds(r, 1), :], sem)

    def start(r, c):
        row_copy(r).start()
        return c

    def wait(r, c):
        row_copy(r).wait()
        return c

    lax.fori_loop(0, MOE_TOK_BLOCK, start, 0, unroll=16)
    lax.fori_loop(0, MOE_TOK_BLOCK, wait, 0, unroll=True)
    x2 = x_ref[...] + mod_ref[5:6, :] * buf[...]
    if not final:
        rest[0][...] = x2
    else:
        y = _rms(x2, fg_ref[...])
        is_ctx = i < N_CTX_TOK // MOE_TOK_BLOCK

        @pl.when(is_ctx)
        def _():
            rest[0][...] = y

        @pl.when(jnp.logical_not(is_ctx))
        def _():
            rest[1][...] = y


def _combine(ys, pos, x1, mod, fg, l, final):
    tm = MOE_TOK_BLOCK
    nctx = N_CTX_TOK // tm
    tok = pl.BlockSpec((tm, D_MODEL), lambda i, pos: (i, 0))
    out = jax.ShapeDtypeStruct((N_TOK, D_MODEL), F32)
    if final:
        out = (jax.ShapeDtypeStruct((N_CTX_TOK, D_MODEL), F32), jax.ShapeDtypeStruct((N_LAT_TOK, D_MODEL), F32))
        out_specs = (pl.BlockSpec((tm, D_MODEL), lambda i, pos: (jnp.minimum(i, nctx - 1), 0)),
                     pl.BlockSpec((tm, D_MODEL), lambda i, pos: (jnp.maximum(i - nctx, 0), 0)))
    else:
        out_specs = tok
    return pl.pallas_call(
        functools.partial(_combine_kernel, final=final),
        grid_spec=pltpu.PrefetchScalarGridSpec(
            num_scalar_prefetch=1,
            grid=(N_TOK // tm,),
            in_specs=[pl.BlockSpec(memory_space=pl.ANY), tok,
                      pl.BlockSpec((None, None, 6, D_MODEL), lambda i, pos: (l, _cond_index(i, tm), 0, 0)),
                      pl.BlockSpec((1, D_MODEL), lambda i, pos: (0, 0))],
            out_specs=out_specs,
            scratch_shapes=[pltpu.VMEM((tm, D_MODEL), F32), pltpu.SemaphoreType.DMA]),
        out_shape=out,
        compiler_params=pltpu.CompilerParams(
            dimension_semantics=("arbitrary",), vmem_limit_bytes=VMEM_LIMIT),
        name="moe_combine",
    )(pos, ys, x1, mod, fg)


def _pack_w_in(w):
    qkv = w[..., :COL_G_V + GQA_KV_WIDTH]
    u = w[..., COL_G_V + GQA_KV_WIDTH:COL_G_V + GQA_KV_WIDTH + S5_CH]
    gates = w[..., COL_G_V + GQA_KV_WIDTH + S5_CH:]
    pad = jnp.zeros(w.shape[:-1] + (COL_U - (COL_G_V + GQA_KV_WIDTH),), w.dtype)
    return jnp.concatenate([qkv, pad, u, gates], axis=-1).astype(BF16)


def kernel(x_prompt, x_sample, cache_na_kv, cache_gqa_kv, state_ssm, c, c_ctx, norm_g, w_ada, b_ada, w_in, na_rpb, gqa_sink, s5_a_re, s5_a_im, s5_log_dt, s5_b_re, s5_b_im, s5_c_re, s5_c_im, s5_d, s5_w_glu, w_branch_a, w_branch_b, w_branch_c, w_out, moe_w_group, moe_b_group, moe_w_expert, moe_b_expert, moe_w_gate, moe_w_up, moe_w_down, final_g):
    cond = jnp.zeros((N_COND, D_MODEL), F32).at[0].set(c_ctx.astype(F32)).at[1:1 + DEC_BATCH].set(c.astype(F32))
    mod = _ada_mod(cond, w_ada.astype(F32), b_ada.astype(F32))

    x = (x_prompt.astype(F32).reshape(N_CTX_TOK, D_MODEL), x_sample.astype(F32).reshape(N_LAT_TOK, D_MODEL))
    cache_na = cache_na_kv.reshape(DEC_BATCH, DEPTH, 2, PAST_LEN, NA_WIDTH)
    cache_gqa = cache_gqa_kv.reshape(DEC_BATCH, DEPTH, 2, PAST_LEN, GQA_KV_WIDTH)
    cos_d, sin_d = _rope_tables()
    cos_q, sin_q = jnp.tile(cos_d, (1, GQA_Q_HEADS)), jnp.tile(sin_d, (1, GQA_Q_HEADS))
    cos_k, sin_k = jnp.tile(cos_d, (1, GQA_KV_HEADS)), jnp.tile(sin_d, (1, GQA_KV_HEADS))
    gn = S5_GROUPS * S5_STATE
    fg = final_g.astype(F32).reshape(1, D_MODEL)

    w_in_packed = _pack_w_in(w_in)
    bf16_weights = [w.astype(BF16) for w in (s5_w_glu, w_branch_a, w_branch_b, w_branch_c, w_out)]
    grouped = lambda w: w.astype(BF16).reshape((DEPTH, MOE_GROUPS, MOE_EXPERTS_PER_GROUP) + w.shape[2:])
    moe_weights = [grouped(w) for w in (moe_w_gate, moe_w_up, moe_w_down)]
    ops = jax.vmap(_s5_operators)(s5_a_re, s5_a_im, s5_log_dt, s5_b_re, s5_b_im, s5_c_re, s5_c_im)

    na_list, gqa_list, ssm_list = [], [], []
    y = None
    for l in range(DEPTH):
        z, u, new_na, new_gqa = _inproj(x, norm_g[l, 0].astype(F32).reshape(1, D_MODEL), mod, w_in_packed, l)
        na_list.append(new_na.reshape(BATCH, 2, SEQ, NA_HEADS, HEAD_DIM))
        gqa_list.append(new_gqa.reshape(BATCH, 2, SEQ, GQA_KV_HEADS, HEAD_DIM))

        sink = gqa_sink[l].astype(F32)
        oa_c, ob_c = _ctx_attn(z, sink)
        oa_l = _na_lat(z, cache_na, _na_bias_tables(na_rpb[l]), l)
        ob_l = _swa_lat(z, cache_gqa, sink, cos_q, sin_q, cos_k, sin_k, l)

        h0 = state_ssm[:, l].astype(F32).reshape(DEC_BATCH, 2, 2, gn).transpose(1, 2, 0, 3)
        yc, fin = _s5_scan(u, ops, h0, l)
        ssm_list.append(fin.transpose(2, 0, 1, 3).reshape(BATCH, 2, 2, S5_GROUPS, S5_STATE).astype(x_prompt.dtype))

        wr = jnp.zeros((D_MODEL, 128), F32)
        wr = wr.at[:, :MOE_EXPERTS].set(moe_w_expert[l].astype(F32))
        wr = wr.at[:, MOE_EXPERTS:MOE_EXPERTS + MOE_GROUPS].set(moe_w_group[l].astype(F32))
        br = jnp.zeros((1, 128), F32)
        br = br.at[0, :MOE_EXPERTS].set(moe_b_expert[l].astype(F32))
        br = br.at[0, MOE_EXPERTS:MOE_EXPERTS + MOE_GROUPS].set(moe_b_group[l].astype(F32))
        wr_hi, wr_lo = _split_bf16(wr)
        x1, hx = _merge(
            x, oa_c, oa_l, ob_c, ob_l, u, yc, z, mod, s5_d[l].astype(F32).reshape(1, S5_CH), *bf16_weights,
            norm_g[l, 1].astype(F32).reshape(1, D_MODEL), wr_hi, wr_lo, br, l)
        cls = lax.bitcast_convert_type(hx[:, D_MODEL // 2 + MOE_EXPERTS], F32).astype(jnp.int32)
        pos, tile_group, need, n_used = _moe_plan(cls)
        ys = _experts(_dispatch(hx, pos), tile_group, need, n_used, *moe_weights, l)
        if l < DEPTH - 1:
            x = _combine(ys, pos, x1, mod, fg, l, False)
        else:
            y_ctx, y_lat = _combine(ys, pos, x1, mod, fg, l, True)

    return (y_ctx.reshape(BATCH, SEQ, D_MODEL), y_lat.reshape(DEC_BATCH, DEC_SEQ, D_MODEL),
            jnp.stack(na_list, axis=1), jnp.stack(gqa_list, axis=1), jnp.stack(ssm_list, axis=1))
```

```python
import functools
import math

import numpy as np
import jax
import jax.numpy as jnp
from jax import lax
from jax.experimental import pallas as pl
from jax.experimental.pallas import tpu as pltpu

F32 = jnp.float32
BF16 = jnp.bfloat16

D_MODEL = 1024
BATCH = 32
SEQ = 256
DEPTH = 2
DEC_BATCH = 4
DEC_SEQ = 2048
PAST_LEN = 512
GRID_W = 64
GRID_ROWS = DEC_SEQ // GRID_W
HEAD_DIM = 64
NA_HEADS = 8
NA_WIN_H = 8
NA_WIN_W = 16
GQA_Q_HEADS = 8
GQA_KV_HEADS = 2
GQA_GROUP = GQA_Q_HEADS // GQA_KV_HEADS
SWA_WINDOW = 128
SWA_BLOCK = 128
ROPE_THETA = 10000.0
S5_CH = 512
S5_GROUP_CH = 16
S5_GROUPS = S5_CH // S5_GROUP_CH
S5_STATE = 64
N_BRANCH = 3
NA_WIDTH = NA_HEADS * HEAD_DIM
GQA_Q_WIDTH = GQA_Q_HEADS * HEAD_DIM
GQA_KV_WIDTH = GQA_KV_HEADS * HEAD_DIM
MOE_GROUPS = 4
MOE_EXPERTS_PER_GROUP = 4
MOE_EXPERTS = MOE_GROUPS * MOE_EXPERTS_PER_GROUP
EXPERT_FF = 512
EPS = 1e-6
NEG_INF = -1e30

N_CTX_TOK = BATCH * SEQ
N_LAT_TOK = DEC_BATCH * DEC_SEQ
N_TOK = N_CTX_TOK + N_LAT_TOK
N_COND = 8

COL_NA_Q = 0
COL_NA_K = 512
COL_NA_V = 1024
COL_G_Q = 1536
COL_G_K = 2048
COL_G_V = 2176
COL_U = 2560
COL_GATES = 3072
Z_COLS = 6144

S5_T = 16
S5_PAIRS = S5_GROUPS // 2
S5_OCT_PAIRS = 4
S5_BLOCK_TOK = 2048
S5_ROWS = S5_BLOCK_TOK // S5_T
S5_NPOW = 7

MOE_ROW_WORDS = D_MODEL // 2 + 128
MOE_TILE = 512
MOE_ROWS = N_TOK + MOE_GROUPS * MOE_TILE
MOE_TOK_BLOCK = 512
MOE_PAIR_MEMBERS = ((0, 1), (0, 2), (0, 3), (1, 3), (1, 2), (2, 3))
MOE_PAIRS = len(MOE_PAIR_MEMBERS)

CTX_HEADS_PER_DOT = 4
NA_HEADS_PER_DOT = 2
NA_QROWS = 4
NA_KROWS = 12

VMEM_LIMIT = 56 * 1024 * 1024


def _dot(a, b):
    return jnp.dot(a, b, preferred_element_type=F32)


def _dot_nt(a, b):
    return lax.dot_general(a, b, (((1,), (1,)), ((), ())), preferred_element_type=F32)


def _split_bf16(x):
    hi = x.astype(BF16)
    lo = (x - hi.astype(F32)).astype(BF16)
    return hi, lo


def _pack_pairs(x):
    w = x.shape[1] // 2
    xb = x.astype(BF16).astype(F32)
    return (pltpu.bitcast(xb[:, :w], jnp.uint32) & jnp.uint32(0xFFFF0000)) | (
        pltpu.bitcast(xb[:, w:], jnp.uint32) >> 16)


def _unpack_pairs(word):
    return jnp.concatenate([pltpu.bitcast(word & jnp.uint32(0xFFFF0000), F32),
                            pltpu.bitcast(word << 16, F32)], axis=1)


def _rms(x, g):
    return x * lax.rsqrt(jnp.mean(x * x, axis=-1, keepdims=True) + EPS) * g


def _cond_index(i, tm):
    nctx = N_CTX_TOK // tm
    return jnp.where(i < nctx, 0, 1 + ((i - nctx) * tm) // DEC_SEQ)


def _ada_kernel(c_ref, w_ref, b_ref, o_ref):
    c = c_ref[...]
    s = c * jax.nn.sigmoid(c)
    s_hi, s_lo = _split_bf16(s)
    w_hi, w_lo = _split_bf16(w_ref[...])
    o_ref[...] = _dot(s_hi, w_hi) + _dot(s_lo, w_hi) + _dot(s_hi, w_lo) + b_ref[...]


def _ada_mod(cond, w_ada, b_ada):
    tn = 1536
    n = 6 * D_MODEL
    out = pl.pallas_call(
        _ada_kernel,
        out_shape=jax.ShapeDtypeStruct((DEPTH, N_COND, n), F32),
        grid=(DEPTH, n // tn),
        in_specs=[
            pl.BlockSpec((N_COND, D_MODEL), lambda l, j: (0, 0)),
            pl.BlockSpec((None, D_MODEL, tn), lambda l, j: (l, 0, j)),
            pl.BlockSpec((None, 1, tn), lambda l, j: (l, 0, j)),
        ],
        out_specs=pl.BlockSpec((None, N_COND, tn), lambda l, j: (l, 0, j)),
        compiler_params=pltpu.CompilerParams(
            dimension_semantics=("parallel", "parallel"), vmem_limit_bytes=VMEM_LIMIT),
        name="ada_mod",
    )(cond, w_ada, b_ada.reshape(DEPTH, 1, n))
    return out.reshape(DEPTH, N_COND, 6, D_MODEL)


INPROJ_TM = 1024
INPROJ_TN = 1536


def _inproj_kernel(xc_ref, xl_ref, g_ref, mod_ref, w_ref, z_ref, u_ref, cna_ref, cgq_ref, h_sc):
    i, j = pl.program_id(0), pl.program_id(1)
    is_ctx = i < N_CTX_TOK // INPROJ_TM

    def norm_mod(x_ref):
        h = _rms(x_ref[...], g_ref[...]) * (1.0 + mod_ref[1:2, :]) + mod_ref[0:1, :]
        h_sc[...] = h.astype(BF16)

    @pl.when((j == 0) & is_ctx)
    def _():
        norm_mod(xc_ref)

    @pl.when((j == 0) & jnp.logical_not(is_ctx))
    def _():
        norm_mod(xl_ref)

    acc = _dot(h_sc[...], w_ref[...])
    z_ref[...] = acc.astype(BF16)

    def rows(b):
        return slice(b * SEQ, (b + 1) * SEQ)

    def cols(c, width):
        return slice(c % INPROJ_TN, c % INPROJ_TN + width)

    @pl.when(is_ctx & (j == COL_NA_K // INPROJ_TN))
    def _():
        for b in range(INPROJ_TM // SEQ):
            cna_ref[b, 0] = acc[rows(b), cols(COL_NA_K, NA_WIDTH)]
            cna_ref[b, 1] = acc[rows(b), cols(COL_NA_V, NA_WIDTH)]

    @pl.when(is_ctx & (j == COL_G_K // INPROJ_TN))
    def _():
        for b in range(INPROJ_TM // SEQ):
            cgq_ref[b, 0] = acc[rows(b), cols(COL_G_K, GQA_KV_WIDTH)]
            cgq_ref[b, 1] = acc[rows(b), cols(COL_G_V, GQA_KV_WIDTH)]

    @pl.when(j == COL_U // INPROJ_TN)
    def _():
        u_ref[...] = acc[:, cols(COL_U, S5_CH)]


def _token_arrays(x):
    return (x, x, N_CTX_TOK) if not isinstance(x, tuple) else (x[0], x[1], 0)


def _inproj(x, g, mod, w, l):
    tm, tn = INPROJ_TM, INPROJ_TN
    nb = tm // SEQ
    last_ctx = N_CTX_TOK // tm - 1
    x_ctx, x_lat, lat0 = _token_arrays(x)
    return pl.pallas_call(
        _inproj_kernel,
        out_shape=(jax.ShapeDtypeStruct((N_TOK, Z_COLS), BF16),
                   jax.ShapeDtypeStruct((N_TOK, S5_CH), F32),
                   jax.ShapeDtypeStruct((BATCH, 2, SEQ, NA_WIDTH), F32),
                   jax.ShapeDtypeStruct((BATCH, 2, SEQ, GQA_KV_WIDTH), F32)),
        grid=(N_TOK // tm, Z_COLS // tn),
        in_specs=[
            pl.BlockSpec((tm, D_MODEL), lambda i, j: (jnp.minimum(i, last_ctx), 0)),
            pl.BlockSpec((tm, D_MODEL), lambda i, j: (jnp.maximum(i - last_ctx - 1, 0) + lat0 // tm, 0)),
            pl.BlockSpec((1, D_MODEL), lambda i, j: (0, 0)),
            pl.BlockSpec((None, None, 6, D_MODEL), lambda i, j: (l, _cond_index(i, tm), 0, 0)),
            pl.BlockSpec((None, D_MODEL, tn), lambda i, j: (l, 0, j)),
        ],
        out_specs=(pl.BlockSpec((tm, tn), lambda i, j: (i, j)),
                   pl.BlockSpec((tm, S5_CH), lambda i, j: (i, 0)),
                   pl.BlockSpec((nb, 2, SEQ, NA_WIDTH), lambda i, j: (jnp.minimum(i, last_ctx), 0, 0, 0)),
                   pl.BlockSpec((nb, 2, SEQ, GQA_KV_WIDTH), lambda i, j: (jnp.minimum(i, last_ctx), 0, 0, 0))),
        scratch_shapes=[pltpu.VMEM((tm, D_MODEL), BF16)],
        compiler_params=pltpu.CompilerParams(
            dimension_semantics=("arbitrary", "arbitrary"), vmem_limit_bytes=VMEM_LIMIT),
        name="inproj",
    )(x_ctx, x_lat, g, mod, w)


def _block_diag_heads(x, nrep):
    t = x.shape[0]
    rows = lax.broadcasted_iota(jnp.int32, (nrep * t, nrep * HEAD_DIM), 0) // t
    lanes = lax.broadcasted_iota(jnp.int32, (nrep * t, nrep * HEAD_DIM), 1) // HEAD_DIM
    return jnp.where(rows == lanes, jnp.concatenate([x] * nrep, axis=0), jnp.zeros((), x.dtype))


def _per_head(cols, shape):
    head = lax.broadcasted_iota(jnp.int32, shape, 1) // HEAD_DIM
    out = cols[-1]
    for h in range(len(cols) - 2, -1, -1):
        out = jnp.where(head == h, cols[h], out)
    return out


def _attend_heads(q, k, v, sinks):
    nh = q.shape[1] // HEAD_DIM
    t = k.shape[0]
    s = _dot_nt(q * HEAD_DIM ** -0.5, _block_diag_heads(k, nh))
    ps, ds = [], []
    for h in range(nh):
        sh = s[:, h * t:(h + 1) * t]
        m = jnp.max(sh, axis=-1, keepdims=True)
        if sinks is not None:
            m = jnp.maximum(m, sinks[h])
        p = jnp.exp(sh - m)
        d = jnp.sum(p, axis=-1, keepdims=True)
        if sinks is not None:
            d = d + jnp.exp(sinks[h] - m)
        ps.append(p.astype(BF16))
        ds.append(d)
    o = _dot(jnp.concatenate(ps, axis=1), _block_diag_heads(v, nh))
    return o / _per_head(ds, o.shape)


def _ctx_attn_kernel(sink_ref, na_ref, gq_ref, gkv_ref, oa_ref, ob_ref):
    hb = CTX_HEADS_PER_DOT
    w = hb * HEAD_DIM
    na = na_ref[...]
    gq = gq_ref[...]
    gkv = gkv_ref[...]
    for g in range(NA_HEADS // hb):
        o = _attend_heads(na[:, COL_NA_Q + g * w:COL_NA_Q + (g + 1) * w],
                          na[:, COL_NA_K + g * w:COL_NA_K + (g + 1) * w],
                          na[:, COL_NA_V + g * w:COL_NA_V + (g + 1) * w], None)
        oa_ref[:, g * w:(g + 1) * w] = o.astype(BF16)
    for g in range(GQA_Q_HEADS // hb):
        kv_heads = [(g * hb + h) // GQA_GROUP for h in range(hb)]
        k = jnp.concatenate([gkv[:, kh * HEAD_DIM:(kh + 1) * HEAD_DIM] for kh in kv_heads], axis=1)
        v = jnp.concatenate([gkv[:, GQA_KV_WIDTH + kh * HEAD_DIM:GQA_KV_WIDTH + (kh + 1) * HEAD_DIM]
                             for kh in kv_heads], axis=1)
        o = _attend_heads(gq[:, g * w:(g + 1) * w], k, v, [sink_ref[g * hb + h] for h in range(hb)])
        ob_ref[:, g * w:(g + 1) * w] = o.astype(BF16)


def _ctx_attn(z, sink):
    out = jax.ShapeDtypeStruct((N_CTX_TOK, NA_WIDTH), BF16)
    return pl.pallas_call(
        _ctx_attn_kernel,
        out_shape=(out, out),
        grid=(BATCH,),
        in_specs=[
            pl.BlockSpec(memory_space=pltpu.SMEM),
            pl.BlockSpec((SEQ, 3 * NA_WIDTH), lambda b: (b, 0)),
            pl.BlockSpec((SEQ, GQA_Q_WIDTH), lambda b: (b, COL_G_Q // GQA_Q_WIDTH)),
            pl.BlockSpec((SEQ, 2 * GQA_KV_WIDTH), lambda b: (b, COL_G_K // (2 * GQA_KV_WIDTH))),
        ],
        out_specs=(pl.BlockSpec((SEQ, NA_WIDTH), lambda b: (b, 0)),
                   pl.BlockSpec((SEQ, GQA_Q_WIDTH), lambda b: (b, 0))),
        compiler_params=pltpu.CompilerParams(
            dimension_semantics=("parallel",), vmem_limit_bytes=VMEM_LIMIT),
        name="ctx_attn",
    )(sink, z, z, z)


def _na_bias_tables(rpb):
    n_dc = 2 * NA_WIN_W - 1
    qc = np.arange(GRID_W)
    kc = np.arange(GRID_W)
    cstart = np.clip(qc - NA_WIN_W // 2, 0, GRID_W - NA_WIN_W)
    col_ok = (kc[None, :] >= cstart[:, None]) & (kc[None, :] < cstart[:, None] + NA_WIN_W)
    dc = np.clip(kc[None, :] - qc[:, None] + (NA_WIN_W - 1), 0, n_dc - 1)
    onehot = ((np.arange(n_dc)[:, None, None] == dc[None]) & col_ok[None]).astype(np.float32)
    band = jnp.einsum("hrd,dqk->hrqk", rpb.astype(F32), onehot, precision=lax.Precision.HIGHEST)
    band = jnp.where(col_ok, band, NEG_INF)
    band = jnp.pad(band, ((0, 0), (1, 1), (0, 0), (0, 0)), constant_values=NEG_INF)
    return jnp.concatenate([band[:, :-1], band[:, 1:]], axis=-1)


def _na_lat_kernel(q_ref, k_ref, v_ref, ck_ref, cv_ref, bias_ref, o_ref):
    i = pl.program_id(1)
    scale = HEAD_DIM ** -0.5
    nk = NA_KROWS * GRID_W
    kh = min(NA_WIN_H, GRID_ROWS)
    ks = jnp.clip(i * NA_QROWS - NA_WIN_H // 2, 0, GRID_ROWS - NA_KROWS)
    start = pl.multiple_of(ks * GRID_W, GRID_W)
    q = q_ref[...].astype(BF16)
    kw = k_ref[pl.ds(start, nk), :].astype(BF16)
    vw = v_ref[pl.ds(start, nk), :].astype(BF16)
    ck = ck_ref[...].astype(BF16)
    cv = cv_ref[...].astype(BF16)
    left = lax.broadcasted_iota(jnp.int32, (GRID_W, 2 * GRID_W), 1) < GRID_W
    pieces = []
    for qr in range(NA_QROWS):
        r = i * NA_QROWS + qr
        st = jnp.clip(r - kh // 2, 0, GRID_ROWS - kh)
        row = []
        for m in range(NA_KROWS // 2):
            k0 = ks + 2 * m
            plane = jnp.clip(k0 - r + NA_WIN_H, 0, 2 * NA_WIN_H - 1)
            ok0 = (k0 >= st) & (k0 < st + kh)
            ok1 = (k0 + 1 >= st) & (k0 + 1 < st + kh)
            row.append((plane, jnp.where(left, ok0.astype(jnp.int32), ok1.astype(jnp.int32)) > 0))
        pieces.append(row)
    hb = NA_HEADS_PER_DOT
    w = hb * HEAD_DIM
    for g in range(NA_HEADS // hb):
        lanes = slice(g * w, (g + 1) * w)
        qg = q[:, lanes] * scale
        s_loc = _dot_nt(qg, _block_diag_heads(kw[:, lanes], hb))
        s_ctx = _dot_nt(qg, _block_diag_heads(ck[:, lanes], hb))
        p_locs, p_ctxs, ds = [], [], []
        for hh in range(hb):
            bias = jnp.concatenate(
                [jnp.concatenate([jnp.where(ok, bias_ref[g * hb + hh, plane], NEG_INF) for plane, ok in row], axis=1)
                 for row in pieces], axis=0)
            sl = s_loc[:, hh * nk:(hh + 1) * nk] + bias
            sc = s_ctx[:, hh * PAST_LEN:(hh + 1) * PAST_LEN]
            m = jnp.maximum(jnp.max(sl, axis=-1, keepdims=True), jnp.max(sc, axis=-1, keepdims=True))
            p_loc = jnp.exp(sl - m)
            p_ctx = jnp.exp(sc - m)
            ds.append(jnp.sum(p_loc, axis=-1, keepdims=True) + jnp.sum(p_ctx, axis=-1, keepdims=True))
            p_locs.append(p_loc.astype(BF16))
            p_ctxs.append(p_ctx.astype(BF16))
        o = (_dot(jnp.concatenate(p_locs, axis=1), _block_diag_heads(vw[:, lanes], hb))
             + _dot(jnp.concatenate(p_ctxs, axis=1), _block_diag_heads(cv[:, lanes], hb)))
        o_ref[:, lanes] = (o / _per_head(ds, o.shape)).astype(BF16)


def _na_lat(z, cache_na, bias, l):
    tq = NA_QROWS * GRID_W
    nsteps = GRID_ROWS // NA_QROWS
    lat_blk = N_CTX_TOK // DEC_SEQ
    return pl.pallas_call(
        _na_lat_kernel,
        out_shape=jax.ShapeDtypeStruct((N_LAT_TOK, NA_WIDTH), BF16),
        grid=(DEC_BATCH, nsteps),
        in_specs=[
            pl.BlockSpec((tq, NA_WIDTH), lambda b, i: (N_CTX_TOK // tq + b * nsteps + i, COL_NA_Q // NA_WIDTH)),
            pl.BlockSpec((DEC_SEQ, NA_WIDTH), lambda b, i: (lat_blk + b, COL_NA_K // NA_WIDTH)),
            pl.BlockSpec((DEC_SEQ, NA_WIDTH), lambda b, i: (lat_blk + b, COL_NA_V // NA_WIDTH)),
            pl.BlockSpec((None, None, None, PAST_LEN, NA_WIDTH), lambda b, i: (b, l, 0, 0, 0)),
            pl.BlockSpec((None, None, None, PAST_LEN, NA_WIDTH), lambda b, i: (b, l, 1, 0, 0)),
            pl.BlockSpec((NA_HEADS, 2 * NA_WIN_H, GRID_W, 2 * GRID_W), lambda b, i: (0, 0, 0, 0)),
        ],
        out_specs=pl.BlockSpec((tq, NA_WIDTH), lambda b, i: (b * nsteps + i, 0)),
        compiler_params=pltpu.CompilerParams(
            dimension_semantics=("parallel", "arbitrary"), vmem_limit_bytes=VMEM_LIMIT),
        name="na_lat",
    )(z, z, z, cache_na, cache_na, bias)


def _rope_tables():
    nf = HEAD_DIM // 4
    t = jnp.arange(DEC_SEQ)
    pos = jnp.stack([t // GRID_W, t % GRID_W], axis=-1).astype(F32)
    inv = ROPE_THETA ** (-jnp.arange(nf, dtype=F32) / nf)
    ang = pos[:, :, None] * inv
    cos = jnp.cos(ang)
    sin = jnp.sin(ang)
    cos_d = jnp.stack([cos, cos], axis=2).reshape(DEC_SEQ, HEAD_DIM)
    sin_d = jnp.stack([-sin, sin], axis=2).reshape(DEC_SEQ, HEAD_DIM)
    return cos_d, sin_d


def _rope(x, cos, sin_signed):
    n = x.shape[-1]
    nf = HEAD_DIM // 4
    lane = lax.broadcasted_iota(jnp.int32, x.shape, 1)
    first_half = (lane // nf) % 2 == 0
    partner = jnp.where(first_half, pltpu.roll(x, n - nf, 1), pltpu.roll(x, nf, 1))
    return x * cos + partner * sin_signed


def _swa_lat_kernel(sink_ref, q_ref, kv_ref, cq_ref, sq_ref, ckt_ref, skt_ref, ck_ref, cv_ref, o_ref, k_sc, v_sc):
    n = pl.program_id(1)
    scale = HEAD_DIM ** -0.5
    nwin = 3 * SWA_BLOCK

    @pl.when(n == 0)
    def _():
        kv = kv_ref[...]
        k_sc[...] = _rope(kv[:, :GQA_KV_WIDTH].astype(F32), ckt_ref[...], skt_ref[...]).astype(BF16)
        v_sc[...] = kv[:, GQA_KV_WIDTH:].astype(BF16)

    q = (_rope(q_ref[...].astype(F32), cq_ref[...], sq_ref[...]) * scale).astype(BF16)
    start = pl.multiple_of(jnp.clip((n - 1) * SWA_BLOCK, 0, DEC_SEQ - nwin), SWA_BLOCK)
    kw = k_sc[pl.ds(start, nwin), :]
    vw = v_sc[pl.ds(start, nwin), :]
    ck = ck_ref[...].astype(BF16)
    cv = cv_ref[...].astype(BF16)
    rows = GQA_GROUP * SWA_BLOCK
    row = lax.broadcasted_iota(jnp.int32, (rows, nwin), 0)
    col = lax.broadcasted_iota(jnp.int32, (rows, nwin), 1)
    qpos = n * SWA_BLOCK + row % SWA_BLOCK
    kpos = start + col
    ok = jnp.abs(qpos - kpos) <= SWA_WINDOW
    grp = lax.broadcasted_iota(jnp.int32, (rows, 1), 0) // SWA_BLOCK
    for kh in range(GQA_KV_HEADS):
        sl = slice(kh * HEAD_DIM, (kh + 1) * HEAD_DIM)
        q4 = jnp.concatenate(
            [q[:, (kh * GQA_GROUP + g) * HEAD_DIM:(kh * GQA_GROUP + g + 1) * HEAD_DIM] for g in range(GQA_GROUP)],
            axis=0)
        sink = jnp.zeros((rows, 1), F32)
        for g in range(GQA_GROUP):
            sink = jnp.where(grp == g, sink_ref[kh * GQA_GROUP + g], sink)
        s_loc = jnp.where(ok, _dot_nt(q4, kw[:, sl]), NEG_INF)
        s_ctx = _dot_nt(q4, ck[:, sl])
        m = jnp.maximum(jnp.maximum(jnp.max(s_loc, axis=-1, keepdims=True),
                                    jnp.max(s_ctx, axis=-1, keepdims=True)), sink)
        p_loc = jnp.exp(s_loc - m)
        p_ctx = jnp.exp(s_ctx - m)
        d = (jnp.sum(p_loc, axis=-1, keepdims=True) + jnp.sum(p_ctx, axis=-1, keepdims=True)
             + jnp.exp(sink - m))
        o4 = (_dot(p_loc.astype(BF16), vw[:, sl]) + _dot(p_ctx.astype(BF16), cv[:, sl])) / d
        for g in range(GQA_GROUP):
            h = kh * GQA_GROUP + g
            o_ref[:, h * HEAD_DIM:(h + 1) * HEAD_DIM] = o4[g * SWA_BLOCK:(g + 1) * SWA_BLOCK].astype(BF16)


def _swa_lat(z, cache_gqa, sink, cos_q, sin_q, cos_k, sin_k, l):
    nb = DEC_SEQ // SWA_BLOCK
    lat_blk = N_CTX_TOK // DEC_SEQ
    return pl.pallas_call(
        _swa_lat_kernel,
        out_shape=jax.ShapeDtypeStruct((N_LAT_TOK, GQA_Q_WIDTH), BF16),
        grid=(DEC_BATCH, nb),
        in_specs=[
            pl.BlockSpec(memory_space=pltpu.SMEM),
            pl.BlockSpec((SWA_BLOCK, GQA_Q_WIDTH),
                         lambda b, n: (N_CTX_TOK // SWA_BLOCK + b * nb + n, COL_G_Q // GQA_Q_WIDTH)),
            pl.BlockSpec((DEC_SEQ, 2 * GQA_KV_WIDTH), lambda b, n: (lat_blk + b, COL_G_K // (2 * GQA_KV_WIDTH))),
            pl.BlockSpec((SWA_BLOCK, GQA_Q_WIDTH), lambda b, n: (n, 0)),
            pl.BlockSpec((SWA_BLOCK, GQA_Q_WIDTH), lambda b, n: (n, 0)),
            pl.BlockSpec((DEC_SEQ, GQA_KV_WIDTH), lambda b, n: (0, 0)),
            pl.BlockSpec((DEC_SEQ, GQA_KV_WIDTH), lambda b, n: (0, 0)),
            pl.BlockSpec((None, None, None, PAST_LEN, GQA_KV_WIDTH), lambda b, n: (b, l, 0, 0, 0)),
            pl.BlockSpec((None, None, None, PAST_LEN, GQA_KV_WIDTH), lambda b, n: (b, l, 1, 0, 0)),
        ],
        out_specs=pl.BlockSpec((SWA_BLOCK, GQA_Q_WIDTH), lambda b, n: (b * nb + n, 0)),
        scratch_shapes=[pltpu.VMEM((DEC_SEQ, GQA_KV_WIDTH), BF16), pltpu.VMEM((DEC_SEQ, GQA_KV_WIDTH), BF16)],
        compiler_params=pltpu.CompilerParams(
            dimension_semantics=("parallel", "arbitrary"), vmem_limit_bytes=VMEM_LIMIT),
        name="swa_lat",
    )(sink, z, z, cos_q, sin_q, cos_k, sin_k, cache_gqa, cache_gqa)


S5_OP_KEYS = ("kt", "bt_re", "bt_im", "ct_re", "ct_im", "pin_re", "pin_im", "po_re", "po_im", "apr", "api")


def _s5_operators(a_re, a_im, log_dt, b_re, b_im, c_re, c_im):
    T, G, N, C = S5_T, S5_GROUPS, S5_STATE, S5_GROUP_CH
    tau = jnp.arange(T + 1, dtype=F32)
    out = {k: [] for k in S5_OP_KEYS}
    for d in range(2):
        A = lax.complex(a_re[d].astype(F32), a_im[d].astype(F32))
        dt = jnp.exp(log_dt[d].astype(F32))[:, None]
        a_bar = jnp.exp(A * dt)
        pw = jnp.exp((A * dt)[None] * tau[:, None, None])
        b_bar = ((a_bar - 1.0) / A)[..., None] * lax.complex(b_re[d].astype(F32), b_im[d].astype(F32))
        c_mat = lax.complex(c_re[d].astype(F32), c_im[d].astype(F32))
        kern = jnp.einsum("gon,tgn,gni->gtoi", c_mat, pw[:T], b_bar, precision=lax.Precision.HIGHEST).real
        kern = kern.transpose(0, 3, 1, 2)
        if d == 0:
            p_in = pw[:T][::-1]
            p_out = pw[1:T + 1]
        else:
            p_in = pw[:T]
            p_out = pw[1:T + 1][::-1]
            kern = kern[:, :, ::-1]
        bt, ct, po = b_bar.transpose(0, 2, 1), c_mat.transpose(0, 2, 1), p_out.transpose(1, 2, 0)
        pw2 = jnp.exp((A * dt)[None] * (T * 2.0 ** jnp.arange(S5_NPOW, dtype=F32))[:, None, None])
        vals = (kern.reshape(G, C, T * C), bt.real, bt.imag, ct.real, ct.imag,
                p_in.real.reshape(T, S5_PAIRS, 1, 2 * N), p_in.imag.reshape(T, S5_PAIRS, 1, 2 * N), po.real, po.imag,
                pw2.real.reshape(S5_NPOW, G * N), pw2.imag.reshape(S5_NPOW, G * N))
        for k, v in zip(S5_OP_KEYS, vals):
            out[k].append(v)
    return {k: jnp.stack(v) for k, v in out.items()}


def _s5_expand(kt_ref, btr_ref, bti_ref, ctr_ref, cti_ref, pir_ref, pii_ref, por_ref, poi_ref,
               bs_sc, m_sc, cre_sc, cim_sc):
    T, C, N = S5_T, S5_GROUP_CH, S5_STATE
    pc, half, width = 2 * C, 2 * N, 2 * T * C
    lane = lax.broadcasted_iota(jnp.int32, (pc, width), 1)
    zc, zn, zo = jnp.zeros((C, C), F32), jnp.zeros((C, N), F32), jnp.zeros((N, C), F32)

    def own_half(x, z, g):
        return [x, z] if g == 0 else [z, x]

    def expand_pair(p, d):
        k0, bx_re, bx_im, co_re, co_im = [], [], [], [], []
        for g in range(2):
            gl = 2 * p + g
            ktg = kt_ref[d, gl]
            k0.append(jnp.concatenate(
                [blk for lag in range(T) for blk in own_half(ktg[:, lag * C:(lag + 1) * C], zc, g)], axis=1))
            bx_re.append(jnp.concatenate(own_half(btr_ref[d, gl], zn, g), axis=1))
            bx_im.append(jnp.concatenate(own_half(bti_ref[d, gl], zn, g), axis=1))
            cr, ci = ctr_ref[d, gl], cti_ref[d, gl]
            pr_all, pi_all = por_ref[d, gl], poi_ref[d, gl]
            re_p, im_p = [], []
            for j in range(T):
                pr, pi = pr_all[:, j:j + 1], pi_all[:, j:j + 1]
                re_p += own_half(cr * pr - ci * pi, zo, g)
                im_p += own_half(cr * pi + ci * pr, zo, g)
            co_re.append(jnp.concatenate(re_p, axis=1))
            co_im.append(jnp.concatenate(im_p, axis=1))
        k0 = jnp.concatenate(k0, axis=0)
        bx_re, bx_im = jnp.concatenate(bx_re, axis=0), jnp.concatenate(bx_im, axis=0)
        for s in range(T):
            if d == 0:
                sh = pc * s
                blk = jnp.where(lane >= sh, pltpu.roll(k0, sh, 1), 0.0) if sh else k0
            else:
                sh = pc * (T - 1 - s)
                blk = jnp.where(lane < width - sh, pltpu.roll(k0, width - sh, 1), 0.0) if sh else k0
            m_sc[d, p, s * pc:(s + 1) * pc, :] = blk.astype(BF16)
            pr, pi = pir_ref[d, s, p], pii_ref[d, s, p]
            bs_sc[d, p, s * pc:(s + 1) * pc, :] = jnp.concatenate(
                [pr * bx_re - pi * bx_im, pr * bx_im + pi * bx_re], axis=1).astype(BF16)
        cre_sc[d, p] = jnp.concatenate(co_re, axis=0).astype(BF16)
        cim_sc[d, p] = (-jnp.concatenate(co_im, axis=0)).astype(BF16)

    for d in range(2):
        lax.fori_loop(0, S5_OCT_PAIRS, lambda p, c, d=d: (expand_pair(p, d), c)[1], 0)


def _s5_kernel(u_ref, kt_ref, btr_ref, bti_ref, ctr_ref, cti_ref, pir_ref, pii_ref, por_ref, poi_ref, apr_ref,
               api_ref, h0_ref, y_ref, fin_ref, fin_sc, bs_sc, m_sc, cre_sc, cim_sc):
    t = pl.program_id(1)
    nctx = N_CTX_TOK // S5_BLOCK_TOK

    @pl.when(t == 0)
    def _():
        _s5_expand(kt_ref, btr_ref, bti_ref, ctr_ref, cti_ref, pir_ref, pii_ref, por_ref, poi_ref,
                   bs_sc, m_sc, cre_sc, cim_sc)

    refs = (u_ref, bs_sc, m_sc, cre_sc, cim_sc, apr_ref, api_ref, h0_ref, y_ref, fin_ref, fin_sc)

    @pl.when(t < nctx)
    def _():
        _s5_block(*refs, t, kseq=SEQ // S5_T, has_h0=False)

    @pl.when(t >= nctx)
    def _():
        _s5_block(*refs, t - nctx, kseq=DEC_SEQ // S5_T, has_h0=True)


def _s5_block(u_ref, bs_ref, m_ref, cre_ref, cim_ref, apr_ref, api_ref, h0_ref, y_ref, fin_ref, fin_sc, t, *,
              kseq, has_h0):
    rows, half, pc = S5_ROWS, 2 * S5_STATE, 2 * S5_GROUP_CH
    k = lax.broadcasted_iota(jnp.int32, (rows, half), 0) % kseq
    shifts = [1 << i for i in range(kseq.bit_length() - 1)]
    xs = [u_ref[pl.ds(s, rows, stride=S5_T), :].astype(BF16) for s in range(S5_T)]
    ys = []
    for p in range(S5_OCT_PAIRS):
        lanes = slice(p * half, (p + 1) * half)
        xp = jnp.concatenate([x[:, p * pc:(p + 1) * pc] for x in xs], axis=1)
        acc = None
        for d in range(2):
            inc = _dot(xp, bs_ref[d, p])
            sr, si = inc[:, :half], inc[:, half:]
            if has_h0:
                h0r = h0_ref[d, 0, t, :, lanes]
                h0i = h0_ref[d, 1, t, :, lanes]
                ar, ai = apr_ref[d, 0:1, lanes], api_ref[d, 0:1, lanes]
                first = (k == 0) if d == 0 else (k == kseq - 1)
                sr = sr + jnp.where(first, ar * h0r - ai * h0i, 0.0)
                si = si + jnp.where(first, ar * h0i + ai * h0r, 0.0)
            else:
                h0r = h0i = 0.0
            for i, sh in enumerate(shifts):
                ar, ai = apr_ref[d, i:i + 1, lanes], api_ref[d, i:i + 1, lanes]
                ok = (k >= sh) if d == 0 else (k < kseq - sh)
                amt = sh if d == 0 else rows - sh
                rr = jnp.where(ok, pltpu.roll(sr, amt, 0), 0.0)
                ri = jnp.where(ok, pltpu.roll(si, amt, 0), 0.0)
                sr, si = sr + ar * rr - ai * ri, si + ar * ri + ai * rr
            inner = (k >= 1) if d == 0 else (k < kseq - 1)
            amt = 1 if d == 0 else rows - 1
            hr = jnp.where(inner, pltpu.roll(sr, amt, 0), h0r)
            hi = jnp.where(inner, pltpu.roll(si, amt, 0), h0i)
            yd = (_dot(xp, m_ref[d, p]) + _dot(hr.astype(BF16), cre_ref[d, p])
                  + _dot(hi.astype(BF16), cim_ref[d, p]))
            acc = yd if acc is None else acc + yd
            if not has_h0:
                last = kseq - 1 if d == 0 else 0
                fin_sc[0] = sr
                fin_sc[1] = si
                fin_ref[d, 0, :, lanes] = fin_sc[0, pl.ds(last, rows // kseq, stride=kseq), :]
                fin_ref[d, 1, :, lanes] = fin_sc[1, pl.ds(last, rows // kseq, stride=kseq), :]
        ys.append(acc)
    for j in range(S5_T):
        y_ref[pl.ds(j, rows, stride=S5_T), :] = jnp.concatenate([y[:, j * pc:(j + 1) * pc] for y in ys], axis=1)


def _s5_scan(u, ops, h0, l):
    nblk, nctx = N_TOK // S5_BLOCK_TOK, N_CTX_TOK // S5_BLOCK_TOK
    noct = S5_PAIRS // S5_OCT_PAIRS
    half, gn = 2 * S5_STATE, S5_GROUPS * S5_STATE
    pw = 2 * S5_T * S5_GROUP_CH
    nseq = S5_BLOCK_TOK // SEQ
    ngrp = 2 * S5_OCT_PAIRS
    C, N, T = S5_GROUP_CH, S5_STATE, S5_T
    gspec = lambda r, c: pl.BlockSpec((None, 2, ngrp, r, c), lambda q, t: (l, 0, q, 0, 0))
    lspec = lambda r: pl.BlockSpec((None, 2, r, S5_OCT_PAIRS * half), lambda q, t: (l, 0, 0, q))
    pspec = pl.BlockSpec((None, 2, T, S5_OCT_PAIRS, 1, half), lambda q, t: (l, 0, 0, q, 0, 0))
    return pl.pallas_call(
        _s5_kernel,
        grid=(noct, nblk),
        in_specs=[
            pl.BlockSpec((S5_BLOCK_TOK, 128), lambda q, t: (t, q)),
            gspec(C, T * C), gspec(C, N), gspec(C, N), gspec(N, C), gspec(N, C), pspec, pspec,
            gspec(N, T), gspec(N, T), lspec(S5_NPOW), lspec(S5_NPOW),
            pl.BlockSpec((2, 2, DEC_BATCH, 1, S5_OCT_PAIRS * half), lambda q, t: (0, 0, 0, 0, q)),
        ],
        out_shape=(jax.ShapeDtypeStruct((N_TOK, S5_CH), F32), jax.ShapeDtypeStruct((2, 2, BATCH, gn), F32)),
        out_specs=(pl.BlockSpec((S5_BLOCK_TOK, 128), lambda q, t: (t, q)),
                   pl.BlockSpec((2, 2, nseq, S5_OCT_PAIRS * half),
                                lambda q, t: (0, 0, jnp.minimum(t, nctx - 1), q))),
        scratch_shapes=[pltpu.VMEM((2, S5_ROWS, half), F32),
                        pltpu.VMEM((2, S5_OCT_PAIRS, pw, 2 * half), BF16), pltpu.VMEM((2, S5_OCT_PAIRS, pw, pw), BF16),
                        pltpu.VMEM((2, S5_OCT_PAIRS, half, pw), BF16), pltpu.VMEM((2, S5_OCT_PAIRS, half, pw), BF16)],
        compiler_params=pltpu.CompilerParams(
            dimension_semantics=("parallel", "arbitrary"), vmem_limit_bytes=VMEM_LIMIT),
        name="s5_scan",
    )(u, *[ops[k] for k in S5_OP_KEYS], h0.reshape(2, 2, DEC_BATCH, 1, gn))


def _route(logits):
    lane = lax.broadcasted_iota(jnp.int32, logits.shape, 1)
    big = jnp.int32(1 << 20)
    is_g = (lane >= MOE_EXPERTS) & (lane < MOE_EXPERTS + MOE_GROUPS)
    lg = jnp.where(is_g, logits, NEG_INF)
    gmax = jnp.max(lg, axis=-1, keepdims=True)
    gsel = jnp.min(jnp.where(is_g & (lg == gmax), lane, big), axis=-1, keepdims=True) - MOE_EXPERTS
    p_group = 1.0 / jnp.sum(jnp.where(is_g, jnp.exp(lg - gmax), 0.0), axis=-1, keepdims=True)
    in_grp = (lane < MOE_EXPERTS) & (lane // MOE_EXPERTS_PER_GROUP == gsel)
    le = jnp.where(in_grp, logits, NEG_INF)
    v1 = jnp.max(le, axis=-1, keepdims=True)
    i1 = jnp.min(jnp.where(in_grp & (le == v1), lane, big), axis=-1, keepdims=True)
    rest = in_grp & (lane != i1)
    le2 = jnp.where(rest, logits, NEG_INF)
    v2 = jnp.max(le2, axis=-1, keepdims=True)
    i2 = jnp.min(jnp.where(rest & (le2 == v2), lane, big), axis=-1, keepdims=True)
    e2 = jnp.exp(v2 - v1)
    w1 = 1.0 / (1.0 + e2)
    w2 = e2 / (1.0 + e2)
    comb = jnp.where(lane == i1, w1 * p_group, 0.0) + jnp.where(lane == i2, w2 * p_group, 0.0)
    a = jnp.minimum(i1, i2) - gsel * MOE_EXPERTS_PER_GROUP
    b = jnp.maximum(i1, i2) - gsel * MOE_EXPERTS_PER_GROUP
    pair = jnp.where(a == 0, b - 1, jnp.where(a == 1, jnp.where(b == 3, 3, 4), 5))
    cls = gsel * MOE_PAIRS + pair
    return jnp.where(lane == MOE_EXPERTS, cls.astype(F32), comb)


def _merge_kernel(xc_ref, xl_ref, oac_ref, oal_ref, obc_ref, obl_ref, u_ref, yc_ref, gates_ref, mod_ref, d_ref,
                  wglu_ref, wa_ref, wb_ref, wc_ref, wout_ref, g2_ref, wrh_ref, wrl_ref, br_ref, x1_ref, hx_ref,
                  *, tm):
    is_ctx = pl.program_id(0) < N_CTX_TOK // tm
    x = jnp.where(is_ctx, xc_ref[...], xl_ref[...])
    oa = jnp.where(is_ctx, oac_ref[...], oal_ref[...])
    ob = jnp.where(is_ctx, obc_ref[...], obl_ref[...])
    y = u_ref[...] * d_ref[...] + yc_ref[...]
    y = y * (0.5 * (1.0 + jnp.tanh(math.sqrt(2.0 / math.pi) * (y + 0.044715 * (y * y * y)))))
    oc = y * jax.nn.sigmoid(_dot(y.astype(BF16), wglu_ref[...]))
    gate = jax.nn.sigmoid(gates_ref[...].astype(F32))
    merged = (gate[:, :D_MODEL] * _dot(oa, wa_ref[...])
              + gate[:, D_MODEL:2 * D_MODEL] * _dot(ob, wb_ref[...])
              + gate[:, 2 * D_MODEL:] * _dot(oc.astype(BF16), wc_ref[...]))
    x1 = x + mod_ref[2:3, :] * _dot(merged.astype(BF16), wout_ref[...])
    x1_ref[...] = x1
    h2 = _rms(x1, g2_ref[...]) * (1.0 + mod_ref[4:5, :]) + mod_ref[3:4, :]
    h_hi, h_lo = _split_bf16(h2)
    logits = _dot(h_hi, wrh_ref[...]) + _dot(h_lo, wrh_ref[...]) + _dot(h_hi, wrl_ref[...]) + br_ref[...]
    half = D_MODEL // 2
    hx_ref[:, :half] = _pack_pairs(h2)
    hx_ref[:, half:] = pltpu.bitcast(_route(logits), jnp.uint32)


def _merge(x, oa_c, oa_l, ob_c, ob_l, u, yc, z, mod, d_s5, wglu, wa, wb, wc, wout, g2, wr_hi, wr_lo, br, l):
    tm = 512
    nctx = N_CTX_TOK // tm
    full = lambda r, c: pl.BlockSpec((r, c), lambda i: (0, 0))
    layer = lambda r, c: pl.BlockSpec((None, r, c), lambda i: (l, 0, 0))
    ctx_blk = lambda w: pl.BlockSpec((tm, w), lambda i: (jnp.minimum(i, nctx - 1), 0))
    lat_blk = lambda w, first=0: pl.BlockSpec((tm, w), lambda i: (jnp.maximum(i - nctx, 0) + first // tm, 0))
    x_ctx, x_lat, lat0 = _token_arrays(x)
    return pl.pallas_call(
        functools.partial(_merge_kernel, tm=tm),
        out_shape=(jax.ShapeDtypeStruct((N_TOK, D_MODEL), F32),
                   jax.ShapeDtypeStruct((N_TOK, MOE_ROW_WORDS), jnp.uint32)),
        grid=(N_TOK // tm,),
        in_specs=[
            ctx_blk(D_MODEL), lat_blk(D_MODEL, lat0),
            ctx_blk(NA_WIDTH), lat_blk(NA_WIDTH), ctx_blk(GQA_Q_WIDTH), lat_blk(GQA_Q_WIDTH),
            pl.BlockSpec((tm, S5_CH), lambda i: (i, 0)),
            pl.BlockSpec((tm, S5_CH), lambda i: (i, 0)),
            pl.BlockSpec((tm, N_BRANCH * D_MODEL), lambda i: (i, COL_GATES // (N_BRANCH * D_MODEL))),
            pl.BlockSpec((None, None, 6, D_MODEL), lambda i: (l, _cond_index(i, tm), 0, 0)),
            full(1, S5_CH), layer(S5_CH, S5_CH), layer(NA_WIDTH, D_MODEL), layer(GQA_Q_WIDTH, D_MODEL),
            layer(S5_CH, D_MODEL), layer(D_MODEL, D_MODEL), full(1, D_MODEL),
            full(D_MODEL, 128), full(D_MODEL, 128), full(1, 128),
        ],
        out_specs=(pl.BlockSpec((tm, D_MODEL), lambda i: (i, 0)),
                   pl.BlockSpec((tm, MOE_ROW_WORDS), lambda i: (i, 0))),
        compiler_params=pltpu.CompilerParams(
            dimension_semantics=("parallel",), vmem_limit_bytes=VMEM_LIMIT),
        name="merge",
    )(x_ctx, x_lat, oa_c, oa_l, ob_c, ob_l, u, yc, z, mod, d_s5, wglu, wa, wb, wc, wout, g2, wr_hi, wr_lo, br)


def _moe_plan(cls):
    ncls = MOE_GROUPS * MOE_PAIRS
    ntiles = MOE_ROWS // MOE_TILE
    onehot = (cls[:, None] == jnp.arange(ncls)[None, :]).astype(jnp.int32)
    rank = jnp.sum((jnp.cumsum(onehot, axis=0) - onehot) * onehot, axis=1)
    ccount = jnp.sum(onehot, axis=0).reshape(MOE_GROUPS, MOE_PAIRS)
    gcount = jnp.sum(ccount, axis=1)
    gpadded = (gcount + MOE_TILE - 1) // MOE_TILE * MOE_TILE
    gend = jnp.cumsum(gpadded)
    cstart = ((gend - gpadded)[:, None] + jnp.cumsum(ccount, axis=1) - ccount).reshape(ncls)
    cend = cstart + ccount.reshape(ncls)
    pos = jnp.sum(onehot * cstart[None, :], axis=1) + rank
    tile_row = jnp.arange(ntiles) * MOE_TILE
    tile_group = jnp.minimum(jnp.sum((tile_row[:, None] >= gend[None, :]).astype(jnp.int32), axis=1),
                             MOE_GROUPS - 1)
    member = np.zeros((ncls, MOE_GROUPS, MOE_EXPERTS_PER_GROUP), np.float32)
    for g in range(MOE_GROUPS):
        for p, pair in enumerate(MOE_PAIR_MEMBERS):
            member[g * MOE_PAIRS + p, g, list(pair)] = 1.0
    overlap = ((cstart[None, :] < tile_row[:, None] + MOE_TILE) & (cend[None, :] > tile_row[:, None])
               & (cend > cstart)[None, :]).astype(F32)
    need = jnp.einsum("tc,cge->tge", overlap, member)
    need = jnp.sum(need * (tile_group[:, None, None] == jnp.arange(MOE_GROUPS)[None, :, None]), axis=1)
    return (pos.astype(jnp.int32), tile_group.astype(jnp.int32), (need > 0).astype(jnp.int32).reshape(-1),
            (gend[-1:] // MOE_TILE).astype(jnp.int32))


def _dispatch_kernel(pos_ref, hx_ref, init_ref, out_ref, sem):
    del init_ref
    base = pl.program_id(0) * MOE_TOK_BLOCK

    def row_copy(r):
        return pltpu.make_async_copy(hx_ref.at[pl.ds(r, 1), :], out_ref.at[pl.ds(pos_ref[base + r], 1), :], sem)

    def start(r, c):
        row_copy(r).start()
        return c

    def wait(r, c):
        row_copy(r).wait()
        return c

    lax.fori_loop(0, MOE_TOK_BLOCK, start, 0, unroll=16)
    lax.fori_loop(0, MOE_TOK_BLOCK, wait, 0, unroll=True)


def _dispatch(hx, pos):
    return pl.pallas_call(
        _dispatch_kernel,
        grid_spec=pltpu.PrefetchScalarGridSpec(
            num_scalar_prefetch=1,
            grid=(N_TOK // MOE_TOK_BLOCK,),
            in_specs=[pl.BlockSpec((MOE_TOK_BLOCK, MOE_ROW_WORDS), lambda i, pos: (i, 0)),
                      pl.BlockSpec(memory_space=pl.ANY)],
            out_specs=pl.BlockSpec(memory_space=pl.ANY),
            scratch_shapes=[pltpu.SemaphoreType.DMA]),
        out_shape=jax.ShapeDtypeStruct((MOE_ROWS, MOE_ROW_WORDS), jnp.uint32),
        input_output_aliases={2: 0},
        compiler_params=pltpu.CompilerParams(dimension_semantics=("arbitrary",)),
        name="moe_dispatch",
    )(pos, hx, jnp.zeros((MOE_ROWS, MOE_ROW_WORDS), jnp.uint32))


def _experts_kernel(tg_ref, need_ref, nu_ref, hx_ref, wg_ref, wu_ref, wd_ref, y_ref):
    t = pl.program_id(0)
    y_ref[...] = jnp.zeros_like(y_ref)
    half = D_MODEL // 2
    for e in range(MOE_EXPERTS_PER_GROUP):
        @pl.when((t < nu_ref[0]) & (need_ref[t * MOE_EXPERTS_PER_GROUP + e] > 0))
        def _():
            h = _unpack_pairs(hx_ref[:, :half]).astype(BF16)
            comb = pltpu.bitcast(hx_ref[:, half:], F32)
            lane = lax.broadcasted_iota(jnp.int32, comb.shape, 1)
            ce = jnp.sum(jnp.where(lane == tg_ref[t] * MOE_EXPERTS_PER_GROUP + e, comb, 0.0), axis=-1,
                         keepdims=True)
            a = _dot(h, wg_ref[e])
            b = _dot(h, wu_ref[e])
            act = (a * jax.nn.sigmoid(a)) * b * ce
            y_ref[...] += _dot(act.astype(BF16), wd_ref[e])


def _experts(hs, tile_group, need, n_used, wg, wu, wd, l):
    wspec = lambda r, c: pl.BlockSpec((None, None, MOE_EXPERTS_PER_GROUP, r, c),
                                      lambda t, tg, need, nu: (l, tg[t], 0, 0, 0))
    return pl.pallas_call(
        _experts_kernel,
        grid_spec=pltpu.PrefetchScalarGridSpec(
            num_scalar_prefetch=3,
            grid=(MOE_ROWS // MOE_TILE,),
            in_specs=[pl.BlockSpec((MOE_TILE, MOE_ROW_WORDS), lambda t, tg, need, nu: (t, 0)),
                      wspec(D_MODEL, EXPERT_FF), wspec(D_MODEL, EXPERT_FF), wspec(EXPERT_FF, D_MODEL)],
            out_specs=pl.BlockSpec((MOE_TILE, D_MODEL), lambda t, tg, need, nu: (t, 0))),
        out_shape=jax.ShapeDtypeStruct((MOE_ROWS, D_MODEL), F32),
        compiler_params=pltpu.CompilerParams(
            dimension_semantics=("arbitrary",), vmem_limit_bytes=VMEM_LIMIT),
        name="moe_experts",
    )(tile_group, need, n_used, hs, wg, wu, wd)


def _combine_kernel(pos_ref, y_ref, x_ref, mod_ref, fg_ref, *rest, final):
    buf, sem = rest[-2:]
    i = pl.program_id(0)
    base = i * MOE_TOK_BLOCK

    def row_copy(r):
        return pltpu.make_async_copy(y_ref.at[pl.ds(pos_ref[base + r], 1), :], buf.at[pl.ds(r, 1), :], sem)

    def start(r, c):
        row_copy(r).start()
        return c

    def wait(r, c):
        row_copy(r).wait()
        return c

    lax.fori_loop(0, MOE_TOK_BLOCK, start, 0, unroll=16)
    lax.fori_loop(0, MOE_TOK_BLOCK, wait, 0, unroll=True)
    x2 = x_ref[...] + mod_ref[5:6, :] * buf[...]
    if not final:
        rest[0][...] = x2
    else:
        y = _rms(x2, fg_ref[...])
        is_ctx = i < N_CTX_TOK // MOE_TOK_BLOCK

        @pl.when(is_ctx)
        def _():
            rest[0][...] = y

        @pl.when(jnp.logical_not(is_ctx))
        def _():
            rest[1][...] = y


def _combine(ys, pos, x1, mod, fg, l, final):
    tm = MOE_TOK_BLOCK
    nctx = N_CTX_TOK // tm
    tok = pl.BlockSpec((tm, D_MODEL), lambda i, pos: (i, 0))
    out = jax.ShapeDtypeStruct((N_TOK, D_MODEL), F32)
    if final:
        out = (jax.ShapeDtypeStruct((N_CTX_TOK, D_MODEL), F32), jax.ShapeDtypeStruct((N_LAT_TOK, D_MODEL), F32))
        out_specs = (pl.BlockSpec((tm, D_MODEL), lambda i, pos: (jnp.minimum(i, nctx - 1), 0)),
                     pl.BlockSpec((tm, D_MODEL), lambda i, pos: (jnp.maximum(i - nctx, 0), 0)))
    else:
        out_specs = tok
    return pl.pallas_call(
        functools.partial(_combine_kernel, final=final),
        grid_spec=pltpu.PrefetchScalarGridSpec(
            num_scalar_prefetch=1,
            grid=(N_TOK // tm,),
            in_specs=[pl.BlockSpec(memory_space=pl.ANY), tok,
                      pl.BlockSpec((None, None, 6, D_MODEL), lambda i, pos: (l, _cond_index(i, tm), 0, 0)),
                      pl.BlockSpec((1, D_MODEL), lambda i, pos: (0, 0))],
            out_specs=out_specs,
            scratch_shapes=[pltpu.VMEM((tm, D_MODEL), F32), pltpu.SemaphoreType.DMA]),
        out_shape=out,
        compiler_params=pltpu.CompilerParams(
            dimension_semantics=("arbitrary",), vmem_limit_bytes=VMEM_LIMIT),
        name="moe_combine",
    )(pos, ys, x1, mod, fg)


def _pack_w_in(w):
    qkv = w[..., :COL_G_V + GQA_KV_WIDTH]
    u = w[..., COL_G_V + GQA_KV_WIDTH:COL_G_V + GQA_KV_WIDTH + S5_CH]
    gates = w[..., COL_G_V + GQA_KV_WIDTH + S5_CH:]
    pad = jnp.zeros(w.shape[:-1] + (COL_U - (COL_G_V + GQA_KV_WIDTH),), w.dtype)
    return jnp.concatenate([qkv, pad, u, gates], axis=-1).astype(BF16)


def kernel(x_prompt, x_sample, cache_na_kv, cache_gqa_kv, state_ssm, c, c_ctx, norm_g, w_ada, b_ada, w_in, na_rpb, gqa_sink, s5_a_re, s5_a_im, s5_log_dt, s5_b_re, s5_b_im, s5_c_re, s5_c_im, s5_d, s5_w_glu, w_branch_a, w_branch_b, w_branch_c, w_out, moe_w_group, moe_b_group, moe_w_expert, moe_b_expert, moe_w_gate, moe_w_up, moe_w_down, final_g):
    cond = jnp.zeros((N_COND, D_MODEL), F32).at[0].set(c_ctx.astype(F32)).at[1:1 + DEC_BATCH].set(c.astype(F32))
    mod = _ada_mod(cond, w_ada.astype(F32), b_ada.astype(F32))

    x = (x_prompt.astype(F32).reshape(N_CTX_TOK, D_MODEL), x_sample.astype(F32).reshape(N_LAT_TOK, D_MODEL))
    cache_na = cache_na_kv.reshape(DEC_BATCH, DEPTH, 2, PAST_LEN, NA_WIDTH)
    cache_gqa = cache_gqa_kv.reshape(DEC_BATCH, DEPTH, 2, PAST_LEN, GQA_KV_WIDTH)
    cos_d, sin_d = _rope_tables()
    cos_q, sin_q = jnp.tile(cos_d, (1, GQA_Q_HEADS)), jnp.tile(sin_d, (1, GQA_Q_HEADS))
    cos_k, sin_k = jnp.tile(cos_d, (1, GQA_KV_HEADS)), jnp.tile(sin_d, (1, GQA_KV_HEADS))
    gn = S5_GROUPS * S5_STATE
    fg = final_g.astype(F32).reshape(1, D_MODEL)

    w_in_packed = _pack_w_in(w_in)
    bf16_weights = [w.astype(BF16) for w in (s5_w_glu, w_branch_a, w_branch_b, w_branch_c, w_out)]
    grouped = lambda w: w.astype(BF16).reshape((DEPTH, MOE_GROUPS, MOE_EXPERTS_PER_GROUP) + w.shape[2:])
    moe_weights = [grouped(w) for w in (moe_w_gate, moe_w_up, moe_w_down)]
    ops = jax.vmap(_s5_operators)(s5_a_re, s5_a_im, s5_log_dt, s5_b_re, s5_b_im, s5_c_re, s5_c_im)

    na_list, gqa_list, ssm_list = [], [], []
    y = None
    for l in range(DEPTH):
        z, u, new_na, new_gqa = _inproj(x, norm_g[l, 0].astype(F32).reshape(1, D_MODEL), mod, w_in_packed, l)
        na_list.append(new_na.reshape(BATCH, 2, SEQ, NA_HEADS, HEAD_DIM))
        gqa_list.append(new_gqa.reshape(BATCH, 2, SEQ, GQA_KV_HEADS, HEAD_DIM))

        sink = gqa_sink[l].astype(F32)
        oa_c, ob_c = _ctx_attn(z, sink)
        oa_l = _na_lat(z, cache_na, _na_bias_tables(na_rpb[l]), l)
        ob_l = _swa_lat(z, cache_gqa, sink, cos_q, sin_q, cos_k, sin_k, l)

        h0 = state_ssm[:, l].astype(F32).reshape(DEC_BATCH, 2, 2, gn).transpose(1, 2, 0, 3)
        yc, fin = _s5_scan(u, ops, h0, l)
        ssm_list.append(fin.transpose(2, 0, 1, 3).reshape(BATCH, 2, 2, S5_GROUPS, S5_STATE).astype(x_prompt.dtype))

        wr = jnp.zeros((D_MODEL, 128), F32)
        wr = wr.at[:, :MOE_EXPERTS].set(moe_w_expert[l].astype(F32))
        wr = wr.at[:, MOE_EXPERTS:MOE_EXPERTS + MOE_GROUPS].set(moe_w_group[l].astype(F32))
        br = jnp.zeros((1, 128), F32)
        br = br.at[0, :MOE_EXPERTS].set(moe_b_expert[l].astype(F32))
        br = br.at[0, MOE_EXPERTS:MOE_EXPERTS + MOE_GROUPS].set(moe_b_group[l].astype(F32))
        wr_hi, wr_lo = _split_bf16(wr)
        x1, hx = _merge(
            x, oa_c, oa_l, ob_c, ob_l, u, yc, z, mod, s5_d[l].astype(F32).reshape(1, S5_CH), *bf16_weights,
            norm_g[l, 1].astype(F32).reshape(1, D_MODEL), wr_hi, wr_lo, br, l)
        cls = lax.bitcast_convert_type(hx[:, D_MODEL // 2 + MOE_EXPERTS], F32).astype(jnp.int32)
        pos, tile_group, need, n_used = _moe_plan(cls)
        ys = _experts(_dispatch(hx, pos), tile_group, need, n_used, *moe_weights, l)
        if l < DEPTH - 1:
            x = _combine(ys, pos, x1, mod, fg, l, False)
        else:
            y_ctx, y_lat = _combine(ys, pos, x1, mod, fg, l, True)

    return (y_ctx.reshape(BATCH, SEQ, D_MODEL), y_lat.reshape(DEC_BATCH, DEC_SEQ, D_MODEL),
            jnp.stack(na_list, axis=1), jnp.stack(gqa_list, axis=1), jnp.stack(ssm_list, axis=1))
```

```python
import functools
import math

import numpy as np
import jax
import jax.numpy as jnp
from jax import lax
from jax.experimental import pallas as pl
from jax.experimental.pallas import tpu as pltpu

F32 = jnp.float32
BF16 = jnp.bfloat16

D_MODEL = 1024
BATCH = 32
SEQ = 256
DEPTH = 2
DEC_BATCH = 4
DEC_SEQ = 2048
PAST_LEN = 512
GRID_W = 64
GRID_ROWS = DEC_SEQ // GRID_W
HEAD_DIM = 64
NA_HEADS = 8
NA_WIN_H = 8
NA_WIN_W = 16
GQA_Q_HEADS = 8
GQA_KV_HEADS = 2
GQA_GROUP = GQA_Q_HEADS // GQA_KV_HEADS
SWA_WINDOW = 128
SWA_BLOCK = 128
ROPE_THETA = 10000.0
S5_CH = 512
S5_GROUP_CH = 16
S5_GROUPS = S5_CH // S5_GROUP_CH
S5_STATE = 64
N_BRANCH = 3
NA_WIDTH = NA_HEADS * HEAD_DIM
GQA_Q_WIDTH = GQA_Q_HEADS * HEAD_DIM
GQA_KV_WIDTH = GQA_KV_HEADS * HEAD_DIM
MOE_GROUPS = 4
MOE_EXPERTS_PER_GROUP = 4
MOE_EXPERTS = MOE_GROUPS * MOE_EXPERTS_PER_GROUP
EXPERT_FF = 512
EPS = 1e-6
NEG_INF = -1e30

N_CTX_TOK = BATCH * SEQ
N_LAT_TOK = DEC_BATCH * DEC_SEQ
N_TOK = N_CTX_TOK + N_LAT_TOK
N_COND = 8

COL_NA_Q = 0
COL_NA_K = 512
COL_NA_V = 1024
COL_G_Q = 1536
COL_G_K = 2048
COL_G_V = 2176
COL_U = 2560
COL_GATES = 3072
Z_COLS = 6144

S5_T = 16
S5_PAIRS = S5_GROUPS // 2
S5_OCT_PAIRS = 4
S5_BLOCK_TOK = 2048
S5_ROWS = S5_BLOCK_TOK // S5_T
S5_NPOW = 7

MOE_ROW_WORDS = D_MODEL // 2 + 128
MOE_TILE = 512
MOE_ROWS = N_TOK + MOE_GROUPS * MOE_TILE
MOE_TOK_BLOCK = 512
MOE_PAIR_MEMBERS = ((0, 1), (0, 2), (0, 3), (1, 3), (1, 2), (2, 3))
MOE_PAIRS = len(MOE_PAIR_MEMBERS)

CTX_HEADS_PER_DOT = 4
NA_HEADS_PER_DOT = 2
NA_QROWS = 4
NA_KROWS = 12

VMEM_LIMIT = 56 * 1024 * 1024


def _dot(a, b):
    return jnp.dot(a, b, preferred_element_type=F32)


def _dot_nt(a, b):
    return lax.dot_general(a, b, (((1,), (1,)), ((), ())), preferred_element_type=F32)


def _split_bf16(x):
    hi = x.astype(BF16)
    lo = (x - hi.astype(F32)).astype(BF16)
    return hi, lo


def _pack_pairs(x):
    w = x.shape[1] // 2
    xb = x.astype(BF16).astype(F32)
    return (pltpu.bitcast(xb[:, :w], jnp.uint32) & jnp.uint32(0xFFFF0000)) | (
        pltpu.bitcast(xb[:, w:], jnp.uint32) >> 16)


def _unpack_pairs(word):
    return jnp.concatenate([pltpu.bitcast(word & jnp.uint32(0xFFFF0000), F32),
                            pltpu.bitcast(word << 16, F32)], axis=1)


def _rms(x, g):
    return x * lax.rsqrt(jnp.mean(x * x, axis=-1, keepdims=True) + EPS) * g


def _cond_index(i, tm):
    nctx = N_CTX_TOK // tm
    return jnp.where(i < nctx, 0, 1 + ((i - nctx) * tm) // DEC_SEQ)


def _ada_kernel(c_ref, w_ref, b_ref, o_ref):
    c = c_ref[...]
    s = c * jax.nn.sigmoid(c)
    s_hi, s_lo = _split_bf16(s)
    w_hi, w_lo = _split_bf16(w_ref[...])
    o_ref[...] = _dot(s_hi, w_hi) + _dot(s_lo, w_hi) + _dot(s_hi, w_lo) + b_ref[...]


def _ada_mod(cond, w_ada, b_ada):
    tn = 1536
    n = 6 * D_MODEL
    out = pl.pallas_call(
        _ada_kernel,
        out_shape=jax.ShapeDtypeStruct((DEPTH, N_COND, n), F32),
        grid=(DEPTH, n // tn),
        in_specs=[
            pl.BlockSpec((N_COND, D_MODEL), lambda l, j: (0, 0)),
            pl.BlockSpec((None, D_MODEL, tn), lambda l, j: (l, 0, j)),
            pl.BlockSpec((None, 1, tn), lambda l, j: (l, 0, j)),
        ],
        out_specs=pl.BlockSpec((None, N_COND, tn), lambda l, j: (l, 0, j)),
        compiler_params=pltpu.CompilerParams(
            dimension_semantics=("parallel", "parallel"), vmem_limit_bytes=VMEM_LIMIT),
        name="ada_mod",
    )(cond, w_ada, b_ada.reshape(DEPTH, 1, n))
    return out.reshape(DEPTH, N_COND, 6, D_MODEL)


INPROJ_TM = 1024
INPROJ_TN = 1536


def _inproj_kernel(xc_ref, xl_ref, g_ref, mod_ref, w_ref, z_ref, u_ref, cna_ref, cgq_ref, h_sc):
    i, j = pl.program_id(0), pl.program_id(1)
    is_ctx = i < N_CTX_TOK // INPROJ_TM

    def norm_mod(x_ref):
        h = _rms(x_ref[...], g_ref[...]) * (1.0 + mod_ref[1:2, :]) + mod_ref[0:1, :]
        h_sc[...] = h.astype(BF16)

    @pl.when((j == 0) & is_ctx)
    def _():
        norm_mod(xc_ref)

    @pl.when((j == 0) & jnp.logical_not(is_ctx))
    def _():
        norm_mod(xl_ref)

    acc = _dot(h_sc[...], w_ref[...])
    z_ref[...] = acc.astype(BF16)

    def rows(b):
        return slice(b * SEQ, (b + 1) * SEQ)

    def cols(c, width):
        return slice(c % INPROJ_TN, c % INPROJ_TN + width)

    @pl.when(is_ctx & (j == COL_NA_K // INPROJ_TN))
    def _():
        for b in range(INPROJ_TM // SEQ):
            cna_ref[b, 0] = acc[rows(b), cols(COL_NA_K, NA_WIDTH)]
            cna_ref[b, 1] = acc[rows(b), cols(COL_NA_V, NA_WIDTH)]

    @pl.when(is_ctx & (j == COL_G_K // INPROJ_TN))
    def _():
        for b in range(INPROJ_TM // SEQ):
            cgq_ref[b, 0] = acc[rows(b), cols(COL_G_K, GQA_KV_WIDTH)]
            cgq_ref[b, 1] = acc[rows(b), cols(COL_G_V, GQA_KV_WIDTH)]

    @pl.when(j == COL_U // INPROJ_TN)
    def _():
        u_ref[...] = acc[:, cols(COL_U, S5_CH)]


def _token_arrays(x):
    return (x, x, N_CTX_TOK) if not isinstance(x, tuple) else (x[0], x[1], 0)


def _inproj(x, g, mod, w, l):
    tm, tn = INPROJ_TM, INPROJ_TN
    nb = tm // SEQ
    last_ctx = N_CTX_TOK // tm - 1
    x_ctx, x_lat, lat0 = _token_arrays(x)
    return pl.pallas_call(
        _inproj_kernel,
        out_shape=(jax.ShapeDtypeStruct((N_TOK, Z_COLS), BF16),
                   jax.ShapeDtypeStruct((N_TOK, S5_CH), F32),
                   jax.ShapeDtypeStruct((BATCH, 2, SEQ, NA_WIDTH), F32),
                   jax.ShapeDtypeStruct((BATCH, 2, SEQ, GQA_KV_WIDTH), F32)),
        grid=(N_TOK // tm, Z_COLS // tn),
        in_specs=[
            pl.BlockSpec((tm, D_MODEL), lambda i, j: (jnp.minimum(i, last_ctx), 0)),
            pl.BlockSpec((tm, D_MODEL), lambda i, j: (jnp.maximum(i - last_ctx - 1, 0) + lat0 // tm, 0)),
            pl.BlockSpec((1, D_MODEL), lambda i, j: (0, 0)),
            pl.BlockSpec((None, None, 6, D_MODEL), lambda i, j: (l, _cond_index(i, tm), 0, 0)),
            pl.BlockSpec((None, D_MODEL, tn), lambda i, j: (l, 0, j)),
        ],
        out_specs=(pl.BlockSpec((tm, tn), lambda i, j: (i, j)),
                   pl.BlockSpec((tm, S5_CH), lambda i, j: (i, 0)),
                   pl.BlockSpec((nb, 2, SEQ, NA_WIDTH), lambda i, j: (jnp.minimum(i, last_ctx), 0, 0, 0)),
                   pl.BlockSpec((nb, 2, SEQ, GQA_KV_WIDTH), lambda i, j: (jnp.minimum(i, last_ctx), 0, 0, 0))),
        scratch_shapes=[pltpu.VMEM((tm, D_MODEL), BF16)],
        compiler_params=pltpu.CompilerParams(
            dimension_semantics=("arbitrary", "arbitrary"), vmem_limit_bytes=VMEM_LIMIT),
        name="inproj",
    )(x_ctx, x_lat, g, mod, w)


def _block_diag_heads(x, nrep):
    t = x.shape[0]
    rows = lax.broadcasted_iota(jnp.int32, (nrep * t, nrep * HEAD_DIM), 0) // t
    lanes = lax.broadcasted_iota(jnp.int32, (nrep * t, nrep * HEAD_DIM), 1) // HEAD_DIM
    return jnp.where(rows == lanes, jnp.concatenate([x] * nrep, axis=0), jnp.zeros((), x.dtype))


def _per_head(cols, shape):
    head = lax.broadcasted_iota(jnp.int32, shape, 1) // HEAD_DIM
    out = cols[-1]
    for h in range(len(cols) - 2, -1, -1):
        out = jnp.where(head == h, cols[h], out)
    return out


def _attend_heads(q, k, v, sinks):
    nh = q.shape[1] // HEAD_DIM
    t = k.shape[0]
    s = _dot_nt(q * HEAD_DIM ** -0.5, _block_diag_heads(k, nh))
    ps, ds = [], []
    for h in range(nh):
        sh = s[:, h * t:(h + 1) * t]
        m = jnp.max(sh, axis=-1, keepdims=True)
        if sinks is not None:
            m = jnp.maximum(m, sinks[h])
        p = jnp.exp(sh - m)
        d = jnp.sum(p, axis=-1, keepdims=True)
        if sinks is not None:
            d = d + jnp.exp(sinks[h] - m)
        ps.append(p.astype(BF16))
        ds.append(d)
    o = _dot(jnp.concatenate(ps, axis=1), _block_diag_heads(v, nh))
    return o / _per_head(ds, o.shape)


def _ctx_attn_kernel(sink_ref, na_ref, gq_ref, gkv_ref, oa_ref, ob_ref):
    hb = CTX_HEADS_PER_DOT
    w = hb * HEAD_DIM
    na = na_ref[...]
    gq = gq_ref[...]
    gkv = gkv_ref[...]
    for g in range(NA_HEADS // hb):
        o = _attend_heads(na[:, COL_NA_Q + g * w:COL_NA_Q + (g + 1) * w],
                          na[:, COL_NA_K + g * w:COL_NA_K + (g + 1) * w],
                          na[:, COL_NA_V + g * w:COL_NA_V + (g + 1) * w], None)
        oa_ref[:, g * w:(g + 1) * w] = o.astype(BF16)
    for g in range(GQA_Q_HEADS // hb):
        kv_heads = [(g * hb + h) // GQA_GROUP for h in range(hb)]
        k = jnp.concatenate([gkv[:, kh * HEAD_DIM:(kh + 1) * HEAD_DIM] for kh in kv_heads], axis=1)
        v = jnp.concatenate([gkv[:, GQA_KV_WIDTH + kh * HEAD_DIM:GQA_KV_WIDTH + (kh + 1) * HEAD_DIM]
                             for kh in kv_heads], axis=1)
        o = _attend_heads(gq[:, g * w:(g + 1) * w], k, v, [sink_ref[g * hb + h] for h in range(hb)])
        ob_ref[:, g * w:(g + 1) * w] = o.astype(BF16)


def _ctx_attn(z, sink):
    out = jax.ShapeDtypeStruct((N_CTX_TOK, NA_WIDTH), BF16)
    return pl.pallas_call(
        _ctx_attn_kernel,
        out_shape=(out, out),
        grid=(BATCH,),
        in_specs=[
            pl.BlockSpec(memory_space=pltpu.SMEM),
            pl.BlockSpec((SEQ, 3 * NA_WIDTH), lambda b: (b, 0)),
            pl.BlockSpec((SEQ, GQA_Q_WIDTH), lambda b: (b, COL_G_Q // GQA_Q_WIDTH)),
            pl.BlockSpec((SEQ, 2 * GQA_KV_WIDTH), lambda b: (b, COL_G_K // (2 * GQA_KV_WIDTH))),
        ],
        out_specs=(pl.BlockSpec((SEQ, NA_WIDTH), lambda b: (b, 0)),
                   pl.BlockSpec((SEQ, GQA_Q_WIDTH), lambda b: (b, 0))),
        compiler_params=pltpu.CompilerParams(
            dimension_semantics=("parallel",), vmem_limit_bytes=VMEM_LIMIT),
        name="ctx_attn",
    )(sink, z, z, z)


def _na_bias_tables(rpb):
    n_dc = 2 * NA_WIN_W - 1
    qc = np.arange(GRID_W)
    kc = np.arange(GRID_W)
    cstart = np.clip(qc - NA_WIN_W // 2, 0, GRID_W - NA_WIN_W)
    col_ok = (kc[None, :] >= cstart[:, None]) & (kc[None, :] < cstart[:, None] + NA_WIN_W)
    dc = np.clip(kc[None, :] - qc[:, None] + (NA_WIN_W - 1), 0, n_dc - 1)
    onehot = ((np.arange(n_dc)[:, None, None] == dc[None]) & col_ok[None]).astype(np.float32)
    band = jnp.einsum("hrd,dqk->hrqk", rpb.astype(F32), onehot, precision=lax.Precision.HIGHEST)
    band = jnp.where(col_ok, band, NEG_INF)
    band = jnp.pad(band, ((0, 0), (1, 1), (0, 0), (0, 0)), constant_values=NEG_INF)
    return jnp.concatenate([band[:, :-1], band[:, 1:]], axis=-1)


def _na_lat_kernel(q_ref, k_ref, v_ref, ck_ref, cv_ref, bias_ref, o_ref):
    i = pl.program_id(1)
    scale = HEAD_DIM ** -0.5
    nk = NA_KROWS * GRID_W
    kh = min(NA_WIN_H, GRID_ROWS)
    ks = jnp.clip(i * NA_QROWS - NA_WIN_H // 2, 0, GRID_ROWS - NA_KROWS)
    start = pl.multiple_of(ks * GRID_W, GRID_W)
    q = q_ref[...].astype(BF16)
    kw = k_ref[pl.ds(start, nk), :].astype(BF16)
    vw = v_ref[pl.ds(start, nk), :].astype(BF16)
    ck = ck_ref[...].astype(BF16)
    cv = cv_ref[...].astype(BF16)
    left = lax.broadcasted_iota(jnp.int32, (GRID_W, 2 * GRID_W), 1) < GRID_W
    pieces = []
    for qr in range(NA_QROWS):
        r = i * NA_QROWS + qr
        st = jnp.clip(r - kh // 2, 0, GRID_ROWS - kh)
        row = []
        for m in range(NA_KROWS // 2):
            k0 = ks + 2 * m
            plane = jnp.clip(k0 - r + NA_WIN_H, 0, 2 * NA_WIN_H - 1)
            ok0 = (k0 >= st) & (k0 < st + kh)
            ok1 = (k0 + 1 >= st) & (k0 + 1 < st + kh)
            row.append((plane, jnp.where(left, ok0.astype(jnp.int32), ok1.astype(jnp.int32)) > 0))
        pieces.append(row)
    hb = NA_HEADS_PER_DOT
    w = hb * HEAD_DIM
    for g in range(NA_HEADS // hb):
        lanes = slice(g * w, (g + 1) * w)
        qg = q[:, lanes] * scale
        s_loc = _dot_nt(qg, _block_diag_heads(kw[:, lanes], hb))
        s_ctx = _dot_nt(qg, _block_diag_heads(ck[:, lanes], hb))
        p_locs, p_ctxs, ds = [], [], []
        for hh in range(hb):
            bias = jnp.concatenate(
                [jnp.concatenate([jnp.where(ok, bias_ref[g * hb + hh, plane], NEG_INF) for plane, ok in row], axis=1)
                 for row in pieces], axis=0)
            sl = s_loc[:, hh * nk:(hh + 1) * nk] + bias
            sc = s_ctx[:, hh * PAST_LEN:(hh + 1) * PAST_LEN]
            m = jnp.maximum(jnp.max(sl, axis=-1, keepdims=True), jnp.max(sc, axis=-1, keepdims=True))
            p_loc = jnp.exp(sl - m)
            p_ctx = jnp.exp(sc - m)
            ds.append(jnp.sum(p_loc, axis=-1, keepdims=True) + jnp.sum(p_ctx, axis=-1, keepdims=True))
            p_locs.append(p_loc.astype(BF16))
            p_ctxs.append(p_ctx.astype(BF16))
        o = (_dot(jnp.concatenate(p_locs, axis=1), _block_diag_heads(vw[:, lanes], hb))
             + _dot(jnp.concatenate(p_ctxs, axis=1), _block_diag_heads(cv[:, lanes], hb)))
        o_ref[:, lanes] = (o / _per_head(ds, o.shape)).astype(BF16)


def _na_lat(z, cache_na, bias, l):
    tq = NA_QROWS * GRID_W
    nsteps = GRID_ROWS // NA_QROWS
    lat_blk = N_CTX_TOK // DEC_SEQ
    return pl.pallas_call(
        _na_lat_kernel,
        out_shape=jax.ShapeDtypeStruct((N_LAT_TOK, NA_WIDTH), BF16),
        grid=(DEC_BATCH, nsteps),
        in_specs=[
            pl.BlockSpec((tq, NA_WIDTH), lambda b, i: (N_CTX_TOK // tq + b * nsteps + i, COL_NA_Q // NA_WIDTH)),
            pl.BlockSpec((DEC_SEQ, NA_WIDTH), lambda b, i: (lat_blk + b, COL_NA_K // NA_WIDTH)),
            pl.BlockSpec((DEC_SEQ, NA_WIDTH), lambda b, i: (lat_blk + b, COL_NA_V // NA_WIDTH)),
            pl.BlockSpec((None, None, None, PAST_LEN, NA_WIDTH), lambda b, i: (b, l, 0, 0, 0)),
            pl.BlockSpec((None, None, None, PAST_LEN, NA_WIDTH), lambda b, i: (b, l, 1, 0, 0)),
            pl.BlockSpec((NA_HEADS, 2 * NA_WIN_H, GRID_W, 2 * GRID_W), lambda b, i: (0, 0, 0, 0)),
        ],
        out_specs=pl.BlockSpec((tq, NA_WIDTH), lambda b, i: (b * nsteps + i, 0)),
        compiler_params=pltpu.CompilerParams(
            dimension_semantics=("parallel", "arbitrary"), vmem_limit_bytes=VMEM_LIMIT),
        name="na_lat",
    )(z, z, z, cache_na, cache_na, bias)


def _rope_tables():
    nf = HEAD_DIM // 4
    t = jnp.arange(DEC_SEQ)
    pos = jnp.stack([t // GRID_W, t % GRID_W], axis=-1).astype(F32)
    inv = ROPE_THETA ** (-jnp.arange(nf, dtype=F32) / nf)
    ang = pos[:, :, None] * inv
    cos = jnp.cos(ang)
    sin = jnp.sin(ang)
    cos_d = jnp.stack([cos, cos], axis=2).reshape(DEC_SEQ, HEAD_DIM)
    sin_d = jnp.stack([-sin, sin], axis=2).reshape(DEC_SEQ, HEAD_DIM)
    return cos_d, sin_d


def _rope(x, cos, sin_signed):
    n = x.shape[-1]
    nf = HEAD_DIM // 4
    lane = lax.broadcasted_iota(jnp.int32, x.shape, 1)
    first_half = (lane // nf) % 2 == 0
    partner = jnp.where(first_half, pltpu.roll(x, n - nf, 1), pltpu.roll(x, nf, 1))
    return x * cos + partner * sin_signed


def _swa_lat_kernel(sink_ref, q_ref, kv_ref, cq_ref, sq_ref, ckt_ref, skt_ref, ck_ref, cv_ref, o_ref, k_sc, v_sc):
    n = pl.program_id(1)
    scale = HEAD_DIM ** -0.5
    nwin = 3 * SWA_BLOCK

    @pl.when(n == 0)
    def _():
        kv = kv_ref[...]
        k_sc[...] = _rope(kv[:, :GQA_KV_WIDTH].astype(F32), ckt_ref[...], skt_ref[...]).astype(BF16)
        v_sc[...] = kv[:, GQA_KV_WIDTH:].astype(BF16)

    q = (_rope(q_ref[...].astype(F32), cq_ref[...], sq_ref[...]) * scale).astype(BF16)
    start = pl.multiple_of(jnp.clip((n - 1) * SWA_BLOCK, 0, DEC_SEQ - nwin), SWA_BLOCK)
    kw = k_sc[pl.ds(start, nwin), :]
    vw = v_sc[pl.ds(start, nwin), :]
    ck = ck_ref[...].astype(BF16)
    cv = cv_ref[...].astype(BF16)
    rows = GQA_GROUP * SWA_BLOCK
    row = lax.broadcasted_iota(jnp.int32, (rows, nwin), 0)
    col = lax.broadcasted_iota(jnp.int32, (rows, nwin), 1)
    qpos = n * SWA_BLOCK + row % SWA_BLOCK
    kpos = start + col
    ok = jnp.abs(qpos - kpos) <= SWA_WINDOW
    grp = lax.broadcasted_iota(jnp.int32, (rows, 1), 0) // SWA_BLOCK
    for kh in range(GQA_KV_HEADS):
        sl = slice(kh * HEAD_DIM, (kh + 1) * HEAD_DIM)
        q4 = jnp.concatenate(
            [q[:, (kh * GQA_GROUP + g) * HEAD_DIM:(kh * GQA_GROUP + g + 1) * HEAD_DIM] for g in range(GQA_GROUP)],
            axis=0)
        sink = jnp.zeros((rows, 1), F32)
        for g in range(GQA_GROUP):
            sink = jnp.where(grp == g, sink_ref[kh * GQA_GROUP + g], sink)
        s_loc = jnp.where(ok, _dot_nt(q4, kw[:, sl]), NEG_INF)
        s_ctx = _dot_nt(q4, ck[:, sl])
        m = jnp.maximum(jnp.maximum(jnp.max(s_loc, axis=-1, keepdims=True),
                                    jnp.max(s_ctx, axis=-1, keepdims=True)), sink)
        p_loc = jnp.exp(s_loc - m)
        p_ctx = jnp.exp(s_ctx - m)
        d = (jnp.sum(p_loc, axis=-1, keepdims=True) + jnp.sum(p_ctx, axis=-1, keepdims=True)
             + jnp.exp(sink - m))
        o4 = (_dot(p_loc.astype(BF16), vw[:, sl]) + _dot(p_ctx.astype(BF16), cv[:, sl])) / d
        for g in range(GQA_GROUP):
            h = kh * GQA_GROUP + g
            o_ref[:, h * HEAD_DIM:(h + 1) * HEAD_DIM] = o4[g * SWA_BLOCK:(g + 1) * SWA_BLOCK].astype(BF16)


def _swa_lat(z, cache_gqa, sink, cos_q, sin_q, cos_k, sin_k, l):
    nb = DEC_SEQ // SWA_BLOCK
    lat_blk = N_CTX_TOK // DEC_SEQ
    return pl.pallas_call(
        _swa_lat_kernel,
        out_shape=jax.ShapeDtypeStruct((N_LAT_TOK, GQA_Q_WIDTH), BF16),
        grid=(DEC_BATCH, nb),
        in_specs=[
            pl.BlockSpec(memory_space=pltpu.SMEM),
            pl.BlockSpec((SWA_BLOCK, GQA_Q_WIDTH),
                         lambda b, n: (N_CTX_TOK // SWA_BLOCK + b * nb + n, COL_G_Q // GQA_Q_WIDTH)),
            pl.BlockSpec((DEC_SEQ, 2 * GQA_KV_WIDTH), lambda b, n: (lat_blk + b, COL_G_K // (2 * GQA_KV_WIDTH))),
            pl.BlockSpec((SWA_BLOCK, GQA_Q_WIDTH), lambda b, n: (n, 0)),
            pl.BlockSpec((SWA_BLOCK, GQA_Q_WIDTH), lambda b, n: (n, 0)),
            pl.BlockSpec((DEC_SEQ, GQA_KV_WIDTH), lambda b, n: (0, 0)),
            pl.BlockSpec((DEC_SEQ, GQA_KV_WIDTH), lambda b, n: (0, 0)),
            pl.BlockSpec((None, None, None, PAST_LEN, GQA_KV_WIDTH), lambda b, n: (b, l, 0, 0, 0)),
            pl.BlockSpec((None, None, None, PAST_LEN, GQA_KV_WIDTH), lambda b, n: (b, l, 1, 0, 0)),
        ],
        out_specs=pl.BlockSpec((SWA_BLOCK, GQA_Q_WIDTH), lambda b, n: (b * nb + n, 0)),
        scratch_shapes=[pltpu.VMEM((DEC_SEQ, GQA_KV_WIDTH), BF16), pltpu.VMEM((DEC_SEQ, GQA_KV_WIDTH), BF16)],
        compiler_params=pltpu.CompilerParams(
            dimension_semantics=("parallel", "arbitrary"), vmem_limit_bytes=VMEM_LIMIT),
        name="swa_lat",
    )(sink, z, z, cos_q, sin_q, cos_k, sin_k, cache_gqa, cache_gqa)


S5_OP_KEYS = ("kt", "bt_re", "bt_im", "ct_re", "ct_im", "pin_re", "pin_im", "po_re", "po_im", "apr", "api")


def _s5_operators(a_re, a_im, log_dt, b_re, b_im, c_re, c_im):
    T, G, N, C = S5_T, S5_GROUPS, S5_STATE, S5_GROUP_CH
    tau = jnp.arange(T + 1, dtype=F32)
    out = {k: [] for k in S5_OP_KEYS}
    for d in range(2):
        A = lax.complex(a_re[d].astype(F32), a_im[d].astype(F32))
        dt = jnp.exp(log_dt[d].astype(F32))[:, None]
        a_bar = jnp.exp(A * dt)
        pw = jnp.exp((A * dt)[None] * tau[:, None, None])
        b_bar = ((a_bar - 1.0) / A)[..., None] * lax.complex(b_re[d].astype(F32), b_im[d].astype(F32))
        c_mat = lax.complex(c_re[d].astype(F32), c_im[d].astype(F32))
        kern = jnp.einsum("gon,tgn,gni->gtoi", c_mat, pw[:T], b_bar, precision=lax.Precision.HIGHEST).real
        kern = lax.optimization_barrier(kern).transpose(0, 3, 1, 2)
        if d == 0:
            p_in = pw[:T][::-1]
            p_out = pw[1:T + 1]
        else:
            p_in = pw[:T]
            p_out = pw[1:T + 1][::-1]
            kern = kern[:, :, ::-1]
        bt, ct, po = b_bar.transpose(0, 2, 1), c_mat.transpose(0, 2, 1), p_out.transpose(1, 2, 0)
        pw2 = jnp.exp((A * dt)[None] * (T * 2.0 ** jnp.arange(S5_NPOW, dtype=F32))[:, None, None])
        vals = (kern.reshape(G, C, T * C), bt.real, bt.imag, ct.real, ct.imag,
                p_in.real.reshape(T, S5_PAIRS, 1, 2 * N), p_in.imag.reshape(T, S5_PAIRS, 1, 2 * N), po.real, po.imag,
                pw2.real.reshape(S5_NPOW, G * N), pw2.imag.reshape(S5_NPOW, G * N))
        for k, v in zip(S5_OP_KEYS, vals):
            out[k].append(v)
    return {k: jnp.stack(v) for k, v in out.items()}


def _s5_expand(kt_ref, btr_ref, bti_ref, ctr_ref, cti_ref, pir_ref, pii_ref, por_ref, poi_ref,
               bs_sc, m_sc, cre_sc, cim_sc):
    T, C, N = S5_T, S5_GROUP_CH, S5_STATE
    pc, half, width = 2 * C, 2 * N, 2 * T * C
    lane = lax.broadcasted_iota(jnp.int32, (pc, width), 1)
    zc, zn, zo = jnp.zeros((C, C), F32), jnp.zeros((C, N), F32), jnp.zeros((N, C), F32)

    def own_half(x, z, g):
        return [x, z] if g == 0 else [z, x]

    def expand_pair(p, d):
        k0, bx_re, bx_im, co_re, co_im = [], [], [], [], []
        for g in range(2):
            gl = 2 * p + g
            ktg = kt_ref[d, gl]
            k0.append(jnp.concatenate(
                [blk for lag in range(T) for blk in own_half(ktg[:, lag * C:(lag + 1) * C], zc, g)], axis=1))
            bx_re.append(jnp.concatenate(own_half(btr_ref[d, gl], zn, g), axis=1))
            bx_im.append(jnp.concatenate(own_half(bti_ref[d, gl], zn, g), axis=1))
            cr, ci = ctr_ref[d, gl], cti_ref[d, gl]
            pr_all, pi_all = por_ref[d, gl], poi_ref[d, gl]
            re_p, im_p = [], []
            for j in range(T):
                pr, pi = pr_all[:, j:j + 1], pi_all[:, j:j + 1]
                re_p += own_half(cr * pr - ci * pi, zo, g)
                im_p += own_half(cr * pi + ci * pr, zo, g)
            co_re.append(jnp.concatenate(re_p, axis=1))
            co_im.append(jnp.concatenate(im_p, axis=1))
        k0 = jnp.concatenate(k0, axis=0)
        bx_re, bx_im = jnp.concatenate(bx_re, axis=0), jnp.concatenate(bx_im, axis=0)
        for s in range(T):
            if d == 0:
                sh = pc * s
                blk = jnp.where(lane >= sh, pltpu.roll(k0, sh, 1), 0.0) if sh else k0
            else:
                sh = pc * (T - 1 - s)
                blk = jnp.where(lane < width - sh, pltpu.roll(k0, width - sh, 1), 0.0) if sh else k0
            m_sc[d, p, s * pc:(s + 1) * pc, :] = blk.astype(BF16)
            pr, pi = pir_ref[d, s, p], pii_ref[d, s, p]
            bs_sc[d, p, s * pc:(s + 1) * pc, :] = jnp.concatenate(
                [pr * bx_re - pi * bx_im, pr * bx_im + pi * bx_re], axis=1).astype(BF16)
        cre_sc[d, p] = jnp.concatenate(co_re, axis=0).astype(BF16)
        cim_sc[d, p] = (-jnp.concatenate(co_im, axis=0)).astype(BF16)

    for d in range(2):
        lax.fori_loop(0, S5_OCT_PAIRS, lambda p, c, d=d: (expand_pair(p, d), c)[1], 0)


def _s5_kernel(u_ref, kt_ref, btr_ref, bti_ref, ctr_ref, cti_ref, pir_ref, pii_ref, por_ref, poi_ref, apr_ref,
               api_ref, h0_ref, y_ref, fin_ref, fin_sc, bs_sc, m_sc, cre_sc, cim_sc):
    t = pl.program_id(1)
    nctx = N_CTX_TOK // S5_BLOCK_TOK

    @pl.when(t == 0)
    def _():
        _s5_expand(kt_ref, btr_ref, bti_ref, ctr_ref, cti_ref, pir_ref, pii_ref, por_ref, poi_ref,
                   bs_sc, m_sc, cre_sc, cim_sc)

    refs = (u_ref, bs_sc, m_sc, cre_sc, cim_sc, apr_ref, api_ref, h0_ref, y_ref, fin_ref, fin_sc)

    @pl.when(t < nctx)
    def _():
        _s5_block(*refs, t, kseq=SEQ // S5_T, has_h0=False)

    @pl.when(t >= nctx)
    def _():
        _s5_block(*refs, t - nctx, kseq=DEC_SEQ // S5_T, has_h0=True)


def _s5_block(u_ref, bs_ref, m_ref, cre_ref, cim_ref, apr_ref, api_ref, h0_ref, y_ref, fin_ref, fin_sc, t, *,
              kseq, has_h0):
    rows, half, pc = S5_ROWS, 2 * S5_STATE, 2 * S5_GROUP_CH
    k = lax.broadcasted_iota(jnp.int32, (rows, half), 0) % kseq
    shifts = [1 << i for i in range(kseq.bit_length() - 1)]
    xs = [u_ref[pl.ds(s, rows, stride=S5_T), :].astype(BF16) for s in range(S5_T)]
    ys = []
    for p in range(S5_OCT_PAIRS):
        lanes = slice(p * half, (p + 1) * half)
        xp = jnp.concatenate([x[:, p * pc:(p + 1) * pc] for x in xs], axis=1)
        acc = None
        for d in range(2):
            inc = _dot(xp, bs_ref[d, p])
            sr, si = inc[:, :half], inc[:, half:]
            if has_h0:
                h0r = h0_ref[d, 0, t, :, lanes]
                h0i = h0_ref[d, 1, t, :, lanes]
                ar, ai = apr_ref[d, 0:1, lanes], api_ref[d, 0:1, lanes]
                first = (k == 0) if d == 0 else (k == kseq - 1)
                sr = sr + jnp.where(first, ar * h0r - ai * h0i, 0.0)
                si = si + jnp.where(first, ar * h0i + ai * h0r, 0.0)
            else:
                h0r = h0i = 0.0
            for i, sh in enumerate(shifts):
                ar, ai = apr_ref[d, i:i + 1, lanes], api_ref[d, i:i + 1, lanes]
                ok = (k >= sh) if d == 0 else (k < kseq - sh)
                amt = sh if d == 0 else rows - sh
                rr = jnp.where(ok, pltpu.roll(sr, amt, 0), 0.0)
                ri = jnp.where(ok, pltpu.roll(si, amt, 0), 0.0)
                sr, si = sr + ar * rr - ai * ri, si + ar * ri + ai * rr
            inner = (k >= 1) if d == 0 else (k < kseq - 1)
            amt = 1 if d == 0 else rows - 1
            hr = jnp.where(inner, pltpu.roll(sr, amt, 0), h0r)
            hi = jnp.where(inner, pltpu.roll(si, amt, 0), h0i)
            yd = (_dot(xp, m_ref[d, p]) + _dot(hr.astype(BF16), cre_ref[d, p])
                  + _dot(hi.astype(BF16), cim_ref[d, p]))
            acc = yd if acc is None else acc + yd
            if not has_h0:
                last = kseq - 1 if d == 0 else 0
                fin_sc[0] = sr
                fin_sc[1] = si
                fin_ref[d, 0, :, lanes] = fin_sc[0, pl.ds(last, rows // kseq, stride=kseq), :]
                fin_ref[d, 1, :, lanes] = fin_sc[1, pl.ds(last, rows // kseq, stride=kseq), :]
        ys.append(acc)
    for j in range(S5_T):
        y_ref[pl.ds(j, rows, stride=S5_T), :] = jnp.concatenate([y[:, j * pc:(j + 1) * pc] for y in ys], axis=1)


def _s5_scan(u, ops, h0, l):
    nblk, nctx = N_TOK // S5_BLOCK_TOK, N_CTX_TOK // S5_BLOCK_TOK
    noct = S5_PAIRS // S5_OCT_PAIRS
    half, gn = 2 * S5_STATE, S5_GROUPS * S5_STATE
    pw = 2 * S5_T * S5_GROUP_CH
    nseq = S5_BLOCK_TOK // SEQ
    ngrp = 2 * S5_OCT_PAIRS
    C, N, T = S5_GROUP_CH, S5_STATE, S5_T
    gspec = lambda r, c: pl.BlockSpec((None, 2, ngrp, r, c), lambda q, t: (l, 0, q, 0, 0))
    lspec = lambda r: pl.BlockSpec((None, 2, r, S5_OCT_PAIRS * half), lambda q, t: (l, 0, 0, q))
    pspec = pl.BlockSpec((None, 2, T, S5_OCT_PAIRS, 1, half), lambda q, t: (l, 0, 0, q, 0, 0))
    return pl.pallas_call(
        _s5_kernel,
        grid=(noct, nblk),
        in_specs=[
            pl.BlockSpec((S5_BLOCK_TOK, 128), lambda q, t: (t, q)),
            gspec(C, T * C), gspec(C, N), gspec(C, N), gspec(N, C), gspec(N, C), pspec, pspec,
            gspec(N, T), gspec(N, T), lspec(S5_NPOW), lspec(S5_NPOW),
            pl.BlockSpec((2, 2, DEC_BATCH, 1, S5_OCT_PAIRS * half), lambda q, t: (0, 0, 0, 0, q)),
        ],
        out_shape=(jax.ShapeDtypeStruct((N_TOK, S5_CH), F32), jax.ShapeDtypeStruct((2, 2, BATCH, gn), F32)),
        out_specs=(pl.BlockSpec((S5_BLOCK_TOK, 128), lambda q, t: (t, q)),
                   pl.BlockSpec((2, 2, nseq, S5_OCT_PAIRS * half),
                                lambda q, t: (0, 0, jnp.minimum(t, nctx - 1), q))),
        scratch_shapes=[pltpu.VMEM((2, S5_ROWS, half), F32),
                        pltpu.VMEM((2, S5_OCT_PAIRS, pw, 2 * half), BF16), pltpu.VMEM((2, S5_OCT_PAIRS, pw, pw), BF16),
                        pltpu.VMEM((2, S5_OCT_PAIRS, half, pw), BF16), pltpu.VMEM((2, S5_OCT_PAIRS, half, pw), BF16)],
        compiler_params=pltpu.CompilerParams(
            dimension_semantics=("parallel", "arbitrary"), vmem_limit_bytes=VMEM_LIMIT),
        name="s5_scan",
    )(u, *[ops[k] for k in S5_OP_KEYS], h0.reshape(2, 2, DEC_BATCH, 1, gn))


def _route(logits):
    lane = lax.broadcasted_iota(jnp.int32, logits.shape, 1)
    big = jnp.int32(1 << 20)
    is_g = (lane >= MOE_EXPERTS) & (lane < MOE_EXPERTS + MOE_GROUPS)
    lg = jnp.where(is_g, logits, NEG_INF)
    gmax = jnp.max(lg, axis=-1, keepdims=True)
    gsel = jnp.min(jnp.where(is_g & (lg == gmax), lane, big), axis=-1, keepdims=True) - MOE_EXPERTS
    p_group = 1.0 / jnp.sum(jnp.where(is_g, jnp.exp(lg - gmax), 0.0), axis=-1, keepdims=True)
    in_grp = (lane < MOE_EXPERTS) & (lane // MOE_EXPERTS_PER_GROUP == gsel)
    le = jnp.where(in_grp, logits, NEG_INF)
    v1 = jnp.max(le, axis=-1, keepdims=True)
    i1 = jnp.min(jnp.where(in_grp & (le == v1), lane, big), axis=-1, keepdims=True)
    rest = in_grp & (lane != i1)
    le2 = jnp.where(rest, logits, NEG_INF)
    v2 = jnp.max(le2, axis=-1, keepdims=True)
    i2 = jnp.min(jnp.where(rest & (le2 == v2), lane, big), axis=-1, keepdims=True)
    e2 = jnp.exp(v2 - v1)
    w1 = 1.0 / (1.0 + e2)
    w2 = e2 / (1.0 + e2)
    comb = jnp.where(lane == i1, w1 * p_group, 0.0) + jnp.where(lane == i2, w2 * p_group, 0.0)
    a = jnp.minimum(i1, i2) - gsel * MOE_EXPERTS_PER_GROUP
    b = jnp.maximum(i1, i2) - gsel * MOE_EXPERTS_PER_GROUP
    pair = jnp.where(a == 0, b - 1, jnp.where(a == 1, jnp.where(b == 3, 3, 4), 5))
    cls = gsel * MOE_PAIRS + pair
    return jnp.where(lane == MOE_EXPERTS, cls.astype(F32), comb)


def _merge_kernel(xc_ref, xl_ref, oac_ref, oal_ref, obc_ref, obl_ref, u_ref, yc_ref, gates_ref, mod_ref, d_ref,
                  wglu_ref, wa_ref, wb_ref, wc_ref, wout_ref, g2_ref, wrh_ref, wrl_ref, br_ref, x1_ref, hx_ref,
                  *, tm):
    is_ctx = pl.program_id(0) < N_CTX_TOK // tm
    x = jnp.where(is_ctx, xc_ref[...], xl_ref[...])
    oa = jnp.where(is_ctx, oac_ref[...], oal_ref[...])
    ob = jnp.where(is_ctx, obc_ref[...], obl_ref[...])
    y = u_ref[...] * d_ref[...] + yc_ref[...]
    y = y * (0.5 * (1.0 + jnp.tanh(math.sqrt(2.0 / math.pi) * (y + 0.044715 * (y * y * y)))))
    oc = y * jax.nn.sigmoid(_dot(y.astype(BF16), wglu_ref[...]))
    gate = jax.nn.sigmoid(gates_ref[...].astype(F32))
    merged = (gate[:, :D_MODEL] * _dot(oa, wa_ref[...])
              + gate[:, D_MODEL:2 * D_MODEL] * _dot(ob, wb_ref[...])
              + gate[:, 2 * D_MODEL:] * _dot(oc.astype(BF16), wc_ref[...]))
    x1 = x + mod_ref[2:3, :] * _dot(merged.astype(BF16), wout_ref[...])
    x1_ref[...] = x1
    h2 = _rms(x1, g2_ref[...]) * (1.0 + mod_ref[4:5, :]) + mod_ref[3:4, :]
    h_hi, h_lo = _split_bf16(h2)
    logits = _dot(h_hi, wrh_ref[...]) + _dot(h_lo, wrh_ref[...]) + _dot(h_hi, wrl_ref[...]) + br_ref[...]
    half = D_MODEL // 2
    hx_ref[:, :half] = _pack_pairs(h2)
    hx_ref[:, half:] = pltpu.bitcast(_route(logits), jnp.uint32)


def _merge(x, oa_c, oa_l, ob_c, ob_l, u, yc, z, mod, d_s5, wglu, wa, wb, wc, wout, g2, wr_hi, wr_lo, br, l):
    tm = 512
    nctx = N_CTX_TOK // tm
    full = lambda r, c: pl.BlockSpec((r, c), lambda i: (0, 0))
    layer = lambda r, c: pl.BlockSpec((None, r, c), lambda i: (l, 0, 0))
    ctx_blk = lambda w: pl.BlockSpec((tm, w), lambda i: (jnp.minimum(i, nctx - 1), 0))
    lat_blk = lambda w, first=0: pl.BlockSpec((tm, w), lambda i: (jnp.maximum(i - nctx, 0) + first // tm, 0))
    x_ctx, x_lat, lat0 = _token_arrays(x)
    return pl.pallas_call(
        functools.partial(_merge_kernel, tm=tm),
        out_shape=(jax.ShapeDtypeStruct((N_TOK, D_MODEL), F32),
                   jax.ShapeDtypeStruct((N_TOK, MOE_ROW_WORDS), jnp.uint32)),
        grid=(N_TOK // tm,),
        in_specs=[
            ctx_blk(D_MODEL), lat_blk(D_MODEL, lat0),
            ctx_blk(NA_WIDTH), lat_blk(NA_WIDTH), ctx_blk(GQA_Q_WIDTH), lat_blk(GQA_Q_WIDTH),
            pl.BlockSpec((tm, S5_CH), lambda i: (i, 0)),
            pl.BlockSpec((tm, S5_CH), lambda i: (i, 0)),
            pl.BlockSpec((tm, N_BRANCH * D_MODEL), lambda i: (i, COL_GATES // (N_BRANCH * D_MODEL))),
            pl.BlockSpec((None, None, 6, D_MODEL), lambda i: (l, _cond_index(i, tm), 0, 0)),
            full(1, S5_CH), layer(S5_CH, S5_CH), layer(NA_WIDTH, D_MODEL), layer(GQA_Q_WIDTH, D_MODEL),
            layer(S5_CH, D_MODEL), layer(D_MODEL, D_MODEL), full(1, D_MODEL),
            full(D_MODEL, 128), full(D_MODEL, 128), full(1, 128),
        ],
        out_specs=(pl.BlockSpec((tm, D_MODEL), lambda i: (i, 0)),
                   pl.BlockSpec((tm, MOE_ROW_WORDS), lambda i: (i, 0))),
        compiler_params=pltpu.CompilerParams(
            dimension_semantics=("parallel",), vmem_limit_bytes=VMEM_LIMIT),
        name="merge",
    )(x_ctx, x_lat, oa_c, oa_l, ob_c, ob_l, u, yc, z, mod, d_s5, wglu, wa, wb, wc, wout, g2, wr_hi, wr_lo, br)


def _moe_plan(cls):
    ncls = MOE_GROUPS * MOE_PAIRS
    ntiles = MOE_ROWS // MOE_TILE
    onehot = (cls[:, None] == jnp.arange(ncls)[None, :]).astype(jnp.int32)
    rank = jnp.sum((jnp.cumsum(onehot, axis=0) - onehot) * onehot, axis=1)
    ccount = jnp.sum(onehot, axis=0).reshape(MOE_GROUPS, MOE_PAIRS)
    gcount = jnp.sum(ccount, axis=1)
    gpadded = (gcount + MOE_TILE - 1) // MOE_TILE * MOE_TILE
    gend = jnp.cumsum(gpadded)
    cstart = ((gend - gpadded)[:, None] + jnp.cumsum(ccount, axis=1) - ccount).reshape(ncls)
    cend = cstart + ccount.reshape(ncls)
    pos = jnp.sum(onehot * cstart[None, :], axis=1) + rank
    tile_row = jnp.arange(ntiles) * MOE_TILE
    tile_group = jnp.minimum(jnp.sum((tile_row[:, None] >= gend[None, :]).astype(jnp.int32), axis=1),
                             MOE_GROUPS - 1)
    member = np.zeros((ncls, MOE_GROUPS, MOE_EXPERTS_PER_GROUP), np.float32)
    for g in range(MOE_GROUPS):
        for p, pair in enumerate(MOE_PAIR_MEMBERS):
            member[g * MOE_PAIRS + p, g, list(pair)] = 1.0
    overlap = ((cstart[None, :] < tile_row[:, None] + MOE_TILE) & (cend[None, :] > tile_row[:, None])
               & (cend > cstart)[None, :]).astype(F32)
    need = jnp.einsum("tc,cge->tge", overlap, member)
    need = jnp.sum(need * (tile_group[:, None, None] == jnp.arange(MOE_GROUPS)[None, :, None]), axis=1)
    return (pos.astype(jnp.int32), tile_group.astype(jnp.int32), (need > 0).astype(jnp.int32).reshape(-1),
            (gend[-1:] // MOE_TILE).astype(jnp.int32))


def _dispatch_kernel(pos_ref, hx_ref, init_ref, out_ref, sem):
    del init_ref
    base = pl.program_id(0) * MOE_TOK_BLOCK

    def row_copy(r):
        return pltpu.make_async_copy(hx_ref.at[pl.ds(r, 1), :], out_ref.at[pl.ds(pos_ref[base + r], 1), :], sem)

    def start(r, c):
        row_copy(r).start()
        return c

    def wait(r, c):
        row_copy(r).wait()
        return c

    lax.fori_loop(0, MOE_TOK_BLOCK, start, 0, unroll=16)
    lax.fori_loop(0, MOE_TOK_BLOCK, wait, 0, unroll=True)


def _dispatch(hx, pos):
    return pl.pallas_call(
        _dispatch_kernel,
        grid_spec=pltpu.PrefetchScalarGridSpec(
            num_scalar_prefetch=1,
            grid=(N_TOK // MOE_TOK_BLOCK,),
            in_specs=[pl.BlockSpec((MOE_TOK_BLOCK, MOE_ROW_WORDS), lambda i, pos: (i, 0)),
                      pl.BlockSpec(memory_space=pl.ANY)],
            out_specs=pl.BlockSpec(memory_space=pl.ANY),
            scratch_shapes=[pltpu.SemaphoreType.DMA]),
        out_shape=jax.ShapeDtypeStruct((MOE_ROWS, MOE_ROW_WORDS), jnp.uint32),
        input_output_aliases={2: 0},
        compiler_params=pltpu.CompilerParams(dimension_semantics=("arbitrary",)),
        name="moe_dispatch",
    )(pos, hx, jnp.zeros((MOE_ROWS, MOE_ROW_WORDS), jnp.uint32))


def _experts_kernel(tg_ref, need_ref, nu_ref, hx_ref, wg_ref, wu_ref, wd_ref, y_ref):
    t = pl.program_id(0)
    y_ref[...] = jnp.zeros_like(y_ref)
    half = D_MODEL // 2
    for e in range(MOE_EXPERTS_PER_GROUP):
        @pl.when((t < nu_ref[0]) & (need_ref[t * MOE_EXPERTS_PER_GROUP + e] > 0))
        def _():
            h = _unpack_pairs(hx_ref[:, :half]).astype(BF16)
            comb = pltpu.bitcast(hx_ref[:, half:], F32)
            lane = lax.broadcasted_iota(jnp.int32, comb.shape, 1)
            ce = jnp.sum(jnp.where(lane == tg_ref[t] * MOE_EXPERTS_PER_GROUP + e, comb, 0.0), axis=-1,
                         keepdims=True)
            a = _dot(h, wg_ref[e])
            b = _dot(h, wu_ref[e])
            act = (a * jax.nn.sigmoid(a)) * b * ce
            y_ref[...] += _dot(act.astype(BF16), wd_ref[e])


def _experts(hs, tile_group, need, n_used, wg, wu, wd, l):
    wspec = lambda r, c: pl.BlockSpec((None, None, MOE_EXPERTS_PER_GROUP, r, c),
                                      lambda t, tg, need, nu: (l, tg[t], 0, 0, 0))
    return pl.pallas_call(
        _experts_kernel,
        grid_spec=pltpu.PrefetchScalarGridSpec(
            num_scalar_prefetch=3,
            grid=(MOE_ROWS // MOE_TILE,),
            in_specs=[pl.BlockSpec((MOE_TILE, MOE_ROW_WORDS), lambda t, tg, need, nu: (t, 0)),
                      wspec(D_MODEL, EXPERT_FF), wspec(D_MODEL, EXPERT_FF), wspec(EXPERT_FF, D_MODEL)],
            out_specs=pl.BlockSpec((MOE_TILE, D_MODEL), lambda t, tg, need, nu: (t, 0))),
        out_shape=jax.ShapeDtypeStruct((MOE_ROWS, D_MODEL), F32),
        compiler_params=pltpu.CompilerParams(
            dimension_semantics=("arbitrary",), vmem_limit_bytes=VMEM_LIMIT),
        name="moe_experts",
    )(tile_group, need, n_used, hs, wg, wu, wd)


def _combine_kernel(pos_ref, y_ref, x_ref, mod_ref, fg_ref, *rest, final):
    buf, sem = rest[-2:]
    i = pl.program_id(0)
    base = i * MOE_TOK_BLOCK

    def row_copy(r):
        return pltpu.make_async_copy(y_ref.at[pl.ds(pos_ref[base + r], 1), :], buf.at[pl.ds(r, 1), :], sem)

    def start(r, c):
        row_copy(r).start()
        return c

    def wait(r, c):
        row_copy(r).wait()
        return c

    lax.fori_loop(0, MOE_TOK_BLOCK, start, 0, unroll=16)
    lax.fori_loop(0, MOE_TOK_BLOCK, wait, 0, unroll=True)
    x2 = x_ref[...] + mod_ref[5:6, :] * buf[...]
    if not final:
        rest[0][...] = x2
    else:
        y = _rms(x2, fg_ref[...])
        is_ctx = i < N_CTX_TOK // MOE_TOK_BLOCK

        @pl.when(is_ctx)
        def _():
            rest[0][...] = y

        @pl.when(jnp.logical_not(is_ctx))
        def _():
            rest[1][...] = y


def _combine(ys, pos, x1, mod, fg, l, final):
    tm = MOE_TOK_BLOCK
    nctx = N_CTX_TOK // tm
    tok = pl.BlockSpec((tm, D_MODEL), lambda i, pos: (i, 0))
    out = jax.ShapeDtypeStruct((N_TOK, D_MODEL), F32)
    if final:
        out = (jax.ShapeDtypeStruct((N_CTX_TOK, D_MODEL), F32), jax.ShapeDtypeStruct((N_LAT_TOK, D_MODEL), F32))
        out_specs = (pl.BlockSpec((tm, D_MODEL), lambda i, pos: (jnp.minimum(i, nctx - 1), 0)),
                     pl.BlockSpec((tm, D_MODEL), lambda i, pos: (jnp.maximum(i - nctx, 0), 0)))
    else:
        out_specs = tok
    return pl.pallas_call(
        functools.partial(_combine_kernel, final=final),
        grid_spec=pltpu.PrefetchScalarGridSpec(
            num_scalar_prefetch=1,
            grid=(N_TOK // tm,),
            in_specs=[pl.BlockSpec(memory_space=pl.ANY), tok,
                      pl.BlockSpec((None, None, 6, D_MODEL), lambda i, pos: (l, _cond_index(i, tm), 0, 0)),
                      pl.BlockSpec((1, D_MODEL), lambda i, pos: (0, 0))],
            out_specs=out_specs,
            scratch_shapes=[pltpu.VMEM((tm, D_MODEL), F32), pltpu.SemaphoreType.DMA]),
        out_shape=out,
        compiler_params=pltpu.CompilerParams(
            dimension_semantics=("arbitrary",), vmem_limit_bytes=VMEM_LIMIT),
        name="moe_combine",
    )(pos, ys, x1, mod, fg)


def _pack_w_in(w):
    qkv = w[..., :COL_G_V + GQA_KV_WIDTH]
    u = w[..., COL_G_V + GQA_KV_WIDTH:COL_G_V + GQA_KV_WIDTH + S5_CH]
    gates = w[..., COL_G_V + GQA_KV_WIDTH + S5_CH:]
    pad = jnp.zeros(w.shape[:-1] + (COL_U - (COL_G_V + GQA_KV_WIDTH),), w.dtype)
    return jnp.concatenate([qkv, pad, u, gates], axis=-1).astype(BF16)


def kernel(x_prompt, x_sample, cache_na_kv, cache_gqa_kv, state_ssm, c, c_ctx, norm_g, w_ada, b_ada, w_in, na_rpb, gqa_sink, s5_a_re, s5_a_im, s5_log_dt, s5_b_re, s5_b_im, s5_c_re, s5_c_im, s5_d, s5_w_glu, w_branch_a, w_branch_b, w_branch_c, w_out, moe_w_group, moe_b_group, moe_w_expert, moe_b_expert, moe_w_gate, moe_w_up, moe_w_down, final_g):
    cond = jnp.zeros((N_COND, D_MODEL), F32).at[0].set(c_ctx.astype(F32)).at[1:1 + DEC_BATCH].set(c.astype(F32))
    mod = _ada_mod(cond, w_ada.astype(F32), b_ada.astype(F32))

    x = (x_prompt.astype(F32).reshape(N_CTX_TOK, D_MODEL), x_sample.astype(F32).reshape(N_LAT_TOK, D_MODEL))
    cache_na = cache_na_kv.reshape(DEC_BATCH, DEPTH, 2, PAST_LEN, NA_WIDTH)
    cache_gqa = cache_gqa_kv.reshape(DEC_BATCH, DEPTH, 2, PAST_LEN, GQA_KV_WIDTH)
    cos_d, sin_d = _rope_tables()
    cos_q, sin_q = jnp.tile(cos_d, (1, GQA_Q_HEADS)), jnp.tile(sin_d, (1, GQA_Q_HEADS))
    cos_k, sin_k = jnp.tile(cos_d, (1, GQA_KV_HEADS)), jnp.tile(sin_d, (1, GQA_KV_HEADS))
    gn = S5_GROUPS * S5_STATE
    fg = final_g.astype(F32).reshape(1, D_MODEL)

    w_in_packed = _pack_w_in(w_in)
    bf16_weights = [w.astype(BF16) for w in (s5_w_glu, w_branch_a, w_branch_b, w_branch_c, w_out)]
    grouped = lambda w: w.astype(BF16).reshape((DEPTH, MOE_GROUPS, MOE_EXPERTS_PER_GROUP) + w.shape[2:])
    moe_weights = [grouped(w) for w in (moe_w_gate, moe_w_up, moe_w_down)]
    ops = jax.vmap(_s5_operators)(s5_a_re, s5_a_im, s5_log_dt, s5_b_re, s5_b_im, s5_c_re, s5_c_im)

    na_list, gqa_list, ssm_list = [], [], []
    y = None
    for l in range(DEPTH):
        z, u, new_na, new_gqa = _inproj(x, norm_g[l, 0].astype(F32).reshape(1, D_MODEL), mod, w_in_packed, l)
        na_list.append(new_na.reshape(BATCH, 2, SEQ, NA_HEADS, HEAD_DIM))
        gqa_list.append(new_gqa.reshape(BATCH, 2, SEQ, GQA_KV_HEADS, HEAD_DIM))

        sink = gqa_sink[l].astype(F32)
        oa_c, ob_c = _ctx_attn(z, sink)
        oa_l = _na_lat(z, cache_na, _na_bias_tables(na_rpb[l]), l)
        ob_l = _swa_lat(z, cache_gqa, sink, cos_q, sin_q, cos_k, sin_k, l)

        h0 = state_ssm[:, l].astype(F32).reshape(DEC_BATCH, 2, 2, gn).transpose(1, 2, 0, 3)
        yc, fin = _s5_scan(u, ops, h0, l)
        ssm_list.append(fin.transpose(2, 0, 1, 3).reshape(BATCH, 2, 2, S5_GROUPS, S5_STATE).astype(x_prompt.dtype))

        wr = jnp.zeros((D_MODEL, 128), F32)
        wr = wr.at[:, :MOE_EXPERTS].set(moe_w_expert[l].astype(F32))
        wr = wr.at[:, MOE_EXPERTS:MOE_EXPERTS + MOE_GROUPS].set(moe_w_group[l].astype(F32))
        br = jnp.zeros((1, 128), F32)
        br = br.at[0, :MOE_EXPERTS].set(moe_b_expert[l].astype(F32))
        br = br.at[0, MOE_EXPERTS:MOE_EXPERTS + MOE_GROUPS].set(moe_b_group[l].astype(F32))
        wr_hi, wr_lo = _split_bf16(wr)
        x1, hx = _merge(
            x, oa_c, oa_l, ob_c, ob_l, u, yc, z, mod, s5_d[l].astype(F32).reshape(1, S5_CH), *bf16_weights,
            norm_g[l, 1].astype(F32).reshape(1, D_MODEL), wr_hi, wr_lo, br, l)
        cls = lax.bitcast_convert_type(hx[:, D_MODEL // 2 + MOE_EXPERTS], F32).astype(jnp.int32)
        pos, tile_group, need, n_used = _moe_plan(cls)
        ys = _experts(_dispatch(hx, pos), tile_group, need, n_used, *moe_weights, l)
        if l < DEPTH - 1:
            x = _combine(ys, pos, x1, mod, fg, l, False)
        else:
            y_ctx, y_lat = _combine(ys, pos, x1, mod, fg, l, True)

    return (y_ctx.reshape(BATCH, SEQ, D_MODEL), y_lat.reshape(DEC_BATCH, DEC_SEQ, D_MODEL),
            jnp.stack(na_list, axis=1), jnp.stack(gqa_list, axis=1), jnp.stack(ssm_list, axis=1))
```

```python
import functools
import math

import numpy as np
import jax
import jax.numpy as jnp
from jax import lax
from jax.experimental import pallas as pl
from jax.experimental.pallas import tpu as pltpu

F32 = jnp.float32
BF16 = jnp.bfloat16

D_MODEL = 1024
BATCH = 32
SEQ = 256
DEPTH = 2
DEC_BATCH = 4
DEC_SEQ = 2048
PAST_LEN = 512
GRID_W = 64
GRID_ROWS = DEC_SEQ // GRID_W
HEAD_DIM = 64
NA_HEADS = 8
NA_WIN_H = 8
NA_WIN_W = 16
GQA_Q_HEADS = 8
GQA_KV_HEADS = 2
GQA_GROUP = GQA_Q_HEADS // GQA_KV_HEADS
SWA_WINDOW = 128
SWA_BLOCK = 128
ROPE_THETA = 10000.0
S5_CH = 512
S5_GROUP_CH = 16
S5_GROUPS = S5_CH // S5_GROUP_CH
S5_STATE = 64
N_BRANCH = 3
NA_WIDTH = NA_HEADS * HEAD_DIM
GQA_Q_WIDTH = GQA_Q_HEADS * HEAD_DIM
GQA_KV_WIDTH = GQA_KV_HEADS * HEAD_DIM
MOE_GROUPS = 4
MOE_EXPERTS_PER_GROUP = 4
MOE_EXPERTS = MOE_GROUPS * MOE_EXPERTS_PER_GROUP
EXPERT_FF = 512
EPS = 1e-6
NEG_INF = -1e30

N_CTX_TOK = BATCH * SEQ
N_LAT_TOK = DEC_BATCH * DEC_SEQ
N_TOK = N_CTX_TOK + N_LAT_TOK
N_COND = 8

COL_NA_Q = 0
COL_NA_K = 512
COL_NA_V = 1024
COL_G_Q = 1536
COL_G_K = 2048
COL_G_V = 2176
COL_U = 2560
COL_GATES = 3072
Z_COLS = 6144

S5_T = 16
S5_PAIRS = S5_GROUPS // 2
S5_OCT_PAIRS = 4
S5_BLOCK_TOK = 2048
S5_ROWS = S5_BLOCK_TOK // S5_T
S5_NPOW = 7

MOE_ROW_WORDS = D_MODEL // 2 + 128
MOE_TILE = 512
MOE_ROWS = N_TOK + MOE_GROUPS * MOE_TILE
MOE_TOK_BLOCK = 512
MOE_PAIR_MEMBERS = ((0, 1), (0, 2), (0, 3), (1, 3), (1, 2), (2, 3))
MOE_PAIRS = len(MOE_PAIR_MEMBERS)

CTX_HEADS_PER_DOT = 4
NA_HEADS_PER_DOT = 2
NA_QROWS = 4
NA_KROWS = 12

VMEM_LIMIT = 56 * 1024 * 1024


def _dot(a, b):
    return jnp.dot(a, b, preferred_element_type=F32)


def _dot_nt(a, b):
    return lax.dot_general(a, b, (((1,), (1,)), ((), ())), preferred_element_type=F32)


def _split_bf16(x):
    hi = x.astype(BF16)
    lo = (x - hi.astype(F32)).astype(BF16)
    return hi, lo


def _pack_pairs(x):
    w = x.shape[1] // 2
    xb = x.astype(BF16).astype(F32)
    return (pltpu.bitcast(xb[:, :w], jnp.uint32) & jnp.uint32(0xFFFF0000)) | (
        pltpu.bitcast(xb[:, w:], jnp.uint32) >> 16)


def _unpack_pairs(word):
    return jnp.concatenate([pltpu.bitcast(word & jnp.uint32(0xFFFF0000), F32),
                            pltpu.bitcast(word << 16, F32)], axis=1)


def _rms(x, g):
    return x * lax.rsqrt(jnp.mean(x * x, axis=-1, keepdims=True) + EPS) * g


def _cond_index(i, tm):
    nctx = N_CTX_TOK // tm
    return jnp.where(i < nctx, 0, 1 + ((i - nctx) * tm) // DEC_SEQ)


def _ada_kernel(c_ref, w_ref, b_ref, o_ref):
    c = c_ref[...]
    s = c * jax.nn.sigmoid(c)
    s_hi, s_lo = _split_bf16(s)
    w_hi, w_lo = _split_bf16(w_ref[...])
    o_ref[...] = _dot(s_hi, w_hi) + _dot(s_lo, w_hi) + _dot(s_hi, w_lo) + b_ref[...]


def _ada_mod(cond, w_ada, b_ada):
    tn = 1536
    n = 6 * D_MODEL
    out = pl.pallas_call(
        _ada_kernel,
        out_shape=jax.ShapeDtypeStruct((DEPTH, N_COND, n), F32),
        grid=(DEPTH, n // tn),
        in_specs=[
            pl.BlockSpec((N_COND, D_MODEL), lambda l, j: (0, 0)),
            pl.BlockSpec((None, D_MODEL, tn), lambda l, j: (l, 0, j)),
            pl.BlockSpec((None, 1, tn), lambda l, j: (l, 0, j)),
        ],
        out_specs=pl.BlockSpec((None, N_COND, tn), lambda l, j: (l, 0, j)),
        compiler_params=pltpu.CompilerParams(
            dimension_semantics=("parallel", "parallel"), vmem_limit_bytes=VMEM_LIMIT),
        name="ada_mod",
    )(cond, w_ada, b_ada.reshape(DEPTH, 1, n))
    return out.reshape(DEPTH, N_COND, 6, D_MODEL)


INPROJ_TM = 1024
INPROJ_TN = 1536


def _inproj_kernel(xc_ref, xl_ref, g_ref, mod_ref, w_ref, z_ref, u_ref, cna_ref, cgq_ref, h_sc):
    i, j = pl.program_id(0), pl.program_id(1)
    is_ctx = i < N_CTX_TOK // INPROJ_TM

    def norm_mod(x_ref):
        h = _rms(x_ref[...], g_ref[...]) * (1.0 + mod_ref[1:2, :]) + mod_ref[0:1, :]
        h_sc[...] = h.astype(BF16)

    @pl.when((j == 0) & is_ctx)
    def _():
        norm_mod(xc_ref)

    @pl.when((j == 0) & jnp.logical_not(is_ctx))
    def _():
        norm_mod(xl_ref)

    acc = _dot(h_sc[...], w_ref[...])
    z_ref[...] = acc.astype(BF16)

    def rows(b):
        return slice(b * SEQ, (b + 1) * SEQ)

    def cols(c, width):
        return slice(c % INPROJ_TN, c % INPROJ_TN + width)

    @pl.when(is_ctx & (j == COL_NA_K // INPROJ_TN))
    def _():
        for b in range(INPROJ_TM // SEQ):
            cna_ref[b, 0] = acc[rows(b), cols(COL_NA_K, NA_WIDTH)]
            cna_ref[b, 1] = acc[rows(b), cols(COL_NA_V, NA_WIDTH)]

    @pl.when(is_ctx & (j == COL_G_K // INPROJ_TN))
    def _():
        for b in range(INPROJ_TM // SEQ):
            cgq_ref[b, 0] = acc[rows(b), cols(COL_G_K, GQA_KV_WIDTH)]
            cgq_ref[b, 1] = acc[rows(b), cols(COL_G_V, GQA_KV_WIDTH)]

    @pl.when(j == COL_U // INPROJ_TN)
    def _():
        u_ref[...] = acc[:, cols(COL_U, S5_CH)]


def _token_arrays(x):
    return (x, x, N_CTX_TOK) if not isinstance(x, tuple) else (x[0], x[1], 0)


def _inproj(x, g, mod, w, l):
    tm, tn = INPROJ_TM, INPROJ_TN
    nb = tm // SEQ
    last_ctx = N_CTX_TOK // tm - 1
    x_ctx, x_lat, lat0 = _token_arrays(x)
    return pl.pallas_call(
        _inproj_kernel,
        out_shape=(jax.ShapeDtypeStruct((N_TOK, Z_COLS), BF16),
                   jax.ShapeDtypeStruct((N_TOK, S5_CH), F32),
                   jax.ShapeDtypeStruct((BATCH, 2, SEQ, NA_WIDTH), F32),
                   jax.ShapeDtypeStruct((BATCH, 2, SEQ, GQA_KV_WIDTH), F32)),
        grid=(N_TOK // tm, Z_COLS // tn),
        in_specs=[
            pl.BlockSpec((tm, D_MODEL), lambda i, j: (jnp.minimum(i, last_ctx), 0)),
            pl.BlockSpec((tm, D_MODEL), lambda i, j: (jnp.maximum(i - last_ctx - 1, 0) + lat0 // tm, 0)),
            pl.BlockSpec((1, D_MODEL), lambda i, j: (0, 0)),
            pl.BlockSpec((None, None, 6, D_MODEL), lambda i, j: (l, _cond_index(i, tm), 0, 0)),
            pl.BlockSpec((None, D_MODEL, tn), lambda i, j: (l, 0, j)),
        ],
        out_specs=(pl.BlockSpec((tm, tn), lambda i, j: (i, j)),
                   pl.BlockSpec((tm, S5_CH), lambda i, j: (i, 0)),
                   pl.BlockSpec((nb, 2, SEQ, NA_WIDTH), lambda i, j: (jnp.minimum(i, last_ctx), 0, 0, 0)),
                   pl.BlockSpec((nb, 2, SEQ, GQA_KV_WIDTH), lambda i, j: (jnp.minimum(i, last_ctx), 0, 0, 0))),
        scratch_shapes=[pltpu.VMEM((tm, D_MODEL), BF16)],
        compiler_params=pltpu.CompilerParams(
            dimension_semantics=("arbitrary", "arbitrary"), vmem_limit_bytes=VMEM_LIMIT),
        name="inproj",
    )(x_ctx, x_lat, g, mod, w)


def _block_diag_heads(x, nrep):
    t = x.shape[0]
    rows = lax.broadcasted_iota(jnp.int32, (nrep * t, nrep * HEAD_DIM), 0) // t
    lanes = lax.broadcasted_iota(jnp.int32, (nrep * t, nrep * HEAD_DIM), 1) // HEAD_DIM
    return jnp.where(rows == lanes, jnp.concatenate([x] * nrep, axis=0), jnp.zeros((), x.dtype))


def _per_head(cols, shape):
    head = lax.broadcasted_iota(jnp.int32, shape, 1) // HEAD_DIM
    out = cols[-1]
    for h in range(len(cols) - 2, -1, -1):
        out = jnp.where(head == h, cols[h], out)
    return out


def _attend_heads(q, k, v, sinks):
    nh = q.shape[1] // HEAD_DIM
    t = k.shape[0]
    s = _dot_nt(q * HEAD_DIM ** -0.5, _block_diag_heads(k, nh))
    ps, ds = [], []
    for h in range(nh):
        sh = s[:, h * t:(h + 1) * t]
        m = jnp.max(sh, axis=-1, keepdims=True)
        if sinks is not None:
            m = jnp.maximum(m, sinks[h])
        p = jnp.exp(sh - m)
        d = jnp.sum(p, axis=-1, keepdims=True)
        if sinks is not None:
            d = d + jnp.exp(sinks[h] - m)
        ps.append(p.astype(BF16))
        ds.append(d)
    o = _dot(jnp.concatenate(ps, axis=1), _block_diag_heads(v, nh))
    return o / _per_head(ds, o.shape)


def _ctx_attn_kernel(sink_ref, na_ref, gq_ref, gkv_ref, oa_ref, ob_ref):
    hb = CTX_HEADS_PER_DOT
    w = hb * HEAD_DIM
    na = na_ref[...]
    gq = gq_ref[...]
    gkv = gkv_ref[...]
    for g in range(NA_HEADS // hb):
        o = _attend_heads(na[:, COL_NA_Q + g * w:COL_NA_Q + (g + 1) * w],
                          na[:, COL_NA_K + g * w:COL_NA_K + (g + 1) * w],
                          na[:, COL_NA_V + g * w:COL_NA_V + (g + 1) * w], None)
        oa_ref[:, g * w:(g + 1) * w] = o.astype(BF16)
    for g in range(GQA_Q_HEADS // hb):
        kv_heads = [(g * hb + h) // GQA_GROUP for h in range(hb)]
        k = jnp.concatenate([gkv[:, kh * HEAD_DIM:(kh + 1) * HEAD_DIM] for kh in kv_heads], axis=1)
        v = jnp.concatenate([gkv[:, GQA_KV_WIDTH + kh * HEAD_DIM:GQA_KV_WIDTH + (kh + 1) * HEAD_DIM]
                             for kh in kv_heads], axis=1)
        o = _attend_heads(gq[:, g * w:(g + 1) * w], k, v, [sink_ref[g * hb + h] for h in range(hb)])
        ob_ref[:, g * w:(g + 1) * w] = o.astype(BF16)


def _ctx_attn(z, sink):
    out = jax.ShapeDtypeStruct((N_CTX_TOK, NA_WIDTH), BF16)
    return pl.pallas_call(
        _ctx_attn_kernel,
        out_shape=(out, out),
        grid=(BATCH,),
        in_specs=[
            pl.BlockSpec(memory_space=pltpu.SMEM),
            pl.BlockSpec((SEQ, 3 * NA_WIDTH), lambda b: (b, 0)),
            pl.BlockSpec((SEQ, GQA_Q_WIDTH), lambda b: (b, COL_G_Q // GQA_Q_WIDTH)),
            pl.BlockSpec((SEQ, 2 * GQA_KV_WIDTH), lambda b: (b, COL_G_K // (2 * GQA_KV_WIDTH))),
        ],
        out_specs=(pl.BlockSpec((SEQ, NA_WIDTH), lambda b: (b, 0)),
                   pl.BlockSpec((SEQ, GQA_Q_WIDTH), lambda b: (b, 0))),
        compiler_params=pltpu.CompilerParams(
            dimension_semantics=("parallel",), vmem_limit_bytes=VMEM_LIMIT),
        name="ctx_attn",
    )(sink, z, z, z)


def _na_bias_tables(rpb):
    n_dc = 2 * NA_WIN_W - 1
    qc = np.arange(GRID_W)
    kc = np.arange(GRID_W)
    cstart = np.clip(qc - NA_WIN_W // 2, 0, GRID_W - NA_WIN_W)
    col_ok = (kc[None, :] >= cstart[:, None]) & (kc[None, :] < cstart[:, None] + NA_WIN_W)
    dc = np.clip(kc[None, :] - qc[:, None] + (NA_WIN_W - 1), 0, n_dc - 1)
    onehot = ((np.arange(n_dc)[:, None, None] == dc[None]) & col_ok[None]).astype(np.float32)
    band = jnp.einsum("hrd,dqk->hrqk", rpb.astype(F32), onehot, precision=lax.Precision.HIGHEST)
    band = jnp.where(col_ok, band, NEG_INF)
    band = jnp.pad(band, ((0, 0), (1, 1), (0, 0), (0, 0)), constant_values=NEG_INF)
    return jnp.concatenate([band[:, :-1], band[:, 1:]], axis=-1)


def _na_lat_kernel(q_ref, k_ref, v_ref, ck_ref, cv_ref, bias_ref, rowmask_ref, o_ref):
    i = pl.program_id(1)
    scale = HEAD_DIM ** -0.5
    nk = NA_KROWS * GRID_W
    kh = min(NA_WIN_H, GRID_ROWS)
    ks = jnp.clip(i * NA_QROWS - NA_WIN_H // 2, 0, GRID_ROWS - NA_KROWS)
    start = pl.multiple_of(ks * GRID_W, GRID_W)
    q = q_ref[...].astype(BF16)
    kw = k_ref[pl.ds(start, nk), :].astype(BF16)
    vw = v_ref[pl.ds(start, nk), :].astype(BF16)
    ck = ck_ref[...].astype(BF16)
    cv = cv_ref[...].astype(BF16)
    planes = []
    for qr in range(NA_QROWS):
        r = i * NA_QROWS + qr
        planes.append([jnp.clip(ks + 2 * m - r + NA_WIN_H, 0, 2 * NA_WIN_H - 1) for m in range(NA_KROWS // 2)])
    hb = NA_HEADS_PER_DOT
    w = hb * HEAD_DIM
    qrow = lax.broadcasted_iota(jnp.int32, (NA_QROWS * GRID_W, 128), 0) // GRID_W
    qsel = (lax.broadcasted_iota(jnp.int32, (NA_QROWS * GRID_W, 128), 1) == qrow).astype(BF16)
    rowmask = rowmask_ref[...]
    for g in range(NA_HEADS // hb):
        lanes = slice(g * w, (g + 1) * w)
        qg = jnp.concatenate([q[:, lanes] * scale, qsel], axis=1)
        kg = jnp.concatenate([_block_diag_heads(kw[:, lanes], hb), rowmask], axis=1)
        s_loc = _dot_nt(qg, kg)
        s_ctx = _dot_nt(q[:, lanes] * scale, _block_diag_heads(ck[:, lanes], hb))
        p_locs, p_ctxs, ds = [], [], []
        for hh in range(hb):
            bias = jnp.concatenate(
                [jnp.concatenate([bias_ref[g * hb + hh, plane] for plane in row], axis=1) for row in planes], axis=0)
            sl = s_loc[:, hh * nk:(hh + 1) * nk] + bias
            sc = s_ctx[:, hh * PAST_LEN:(hh + 1) * PAST_LEN]
            m = jnp.maximum(jnp.max(sl, axis=-1, keepdims=True), jnp.max(sc, axis=-1, keepdims=True))
            p_loc = jnp.exp(sl - m)
            p_ctx = jnp.exp(sc - m)
            ds.append(jnp.sum(p_loc, axis=-1, keepdims=True) + jnp.sum(p_ctx, axis=-1, keepdims=True))
            p_locs.append(p_loc.astype(BF16))
            p_ctxs.append(p_ctx.astype(BF16))
        o = (_dot(jnp.concatenate(p_locs, axis=1), _block_diag_heads(vw[:, lanes], hb))
             + _dot(jnp.concatenate(p_ctxs, axis=1), _block_diag_heads(cv[:, lanes], hb)))
        o_ref[:, lanes] = (o / _per_head(ds, o.shape)).astype(BF16)


def _na_row_masks():
    kh = min(NA_WIN_H, GRID_ROWS)
    nsteps = GRID_ROWS // NA_QROWS
    out = np.zeros((3, NA_HEADS_PER_DOT, NA_KROWS, GRID_W, 128), np.float32)
    for pat, i in enumerate((0, 1, nsteps - 1)):
        ks = min(max(i * NA_QROWS - NA_WIN_H // 2, 0), GRID_ROWS - NA_KROWS)
        for qr in range(NA_QROWS):
            r = i * NA_QROWS + qr
            st = min(max(r - kh // 2, 0), GRID_ROWS - kh)
            for kr in range(NA_KROWS):
                if not st <= ks + kr < st + kh:
                    out[pat, :, kr, :, qr] = NEG_INF
    return jnp.asarray(out.reshape(3, NA_HEADS_PER_DOT * NA_KROWS * GRID_W, 128), BF16)


def _na_lat(z, cache_na, bias, l):
    tq = NA_QROWS * GRID_W
    nsteps = GRID_ROWS // NA_QROWS
    lat_blk = N_CTX_TOK // DEC_SEQ
    return pl.pallas_call(
        _na_lat_kernel,
        out_shape=jax.ShapeDtypeStruct((N_LAT_TOK, NA_WIDTH), BF16),
        grid=(DEC_BATCH, nsteps),
        in_specs=[
            pl.BlockSpec((tq, NA_WIDTH), lambda b, i: (N_CTX_TOK // tq + b * nsteps + i, COL_NA_Q // NA_WIDTH)),
            pl.BlockSpec((DEC_SEQ, NA_WIDTH), lambda b, i: (lat_blk + b, COL_NA_K // NA_WIDTH)),
            pl.BlockSpec((DEC_SEQ, NA_WIDTH), lambda b, i: (lat_blk + b, COL_NA_V // NA_WIDTH)),
            pl.BlockSpec((None, None, None, PAST_LEN, NA_WIDTH), lambda b, i: (b, l, 0, 0, 0)),
            pl.BlockSpec((None, None, None, PAST_LEN, NA_WIDTH), lambda b, i: (b, l, 1, 0, 0)),
            pl.BlockSpec((NA_HEADS, 2 * NA_WIN_H, GRID_W, 2 * GRID_W), lambda b, i: (0, 0, 0, 0)),
            pl.BlockSpec((None, NA_HEADS_PER_DOT * NA_KROWS * GRID_W, 128),
                         lambda b, i: (jnp.where(i == 0, 0, jnp.where(i == nsteps - 1, 2, 1)), 0, 0)),
        ],
        out_specs=pl.BlockSpec((tq, NA_WIDTH), lambda b, i: (b * nsteps + i, 0)),
        compiler_params=pltpu.CompilerParams(
            dimension_semantics=("parallel", "arbitrary"), vmem_limit_bytes=VMEM_LIMIT),
        name="na_lat",
    )(z, z, z, cache_na, cache_na, bias, _na_row_masks())


def _rope_tables():
    nf = HEAD_DIM // 4
    t = jnp.arange(DEC_SEQ)
    pos = jnp.stack([t // GRID_W, t % GRID_W], axis=-1).astype(F32)
    inv = ROPE_THETA ** (-jnp.arange(nf, dtype=F32) / nf)
    ang = pos[:, :, None] * inv
    cos = jnp.cos(ang)
    sin = jnp.sin(ang)
    cos_d = jnp.stack([cos, cos], axis=2).reshape(DEC_SEQ, HEAD_DIM)
    sin_d = jnp.stack([-sin, sin], axis=2).reshape(DEC_SEQ, HEAD_DIM)
    return cos_d, sin_d


def _rope(x, cos, sin_signed):
    n = x.shape[-1]
    nf = HEAD_DIM // 4
    lane = lax.broadcasted_iota(jnp.int32, x.shape, 1)
    first_half = (lane // nf) % 2 == 0
    partner = jnp.where(first_half, pltpu.roll(x, n - nf, 1), pltpu.roll(x, nf, 1))
    return x * cos + partner * sin_signed


def _swa_lat_kernel(sink_ref, q_ref, kv_ref, cq_ref, sq_ref, ckt_ref, skt_ref, ck_ref, cv_ref, o_ref, k_sc, v_sc):
    n = pl.program_id(1)
    scale = HEAD_DIM ** -0.5
    nwin = 3 * SWA_BLOCK

    @pl.when(n == 0)
    def _():
        kv = kv_ref[...]
        k_sc[...] = _rope(kv[:, :GQA_KV_WIDTH].astype(F32), ckt_ref[...], skt_ref[...]).astype(BF16)
        v_sc[...] = kv[:, GQA_KV_WIDTH:].astype(BF16)

    q = (_rope(q_ref[...].astype(F32), cq_ref[...], sq_ref[...]) * scale).astype(BF16)
    start = pl.multiple_of(jnp.clip((n - 1) * SWA_BLOCK, 0, DEC_SEQ - nwin), SWA_BLOCK)
    kw = k_sc[pl.ds(start, nwin), :]
    vw = v_sc[pl.ds(start, nwin), :]
    ck = ck_ref[...].astype(BF16)
    cv = cv_ref[...].astype(BF16)
    rows = GQA_GROUP * SWA_BLOCK
    row = lax.broadcasted_iota(jnp.int32, (rows, nwin), 0)
    col = lax.broadcasted_iota(jnp.int32, (rows, nwin), 1)
    qpos = n * SWA_BLOCK + row % SWA_BLOCK
    kpos = start + col
    ok = jnp.abs(qpos - kpos) <= SWA_WINDOW
    grp = lax.broadcasted_iota(jnp.int32, (rows, 1), 0) // SWA_BLOCK
    for kh in range(GQA_KV_HEADS):
        sl = slice(kh * HEAD_DIM, (kh + 1) * HEAD_DIM)
        q4 = jnp.concatenate(
            [q[:, (kh * GQA_GROUP + g) * HEAD_DIM:(kh * GQA_GROUP + g + 1) * HEAD_DIM] for g in range(GQA_GROUP)],
            axis=0)
        sink = jnp.zeros((rows, 1), F32)
        for g in range(GQA_GROUP):
            sink = jnp.where(grp == g, sink_ref[kh * GQA_GROUP + g], sink)
        s_loc = jnp.where(ok, _dot_nt(q4, kw[:, sl]), NEG_INF)
        s_ctx = _dot_nt(q4, ck[:, sl])
        m = jnp.maximum(jnp.maximum(jnp.max(s_loc, axis=-1, keepdims=True),
                                    jnp.max(s_ctx, axis=-1, keepdims=True)), sink)
        p_loc = jnp.exp(s_loc - m)
        p_ctx = jnp.exp(s_ctx - m)
        d = (jnp.sum(p_loc, axis=-1, keepdims=True) + jnp.sum(p_ctx, axis=-1, keepdims=True)
             + jnp.exp(sink - m))
        o4 = (_dot(p_loc.astype(BF16), vw[:, sl]) + _dot(p_ctx.astype(BF16), cv[:, sl])) / d
        for g in range(GQA_GROUP):
            h = kh * GQA_GROUP + g
            o_ref[:, h * HEAD_DIM:(h + 1) * HEAD_DIM] = o4[g * SWA_BLOCK:(g + 1) * SWA_BLOCK].astype(BF16)


def _swa_lat(z, cache_gqa, sink, cos_q, sin_q, cos_k, sin_k, l):
    nb = DEC_SEQ // SWA_BLOCK
    lat_blk = N_CTX_TOK // DEC_SEQ
    return pl.pallas_call(
        _swa_lat_kernel,
        out_shape=jax.ShapeDtypeStruct((N_LAT_TOK, GQA_Q_WIDTH), BF16),
        grid=(DEC_BATCH, nb),
        in_specs=[
            pl.BlockSpec(memory_space=pltpu.SMEM),
            pl.BlockSpec((SWA_BLOCK, GQA_Q_WIDTH),
                         lambda b, n: (N_CTX_TOK // SWA_BLOCK + b * nb + n, COL_G_Q // GQA_Q_WIDTH)),
            pl.BlockSpec((DEC_SEQ, 2 * GQA_KV_WIDTH), lambda b, n: (lat_blk + b, COL_G_K // (2 * GQA_KV_WIDTH))),
            pl.BlockSpec((SWA_BLOCK, GQA_Q_WIDTH), lambda b, n: (n, 0)),
            pl.BlockSpec((SWA_BLOCK, GQA_Q_WIDTH), lambda b, n: (n, 0)),
            pl.BlockSpec((DEC_SEQ, GQA_KV_WIDTH), lambda b, n: (0, 0)),
            pl.BlockSpec((DEC_SEQ, GQA_KV_WIDTH), lambda b, n: (0, 0)),
            pl.BlockSpec((None, None, None, PAST_LEN, GQA_KV_WIDTH), lambda b, n: (b, l, 0, 0, 0)),
            pl.BlockSpec((None, None, None, PAST_LEN, GQA_KV_WIDTH), lambda b, n: (b, l, 1, 0, 0)),
        ],
        out_specs=pl.BlockSpec((SWA_BLOCK, GQA_Q_WIDTH), lambda b, n: (b * nb + n, 0)),
        scratch_shapes=[pltpu.VMEM((DEC_SEQ, GQA_KV_WIDTH), BF16), pltpu.VMEM((DEC_SEQ, GQA_KV_WIDTH), BF16)],
        compiler_params=pltpu.CompilerParams(
            dimension_semantics=("parallel", "arbitrary"), vmem_limit_bytes=VMEM_LIMIT),
        name="swa_lat",
    )(sink, z, z, cos_q, sin_q, cos_k, sin_k, cache_gqa, cache_gqa)


S5_OP_KEYS = ("kt", "bt_re", "bt_im", "ct_re", "ct_im", "pin_re", "pin_im", "po_re", "po_im", "apr", "api")


def _s5_operators(a_re, a_im, log_dt, b_re, b_im, c_re, c_im):
    T, G, N, C = S5_T, S5_GROUPS, S5_STATE, S5_GROUP_CH
    tau = jnp.arange(T + 1, dtype=F32)
    out = {k: [] for k in S5_OP_KEYS}
    for d in range(2):
        A = lax.complex(a_re[d].astype(F32), a_im[d].astype(F32))
        dt = jnp.exp(log_dt[d].astype(F32))[:, None]
        a_bar = jnp.exp(A * dt)
        pw = jnp.exp((A * dt)[None] * tau[:, None, None])
        b_bar = ((a_bar - 1.0) / A)[..., None] * lax.complex(b_re[d].astype(F32), b_im[d].astype(F32))
        c_mat = lax.complex(c_re[d].astype(F32), c_im[d].astype(F32))
        kern = jnp.einsum("gon,tgn,gni->gtoi", c_mat, pw[:T], b_bar, precision=lax.Precision.HIGHEST).real
        kern = lax.optimization_barrier(kern).transpose(0, 3, 1, 2)
        if d == 0:
            p_in = pw[:T][::-1]
            p_out = pw[1:T + 1]
        else:
            p_in = pw[:T]
            p_out = pw[1:T + 1][::-1]
            kern = kern[:, :, ::-1]
        bt, ct, po = b_bar.transpose(0, 2, 1), c_mat.transpose(0, 2, 1), p_out.transpose(1, 2, 0)
        pw2 = jnp.exp((A * dt)[None] * (T * 2.0 ** jnp.arange(S5_NPOW, dtype=F32))[:, None, None])
        vals = (kern.reshape(G, C, T * C), bt.real, bt.imag, ct.real, ct.imag,
                p_in.real.reshape(T, S5_PAIRS, 1, 2 * N), p_in.imag.reshape(T, S5_PAIRS, 1, 2 * N), po.real, po.imag,
                pw2.real.reshape(S5_NPOW, G * N), pw2.imag.reshape(S5_NPOW, G * N))
        for k, v in zip(S5_OP_KEYS, vals):
            out[k].append(v)
    return {k: jnp.stack(v) for k, v in out.items()}


def _s5_expand(kt_ref, btr_ref, bti_ref, ctr_ref, cti_ref, pir_ref, pii_ref, por_ref, poi_ref,
               bs_sc, m_sc, cre_sc, cim_sc):
    T, C, N = S5_T, S5_GROUP_CH, S5_STATE
    pc, half, width = 2 * C, 2 * N, 2 * T * C
    lane = lax.broadcasted_iota(jnp.int32, (pc, width), 1)
    zc, zn, zo = jnp.zeros((C, C), F32), jnp.zeros((C, N), F32), jnp.zeros((N, C), F32)

    def own_half(x, z, g):
        return [x, z] if g == 0 else [z, x]

    def expand_pair(p, d):
        k0, bx_re, bx_im, co_re, co_im = [], [], [], [], []
        for g in range(2):
            gl = 2 * p + g
            ktg = kt_ref[d, gl]
            k0.append(jnp.concatenate(
                [blk for lag in range(T) for blk in own_half(ktg[:, lag * C:(lag + 1) * C], zc, g)], axis=1))
            bx_re.append(jnp.concatenate(own_half(btr_ref[d, gl], zn, g), axis=1))
            bx_im.append(jnp.concatenate(own_half(bti_ref[d, gl], zn, g), axis=1))
            cr, ci = ctr_ref[d, gl], cti_ref[d, gl]
            pr_all, pi_all = por_ref[d, gl], poi_ref[d, gl]
            re_p, im_p = [], []
            for j in range(T):
                pr, pi = pr_all[:, j:j + 1], pi_all[:, j:j + 1]
                re_p += own_half(cr * pr - ci * pi, zo, g)
                im_p += own_half(cr * pi + ci * pr, zo, g)
            co_re.append(jnp.concatenate(re_p, axis=1))
            co_im.append(jnp.concatenate(im_p, axis=1))
        k0 = jnp.concatenate(k0, axis=0)
        bx_re, bx_im = jnp.concatenate(bx_re, axis=0), jnp.concatenate(bx_im, axis=0)
        for s in range(T):
            if d == 0:
                sh = pc * s
                blk = jnp.where(lane >= sh, pltpu.roll(k0, sh, 1), 0.0) if sh else k0
            else:
                sh = pc * (T - 1 - s)
                blk = jnp.where(lane < width - sh, pltpu.roll(k0, width - sh, 1), 0.0) if sh else k0
            m_sc[d, p, s * pc:(s + 1) * pc, :] = blk.astype(BF16)
            pr, pi = pir_ref[d, s, p], pii_ref[d, s, p]
            bs_sc[d, p, s * pc:(s + 1) * pc, :] = jnp.concatenate(
                [pr * bx_re - pi * bx_im, pr * bx_im + pi * bx_re], axis=1).astype(BF16)
        cre_sc[d, p] = jnp.concatenate(co_re, axis=0).astype(BF16)
        cim_sc[d, p] = (-jnp.concatenate(co_im, axis=0)).astype(BF16)

    for d in range(2):
        lax.fori_loop(0, S5_OCT_PAIRS, lambda p, c, d=d: (expand_pair(p, d), c)[1], 0)


def _s5_kernel(u_ref, kt_ref, btr_ref, bti_ref, ctr_ref, cti_ref, pir_ref, pii_ref, por_ref, poi_ref, apr_ref,
               api_ref, h0_ref, y_ref, fin_ref, fin_sc, bs_sc, m_sc, cre_sc, cim_sc):
    t = pl.program_id(1)
    nctx = N_CTX_TOK // S5_BLOCK_TOK

    @pl.when(t == 0)
    def _():
        _s5_expand(kt_ref, btr_ref, bti_ref, ctr_ref, cti_ref, pir_ref, pii_ref, por_ref, poi_ref,
                   bs_sc, m_sc, cre_sc, cim_sc)

    refs = (u_ref, bs_sc, m_sc, cre_sc, cim_sc, apr_ref, api_ref, h0_ref, y_ref, fin_ref, fin_sc)

    @pl.when(t < nctx)
    def _():
        _s5_block(*refs, t, kseq=SEQ // S5_T, has_h0=False)

    @pl.when(t >= nctx)
    def _():
        _s5_block(*refs, t - nctx, kseq=DEC_SEQ // S5_T, has_h0=True)


def _s5_block(u_ref, bs_ref, m_ref, cre_ref, cim_ref, apr_ref, api_ref, h0_ref, y_ref, fin_ref, fin_sc, t, *,
              kseq, has_h0):
    rows, half, pc = S5_ROWS, 2 * S5_STATE, 2 * S5_GROUP_CH
    k = lax.broadcasted_iota(jnp.int32, (rows, half), 0) % kseq
    shifts = [1 << i for i in range(kseq.bit_length() - 1)]
    xs = [u_ref[pl.ds(s, rows, stride=S5_T), :].astype(BF16) for s in range(S5_T)]
    ys = []
    for p in range(S5_OCT_PAIRS):
        lanes = slice(p * half, (p + 1) * half)
        xp = jnp.concatenate([x[:, p * pc:(p + 1) * pc] for x in xs], axis=1)
        acc = None
        for d in range(2):
            inc = _dot(xp, bs_ref[d, p])
            sr, si = inc[:, :half], inc[:, half:]
            if has_h0:
                h0r = h0_ref[d, 0, t, :, lanes]
                h0i = h0_ref[d, 1, t, :, lanes]
                ar, ai = apr_ref[d, 0:1, lanes], api_ref[d, 0:1, lanes]
                first = (k == 0) if d == 0 else (k == kseq - 1)
                sr = sr + jnp.where(first, ar * h0r - ai * h0i, 0.0)
                si = si + jnp.where(first, ar * h0i + ai * h0r, 0.0)
            else:
                h0r = h0i = 0.0
            for i, sh in enumerate(shifts):
                ar, ai = apr_ref[d, i:i + 1, lanes], api_ref[d, i:i + 1, lanes]
                ok = (k >= sh) if d == 0 else (k < kseq - sh)
                amt = sh if d == 0 else rows - sh
                rr = jnp.where(ok, pltpu.roll(sr, amt, 0), 0.0)
                ri = jnp.where(ok, pltpu.roll(si, amt, 0), 0.0)
                sr, si = sr + ar * rr - ai * ri, si + ar * ri + ai * rr
            inner = (k >= 1) if d == 0 else (k < kseq - 1)
            amt = 1 if d == 0 else rows - 1
            hr = jnp.where(inner, pltpu.roll(sr, amt, 0), h0r)
            hi = jnp.where(inner, pltpu.roll(si, amt, 0), h0i)
            yd = (_dot(xp, m_ref[d, p]) + _dot(hr.astype(BF16), cre_ref[d, p])
                  + _dot(hi.astype(BF16), cim_ref[d, p]))
            acc = yd if acc is None else acc + yd
            if not has_h0:
                last = kseq - 1 if d == 0 else 0
                fin_sc[0] = sr
                fin_sc[1] = si
                fin_ref[d, 0, :, lanes] = fin_sc[0, pl.ds(last, rows // kseq, stride=kseq), :]
                fin_ref[d, 1, :, lanes] = fin_sc[1, pl.ds(last, rows // kseq, stride=kseq), :]
        ys.append(acc)
    for j in range(S5_T):
        y_ref[pl.ds(j, rows, stride=S5_T), :] = jnp.concatenate([y[:, j * pc:(j + 1) * pc] for y in ys], axis=1)


def _s5_scan(u, ops, h0, l):
    nblk, nctx = N_TOK // S5_BLOCK_TOK, N_CTX_TOK // S5_BLOCK_TOK
    noct = S5_PAIRS // S5_OCT_PAIRS
    half, gn = 2 * S5_STATE, S5_GROUPS * S5_STATE
    pw = 2 * S5_T * S5_GROUP_CH
    nseq = S5_BLOCK_TOK // SEQ
    ngrp = 2 * S5_OCT_PAIRS
    C, N, T = S5_GROUP_CH, S5_STATE, S5_T
    gspec = lambda r, c: pl.BlockSpec((None, 2, ngrp, r, c), lambda q, t: (l, 0, q, 0, 0))
    lspec = lambda r: pl.BlockSpec((None, 2, r, S5_OCT_PAIRS * half), lambda q, t: (l, 0, 0, q))
    pspec = pl.BlockSpec((None, 2, T, S5_OCT_PAIRS, 1, half), lambda q, t: (l, 0, 0, q, 0, 0))
    return pl.pallas_call(
        _s5_kernel,
        grid=(noct, nblk),
        in_specs=[
            pl.BlockSpec((S5_BLOCK_TOK, 128), lambda q, t: (t, q)),
            gspec(C, T * C), gspec(C, N), gspec(C, N), gspec(N, C), gspec(N, C), pspec, pspec,
            gspec(N, T), gspec(N, T), lspec(S5_NPOW), lspec(S5_NPOW),
            pl.BlockSpec((2, 2, DEC_BATCH, 1, S5_OCT_PAIRS * half), lambda q, t: (0, 0, 0, 0, q)),
        ],
        out_shape=(jax.ShapeDtypeStruct((N_TOK, S5_CH), F32), jax.ShapeDtypeStruct((2, 2, BATCH, gn), F32)),
        out_specs=(pl.BlockSpec((S5_BLOCK_TOK, 128), lambda q, t: (t, q)),
                   pl.BlockSpec((2, 2, nseq, S5_OCT_PAIRS * half),
                                lambda q, t: (0, 0, jnp.minimum(t, nctx - 1), q))),
        scratch_shapes=[pltpu.VMEM((2, S5_ROWS, half), F32),
                        pltpu.VMEM((2, S5_OCT_PAIRS, pw, 2 * half), BF16), pltpu.VMEM((2, S5_OCT_PAIRS, pw, pw), BF16),
                        pltpu.VMEM((2, S5_OCT_PAIRS, half, pw), BF16), pltpu.VMEM((2, S5_OCT_PAIRS, half, pw), BF16)],
        compiler_params=pltpu.CompilerParams(
            dimension_semantics=("parallel", "arbitrary"), vmem_limit_bytes=VMEM_LIMIT),
        name="s5_scan",
    )(u, *[ops[k] for k in S5_OP_KEYS], h0.reshape(2, 2, DEC_BATCH, 1, gn))


def _route(logits):
    lane = lax.broadcasted_iota(jnp.int32, logits.shape, 1)
    big = jnp.int32(1 << 20)
    is_g = (lane >= MOE_EXPERTS) & (lane < MOE_EXPERTS + MOE_GROUPS)
    lg = jnp.where(is_g, logits, NEG_INF)
    gmax = jnp.max(lg, axis=-1, keepdims=True)
    gsel = jnp.min(jnp.where(is_g & (lg == gmax), lane, big), axis=-1, keepdims=True) - MOE_EXPERTS
    p_group = 1.0 / jnp.sum(jnp.where(is_g, jnp.exp(lg - gmax), 0.0), axis=-1, keepdims=True)
    in_grp = (lane < MOE_EXPERTS) & (lane // MOE_EXPERTS_PER_GROUP == gsel)
    le = jnp.where(in_grp, logits, NEG_INF)
    v1 = jnp.max(le, axis=-1, keepdims=True)
    i1 = jnp.min(jnp.where(in_grp & (le == v1), lane, big), axis=-1, keepdims=True)
    rest = in_grp & (lane != i1)
    le2 = jnp.where(rest, logits, NEG_INF)
    v2 = jnp.max(le2, axis=-1, keepdims=True)
    i2 = jnp.min(jnp.where(rest & (le2 == v2), lane, big), axis=-1, keepdims=True)
    e2 = jnp.exp(v2 - v1)
    w1 = 1.0 / (1.0 + e2)
    w2 = e2 / (1.0 + e2)
    comb = jnp.where(lane == i1, w1 * p_group, 0.0) + jnp.where(lane == i2, w2 * p_group, 0.0)
    a = jnp.minimum(i1, i2) - gsel * MOE_EXPERTS_PER_GROUP
    b = jnp.maximum(i1, i2) - gsel * MOE_EXPERTS_PER_GROUP
    pair = jnp.where(a == 0, b - 1, jnp.where(a == 1, jnp.where(b == 3, 3, 4), 5))
    cls = gsel * MOE_PAIRS + pair
    return jnp.where(lane == MOE_EXPERTS, cls.astype(F32), comb)


def _merge_kernel(xc_ref, xl_ref, oac_ref, oal_ref, obc_ref, obl_ref, u_ref, yc_ref, gates_ref, mod_ref, d_ref,
                  wglu_ref, wa_ref, wb_ref, wc_ref, wout_ref, g2_ref, wrh_ref, wrl_ref, br_ref, x1_ref, hx_ref,
                  *, tm):
    is_ctx = pl.program_id(0) < N_CTX_TOK // tm
    x = jnp.where(is_ctx, xc_ref[...], xl_ref[...])
    oa = jnp.where(is_ctx, oac_ref[...], oal_ref[...])
    ob = jnp.where(is_ctx, obc_ref[...], obl_ref[...])
    y = u_ref[...] * d_ref[...] + yc_ref[...]
    y = y * (0.5 * (1.0 + jnp.tanh(math.sqrt(2.0 / math.pi) * (y + 0.044715 * (y * y * y)))))
    oc = y * jax.nn.sigmoid(_dot(y.astype(BF16), wglu_ref[...]))
    gate = jax.nn.sigmoid(gates_ref[...].astype(F32))
    merged = (gate[:, :D_MODEL] * _dot(oa, wa_ref[...])
              + gate[:, D_MODEL:2 * D_MODEL] * _dot(ob, wb_ref[...])
              + gate[:, 2 * D_MODEL:] * _dot(oc.astype(BF16), wc_ref[...]))
    x1 = x + mod_ref[2:3, :] * _dot(merged.astype(BF16), wout_ref[...])
    x1_ref[...] = x1
    h2 = _rms(x1, g2_ref[...]) * (1.0 + mod_ref[4:5, :]) + mod_ref[3:4, :]
    h_hi, h_lo = _split_bf16(h2)
    logits = _dot(h_hi, wrh_ref[...]) + _dot(h_lo, wrh_ref[...]) + _dot(h_hi, wrl_ref[...]) + br_ref[...]
    half = D_MODEL // 2
    hx_ref[:, :half] = _pack_pairs(h2)
    hx_ref[:, half:] = pltpu.bitcast(_route(logits), jnp.uint32)


def _merge(x, oa_c, oa_l, ob_c, ob_l, u, yc, z, mod, d_s5, wglu, wa, wb, wc, wout, g2, wr_hi, wr_lo, br, l):
    tm = 512
    nctx = N_CTX_TOK // tm
    full = lambda r, c: pl.BlockSpec((r, c), lambda i: (0, 0))
    layer = lambda r, c: pl.BlockSpec((None, r, c), lambda i: (l, 0, 0))
    ctx_blk = lambda w: pl.BlockSpec((tm, w), lambda i: (jnp.minimum(i, nctx - 1), 0))
    lat_blk = lambda w, first=0: pl.BlockSpec((tm, w), lambda i: (jnp.maximum(i - nctx, 0) + first // tm, 0))
    x_ctx, x_lat, lat0 = _token_arrays(x)
    return pl.pallas_call(
        functools.partial(_merge_kernel, tm=tm),
        out_shape=(jax.ShapeDtypeStruct((N_TOK, D_MODEL), F32),
                   jax.ShapeDtypeStruct((N_TOK, MOE_ROW_WORDS), jnp.uint32)),
        grid=(N_TOK // tm,),
        in_specs=[
            ctx_blk(D_MODEL), lat_blk(D_MODEL, lat0),
            ctx_blk(NA_WIDTH), lat_blk(NA_WIDTH), ctx_blk(GQA_Q_WIDTH), lat_blk(GQA_Q_WIDTH),
            pl.BlockSpec((tm, S5_CH), lambda i: (i, 0)),
            pl.BlockSpec((tm, S5_CH), lambda i: (i, 0)),
            pl.BlockSpec((tm, N_BRANCH * D_MODEL), lambda i: (i, COL_GATES // (N_BRANCH * D_MODEL))),
            pl.BlockSpec((None, None, 6, D_MODEL), lambda i: (l, _cond_index(i, tm), 0, 0)),
            full(1, S5_CH), layer(S5_CH, S5_CH), layer(NA_WIDTH, D_MODEL), layer(GQA_Q_WIDTH, D_MODEL),
            layer(S5_CH, D_MODEL), layer(D_MODEL, D_MODEL), full(1, D_MODEL),
            full(D_MODEL, 128), full(D_MODEL, 128), full(1, 128),
        ],
        out_specs=(pl.BlockSpec((tm, D_MODEL), lambda i: (i, 0)),
                   pl.BlockSpec((tm, MOE_ROW_WORDS), lambda i: (i, 0))),
        compiler_params=pltpu.CompilerParams(
            dimension_semantics=("parallel",), vmem_limit_bytes=VMEM_LIMIT),
        name="merge",
    )(x_ctx, x_lat, oa_c, oa_l, ob_c, ob_l, u, yc, z, mod, d_s5, wglu, wa, wb, wc, wout, g2, wr_hi, wr_lo, br)


def _moe_plan(cls):
    ncls = MOE_GROUPS * MOE_PAIRS
    ntiles = MOE_ROWS // MOE_TILE
    onehot = (cls[:, None] == jnp.arange(ncls)[None, :]).astype(jnp.int32)
    rank = jnp.sum((jnp.cumsum(onehot, axis=0) - onehot) * onehot, axis=1)
    ccount = jnp.sum(onehot, axis=0).reshape(MOE_GROUPS, MOE_PAIRS)
    gcount = jnp.sum(ccount, axis=1)
    gpadded = (gcount + MOE_TILE - 1) // MOE_TILE * MOE_TILE
    gend = jnp.cumsum(gpadded)
    cstart = ((gend - gpadded)[:, None] + jnp.cumsum(ccount, axis=1) - ccount).reshape(ncls)
    cend = cstart + ccount.reshape(ncls)
    pos = jnp.sum(onehot * cstart[None, :], axis=1) + rank
    tile_row = jnp.arange(ntiles) * MOE_TILE
    tile_group = jnp.minimum(jnp.sum((tile_row[:, None] >= gend[None, :]).astype(jnp.int32), axis=1),
                             MOE_GROUPS - 1)
    member = np.zeros((ncls, MOE_GROUPS, MOE_EXPERTS_PER_GROUP), np.float32)
    for g in range(MOE_GROUPS):
        for p, pair in enumerate(MOE_PAIR_MEMBERS):
            member[g * MOE_PAIRS + p, g, list(pair)] = 1.0
    overlap = ((cstart[None, :] < tile_row[:, None] + MOE_TILE) & (cend[None, :] > tile_row[:, None])
               & (cend > cstart)[None, :]).astype(F32)
    need = jnp.einsum("tc,cge->tge", overlap, member)
    need = jnp.sum(need * (tile_group[:, None, None] == jnp.arange(MOE_GROUPS)[None, :, None]), axis=1)
    return (pos.astype(jnp.int32), tile_group.astype(jnp.int32), (need > 0).astype(jnp.int32).reshape(-1),
            (gend[-1:] // MOE_TILE).astype(jnp.int32))


def _dispatch_kernel(pos_ref, hx_ref, init_ref, out_ref, sem):
    del init_ref
    base = pl.program_id(0) * MOE_TOK_BLOCK

    def row_copy(r):
        return pltpu.make_async_copy(hx_ref.at[pl.ds(r, 1), :], out_ref.at[pl.ds(pos_ref[base + r], 1), :], sem)

    def start(r, c):
        row_copy(r).start()
        return c

    def wait(r, c):
        row_copy(r).wait()
        return c

    lax.fori_loop(0, MOE_TOK_BLOCK, start, 0, unroll=16)
    lax.fori_loop(0, MOE_TOK_BLOCK, wait, 0, unroll=True)


def _dispatch(hx, pos):
    return pl.pallas_call(
        _dispatch_kernel,
        grid_spec=pltpu.PrefetchScalarGridSpec(
            num_scalar_prefetch=1,
            grid=(N_TOK // MOE_TOK_BLOCK,),
            in_specs=[pl.BlockSpec((MOE_TOK_BLOCK, MOE_ROW_WORDS), lambda i, pos: (i, 0)),
                      pl.BlockSpec(memory_space=pl.ANY)],
            out_specs=pl.BlockSpec(memory_space=pl.ANY),
            scratch_shapes=[pltpu.SemaphoreType.DMA]),
        out_shape=jax.ShapeDtypeStruct((MOE_ROWS, MOE_ROW_WORDS), jnp.uint32),
        input_output_aliases={2: 0},
        compiler_params=pltpu.CompilerParams(dimension_semantics=("arbitrary",)),
        name="moe_dispatch",
    )(pos, hx, jnp.zeros((MOE_ROWS, MOE_ROW_WORDS), jnp.uint32))


def _experts_kernel(tg_ref, need_ref, nu_ref, hx_ref, wg_ref, wu_ref, wd_ref, y_ref):
    t = pl.program_id(0)
    y_ref[...] = jnp.zeros_like(y_ref)
    half = D_MODEL // 2
    for e in range(MOE_EXPERTS_PER_GROUP):
        @pl.when((t < nu_ref[0]) & (need_ref[t * MOE_EXPERTS_PER_GROUP + e] > 0))
        def _():
            h = _unpack_pairs(hx_ref[:, :half]).astype(BF16)
            comb = pltpu.bitcast(hx_ref[:, half:], F32)
            lane = lax.broadcasted_iota(jnp.int32, comb.shape, 1)
            ce = jnp.sum(jnp.where(lane == tg_ref[t] * MOE_EXPERTS_PER_GROUP + e, comb, 0.0), axis=-1,
                         keepdims=True)
            a = _dot(h, wg_ref[e])
            b = _dot(h, wu_ref[e])
            act = (a * jax.nn.sigmoid(a)) * b * ce
            y_ref[...] += _dot(act.astype(BF16), wd_ref[e])


def _experts(hs, tile_group, need, n_used, wg, wu, wd, l):
    wspec = lambda r, c: pl.BlockSpec((None, None, MOE_EXPERTS_PER_GROUP, r, c),
                                      lambda t, tg, need, nu: (l, tg[t], 0, 0, 0))
    return pl.pallas_call(
        _experts_kernel,
        grid_spec=pltpu.PrefetchScalarGridSpec(
            num_scalar_prefetch=3,
            grid=(MOE_ROWS // MOE_TILE,),
            in_specs=[pl.BlockSpec((MOE_TILE, MOE_ROW_WORDS), lambda t, tg, need, nu: (t, 0)),
                      wspec(D_MODEL, EXPERT_FF), wspec(D_MODEL, EXPERT_FF), wspec(EXPERT_FF, D_MODEL)],
            out_specs=pl.BlockSpec((MOE_TILE, D_MODEL), lambda t, tg, need, nu: (t, 0))),
        out_shape=jax.ShapeDtypeStruct((MOE_ROWS, D_MODEL), F32),
        compiler_params=pltpu.CompilerParams(
            dimension_semantics=("arbitrary",), vmem_limit_bytes=VMEM_LIMIT),
        name="moe_experts",
    )(tile_group, need, n_used, hs, wg, wu, wd)


def _combine_kernel(pos_ref, y_ref, x_ref, mod_ref, fg_ref, *rest, final):
    buf, sem = rest[-2:]
    i = pl.program_id(0)
    base = i * MOE_TOK_BLOCK

    def row_copy(r):
        return pltpu.make_async_copy(y_ref.at[pl.ds(pos_ref[base + r], 1), :], buf.at[pl.ds(r, 1), :], sem)

    def start(r, c):
        row_copy(r).start()
        return c

    def wait(r, c):
        row_copy(r).wait()
        return c

    lax.fori_loop(0, MOE_TOK_BLOCK, start, 0, unroll=16)
    lax.fori_loop(0, MOE_TOK_BLOCK, wait, 0, unroll=True)
    x2 = x_ref[...] + mod_ref[5:6, :] * buf[...]
    if not final:
        rest[0][...] = x2
    else:
        y = _rms(x2, fg_ref[...])
        is_ctx = i < N_CTX_TOK // MOE_TOK_BLOCK

        @pl.when(is_ctx)
        def _():
            rest[0][...] = y

        @pl.when(jnp.logical_not(is_ctx))
        def _():
            rest[1][...] = y


def _combine(ys, pos, x1, mod, fg, l, final):
    tm = MOE_TOK_BLOCK
    nctx = N_CTX_TOK // tm
    tok = pl.BlockSpec((tm, D_MODEL), lambda i, pos: (i, 0))
    out = jax.ShapeDtypeStruct((N_TOK, D_MODEL), F32)
    if final:
        out = (jax.ShapeDtypeStruct((N_CTX_TOK, D_MODEL), F32), jax.ShapeDtypeStruct((N_LAT_TOK, D_MODEL), F32))
        out_specs = (pl.BlockSpec((tm, D_MODEL), lambda i, pos: (jnp.minimum(i, nctx - 1), 0)),
                     pl.BlockSpec((tm, D_MODEL), lambda i, pos: (jnp.maximum(i - nctx, 0), 0)))
    else:
        out_specs = tok
    return pl.pallas_call(
        functools.partial(_combine_kernel, final=final),
        grid_spec=pltpu.PrefetchScalarGridSpec(
            num_scalar_prefetch=1,
            grid=(N_TOK // tm,),
            in_specs=[pl.BlockSpec(memory_space=pl.ANY), tok,
                      pl.BlockSpec((None, None, 6, D_MODEL), lambda i, pos: (l, _cond_index(i, tm), 0, 0)),
                      pl.BlockSpec((1, D_MODEL), lambda i, pos: (0, 0))],
            out_specs=out_specs,
            scratch_shapes=[pltpu.VMEM((tm, D_MODEL), F32), pltpu.SemaphoreType.DMA]),
        out_shape=out,
        compiler_params=pltpu.CompilerParams(
            dimension_semantics=("arbitrary",), vmem_limit_bytes=VMEM_LIMIT),
        name="moe_combine",
    )(pos, ys, x1, mod, fg)


def _pack_w_in(w):
    qkv = w[..., :COL_G_V + GQA_KV_WIDTH]
    u = w[..., COL_G_V + GQA_KV_WIDTH:COL_G_V + GQA_KV_WIDTH + S5_CH]
    gates = w[..., COL_G_V + GQA_KV_WIDTH + S5_CH:]
    pad = jnp.zeros(w.shape[:-1] + (COL_U - (COL_G_V + GQA_KV_WIDTH),), w.dtype)
    return jnp.concatenate([qkv, pad, u, gates], axis=-1).astype(BF16)


def kernel(x_prompt, x_sample, cache_na_kv, cache_gqa_kv, state_ssm, c, c_ctx, norm_g, w_ada, b_ada, w_in, na_rpb, gqa_sink, s5_a_re, s5_a_im, s5_log_dt, s5_b_re, s5_b_im, s5_c_re, s5_c_im, s5_d, s5_w_glu, w_branch_a, w_branch_b, w_branch_c, w_out, moe_w_group, moe_b_group, moe_w_expert, moe_b_expert, moe_w_gate, moe_w_up, moe_w_down, final_g):
    cond = jnp.zeros((N_COND, D_MODEL), F32).at[0].set(c_ctx.astype(F32)).at[1:1 + DEC_BATCH].set(c.astype(F32))
    mod = _ada_mod(cond, w_ada.astype(F32), b_ada.astype(F32))

    x = (x_prompt.astype(F32).reshape(N_CTX_TOK, D_MODEL), x_sample.astype(F32).reshape(N_LAT_TOK, D_MODEL))
    cache_na = cache_na_kv.reshape(DEC_BATCH, DEPTH, 2, PAST_LEN, NA_WIDTH)
    cache_gqa = cache_gqa_kv.reshape(DEC_BATCH, DEPTH, 2, PAST_LEN, GQA_KV_WIDTH)
    cos_d, sin_d = _rope_tables()
    cos_q, sin_q = jnp.tile(cos_d, (1, GQA_Q_HEADS)), jnp.tile(sin_d, (1, GQA_Q_HEADS))
    cos_k, sin_k = jnp.tile(cos_d, (1, GQA_KV_HEADS)), jnp.tile(sin_d, (1, GQA_KV_HEADS))
    gn = S5_GROUPS * S5_STATE
    fg = final_g.astype(F32).reshape(1, D_MODEL)

    w_in_packed = _pack_w_in(w_in)
    bf16_weights = [w.astype(BF16) for w in (s5_w_glu, w_branch_a, w_branch_b, w_branch_c, w_out)]
    grouped = lambda w: w.astype(BF16).reshape((DEPTH, MOE_GROUPS, MOE_EXPERTS_PER_GROUP) + w.shape[2:])
    moe_weights = [grouped(w) for w in (moe_w_gate, moe_w_up, moe_w_down)]
    ops = jax.vmap(_s5_operators)(s5_a_re, s5_a_im, s5_log_dt, s5_b_re, s5_b_im, s5_c_re, s5_c_im)

    na_list, gqa_list, ssm_list = [], [], []
    y = None
    for l in range(DEPTH):
        z, u, new_na, new_gqa = _inproj(x, norm_g[l, 0].astype(F32).reshape(1, D_MODEL), mod, w_in_packed, l)
        na_list.append(new_na.reshape(BATCH, 2, SEQ, NA_HEADS, HEAD_DIM))
        gqa_list.append(new_gqa.reshape(BATCH, 2, SEQ, GQA_KV_HEADS, HEAD_DIM))

        sink = gqa_sink[l].astype(F32)
        oa_c, ob_c = _ctx_attn(z, sink)
        oa_l = _na_lat(z, cache_na, _na_bias_tables(na_rpb[l]), l)
        ob_l = _swa_lat(z, cache_gqa, sink, cos_q, sin_q, cos_k, sin_k, l)

        h0 = state_ssm[:, l].astype(F32).reshape(DEC_BATCH, 2, 2, gn).transpose(1, 2, 0, 3)
        yc, fin = _s5_scan(u, ops, h0, l)
        ssm_list.append(fin.transpose(2, 0, 1, 3).reshape(BATCH, 2, 2, S5_GROUPS, S5_STATE).astype(x_prompt.dtype))

        wr = jnp.zeros((D_MODEL, 128), F32)
        wr = wr.at[:, :MOE_EXPERTS].set(moe_w_expert[l].astype(F32))
        wr = wr.at[:, MOE_EXPERTS:MOE_EXPERTS + MOE_GROUPS].set(moe_w_group[l].astype(F32))
        br = jnp.zeros((1, 128), F32)
        br = br.at[0, :MOE_EXPERTS].set(moe_b_expert[l].astype(F32))
        br = br.at[0, MOE_EXPERTS:MOE_EXPERTS + MOE_GROUPS].set(moe_b_group[l].astype(F32))
        wr_hi, wr_lo = _split_bf16(wr)
        x1, hx = _merge(
            x, oa_c, oa_l, ob_c, ob_l, u, yc, z, mod, s5_d[l].astype(F32).reshape(1, S5_CH), *bf16_weights,
            norm_g[l, 1].astype(F32).reshape(1, D_MODEL), wr_hi, wr_lo, br, l)
        cls = lax.bitcast_convert_type(hx[:, D_MODEL // 2 + MOE_EXPERTS], F32).astype(jnp.int32)
        pos, tile_group, need, n_used = _moe_plan(cls)
        ys = _experts(_dispatch(hx, pos), tile_group, need, n_used, *moe_weights, l)
        if l < DEPTH - 1:
            x = _combine(ys, pos, x1, mod, fg, l, False)
        else:
            y_ctx, y_lat = _combine(ys, pos, x1, mod, fg, l, True)

    return (y_ctx.reshape(BATCH, SEQ, D_MODEL), y_lat.reshape(DEC_BATCH, DEC_SEQ, D_MODEL),
            jnp.stack(na_list, axis=1), jnp.stack(gqa_list, axis=1), jnp.stack(ssm_list, axis=1))
```

```python
import functools
import math

import numpy as np
import jax
import jax.numpy as jnp
from jax import lax
from jax.experimental import pallas as pl
from jax.experimental.pallas import tpu as pltpu

F32 = jnp.float32
BF16 = jnp.bfloat16

D_MODEL = 1024
BATCH = 32
SEQ = 256
DEPTH = 2
DEC_BATCH = 4
DEC_SEQ = 2048
PAST_LEN = 512
GRID_W = 64
GRID_ROWS = DEC_SEQ // GRID_W
HEAD_DIM = 64
NA_HEADS = 8
NA_WIN_H = 8
NA_WIN_W = 16
GQA_Q_HEADS = 8
GQA_KV_HEADS = 2
GQA_GROUP = GQA_Q_HEADS // GQA_KV_HEADS
SWA_WINDOW = 128
SWA_BLOCK = 128
ROPE_THETA = 10000.0
S5_CH = 512
S5_GROUP_CH = 16
S5_GROUPS = S5_CH // S5_GROUP_CH
S5_STATE = 64
N_BRANCH = 3
NA_WIDTH = NA_HEADS * HEAD_DIM
GQA_Q_WIDTH = GQA_Q_HEADS * HEAD_DIM
GQA_KV_WIDTH = GQA_KV_HEADS * HEAD_DIM
MOE_GROUPS = 4
MOE_EXPERTS_PER_GROUP = 4
MOE_EXPERTS = MOE_GROUPS * MOE_EXPERTS_PER_GROUP
EXPERT_FF = 512
EPS = 1e-6
NEG_INF = -1e30

N_CTX_TOK = BATCH * SEQ
N_LAT_TOK = DEC_BATCH * DEC_SEQ
N_TOK = N_CTX_TOK + N_LAT_TOK
N_COND = 8

COL_NA_Q = 0
COL_NA_K = 512
COL_NA_V = 1024
COL_G_Q = 1536
COL_G_K = 2048
COL_G_V = 2176
COL_U = 2560
COL_GATES = 3072
Z_COLS = 6144

S5_T = 16
S5_PAIRS = S5_GROUPS // 2
S5_OCT_PAIRS = 4
S5_BLOCK_TOK = 4096
S5_ROWS = S5_BLOCK_TOK // S5_T
S5_NPOW = 7

MOE_ROW_WORDS = D_MODEL // 2 + 128
MOE_TILE = 512
MOE_ROWS = N_TOK + MOE_GROUPS * MOE_TILE
MOE_TOK_BLOCK = 512
MOE_PAIR_MEMBERS = ((0, 1), (0, 2), (0, 3), (1, 3), (1, 2), (2, 3))
MOE_PAIRS = len(MOE_PAIR_MEMBERS)

CTX_HEADS_PER_DOT = 4
NA_HEADS_PER_DOT = 2
NA_QROWS = 4
NA_KROWS = 12

VMEM_LIMIT = 56 * 1024 * 1024


def _dot(a, b):
    return jnp.dot(a, b, preferred_element_type=F32)


def _dot_nt(a, b):
    return lax.dot_general(a, b, (((1,), (1,)), ((), ())), preferred_element_type=F32)


def _split_bf16(x):
    hi = x.astype(BF16)
    lo = (x - hi.astype(F32)).astype(BF16)
    return hi, lo


def _pack_pairs(x):
    w = x.shape[1] // 2
    xb = x.astype(BF16).astype(F32)
    return (pltpu.bitcast(xb[:, :w], jnp.uint32) & jnp.uint32(0xFFFF0000)) | (
        pltpu.bitcast(xb[:, w:], jnp.uint32) >> 16)


def _unpack_pairs(word):
    return jnp.concatenate([pltpu.bitcast(word & jnp.uint32(0xFFFF0000), F32),
                            pltpu.bitcast(word << 16, F32)], axis=1)


def _rms(x, g):
    return x * lax.rsqrt(jnp.mean(x * x, axis=-1, keepdims=True) + EPS) * g


def _cond_index(i, tm):
    nctx = N_CTX_TOK // tm
    return jnp.where(i < nctx, 0, 1 + ((i - nctx) * tm) // DEC_SEQ)


def _ada_kernel(c_ref, w_ref, b_ref, o_ref):
    c = c_ref[...]
    s = c * jax.nn.sigmoid(c)
    s_hi, s_lo = _split_bf16(s)
    w_hi, w_lo = _split_bf16(w_ref[...])
    o_ref[...] = _dot(s_hi, w_hi) + _dot(s_lo, w_hi) + _dot(s_hi, w_lo) + b_ref[...]


def _ada_mod(cond, w_ada, b_ada):
    tn = 1536
    n = 6 * D_MODEL
    out = pl.pallas_call(
        _ada_kernel,
        out_shape=jax.ShapeDtypeStruct((DEPTH, N_COND, n), F32),
        grid=(DEPTH, n // tn),
        in_specs=[
            pl.BlockSpec((N_COND, D_MODEL), lambda l, j: (0, 0)),
            pl.BlockSpec((None, D_MODEL, tn), lambda l, j: (l, 0, j)),
            pl.BlockSpec((None, 1, tn), lambda l, j: (l, 0, j)),
        ],
        out_specs=pl.BlockSpec((None, N_COND, tn), lambda l, j: (l, 0, j)),
        compiler_params=pltpu.CompilerParams(
            dimension_semantics=("parallel", "parallel"), vmem_limit_bytes=VMEM_LIMIT),
        name="ada_mod",
    )(cond, w_ada, b_ada.reshape(DEPTH, 1, n))
    return out.reshape(DEPTH, N_COND, 6, D_MODEL)


INPROJ_TM = 1024
INPROJ_TN = 1536


def _inproj_kernel(xc_ref, xl_ref, g_ref, mod_ref, w_ref, z_ref, u_ref, cna_ref, cgq_ref, h_sc):
    i, j = pl.program_id(0), pl.program_id(1)
    is_ctx = i < N_CTX_TOK // INPROJ_TM

    def norm_mod(x_ref):
        h = _rms(x_ref[...], g_ref[...]) * (1.0 + mod_ref[1:2, :]) + mod_ref[0:1, :]
        h_sc[...] = h.astype(BF16)

    @pl.when((j == 0) & is_ctx)
    def _():
        norm_mod(xc_ref)

    @pl.when((j == 0) & jnp.logical_not(is_ctx))
    def _():
        norm_mod(xl_ref)

    acc = _dot(h_sc[...], w_ref[...])
    z_ref[...] = acc.astype(BF16)

    def rows(b):
        return slice(b * SEQ, (b + 1) * SEQ)

    def cols(c, width):
        return slice(c % INPROJ_TN, c % INPROJ_TN + width)

    @pl.when(is_ctx & (j == COL_NA_K // INPROJ_TN))
    def _():
        for b in range(INPROJ_TM // SEQ):
            cna_ref[b, 0] = acc[rows(b), cols(COL_NA_K, NA_WIDTH)]
            cna_ref[b, 1] = acc[rows(b), cols(COL_NA_V, NA_WIDTH)]

    @pl.when(is_ctx & (j == COL_G_K // INPROJ_TN))
    def _():
        for b in range(INPROJ_TM // SEQ):
            cgq_ref[b, 0] = acc[rows(b), cols(COL_G_K, GQA_KV_WIDTH)]
            cgq_ref[b, 1] = acc[rows(b), cols(COL_G_V, GQA_KV_WIDTH)]

    @pl.when(j == COL_U // INPROJ_TN)
    def _():
        u_ref[...] = acc[:, cols(COL_U, S5_CH)]


def _token_arrays(x):
    return (x, x, N_CTX_TOK) if not isinstance(x, tuple) else (x[0], x[1], 0)


def _inproj(x, g, mod, w, l):
    tm, tn = INPROJ_TM, INPROJ_TN
    nb = tm // SEQ
    last_ctx = N_CTX_TOK // tm - 1
    x_ctx, x_lat, lat0 = _token_arrays(x)
    return pl.pallas_call(
        _inproj_kernel,
        out_shape=(jax.ShapeDtypeStruct((N_TOK, Z_COLS), BF16),
                   jax.ShapeDtypeStruct((N_TOK, S5_CH), F32),
                   jax.ShapeDtypeStruct((BATCH, 2, SEQ, NA_WIDTH), F32),
                   jax.ShapeDtypeStruct((BATCH, 2, SEQ, GQA_KV_WIDTH), F32)),
        grid=(N_TOK // tm, Z_COLS // tn),
        in_specs=[
            pl.BlockSpec((tm, D_MODEL), lambda i, j: (jnp.minimum(i, last_ctx), 0)),
            pl.BlockSpec((tm, D_MODEL), lambda i, j: (jnp.maximum(i - last_ctx - 1, 0) + lat0 // tm, 0)),
            pl.BlockSpec((1, D_MODEL), lambda i, j: (0, 0)),
            pl.BlockSpec((None, None, 6, D_MODEL), lambda i, j: (l, _cond_index(i, tm), 0, 0)),
            pl.BlockSpec((None, D_MODEL, tn), lambda i, j: (l, 0, j)),
        ],
        out_specs=(pl.BlockSpec((tm, tn), lambda i, j: (i, j)),
                   pl.BlockSpec((tm, S5_CH), lambda i, j: (i, 0)),
                   pl.BlockSpec((nb, 2, SEQ, NA_WIDTH), lambda i, j: (jnp.minimum(i, last_ctx), 0, 0, 0)),
                   pl.BlockSpec((nb, 2, SEQ, GQA_KV_WIDTH), lambda i, j: (jnp.minimum(i, last_ctx), 0, 0, 0))),
        scratch_shapes=[pltpu.VMEM((tm, D_MODEL), BF16)],
        compiler_params=pltpu.CompilerParams(
            dimension_semantics=("arbitrary", "arbitrary"), vmem_limit_bytes=VMEM_LIMIT),
        name="inproj",
    )(x_ctx, x_lat, g, mod, w)


def _block_diag_heads(x, nrep):
    t = x.shape[0]
    rows = lax.broadcasted_iota(jnp.int32, (nrep * t, nrep * HEAD_DIM), 0) // t
    lanes = lax.broadcasted_iota(jnp.int32, (nrep * t, nrep * HEAD_DIM), 1) // HEAD_DIM
    return jnp.where(rows == lanes, jnp.concatenate([x] * nrep, axis=0), jnp.zeros((), x.dtype))


def _per_head(cols, shape):
    head = lax.broadcasted_iota(jnp.int32, shape, 1) // HEAD_DIM
    out = cols[-1]
    for h in range(len(cols) - 2, -1, -1):
        out = jnp.where(head == h, cols[h], out)
    return out


def _attend_heads(q, k, v, sinks):
    nh = q.shape[1] // HEAD_DIM
    t = k.shape[0]
    s = _dot_nt(q * HEAD_DIM ** -0.5, _block_diag_heads(k, nh))
    ps, ds = [], []
    for h in range(nh):
        sh = s[:, h * t:(h + 1) * t]
        m = jnp.max(sh, axis=-1, keepdims=True)
        if sinks is not None:
            m = jnp.maximum(m, sinks[h])
        p = jnp.exp(sh - m)
        d = jnp.sum(p, axis=-1, keepdims=True)
        if sinks is not None:
            d = d + jnp.exp(sinks[h] - m)
        ps.append(p.astype(BF16))
        ds.append(d)
    o = _dot(jnp.concatenate(ps, axis=1), _block_diag_heads(v, nh))
    return o / _per_head(ds, o.shape)


def _ctx_attn_kernel(sink_ref, na_ref, gq_ref, gkv_ref, oa_ref, ob_ref):
    hb = CTX_HEADS_PER_DOT
    w = hb * HEAD_DIM
    na = na_ref[...]
    gq = gq_ref[...]
    gkv = gkv_ref[...]
    for g in range(NA_HEADS // hb):
        o = _attend_heads(na[:, COL_NA_Q + g * w:COL_NA_Q + (g + 1) * w],
                          na[:, COL_NA_K + g * w:COL_NA_K + (g + 1) * w],
                          na[:, COL_NA_V + g * w:COL_NA_V + (g + 1) * w], None)
        oa_ref[:, g * w:(g + 1) * w] = o.astype(BF16)
    for g in range(GQA_Q_HEADS // hb):
        kv_heads = [(g * hb + h) // GQA_GROUP for h in range(hb)]
        k = jnp.concatenate([gkv[:, kh * HEAD_DIM:(kh + 1) * HEAD_DIM] for kh in kv_heads], axis=1)
        v = jnp.concatenate([gkv[:, GQA_KV_WIDTH + kh * HEAD_DIM:GQA_KV_WIDTH + (kh + 1) * HEAD_DIM]
                             for kh in kv_heads], axis=1)
        o = _attend_heads(gq[:, g * w:(g + 1) * w], k, v, [sink_ref[g * hb + h] for h in range(hb)])
        ob_ref[:, g * w:(g + 1) * w] = o.astype(BF16)


def _ctx_attn(z, sink):
    out = jax.ShapeDtypeStruct((N_CTX_TOK, NA_WIDTH), BF16)
    return pl.pallas_call(
        _ctx_attn_kernel,
        out_shape=(out, out),
        grid=(BATCH,),
        in_specs=[
            pl.BlockSpec(memory_space=pltpu.SMEM),
            pl.BlockSpec((SEQ, 3 * NA_WIDTH), lambda b: (b, 0)),
            pl.BlockSpec((SEQ, GQA_Q_WIDTH), lambda b: (b, COL_G_Q // GQA_Q_WIDTH)),
            pl.BlockSpec((SEQ, 2 * GQA_KV_WIDTH), lambda b: (b, COL_G_K // (2 * GQA_KV_WIDTH))),
        ],
        out_specs=(pl.BlockSpec((SEQ, NA_WIDTH), lambda b: (b, 0)),
                   pl.BlockSpec((SEQ, GQA_Q_WIDTH), lambda b: (b, 0))),
        compiler_params=pltpu.CompilerParams(
            dimension_semantics=("parallel",), vmem_limit_bytes=VMEM_LIMIT),
        name="ctx_attn",
    )(sink, z, z, z)


def _na_bias_tables(rpb):
    n_dc = 2 * NA_WIN_W - 1
    qc = np.arange(GRID_W)
    kc = np.arange(GRID_W)
    cstart = np.clip(qc - NA_WIN_W // 2, 0, GRID_W - NA_WIN_W)
    col_ok = (kc[None, :] >= cstart[:, None]) & (kc[None, :] < cstart[:, None] + NA_WIN_W)
    dc = np.clip(kc[None, :] - qc[:, None] + (NA_WIN_W - 1), 0, n_dc - 1)
    onehot = ((np.arange(n_dc)[:, None, None] == dc[None]) & col_ok[None]).astype(np.float32)
    band = jnp.einsum("hrd,dqk->hrqk", rpb.astype(F32), onehot, precision=lax.Precision.HIGHEST)
    band = jnp.where(col_ok, band, NEG_INF)
    band = jnp.pad(band, ((0, 0), (1, 1), (0, 0), (0, 0)), constant_values=NEG_INF)
    return jnp.concatenate([band[:, :-1], band[:, 1:]], axis=-1)


def _na_lat_kernel(q_ref, k_ref, v_ref, ck_ref, cv_ref, bias_ref, rowmask_ref, o_ref):
    i = pl.program_id(1)
    scale = HEAD_DIM ** -0.5
    nk = NA_KROWS * GRID_W
    ks = jnp.clip(i * NA_QROWS - NA_WIN_H // 2, 0, GRID_ROWS - NA_KROWS)
    start = pl.multiple_of(ks * GRID_W, GRID_W)
    q = q_ref[...].astype(BF16)
    kw = k_ref[pl.ds(start, nk), :].astype(BF16)
    vw = v_ref[pl.ds(start, nk), :].astype(BF16)
    ck = ck_ref[...].astype(BF16)
    cv = cv_ref[...].astype(BF16)
    planes = []
    for qr in range(NA_QROWS):
        r = i * NA_QROWS + qr
        planes.append([jnp.clip(ks + 2 * m - r + NA_WIN_H, 0, 2 * NA_WIN_H - 1) for m in range(NA_KROWS // 2)])
    hb = NA_HEADS_PER_DOT
    w = hb * HEAD_DIM
    qrow = lax.broadcasted_iota(jnp.int32, (NA_QROWS * GRID_W, 128), 0) // GRID_W
    qsel = (lax.broadcasted_iota(jnp.int32, (NA_QROWS * GRID_W, 128), 1) == qrow).astype(BF16)
    rowmask = rowmask_ref[...]
    for g in range(NA_HEADS // hb):
        lanes = slice(g * w, (g + 1) * w)
        qg = jnp.concatenate([q[:, lanes] * scale, qsel], axis=1)
        kg = jnp.concatenate([_block_diag_heads(kw[:, lanes], hb), rowmask], axis=1)
        s_loc = _dot_nt(qg, kg)
        s_ctx = _dot_nt(q[:, lanes] * scale, _block_diag_heads(ck[:, lanes], hb))
        p_locs, p_ctxs, ds = [], [], []
        for hh in range(hb):
            bias = jnp.concatenate(
                [jnp.concatenate([bias_ref[g * hb + hh, plane] for plane in row], axis=1) for row in planes], axis=0)
            sl = s_loc[:, hh * nk:(hh + 1) * nk] + bias
            sc = s_ctx[:, hh * PAST_LEN:(hh + 1) * PAST_LEN]
            m = jnp.maximum(jnp.max(sl, axis=-1, keepdims=True), jnp.max(sc, axis=-1, keepdims=True))
            p_loc = jnp.exp(sl - m)
            p_ctx = jnp.exp(sc - m)
            ds.append(jnp.sum(p_loc, axis=-1, keepdims=True) + jnp.sum(p_ctx, axis=-1, keepdims=True))
            p_locs.append(p_loc.astype(BF16))
            p_ctxs.append(p_ctx.astype(BF16))
        o = (_dot(jnp.concatenate(p_locs, axis=1), _block_diag_heads(vw[:, lanes], hb))
             + _dot(jnp.concatenate(p_ctxs, axis=1), _block_diag_heads(cv[:, lanes], hb)))
        o_ref[:, lanes] = (o / _per_head(ds, o.shape)).astype(BF16)


def _na_row_masks():
    kh = min(NA_WIN_H, GRID_ROWS)
    nsteps = GRID_ROWS // NA_QROWS
    out = np.zeros((3, NA_HEADS_PER_DOT, NA_KROWS, GRID_W, 128), np.float32)
    for pat, i in enumerate((0, 1, nsteps - 1)):
        ks = min(max(i * NA_QROWS - NA_WIN_H // 2, 0), GRID_ROWS - NA_KROWS)
        for qr in range(NA_QROWS):
            r = i * NA_QROWS + qr
            st = min(max(r - kh // 2, 0), GRID_ROWS - kh)
            for kr in range(NA_KROWS):
                if not st <= ks + kr < st + kh:
                    out[pat, :, kr, :, qr] = NEG_INF
    return jnp.asarray(out.reshape(3, NA_HEADS_PER_DOT * NA_KROWS * GRID_W, 128), BF16)


def _na_lat(z, cache_na, bias, l):
    tq = NA_QROWS * GRID_W
    nsteps = GRID_ROWS // NA_QROWS
    lat_blk = N_CTX_TOK // DEC_SEQ
    return pl.pallas_call(
        _na_lat_kernel,
        out_shape=jax.ShapeDtypeStruct((N_LAT_TOK, NA_WIDTH), BF16),
        grid=(DEC_BATCH, nsteps),
        in_specs=[
            pl.BlockSpec((tq, NA_WIDTH), lambda b, i: (N_CTX_TOK // tq + b * nsteps + i, COL_NA_Q // NA_WIDTH)),
            pl.BlockSpec((DEC_SEQ, NA_WIDTH), lambda b, i: (lat_blk + b, COL_NA_K // NA_WIDTH)),
            pl.BlockSpec((DEC_SEQ, NA_WIDTH), lambda b, i: (lat_blk + b, COL_NA_V // NA_WIDTH)),
            pl.BlockSpec((None, None, None, PAST_LEN, NA_WIDTH), lambda b, i: (b, l, 0, 0, 0)),
            pl.BlockSpec((None, None, None, PAST_LEN, NA_WIDTH), lambda b, i: (b, l, 1, 0, 0)),
            pl.BlockSpec((NA_HEADS, 2 * NA_WIN_H, GRID_W, 2 * GRID_W), lambda b, i: (0, 0, 0, 0)),
            pl.BlockSpec((None, NA_HEADS_PER_DOT * NA_KROWS * GRID_W, 128),
                         lambda b, i: (jnp.where(i == 0, 0, jnp.where(i == nsteps - 1, 2, 1)), 0, 0)),
        ],
        out_specs=pl.BlockSpec((tq, NA_WIDTH), lambda b, i: (b * nsteps + i, 0)),
        compiler_params=pltpu.CompilerParams(
            dimension_semantics=("parallel", "arbitrary"), vmem_limit_bytes=VMEM_LIMIT),
        name="na_lat",
    )(z, z, z, cache_na, cache_na, bias, _na_row_masks())


def _rope_tables():
    nf = HEAD_DIM // 4
    t = jnp.arange(DEC_SEQ)
    pos = jnp.stack([t // GRID_W, t % GRID_W], axis=-1).astype(F32)
    inv = ROPE_THETA ** (-jnp.arange(nf, dtype=F32) / nf)
    ang = pos[:, :, None] * inv
    cos = jnp.cos(ang)
    sin = jnp.sin(ang)
    cos_d = jnp.stack([cos, cos], axis=2).reshape(DEC_SEQ, HEAD_DIM)
    sin_d = jnp.stack([-sin, sin], axis=2).reshape(DEC_SEQ, HEAD_DIM)
    return cos_d, sin_d


def _rope(x, cos, sin_signed):
    n = x.shape[-1]
    nf = HEAD_DIM // 4
    lane = lax.broadcasted_iota(jnp.int32, x.shape, 1)
    first_half = (lane // nf) % 2 == 0
    partner = jnp.where(first_half, pltpu.roll(x, n - nf, 1), pltpu.roll(x, nf, 1))
    return x * cos + partner * sin_signed


def _swa_lat_kernel(sink_ref, q_ref, kv_ref, cq_ref, sq_ref, ckt_ref, skt_ref, ck_ref, cv_ref, o_ref, k_sc, v_sc):
    n = pl.program_id(1)
    scale = HEAD_DIM ** -0.5
    nwin = 3 * SWA_BLOCK

    @pl.when(n == 0)
    def _():
        kv = kv_ref[...]
        k_sc[...] = _rope(kv[:, :GQA_KV_WIDTH].astype(F32), ckt_ref[...], skt_ref[...]).astype(BF16)
        v_sc[...] = kv[:, GQA_KV_WIDTH:].astype(BF16)

    q = (_rope(q_ref[...].astype(F32), cq_ref[...], sq_ref[...]) * scale).astype(BF16)
    start = pl.multiple_of(jnp.clip((n - 1) * SWA_BLOCK, 0, DEC_SEQ - nwin), SWA_BLOCK)
    kw = k_sc[pl.ds(start, nwin), :]
    vw = v_sc[pl.ds(start, nwin), :]
    ck = ck_ref[...].astype(BF16)
    cv = cv_ref[...].astype(BF16)
    rows = GQA_GROUP * SWA_BLOCK
    row = lax.broadcasted_iota(jnp.int32, (rows, nwin), 0)
    col = lax.broadcasted_iota(jnp.int32, (rows, nwin), 1)
    qpos = n * SWA_BLOCK + row % SWA_BLOCK
    kpos = start + col
    ok = jnp.abs(qpos - kpos) <= SWA_WINDOW
    grp = lax.broadcasted_iota(jnp.int32, (rows, 1), 0) // SWA_BLOCK
    for kh in range(GQA_KV_HEADS):
        sl = slice(kh * HEAD_DIM, (kh + 1) * HEAD_DIM)
        q4 = jnp.concatenate(
            [q[:, (kh * GQA_GROUP + g) * HEAD_DIM:(kh * GQA_GROUP + g + 1) * HEAD_DIM] for g in range(GQA_GROUP)],
            axis=0)
        sink = jnp.zeros((rows, 1), F32)
        for g in range(GQA_GROUP):
            sink = jnp.where(grp == g, sink_ref[kh * GQA_GROUP + g], sink)
        s_loc = jnp.where(ok, _dot_nt(q4, kw[:, sl]), NEG_INF)
        s_ctx = _dot_nt(q4, ck[:, sl])
        m = jnp.maximum(jnp.maximum(jnp.max(s_loc, axis=-1, keepdims=True),
                                    jnp.max(s_ctx, axis=-1, keepdims=True)), sink)
        p_loc = jnp.exp(s_loc - m)
        p_ctx = jnp.exp(s_ctx - m)
        d = (jnp.sum(p_loc, axis=-1, keepdims=True) + jnp.sum(p_ctx, axis=-1, keepdims=True)
             + jnp.exp(sink - m))
        o4 = (_dot(p_loc.astype(BF16), vw[:, sl]) + _dot(p_ctx.astype(BF16), cv[:, sl])) / d
        for g in range(GQA_GROUP):
            h = kh * GQA_GROUP + g
            o_ref[:, h * HEAD_DIM:(h + 1) * HEAD_DIM] = o4[g * SWA_BLOCK:(g + 1) * SWA_BLOCK].astype(BF16)


def _swa_lat(z, cache_gqa, sink, cos_q, sin_q, cos_k, sin_k, l):
    nb = DEC_SEQ // SWA_BLOCK
    lat_blk = N_CTX_TOK // DEC_SEQ
    return pl.pallas_call(
        _swa_lat_kernel,
        out_shape=jax.ShapeDtypeStruct((N_LAT_TOK, GQA_Q_WIDTH), BF16),
        grid=(DEC_BATCH, nb),
        in_specs=[
            pl.BlockSpec(memory_space=pltpu.SMEM),
            pl.BlockSpec((SWA_BLOCK, GQA_Q_WIDTH),
                         lambda b, n: (N_CTX_TOK // SWA_BLOCK + b * nb + n, COL_G_Q // GQA_Q_WIDTH)),
            pl.BlockSpec((DEC_SEQ, 2 * GQA_KV_WIDTH), lambda b, n: (lat_blk + b, COL_G_K // (2 * GQA_KV_WIDTH))),
            pl.BlockSpec((SWA_BLOCK, GQA_Q_WIDTH), lambda b, n: (n, 0)),
            pl.BlockSpec((SWA_BLOCK, GQA_Q_WIDTH), lambda b, n: (n, 0)),
            pl.BlockSpec((DEC_SEQ, GQA_KV_WIDTH), lambda b, n: (0, 0)),
            pl.BlockSpec((DEC_SEQ, GQA_KV_WIDTH), lambda b, n: (0, 0)),
            pl.BlockSpec((None, None, None, PAST_LEN, GQA_KV_WIDTH), lambda b, n: (b, l, 0, 0, 0)),
            pl.BlockSpec((None, None, None, PAST_LEN, GQA_KV_WIDTH), lambda b, n: (b, l, 1, 0, 0)),
        ],
        out_specs=pl.BlockSpec((SWA_BLOCK, GQA_Q_WIDTH), lambda b, n: (b * nb + n, 0)),
        scratch_shapes=[pltpu.VMEM((DEC_SEQ, GQA_KV_WIDTH), BF16), pltpu.VMEM((DEC_SEQ, GQA_KV_WIDTH), BF16)],
        compiler_params=pltpu.CompilerParams(
            dimension_semantics=("parallel", "arbitrary"), vmem_limit_bytes=VMEM_LIMIT),
        name="swa_lat",
    )(sink, z, z, cos_q, sin_q, cos_k, sin_k, cache_gqa, cache_gqa)


S5_OP_KEYS = ("kt", "bt_re", "bt_im", "ct_re", "ct_im", "pin_re", "pin_im", "po_re", "po_im", "apr", "api")


def _s5_operators(a_re, a_im, log_dt, b_re, b_im, c_re, c_im):
    T, G, N, C = S5_T, S5_GROUPS, S5_STATE, S5_GROUP_CH
    tau = jnp.arange(T + 1, dtype=F32)
    out = {k: [] for k in S5_OP_KEYS}
    for d in range(2):
        A = lax.complex(a_re[d].astype(F32), a_im[d].astype(F32))
        dt = jnp.exp(log_dt[d].astype(F32))[:, None]
        a_bar = jnp.exp(A * dt)
        pw = jnp.exp((A * dt)[None] * tau[:, None, None])
        b_bar = ((a_bar - 1.0) / A)[..., None] * lax.complex(b_re[d].astype(F32), b_im[d].astype(F32))
        c_mat = lax.complex(c_re[d].astype(F32), c_im[d].astype(F32))
        kern = jnp.einsum("gon,tgn,gni->gtoi", c_mat, pw[:T], b_bar, precision=lax.Precision.HIGHEST).real
        kern = lax.optimization_barrier(kern).transpose(0, 3, 1, 2)
        if d == 0:
            p_in = pw[:T][::-1]
            p_out = pw[1:T + 1]
        else:
            p_in = pw[:T]
            p_out = pw[1:T + 1][::-1]
            kern = kern[:, :, ::-1]
        bt, ct, po = b_bar.transpose(0, 2, 1), c_mat.transpose(0, 2, 1), p_out.transpose(1, 2, 0)
        pw2 = jnp.exp((A * dt)[None] * (T * 2.0 ** jnp.arange(S5_NPOW, dtype=F32))[:, None, None])
        vals = (kern.reshape(G, C, T * C), bt.real, bt.imag, ct.real, ct.imag,
                p_in.real.reshape(T, S5_PAIRS, 1, 2 * N), p_in.imag.reshape(T, S5_PAIRS, 1, 2 * N), po.real, po.imag,
                pw2.real.reshape(S5_NPOW, G * N), pw2.imag.reshape(S5_NPOW, G * N))
        for k, v in zip(S5_OP_KEYS, vals):
            out[k].append(v)
    return {k: jnp.stack(v) for k, v in out.items()}


def _s5_expand(kt_ref, btr_ref, bti_ref, ctr_ref, cti_ref, pir_ref, pii_ref, por_ref, poi_ref,
               bs_sc, m_sc, cre_sc, cim_sc):
    T, C, N = S5_T, S5_GROUP_CH, S5_STATE
    pc, half, width = 2 * C, 2 * N, 2 * T * C
    lane = lax.broadcasted_iota(jnp.int32, (pc, width), 1)
    zc, zn, zo = jnp.zeros((C, C), F32), jnp.zeros((C, N), F32), jnp.zeros((N, C), F32)

    def own_half(x, z, g):
        return [x, z] if g == 0 else [z, x]

    def expand_pair(p, d):
        k0, bx_re, bx_im, co_re, co_im = [], [], [], [], []
        for g in range(2):
            gl = 2 * p + g
            ktg = kt_ref[d, gl]
            k0.append(jnp.concatenate(
                [blk for lag in range(T) for blk in own_half(ktg[:, lag * C:(lag + 1) * C], zc, g)], axis=1))
            bx_re.append(jnp.concatenate(own_half(btr_ref[d, gl], zn, g), axis=1))
            bx_im.append(jnp.concatenate(own_half(bti_ref[d, gl], zn, g), axis=1))
            cr, ci = ctr_ref[d, gl], cti_ref[d, gl]
            pr_all, pi_all = por_ref[d, gl], poi_ref[d, gl]
            re_p, im_p = [], []
            for j in range(T):
                pr, pi = pr_all[:, j:j + 1], pi_all[:, j:j + 1]
                re_p += own_half(cr * pr - ci * pi, zo, g)
                im_p += own_half(cr * pi + ci * pr, zo, g)
            co_re.append(jnp.concatenate(re_p, axis=1))
            co_im.append(jnp.concatenate(im_p, axis=1))
        k0 = jnp.concatenate(k0, axis=0)
        bx_re, bx_im = jnp.concatenate(bx_re, axis=0), jnp.concatenate(bx_im, axis=0)
        for s in range(T):
            if d == 0:
                sh = pc * s
                blk = jnp.where(lane >= sh, pltpu.roll(k0, sh, 1), 0.0) if sh else k0
            else:
                sh = pc * (T - 1 - s)
                blk = jnp.where(lane < width - sh, pltpu.roll(k0, width - sh, 1), 0.0) if sh else k0
            m_sc[d, p, s * pc:(s + 1) * pc, :] = blk.astype(BF16)
            pr, pi = pir_ref[d, s, p], pii_ref[d, s, p]
            bs_sc[d, p, s * pc:(s + 1) * pc, :] = jnp.concatenate(
                [pr * bx_re - pi * bx_im, pr * bx_im + pi * bx_re], axis=1).astype(BF16)
        cre_sc[d, p] = jnp.concatenate(co_re, axis=0).astype(BF16)
        cim_sc[d, p] = (-jnp.concatenate(co_im, axis=0)).astype(BF16)

    for d in range(2):
        lax.fori_loop(0, S5_OCT_PAIRS, lambda p, c, d=d: (expand_pair(p, d), c)[1], 0)


def _s5_kernel(u_ref, kt_ref, btr_ref, bti_ref, ctr_ref, cti_ref, pir_ref, pii_ref, por_ref, poi_ref, apr_ref,
               api_ref, h0_ref, y_ref, fin_ref, fin_sc, bs_sc, m_sc, cre_sc, cim_sc):
    t = pl.program_id(1)
    nctx = N_CTX_TOK // S5_BLOCK_TOK

    @pl.when(t == 0)
    def _():
        _s5_expand(kt_ref, btr_ref, bti_ref, ctr_ref, cti_ref, pir_ref, pii_ref, por_ref, poi_ref,
                   bs_sc, m_sc, cre_sc, cim_sc)

    refs = (u_ref, bs_sc, m_sc, cre_sc, cim_sc, apr_ref, api_ref, h0_ref, y_ref, fin_ref, fin_sc)

    @pl.when(t < nctx)
    def _():
        _s5_block(*refs, t, kseq=SEQ // S5_T, has_h0=False)

    @pl.when(t >= nctx)
    def _():
        _s5_block(*refs, t - nctx, kseq=DEC_SEQ // S5_T, has_h0=True)


def _s5_block(u_ref, bs_ref, m_ref, cre_ref, cim_ref, apr_ref, api_ref, h0_ref, y_ref, fin_ref, fin_sc, t, *,
              kseq, has_h0):
    rows, half, pc = S5_ROWS, 2 * S5_STATE, 2 * S5_GROUP_CH
    k = lax.broadcasted_iota(jnp.int32, (rows, half), 0) % kseq
    shifts = [1 << i for i in range(kseq.bit_length() - 1)]
    xs = [u_ref[pl.ds(s, rows, stride=S5_T), :].astype(BF16) for s in range(S5_T)]
    ys = []
    for p in range(S5_OCT_PAIRS):
        lanes = slice(p * half, (p + 1) * half)
        xp = jnp.concatenate([x[:, p * pc:(p + 1) * pc] for x in xs], axis=1)
        acc = None
        for d in range(2):
            inc = _dot(xp, bs_ref[d, p])
            sr, si = inc[:, :half], inc[:, half:]
            if has_h0:
                nseq = rows // kseq
                h0r, h0i = h0_ref[d, 0, t * nseq, :, lanes], h0_ref[d, 1, t * nseq, :, lanes]
                seq = lax.broadcasted_iota(jnp.int32, (rows, half), 0) // kseq
                for j in range(1, nseq):
                    h0r = jnp.where(seq == j, h0_ref[d, 0, t * nseq + j, :, lanes], h0r)
                    h0i = jnp.where(seq == j, h0_ref[d, 1, t * nseq + j, :, lanes], h0i)
                ar, ai = apr_ref[d, 0:1, lanes], api_ref[d, 0:1, lanes]
                first = (k == 0) if d == 0 else (k == kseq - 1)
                sr = sr + jnp.where(first, ar * h0r - ai * h0i, 0.0)
                si = si + jnp.where(first, ar * h0i + ai * h0r, 0.0)
            else:
                h0r = h0i = 0.0
            for i, sh in enumerate(shifts):
                ar, ai = apr_ref[d, i:i + 1, lanes], api_ref[d, i:i + 1, lanes]
                ok = (k >= sh) if d == 0 else (k < kseq - sh)
                amt = sh if d == 0 else rows - sh
                rr = jnp.where(ok, pltpu.roll(sr, amt, 0), 0.0)
                ri = jnp.where(ok, pltpu.roll(si, amt, 0), 0.0)
                sr, si = sr + ar * rr - ai * ri, si + ar * ri + ai * rr
            inner = (k >= 1) if d == 0 else (k < kseq - 1)
            amt = 1 if d == 0 else rows - 1
            hr = jnp.where(inner, pltpu.roll(sr, amt, 0), h0r)
            hi = jnp.where(inner, pltpu.roll(si, amt, 0), h0i)
            yd = (_dot(xp, m_ref[d, p]) + _dot(hr.astype(BF16), cre_ref[d, p])
                  + _dot(hi.astype(BF16), cim_ref[d, p]))
            acc = yd if acc is None else acc + yd
            if not has_h0:
                last = kseq - 1 if d == 0 else 0
                fin_sc[0] = sr
                fin_sc[1] = si
                fin_ref[d, 0, :, lanes] = fin_sc[0, pl.ds(last, rows // kseq, stride=kseq), :]
                fin_ref[d, 1, :, lanes] = fin_sc[1, pl.ds(last, rows // kseq, stride=kseq), :]
        ys.append(acc)
    for j in range(S5_T):
        y_ref[pl.ds(j, rows, stride=S5_T), :] = jnp.concatenate([y[:, j * pc:(j + 1) * pc] for y in ys], axis=1)


def _s5_scan(u, ops, h0, l):
    nblk, nctx = N_TOK // S5_BLOCK_TOK, N_CTX_TOK // S5_BLOCK_TOK
    noct = S5_PAIRS // S5_OCT_PAIRS
    half, gn = 2 * S5_STATE, S5_GROUPS * S5_STATE
    pw = 2 * S5_T * S5_GROUP_CH
    nseq = S5_BLOCK_TOK // SEQ
    ngrp = 2 * S5_OCT_PAIRS
    C, N, T = S5_GROUP_CH, S5_STATE, S5_T
    gspec = lambda r, c: pl.BlockSpec((None, 2, ngrp, r, c), lambda q, t: (l, 0, q, 0, 0))
    lspec = lambda r: pl.BlockSpec((None, 2, r, S5_OCT_PAIRS * half), lambda q, t: (l, 0, 0, q))
    pspec = pl.BlockSpec((None, 2, T, S5_OCT_PAIRS, 1, half), lambda q, t: (l, 0, 0, q, 0, 0))
    return pl.pallas_call(
        _s5_kernel,
        grid=(noct, nblk),
        in_specs=[
            pl.BlockSpec((S5_BLOCK_TOK, 128), lambda q, t: (t, q)),
            gspec(C, T * C), gspec(C, N), gspec(C, N), gspec(N, C), gspec(N, C), pspec, pspec,
            gspec(N, T), gspec(N, T), lspec(S5_NPOW), lspec(S5_NPOW),
            pl.BlockSpec((2, 2, DEC_BATCH, 1, S5_OCT_PAIRS * half), lambda q, t: (0, 0, 0, 0, q)),
        ],
        out_shape=(jax.ShapeDtypeStruct((N_TOK, S5_CH), F32), jax.ShapeDtypeStruct((2, 2, BATCH, gn), F32)),
        out_specs=(pl.BlockSpec((S5_BLOCK_TOK, 128), lambda q, t: (t, q)),
                   pl.BlockSpec((2, 2, nseq, S5_OCT_PAIRS * half),
                                lambda q, t: (0, 0, jnp.minimum(t, nctx - 1), q))),
        scratch_shapes=[pltpu.VMEM((2, S5_ROWS, half), F32),
                        pltpu.VMEM((2, S5_OCT_PAIRS, pw, 2 * half), BF16), pltpu.VMEM((2, S5_OCT_PAIRS, pw, pw), BF16),
                        pltpu.VMEM((2, S5_OCT_PAIRS, half, pw), BF16), pltpu.VMEM((2, S5_OCT_PAIRS, half, pw), BF16)],
        compiler_params=pltpu.CompilerParams(
            dimension_semantics=("parallel", "arbitrary"), vmem_limit_bytes=VMEM_LIMIT),
        name="s5_scan",
    )(u, *[ops[k] for k in S5_OP_KEYS], h0.reshape(2, 2, DEC_BATCH, 1, gn))


def _route(logits):
    lane = lax.broadcasted_iota(jnp.int32, logits.shape, 1)
    big = jnp.int32(1 << 20)
    is_g = (lane >= MOE_EXPERTS) & (lane < MOE_EXPERTS + MOE_GROUPS)
    lg = jnp.where(is_g, logits, NEG_INF)
    gmax = jnp.max(lg, axis=-1, keepdims=True)
    gsel = jnp.min(jnp.where(is_g & (lg == gmax), lane, big), axis=-1, keepdims=True) - MOE_EXPERTS
    p_group = 1.0 / jnp.sum(jnp.where(is_g, jnp.exp(lg - gmax), 0.0), axis=-1, keepdims=True)
    in_grp = (lane < MOE_EXPERTS) & (lane // MOE_EXPERTS_PER_GROUP == gsel)
    le = jnp.where(in_grp, logits, NEG_INF)
    v1 = jnp.max(le, axis=-1, keepdims=True)
    i1 = jnp.min(jnp.where(in_grp & (le == v1), lane, big), axis=-1, keepdims=True)
    rest = in_grp & (lane != i1)
    le2 = jnp.where(rest, logits, NEG_INF)
    v2 = jnp.max(le2, axis=-1, keepdims=True)
    i2 = jnp.min(jnp.where(rest & (le2 == v2), lane, big), axis=-1, keepdims=True)
    e2 = jnp.exp(v2 - v1)
    w1 = 1.0 / (1.0 + e2)
    w2 = e2 / (1.0 + e2)
    comb = jnp.where(lane == i1, w1 * p_group, 0.0) + jnp.where(lane == i2, w2 * p_group, 0.0)
    a = jnp.minimum(i1, i2) - gsel * MOE_EXPERTS_PER_GROUP
    b = jnp.maximum(i1, i2) - gsel * MOE_EXPERTS_PER_GROUP
    pair = jnp.where(a == 0, b - 1, jnp.where(a == 1, jnp.where(b == 3, 3, 4), 5))
    cls = gsel * MOE_PAIRS + pair
    return jnp.where(lane == MOE_EXPERTS, cls.astype(F32), comb)


def _merge_kernel(xc_ref, xl_ref, oac_ref, oal_ref, obc_ref, obl_ref, u_ref, yc_ref, gates_ref, mod_ref, d_ref,
                  wglu_ref, wa_ref, wb_ref, wc_ref, wout_ref, g2_ref, wrh_ref, wrl_ref, br_ref, x1_ref, hx_ref,
                  *, tm):
    is_ctx = pl.program_id(0) < N_CTX_TOK // tm
    x = jnp.where(is_ctx, xc_ref[...], xl_ref[...])
    oa = jnp.where(is_ctx, oac_ref[...], oal_ref[...])
    ob = jnp.where(is_ctx, obc_ref[...], obl_ref[...])
    y = u_ref[...] * d_ref[...] + yc_ref[...]
    y = y * (0.5 * (1.0 + jnp.tanh(math.sqrt(2.0 / math.pi) * (y + 0.044715 * (y * y * y)))))
    oc = y * jax.nn.sigmoid(_dot(y.astype(BF16), wglu_ref[...]))
    gate = jax.nn.sigmoid(gates_ref[...].astype(F32))
    merged = (gate[:, :D_MODEL] * _dot(oa, wa_ref[...])
              + gate[:, D_MODEL:2 * D_MODEL] * _dot(ob, wb_ref[...])
              + gate[:, 2 * D_MODEL:] * _dot(oc.astype(BF16), wc_ref[...]))
    x1 = x + mod_ref[2:3, :] * _dot(merged.astype(BF16), wout_ref[...])
    x1_ref[...] = x1
    h2 = _rms(x1, g2_ref[...]) * (1.0 + mod_ref[4:5, :]) + mod_ref[3:4, :]
    h_hi, h_lo = _split_bf16(h2)
    logits = _dot(h_hi, wrh_ref[...]) + _dot(h_lo, wrh_ref[...]) + _dot(h_hi, wrl_ref[...]) + br_ref[...]
    half = D_MODEL // 2
    hx_ref[:, :half] = _pack_pairs(h2)
    hx_ref[:, half:] = pltpu.bitcast(_route(logits), jnp.uint32)


def _merge(x, oa_c, oa_l, ob_c, ob_l, u, yc, z, mod, d_s5, wglu, wa, wb, wc, wout, g2, wr_hi, wr_lo, br, l):
    tm = 512
    nctx = N_CTX_TOK // tm
    full = lambda r, c: pl.BlockSpec((r, c), lambda i: (0, 0))
    layer = lambda r, c: pl.BlockSpec((None, r, c), lambda i: (l, 0, 0))
    ctx_blk = lambda w: pl.BlockSpec((tm, w), lambda i: (jnp.minimum(i, nctx - 1), 0))
    lat_blk = lambda w, first=0: pl.BlockSpec((tm, w), lambda i: (jnp.maximum(i - nctx, 0) + first // tm, 0))
    x_ctx, x_lat, lat0 = _token_arrays(x)
    return pl.pallas_call(
        functools.partial(_merge_kernel, tm=tm),
        out_shape=(jax.ShapeDtypeStruct((N_TOK, D_MODEL), F32),
                   jax.ShapeDtypeStruct((N_TOK, MOE_ROW_WORDS), jnp.uint32)),
        grid=(N_TOK // tm,),
        in_specs=[
            ctx_blk(D_MODEL), lat_blk(D_MODEL, lat0),
            ctx_blk(NA_WIDTH), lat_blk(NA_WIDTH), ctx_blk(GQA_Q_WIDTH), lat_blk(GQA_Q_WIDTH),
            pl.BlockSpec((tm, S5_CH), lambda i: (i, 0)),
            pl.BlockSpec((tm, S5_CH), lambda i: (i, 0)),
            pl.BlockSpec((tm, N_BRANCH * D_MODEL), lambda i: (i, COL_GATES // (N_BRANCH * D_MODEL))),
            pl.BlockSpec((None, None, 6, D_MODEL), lambda i: (l, _cond_index(i, tm), 0, 0)),
            full(1, S5_CH), layer(S5_CH, S5_CH), layer(NA_WIDTH, D_MODEL), layer(GQA_Q_WIDTH, D_MODEL),
            layer(S5_CH, D_MODEL), layer(D_MODEL, D_MODEL), full(1, D_MODEL),
            full(D_MODEL, 128), full(D_MODEL, 128), full(1, 128),
        ],
        out_specs=(pl.BlockSpec((tm, D_MODEL), lambda i: (i, 0)),
                   pl.BlockSpec((tm, MOE_ROW_WORDS), lambda i: (i, 0))),
        compiler_params=pltpu.CompilerParams(
            dimension_semantics=("parallel",), vmem_limit_bytes=VMEM_LIMIT),
        name="merge",
    )(x_ctx, x_lat, oa_c, oa_l, ob_c, ob_l, u, yc, z, mod, d_s5, wglu, wa, wb, wc, wout, g2, wr_hi, wr_lo, br)


def _moe_plan(cls):
    ncls = MOE_GROUPS * MOE_PAIRS
    ntiles = MOE_ROWS // MOE_TILE
    onehot = (cls[:, None] == jnp.arange(ncls)[None, :]).astype(jnp.int32)
    rank = jnp.sum((jnp.cumsum(onehot, axis=0) - onehot) * onehot, axis=1)
    ccount = jnp.sum(onehot, axis=0).reshape(MOE_GROUPS, MOE_PAIRS)
    gcount = jnp.sum(ccount, axis=1)
    gpadded = (gcount + MOE_TILE - 1) // MOE_TILE * MOE_TILE
    gend = jnp.cumsum(gpadded)
    cstart = ((gend - gpadded)[:, None] + jnp.cumsum(ccount, axis=1) - ccount).reshape(ncls)
    cend = cstart + ccount.reshape(ncls)
    pos = jnp.sum(onehot * cstart[None, :], axis=1) + rank
    tile_row = jnp.arange(ntiles) * MOE_TILE
    tile_group = jnp.minimum(jnp.sum((tile_row[:, None] >= gend[None, :]).astype(jnp.int32), axis=1),
                             MOE_GROUPS - 1)
    member = np.zeros((ncls, MOE_GROUPS, MOE_EXPERTS_PER_GROUP), np.float32)
    for g in range(MOE_GROUPS):
        for p, pair in enumerate(MOE_PAIR_MEMBERS):
            member[g * MOE_PAIRS + p, g, list(pair)] = 1.0
    overlap = ((cstart[None, :] < tile_row[:, None] + MOE_TILE) & (cend[None, :] > tile_row[:, None])
               & (cend > cstart)[None, :]).astype(F32)
    need = jnp.einsum("tc,cge->tge", overlap, member)
    need = jnp.sum(need * (tile_group[:, None, None] == jnp.arange(MOE_GROUPS)[None, :, None]), axis=1)
    return (pos.astype(jnp.int32), tile_group.astype(jnp.int32), (need > 0).astype(jnp.int32).reshape(-1),
            (gend[-1:] // MOE_TILE).astype(jnp.int32))


def _dispatch_kernel(pos_ref, hx_ref, init_ref, out_ref, sem):
    del init_ref
    base = pl.program_id(0) * MOE_TOK_BLOCK

    def row_copy(r):
        return pltpu.make_async_copy(hx_ref.at[pl.ds(r, 1), :], out_ref.at[pl.ds(pos_ref[base + r], 1), :], sem)

    def start(r, c):
        row_copy(r).start()
        return c

    def wait(r, c):
        row_copy(r).wait()
        return c

    lax.fori_loop(0, MOE_TOK_BLOCK, start, 0, unroll=16)
    lax.fori_loop(0, MOE_TOK_BLOCK, wait, 0, unroll=True)


def _dispatch(hx, pos):
    return pl.pallas_call(
        _dispatch_kernel,
        grid_spec=pltpu.PrefetchScalarGridSpec(
            num_scalar_prefetch=1,
            grid=(N_TOK // MOE_TOK_BLOCK,),
            in_specs=[pl.BlockSpec((MOE_TOK_BLOCK, MOE_ROW_WORDS), lambda i, pos: (i, 0)),
                      pl.BlockSpec(memory_space=pl.ANY)],
            out_specs=pl.BlockSpec(memory_space=pl.ANY),
            scratch_shapes=[pltpu.SemaphoreType.DMA]),
        out_shape=jax.ShapeDtypeStruct((MOE_ROWS, MOE_ROW_WORDS), jnp.uint32),
        input_output_aliases={2: 0},
        compiler_params=pltpu.CompilerParams(dimension_semantics=("arbitrary",)),
        name="moe_dispatch",
    )(pos, hx, jnp.zeros((MOE_ROWS, MOE_ROW_WORDS), jnp.uint32))


def _experts_kernel(tg_ref, need_ref, nu_ref, hx_ref, wg_ref, wu_ref, wd_ref, y_ref):
    t = pl.program_id(0)
    y_ref[...] = jnp.zeros_like(y_ref)
    half = D_MODEL // 2
    for e in range(MOE_EXPERTS_PER_GROUP):
        @pl.when((t < nu_ref[0]) & (need_ref[t * MOE_EXPERTS_PER_GROUP + e] > 0))
        def _():
            h = _unpack_pairs(hx_ref[:, :half]).astype(BF16)
            comb = pltpu.bitcast(hx_ref[:, half:], F32)
            lane = lax.broadcasted_iota(jnp.int32, comb.shape, 1)
            ce = jnp.sum(jnp.where(lane == tg_ref[t] * MOE_EXPERTS_PER_GROUP + e, comb, 0.0), axis=-1,
                         keepdims=True)
            a = _dot(h, wg_ref[e])
            b = _dot(h, wu_ref[e])
            act = (a * jax.nn.sigmoid(a)) * b * ce
            y_ref[...] += _dot(act.astype(BF16), wd_ref[e])


def _experts(hs, tile_group, need, n_used, wg, wu, wd, l):
    wspec = lambda r, c: pl.BlockSpec((None, None, MOE_EXPERTS_PER_GROUP, r, c),
                                      lambda t, tg, need, nu: (l, tg[t], 0, 0, 0))
    return pl.pallas_call(
        _experts_kernel,
        grid_spec=pltpu.PrefetchScalarGridSpec(
            num_scalar_prefetch=3,
            grid=(MOE_ROWS // MOE_TILE,),
            in_specs=[pl.BlockSpec((MOE_TILE, MOE_ROW_WORDS), lambda t, tg, need, nu: (t, 0)),
                      wspec(D_MODEL, EXPERT_FF), wspec(D_MODEL, EXPERT_FF), wspec(EXPERT_FF, D_MODEL)],
            out_specs=pl.BlockSpec((MOE_TILE, D_MODEL), lambda t, tg, need, nu: (t, 0))),
        out_shape=jax.ShapeDtypeStruct((MOE_ROWS, D_MODEL), F32),
        compiler_params=pltpu.CompilerParams(
            dimension_semantics=("arbitrary",), vmem_limit_bytes=VMEM_LIMIT),
        name="moe_experts",
    )(tile_group, need, n_used, hs, wg, wu, wd)


def _combine_kernel(pos_ref, y_ref, x_ref, mod_ref, fg_ref, *rest, final):
    buf, sem = rest[-2:]
    i = pl.program_id(0)
    base = i * MOE_TOK_BLOCK

    def row_copy(r):
        return pltpu.make_async_copy(y_ref.at[pl.ds(pos_ref[base + r], 1), :], buf.at[pl.ds(r, 1), :], sem)

    def start(r, c):
        row_copy(r).start()
        return c

    def wait(r, c):
        row_copy(r).wait()
        return c

    lax.fori_loop(0, MOE_TOK_BLOCK, start, 0, unroll=16)
    lax.fori_loop(0, MOE_TOK_BLOCK, wait, 0, unroll=True)
    x2 = x_ref[...] + mod_ref[5:6, :] * buf[...]
    if not final:
        rest[0][...] = x2
    else:
        y = _rms(x2, fg_ref[...])
        is_ctx = i < N_CTX_TOK // MOE_TOK_BLOCK

        @pl.when(is_ctx)
        def _():
            rest[0][...] = y

        @pl.when(jnp.logical_not(is_ctx))
        def _():
            rest[1][...] = y


def _combine(ys, pos, x1, mod, fg, l, final):
    tm = MOE_TOK_BLOCK
    nctx = N_CTX_TOK // tm
    tok = pl.BlockSpec((tm, D_MODEL), lambda i, pos: (i, 0))
    out = jax.ShapeDtypeStruct((N_TOK, D_MODEL), F32)
    if final:
        out = (jax.ShapeDtypeStruct((N_CTX_TOK, D_MODEL), F32), jax.ShapeDtypeStruct((N_LAT_TOK, D_MODEL), F32))
        out_specs = (pl.BlockSpec((tm, D_MODEL), lambda i, pos: (jnp.minimum(i, nctx - 1), 0)),
                     pl.BlockSpec((tm, D_MODEL), lambda i, pos: (jnp.maximum(i - nctx, 0), 0)))
    else:
        out_specs = tok
    return pl.pallas_call(
        functools.partial(_combine_kernel, final=final),
        grid_spec=pltpu.PrefetchScalarGridSpec(
            num_scalar_prefetch=1,
            grid=(N_TOK // tm,),
            in_specs=[pl.BlockSpec(memory_space=pl.ANY), tok,
                      pl.BlockSpec((None, None, 6, D_MODEL), lambda i, pos: (l, _cond_index(i, tm), 0, 0)),
                      pl.BlockSpec((1, D_MODEL), lambda i, pos: (0, 0))],
            out_specs=out_specs,
            scratch_shapes=[pltpu.VMEM((tm, D_MODEL), F32), pltpu.SemaphoreType.DMA]),
        out_shape=out,
        compiler_params=pltpu.CompilerParams(
            dimension_semantics=("arbitrary",), vmem_limit_bytes=VMEM_LIMIT),
        name="moe_combine",
    )(pos, ys, x1, mod, fg)


def _pack_w_in(w):
    qkv = w[..., :COL_G_V + GQA_KV_WIDTH]
    u = w[..., COL_G_V + GQA_KV_WIDTH:COL_G_V + GQA_KV_WIDTH + S5_CH]
    gates = w[..., COL_G_V + GQA_KV_WIDTH + S5_CH:]
    pad = jnp.zeros(w.shape[:-1] + (COL_U - (COL_G_V + GQA_KV_WIDTH),), w.dtype)
    return jnp.concatenate([qkv, pad, u, gates], axis=-1).astype(BF16)


def kernel(x_prompt, x_sample, cache_na_kv, cache_gqa_kv, state_ssm, c, c_ctx, norm_g, w_ada, b_ada, w_in, na_rpb, gqa_sink, s5_a_re, s5_a_im, s5_log_dt, s5_b_re, s5_b_im, s5_c_re, s5_c_im, s5_d, s5_w_glu, w_branch_a, w_branch_b, w_branch_c, w_out, moe_w_group, moe_b_group, moe_w_expert, moe_b_expert, moe_w_gate, moe_w_up, moe_w_down, final_g):
    cond = jnp.zeros((N_COND, D_MODEL), F32).at[0].set(c_ctx.astype(F32)).at[1:1 + DEC_BATCH].set(c.astype(F32))
    mod = _ada_mod(cond, w_ada.astype(F32), b_ada.astype(F32))

    x = (x_prompt.astype(F32).reshape(N_CTX_TOK, D_MODEL), x_sample.astype(F32).reshape(N_LAT_TOK, D_MODEL))
    cache_na = cache_na_kv.reshape(DEC_BATCH, DEPTH, 2, PAST_LEN, NA_WIDTH)
    cache_gqa = cache_gqa_kv.reshape(DEC_BATCH, DEPTH, 2, PAST_LEN, GQA_KV_WIDTH)
    cos_d, sin_d = _rope_tables()
    cos_q, sin_q = jnp.tile(cos_d, (1, GQA_Q_HEADS)), jnp.tile(sin_d, (1, GQA_Q_HEADS))
    cos_k, sin_k = jnp.tile(cos_d, (1, GQA_KV_HEADS)), jnp.tile(sin_d, (1, GQA_KV_HEADS))
    gn = S5_GROUPS * S5_STATE
    fg = final_g.astype(F32).reshape(1, D_MODEL)

    w_in_packed = _pack_w_in(w_in)
    bf16_weights = [w.astype(BF16) for w in (s5_w_glu, w_branch_a, w_branch_b, w_branch_c, w_out)]
    grouped = lambda w: w.astype(BF16).reshape((DEPTH, MOE_GROUPS, MOE_EXPERTS_PER_GROUP) + w.shape[2:])
    moe_weights = [grouped(w) for w in (moe_w_gate, moe_w_up, moe_w_down)]
    ops = jax.vmap(_s5_operators)(s5_a_re, s5_a_im, s5_log_dt, s5_b_re, s5_b_im, s5_c_re, s5_c_im)

    na_list, gqa_list, ssm_list = [], [], []
    y = None
    for l in range(DEPTH):
        z, u, new_na, new_gqa = _inproj(x, norm_g[l, 0].astype(F32).reshape(1, D_MODEL), mod, w_in_packed, l)
        na_list.append(new_na.reshape(BATCH, 2, SEQ, NA_HEADS, HEAD_DIM))
        gqa_list.append(new_gqa.reshape(BATCH, 2, SEQ, GQA_KV_HEADS, HEAD_DIM))

        sink = gqa_sink[l].astype(F32)
        oa_c, ob_c = _ctx_attn(z, sink)
        oa_l = _na_lat(z, cache_na, _na_bias_tables(na_rpb[l]), l)
        ob_l = _swa_lat(z, cache_gqa, sink, cos_q, sin_q, cos_k, sin_k, l)

        h0 = state_ssm[:, l].astype(F32).reshape(DEC_BATCH, 2, 2, gn).transpose(1, 2, 0, 3)
        yc, fin = _s5_scan(u, ops, h0, l)
        ssm_list.append(fin.transpose(2, 0, 1, 3).reshape(BATCH, 2, 2, S5_GROUPS, S5_STATE).astype(x_prompt.dtype))

        wr = jnp.zeros((D_MODEL, 128), F32)
        wr = wr.at[:, :MOE_EXPERTS].set(moe_w_expert[l].astype(F32))
        wr = wr.at[:, MOE_EXPERTS:MOE_EXPERTS + MOE_GROUPS].set(moe_w_group[l].astype(F32))
        br = jnp.zeros((1, 128), F32)
        br = br.at[0, :MOE_EXPERTS].set(moe_b_expert[l].astype(F32))
        br = br.at[0, MOE_EXPERTS:MOE_EXPERTS + MOE_GROUPS].set(moe_b_group[l].astype(F32))
        wr_hi, wr_lo = _split_bf16(wr)
        x1, hx = _merge(
            x, oa_c, oa_l, ob_c, ob_l, u, yc, z, mod, s5_d[l].astype(F32).reshape(1, S5_CH), *bf16_weights,
            norm_g[l, 1].astype(F32).reshape(1, D_MODEL), wr_hi, wr_lo, br, l)
        cls = lax.bitcast_convert_type(hx[:, D_MODEL // 2 + MOE_EXPERTS], F32).astype(jnp.int32)
        pos, tile_group, need, n_used = _moe_plan(cls)
        ys = _experts(_dispatch(hx, pos), tile_group, need, n_used, *moe_weights, l)
        if l < DEPTH - 1:
            x = _combine(ys, pos, x1, mod, fg, l, False)
        else:
            y_ctx, y_lat = _combine(ys, pos, x1, mod, fg, l, True)

    return (y_ctx.reshape(BATCH, SEQ, D_MODEL), y_lat.reshape(DEC_BATCH, DEC_SEQ, D_MODEL),
            jnp.stack(na_list, axis=1), jnp.stack(gqa_list, axis=1), jnp.stack(ssm_list, axis=1))
```

```python
import functools
import math

import numpy as np
import jax
import jax.numpy as jnp
from jax import lax
from jax.experimental import pallas as pl
from jax.experimental.pallas import tpu as pltpu

F32 = jnp.float32
BF16 = jnp.bfloat16

D_MODEL = 1024
BATCH = 32
SEQ = 256
DEPTH = 2
DEC_BATCH = 4
DEC_SEQ = 2048
PAST_LEN = 512
GRID_W = 64
GRID_ROWS = DEC_SEQ // GRID_W
HEAD_DIM = 64
NA_HEADS = 8
NA_WIN_H = 8
NA_WIN_W = 16
GQA_Q_HEADS = 8
GQA_KV_HEADS = 2
GQA_GROUP = GQA_Q_HEADS // GQA_KV_HEADS
SWA_WINDOW = 128
SWA_BLOCK = 128
ROPE_THETA = 10000.0
S5_CH = 512
S5_GROUP_CH = 16
S5_GROUPS = S5_CH // S5_GROUP_CH
S5_STATE = 64
N_BRANCH = 3
NA_WIDTH = NA_HEADS * HEAD_DIM
GQA_Q_WIDTH = GQA_Q_HEADS * HEAD_DIM
GQA_KV_WIDTH = GQA_KV_HEADS * HEAD_DIM
MOE_GROUPS = 4
MOE_EXPERTS_PER_GROUP = 4
MOE_EXPERTS = MOE_GROUPS * MOE_EXPERTS_PER_GROUP
EXPERT_FF = 512
EPS = 1e-6
NEG_INF = -1e30

N_CTX_TOK = BATCH * SEQ
N_LAT_TOK = DEC_BATCH * DEC_SEQ
N_TOK = N_CTX_TOK + N_LAT_TOK
N_COND = 8

COL_NA_Q = 0
COL_NA_K = 512
COL_NA_V = 1024
COL_G_Q = 1536
COL_G_K = 2048
COL_G_V = 2176
COL_U = 2560
COL_GATES = 3072
Z_COLS = 6144

S5_T = 16
S5_PAIRS = S5_GROUPS // 2
S5_OCT_PAIRS = 4
S5_BLOCK_TOK = 4096
S5_ROWS = S5_BLOCK_TOK // S5_T
S5_NPOW = 7

MOE_ROW_WORDS = D_MODEL // 2 + 128
MOE_TILE = 512
MOE_ROWS = N_TOK + MOE_GROUPS * MOE_TILE
MOE_TOK_BLOCK = 1024
MOE_PAIR_MEMBERS = ((0, 1), (0, 2), (0, 3), (1, 3), (1, 2), (2, 3))
MOE_PAIRS = len(MOE_PAIR_MEMBERS)

CTX_HEADS_PER_DOT = 4
NA_HEADS_PER_DOT = 2
NA_QROWS = 4
NA_KROWS = 12

VMEM_LIMIT = 56 * 1024 * 1024


def _dot(a, b):
    return jnp.dot(a, b, preferred_element_type=F32)


def _dot_nt(a, b):
    return lax.dot_general(a, b, (((1,), (1,)), ((), ())), preferred_element_type=F32)


def _split_bf16(x):
    hi = x.astype(BF16)
    lo = (x - hi.astype(F32)).astype(BF16)
    return hi, lo


def _pack_pairs(x):
    w = x.shape[1] // 2
    xb = x.astype(BF16).astype(F32)
    return (pltpu.bitcast(xb[:, :w], jnp.uint32) & jnp.uint32(0xFFFF0000)) | (
        pltpu.bitcast(xb[:, w:], jnp.uint32) >> 16)


def _unpack_pairs(word):
    return jnp.concatenate([pltpu.bitcast(word & jnp.uint32(0xFFFF0000), F32),
                            pltpu.bitcast(word << 16, F32)], axis=1)


def _rms(x, g):
    return x * lax.rsqrt(jnp.mean(x * x, axis=-1, keepdims=True) + EPS) * g


def _cond_index(i, tm):
    nctx = N_CTX_TOK // tm
    return jnp.where(i < nctx, 0, 1 + ((i - nctx) * tm) // DEC_SEQ)


def _ada_kernel(c_ref, w_ref, b_ref, o_ref):
    c = c_ref[...]
    s = c * jax.nn.sigmoid(c)
    s_hi, s_lo = _split_bf16(s)
    w_hi, w_lo = _split_bf16(w_ref[...])
    o_ref[...] = _dot(s_hi, w_hi) + _dot(s_lo, w_hi) + _dot(s_hi, w_lo) + b_ref[...]


def _ada_mod(cond, w_ada, b_ada):
    tn = 1536
    n = 6 * D_MODEL
    out = pl.pallas_call(
        _ada_kernel,
        out_shape=jax.ShapeDtypeStruct((DEPTH, N_COND, n), F32),
        grid=(DEPTH, n // tn),
        in_specs=[
            pl.BlockSpec((N_COND, D_MODEL), lambda l, j: (0, 0)),
            pl.BlockSpec((None, D_MODEL, tn), lambda l, j: (l, 0, j)),
            pl.BlockSpec((None, 1, tn), lambda l, j: (l, 0, j)),
        ],
        out_specs=pl.BlockSpec((None, N_COND, tn), lambda l, j: (l, 0, j)),
        compiler_params=pltpu.CompilerParams(
            dimension_semantics=("parallel", "parallel"), vmem_limit_bytes=VMEM_LIMIT),
        name="ada_mod",
    )(cond, w_ada, b_ada.reshape(DEPTH, 1, n))
    return out.reshape(DEPTH, N_COND, 6, D_MODEL)


INPROJ_TM = 1024
INPROJ_TN = 1536


def _inproj_kernel(xc_ref, xl_ref, g_ref, mod_ref, w_ref, z_ref, u_ref, cna_ref, cgq_ref, h_sc):
    i, j = pl.program_id(0), pl.program_id(1)
    is_ctx = i < N_CTX_TOK // INPROJ_TM

    def norm_mod(x_ref):
        h = _rms(x_ref[...], g_ref[...]) * (1.0 + mod_ref[1:2, :]) + mod_ref[0:1, :]
        h_sc[...] = h.astype(BF16)

    @pl.when((j == 0) & is_ctx)
    def _():
        norm_mod(xc_ref)

    @pl.when((j == 0) & jnp.logical_not(is_ctx))
    def _():
        norm_mod(xl_ref)

    acc = _dot(h_sc[...], w_ref[...])
    z_ref[...] = acc.astype(BF16)

    def rows(b):
        return slice(b * SEQ, (b + 1) * SEQ)

    def cols(c, width):
        return slice(c % INPROJ_TN, c % INPROJ_TN + width)

    @pl.when(is_ctx & (j == COL_NA_K // INPROJ_TN))
    def _():
        for b in range(INPROJ_TM // SEQ):
            cna_ref[b, 0] = acc[rows(b), cols(COL_NA_K, NA_WIDTH)]
            cna_ref[b, 1] = acc[rows(b), cols(COL_NA_V, NA_WIDTH)]

    @pl.when(is_ctx & (j == COL_G_K // INPROJ_TN))
    def _():
        for b in range(INPROJ_TM // SEQ):
            cgq_ref[b, 0] = acc[rows(b), cols(COL_G_K, GQA_KV_WIDTH)]
            cgq_ref[b, 1] = acc[rows(b), cols(COL_G_V, GQA_KV_WIDTH)]

    @pl.when(j == COL_U // INPROJ_TN)
    def _():
        u_ref[...] = acc[:, cols(COL_U, S5_CH)]


def _token_arrays(x):
    return (x, x, N_CTX_TOK) if not isinstance(x, tuple) else (x[0], x[1], 0)


def _inproj(x, g, mod, w, l):
    tm, tn = INPROJ_TM, INPROJ_TN
    nb = tm // SEQ
    last_ctx = N_CTX_TOK // tm - 1
    x_ctx, x_lat, lat0 = _token_arrays(x)
    return pl.pallas_call(
        _inproj_kernel,
        out_shape=(jax.ShapeDtypeStruct((N_TOK, Z_COLS), BF16),
                   jax.ShapeDtypeStruct((N_TOK, S5_CH), F32),
                   jax.ShapeDtypeStruct((BATCH, 2, SEQ, NA_WIDTH), F32),
                   jax.ShapeDtypeStruct((BATCH, 2, SEQ, GQA_KV_WIDTH), F32)),
        grid=(N_TOK // tm, Z_COLS // tn),
        in_specs=[
            pl.BlockSpec((tm, D_MODEL), lambda i, j: (jnp.minimum(i, last_ctx), 0)),
            pl.BlockSpec((tm, D_MODEL), lambda i, j: (jnp.maximum(i - last_ctx - 1, 0) + lat0 // tm, 0)),
            pl.BlockSpec((1, D_MODEL), lambda i, j: (0, 0)),
            pl.BlockSpec((None, None, 6, D_MODEL), lambda i, j: (l, _cond_index(i, tm), 0, 0)),
            pl.BlockSpec((None, D_MODEL, tn), lambda i, j: (l, 0, j)),
        ],
        out_specs=(pl.BlockSpec((tm, tn), lambda i, j: (i, j)),
                   pl.BlockSpec((tm, S5_CH), lambda i, j: (i, 0)),
                   pl.BlockSpec((nb, 2, SEQ, NA_WIDTH), lambda i, j: (jnp.minimum(i, last_ctx), 0, 0, 0)),
                   pl.BlockSpec((nb, 2, SEQ, GQA_KV_WIDTH), lambda i, j: (jnp.minimum(i, last_ctx), 0, 0, 0))),
        scratch_shapes=[pltpu.VMEM((tm, D_MODEL), BF16)],
        compiler_params=pltpu.CompilerParams(
            dimension_semantics=("arbitrary", "arbitrary"), vmem_limit_bytes=VMEM_LIMIT),
        name="inproj",
    )(x_ctx, x_lat, g, mod, w)


def _block_diag_heads(x, nrep):
    t = x.shape[0]
    rows = lax.broadcasted_iota(jnp.int32, (nrep * t, nrep * HEAD_DIM), 0) // t
    lanes = lax.broadcasted_iota(jnp.int32, (nrep * t, nrep * HEAD_DIM), 1) // HEAD_DIM
    return jnp.where(rows == lanes, jnp.concatenate([x] * nrep, axis=0), jnp.zeros((), x.dtype))


def _per_head(cols, shape):
    head = lax.broadcasted_iota(jnp.int32, shape, 1) // HEAD_DIM
    out = cols[-1]
    for h in range(len(cols) - 2, -1, -1):
        out = jnp.where(head == h, cols[h], out)
    return out


def _attend_heads(q, k, v, sinks):
    nh = q.shape[1] // HEAD_DIM
    t = k.shape[0]
    s = _dot_nt(q * HEAD_DIM ** -0.5, _block_diag_heads(k, nh))
    ps, ds = [], []
    for h in range(nh):
        sh = s[:, h * t:(h + 1) * t]
        m = jnp.max(sh, axis=-1, keepdims=True)
        if sinks is not None:
            m = jnp.maximum(m, sinks[h])
        p = jnp.exp(sh - m)
        d = jnp.sum(p, axis=-1, keepdims=True)
        if sinks is not None:
            d = d + jnp.exp(sinks[h] - m)
        ps.append(p.astype(BF16))
        ds.append(d)
    o = _dot(jnp.concatenate(ps, axis=1), _block_diag_heads(v, nh))
    return o / _per_head(ds, o.shape)


def _ctx_attn_kernel(sink_ref, na_ref, gq_ref, gkv_ref, oa_ref, ob_ref):
    hb = CTX_HEADS_PER_DOT
    w = hb * HEAD_DIM
    na = na_ref[...]
    gq = gq_ref[...]
    gkv = gkv_ref[...]
    for g in range(NA_HEADS // hb):
        o = _attend_heads(na[:, COL_NA_Q + g * w:COL_NA_Q + (g + 1) * w],
                          na[:, COL_NA_K + g * w:COL_NA_K + (g + 1) * w],
                          na[:, COL_NA_V + g * w:COL_NA_V + (g + 1) * w], None)
        oa_ref[:, g * w:(g + 1) * w] = o.astype(BF16)
    for g in range(GQA_Q_HEADS // hb):
        kv_heads = [(g * hb + h) // GQA_GROUP for h in range(hb)]
        k = jnp.concatenate([gkv[:, kh * HEAD_DIM:(kh + 1) * HEAD_DIM] for kh in kv_heads], axis=1)
        v = jnp.concatenate([gkv[:, GQA_KV_WIDTH + kh * HEAD_DIM:GQA_KV_WIDTH + (kh + 1) * HEAD_DIM]
                             for kh in kv_heads], axis=1)
        o = _attend_heads(gq[:, g * w:(g + 1) * w], k, v, [sink_ref[g * hb + h] for h in range(hb)])
        ob_ref[:, g * w:(g + 1) * w] = o.astype(BF16)


def _ctx_attn(z, sink):
    out = jax.ShapeDtypeStruct((N_CTX_TOK, NA_WIDTH), BF16)
    return pl.pallas_call(
        _ctx_attn_kernel,
        out_shape=(out, out),
        grid=(BATCH,),
        in_specs=[
            pl.BlockSpec(memory_space=pltpu.SMEM),
            pl.BlockSpec((SEQ, 3 * NA_WIDTH), lambda b: (b, 0)),
            pl.BlockSpec((SEQ, GQA_Q_WIDTH), lambda b: (b, COL_G_Q // GQA_Q_WIDTH)),
            pl.BlockSpec((SEQ, 2 * GQA_KV_WIDTH), lambda b: (b, COL_G_K // (2 * GQA_KV_WIDTH))),
        ],
        out_specs=(pl.BlockSpec((SEQ, NA_WIDTH), lambda b: (b, 0)),
                   pl.BlockSpec((SEQ, GQA_Q_WIDTH), lambda b: (b, 0))),
        compiler_params=pltpu.CompilerParams(
            dimension_semantics=("parallel",), vmem_limit_bytes=VMEM_LIMIT),
        name="ctx_attn",
    )(sink, z, z, z)


def _na_bias_tables(rpb):
    n_dc = 2 * NA_WIN_W - 1
    qc = np.arange(GRID_W)
    kc = np.arange(GRID_W)
    cstart = np.clip(qc - NA_WIN_W // 2, 0, GRID_W - NA_WIN_W)
    col_ok = (kc[None, :] >= cstart[:, None]) & (kc[None, :] < cstart[:, None] + NA_WIN_W)
    dc = np.clip(kc[None, :] - qc[:, None] + (NA_WIN_W - 1), 0, n_dc - 1)
    onehot = ((np.arange(n_dc)[:, None, None] == dc[None]) & col_ok[None]).astype(np.float32)
    band = jnp.einsum("hrd,dqk->hrqk", rpb.astype(F32), onehot, precision=lax.Precision.HIGHEST)
    band = jnp.where(col_ok, band, NEG_INF)
    band = jnp.pad(band, ((0, 0), (1, 1), (0, 0), (0, 0)), constant_values=NEG_INF)
    return jnp.concatenate([band[:, :-1], band[:, 1:]], axis=-1)


def _na_lat_kernel(q_ref, k_ref, v_ref, ck_ref, cv_ref, bias_ref, rowmask_ref, o_ref):
    i = pl.program_id(1)
    scale = HEAD_DIM ** -0.5
    nk = NA_KROWS * GRID_W
    ks = jnp.clip(i * NA_QROWS - NA_WIN_H // 2, 0, GRID_ROWS - NA_KROWS)
    start = pl.multiple_of(ks * GRID_W, GRID_W)
    q = q_ref[...].astype(BF16)
    kw = k_ref[pl.ds(start, nk), :].astype(BF16)
    vw = v_ref[pl.ds(start, nk), :].astype(BF16)
    ck = ck_ref[...].astype(BF16)
    cv = cv_ref[...].astype(BF16)
    planes = []
    for qr in range(NA_QROWS):
        r = i * NA_QROWS + qr
        planes.append([jnp.clip(ks + 2 * m - r + NA_WIN_H, 0, 2 * NA_WIN_H - 1) for m in range(NA_KROWS // 2)])
    hb = NA_HEADS_PER_DOT
    w = hb * HEAD_DIM
    qrow = lax.broadcasted_iota(jnp.int32, (NA_QROWS * GRID_W, 128), 0) // GRID_W
    qsel = (lax.broadcasted_iota(jnp.int32, (NA_QROWS * GRID_W, 128), 1) == qrow).astype(BF16)
    rowmask = rowmask_ref[...]
    for g in range(NA_HEADS // hb):
        lanes = slice(g * w, (g + 1) * w)
        qg = jnp.concatenate([q[:, lanes] * scale, qsel], axis=1)
        kg = jnp.concatenate([_block_diag_heads(kw[:, lanes], hb), rowmask], axis=1)
        s_loc = _dot_nt(qg, kg)
        s_ctx = _dot_nt(q[:, lanes] * scale, _block_diag_heads(ck[:, lanes], hb))
        p_locs, p_ctxs, ds = [], [], []
        for hh in range(hb):
            bias = jnp.concatenate(
                [jnp.concatenate([bias_ref[g * hb + hh, plane] for plane in row], axis=1) for row in planes], axis=0)
            sl = s_loc[:, hh * nk:(hh + 1) * nk] + bias
            sc = s_ctx[:, hh * PAST_LEN:(hh + 1) * PAST_LEN]
            m = jnp.maximum(jnp.max(sl, axis=-1, keepdims=True), jnp.max(sc, axis=-1, keepdims=True))
            p_loc = jnp.exp(sl - m)
            p_ctx = jnp.exp(sc - m)
            ds.append(jnp.sum(p_loc, axis=-1, keepdims=True) + jnp.sum(p_ctx, axis=-1, keepdims=True))
            p_locs.append(p_loc.astype(BF16))
            p_ctxs.append(p_ctx.astype(BF16))
        o = (_dot(jnp.concatenate(p_locs, axis=1), _block_diag_heads(vw[:, lanes], hb))
             + _dot(jnp.concatenate(p_ctxs, axis=1), _block_diag_heads(cv[:, lanes], hb)))
        o_ref[:, lanes] = (o / _per_head(ds, o.shape)).astype(BF16)


def _na_row_masks():
    kh = min(NA_WIN_H, GRID_ROWS)
    nsteps = GRID_ROWS // NA_QROWS
    out = np.zeros((3, NA_HEADS_PER_DOT, NA_KROWS, GRID_W, 128), np.float32)
    for pat, i in enumerate((0, 1, nsteps - 1)):
        ks = min(max(i * NA_QROWS - NA_WIN_H // 2, 0), GRID_ROWS - NA_KROWS)
        for qr in range(NA_QROWS):
            r = i * NA_QROWS + qr
            st = min(max(r - kh // 2, 0), GRID_ROWS - kh)
            for kr in range(NA_KROWS):
                if not st <= ks + kr < st + kh:
                    out[pat, :, kr, :, qr] = NEG_INF
    return jnp.asarray(out.reshape(3, NA_HEADS_PER_DOT * NA_KROWS * GRID_W, 128), BF16)


def _na_lat(z, cache_na, bias, l):
    tq = NA_QROWS * GRID_W
    nsteps = GRID_ROWS // NA_QROWS
    lat_blk = N_CTX_TOK // DEC_SEQ
    return pl.pallas_call(
        _na_lat_kernel,
        out_shape=jax.ShapeDtypeStruct((N_LAT_TOK, NA_WIDTH), BF16),
        grid=(DEC_BATCH, nsteps),
        in_specs=[
            pl.BlockSpec((tq, NA_WIDTH), lambda b, i: (N_CTX_TOK // tq + b * nsteps + i, COL_NA_Q // NA_WIDTH)),
            pl.BlockSpec((DEC_SEQ, NA_WIDTH), lambda b, i: (lat_blk + b, COL_NA_K // NA_WIDTH)),
            pl.BlockSpec((DEC_SEQ, NA_WIDTH), lambda b, i: (lat_blk + b, COL_NA_V // NA_WIDTH)),
            pl.BlockSpec((None, None, None, PAST_LEN, NA_WIDTH), lambda b, i: (b, l, 0, 0, 0)),
            pl.BlockSpec((None, None, None, PAST_LEN, NA_WIDTH), lambda b, i: (b, l, 1, 0, 0)),
            pl.BlockSpec((NA_HEADS, 2 * NA_WIN_H, GRID_W, 2 * GRID_W), lambda b, i: (0, 0, 0, 0)),
            pl.BlockSpec((None, NA_HEADS_PER_DOT * NA_KROWS * GRID_W, 128),
                         lambda b, i: (jnp.where(i == 0, 0, jnp.where(i == nsteps - 1, 2, 1)), 0, 0)),
        ],
        out_specs=pl.BlockSpec((tq, NA_WIDTH), lambda b, i: (b * nsteps + i, 0)),
        compiler_params=pltpu.CompilerParams(
            dimension_semantics=("parallel", "arbitrary"), vmem_limit_bytes=VMEM_LIMIT),
        name="na_lat",
    )(z, z, z, cache_na, cache_na, bias, _na_row_masks())


def _rope_tables():
    nf = HEAD_DIM // 4
    t = jnp.arange(DEC_SEQ)
    pos = jnp.stack([t // GRID_W, t % GRID_W], axis=-1).astype(F32)
    inv = ROPE_THETA ** (-jnp.arange(nf, dtype=F32) / nf)
    ang = pos[:, :, None] * inv
    cos = jnp.cos(ang)
    sin = jnp.sin(ang)
    cos_d = jnp.stack([cos, cos], axis=2).reshape(DEC_SEQ, HEAD_DIM)
    sin_d = jnp.stack([-sin, sin], axis=2).reshape(DEC_SEQ, HEAD_DIM)
    return cos_d, sin_d


def _rope(x, cos, sin_signed):
    n = x.shape[-1]
    nf = HEAD_DIM // 4
    lane = lax.broadcasted_iota(jnp.int32, x.shape, 1)
    first_half = (lane // nf) % 2 == 0
    partner = jnp.where(first_half, pltpu.roll(x, n - nf, 1), pltpu.roll(x, nf, 1))
    return x * cos + partner * sin_signed


def _swa_lat_kernel(sink_ref, q_ref, kv_ref, cq_ref, sq_ref, ckt_ref, skt_ref, ck_ref, cv_ref, o_ref, k_sc, v_sc):
    n = pl.program_id(1)
    scale = HEAD_DIM ** -0.5
    nwin = 3 * SWA_BLOCK

    @pl.when(n == 0)
    def _():
        kv = kv_ref[...]
        k_sc[...] = _rope(kv[:, :GQA_KV_WIDTH].astype(F32), ckt_ref[...], skt_ref[...]).astype(BF16)
        v_sc[...] = kv[:, GQA_KV_WIDTH:].astype(BF16)

    q = (_rope(q_ref[...].astype(F32), cq_ref[...], sq_ref[...]) * scale).astype(BF16)
    start = pl.multiple_of(jnp.clip((n - 1) * SWA_BLOCK, 0, DEC_SEQ - nwin), SWA_BLOCK)
    kw = k_sc[pl.ds(start, nwin), :]
    vw = v_sc[pl.ds(start, nwin), :]
    ck = ck_ref[...].astype(BF16)
    cv = cv_ref[...].astype(BF16)
    rows = GQA_GROUP * SWA_BLOCK
    row = lax.broadcasted_iota(jnp.int32, (rows, nwin), 0)
    col = lax.broadcasted_iota(jnp.int32, (rows, nwin), 1)
    qpos = n * SWA_BLOCK + row % SWA_BLOCK
    kpos = start + col
    ok = jnp.abs(qpos - kpos) <= SWA_WINDOW
    grp = lax.broadcasted_iota(jnp.int32, (rows, 1), 0) // SWA_BLOCK
    for kh in range(GQA_KV_HEADS):
        sl = slice(kh * HEAD_DIM, (kh + 1) * HEAD_DIM)
        q4 = jnp.concatenate(
            [q[:, (kh * GQA_GROUP + g) * HEAD_DIM:(kh * GQA_GROUP + g + 1) * HEAD_DIM] for g in range(GQA_GROUP)],
            axis=0)
        sink = jnp.zeros((rows, 1), F32)
        for g in range(GQA_GROUP):
            sink = jnp.where(grp == g, sink_ref[kh * GQA_GROUP + g], sink)
        s_loc = jnp.where(ok, _dot_nt(q4, kw[:, sl]), NEG_INF)
        s_ctx = _dot_nt(q4, ck[:, sl])
        m = jnp.maximum(jnp.maximum(jnp.max(s_loc, axis=-1, keepdims=True),
                                    jnp.max(s_ctx, axis=-1, keepdims=True)), sink)
        p_loc = jnp.exp(s_loc - m)
        p_ctx = jnp.exp(s_ctx - m)
        d = (jnp.sum(p_loc, axis=-1, keepdims=True) + jnp.sum(p_ctx, axis=-1, keepdims=True)
             + jnp.exp(sink - m))
        o4 = (_dot(p_loc.astype(BF16), vw[:, sl]) + _dot(p_ctx.astype(BF16), cv[:, sl])) / d
        for g in range(GQA_GROUP):
            h = kh * GQA_GROUP + g
            o_ref[:, h * HEAD_DIM:(h + 1) * HEAD_DIM] = o4[g * SWA_BLOCK:(g + 1) * SWA_BLOCK].astype(BF16)


def _swa_lat(z, cache_gqa, sink, cos_q, sin_q, cos_k, sin_k, l):
    nb = DEC_SEQ // SWA_BLOCK
    lat_blk = N_CTX_TOK // DEC_SEQ
    return pl.pallas_call(
        _swa_lat_kernel,
        out_shape=jax.ShapeDtypeStruct((N_LAT_TOK, GQA_Q_WIDTH), BF16),
        grid=(DEC_BATCH, nb),
        in_specs=[
            pl.BlockSpec(memory_space=pltpu.SMEM),
            pl.BlockSpec((SWA_BLOCK, GQA_Q_WIDTH),
                         lambda b, n: (N_CTX_TOK // SWA_BLOCK + b * nb + n, COL_G_Q // GQA_Q_WIDTH)),
            pl.BlockSpec((DEC_SEQ, 2 * GQA_KV_WIDTH), lambda b, n: (lat_blk + b, COL_G_K // (2 * GQA_KV_WIDTH))),
            pl.BlockSpec((SWA_BLOCK, GQA_Q_WIDTH), lambda b, n: (n, 0)),
            pl.BlockSpec((SWA_BLOCK, GQA_Q_WIDTH), lambda b, n: (n, 0)),
            pl.BlockSpec((DEC_SEQ, GQA_KV_WIDTH), lambda b, n: (0, 0)),
            pl.BlockSpec((DEC_SEQ, GQA_KV_WIDTH), lambda b, n: (0, 0)),
            pl.BlockSpec((None, None, None, PAST_LEN, GQA_KV_WIDTH), lambda b, n: (b, l, 0, 0, 0)),
            pl.BlockSpec((None, None, None, PAST_LEN, GQA_KV_WIDTH), lambda b, n: (b, l, 1, 0, 0)),
        ],
        out_specs=pl.BlockSpec((SWA_BLOCK, GQA_Q_WIDTH), lambda b, n: (b * nb + n, 0)),
        scratch_shapes=[pltpu.VMEM((DEC_SEQ, GQA_KV_WIDTH), BF16), pltpu.VMEM((DEC_SEQ, GQA_KV_WIDTH), BF16)],
        compiler_params=pltpu.CompilerParams(
            dimension_semantics=("parallel", "arbitrary"), vmem_limit_bytes=VMEM_LIMIT),
        name="swa_lat",
    )(sink, z, z, cos_q, sin_q, cos_k, sin_k, cache_gqa, cache_gqa)


S5_OP_KEYS = ("kt", "bt_re", "bt_im", "ct_re", "ct_im", "pin_re", "pin_im", "po_re", "po_im", "apr", "api")


def _s5_operators(a_re, a_im, log_dt, b_re, b_im, c_re, c_im):
    T, G, N, C = S5_T, S5_GROUPS, S5_STATE, S5_GROUP_CH
    tau = jnp.arange(T + 1, dtype=F32)
    out = {k: [] for k in S5_OP_KEYS}
    for d in range(2):
        A = lax.complex(a_re[d].astype(F32), a_im[d].astype(F32))
        dt = jnp.exp(log_dt[d].astype(F32))[:, None]
        a_bar = jnp.exp(A * dt)
        pw = jnp.exp((A * dt)[None] * tau[:, None, None])
        b_bar = ((a_bar - 1.0) / A)[..., None] * lax.complex(b_re[d].astype(F32), b_im[d].astype(F32))
        c_mat = lax.complex(c_re[d].astype(F32), c_im[d].astype(F32))
        kern = jnp.einsum("gon,tgn,gni->gtoi", c_mat, pw[:T], b_bar, precision=lax.Precision.HIGHEST).real
        kern = lax.optimization_barrier(kern).transpose(0, 3, 1, 2)
        if d == 0:
            p_in = pw[:T][::-1]
            p_out = pw[1:T + 1]
        else:
            p_in = pw[:T]
            p_out = pw[1:T + 1][::-1]
            kern = kern[:, :, ::-1]
        bt, ct, po = b_bar.transpose(0, 2, 1), c_mat.transpose(0, 2, 1), p_out.transpose(1, 2, 0)
        pw2 = jnp.exp((A * dt)[None] * (T * 2.0 ** jnp.arange(S5_NPOW, dtype=F32))[:, None, None])
        vals = (kern.reshape(G, C, T * C), bt.real, bt.imag, ct.real, ct.imag,
                p_in.real.reshape(T, S5_PAIRS, 1, 2 * N), p_in.imag.reshape(T, S5_PAIRS, 1, 2 * N), po.real, po.imag,
                pw2.real.reshape(S5_NPOW, G * N), pw2.imag.reshape(S5_NPOW, G * N))
        for k, v in zip(S5_OP_KEYS, vals):
            out[k].append(v)
    return {k: jnp.stack(v) for k, v in out.items()}


def _s5_expand(kt_ref, btr_ref, bti_ref, ctr_ref, cti_ref, pir_ref, pii_ref, por_ref, poi_ref,
               bs_sc, m_sc, cre_sc, cim_sc):
    T, C, N = S5_T, S5_GROUP_CH, S5_STATE
    pc, half, width = 2 * C, 2 * N, 2 * T * C
    lane = lax.broadcasted_iota(jnp.int32, (pc, width), 1)
    zc, zn, zo = jnp.zeros((C, C), F32), jnp.zeros((C, N), F32), jnp.zeros((N, C), F32)

    def own_half(x, z, g):
        return [x, z] if g == 0 else [z, x]

    def expand_pair(p, d):
        k0, bx_re, bx_im, co_re, co_im = [], [], [], [], []
        for g in range(2):
            gl = 2 * p + g
            ktg = kt_ref[d, gl]
            k0.append(jnp.concatenate(
                [blk for lag in range(T) for blk in own_half(ktg[:, lag * C:(lag + 1) * C], zc, g)], axis=1))
            bx_re.append(jnp.concatenate(own_half(btr_ref[d, gl], zn, g), axis=1))
            bx_im.append(jnp.concatenate(own_half(bti_ref[d, gl], zn, g), axis=1))
            cr, ci = ctr_ref[d, gl], cti_ref[d, gl]
            pr_all, pi_all = por_ref[d, gl], poi_ref[d, gl]
            re_p, im_p = [], []
            for j in range(T):
                pr, pi = pr_all[:, j:j + 1], pi_all[:, j:j + 1]
                re_p += own_half(cr * pr - ci * pi, zo, g)
                im_p += own_half(cr * pi + ci * pr, zo, g)
            co_re.append(jnp.concatenate(re_p, axis=1))
            co_im.append(jnp.concatenate(im_p, axis=1))
        k0 = jnp.concatenate(k0, axis=0)
        bx_re, bx_im = jnp.concatenate(bx_re, axis=0), jnp.concatenate(bx_im, axis=0)
        for s in range(T):
            if d == 0:
                sh = pc * s
                blk = jnp.where(lane >= sh, pltpu.roll(k0, sh, 1), 0.0) if sh else k0
            else:
                sh = pc * (T - 1 - s)
                blk = jnp.where(lane < width - sh, pltpu.roll(k0, width - sh, 1), 0.0) if sh else k0
            m_sc[d, p, s * pc:(s + 1) * pc, :] = blk.astype(BF16)
            pr, pi = pir_ref[d, s, p], pii_ref[d, s, p]
            bs_sc[d, p, s * pc:(s + 1) * pc, :] = jnp.concatenate(
                [pr * bx_re - pi * bx_im, pr * bx_im + pi * bx_re], axis=1).astype(BF16)
        cre_sc[d, p] = jnp.concatenate(co_re, axis=0).astype(BF16)
        cim_sc[d, p] = (-jnp.concatenate(co_im, axis=0)).astype(BF16)

    for d in range(2):
        lax.fori_loop(0, S5_OCT_PAIRS, lambda p, c, d=d: (expand_pair(p, d), c)[1], 0)


def _s5_kernel(u_ref, kt_ref, btr_ref, bti_ref, ctr_ref, cti_ref, pir_ref, pii_ref, por_ref, poi_ref, apr_ref,
               api_ref, h0_ref, y_ref, fin_ref, fin_sc, bs_sc, m_sc, cre_sc, cim_sc):
    t = pl.program_id(1)
    nctx = N_CTX_TOK // S5_BLOCK_TOK

    @pl.when(t == 0)
    def _():
        _s5_expand(kt_ref, btr_ref, bti_ref, ctr_ref, cti_ref, pir_ref, pii_ref, por_ref, poi_ref,
                   bs_sc, m_sc, cre_sc, cim_sc)

    refs = (u_ref, bs_sc, m_sc, cre_sc, cim_sc, apr_ref, api_ref, h0_ref, y_ref, fin_ref, fin_sc)

    @pl.when(t < nctx)
    def _():
        _s5_block(*refs, t, kseq=SEQ // S5_T, has_h0=False)

    @pl.when(t >= nctx)
    def _():
        _s5_block(*refs, t - nctx, kseq=DEC_SEQ // S5_T, has_h0=True)


def _s5_block(u_ref, bs_ref, m_ref, cre_ref, cim_ref, apr_ref, api_ref, h0_ref, y_ref, fin_ref, fin_sc, t, *,
              kseq, has_h0):
    rows, half, pc = S5_ROWS, 2 * S5_STATE, 2 * S5_GROUP_CH
    k = lax.broadcasted_iota(jnp.int32, (rows, half), 0) % kseq
    shifts = [1 << i for i in range(kseq.bit_length() - 1)]
    xs = [u_ref[pl.ds(s, rows, stride=S5_T), :].astype(BF16) for s in range(S5_T)]
    ys = []
    for p in range(S5_OCT_PAIRS):
        lanes = slice(p * half, (p + 1) * half)
        xp = jnp.concatenate([x[:, p * pc:(p + 1) * pc] for x in xs], axis=1)
        acc = None
        for d in range(2):
            inc = _dot(xp, bs_ref[d, p])
            sr, si = inc[:, :half], inc[:, half:]
            if has_h0:
                nseq = rows // kseq
                h0r, h0i = h0_ref[d, 0, t * nseq, :, lanes], h0_ref[d, 1, t * nseq, :, lanes]
                seq = lax.broadcasted_iota(jnp.int32, (rows, half), 0) // kseq
                for j in range(1, nseq):
                    h0r = jnp.where(seq == j, h0_ref[d, 0, t * nseq + j, :, lanes], h0r)
                    h0i = jnp.where(seq == j, h0_ref[d, 1, t * nseq + j, :, lanes], h0i)
                ar, ai = apr_ref[d, 0:1, lanes], api_ref[d, 0:1, lanes]
                first = (k == 0) if d == 0 else (k == kseq - 1)
                sr = sr + jnp.where(first, ar * h0r - ai * h0i, 0.0)
                si = si + jnp.where(first, ar * h0i + ai * h0r, 0.0)
            else:
                h0r = h0i = 0.0
            for i, sh in enumerate(shifts):
                ar, ai = apr_ref[d, i:i + 1, lanes], api_ref[d, i:i + 1, lanes]
                ok = (k >= sh) if d == 0 else (k < kseq - sh)
                amt = sh if d == 0 else rows - sh
                rr = jnp.where(ok, pltpu.roll(sr, amt, 0), 0.0)
                ri = jnp.where(ok, pltpu.roll(si, amt, 0), 0.0)
                sr, si = sr + ar * rr - ai * ri, si + ar * ri + ai * rr
            inner = (k >= 1) if d == 0 else (k < kseq - 1)
            amt = 1 if d == 0 else rows - 1
            hr = jnp.where(inner, pltpu.roll(sr, amt, 0), h0r)
            hi = jnp.where(inner, pltpu.roll(si, amt, 0), h0i)
            yd = (_dot(xp, m_ref[d, p]) + _dot(hr.astype(BF16), cre_ref[d, p])
                  + _dot(hi.astype(BF16), cim_ref[d, p]))
            acc = yd if acc is None else acc + yd
            if not has_h0:
                last = kseq - 1 if d == 0 else 0
                fin_sc[0] = sr
                fin_sc[1] = si
                fin_ref[d, 0, :, lanes] = fin_sc[0, pl.ds(last, rows // kseq, stride=kseq), :]
                fin_ref[d, 1, :, lanes] = fin_sc[1, pl.ds(last, rows // kseq, stride=kseq), :]
        ys.append(acc)
    for j in range(S5_T):
        y_ref[pl.ds(j, rows, stride=S5_T), :] = jnp.concatenate([y[:, j * pc:(j + 1) * pc] for y in ys], axis=1)


def _s5_scan(u, ops, h0, l):
    nblk, nctx = N_TOK // S5_BLOCK_TOK, N_CTX_TOK // S5_BLOCK_TOK
    noct = S5_PAIRS // S5_OCT_PAIRS
    half, gn = 2 * S5_STATE, S5_GROUPS * S5_STATE
    pw = 2 * S5_T * S5_GROUP_CH
    nseq = S5_BLOCK_TOK // SEQ
    ngrp = 2 * S5_OCT_PAIRS
    C, N, T = S5_GROUP_CH, S5_STATE, S5_T
    gspec = lambda r, c: pl.BlockSpec((None, 2, ngrp, r, c), lambda q, t: (l, 0, q, 0, 0))
    lspec = lambda r: pl.BlockSpec((None, 2, r, S5_OCT_PAIRS * half), lambda q, t: (l, 0, 0, q))
    pspec = pl.BlockSpec((None, 2, T, S5_OCT_PAIRS, 1, half), lambda q, t: (l, 0, 0, q, 0, 0))
    return pl.pallas_call(
        _s5_kernel,
        grid=(noct, nblk),
        in_specs=[
            pl.BlockSpec((S5_BLOCK_TOK, 128), lambda q, t: (t, q)),
            gspec(C, T * C), gspec(C, N), gspec(C, N), gspec(N, C), gspec(N, C), pspec, pspec,
            gspec(N, T), gspec(N, T), lspec(S5_NPOW), lspec(S5_NPOW),
            pl.BlockSpec((2, 2, DEC_BATCH, 1, S5_OCT_PAIRS * half), lambda q, t: (0, 0, 0, 0, q)),
        ],
        out_shape=(jax.ShapeDtypeStruct((N_TOK, S5_CH), F32), jax.ShapeDtypeStruct((2, 2, BATCH, gn), F32)),
        out_specs=(pl.BlockSpec((S5_BLOCK_TOK, 128), lambda q, t: (t, q)),
                   pl.BlockSpec((2, 2, nseq, S5_OCT_PAIRS * half),
                                lambda q, t: (0, 0, jnp.minimum(t, nctx - 1), q))),
        scratch_shapes=[pltpu.VMEM((2, S5_ROWS, half), F32),
                        pltpu.VMEM((2, S5_OCT_PAIRS, pw, 2 * half), BF16), pltpu.VMEM((2, S5_OCT_PAIRS, pw, pw), BF16),
                        pltpu.VMEM((2, S5_OCT_PAIRS, half, pw), BF16), pltpu.VMEM((2, S5_OCT_PAIRS, half, pw), BF16)],
        compiler_params=pltpu.CompilerParams(
            dimension_semantics=("parallel", "arbitrary"), vmem_limit_bytes=VMEM_LIMIT),
        name="s5_scan",
    )(u, *[ops[k] for k in S5_OP_KEYS], h0.reshape(2, 2, DEC_BATCH, 1, gn))


def _route(logits):
    lane = lax.broadcasted_iota(jnp.int32, logits.shape, 1)
    big = jnp.int32(1 << 20)
    is_g = (lane >= MOE_EXPERTS) & (lane < MOE_EXPERTS + MOE_GROUPS)
    lg = jnp.where(is_g, logits, NEG_INF)
    gmax = jnp.max(lg, axis=-1, keepdims=True)
    gsel = jnp.min(jnp.where(is_g & (lg == gmax), lane, big), axis=-1, keepdims=True) - MOE_EXPERTS
    p_group = 1.0 / jnp.sum(jnp.where(is_g, jnp.exp(lg - gmax), 0.0), axis=-1, keepdims=True)
    in_grp = (lane < MOE_EXPERTS) & (lane // MOE_EXPERTS_PER_GROUP == gsel)
    le = jnp.where(in_grp, logits, NEG_INF)
    v1 = jnp.max(le, axis=-1, keepdims=True)
    i1 = jnp.min(jnp.where(in_grp & (le == v1), lane, big), axis=-1, keepdims=True)
    rest = in_grp & (lane != i1)
    le2 = jnp.where(rest, logits, NEG_INF)
    v2 = jnp.max(le2, axis=-1, keepdims=True)
    i2 = jnp.min(jnp.where(rest & (le2 == v2), lane, big), axis=-1, keepdims=True)
    e2 = jnp.exp(v2 - v1)
    w1 = 1.0 / (1.0 + e2)
    w2 = e2 / (1.0 + e2)
    comb = jnp.where(lane == i1, w1 * p_group, 0.0) + jnp.where(lane == i2, w2 * p_group, 0.0)
    a = jnp.minimum(i1, i2) - gsel * MOE_EXPERTS_PER_GROUP
    b = jnp.maximum(i1, i2) - gsel * MOE_EXPERTS_PER_GROUP
    pair = jnp.where(a == 0, b - 1, jnp.where(a == 1, jnp.where(b == 3, 3, 4), 5))
    cls = gsel * MOE_PAIRS + pair
    return jnp.where(lane == MOE_EXPERTS, cls.astype(F32), comb)


def _merge_kernel(xc_ref, xl_ref, oac_ref, oal_ref, obc_ref, obl_ref, u_ref, yc_ref, gates_ref, mod_ref, d_ref,
                  wglu_ref, wa_ref, wb_ref, wc_ref, wout_ref, g2_ref, wrh_ref, wrl_ref, br_ref, x1_ref, hx_ref,
                  *, tm):
    is_ctx = pl.program_id(0) < N_CTX_TOK // tm
    x = jnp.where(is_ctx, xc_ref[...], xl_ref[...])
    oa = jnp.where(is_ctx, oac_ref[...], oal_ref[...])
    ob = jnp.where(is_ctx, obc_ref[...], obl_ref[...])
    y = u_ref[...] * d_ref[...] + yc_ref[...]
    y = y * (0.5 * (1.0 + jnp.tanh(math.sqrt(2.0 / math.pi) * (y + 0.044715 * (y * y * y)))))
    oc = y * jax.nn.sigmoid(_dot(y.astype(BF16), wglu_ref[...]))
    gate = jax.nn.sigmoid(gates_ref[...].astype(F32))
    merged = (gate[:, :D_MODEL] * _dot(oa, wa_ref[...])
              + gate[:, D_MODEL:2 * D_MODEL] * _dot(ob, wb_ref[...])
              + gate[:, 2 * D_MODEL:] * _dot(oc.astype(BF16), wc_ref[...]))
    x1 = x + mod_ref[2:3, :] * _dot(merged.astype(BF16), wout_ref[...])
    x1_ref[...] = x1
    h2 = _rms(x1, g2_ref[...]) * (1.0 + mod_ref[4:5, :]) + mod_ref[3:4, :]
    h_hi, h_lo = _split_bf16(h2)
    logits = _dot(h_hi, wrh_ref[...]) + _dot(h_lo, wrh_ref[...]) + _dot(h_hi, wrl_ref[...]) + br_ref[...]
    half = D_MODEL // 2
    hx_ref[:, :half] = _pack_pairs(h2)
    hx_ref[:, half:] = pltpu.bitcast(_route(logits), jnp.uint32)


def _merge(x, oa_c, oa_l, ob_c, ob_l, u, yc, z, mod, d_s5, wglu, wa, wb, wc, wout, g2, wr_hi, wr_lo, br, l):
    tm = 512
    nctx = N_CTX_TOK // tm
    full = lambda r, c: pl.BlockSpec((r, c), lambda i: (0, 0))
    layer = lambda r, c: pl.BlockSpec((None, r, c), lambda i: (l, 0, 0))
    ctx_blk = lambda w: pl.BlockSpec((tm, w), lambda i: (jnp.minimum(i, nctx - 1), 0))
    lat_blk = lambda w, first=0: pl.BlockSpec((tm, w), lambda i: (jnp.maximum(i - nctx, 0) + first // tm, 0))
    x_ctx, x_lat, lat0 = _token_arrays(x)
    return pl.pallas_call(
        functools.partial(_merge_kernel, tm=tm),
        out_shape=(jax.ShapeDtypeStruct((N_TOK, D_MODEL), F32),
                   jax.ShapeDtypeStruct((N_TOK, MOE_ROW_WORDS), jnp.uint32)),
        grid=(N_TOK // tm,),
        in_specs=[
            ctx_blk(D_MODEL), lat_blk(D_MODEL, lat0),
            ctx_blk(NA_WIDTH), lat_blk(NA_WIDTH), ctx_blk(GQA_Q_WIDTH), lat_blk(GQA_Q_WIDTH),
            pl.BlockSpec((tm, S5_CH), lambda i: (i, 0)),
            pl.BlockSpec((tm, S5_CH), lambda i: (i, 0)),
            pl.BlockSpec((tm, N_BRANCH * D_MODEL), lambda i: (i, COL_GATES // (N_BRANCH * D_MODEL))),
            pl.BlockSpec((None, None, 6, D_MODEL), lambda i: (l, _cond_index(i, tm), 0, 0)),
            full(1, S5_CH), layer(S5_CH, S5_CH), layer(NA_WIDTH, D_MODEL), layer(GQA_Q_WIDTH, D_MODEL),
            layer(S5_CH, D_MODEL), layer(D_MODEL, D_MODEL), full(1, D_MODEL),
            full(D_MODEL, 128), full(D_MODEL, 128), full(1, 128),
        ],
        out_specs=(pl.BlockSpec((tm, D_MODEL), lambda i: (i, 0)),
                   pl.BlockSpec((tm, MOE_ROW_WORDS), lambda i: (i, 0))),
        compiler_params=pltpu.CompilerParams(
            dimension_semantics=("parallel",), vmem_limit_bytes=VMEM_LIMIT),
        name="merge",
    )(x_ctx, x_lat, oa_c, oa_l, ob_c, ob_l, u, yc, z, mod, d_s5, wglu, wa, wb, wc, wout, g2, wr_hi, wr_lo, br)


def _moe_plan(cls):
    ncls = MOE_GROUPS * MOE_PAIRS
    ntiles = MOE_ROWS // MOE_TILE
    onehot = (cls[:, None] == jnp.arange(ncls)[None, :]).astype(jnp.int32)
    rank = jnp.sum((jnp.cumsum(onehot, axis=0) - onehot) * onehot, axis=1)
    ccount = jnp.sum(onehot, axis=0).reshape(MOE_GROUPS, MOE_PAIRS)
    gcount = jnp.sum(ccount, axis=1)
    gpadded = (gcount + MOE_TILE - 1) // MOE_TILE * MOE_TILE
    gend = jnp.cumsum(gpadded)
    cstart = ((gend - gpadded)[:, None] + jnp.cumsum(ccount, axis=1) - ccount).reshape(ncls)
    cend = cstart + ccount.reshape(ncls)
    pos = jnp.sum(onehot * cstart[None, :], axis=1) + rank
    tile_row = jnp.arange(ntiles) * MOE_TILE
    tile_group = jnp.minimum(jnp.sum((tile_row[:, None] >= gend[None, :]).astype(jnp.int32), axis=1),
                             MOE_GROUPS - 1)
    member = np.zeros((ncls, MOE_GROUPS, MOE_EXPERTS_PER_GROUP), np.float32)
    for g in range(MOE_GROUPS):
        for p, pair in enumerate(MOE_PAIR_MEMBERS):
            member[g * MOE_PAIRS + p, g, list(pair)] = 1.0
    overlap = ((cstart[None, :] < tile_row[:, None] + MOE_TILE) & (cend[None, :] > tile_row[:, None])
               & (cend > cstart)[None, :]).astype(F32)
    need = jnp.einsum("tc,cge->tge", overlap, member)
    need = jnp.sum(need * (tile_group[:, None, None] == jnp.arange(MOE_GROUPS)[None, :, None]), axis=1)
    return (pos.astype(jnp.int32), tile_group.astype(jnp.int32), (need > 0).astype(jnp.int32).reshape(-1),
            (gend[-1:] // MOE_TILE).astype(jnp.int32))


def _dispatch_kernel(pos_ref, hx_ref, init_ref, out_ref, sem):
    del init_ref
    base = pl.program_id(0) * MOE_TOK_BLOCK

    def row_copy(r):
        return pltpu.make_async_copy(hx_ref.at[pl.ds(r, 1), :], out_ref.at[pl.ds(pos_ref[base + r], 1), :], sem)

    def start(r, c):
        row_copy(r).start()
        return c

    def wait(r, c):
        row_copy(r).wait()
        return c

    lax.fori_loop(0, MOE_TOK_BLOCK, start, 0, unroll=True)
    lax.fori_loop(0, MOE_TOK_BLOCK, wait, 0, unroll=True)


def _dispatch(hx, pos):
    return pl.pallas_call(
        _dispatch_kernel,
        grid_spec=pltpu.PrefetchScalarGridSpec(
            num_scalar_prefetch=1,
            grid=(N_TOK // MOE_TOK_BLOCK,),
            in_specs=[pl.BlockSpec((MOE_TOK_BLOCK, MOE_ROW_WORDS), lambda i, pos: (i, 0)),
                      pl.BlockSpec(memory_space=pl.ANY)],
            out_specs=pl.BlockSpec(memory_space=pl.ANY),
            scratch_shapes=[pltpu.SemaphoreType.DMA]),
        out_shape=jax.ShapeDtypeStruct((MOE_ROWS, MOE_ROW_WORDS), jnp.uint32),
        input_output_aliases={2: 0},
        compiler_params=pltpu.CompilerParams(dimension_semantics=("arbitrary",)),
        name="moe_dispatch",
    )(pos, hx, jnp.zeros((MOE_ROWS, MOE_ROW_WORDS), jnp.uint32))


def _experts_kernel(tg_ref, need_ref, nu_ref, hx_ref, wg_ref, wu_ref, wd_ref, y_ref):
    t = pl.program_id(0)
    y_ref[...] = jnp.zeros_like(y_ref)
    half = D_MODEL // 2
    for e in range(MOE_EXPERTS_PER_GROUP):
        @pl.when((t < nu_ref[0]) & (need_ref[t * MOE_EXPERTS_PER_GROUP + e] > 0))
        def _():
            h = _unpack_pairs(hx_ref[:, :half]).astype(BF16)
            comb = pltpu.bitcast(hx_ref[:, half:], F32)
            lane = lax.broadcasted_iota(jnp.int32, comb.shape, 1)
            ce = jnp.sum(jnp.where(lane == tg_ref[t] * MOE_EXPERTS_PER_GROUP + e, comb, 0.0), axis=-1,
                         keepdims=True)
            a = _dot(h, wg_ref[e])
            b = _dot(h, wu_ref[e])
            act = (a * jax.nn.sigmoid(a)) * b * ce
            y_ref[...] += _dot(act.astype(BF16), wd_ref[e])


def _experts(hs, tile_group, need, n_used, wg, wu, wd, l):
    wspec = lambda r, c: pl.BlockSpec((None, None, MOE_EXPERTS_PER_GROUP, r, c),
                                      lambda t, tg, need, nu: (l, tg[t], 0, 0, 0))
    return pl.pallas_call(
        _experts_kernel,
        grid_spec=pltpu.PrefetchScalarGridSpec(
            num_scalar_prefetch=3,
            grid=(MOE_ROWS // MOE_TILE,),
            in_specs=[pl.BlockSpec((MOE_TILE, MOE_ROW_WORDS), lambda t, tg, need, nu: (t, 0)),
                      wspec(D_MODEL, EXPERT_FF), wspec(D_MODEL, EXPERT_FF), wspec(EXPERT_FF, D_MODEL)],
            out_specs=pl.BlockSpec((MOE_TILE, D_MODEL), lambda t, tg, need, nu: (t, 0))),
        out_shape=jax.ShapeDtypeStruct((MOE_ROWS, D_MODEL), F32),
        compiler_params=pltpu.CompilerParams(
            dimension_semantics=("arbitrary",), vmem_limit_bytes=VMEM_LIMIT),
        name="moe_experts",
    )(tile_group, need, n_used, hs, wg, wu, wd)


def _combine_kernel(pos_ref, y_ref, x_ref, mod_ref, fg_ref, *rest, final):
    buf, sem = rest[-2:]
    i = pl.program_id(0)
    base = i * MOE_TOK_BLOCK

    def row_copy(r):
        return pltpu.make_async_copy(y_ref.at[pl.ds(pos_ref[base + r], 1), :], buf.at[pl.ds(r, 1), :], sem)

    def start(r, c):
        row_copy(r).start()
        return c

    def wait(r, c):
        row_copy(r).wait()
        return c

    lax.fori_loop(0, MOE_TOK_BLOCK, start, 0, unroll=True)
    lax.fori_loop(0, MOE_TOK_BLOCK, wait, 0, unroll=True)
    x2 = x_ref[...] + mod_ref[5:6, :] * buf[...]
    if not final:
        rest[0][...] = x2
    else:
        y = _rms(x2, fg_ref[...])
        is_ctx = i < N_CTX_TOK // MOE_TOK_BLOCK

        @pl.when(is_ctx)
        def _():
            rest[0][...] = y

        @pl.when(jnp.logical_not(is_ctx))
        def _():
            rest[1][...] = y


def _combine(ys, pos, x1, mod, fg, l, final):
    tm = MOE_TOK_BLOCK
    nctx = N_CTX_TOK // tm
    tok = pl.BlockSpec((tm, D_MODEL), lambda i, pos: (i, 0))
    out = jax.ShapeDtypeStruct((N_TOK, D_MODEL), F32)
    if final:
        out = (jax.ShapeDtypeStruct((N_CTX_TOK, D_MODEL), F32), jax.ShapeDtypeStruct((N_LAT_TOK, D_MODEL), F32))
        out_specs = (pl.BlockSpec((tm, D_MODEL), lambda i, pos: (jnp.minimum(i, nctx - 1), 0)),
                     pl.BlockSpec((tm, D_MODEL), lambda i, pos: (jnp.maximum(i - nctx, 0), 0)))
    else:
        out_specs = tok
    return pl.pallas_call(
        functools.partial(_combine_kernel, final=final),
        grid_spec=pltpu.PrefetchScalarGridSpec(
            num_scalar_prefetch=1,
            grid=(N_TOK // tm,),
            in_specs=[pl.BlockSpec(memory_space=pl.ANY), tok,
                      pl.BlockSpec((None, None, 6, D_MODEL), lambda i, pos: (l, _cond_index(i, tm), 0, 0)),
                      pl.BlockSpec((1, D_MODEL), lambda i, pos: (0, 0))],
            out_specs=out_specs,
            scratch_shapes=[pltpu.VMEM((tm, D_MODEL), F32), pltpu.SemaphoreType.DMA]),
        out_shape=out,
        compiler_params=pltpu.CompilerParams(
            dimension_semantics=("arbitrary",), vmem_limit_bytes=VMEM_LIMIT),
        name="moe_combine",
    )(pos, ys, x1, mod, fg)


def _pack_w_in(w):
    qkv = w[..., :COL_G_V + GQA_KV_WIDTH]
    u = w[..., COL_G_V + GQA_KV_WIDTH:COL_G_V + GQA_KV_WIDTH + S5_CH]
    gates = w[..., COL_G_V + GQA_KV_WIDTH + S5_CH:]
    pad = jnp.zeros(w.shape[:-1] + (COL_U - (COL_G_V + GQA_KV_WIDTH),), w.dtype)
    return jnp.concatenate([qkv, pad, u, gates], axis=-1).astype(BF16)


def kernel(x_prompt, x_sample, cache_na_kv, cache_gqa_kv, state_ssm, c, c_ctx, norm_g, w_ada, b_ada, w_in, na_rpb, gqa_sink, s5_a_re, s5_a_im, s5_log_dt, s5_b_re, s5_b_im, s5_c_re, s5_c_im, s5_d, s5_w_glu, w_branch_a, w_branch_b, w_branch_c, w_out, moe_w_group, moe_b_group, moe_w_expert, moe_b_expert, moe_w_gate, moe_w_up, moe_w_down, final_g):
    cond = jnp.zeros((N_COND, D_MODEL), F32).at[0].set(c_ctx.astype(F32)).at[1:1 + DEC_BATCH].set(c.astype(F32))
    mod = _ada_mod(cond, w_ada.astype(F32), b_ada.astype(F32))

    x = (x_prompt.astype(F32).reshape(N_CTX_TOK, D_MODEL), x_sample.astype(F32).reshape(N_LAT_TOK, D_MODEL))
    cache_na = cache_na_kv.reshape(DEC_BATCH, DEPTH, 2, PAST_LEN, NA_WIDTH)
    cache_gqa = cache_gqa_kv.reshape(DEC_BATCH, DEPTH, 2, PAST_LEN, GQA_KV_WIDTH)
    cos_d, sin_d = _rope_tables()
    cos_q, sin_q = jnp.tile(cos_d, (1, GQA_Q_HEADS)), jnp.tile(sin_d, (1, GQA_Q_HEADS))
    cos_k, sin_k = jnp.tile(cos_d, (1, GQA_KV_HEADS)), jnp.tile(sin_d, (1, GQA_KV_HEADS))
    gn = S5_GROUPS * S5_STATE
    fg = final_g.astype(F32).reshape(1, D_MODEL)

    w_in_packed = _pack_w_in(w_in)
    bf16_weights = [w.astype(BF16) for w in (s5_w_glu, w_branch_a, w_branch_b, w_branch_c, w_out)]
    grouped = lambda w: w.astype(BF16).reshape((DEPTH, MOE_GROUPS, MOE_EXPERTS_PER_GROUP) + w.shape[2:])
    moe_weights = [grouped(w) for w in (moe_w_gate, moe_w_up, moe_w_down)]
    ops = jax.vmap(_s5_operators)(s5_a_re, s5_a_im, s5_log_dt, s5_b_re, s5_b_im, s5_c_re, s5_c_im)

    na_list, gqa_list, ssm_list = [], [], []
    y = None
    for l in range(DEPTH):
        z, u, new_na, new_gqa = _inproj(x, norm_g[l, 0].astype(F32).reshape(1, D_MODEL), mod, w_in_packed, l)
        na_list.append(new_na.reshape(BATCH, 2, SEQ, NA_HEADS, HEAD_DIM))
        gqa_list.append(new_gqa.reshape(BATCH, 2, SEQ, GQA_KV_HEADS, HEAD_DIM))

        sink = gqa_sink[l].astype(F32)
        oa_c, ob_c = _ctx_attn(z, sink)
        oa_l = _na_lat(z, cache_na, _na_bias_tables(na_rpb[l]), l)
        ob_l = _swa_lat(z, cache_gqa, sink, cos_q, sin_q, cos_k, sin_k, l)

        h0 = state_ssm[:, l].astype(F32).reshape(DEC_BATCH, 2, 2, gn).transpose(1, 2, 0, 3)
        yc, fin = _s5_scan(u, ops, h0, l)
        ssm_list.append(fin.transpose(2, 0, 1, 3).reshape(BATCH, 2, 2, S5_GROUPS, S5_STATE).astype(x_prompt.dtype))

        wr = jnp.zeros((D_MODEL, 128), F32)
        wr = wr.at[:, :MOE_EXPERTS].set(moe_w_expert[l].astype(F32))
        wr = wr.at[:, MOE_EXPERTS:MOE_EXPERTS + MOE_GROUPS].set(moe_w_group[l].astype(F32))
        br = jnp.zeros((1, 128), F32)
        br = br.at[0, :MOE_EXPERTS].set(moe_b_expert[l].astype(F32))
        br = br.at[0, MOE_EXPERTS:MOE_EXPERTS + MOE_GROUPS].set(moe_b_group[l].astype(F32))
        wr_hi, wr_lo = _split_bf16(wr)
        x1, hx = _merge(
            x, oa_c, oa_l, ob_c, ob_l, u, yc, z, mod, s5_d[l].astype(F32).reshape(1, S5_CH), *bf16_weights,
            norm_g[l, 1].astype(F32).reshape(1, D_MODEL), wr_hi, wr_lo, br, l)
        cls = lax.bitcast_convert_type(hx[:, D_MODEL // 2 + MOE_EXPERTS], F32).astype(jnp.int32)
        pos, tile_group, need, n_used = _moe_plan(cls)
        ys = _experts(_dispatch(hx, pos), tile_group, need, n_used, *moe_weights, l)
        if l < DEPTH - 1:
            x = _combine(ys, pos, x1, mod, fg, l, False)
        else:
            y_ctx, y_lat = _combine(ys, pos, x1, mod, fg, l, True)

    return (y_ctx.reshape(BATCH, SEQ, D_MODEL), y_lat.reshape(DEC_BATCH, DEC_SEQ, D_MODEL),
            jnp.stack(na_list, axis=1), jnp.stack(gqa_list, axis=1), jnp.stack(ssm_list, axis=1))
```

```python
import functools
import math

import numpy as np
import jax
import jax.numpy as jnp
from jax import lax
from jax.experimental import pallas as pl
from jax.experimental.pallas import tpu as pltpu

F32 = jnp.float32
BF16 = jnp.bfloat16

D_MODEL = 1024
BATCH = 32
SEQ = 256
DEPTH = 2
DEC_BATCH = 4
DEC_SEQ = 2048
PAST_LEN = 512
GRID_W = 64
GRID_ROWS = DEC_SEQ // GRID_W
HEAD_DIM = 64
NA_HEADS = 8
NA_WIN_H = 8
NA_WIN_W = 16
GQA_Q_HEADS = 8
GQA_KV_HEADS = 2
GQA_GROUP = GQA_Q_HEADS // GQA_KV_HEADS
SWA_WINDOW = 128
SWA_BLOCK = 128
ROPE_THETA = 10000.0
S5_CH = 512
S5_GROUP_CH = 16
S5_GROUPS = S5_CH // S5_GROUP_CH
S5_STATE = 64
N_BRANCH = 3
NA_WIDTH = NA_HEADS * HEAD_DIM
GQA_Q_WIDTH = GQA_Q_HEADS * HEAD_DIM
GQA_KV_WIDTH = GQA_KV_HEADS * HEAD_DIM
MOE_GROUPS = 4
MOE_EXPERTS_PER_GROUP = 4
MOE_EXPERTS = MOE_GROUPS * MOE_EXPERTS_PER_GROUP
EXPERT_FF = 512
EPS = 1e-6
NEG_INF = -1e30

N_CTX_TOK = BATCH * SEQ
N_LAT_TOK = DEC_BATCH * DEC_SEQ
N_TOK = N_CTX_TOK + N_LAT_TOK
N_COND = 8

COL_NA_Q = 0
COL_NA_K = 512
COL_NA_V = 1024
COL_G_Q = 1536
COL_G_K = 2048
COL_G_V = 2176
COL_U = 2560
COL_GATES = 3072
Z_COLS = 6144

S5_T = 16
S5_PAIRS = S5_GROUPS // 2
S5_OCT_PAIRS = 4
S5_BLOCK_TOK = 4096
S5_ROWS = S5_BLOCK_TOK // S5_T
S5_NPOW = 7

MOE_ROW_WORDS = D_MODEL // 2 + 128
MOE_TILE = 512
MOE_ROWS = N_TOK + MOE_GROUPS * MOE_TILE
MOE_TOK_BLOCK = 1024
MOE_PAIR_MEMBERS = ((0, 1), (0, 2), (0, 3), (1, 3), (1, 2), (2, 3))
MOE_PAIRS = len(MOE_PAIR_MEMBERS)

CTX_HEADS_PER_DOT = 4
NA_HEADS_PER_DOT = 2
NA_QROWS = 4
NA_KROWS = 12

VMEM_LIMIT = 56 * 1024 * 1024


def _dot(a, b):
    return jnp.dot(a, b, preferred_element_type=F32)


def _dot_nt(a, b):
    return lax.dot_general(a, b, (((1,), (1,)), ((), ())), preferred_element_type=F32)


def _split_bf16(x):
    hi = x.astype(BF16)
    lo = (x - hi.astype(F32)).astype(BF16)
    return hi, lo


def _pack_pairs(x):
    w = x.shape[1] // 2
    xb = x.astype(BF16).astype(F32)
    return (pltpu.bitcast(xb[:, :w], jnp.uint32) & jnp.uint32(0xFFFF0000)) | (
        pltpu.bitcast(xb[:, w:], jnp.uint32) >> 16)


def _unpack_pairs(word):
    return jnp.concatenate([pltpu.bitcast(word & jnp.uint32(0xFFFF0000), F32),
                            pltpu.bitcast(word << 16, F32)], axis=1)


def _rms(x, g):
    return x * lax.rsqrt(jnp.mean(x * x, axis=-1, keepdims=True) + EPS) * g


def _cond_index(i, tm):
    nctx = N_CTX_TOK // tm
    return jnp.where(i < nctx, 0, 1 + ((i - nctx) * tm) // DEC_SEQ)


def _ada_kernel(c_ref, w_ref, b_ref, o_ref):
    c = c_ref[...]
    s = c * jax.nn.sigmoid(c)
    s_hi, s_lo = _split_bf16(s)
    w_hi, w_lo = _split_bf16(w_ref[...])
    o_ref[...] = _dot(s_hi, w_hi) + _dot(s_lo, w_hi) + _dot(s_hi, w_lo) + b_ref[...]


def _ada_mod(cond, w_ada, b_ada):
    tn = 1536
    n = 6 * D_MODEL
    out = pl.pallas_call(
        _ada_kernel,
        out_shape=jax.ShapeDtypeStruct((DEPTH, N_COND, n), F32),
        grid=(DEPTH, n // tn),
        in_specs=[
            pl.BlockSpec((N_COND, D_MODEL), lambda l, j: (0, 0)),
            pl.BlockSpec((None, D_MODEL, tn), lambda l, j: (l, 0, j)),
            pl.BlockSpec((None, 1, tn), lambda l, j: (l, 0, j)),
        ],
        out_specs=pl.BlockSpec((None, N_COND, tn), lambda l, j: (l, 0, j)),
        compiler_params=pltpu.CompilerParams(
            dimension_semantics=("parallel", "parallel"), vmem_limit_bytes=VMEM_LIMIT),
        name="ada_mod",
    )(cond, w_ada, b_ada.reshape(DEPTH, 1, n))
    return out.reshape(DEPTH, N_COND, 6, D_MODEL)


INPROJ_TM = 1024
INPROJ_TN = 1536


def _inproj_kernel(xc_ref, xl_ref, g_ref, mod_ref, w_ref, z_ref, u_ref, cna_ref, cgq_ref, h_sc):
    i, j = pl.program_id(0), pl.program_id(1)
    is_ctx = i < N_CTX_TOK // INPROJ_TM

    def norm_mod(x_ref):
        h = _rms(x_ref[...], g_ref[...]) * (1.0 + mod_ref[1:2, :]) + mod_ref[0:1, :]
        h_sc[...] = h.astype(BF16)

    @pl.when((j == 0) & is_ctx)
    def _():
        norm_mod(xc_ref)

    @pl.when((j == 0) & jnp.logical_not(is_ctx))
    def _():
        norm_mod(xl_ref)

    acc = _dot(h_sc[...], w_ref[...])
    z_ref[...] = acc.astype(BF16)

    def rows(b):
        return slice(b * SEQ, (b + 1) * SEQ)

    def cols(c, width):
        return slice(c % INPROJ_TN, c % INPROJ_TN + width)

    @pl.when(is_ctx & (j == COL_NA_K // INPROJ_TN))
    def _():
        for b in range(INPROJ_TM // SEQ):
            cna_ref[b, 0] = acc[rows(b), cols(COL_NA_K, NA_WIDTH)]
            cna_ref[b, 1] = acc[rows(b), cols(COL_NA_V, NA_WIDTH)]

    @pl.when(is_ctx & (j == COL_G_K // INPROJ_TN))
    def _():
        for b in range(INPROJ_TM // SEQ):
            cgq_ref[b, 0] = acc[rows(b), cols(COL_G_K, GQA_KV_WIDTH)]
            cgq_ref[b, 1] = acc[rows(b), cols(COL_G_V, GQA_KV_WIDTH)]

    @pl.when(j == COL_U // INPROJ_TN)
    def _():
        u_ref[...] = acc[:, cols(COL_U, S5_CH)]


def _token_arrays(x):
    return (x, x, N_CTX_TOK) if not isinstance(x, tuple) else (x[0], x[1], 0)


def _inproj(x, g, mod, w, l):
    tm, tn = INPROJ_TM, INPROJ_TN
    nb = tm // SEQ
    last_ctx = N_CTX_TOK // tm - 1
    x_ctx, x_lat, lat0 = _token_arrays(x)
    return pl.pallas_call(
        _inproj_kernel,
        out_shape=(jax.ShapeDtypeStruct((N_TOK, Z_COLS), BF16),
                   jax.ShapeDtypeStruct((N_TOK, S5_CH), F32),
                   jax.ShapeDtypeStruct((BATCH, 2, SEQ, NA_WIDTH), F32),
                   jax.ShapeDtypeStruct((BATCH, 2, SEQ, GQA_KV_WIDTH), F32)),
        grid=(N_TOK // tm, Z_COLS // tn),
        in_specs=[
            pl.BlockSpec((tm, D_MODEL), lambda i, j: (jnp.minimum(i, last_ctx), 0)),
            pl.BlockSpec((tm, D_MODEL), lambda i, j: (jnp.maximum(i - last_ctx - 1, 0) + lat0 // tm, 0)),
            pl.BlockSpec((1, D_MODEL), lambda i, j: (0, 0)),
            pl.BlockSpec((None, None, 6, D_MODEL), lambda i, j: (l, _cond_index(i, tm), 0, 0)),
            pl.BlockSpec((None, D_MODEL, tn), lambda i, j: (l, 0, j)),
        ],
        out_specs=(pl.BlockSpec((tm, tn), lambda i, j: (i, j)),
                   pl.BlockSpec((tm, S5_CH), lambda i, j: (i, 0)),
                   pl.BlockSpec((nb, 2, SEQ, NA_WIDTH), lambda i, j: (jnp.minimum(i, last_ctx), 0, 0, 0)),
                   pl.BlockSpec((nb, 2, SEQ, GQA_KV_WIDTH), lambda i, j: (jnp.minimum(i, last_ctx), 0, 0, 0))),
        scratch_shapes=[pltpu.VMEM((tm, D_MODEL), BF16)],
        compiler_params=pltpu.CompilerParams(
            dimension_semantics=("arbitrary", "arbitrary"), vmem_limit_bytes=VMEM_LIMIT),
        name="inproj",
    )(x_ctx, x_lat, g, mod, w)


def _block_diag_heads(x, nrep):
    t = x.shape[0]
    rows = lax.broadcasted_iota(jnp.int32, (nrep * t, nrep * HEAD_DIM), 0) // t
    lanes = lax.broadcasted_iota(jnp.int32, (nrep * t, nrep * HEAD_DIM), 1) // HEAD_DIM
    return jnp.where(rows == lanes, jnp.concatenate([x] * nrep, axis=0), jnp.zeros((), x.dtype))


def _per_head(cols, shape):
    head = lax.broadcasted_iota(jnp.int32, shape, 1) // HEAD_DIM
    out = cols[-1]
    for h in range(len(cols) - 2, -1, -1):
        out = jnp.where(head == h, cols[h], out)
    return out


def _attend_heads(q, k, v, sinks):
    nh = q.shape[1] // HEAD_DIM
    t = k.shape[0]
    s = _dot_nt(q * HEAD_DIM ** -0.5, _block_diag_heads(k, nh))
    ps, ds = [], []
    for h in range(nh):
        sh = s[:, h * t:(h + 1) * t]
        m = jnp.max(sh, axis=-1, keepdims=True)
        if sinks is not None:
            m = jnp.maximum(m, sinks[h])
        p = jnp.exp(sh - m)
        d = jnp.sum(p, axis=-1, keepdims=True)
        if sinks is not None:
            d = d + jnp.exp(sinks[h] - m)
        ps.append(p.astype(BF16))
        ds.append(d)
    o = _dot(jnp.concatenate(ps, axis=1), _block_diag_heads(v, nh))
    return o / _per_head(ds, o.shape)


def _ctx_attn_kernel(sink_ref, na_ref, gq_ref, gkv_ref, oa_ref, ob_ref):
    hb = CTX_HEADS_PER_DOT
    w = hb * HEAD_DIM
    na = na_ref[...]
    gq = gq_ref[...]
    gkv = gkv_ref[...]
    for g in range(NA_HEADS // hb):
        o = _attend_heads(na[:, COL_NA_Q + g * w:COL_NA_Q + (g + 1) * w],
                          na[:, COL_NA_K + g * w:COL_NA_K + (g + 1) * w],
                          na[:, COL_NA_V + g * w:COL_NA_V + (g + 1) * w], None)
        oa_ref[:, g * w:(g + 1) * w] = o.astype(BF16)
    for g in range(GQA_Q_HEADS // hb):
        kv_heads = [(g * hb + h) // GQA_GROUP for h in range(hb)]
        k = jnp.concatenate([gkv[:, kh * HEAD_DIM:(kh + 1) * HEAD_DIM] for kh in kv_heads], axis=1)
        v = jnp.concatenate([gkv[:, GQA_KV_WIDTH + kh * HEAD_DIM:GQA_KV_WIDTH + (kh + 1) * HEAD_DIM]
                             for kh in kv_heads], axis=1)
        o = _attend_heads(gq[:, g * w:(g + 1) * w], k, v, [sink_ref[g * hb + h] for h in range(hb)])
        ob_ref[:, g * w:(g + 1) * w] = o.astype(BF16)


def _ctx_attn(z, sink):
    out = jax.ShapeDtypeStruct((N_CTX_TOK, NA_WIDTH), BF16)
    return pl.pallas_call(
        _ctx_attn_kernel,
        out_shape=(out, out),
        grid=(BATCH,),
        in_specs=[
            pl.BlockSpec(memory_space=pltpu.SMEM),
            pl.BlockSpec((SEQ, 3 * NA_WIDTH), lambda b: (b, 0)),
            pl.BlockSpec((SEQ, GQA_Q_WIDTH), lambda b: (b, COL_G_Q // GQA_Q_WIDTH)),
            pl.BlockSpec((SEQ, 2 * GQA_KV_WIDTH), lambda b: (b, COL_G_K // (2 * GQA_KV_WIDTH))),
        ],
        out_specs=(pl.BlockSpec((SEQ, NA_WIDTH), lambda b: (b, 0)),
                   pl.BlockSpec((SEQ, GQA_Q_WIDTH), lambda b: (b, 0))),
        compiler_params=pltpu.CompilerParams(
            dimension_semantics=("parallel",), vmem_limit_bytes=VMEM_LIMIT),
        name="ctx_attn",
    )(sink, z, z, z)


def _na_bias_tables(rpb):
    n_dc = 2 * NA_WIN_W - 1
    qc = np.arange(GRID_W)
    kc = np.arange(GRID_W)
    cstart = np.clip(qc - NA_WIN_W // 2, 0, GRID_W - NA_WIN_W)
    col_ok = (kc[None, :] >= cstart[:, None]) & (kc[None, :] < cstart[:, None] + NA_WIN_W)
    dc = np.clip(kc[None, :] - qc[:, None] + (NA_WIN_W - 1), 0, n_dc - 1)
    onehot = ((np.arange(n_dc)[:, None, None] == dc[None]) & col_ok[None]).astype(np.float32)
    band = jnp.einsum("hrd,dqk->hrqk", rpb.astype(F32), onehot, precision=lax.Precision.HIGHEST)
    band = jnp.where(col_ok, band, NEG_INF)
    band = jnp.pad(band, ((0, 0), (1, 1), (0, 0), (0, 0)), constant_values=NEG_INF)
    return jnp.concatenate([band[:, :-1], band[:, 1:]], axis=-1)


def _na_lat_kernel(q_ref, k_ref, v_ref, ck_ref, cv_ref, bias_ref, rowmask_ref, o_ref):
    i = pl.program_id(1)
    scale = HEAD_DIM ** -0.5
    nk = NA_KROWS * GRID_W
    ks = jnp.clip(i * NA_QROWS - NA_WIN_H // 2, 0, GRID_ROWS - NA_KROWS)
    start = pl.multiple_of(ks * GRID_W, GRID_W)
    q = q_ref[...].astype(BF16)
    kw = k_ref[pl.ds(start, nk), :].astype(BF16)
    vw = v_ref[pl.ds(start, nk), :].astype(BF16)
    ck = ck_ref[...].astype(BF16)
    cv = cv_ref[...].astype(BF16)
    planes = []
    for qr in range(NA_QROWS):
        r = i * NA_QROWS + qr
        planes.append([jnp.clip(ks + 2 * m - r + NA_WIN_H, 0, 2 * NA_WIN_H - 1) for m in range(NA_KROWS // 2)])
    hb = NA_HEADS_PER_DOT
    w = hb * HEAD_DIM
    qrow = lax.broadcasted_iota(jnp.int32, (NA_QROWS * GRID_W, 128), 0) // GRID_W
    qsel = (lax.broadcasted_iota(jnp.int32, (NA_QROWS * GRID_W, 128), 1) == qrow).astype(BF16)
    rowmask = rowmask_ref[...]
    for g in range(NA_HEADS // hb):
        lanes = slice(g * w, (g + 1) * w)
        qg = jnp.concatenate([q[:, lanes] * scale, qsel], axis=1)
        kg = jnp.concatenate([_block_diag_heads(kw[:, lanes], hb), rowmask], axis=1)
        s_loc = _dot_nt(qg, kg)
        s_ctx = _dot_nt(q[:, lanes] * scale, _block_diag_heads(ck[:, lanes], hb))
        p_locs, p_ctxs, ds = [], [], []
        for hh in range(hb):
            bias = jnp.concatenate(
                [jnp.concatenate([bias_ref[g * hb + hh, plane] for plane in row], axis=1) for row in planes], axis=0)
            sl = s_loc[:, hh * nk:(hh + 1) * nk] + bias
            sc = s_ctx[:, hh * PAST_LEN:(hh + 1) * PAST_LEN]
            m = jnp.maximum(jnp.max(sl, axis=-1, keepdims=True), jnp.max(sc, axis=-1, keepdims=True))
            p_loc = jnp.exp(sl - m)
            p_ctx = jnp.exp(sc - m)
            ds.append(jnp.sum(p_loc, axis=-1, keepdims=True) + jnp.sum(p_ctx, axis=-1, keepdims=True))
            p_locs.append(p_loc.astype(BF16))
            p_ctxs.append(p_ctx.astype(BF16))
        o = (_dot(jnp.concatenate(p_locs, axis=1), _block_diag_heads(vw[:, lanes], hb))
             + _dot(jnp.concatenate(p_ctxs, axis=1), _block_diag_heads(cv[:, lanes], hb)))
        o_ref[:, lanes] = (o / _per_head(ds, o.shape)).astype(BF16)


def _na_row_masks():
    kh = min(NA_WIN_H, GRID_ROWS)
    nsteps = GRID_ROWS // NA_QROWS
    out = np.zeros((3, NA_HEADS_PER_DOT, NA_KROWS, GRID_W, 128), np.float32)
    for pat, i in enumerate((0, 1, nsteps - 1)):
        ks = min(max(i * NA_QROWS - NA_WIN_H // 2, 0), GRID_ROWS - NA_KROWS)
        for qr in range(NA_QROWS):
            r = i * NA_QROWS + qr
            st = min(max(r - kh // 2, 0), GRID_ROWS - kh)
            for kr in range(NA_KROWS):
                if not st <= ks + kr < st + kh:
                    out[pat, :, kr, :, qr] = NEG_INF
    return jnp.asarray(out.reshape(3, NA_HEADS_PER_DOT * NA_KROWS * GRID_W, 128), BF16)


def _na_lat(z, cache_na, bias, l):
    tq = NA_QROWS * GRID_W
    nsteps = GRID_ROWS // NA_QROWS
    lat_blk = N_CTX_TOK // DEC_SEQ
    return pl.pallas_call(
        _na_lat_kernel,
        out_shape=jax.ShapeDtypeStruct((N_LAT_TOK, NA_WIDTH), BF16),
        grid=(DEC_BATCH, nsteps),
        in_specs=[
            pl.BlockSpec((tq, NA_WIDTH), lambda b, i: (N_CTX_TOK // tq + b * nsteps + i, COL_NA_Q // NA_WIDTH)),
            pl.BlockSpec((DEC_SEQ, NA_WIDTH), lambda b, i: (lat_blk + b, COL_NA_K // NA_WIDTH)),
            pl.BlockSpec((DEC_SEQ, NA_WIDTH), lambda b, i: (lat_blk + b, COL_NA_V // NA_WIDTH)),
            pl.BlockSpec((None, None, None, PAST_LEN, NA_WIDTH), lambda b, i: (b, l, 0, 0, 0)),
            pl.BlockSpec((None, None, None, PAST_LEN, NA_WIDTH), lambda b, i: (b, l, 1, 0, 0)),
            pl.BlockSpec((NA_HEADS, 2 * NA_WIN_H, GRID_W, 2 * GRID_W), lambda b, i: (0, 0, 0, 0)),
            pl.BlockSpec((None, NA_HEADS_PER_DOT * NA_KROWS * GRID_W, 128),
                         lambda b, i: (jnp.where(i == 0, 0, jnp.where(i == nsteps - 1, 2, 1)), 0, 0)),
        ],
        out_specs=pl.BlockSpec((tq, NA_WIDTH), lambda b, i: (b * nsteps + i, 0)),
        compiler_params=pltpu.CompilerParams(
            dimension_semantics=("parallel", "arbitrary"), vmem_limit_bytes=VMEM_LIMIT),
        name="na_lat",
    )(z, z, z, cache_na, cache_na, bias, _na_row_masks())


def _rope_tables():
    nf = HEAD_DIM // 4
    t = jnp.arange(DEC_SEQ)
    pos = jnp.stack([t // GRID_W, t % GRID_W], axis=-1).astype(F32)
    inv = ROPE_THETA ** (-jnp.arange(nf, dtype=F32) / nf)
    ang = pos[:, :, None] * inv
    cos = jnp.cos(ang)
    sin = jnp.sin(ang)
    cos_d = jnp.stack([cos, cos], axis=2).reshape(DEC_SEQ, HEAD_DIM)
    sin_d = jnp.stack([-sin, sin], axis=2).reshape(DEC_SEQ, HEAD_DIM)
    return cos_d, sin_d


def _rope(x, cos, sin_signed):
    n = x.shape[-1]
    nf = HEAD_DIM // 4
    lane = lax.broadcasted_iota(jnp.int32, x.shape, 1)
    first_half = (lane // nf) % 2 == 0
    partner = jnp.where(first_half, pltpu.roll(x, n - nf, 1), pltpu.roll(x, nf, 1))
    return x * cos + partner * sin_signed


def _swa_lat_kernel(sink_ref, q_ref, kv_ref, cq_ref, sq_ref, ckt_ref, skt_ref, ck_ref, cv_ref, o_ref, k_sc, v_sc):
    n = pl.program_id(1)
    scale = HEAD_DIM ** -0.5
    nwin = 3 * SWA_BLOCK

    @pl.when(n == 0)
    def _():
        kv = kv_ref[...]
        k_sc[...] = _rope(kv[:, :GQA_KV_WIDTH].astype(F32), ckt_ref[...], skt_ref[...]).astype(BF16)
        v_sc[...] = kv[:, GQA_KV_WIDTH:].astype(BF16)

    q = (_rope(q_ref[...].astype(F32), cq_ref[...], sq_ref[...]) * scale).astype(BF16)
    start = pl.multiple_of(jnp.clip((n - 1) * SWA_BLOCK, 0, DEC_SEQ - nwin), SWA_BLOCK)
    kw = k_sc[pl.ds(start, nwin), :]
    vw = v_sc[pl.ds(start, nwin), :]
    ck = ck_ref[...].astype(BF16)
    cv = cv_ref[...].astype(BF16)
    rows = GQA_GROUP * SWA_BLOCK
    row = lax.broadcasted_iota(jnp.int32, (rows, nwin), 0)
    col = lax.broadcasted_iota(jnp.int32, (rows, nwin), 1)
    qpos = n * SWA_BLOCK + row % SWA_BLOCK
    kpos = start + col
    ok = jnp.abs(qpos - kpos) <= SWA_WINDOW
    grp = lax.broadcasted_iota(jnp.int32, (rows, 1), 0) // SWA_BLOCK
    for kh in range(GQA_KV_HEADS):
        sl = slice(kh * HEAD_DIM, (kh + 1) * HEAD_DIM)
        q4 = jnp.concatenate(
            [q[:, (kh * GQA_GROUP + g) * HEAD_DIM:(kh * GQA_GROUP + g + 1) * HEAD_DIM] for g in range(GQA_GROUP)],
            axis=0)
        sink = jnp.zeros((rows, 1), F32)
        for g in range(GQA_GROUP):
            sink = jnp.where(grp == g, sink_ref[kh * GQA_GROUP + g], sink)
        s_loc = jnp.where(ok, _dot_nt(q4, kw[:, sl]), NEG_INF)
        s_ctx = _dot_nt(q4, ck[:, sl])
        m = jnp.maximum(jnp.maximum(jnp.max(s_loc, axis=-1, keepdims=True),
                                    jnp.max(s_ctx, axis=-1, keepdims=True)), sink)
        p_loc = jnp.exp(s_loc - m)
        p_ctx = jnp.exp(s_ctx - m)
        d = (jnp.sum(p_loc, axis=-1, keepdims=True) + jnp.sum(p_ctx, axis=-1, keepdims=True)
             + jnp.exp(sink - m))
        o4 = (_dot(p_loc.astype(BF16), vw[:, sl]) + _dot(p_ctx.astype(BF16), cv[:, sl])) / d
        for g in range(GQA_GROUP):
            h = kh * GQA_GROUP + g
            o_ref[:, h * HEAD_DIM:(h + 1) * HEAD_DIM] = o4[g * SWA_BLOCK:(g + 1) * SWA_BLOCK].astype(BF16)


def _swa_lat(z, cache_gqa, sink, cos_q, sin_q, cos_k, sin_k, l):
    nb = DEC_SEQ // SWA_BLOCK
    lat_blk = N_CTX_TOK // DEC_SEQ
    return pl.pallas_call(
        _swa_lat_kernel,
        out_shape=jax.ShapeDtypeStruct((N_LAT_TOK, GQA_Q_WIDTH), BF16),
        grid=(DEC_BATCH, nb),
        in_specs=[
            pl.BlockSpec(memory_space=pltpu.SMEM),
            pl.BlockSpec((SWA_BLOCK, GQA_Q_WIDTH),
                         lambda b, n: (N_CTX_TOK // SWA_BLOCK + b * nb + n, COL_G_Q // GQA_Q_WIDTH)),
            pl.BlockSpec((DEC_SEQ, 2 * GQA_KV_WIDTH), lambda b, n: (lat_blk + b, COL_G_K // (2 * GQA_KV_WIDTH))),
            pl.BlockSpec((SWA_BLOCK, GQA_Q_WIDTH), lambda b, n: (n, 0)),
            pl.BlockSpec((SWA_BLOCK, GQA_Q_WIDTH), lambda b, n: (n, 0)),
            pl.BlockSpec((DEC_SEQ, GQA_KV_WIDTH), lambda b, n: (0, 0)),
            pl.BlockSpec((DEC_SEQ, GQA_KV_WIDTH), lambda b, n: (0, 0)),
            pl.BlockSpec((None, None, None, PAST_LEN, GQA_KV_WIDTH), lambda b, n: (b, l, 0, 0, 0)),
            pl.BlockSpec((None, None, None, PAST_LEN, GQA_KV_WIDTH), lambda b, n: (b, l, 1, 0, 0)),
        ],
        out_specs=pl.BlockSpec((SWA_BLOCK, GQA_Q_WIDTH), lambda b, n: (b * nb + n, 0)),
        scratch_shapes=[pltpu.VMEM((DEC_SEQ, GQA_KV_WIDTH), BF16), pltpu.VMEM((DEC_SEQ, GQA_KV_WIDTH), BF16)],
        compiler_params=pltpu.CompilerParams(
            dimension_semantics=("parallel", "arbitrary"), vmem_limit_bytes=VMEM_LIMIT),
        name="swa_lat",
    )(sink, z, z, cos_q, sin_q, cos_k, sin_k, cache_gqa, cache_gqa)


S5_OP_KEYS = ("kt", "bt_re", "bt_im", "ct_re", "ct_im", "pin_re", "pin_im", "po_re", "po_im", "apr", "api")


def _s5_operators(a_re, a_im, log_dt, b_re, b_im, c_re, c_im):
    T, G, N, C = S5_T, S5_GROUPS, S5_STATE, S5_GROUP_CH
    tau = jnp.arange(T + 1, dtype=F32)
    out = {k: [] for k in S5_OP_KEYS}
    for d in range(2):
        A = lax.complex(a_re[d].astype(F32), a_im[d].astype(F32))
        dt = jnp.exp(log_dt[d].astype(F32))[:, None]
        a_bar = jnp.exp(A * dt)
        pw = jnp.exp((A * dt)[None] * tau[:, None, None])
        b_bar = ((a_bar - 1.0) / A)[..., None] * lax.complex(b_re[d].astype(F32), b_im[d].astype(F32))
        c_mat = lax.complex(c_re[d].astype(F32), c_im[d].astype(F32))
        kern = jnp.einsum("gon,tgn,gni->gtoi", c_mat, pw[:T], b_bar, precision=lax.Precision.HIGHEST).real
        kern = lax.optimization_barrier(kern).transpose(0, 3, 1, 2)
        if d == 0:
            p_in = pw[:T][::-1]
            p_out = pw[1:T + 1]
        else:
            p_in = pw[:T]
            p_out = pw[1:T + 1][::-1]
            kern = kern[:, :, ::-1]
        bt, ct, po = b_bar.transpose(0, 2, 1), c_mat.transpose(0, 2, 1), p_out.transpose(1, 2, 0)
        pw2 = jnp.exp((A * dt)[None] * (T * 2.0 ** jnp.arange(S5_NPOW, dtype=F32))[:, None, None])
        vals = (kern.reshape(G, C, T * C), bt.real, bt.imag, ct.real, ct.imag,
                p_in.real.reshape(T, S5_PAIRS, 1, 2 * N), p_in.imag.reshape(T, S5_PAIRS, 1, 2 * N), po.real, po.imag,
                pw2.real.reshape(S5_NPOW, G * N), pw2.imag.reshape(S5_NPOW, G * N))
        for k, v in zip(S5_OP_KEYS, vals):
            out[k].append(v)
    return {k: jnp.stack(v) for k, v in out.items()}


def _s5_expand(kt_ref, btr_ref, bti_ref, ctr_ref, cti_ref, pir_ref, pii_ref, por_ref, poi_ref,
               bs_sc, m_sc, cre_sc, cim_sc):
    T, C, N = S5_T, S5_GROUP_CH, S5_STATE
    pc, half, width = 2 * C, 2 * N, 2 * T * C
    lane = lax.broadcasted_iota(jnp.int32, (pc, width), 1)
    zc, zn, zo = jnp.zeros((C, C), F32), jnp.zeros((C, N), F32), jnp.zeros((N, C), F32)

    def own_half(x, z, g):
        return [x, z] if g == 0 else [z, x]

    def expand_pair(p, d):
        k0, bx_re, bx_im, co_re, co_im = [], [], [], [], []
        for g in range(2):
            gl = 2 * p + g
            ktg = kt_ref[d, gl]
            k0.append(jnp.concatenate(
                [blk for lag in range(T) for blk in own_half(ktg[:, lag * C:(lag + 1) * C], zc, g)], axis=1))
            bx_re.append(jnp.concatenate(own_half(btr_ref[d, gl], zn, g), axis=1))
            bx_im.append(jnp.concatenate(own_half(bti_ref[d, gl], zn, g), axis=1))
            cr, ci = ctr_ref[d, gl], cti_ref[d, gl]
            pr_all, pi_all = por_ref[d, gl], poi_ref[d, gl]
            re_p, im_p = [], []
            for j in range(T):
                pr, pi = pr_all[:, j:j + 1], pi_all[:, j:j + 1]
                re_p += own_half(cr * pr - ci * pi, zo, g)
                im_p += own_half(cr * pi + ci * pr, zo, g)
            co_re.append(jnp.concatenate(re_p, axis=1))
            co_im.append(jnp.concatenate(im_p, axis=1))
        k0 = jnp.concatenate(k0, axis=0)
        bx_re, bx_im = jnp.concatenate(bx_re, axis=0), jnp.concatenate(bx_im, axis=0)
        for s in range(T):
            if d == 0:
                sh = pc * s
                blk = jnp.where(lane >= sh, pltpu.roll(k0, sh, 1), 0.0) if sh else k0
            else:
                sh = pc * (T - 1 - s)
                blk = jnp.where(lane < width - sh, pltpu.roll(k0, width - sh, 1), 0.0) if sh else k0
            m_sc[d, p, s * pc:(s + 1) * pc, :] = blk.astype(BF16)
            pr, pi = pir_ref[d, s, p], pii_ref[d, s, p]
            bs_sc[d, p, s * pc:(s + 1) * pc, :] = jnp.concatenate(
                [pr * bx_re - pi * bx_im, pr * bx_im + pi * bx_re], axis=1).astype(BF16)
        cre_sc[d, p] = jnp.concatenate(co_re, axis=0).astype(BF16)
        cim_sc[d, p] = (-jnp.concatenate(co_im, axis=0)).astype(BF16)

    for d in range(2):
        lax.fori_loop(0, S5_OCT_PAIRS, lambda p, c, d=d: (expand_pair(p, d), c)[1], 0)


def _s5_kernel(u_ref, kt_ref, btr_ref, bti_ref, ctr_ref, cti_ref, pir_ref, pii_ref, por_ref, poi_ref, apr_ref,
               api_ref, h0_ref, y_ref, fin_ref, fin_sc, bs_sc, m_sc, cre_sc, cim_sc):
    t = pl.program_id(1)
    nctx = N_CTX_TOK // S5_BLOCK_TOK

    @pl.when(t == 0)
    def _():
        _s5_expand(kt_ref, btr_ref, bti_ref, ctr_ref, cti_ref, pir_ref, pii_ref, por_ref, poi_ref,
                   bs_sc, m_sc, cre_sc, cim_sc)

    refs = (u_ref, bs_sc, m_sc, cre_sc, cim_sc, apr_ref, api_ref, h0_ref, y_ref, fin_ref, fin_sc)

    @pl.when(t < nctx)
    def _():
        _s5_block(*refs, t, kseq=SEQ // S5_T, has_h0=False)

    @pl.when(t >= nctx)
    def _():
        _s5_block(*refs, t - nctx, kseq=DEC_SEQ // S5_T, has_h0=True)


def _s5_block(u_ref, bs_ref, m_ref, cre_ref, cim_ref, apr_ref, api_ref, h0_ref, y_ref, fin_ref, fin_sc, t, *,
              kseq, has_h0):
    rows, half, pc = S5_ROWS, 2 * S5_STATE, 2 * S5_GROUP_CH
    k = lax.broadcasted_iota(jnp.int32, (rows, half), 0) % kseq
    shifts = [1 << i for i in range(kseq.bit_length() - 1)]
    xs = [u_ref[pl.ds(s, rows, stride=S5_T), :].astype(BF16) for s in range(S5_T)]
    ys = []
    for p in range(S5_OCT_PAIRS):
        lanes = slice(p * half, (p + 1) * half)
        xp = jnp.concatenate([x[:, p * pc:(p + 1) * pc] for x in xs], axis=1)
        acc = None
        for d in range(2):
            inc = _dot(xp, bs_ref[d, p])
            sr, si = inc[:, :half], inc[:, half:]
            if has_h0:
                nseq = rows // kseq
                h0r, h0i = h0_ref[d, 0, t * nseq, :, lanes], h0_ref[d, 1, t * nseq, :, lanes]
                seq = lax.broadcasted_iota(jnp.int32, (rows, half), 0) // kseq
                for j in range(1, nseq):
                    h0r = jnp.where(seq == j, h0_ref[d, 0, t * nseq + j, :, lanes], h0r)
                    h0i = jnp.where(seq == j, h0_ref[d, 1, t * nseq + j, :, lanes], h0i)
                ar, ai = apr_ref[d, 0:1, lanes], api_ref[d, 0:1, lanes]
                first = (k == 0) if d == 0 else (k == kseq - 1)
                sr = sr + jnp.where(first, ar * h0r - ai * h0i, 0.0)
                si = si + jnp.where(first, ar * h0i + ai * h0r, 0.0)
            else:
                h0r = h0i = 0.0
            for i, sh in enumerate(shifts):
                ar, ai = apr_ref[d, i:i + 1, lanes], api_ref[d, i:i + 1, lanes]
                ok = (k >= sh) if d == 0 else (k < kseq - sh)
                amt = sh if d == 0 else rows - sh
                rr = jnp.where(ok, pltpu.roll(sr, amt, 0), 0.0)
                ri = jnp.where(ok, pltpu.roll(si, amt, 0), 0.0)
                sr, si = sr + ar * rr - ai * ri, si + ar * ri + ai * rr
            inner = (k >= 1) if d == 0 else (k < kseq - 1)
            amt = 1 if d == 0 else rows - 1
            hr = jnp.where(inner, pltpu.roll(sr, amt, 0), h0r)
            hi = jnp.where(inner, pltpu.roll(si, amt, 0), h0i)
            yd = (_dot(xp, m_ref[d, p]) + _dot(hr.astype(BF16), cre_ref[d, p])
                  + _dot(hi.astype(BF16), cim_ref[d, p]))
            acc = yd if acc is None else acc + yd
            if not has_h0:
                last = kseq - 1 if d == 0 else 0
                fin_sc[0] = sr
                fin_sc[1] = si
                fin_ref[d, 0, :, lanes] = fin_sc[0, pl.ds(last, rows // kseq, stride=kseq), :]
                fin_ref[d, 1, :, lanes] = fin_sc[1, pl.ds(last, rows // kseq, stride=kseq), :]
        ys.append(acc)
    for j in range(S5_T):
        y_ref[pl.ds(j, rows, stride=S5_T), :] = jnp.concatenate([y[:, j * pc:(j + 1) * pc] for y in ys], axis=1)


def _s5_scan(u, ops, h0, l):
    nblk, nctx = N_TOK // S5_BLOCK_TOK, N_CTX_TOK // S5_BLOCK_TOK
    noct = S5_PAIRS // S5_OCT_PAIRS
    half, gn = 2 * S5_STATE, S5_GROUPS * S5_STATE
    pw = 2 * S5_T * S5_GROUP_CH
    nseq = S5_BLOCK_TOK // SEQ
    ngrp = 2 * S5_OCT_PAIRS
    C, N, T = S5_GROUP_CH, S5_STATE, S5_T
    gspec = lambda r, c: pl.BlockSpec((None, 2, ngrp, r, c), lambda q, t: (l, 0, q, 0, 0))
    lspec = lambda r: pl.BlockSpec((None, 2, r, S5_OCT_PAIRS * half), lambda q, t: (l, 0, 0, q))
    pspec = pl.BlockSpec((None, 2, T, S5_OCT_PAIRS, 1, half), lambda q, t: (l, 0, 0, q, 0, 0))
    return pl.pallas_call(
        _s5_kernel,
        grid=(noct, nblk),
        in_specs=[
            pl.BlockSpec((S5_BLOCK_TOK, 128), lambda q, t: (t, q)),
            gspec(C, T * C), gspec(C, N), gspec(C, N), gspec(N, C), gspec(N, C), pspec, pspec,
            gspec(N, T), gspec(N, T), lspec(S5_NPOW), lspec(S5_NPOW),
            pl.BlockSpec((2, 2, DEC_BATCH, 1, S5_OCT_PAIRS * half), lambda q, t: (0, 0, 0, 0, q)),
        ],
        out_shape=(jax.ShapeDtypeStruct((N_TOK, S5_CH), F32), jax.ShapeDtypeStruct((2, 2, BATCH, gn), F32)),
        out_specs=(pl.BlockSpec((S5_BLOCK_TOK, 128), lambda q, t: (t, q)),
                   pl.BlockSpec((2, 2, nseq, S5_OCT_PAIRS * half),
                                lambda q, t: (0, 0, jnp.minimum(t, nctx - 1), q))),
        scratch_shapes=[pltpu.VMEM((2, S5_ROWS, half), F32),
                        pltpu.VMEM((2, S5_OCT_PAIRS, pw, 2 * half), BF16), pltpu.VMEM((2, S5_OCT_PAIRS, pw, pw), BF16),
                        pltpu.VMEM((2, S5_OCT_PAIRS, half, pw), BF16), pltpu.VMEM((2, S5_OCT_PAIRS, half, pw), BF16)],
        compiler_params=pltpu.CompilerParams(
            dimension_semantics=("parallel", "arbitrary"), vmem_limit_bytes=VMEM_LIMIT),
        name="s5_scan",
    )(u, *[ops[k] for k in S5_OP_KEYS], h0.reshape(2, 2, DEC_BATCH, 1, gn))


def _route(logits):
    lane = lax.broadcasted_iota(jnp.int32, logits.shape, 1)
    big = jnp.int32(1 << 20)
    is_g = (lane >= MOE_EXPERTS) & (lane < MOE_EXPERTS + MOE_GROUPS)
    lg = jnp.where(is_g, logits, NEG_INF)
    gmax = jnp.max(lg, axis=-1, keepdims=True)
    gsel = jnp.min(jnp.where(is_g & (lg == gmax), lane, big), axis=-1, keepdims=True) - MOE_EXPERTS
    p_group = 1.0 / jnp.sum(jnp.where(is_g, jnp.exp(lg - gmax), 0.0), axis=-1, keepdims=True)
    in_grp = (lane < MOE_EXPERTS) & (lane // MOE_EXPERTS_PER_GROUP == gsel)
    le = jnp.where(in_grp, logits, NEG_INF)
    v1 = jnp.max(le, axis=-1, keepdims=True)
    i1 = jnp.min(jnp.where(in_grp & (le == v1), lane, big), axis=-1, keepdims=True)
    rest = in_grp & (lane != i1)
    le2 = jnp.where(rest, logits, NEG_INF)
    v2 = jnp.max(le2, axis=-1, keepdims=True)
    i2 = jnp.min(jnp.where(rest & (le2 == v2), lane, big), axis=-1, keepdims=True)
    e2 = jnp.exp(v2 - v1)
    w1 = 1.0 / (1.0 + e2)
    w2 = e2 / (1.0 + e2)
    comb = jnp.where(lane == i1, w1 * p_group, 0.0) + jnp.where(lane == i2, w2 * p_group, 0.0)
    a = jnp.minimum(i1, i2) - gsel * MOE_EXPERTS_PER_GROUP
    b = jnp.maximum(i1, i2) - gsel * MOE_EXPERTS_PER_GROUP
    pair = jnp.where(a == 0, b - 1, jnp.where(a == 1, jnp.where(b == 3, 3, 4), 5))
    cls = gsel * MOE_PAIRS + pair
    return jnp.where(lane == MOE_EXPERTS, cls.astype(F32), comb)


def _merge_kernel(xc_ref, xl_ref, oac_ref, oal_ref, obc_ref, obl_ref, u_ref, yc_ref, gates_ref, mod_ref, d_ref,
                  wglu_ref, wa_ref, wb_ref, wc_ref, wout_ref, g2_ref, wrh_ref, wrl_ref, br_ref, x1_ref, hx_ref,
                  *, tm):
    is_ctx = pl.program_id(0) < N_CTX_TOK // tm
    x = jnp.where(is_ctx, xc_ref[...], xl_ref[...])
    oa = jnp.where(is_ctx, oac_ref[...], oal_ref[...])
    ob = jnp.where(is_ctx, obc_ref[...], obl_ref[...])
    y = u_ref[...] * d_ref[...] + yc_ref[...]
    y = y * (0.5 * (1.0 + jnp.tanh(math.sqrt(2.0 / math.pi) * (y + 0.044715 * (y * y * y)))))
    oc = y * jax.nn.sigmoid(_dot(y.astype(BF16), wglu_ref[...]))
    gate = jax.nn.sigmoid(gates_ref[...].astype(F32))
    merged = (gate[:, :D_MODEL] * _dot(oa, wa_ref[...])
              + gate[:, D_MODEL:2 * D_MODEL] * _dot(ob, wb_ref[...])
              + gate[:, 2 * D_MODEL:] * _dot(oc.astype(BF16), wc_ref[...]))
    x1 = x + mod_ref[2:3, :] * _dot(merged.astype(BF16), wout_ref[...])
    x1_ref[...] = x1
    h2 = _rms(x1, g2_ref[...]) * (1.0 + mod_ref[4:5, :]) + mod_ref[3:4, :]
    h_hi, h_lo = _split_bf16(h2)
    logits = _dot(h_hi, wrh_ref[...]) + _dot(h_lo, wrh_ref[...]) + _dot(h_hi, wrl_ref[...]) + br_ref[...]
    half = D_MODEL // 2
    hx_ref[:, :half] = _pack_pairs(h2)
    hx_ref[:, half:] = pltpu.bitcast(_route(logits), jnp.uint32)


def _merge(x, oa_c, oa_l, ob_c, ob_l, u, yc, z, mod, d_s5, wglu, wa, wb, wc, wout, g2, wr_hi, wr_lo, br, l):
    tm = 512
    nctx = N_CTX_TOK // tm
    full = lambda r, c: pl.BlockSpec((r, c), lambda i: (0, 0))
    layer = lambda r, c: pl.BlockSpec((None, r, c), lambda i: (l, 0, 0))
    ctx_blk = lambda w: pl.BlockSpec((tm, w), lambda i: (jnp.minimum(i, nctx - 1), 0))
    lat_blk = lambda w, first=0: pl.BlockSpec((tm, w), lambda i: (jnp.maximum(i - nctx, 0) + first // tm, 0))
    x_ctx, x_lat, lat0 = _token_arrays(x)
    return pl.pallas_call(
        functools.partial(_merge_kernel, tm=tm),
        out_shape=(jax.ShapeDtypeStruct((N_TOK, D_MODEL), F32),
                   jax.ShapeDtypeStruct((N_TOK, MOE_ROW_WORDS), jnp.uint32)),
        grid=(N_TOK // tm,),
        in_specs=[
            ctx_blk(D_MODEL), lat_blk(D_MODEL, lat0),
            ctx_blk(NA_WIDTH), lat_blk(NA_WIDTH), ctx_blk(GQA_Q_WIDTH), lat_blk(GQA_Q_WIDTH),
            pl.BlockSpec((tm, S5_CH), lambda i: (i, 0)),
            pl.BlockSpec((tm, S5_CH), lambda i: (i, 0)),
            pl.BlockSpec((tm, N_BRANCH * D_MODEL), lambda i: (i, COL_GATES // (N_BRANCH * D_MODEL))),
            pl.BlockSpec((None, None, 6, D_MODEL), lambda i: (l, _cond_index(i, tm), 0, 0)),
            full(1, S5_CH), layer(S5_CH, S5_CH), layer(NA_WIDTH, D_MODEL), layer(GQA_Q_WIDTH, D_MODEL),
            layer(S5_CH, D_MODEL), layer(D_MODEL, D_MODEL), full(1, D_MODEL),
            full(D_MODEL, 128), full(D_MODEL, 128), full(1, 128),
        ],
        out_specs=(pl.BlockSpec((tm, D_MODEL), lambda i: (i, 0)),
                   pl.BlockSpec((tm, MOE_ROW_WORDS), lambda i: (i, 0))),
        compiler_params=pltpu.CompilerParams(
            dimension_semantics=("parallel",), vmem_limit_bytes=VMEM_LIMIT),
        name="merge",
    )(x_ctx, x_lat, oa_c, oa_l, ob_c, ob_l, u, yc, z, mod, d_s5, wglu, wa, wb, wc, wout, g2, wr_hi, wr_lo, br)


def _moe_plan(cls):
    ncls = MOE_GROUPS * MOE_PAIRS
    ntiles = MOE_ROWS // MOE_TILE
    onehot = (cls[:, None] == jnp.arange(ncls)[None, :]).astype(jnp.int32)
    rank = jnp.sum((jnp.cumsum(onehot, axis=0) - onehot) * onehot, axis=1)
    ccount = jnp.sum(onehot, axis=0).reshape(MOE_GROUPS, MOE_PAIRS)
    gcount = jnp.sum(ccount, axis=1)
    gpadded = (gcount + MOE_TILE - 1) // MOE_TILE * MOE_TILE
    gend = jnp.cumsum(gpadded)
    cstart = ((gend - gpadded)[:, None] + jnp.cumsum(ccount, axis=1) - ccount).reshape(ncls)
    cend = cstart + ccount.reshape(ncls)
    pos = jnp.sum(onehot * cstart[None, :], axis=1) + rank
    tile_row = jnp.arange(ntiles) * MOE_TILE
    tile_group = jnp.minimum(jnp.sum((tile_row[:, None] >= gend[None, :]).astype(jnp.int32), axis=1),
                             MOE_GROUPS - 1)
    member = np.zeros((ncls, MOE_GROUPS, MOE_EXPERTS_PER_GROUP), np.float32)
    for g in range(MOE_GROUPS):
        for p, pair in enumerate(MOE_PAIR_MEMBERS):
            member[g * MOE_PAIRS + p, g, list(pair)] = 1.0
    overlap = ((cstart[None, :] < tile_row[:, None] + MOE_TILE) & (cend[None, :] > tile_row[:, None])
               & (cend > cstart)[None, :]).astype(F32)
    need = jnp.einsum("tc,cge->tge", overlap, member)
    need = jnp.sum(need * (tile_group[:, None, None] == jnp.arange(MOE_GROUPS)[None, :, None]), axis=1)
    return (pos.astype(jnp.int32), tile_group.astype(jnp.int32), (need > 0).astype(jnp.int32).reshape(-1),
            (gend[-1:] // MOE_TILE).astype(jnp.int32))


def _dispatch_kernel(pos_ref, hx_ref, init_ref, out_ref, sem):
    del init_ref
    base = pl.program_id(0) * MOE_TOK_BLOCK

    def row_copy(r):
        return pltpu.make_async_copy(hx_ref.at[pl.ds(r, 1), :], out_ref.at[pl.ds(pos_ref[base + r], 1), :], sem)

    def start(r, c):
        row_copy(r).start()
        return c

    def wait(r, c):
        row_copy(r).wait()
        return c

    lax.fori_loop(0, MOE_TOK_BLOCK, start, 0, unroll=True)
    lax.fori_loop(0, MOE_TOK_BLOCK, wait, 0, unroll=True)


def _dispatch(hx, pos):
    return pl.pallas_call(
        _dispatch_kernel,
        grid_spec=pltpu.PrefetchScalarGridSpec(
            num_scalar_prefetch=1,
            grid=(N_TOK // MOE_TOK_BLOCK,),
            in_specs=[pl.BlockSpec((MOE_TOK_BLOCK, MOE_ROW_WORDS), lambda i, pos: (i, 0)),
                      pl.BlockSpec(memory_space=pl.ANY)],
            out_specs=pl.BlockSpec(memory_space=pl.ANY),
            scratch_shapes=[pltpu.SemaphoreType.DMA]),
        out_shape=jax.ShapeDtypeStruct((MOE_ROWS, MOE_ROW_WORDS), jnp.uint32),
        input_output_aliases={2: 0},
        compiler_params=pltpu.CompilerParams(dimension_semantics=("arbitrary",)),
        name="moe_dispatch",
    )(pos, hx, jnp.zeros((MOE_ROWS, MOE_ROW_WORDS), jnp.uint32))


def _experts_kernel(tg_ref, need_ref, nu_ref, hx_ref, wg_ref, wu_ref, wd_ref, y_ref, acc_sc):
    t = pl.program_id(0)
    acc_sc[...] = jnp.zeros_like(acc_sc)
    half = D_MODEL // 2
    for e in range(MOE_EXPERTS_PER_GROUP):
        @pl.when((t < nu_ref[0]) & (need_ref[t * MOE_EXPERTS_PER_GROUP + e] > 0))
        def _():
            h = _unpack_pairs(hx_ref[:, :half]).astype(BF16)
            comb = pltpu.bitcast(hx_ref[:, half:], F32)
            lane = lax.broadcasted_iota(jnp.int32, comb.shape, 1)
            ce = jnp.sum(jnp.where(lane == tg_ref[t] * MOE_EXPERTS_PER_GROUP + e, comb, 0.0), axis=-1,
                         keepdims=True)
            a = _dot(h, wg_ref[e])
            b = _dot(h, wu_ref[e])
            act = (a * jax.nn.sigmoid(a)) * b * ce
            acc_sc[...] += _dot(act.astype(BF16), wd_ref[e])

    y_ref[...] = _pack_pairs(acc_sc[...])


def _experts(hs, tile_group, need, n_used, wg, wu, wd, l):
    wspec = lambda r, c: pl.BlockSpec((None, None, MOE_EXPERTS_PER_GROUP, r, c),
                                      lambda t, tg, need, nu: (l, tg[t], 0, 0, 0))
    return pl.pallas_call(
        _experts_kernel,
        grid_spec=pltpu.PrefetchScalarGridSpec(
            num_scalar_prefetch=3,
            grid=(MOE_ROWS // MOE_TILE,),
            in_specs=[pl.BlockSpec((MOE_TILE, MOE_ROW_WORDS), lambda t, tg, need, nu: (t, 0)),
                      wspec(D_MODEL, EXPERT_FF), wspec(D_MODEL, EXPERT_FF), wspec(EXPERT_FF, D_MODEL)],
            out_specs=pl.BlockSpec((MOE_TILE, D_MODEL // 2), lambda t, tg, need, nu: (t, 0)),
            scratch_shapes=[pltpu.VMEM((MOE_TILE, D_MODEL), F32)]),
        out_shape=jax.ShapeDtypeStruct((MOE_ROWS, D_MODEL // 2), jnp.uint32),
        compiler_params=pltpu.CompilerParams(
            dimension_semantics=("arbitrary",), vmem_limit_bytes=VMEM_LIMIT),
        name="moe_experts",
    )(tile_group, need, n_used, hs, wg, wu, wd)


def _combine_kernel(pos_ref, y_ref, x_ref, mod_ref, fg_ref, *rest, final):
    buf, sem = rest[-2:]
    i = pl.program_id(0)
    base = i * MOE_TOK_BLOCK

    def row_copy(r):
        return pltpu.make_async_copy(y_ref.at[pl.ds(pos_ref[base + r], 1), :], buf.at[pl.ds(r, 1), :], sem)

    def start(r, c):
        row_copy(r).start()
        return c

    def wait(r, c):
        row_copy(r).wait()
        return c

    lax.fori_loop(0, MOE_TOK_BLOCK, start, 0, unroll=True)
    lax.fori_loop(0, MOE_TOK_BLOCK, wait, 0, unroll=True)
    x2 = x_ref[...] + mod_ref[5:6, :] * _unpack_pairs(buf[...])
    if not final:
        rest[0][...] = x2
    else:
        y = _rms(x2, fg_ref[...])
        is_ctx = i < N_CTX_TOK // MOE_TOK_BLOCK

        @pl.when(is_ctx)
        def _():
            rest[0][...] = y

        @pl.when(jnp.logical_not(is_ctx))
        def _():
            rest[1][...] = y


def _combine(ys, pos, x1, mod, fg, l, final):
    tm = MOE_TOK_BLOCK
    nctx = N_CTX_TOK // tm
    tok = pl.BlockSpec((tm, D_MODEL), lambda i, pos: (i, 0))
    out = jax.ShapeDtypeStruct((N_TOK, D_MODEL), F32)
    if final:
        out = (jax.ShapeDtypeStruct((N_CTX_TOK, D_MODEL), F32), jax.ShapeDtypeStruct((N_LAT_TOK, D_MODEL), F32))
        out_specs = (pl.BlockSpec((tm, D_MODEL), lambda i, pos: (jnp.minimum(i, nctx - 1), 0)),
                     pl.BlockSpec((tm, D_MODEL), lambda i, pos: (jnp.maximum(i - nctx, 0), 0)))
    else:
        out_specs = tok
    return pl.pallas_call(
        functools.partial(_combine_kernel, final=final),
        grid_spec=pltpu.PrefetchScalarGridSpec(
            num_scalar_prefetch=1,
            grid=(N_TOK // tm,),
            in_specs=[pl.BlockSpec(memory_space=pl.ANY), tok,
                      pl.BlockSpec((None, None, 6, D_MODEL), lambda i, pos: (l, _cond_index(i, tm), 0, 0)),
                      pl.BlockSpec((1, D_MODEL), lambda i, pos: (0, 0))],
            out_specs=out_specs,
            scratch_shapes=[pltpu.VMEM((tm, D_MODEL // 2), jnp.uint32), pltpu.SemaphoreType.DMA]),
        out_shape=out,
        compiler_params=pltpu.CompilerParams(
            dimension_semantics=("arbitrary",), vmem_limit_bytes=VMEM_LIMIT),
        name="moe_combine",
    )(pos, ys, x1, mod, fg)


def _pack_w_in(w):
    qkv = w[..., :COL_G_V + GQA_KV_WIDTH]
    u = w[..., COL_G_V + GQA_KV_WIDTH:COL_G_V + GQA_KV_WIDTH + S5_CH]
    gates = w[..., COL_G_V + GQA_KV_WIDTH + S5_CH:]
    pad = jnp.zeros(w.shape[:-1] + (COL_U - (COL_G_V + GQA_KV_WIDTH),), w.dtype)
    return jnp.concatenate([qkv, pad, u, gates], axis=-1).astype(BF16)


def kernel(x_prompt, x_sample, cache_na_kv, cache_gqa_kv, state_ssm, c, c_ctx, norm_g, w_ada, b_ada, w_in, na_rpb, gqa_sink, s5_a_re, s5_a_im, s5_log_dt, s5_b_re, s5_b_im, s5_c_re, s5_c_im, s5_d, s5_w_glu, w_branch_a, w_branch_b, w_branch_c, w_out, moe_w_group, moe_b_group, moe_w_expert, moe_b_expert, moe_w_gate, moe_w_up, moe_w_down, final_g):
    cond = jnp.zeros((N_COND, D_MODEL), F32).at[0].set(c_ctx.astype(F32)).at[1:1 + DEC_BATCH].set(c.astype(F32))
    mod = _ada_mod(cond, w_ada.astype(F32), b_ada.astype(F32))

    x = (x_prompt.astype(F32).reshape(N_CTX_TOK, D_MODEL), x_sample.astype(F32).reshape(N_LAT_TOK, D_MODEL))
    cache_na = cache_na_kv.reshape(DEC_BATCH, DEPTH, 2, PAST_LEN, NA_WIDTH)
    cache_gqa = cache_gqa_kv.reshape(DEC_BATCH, DEPTH, 2, PAST_LEN, GQA_KV_WIDTH)
    cos_d, sin_d = _rope_tables()
    cos_q, sin_q = jnp.tile(cos_d, (1, GQA_Q_HEADS)), jnp.tile(sin_d, (1, GQA_Q_HEADS))
    cos_k, sin_k = jnp.tile(cos_d, (1, GQA_KV_HEADS)), jnp.tile(sin_d, (1, GQA_KV_HEADS))
    gn = S5_GROUPS * S5_STATE
    fg = final_g.astype(F32).reshape(1, D_MODEL)

    w_in_packed = _pack_w_in(w_in)
    bf16_weights = [w.astype(BF16) for w in (s5_w_glu, w_branch_a, w_branch_b, w_branch_c, w_out)]
    grouped = lambda w: w.astype(BF16).reshape((DEPTH, MOE_GROUPS, MOE_EXPERTS_PER_GROUP) + w.shape[2:])
    moe_weights = [grouped(w) for w in (moe_w_gate, moe_w_up, moe_w_down)]
    ops = jax.vmap(_s5_operators)(s5_a_re, s5_a_im, s5_log_dt, s5_b_re, s5_b_im, s5_c_re, s5_c_im)

    na_list, gqa_list, ssm_list = [], [], []
    y = None
    for l in range(DEPTH):
        z, u, new_na, new_gqa = _inproj(x, norm_g[l, 0].astype(F32).reshape(1, D_MODEL), mod, w_in_packed, l)
        na_list.append(new_na.reshape(BATCH, 2, SEQ, NA_HEADS, HEAD_DIM))
        gqa_list.append(new_gqa.reshape(BATCH, 2, SEQ, GQA_KV_HEADS, HEAD_DIM))

        sink = gqa_sink[l].astype(F32)
        oa_c, ob_c = _ctx_attn(z, sink)
        oa_l = _na_lat(z, cache_na, _na_bias_tables(na_rpb[l]), l)
        ob_l = _swa_lat(z, cache_gqa, sink, cos_q, sin_q, cos_k, sin_k, l)

        h0 = state_ssm[:, l].astype(F32).reshape(DEC_BATCH, 2, 2, gn).transpose(1, 2, 0, 3)
        yc, fin = _s5_scan(u, ops, h0, l)
        ssm_list.append(fin.transpose(2, 0, 1, 3).reshape(BATCH, 2, 2, S5_GROUPS, S5_STATE).astype(x_prompt.dtype))

        wr = jnp.zeros((D_MODEL, 128), F32)
        wr = wr.at[:, :MOE_EXPERTS].set(moe_w_expert[l].astype(F32))
        wr = wr.at[:, MOE_EXPERTS:MOE_EXPERTS + MOE_GROUPS].set(moe_w_group[l].astype(F32))
        br = jnp.zeros((1, 128), F32)
        br = br.at[0, :MOE_EXPERTS].set(moe_b_expert[l].astype(F32))
        br = br.at[0, MOE_EXPERTS:MOE_EXPERTS + MOE_GROUPS].set(moe_b_group[l].astype(F32))
        wr_hi, wr_lo = _split_bf16(wr)
        x1, hx = _merge(
            x, oa_c, oa_l, ob_c, ob_l, u, yc, z, mod, s5_d[l].astype(F32).reshape(1, S5_CH), *bf16_weights,
            norm_g[l, 1].astype(F32).reshape(1, D_MODEL), wr_hi, wr_lo, br, l)
        cls = lax.bitcast_convert_type(hx[:, D_MODEL // 2 + MOE_EXPERTS], F32).astype(jnp.int32)
        pos, tile_group, need, n_used = _moe_plan(cls)
        ys = _experts(_dispatch(hx, pos), tile_group, need, n_used, *moe_weights, l)
        if l < DEPTH - 1:
            x = _combine(ys, pos, x1, mod, fg, l, False)
        else:
            y_ctx, y_lat = _combine(ys, pos, x1, mod, fg, l, True)

    return (y_ctx.reshape(BATCH, SEQ, D_MODEL), y_lat.reshape(DEC_BATCH, DEC_SEQ, D_MODEL),
            jnp.stack(na_list, axis=1), jnp.stack(gqa_list, axis=1), jnp.stack(ssm_list, axis=1))
```

```python
import functools
import math

import numpy as np
import jax
import jax.numpy as jnp
from jax import lax
from jax.experimental import pallas as pl
from jax.experimental.pallas import tpu as pltpu

F32 = jnp.float32
BF16 = jnp.bfloat16

D_MODEL = 1024
BATCH = 32
SEQ = 256
DEPTH = 2
DEC_BATCH = 4
DEC_SEQ = 2048
PAST_LEN = 512
GRID_W = 64
GRID_ROWS = DEC_SEQ // GRID_W
HEAD_DIM = 64
NA_HEADS = 8
NA_WIN_H = 8
NA_WIN_W = 16
GQA_Q_HEADS = 8
GQA_KV_HEADS = 2
GQA_GROUP = GQA_Q_HEADS // GQA_KV_HEADS
SWA_WINDOW = 128
SWA_BLOCK = 128
ROPE_THETA = 10000.0
S5_CH = 512
S5_GROUP_CH = 16
S5_GROUPS = S5_CH // S5_GROUP_CH
S5_STATE = 64
N_BRANCH = 3
NA_WIDTH = NA_HEADS * HEAD_DIM
GQA_Q_WIDTH = GQA_Q_HEADS * HEAD_DIM
GQA_KV_WIDTH = GQA_KV_HEADS * HEAD_DIM
MOE_GROUPS = 4
MOE_EXPERTS_PER_GROUP = 4
MOE_EXPERTS = MOE_GROUPS * MOE_EXPERTS_PER_GROUP
EXPERT_FF = 512
EPS = 1e-6
NEG_INF = -1e30

N_CTX_TOK = BATCH * SEQ
N_LAT_TOK = DEC_BATCH * DEC_SEQ
N_TOK = N_CTX_TOK + N_LAT_TOK
N_COND = 8

COL_NA_Q = 0
COL_NA_K = 512
COL_NA_V = 1024
COL_G_Q = 1536
COL_G_K = 2048
COL_G_V = 2176
COL_U = 2560
COL_GATES = 3072
Z_COLS = 6144

S5_T = 16
S5_PAIRS = S5_GROUPS // 2
S5_OCT_PAIRS = 4
S5_BLOCK_TOK = 4096
S5_ROWS = S5_BLOCK_TOK // S5_T
S5_NPOW = 7

MOE_ROW_WORDS = D_MODEL // 2 + 128
MOE_TILE = 512
MOE_ROWS = N_TOK + MOE_GROUPS * MOE_TILE
MOE_TOK_BLOCK = 1024
MOE_PAIR_MEMBERS = ((0, 1), (0, 2), (0, 3), (1, 3), (1, 2), (2, 3))
MOE_PAIRS = len(MOE_PAIR_MEMBERS)

CTX_HEADS_PER_DOT = 4
NA_HEADS_PER_DOT = 2
NA_QROWS = 4
NA_KROWS = 12

VMEM_LIMIT = 56 * 1024 * 1024


def _dot(a, b):
    return jnp.dot(a, b, preferred_element_type=F32)


def _dot_nt(a, b):
    return lax.dot_general(a, b, (((1,), (1,)), ((), ())), preferred_element_type=F32)


def _split_bf16(x):
    hi = x.astype(BF16)
    lo = (x - hi.astype(F32)).astype(BF16)
    return hi, lo


def _pack_pairs(x):
    w = x.shape[1] // 2
    xb = x.astype(BF16).astype(F32)
    return (pltpu.bitcast(xb[:, :w], jnp.uint32) & jnp.uint32(0xFFFF0000)) | (
        pltpu.bitcast(xb[:, w:], jnp.uint32) >> 16)


def _unpack_pairs(word):
    return jnp.concatenate([pltpu.bitcast(word & jnp.uint32(0xFFFF0000), F32),
                            pltpu.bitcast(word << 16, F32)], axis=1)


def _rms(x, g):
    return x * lax.rsqrt(jnp.mean(x * x, axis=-1, keepdims=True) + EPS) * g


def _cond_index(i, tm):
    nctx = N_CTX_TOK // tm
    return jnp.where(i < nctx, 0, 1 + ((i - nctx) * tm) // DEC_SEQ)


def _ada_kernel(c_ref, w_ref, b_ref, o_ref):
    c = c_ref[...]
    s = c * jax.nn.sigmoid(c)
    s_hi, s_lo = _split_bf16(s)
    w_hi, w_lo = _split_bf16(w_ref[...])
    o_ref[...] = _dot(s_hi, w_hi) + _dot(s_lo, w_hi) + _dot(s_hi, w_lo) + b_ref[...]


def _ada_mod(cond, w_ada, b_ada):
    tn = 1536
    n = 6 * D_MODEL
    out = pl.pallas_call(
        _ada_kernel,
        out_shape=jax.ShapeDtypeStruct((DEPTH, N_COND, n), F32),
        grid=(DEPTH, n // tn),
        in_specs=[
            pl.BlockSpec((N_COND, D_MODEL), lambda l, j: (0, 0)),
            pl.BlockSpec((None, D_MODEL, tn), lambda l, j: (l, 0, j)),
            pl.BlockSpec((None, 1, tn), lambda l, j: (l, 0, j)),
        ],
        out_specs=pl.BlockSpec((None, N_COND, tn), lambda l, j: (l, 0, j)),
        compiler_params=pltpu.CompilerParams(
            dimension_semantics=("parallel", "parallel"), vmem_limit_bytes=VMEM_LIMIT),
        name="ada_mod",
    )(cond, w_ada, b_ada.reshape(DEPTH, 1, n))
    return out.reshape(DEPTH, N_COND, 6, D_MODEL)


INPROJ_TM = 1024
INPROJ_TN = 1536


def _inproj_kernel(xc_ref, xl_ref, g_ref, mod_ref, w_ref, z_ref, u_ref, cna_ref, cgq_ref, h_sc):
    i, j = pl.program_id(0), pl.program_id(1)
    is_ctx = i < N_CTX_TOK // INPROJ_TM

    def norm_mod(x_ref):
        h = _rms(x_ref[...], g_ref[...]) * (1.0 + mod_ref[1:2, :]) + mod_ref[0:1, :]
        h_sc[...] = h.astype(BF16)

    @pl.when((j == 0) & is_ctx)
    def _():
        norm_mod(xc_ref)

    @pl.when((j == 0) & jnp.logical_not(is_ctx))
    def _():
        norm_mod(xl_ref)

    acc = _dot(h_sc[...], w_ref[...])
    z_ref[...] = acc.astype(BF16)

    def rows(b):
        return slice(b * SEQ, (b + 1) * SEQ)

    def cols(c, width):
        return slice(c % INPROJ_TN, c % INPROJ_TN + width)

    @pl.when(is_ctx & (j == COL_NA_K // INPROJ_TN))
    def _():
        for b in range(INPROJ_TM // SEQ):
            cna_ref[b, 0] = acc[rows(b), cols(COL_NA_K, NA_WIDTH)]
            cna_ref[b, 1] = acc[rows(b), cols(COL_NA_V, NA_WIDTH)]

    @pl.when(is_ctx & (j == COL_G_K // INPROJ_TN))
    def _():
        for b in range(INPROJ_TM // SEQ):
            cgq_ref[b, 0] = acc[rows(b), cols(COL_G_K, GQA_KV_WIDTH)]
            cgq_ref[b, 1] = acc[rows(b), cols(COL_G_V, GQA_KV_WIDTH)]

    @pl.when(j == COL_U // INPROJ_TN)
    def _():
        u_ref[...] = acc[:, cols(COL_U, S5_CH)]


def _token_arrays(x):
    return (x, x, N_CTX_TOK) if not isinstance(x, tuple) else (x[0], x[1], 0)


def _inproj(x, g, mod, w, l):
    tm, tn = INPROJ_TM, INPROJ_TN
    nb = tm // SEQ
    last_ctx = N_CTX_TOK // tm - 1
    x_ctx, x_lat, lat0 = _token_arrays(x)
    return pl.pallas_call(
        _inproj_kernel,
        out_shape=(jax.ShapeDtypeStruct((N_TOK, Z_COLS), BF16),
                   jax.ShapeDtypeStruct((N_TOK, S5_CH), F32),
                   jax.ShapeDtypeStruct((BATCH, 2, SEQ, NA_WIDTH), F32),
                   jax.ShapeDtypeStruct((BATCH, 2, SEQ, GQA_KV_WIDTH), F32)),
        grid=(N_TOK // tm, Z_COLS // tn),
        in_specs=[
            pl.BlockSpec((tm, D_MODEL), lambda i, j: (jnp.minimum(i, last_ctx), 0)),
            pl.BlockSpec((tm, D_MODEL), lambda i, j: (jnp.maximum(i - last_ctx - 1, 0) + lat0 // tm, 0)),
            pl.BlockSpec((1, D_MODEL), lambda i, j: (0, 0)),
            pl.BlockSpec((None, None, 6, D_MODEL), lambda i, j: (l, _cond_index(i, tm), 0, 0)),
            pl.BlockSpec((None, D_MODEL, tn), lambda i, j: (l, 0, j)),
        ],
        out_specs=(pl.BlockSpec((tm, tn), lambda i, j: (i, j)),
                   pl.BlockSpec((tm, S5_CH), lambda i, j: (i, 0)),
                   pl.BlockSpec((nb, 2, SEQ, NA_WIDTH), lambda i, j: (jnp.minimum(i, last_ctx), 0, 0, 0)),
                   pl.BlockSpec((nb, 2, SEQ, GQA_KV_WIDTH), lambda i, j: (jnp.minimum(i, last_ctx), 0, 0, 0))),
        scratch_shapes=[pltpu.VMEM((tm, D_MODEL), BF16)],
        compiler_params=pltpu.CompilerParams(
            dimension_semantics=("arbitrary", "arbitrary"), vmem_limit_bytes=VMEM_LIMIT),
        name="inproj",
    )(x_ctx, x_lat, g, mod, w)


def _block_diag_heads(x, nrep):
    t = x.shape[0]
    rows = lax.broadcasted_iota(jnp.int32, (nrep * t, nrep * HEAD_DIM), 0) // t
    lanes = lax.broadcasted_iota(jnp.int32, (nrep * t, nrep * HEAD_DIM), 1) // HEAD_DIM
    return jnp.where(rows == lanes, jnp.concatenate([x] * nrep, axis=0), jnp.zeros((), x.dtype))


def _per_head(cols, shape):
    head = lax.broadcasted_iota(jnp.int32, shape, 1) // HEAD_DIM
    out = cols[-1]
    for h in range(len(cols) - 2, -1, -1):
        out = jnp.where(head == h, cols[h], out)
    return out


def _attend_heads(q, k, v, sinks):
    nh = q.shape[1] // HEAD_DIM
    t = k.shape[0]
    s = _dot_nt(q * HEAD_DIM ** -0.5, _block_diag_heads(k, nh))
    ps, ds = [], []
    for h in range(nh):
        sh = s[:, h * t:(h + 1) * t]
        m = jnp.max(sh, axis=-1, keepdims=True)
        if sinks is not None:
            m = jnp.maximum(m, sinks[h])
        p = jnp.exp(sh - m)
        d = jnp.sum(p, axis=-1, keepdims=True)
        if sinks is not None:
            d = d + jnp.exp(sinks[h] - m)
        ps.append(p.astype(BF16))
        ds.append(d)
    o = _dot(jnp.concatenate(ps, axis=1), _block_diag_heads(v, nh))
    return o / _per_head(ds, o.shape)


def _ctx_attn_kernel(sink_ref, na_ref, gq_ref, gkv_ref, oa_ref, ob_ref):
    hb = CTX_HEADS_PER_DOT
    w = hb * HEAD_DIM
    na = na_ref[...]
    gq = gq_ref[...]
    gkv = gkv_ref[...]
    for g in range(NA_HEADS // hb):
        o = _attend_heads(na[:, COL_NA_Q + g * w:COL_NA_Q + (g + 1) * w],
                          na[:, COL_NA_K + g * w:COL_NA_K + (g + 1) * w],
                          na[:, COL_NA_V + g * w:COL_NA_V + (g + 1) * w], None)
        oa_ref[:, g * w:(g + 1) * w] = o.astype(BF16)
    for g in range(GQA_Q_HEADS // hb):
        kv_heads = [(g * hb + h) // GQA_GROUP for h in range(hb)]
        k = jnp.concatenate([gkv[:, kh * HEAD_DIM:(kh + 1) * HEAD_DIM] for kh in kv_heads], axis=1)
        v = jnp.concatenate([gkv[:, GQA_KV_WIDTH + kh * HEAD_DIM:GQA_KV_WIDTH + (kh + 1) * HEAD_DIM]
                             for kh in kv_heads], axis=1)
        o = _attend_heads(gq[:, g * w:(g + 1) * w], k, v, [sink_ref[g * hb + h] for h in range(hb)])
        ob_ref[:, g * w:(g + 1) * w] = o.astype(BF16)


def _ctx_attn(z, sink):
    out = jax.ShapeDtypeStruct((N_CTX_TOK, NA_WIDTH), BF16)
    return pl.pallas_call(
        _ctx_attn_kernel,
        out_shape=(out, out),
        grid=(BATCH,),
        in_specs=[
            pl.BlockSpec(memory_space=pltpu.SMEM),
            pl.BlockSpec((SEQ, 3 * NA_WIDTH), lambda b: (b, 0)),
            pl.BlockSpec((SEQ, GQA_Q_WIDTH), lambda b: (b, COL_G_Q // GQA_Q_WIDTH)),
            pl.BlockSpec((SEQ, 2 * GQA_KV_WIDTH), lambda b: (b, COL_G_K // (2 * GQA_KV_WIDTH))),
        ],
        out_specs=(pl.BlockSpec((SEQ, NA_WIDTH), lambda b: (b, 0)),
                   pl.BlockSpec((SEQ, GQA_Q_WIDTH), lambda b: (b, 0))),
        compiler_params=pltpu.CompilerParams(
            dimension_semantics=("parallel",), vmem_limit_bytes=VMEM_LIMIT),
        name="ctx_attn",
    )(sink, z, z, z)


def _na_bias_tables(rpb):
    n_dc = 2 * NA_WIN_W - 1
    qc = np.arange(GRID_W)
    kc = np.arange(GRID_W)
    cstart = np.clip(qc - NA_WIN_W // 2, 0, GRID_W - NA_WIN_W)
    col_ok = (kc[None, :] >= cstart[:, None]) & (kc[None, :] < cstart[:, None] + NA_WIN_W)
    dc = np.clip(kc[None, :] - qc[:, None] + (NA_WIN_W - 1), 0, n_dc - 1)
    onehot = ((np.arange(n_dc)[:, None, None] == dc[None]) & col_ok[None]).astype(np.float32)
    band = jnp.einsum("hrd,dqk->hrqk", rpb.astype(F32), onehot, precision=lax.Precision.HIGHEST)
    band = jnp.where(col_ok, band, NEG_INF)
    band = jnp.pad(band, ((0, 0), (1, 1), (0, 0), (0, 0)), constant_values=NEG_INF)
    return jnp.concatenate([band[:, :-1], band[:, 1:]], axis=-1)


def _na_lat_kernel(q_ref, k_ref, v_ref, ck_ref, cv_ref, bias_ref, rowmask_ref, o_ref):
    i = pl.program_id(1)
    scale = HEAD_DIM ** -0.5
    nk = NA_KROWS * GRID_W
    ks = jnp.clip(i * NA_QROWS - NA_WIN_H // 2, 0, GRID_ROWS - NA_KROWS)
    start = pl.multiple_of(ks * GRID_W, GRID_W)
    q = q_ref[...].astype(BF16)
    kw = k_ref[pl.ds(start, nk), :].astype(BF16)
    vw = v_ref[pl.ds(start, nk), :].astype(BF16)
    ck = ck_ref[...].astype(BF16)
    cv = cv_ref[...].astype(BF16)
    planes = []
    for qr in range(NA_QROWS):
        r = i * NA_QROWS + qr
        planes.append([jnp.clip(ks + 2 * m - r + NA_WIN_H, 0, 2 * NA_WIN_H - 1) for m in range(NA_KROWS // 2)])
    hb = NA_HEADS_PER_DOT
    w = hb * HEAD_DIM
    qrow = lax.broadcasted_iota(jnp.int32, (NA_QROWS * GRID_W, 128), 0) // GRID_W
    qsel = (lax.broadcasted_iota(jnp.int32, (NA_QROWS * GRID_W, 128), 1) == qrow).astype(BF16)
    rowmask = rowmask_ref[...]
    for g in range(NA_HEADS // hb):
        lanes = slice(g * w, (g + 1) * w)
        qg = jnp.concatenate([q[:, lanes] * scale, qsel], axis=1)
        kg = jnp.concatenate([_block_diag_heads(kw[:, lanes], hb), rowmask], axis=1)
        s_loc = _dot_nt(qg, kg)
        s_ctx = _dot_nt(q[:, lanes] * scale, _block_diag_heads(ck[:, lanes], hb))
        p_locs, p_ctxs, ds = [], [], []
        for hh in range(hb):
            bias = jnp.concatenate(
                [jnp.concatenate([bias_ref[g * hb + hh, plane] for plane in row], axis=1) for row in planes], axis=0)
            sl = s_loc[:, hh * nk:(hh + 1) * nk] + bias
            sc = s_ctx[:, hh * PAST_LEN:(hh + 1) * PAST_LEN]
            m = jnp.maximum(jnp.max(sl, axis=-1, keepdims=True), jnp.max(sc, axis=-1, keepdims=True))
            p_loc = jnp.exp(sl - m)
            p_ctx = jnp.exp(sc - m)
            ds.append(jnp.sum(p_loc, axis=-1, keepdims=True) + jnp.sum(p_ctx, axis=-1, keepdims=True))
            p_locs.append(p_loc.astype(BF16))
            p_ctxs.append(p_ctx.astype(BF16))
        o = (_dot(jnp.concatenate(p_locs, axis=1), _block_diag_heads(vw[:, lanes], hb))
             + _dot(jnp.concatenate(p_ctxs, axis=1), _block_diag_heads(cv[:, lanes], hb)))
        o_ref[:, lanes] = (o / _per_head(ds, o.shape)).astype(BF16)


def _na_row_masks():
    kh = min(NA_WIN_H, GRID_ROWS)
    nsteps = GRID_ROWS // NA_QROWS
    out = np.zeros((3, NA_HEADS_PER_DOT, NA_KROWS, GRID_W, 128), np.float32)
    for pat, i in enumerate((0, 1, nsteps - 1)):
        ks = min(max(i * NA_QROWS - NA_WIN_H // 2, 0), GRID_ROWS - NA_KROWS)
        for qr in range(NA_QROWS):
            r = i * NA_QROWS + qr
            st = min(max(r - kh // 2, 0), GRID_ROWS - kh)
            for kr in range(NA_KROWS):
                if not st <= ks + kr < st + kh:
                    out[pat, :, kr, :, qr] = NEG_INF
    return jnp.asarray(out.reshape(3, NA_HEADS_PER_DOT * NA_KROWS * GRID_W, 128), BF16)


def _na_lat(z, cache_na, bias, l):
    tq = NA_QROWS * GRID_W
    nsteps = GRID_ROWS // NA_QROWS
    lat_blk = N_CTX_TOK // DEC_SEQ
    return pl.pallas_call(
        _na_lat_kernel,
        out_shape=jax.ShapeDtypeStruct((N_LAT_TOK, NA_WIDTH), BF16),
        grid=(DEC_BATCH, nsteps),
        in_specs=[
            pl.BlockSpec((tq, NA_WIDTH), lambda b, i: (N_CTX_TOK // tq + b * nsteps + i, COL_NA_Q // NA_WIDTH)),
            pl.BlockSpec((DEC_SEQ, NA_WIDTH), lambda b, i: (lat_blk + b, COL_NA_K // NA_WIDTH)),
            pl.BlockSpec((DEC_SEQ, NA_WIDTH), lambda b, i: (lat_blk + b, COL_NA_V // NA_WIDTH)),
            pl.BlockSpec((None, None, None, PAST_LEN, NA_WIDTH), lambda b, i: (b, l, 0, 0, 0)),
            pl.BlockSpec((None, None, None, PAST_LEN, NA_WIDTH), lambda b, i: (b, l, 1, 0, 0)),
            pl.BlockSpec((NA_HEADS, 2 * NA_WIN_H, GRID_W, 2 * GRID_W), lambda b, i: (0, 0, 0, 0)),
            pl.BlockSpec((None, NA_HEADS_PER_DOT * NA_KROWS * GRID_W, 128),
                         lambda b, i: (jnp.where(i == 0, 0, jnp.where(i == nsteps - 1, 2, 1)), 0, 0)),
        ],
        out_specs=pl.BlockSpec((tq, NA_WIDTH), lambda b, i: (b * nsteps + i, 0)),
        compiler_params=pltpu.CompilerParams(
            dimension_semantics=("parallel", "arbitrary"), vmem_limit_bytes=VMEM_LIMIT),
        name="na_lat",
    )(z, z, z, cache_na, cache_na, bias, _na_row_masks())


def _rope_tables():
    nf = HEAD_DIM // 4
    t = jnp.arange(DEC_SEQ)
    pos = jnp.stack([t // GRID_W, t % GRID_W], axis=-1).astype(F32)
    inv = ROPE_THETA ** (-jnp.arange(nf, dtype=F32) / nf)
    ang = pos[:, :, None] * inv
    cos = jnp.cos(ang)
    sin = jnp.sin(ang)
    cos_d = jnp.stack([cos, cos], axis=2).reshape(DEC_SEQ, HEAD_DIM)
    sin_d = jnp.stack([-sin, sin], axis=2).reshape(DEC_SEQ, HEAD_DIM)
    return cos_d, sin_d


def _rope(x, cos, sin_signed):
    n = x.shape[-1]
    nf = HEAD_DIM // 4
    lane = lax.broadcasted_iota(jnp.int32, x.shape, 1)
    first_half = (lane // nf) % 2 == 0
    partner = jnp.where(first_half, pltpu.roll(x, n - nf, 1), pltpu.roll(x, nf, 1))
    return x * cos + partner * sin_signed


def _swa_lat_kernel(sink_ref, q_ref, kv_ref, cq_ref, sq_ref, ckt_ref, skt_ref, ck_ref, cv_ref, o_ref, k_sc, v_sc):
    n = pl.program_id(1)
    scale = HEAD_DIM ** -0.5
    nwin = 3 * SWA_BLOCK

    @pl.when(n == 0)
    def _():
        kv = kv_ref[...]
        k_sc[...] = _rope(kv[:, :GQA_KV_WIDTH].astype(F32), ckt_ref[...], skt_ref[...]).astype(BF16)
        v_sc[...] = kv[:, GQA_KV_WIDTH:].astype(BF16)

    q = (_rope(q_ref[...].astype(F32), cq_ref[...], sq_ref[...]) * scale).astype(BF16)
    start = pl.multiple_of(jnp.clip((n - 1) * SWA_BLOCK, 0, DEC_SEQ - nwin), SWA_BLOCK)
    kw = k_sc[pl.ds(start, nwin), :]
    vw = v_sc[pl.ds(start, nwin), :]
    ck = ck_ref[...].astype(BF16)
    cv = cv_ref[...].astype(BF16)
    rows = GQA_GROUP * SWA_BLOCK
    row = lax.broadcasted_iota(jnp.int32, (rows, nwin), 0)
    col = lax.broadcasted_iota(jnp.int32, (rows, nwin), 1)
    qpos = n * SWA_BLOCK + row % SWA_BLOCK
    kpos = start + col
    ok = jnp.abs(qpos - kpos) <= SWA_WINDOW
    grp = lax.broadcasted_iota(jnp.int32, (rows, 1), 0) // SWA_BLOCK
    for kh in range(GQA_KV_HEADS):
        sl = slice(kh * HEAD_DIM, (kh + 1) * HEAD_DIM)
        q4 = jnp.concatenate(
            [q[:, (kh * GQA_GROUP + g) * HEAD_DIM:(kh * GQA_GROUP + g + 1) * HEAD_DIM] for g in range(GQA_GROUP)],
            axis=0)
        sink = jnp.zeros((rows, 1), F32)
        for g in range(GQA_GROUP):
            sink = jnp.where(grp == g, sink_ref[kh * GQA_GROUP + g], sink)
        s_loc = jnp.where(ok, _dot_nt(q4, kw[:, sl]), NEG_INF)
        s_ctx = _dot_nt(q4, ck[:, sl])
        m = jnp.maximum(jnp.maximum(jnp.max(s_loc, axis=-1, keepdims=True),
                                    jnp.max(s_ctx, axis=-1, keepdims=True)), sink)
        p_loc = jnp.exp(s_loc - m)
        p_ctx = jnp.exp(s_ctx - m)
        d = (jnp.sum(p_loc, axis=-1, keepdims=True) + jnp.sum(p_ctx, axis=-1, keepdims=True)
             + jnp.exp(sink - m))
        o4 = (_dot(p_loc.astype(BF16), vw[:, sl]) + _dot(p_ctx.astype(BF16), cv[:, sl])) / d
        for g in range(GQA_GROUP):
            h = kh * GQA_GROUP + g
            o_ref[:, h * HEAD_DIM:(h + 1) * HEAD_DIM] = o4[g * SWA_BLOCK:(g + 1) * SWA_BLOCK].astype(BF16)


def _swa_lat(z, cache_gqa, sink, cos_q, sin_q, cos_k, sin_k, l):
    nb = DEC_SEQ // SWA_BLOCK
    lat_blk = N_CTX_TOK // DEC_SEQ
    return pl.pallas_call(
        _swa_lat_kernel,
        out_shape=jax.ShapeDtypeStruct((N_LAT_TOK, GQA_Q_WIDTH), BF16),
        grid=(DEC_BATCH, nb),
        in_specs=[
            pl.BlockSpec(memory_space=pltpu.SMEM),
            pl.BlockSpec((SWA_BLOCK, GQA_Q_WIDTH),
                         lambda b, n: (N_CTX_TOK // SWA_BLOCK + b * nb + n, COL_G_Q // GQA_Q_WIDTH)),
            pl.BlockSpec((DEC_SEQ, 2 * GQA_KV_WIDTH), lambda b, n: (lat_blk + b, COL_G_K // (2 * GQA_KV_WIDTH))),
            pl.BlockSpec((SWA_BLOCK, GQA_Q_WIDTH), lambda b, n: (n, 0)),
            pl.BlockSpec((SWA_BLOCK, GQA_Q_WIDTH), lambda b, n: (n, 0)),
            pl.BlockSpec((DEC_SEQ, GQA_KV_WIDTH), lambda b, n: (0, 0)),
            pl.BlockSpec((DEC_SEQ, GQA_KV_WIDTH), lambda b, n: (0, 0)),
            pl.BlockSpec((None, None, None, PAST_LEN, GQA_KV_WIDTH), lambda b, n: (b, l, 0, 0, 0)),
            pl.BlockSpec((None, None, None, PAST_LEN, GQA_KV_WIDTH), lambda b, n: (b, l, 1, 0, 0)),
        ],
        out_specs=pl.BlockSpec((SWA_BLOCK, GQA_Q_WIDTH), lambda b, n: (b * nb + n, 0)),
        scratch_shapes=[pltpu.VMEM((DEC_SEQ, GQA_KV_WIDTH), BF16), pltpu.VMEM((DEC_SEQ, GQA_KV_WIDTH), BF16)],
        compiler_params=pltpu.CompilerParams(
            dimension_semantics=("parallel", "arbitrary"), vmem_limit_bytes=VMEM_LIMIT),
        name="swa_lat",
    )(sink, z, z, cos_q, sin_q, cos_k, sin_k, cache_gqa, cache_gqa)


S5_OP_KEYS = ("kt", "bt_re", "bt_im", "ct_re", "ct_im", "pin_re", "pin_im", "po_re", "po_im", "apr", "api")


def _s5_operators(a_re, a_im, log_dt, b_re, b_im, c_re, c_im):
    T, G, N, C = S5_T, S5_GROUPS, S5_STATE, S5_GROUP_CH
    tau = jnp.arange(T + 1, dtype=F32)
    out = {k: [] for k in S5_OP_KEYS}
    for d in range(2):
        A = lax.complex(a_re[d].astype(F32), a_im[d].astype(F32))
        dt = jnp.exp(log_dt[d].astype(F32))[:, None]
        a_bar = jnp.exp(A * dt)
        pw = jnp.exp((A * dt)[None] * tau[:, None, None])
        b_bar = ((a_bar - 1.0) / A)[..., None] * lax.complex(b_re[d].astype(F32), b_im[d].astype(F32))
        c_mat = lax.complex(c_re[d].astype(F32), c_im[d].astype(F32))
        kern = jnp.einsum("gon,tgn,gni->gtoi", c_mat, pw[:T], b_bar, precision=lax.Precision.HIGHEST).real
        kern = lax.optimization_barrier(kern).transpose(0, 3, 1, 2)
        if d == 0:
            p_in = pw[:T][::-1]
            p_out = pw[1:T + 1]
        else:
            p_in = pw[:T]
            p_out = pw[1:T + 1][::-1]
            kern = kern[:, :, ::-1]
        bt, ct, po = b_bar.transpose(0, 2, 1), c_mat.transpose(0, 2, 1), p_out.transpose(1, 2, 0)
        pw2 = jnp.exp((A * dt)[None] * (T * 2.0 ** jnp.arange(S5_NPOW, dtype=F32))[:, None, None])
        vals = (kern.reshape(G, C, T * C), bt.real, bt.imag, ct.real, ct.imag,
                p_in.real.reshape(T, S5_PAIRS, 1, 2 * N), p_in.imag.reshape(T, S5_PAIRS, 1, 2 * N), po.real, po.imag,
                pw2.real.reshape(S5_NPOW, G * N), pw2.imag.reshape(S5_NPOW, G * N))
        for k, v in zip(S5_OP_KEYS, vals):
            out[k].append(v)
    return {k: jnp.stack(v) for k, v in out.items()}


def _s5_expand(kt_ref, btr_ref, bti_ref, ctr_ref, cti_ref, pir_ref, pii_ref, por_ref, poi_ref,
               bs_sc, m_sc, cre_sc, cim_sc):
    T, C, N = S5_T, S5_GROUP_CH, S5_STATE
    pc, half, width = 2 * C, 2 * N, 2 * T * C
    lane = lax.broadcasted_iota(jnp.int32, (pc, width), 1)
    zc, zn, zo = jnp.zeros((C, C), F32), jnp.zeros((C, N), F32), jnp.zeros((N, C), F32)

    def own_half(x, z, g):
        return [x, z] if g == 0 else [z, x]

    def expand_pair(p, d):
        k0, bx_re, bx_im, co_re, co_im = [], [], [], [], []
        for g in range(2):
            gl = 2 * p + g
            ktg = kt_ref[d, gl]
            k0.append(jnp.concatenate(
                [blk for lag in range(T) for blk in own_half(ktg[:, lag * C:(lag + 1) * C], zc, g)], axis=1))
            bx_re.append(jnp.concatenate(own_half(btr_ref[d, gl], zn, g), axis=1))
            bx_im.append(jnp.concatenate(own_half(bti_ref[d, gl], zn, g), axis=1))
            cr, ci = ctr_ref[d, gl], cti_ref[d, gl]
            pr_all, pi_all = por_ref[d, gl], poi_ref[d, gl]
            re_p, im_p = [], []
            for j in range(T):
                pr, pi = pr_all[:, j:j + 1], pi_all[:, j:j + 1]
                re_p += own_half(cr * pr - ci * pi, zo, g)
                im_p += own_half(cr * pi + ci * pr, zo, g)
            co_re.append(jnp.concatenate(re_p, axis=1))
            co_im.append(jnp.concatenate(im_p, axis=1))
        k0 = jnp.concatenate(k0, axis=0)
        bx_re, bx_im = jnp.concatenate(bx_re, axis=0), jnp.concatenate(bx_im, axis=0)
        for s in range(T):
            if d == 0:
                sh = pc * s
                blk = jnp.where(lane >= sh, pltpu.roll(k0, sh, 1), 0.0) if sh else k0
            else:
                sh = pc * (T - 1 - s)
                blk = jnp.where(lane < width - sh, pltpu.roll(k0, width - sh, 1), 0.0) if sh else k0
            m_sc[d, p, s * pc:(s + 1) * pc, :] = blk.astype(BF16)
            pr, pi = pir_ref[d, s, p], pii_ref[d, s, p]
            bs_sc[d, p, s * pc:(s + 1) * pc, :] = jnp.concatenate(
                [pr * bx_re - pi * bx_im, pr * bx_im + pi * bx_re], axis=1).astype(BF16)
        cre_sc[d, p] = jnp.concatenate(co_re, axis=0).astype(BF16)
        cim_sc[d, p] = (-jnp.concatenate(co_im, axis=0)).astype(BF16)

    for d in range(2):
        lax.fori_loop(0, S5_OCT_PAIRS, lambda p, c, d=d: (expand_pair(p, d), c)[1], 0)


def _s5_kernel(u_ref, kt_ref, btr_ref, bti_ref, ctr_ref, cti_ref, pir_ref, pii_ref, por_ref, poi_ref, apr_ref,
               api_ref, h0_ref, y_ref, fin_ref, fin_sc, bs_sc, m_sc, cre_sc, cim_sc):
    t = pl.program_id(1)
    nctx = N_CTX_TOK // S5_BLOCK_TOK

    @pl.when(t == 0)
    def _():
        _s5_expand(kt_ref, btr_ref, bti_ref, ctr_ref, cti_ref, pir_ref, pii_ref, por_ref, poi_ref,
                   bs_sc, m_sc, cre_sc, cim_sc)

    refs = (u_ref, bs_sc, m_sc, cre_sc, cim_sc, apr_ref, api_ref, h0_ref, y_ref, fin_ref, fin_sc)

    @pl.when(t < nctx)
    def _():
        _s5_block(*refs, t, kseq=SEQ // S5_T, has_h0=False)

    @pl.when(t >= nctx)
    def _():
        _s5_block(*refs, t - nctx, kseq=DEC_SEQ // S5_T, has_h0=True)


def _s5_block(u_ref, bs_ref, m_ref, cre_ref, cim_ref, apr_ref, api_ref, h0_ref, y_ref, fin_ref, fin_sc, t, *,
              kseq, has_h0):
    rows, half, pc = S5_ROWS, 2 * S5_STATE, 2 * S5_GROUP_CH
    k = lax.broadcasted_iota(jnp.int32, (rows, half), 0) % kseq
    shifts = [1 << i for i in range(kseq.bit_length() - 1)]
    xs = [u_ref[pl.ds(s, rows, stride=S5_T), :].astype(BF16) for s in range(S5_T)]
    ys = []
    for p in range(S5_OCT_PAIRS):
        lanes = slice(p * half, (p + 1) * half)
        xp = jnp.concatenate([x[:, p * pc:(p + 1) * pc] for x in xs], axis=1)
        acc = None
        for d in range(2):
            inc = _dot(xp, bs_ref[d, p])
            sr, si = inc[:, :half], inc[:, half:]
            if has_h0:
                nseq = rows // kseq
                h0r, h0i = h0_ref[d, 0, t * nseq, :, lanes], h0_ref[d, 1, t * nseq, :, lanes]
                seq = lax.broadcasted_iota(jnp.int32, (rows, half), 0) // kseq
                for j in range(1, nseq):
                    h0r = jnp.where(seq == j, h0_ref[d, 0, t * nseq + j, :, lanes], h0r)
                    h0i = jnp.where(seq == j, h0_ref[d, 1, t * nseq + j, :, lanes], h0i)
                ar, ai = apr_ref[d, 0:1, lanes], api_ref[d, 0:1, lanes]
                first = (k == 0) if d == 0 else (k == kseq - 1)
                sr = sr + jnp.where(first, ar * h0r - ai * h0i, 0.0)
                si = si + jnp.where(first, ar * h0i + ai * h0r, 0.0)
            else:
                h0r = h0i = 0.0
            for i, sh in enumerate(shifts):
                ar, ai = apr_ref[d, i:i + 1, lanes], api_ref[d, i:i + 1, lanes]
                ok = (k >= sh) if d == 0 else (k < kseq - sh)
                amt = sh if d == 0 else rows - sh
                rr = jnp.where(ok, pltpu.roll(sr, amt, 0), 0.0)
                ri = jnp.where(ok, pltpu.roll(si, amt, 0), 0.0)
                sr, si = sr + ar * rr - ai * ri, si + ar * ri + ai * rr
            inner = (k >= 1) if d == 0 else (k < kseq - 1)
            amt = 1 if d == 0 else rows - 1
            hr = jnp.where(inner, pltpu.roll(sr, amt, 0), h0r)
            hi = jnp.where(inner, pltpu.roll(si, amt, 0), h0i)
            yd = (_dot(xp, m_ref[d, p]) + _dot(hr.astype(BF16), cre_ref[d, p])
                  + _dot(hi.astype(BF16), cim_ref[d, p]))
            acc = yd if acc is None else acc + yd
            if not has_h0:
                last = kseq - 1 if d == 0 else 0
                fin_sc[0] = sr
                fin_sc[1] = si
                fin_ref[d, 0, :, lanes] = fin_sc[0, pl.ds(last, rows // kseq, stride=kseq), :]
                fin_ref[d, 1, :, lanes] = fin_sc[1, pl.ds(last, rows // kseq, stride=kseq), :]
        ys.append(acc)
    for j in range(S5_T):
        y_ref[pl.ds(j, rows, stride=S5_T), :] = jnp.concatenate([y[:, j * pc:(j + 1) * pc] for y in ys], axis=1)


def _s5_scan(u, ops, h0, l):
    nblk, nctx = N_TOK // S5_BLOCK_TOK, N_CTX_TOK // S5_BLOCK_TOK
    noct = S5_PAIRS // S5_OCT_PAIRS
    half, gn = 2 * S5_STATE, S5_GROUPS * S5_STATE
    pw = 2 * S5_T * S5_GROUP_CH
    nseq = S5_BLOCK_TOK // SEQ
    ngrp = 2 * S5_OCT_PAIRS
    C, N, T = S5_GROUP_CH, S5_STATE, S5_T
    gspec = lambda r, c: pl.BlockSpec((None, 2, ngrp, r, c), lambda q, t: (l, 0, q, 0, 0))
    lspec = lambda r: pl.BlockSpec((None, 2, r, S5_OCT_PAIRS * half), lambda q, t: (l, 0, 0, q))
    pspec = pl.BlockSpec((None, 2, T, S5_OCT_PAIRS, 1, half), lambda q, t: (l, 0, 0, q, 0, 0))
    return pl.pallas_call(
        _s5_kernel,
        grid=(noct, nblk),
        in_specs=[
            pl.BlockSpec((S5_BLOCK_TOK, 128), lambda q, t: (t, q)),
            gspec(C, T * C), gspec(C, N), gspec(C, N), gspec(N, C), gspec(N, C), pspec, pspec,
            gspec(N, T), gspec(N, T), lspec(S5_NPOW), lspec(S5_NPOW),
            pl.BlockSpec((2, 2, DEC_BATCH, 1, S5_OCT_PAIRS * half), lambda q, t: (0, 0, 0, 0, q)),
        ],
        out_shape=(jax.ShapeDtypeStruct((N_TOK, S5_CH), F32), jax.ShapeDtypeStruct((2, 2, BATCH, gn), F32)),
        out_specs=(pl.BlockSpec((S5_BLOCK_TOK, 128), lambda q, t: (t, q)),
                   pl.BlockSpec((2, 2, nseq, S5_OCT_PAIRS * half),
                                lambda q, t: (0, 0, jnp.minimum(t, nctx - 1), q))),
        scratch_shapes=[pltpu.VMEM((2, S5_ROWS, half), F32),
                        pltpu.VMEM((2, S5_OCT_PAIRS, pw, 2 * half), BF16), pltpu.VMEM((2, S5_OCT_PAIRS, pw, pw), BF16),
                        pltpu.VMEM((2, S5_OCT_PAIRS, half, pw), BF16), pltpu.VMEM((2, S5_OCT_PAIRS, half, pw), BF16)],
        compiler_params=pltpu.CompilerParams(
            dimension_semantics=("parallel", "arbitrary"), vmem_limit_bytes=VMEM_LIMIT),
        name="s5_scan",
    )(u, *[ops[k] for k in S5_OP_KEYS], h0.reshape(2, 2, DEC_BATCH, 1, gn))


def _route(logits):
    lane = lax.broadcasted_iota(jnp.int32, logits.shape, 1)
    big = jnp.int32(1 << 20)
    is_g = (lane >= MOE_EXPERTS) & (lane < MOE_EXPERTS + MOE_GROUPS)
    lg = jnp.where(is_g, logits, NEG_INF)
    gmax = jnp.max(lg, axis=-1, keepdims=True)
    gsel = jnp.min(jnp.where(is_g & (lg == gmax), lane, big), axis=-1, keepdims=True) - MOE_EXPERTS
    p_group = 1.0 / jnp.sum(jnp.where(is_g, jnp.exp(lg - gmax), 0.0), axis=-1, keepdims=True)
    in_grp = (lane < MOE_EXPERTS) & (lane // MOE_EXPERTS_PER_GROUP == gsel)
    le = jnp.where(in_grp, logits, NEG_INF)
    v1 = jnp.max(le, axis=-1, keepdims=True)
    i1 = jnp.min(jnp.where(in_grp & (le == v1), lane, big), axis=-1, keepdims=True)
    rest = in_grp & (lane != i1)
    le2 = jnp.where(rest, logits, NEG_INF)
    v2 = jnp.max(le2, axis=-1, keepdims=True)
    i2 = jnp.min(jnp.where(rest & (le2 == v2), lane, big), axis=-1, keepdims=True)
    e2 = jnp.exp(v2 - v1)
    w1 = 1.0 / (1.0 + e2)
    w2 = e2 / (1.0 + e2)
    comb = jnp.where(lane == i1, w1 * p_group, 0.0) + jnp.where(lane == i2, w2 * p_group, 0.0)
    a = jnp.minimum(i1, i2) - gsel * MOE_EXPERTS_PER_GROUP
    b = jnp.maximum(i1, i2) - gsel * MOE_EXPERTS_PER_GROUP
    pair = jnp.where(a == 0, b - 1, jnp.where(a == 1, jnp.where(b == 3, 3, 4), 5))
    cls = gsel * MOE_PAIRS + pair
    return jnp.where(lane == MOE_EXPERTS, cls.astype(F32), comb)


def _merge_kernel(xc_ref, xl_ref, oac_ref, oal_ref, obc_ref, obl_ref, u_ref, yc_ref, gates_ref, mod_ref, d_ref,
                  wglu_ref, wa_ref, wb_ref, wc_ref, wout_ref, g2_ref, wrh_ref, wrl_ref, br_ref, x1_ref, hx_ref,
                  *, tm):
    is_ctx = pl.program_id(0) < N_CTX_TOK // tm
    x = jnp.where(is_ctx, xc_ref[...], xl_ref[...])
    oa = jnp.where(is_ctx, oac_ref[...], oal_ref[...])
    ob = jnp.where(is_ctx, obc_ref[...], obl_ref[...])
    y = u_ref[...] * d_ref[...] + yc_ref[...]
    y = y * (0.5 * (1.0 + jnp.tanh(math.sqrt(2.0 / math.pi) * (y + 0.044715 * (y * y * y)))))
    oc = y * jax.nn.sigmoid(_dot(y.astype(BF16), wglu_ref[...]))
    gate = jax.nn.sigmoid(gates_ref[...].astype(F32))
    merged = (gate[:, :D_MODEL] * _dot(oa, wa_ref[...])
              + gate[:, D_MODEL:2 * D_MODEL] * _dot(ob, wb_ref[...])
              + gate[:, 2 * D_MODEL:] * _dot(oc.astype(BF16), wc_ref[...]))
    x1 = x + mod_ref[2:3, :] * _dot(merged.astype(BF16), wout_ref[...])
    x1_ref[...] = x1
    h2 = _rms(x1, g2_ref[...]) * (1.0 + mod_ref[4:5, :]) + mod_ref[3:4, :]
    h_hi, h_lo = _split_bf16(h2)
    logits = _dot(h_hi, wrh_ref[...]) + _dot(h_lo, wrh_ref[...]) + _dot(h_hi, wrl_ref[...]) + br_ref[...]
    half = D_MODEL // 2
    hx_ref[:, :half] = _pack_pairs(h2)
    hx_ref[:, half:] = pltpu.bitcast(_route(logits), jnp.uint32)


def _merge(x, oa_c, oa_l, ob_c, ob_l, u, yc, z, mod, d_s5, wglu, wa, wb, wc, wout, g2, wr_hi, wr_lo, br, l):
    tm = 512
    nctx = N_CTX_TOK // tm
    full = lambda r, c: pl.BlockSpec((r, c), lambda i: (0, 0))
    layer = lambda r, c: pl.BlockSpec((None, r, c), lambda i: (l, 0, 0))
    ctx_blk = lambda w: pl.BlockSpec((tm, w), lambda i: (jnp.minimum(i, nctx - 1), 0))
    lat_blk = lambda w, first=0: pl.BlockSpec((tm, w), lambda i: (jnp.maximum(i - nctx, 0) + first // tm, 0))
    x_ctx, x_lat, lat0 = _token_arrays(x)
    return pl.pallas_call(
        functools.partial(_merge_kernel, tm=tm),
        out_shape=(jax.ShapeDtypeStruct((N_TOK, D_MODEL), F32),
                   jax.ShapeDtypeStruct((N_TOK, MOE_ROW_WORDS), jnp.uint32)),
        grid=(N_TOK // tm,),
        in_specs=[
            ctx_blk(D_MODEL), lat_blk(D_MODEL, lat0),
            ctx_blk(NA_WIDTH), lat_blk(NA_WIDTH), ctx_blk(GQA_Q_WIDTH), lat_blk(GQA_Q_WIDTH),
            pl.BlockSpec((tm, S5_CH), lambda i: (i, 0)),
            pl.BlockSpec((tm, S5_CH), lambda i: (i, 0)),
            pl.BlockSpec((tm, N_BRANCH * D_MODEL), lambda i: (i, COL_GATES // (N_BRANCH * D_MODEL))),
            pl.BlockSpec((None, None, 6, D_MODEL), lambda i: (l, _cond_index(i, tm), 0, 0)),
            full(1, S5_CH), layer(S5_CH, S5_CH), layer(NA_WIDTH, D_MODEL), layer(GQA_Q_WIDTH, D_MODEL),
            layer(S5_CH, D_MODEL), layer(D_MODEL, D_MODEL), full(1, D_MODEL),
            full(D_MODEL, 128), full(D_MODEL, 128), full(1, 128),
        ],
        out_specs=(pl.BlockSpec((tm, D_MODEL), lambda i: (i, 0)),
                   pl.BlockSpec((tm, MOE_ROW_WORDS), lambda i: (i, 0))),
        compiler_params=pltpu.CompilerParams(
            dimension_semantics=("parallel",), vmem_limit_bytes=VMEM_LIMIT),
        name="merge",
    )(x_ctx, x_lat, oa_c, oa_l, ob_c, ob_l, u, yc, z, mod, d_s5, wglu, wa, wb, wc, wout, g2, wr_hi, wr_lo, br)


def _moe_plan(cls):
    ncls = MOE_GROUPS * MOE_PAIRS
    ntiles = MOE_ROWS // MOE_TILE
    onehot = (cls[:, None] == jnp.arange(ncls)[None, :]).astype(jnp.int32)
    rank = jnp.sum((jnp.cumsum(onehot, axis=0) - onehot) * onehot, axis=1)
    ccount = jnp.sum(onehot, axis=0).reshape(MOE_GROUPS, MOE_PAIRS)
    gcount = jnp.sum(ccount, axis=1)
    gpadded = (gcount + MOE_TILE - 1) // MOE_TILE * MOE_TILE
    gend = jnp.cumsum(gpadded)
    cstart = ((gend - gpadded)[:, None] + jnp.cumsum(ccount, axis=1) - ccount).reshape(ncls)
    cend = cstart + ccount.reshape(ncls)
    pos = jnp.sum(onehot * cstart[None, :], axis=1) + rank
    tile_row = jnp.arange(ntiles) * MOE_TILE
    tile_group = jnp.minimum(jnp.sum((tile_row[:, None] >= gend[None, :]).astype(jnp.int32), axis=1),
                             MOE_GROUPS - 1)
    member = np.zeros((ncls, MOE_GROUPS, MOE_EXPERTS_PER_GROUP), np.float32)
    for g in range(MOE_GROUPS):
        for p, pair in enumerate(MOE_PAIR_MEMBERS):
            member[g * MOE_PAIRS + p, g, list(pair)] = 1.0
    overlap = ((cstart[None, :] < tile_row[:, None] + MOE_TILE) & (cend[None, :] > tile_row[:, None])
               & (cend > cstart)[None, :]).astype(F32)
    need = jnp.einsum("tc,cge->tge", overlap, member)
    need = jnp.sum(need * (tile_group[:, None, None] == jnp.arange(MOE_GROUPS)[None, :, None]), axis=1)
    return (pos.astype(jnp.int32), tile_group.astype(jnp.int32), (need > 0).astype(jnp.int32).reshape(-1),
            (gend[-1:] // MOE_TILE).astype(jnp.int32))


def _dispatch_kernel(pos_ref, hx_ref, init_ref, out_ref, sem):
    del init_ref
    base = pl.program_id(0) * MOE_TOK_BLOCK

    def row_copy(r):
        return pltpu.make_async_copy(hx_ref.at[pl.ds(r, 1), :], out_ref.at[pl.ds(pos_ref[base + r], 1), :], sem)

    for r in range(MOE_TOK_BLOCK):
        row_copy(r).start(priority=r % 2)
    for r in range(MOE_TOK_BLOCK):
        row_copy(r).wait()


def _dispatch(hx, pos):
    return pl.pallas_call(
        _dispatch_kernel,
        grid_spec=pltpu.PrefetchScalarGridSpec(
            num_scalar_prefetch=1,
            grid=(N_TOK // MOE_TOK_BLOCK,),
            in_specs=[pl.BlockSpec((MOE_TOK_BLOCK, MOE_ROW_WORDS), lambda i, pos: (i, 0)),
                      pl.BlockSpec(memory_space=pl.ANY)],
            out_specs=pl.BlockSpec(memory_space=pl.ANY),
            scratch_shapes=[pltpu.SemaphoreType.DMA]),
        out_shape=jax.ShapeDtypeStruct((MOE_ROWS, MOE_ROW_WORDS), jnp.uint32),
        input_output_aliases={2: 0},
        compiler_params=pltpu.CompilerParams(dimension_semantics=("arbitrary",)),
        name="moe_dispatch",
    )(pos, hx, jnp.zeros((MOE_ROWS, MOE_ROW_WORDS), jnp.uint32))


def _experts_kernel(tg_ref, need_ref, nu_ref, hx_ref, wg_ref, wu_ref, wd_ref, y_ref, acc_sc):
    t = pl.program_id(0)
    acc_sc[...] = jnp.zeros_like(acc_sc)
    half = D_MODEL // 2
    for e in range(MOE_EXPERTS_PER_GROUP):
        @pl.when((t < nu_ref[0]) & (need_ref[t * MOE_EXPERTS_PER_GROUP + e] > 0))
        def _():
            h = _unpack_pairs(hx_ref[:, :half]).astype(BF16)
            comb = pltpu.bitcast(hx_ref[:, half:], F32)
            lane = lax.broadcasted_iota(jnp.int32, comb.shape, 1)
            ce = jnp.sum(jnp.where(lane == tg_ref[t] * MOE_EXPERTS_PER_GROUP + e, comb, 0.0), axis=-1,
                         keepdims=True)
            a = _dot(h, wg_ref[e])
            b = _dot(h, wu_ref[e])
            act = (a * jax.nn.sigmoid(a)) * b * ce
            acc_sc[...] += _dot(act.astype(BF16), wd_ref[e])

    y_ref[...] = _pack_pairs(acc_sc[...])


def _experts(hs, tile_group, need, n_used, wg, wu, wd, l):
    wspec = lambda r, c: pl.BlockSpec((None, None, MOE_EXPERTS_PER_GROUP, r, c),
                                      lambda t, tg, need, nu: (l, tg[t], 0, 0, 0))
    return pl.pallas_call(
        _experts_kernel,
        grid_spec=pltpu.PrefetchScalarGridSpec(
            num_scalar_prefetch=3,
            grid=(MOE_ROWS // MOE_TILE,),
            in_specs=[pl.BlockSpec((MOE_TILE, MOE_ROW_WORDS), lambda t, tg, need, nu: (t, 0)),
                      wspec(D_MODEL, EXPERT_FF), wspec(D_MODEL, EXPERT_FF), wspec(EXPERT_FF, D_MODEL)],
            out_specs=pl.BlockSpec((MOE_TILE, D_MODEL // 2), lambda t, tg, need, nu: (t, 0)),
            scratch_shapes=[pltpu.VMEM((MOE_TILE, D_MODEL), F32)]),
        out_shape=jax.ShapeDtypeStruct((MOE_ROWS, D_MODEL // 2), jnp.uint32),
        compiler_params=pltpu.CompilerParams(
            dimension_semantics=("arbitrary",), vmem_limit_bytes=VMEM_LIMIT),
        name="moe_experts",
    )(tile_group, need, n_used, hs, wg, wu, wd)


def _combine_kernel(pos_ref, y_ref, x_ref, mod_ref, fg_ref, *rest, final):
    buf, sem = rest[-2:]
    i = pl.program_id(0)
    base = i * MOE_TOK_BLOCK

    def row_copy(r):
        return pltpu.make_async_copy(y_ref.at[pl.ds(pos_ref[base + r], 1), :], buf.at[pl.ds(r, 1), :], sem)

    for r in range(MOE_TOK_BLOCK):
        row_copy(r).start(priority=r % 2)
    for r in range(MOE_TOK_BLOCK):
        row_copy(r).wait()
    x2 = x_ref[...] + mod_ref[5:6, :] * _unpack_pairs(buf[...])
    if not final:
        rest[0][...] = x2
    else:
        y = _rms(x2, fg_ref[...])
        is_ctx = i < N_CTX_TOK // MOE_TOK_BLOCK

        @pl.when(is_ctx)
        def _():
            rest[0][...] = y

        @pl.when(jnp.logical_not(is_ctx))
        def _():
            rest[1][...] = y


def _combine(ys, pos, x1, mod, fg, l, final):
    tm = MOE_TOK_BLOCK
    nctx = N_CTX_TOK // tm
    tok = pl.BlockSpec((tm, D_MODEL), lambda i, pos: (i, 0))
    out = jax.ShapeDtypeStruct((N_TOK, D_MODEL), F32)
    if final:
        out = (jax.ShapeDtypeStruct((N_CTX_TOK, D_MODEL), F32), jax.ShapeDtypeStruct((N_LAT_TOK, D_MODEL), F32))
        out_specs = (pl.BlockSpec((tm, D_MODEL), lambda i, pos: (jnp.minimum(i, nctx - 1), 0)),
                     pl.BlockSpec((tm, D_MODEL), lambda i, pos: (jnp.maximum(i - nctx, 0), 0)))
    else:
        out_specs = tok
    return pl.pallas_call(
        functools.partial(_combine_kernel, final=final),
        grid_spec=pltpu.PrefetchScalarGridSpec(
            num_scalar_prefetch=1,
            grid=(N_TOK // tm,),
            in_specs=[pl.BlockSpec(memory_space=pl.ANY), tok,
                      pl.BlockSpec((None, None, 6, D_MODEL), lambda i, pos: (l, _cond_index(i, tm), 0, 0)),
                      pl.BlockSpec((1, D_MODEL), lambda i, pos: (0, 0))],
            out_specs=out_specs,
            scratch_shapes=[pltpu.VMEM((tm, D_MODEL // 2), jnp.uint32), pltpu.SemaphoreType.DMA]),
        out_shape=out,
        compiler_params=pltpu.CompilerParams(
            dimension_semantics=("arbitrary",), vmem_limit_bytes=VMEM_LIMIT),
        name="moe_combine",
    )(pos, ys, x1, mod, fg)


def _pack_w_in(w):
    qkv = w[..., :COL_G_V + GQA_KV_WIDTH]
    u = w[..., COL_G_V + GQA_KV_WIDTH:COL_G_V + GQA_KV_WIDTH + S5_CH]
    gates = w[..., COL_G_V + GQA_KV_WIDTH + S5_CH:]
    pad = jnp.zeros(w.shape[:-1] + (COL_U - (COL_G_V + GQA_KV_WIDTH),), w.dtype)
    return jnp.concatenate([qkv, pad, u, gates], axis=-1).astype(BF16)


def kernel(x_prompt, x_sample, cache_na_kv, cache_gqa_kv, state_ssm, c, c_ctx, norm_g, w_ada, b_ada, w_in, na_rpb, gqa_sink, s5_a_re, s5_a_im, s5_log_dt, s5_b_re, s5_b_im, s5_c_re, s5_c_im, s5_d, s5_w_glu, w_branch_a, w_branch_b, w_branch_c, w_out, moe_w_group, moe_b_group, moe_w_expert, moe_b_expert, moe_w_gate, moe_w_up, moe_w_down, final_g):
    cond = jnp.zeros((N_COND, D_MODEL), F32).at[0].set(c_ctx.astype(F32)).at[1:1 + DEC_BATCH].set(c.astype(F32))
    mod = _ada_mod(cond, w_ada.astype(F32), b_ada.astype(F32))

    x = (x_prompt.astype(F32).reshape(N_CTX_TOK, D_MODEL), x_sample.astype(F32).reshape(N_LAT_TOK, D_MODEL))
    cache_na = cache_na_kv.reshape(DEC_BATCH, DEPTH, 2, PAST_LEN, NA_WIDTH)
    cache_gqa = cache_gqa_kv.reshape(DEC_BATCH, DEPTH, 2, PAST_LEN, GQA_KV_WIDTH)
    cos_d, sin_d = _rope_tables()
    cos_q, sin_q = jnp.tile(cos_d, (1, GQA_Q_HEADS)), jnp.tile(sin_d, (1, GQA_Q_HEADS))
    cos_k, sin_k = jnp.tile(cos_d, (1, GQA_KV_HEADS)), jnp.tile(sin_d, (1, GQA_KV_HEADS))
    gn = S5_GROUPS * S5_STATE
    fg = final_g.astype(F32).reshape(1, D_MODEL)

    w_in_packed = _pack_w_in(w_in)
    bf16_weights = [w.astype(BF16) for w in (s5_w_glu, w_branch_a, w_branch_b, w_branch_c, w_out)]
    grouped = lambda w: w.astype(BF16).reshape((DEPTH, MOE_GROUPS, MOE_EXPERTS_PER_GROUP) + w.shape[2:])
    moe_weights = [grouped(w) for w in (moe_w_gate, moe_w_up, moe_w_down)]
    ops = jax.vmap(_s5_operators)(s5_a_re, s5_a_im, s5_log_dt, s5_b_re, s5_b_im, s5_c_re, s5_c_im)

    na_list, gqa_list, ssm_list = [], [], []
    y = None
    for l in range(DEPTH):
        z, u, new_na, new_gqa = _inproj(x, norm_g[l, 0].astype(F32).reshape(1, D_MODEL), mod, w_in_packed, l)
        na_list.append(new_na.reshape(BATCH, 2, SEQ, NA_HEADS, HEAD_DIM))
        gqa_list.append(new_gqa.reshape(BATCH, 2, SEQ, GQA_KV_HEADS, HEAD_DIM))

        sink = gqa_sink[l].astype(F32)
        oa_c, ob_c = _ctx_attn(z, sink)
        oa_l = _na_lat(z, cache_na, _na_bias_tables(na_rpb[l]), l)
        ob_l = _swa_lat(z, cache_gqa, sink, cos_q, sin_q, cos_k, sin_k, l)

        h0 = state_ssm[:, l].astype(F32).reshape(DEC_BATCH, 2, 2, gn).transpose(1, 2, 0, 3)
        yc, fin = _s5_scan(u, ops, h0, l)
        ssm_list.append(fin.transpose(2, 0, 1, 3).reshape(BATCH, 2, 2, S5_GROUPS, S5_STATE).astype(x_prompt.dtype))

        wr = jnp.zeros((D_MODEL, 128), F32)
        wr = wr.at[:, :MOE_EXPERTS].set(moe_w_expert[l].astype(F32))
        wr = wr.at[:, MOE_EXPERTS:MOE_EXPERTS + MOE_GROUPS].set(moe_w_group[l].astype(F32))
        br = jnp.zeros((1, 128), F32)
        br = br.at[0, :MOE_EXPERTS].set(moe_b_expert[l].astype(F32))
        br = br.at[0, MOE_EXPERTS:MOE_EXPERTS + MOE_GROUPS].set(moe_b_group[l].astype(F32))
        wr_hi, wr_lo = _split_bf16(wr)
        x1, hx = _merge(
            x, oa_c, oa_l, ob_c, ob_l, u, yc, z, mod, s5_d[l].astype(F32).reshape(1, S5_CH), *bf16_weights,
            norm_g[l, 1].astype(F32).reshape(1, D_MODEL), wr_hi, wr_lo, br, l)
        cls = lax.bitcast_convert_type(hx[:, D_MODEL // 2 + MOE_EXPERTS], F32).astype(jnp.int32)
        pos, tile_group, need, n_used = _moe_plan(cls)
        ys = _experts(_dispatch(hx, pos), tile_group, need, n_used, *moe_weights, l)
        if l < DEPTH - 1:
            x = _combine(ys, pos, x1, mod, fg, l, False)
        else:
            y_ctx, y_lat = _combine(ys, pos, x1, mod, fg, l, True)

    return (y_ctx.reshape(BATCH, SEQ, D_MODEL), y_lat.reshape(DEC_BATCH, DEC_SEQ, D_MODEL),
            jnp.stack(na_list, axis=1), jnp.stack(gqa_list, axis=1), jnp.stack(ssm_list, axis=1))
```

```python
import functools
import math

import numpy as np
import jax
import jax.numpy as jnp
from jax import lax
from jax.experimental import pallas as pl
from jax.experimental.pallas import tpu as pltpu

F32 = jnp.float32
BF16 = jnp.bfloat16

D_MODEL = 1024
BATCH = 32
SEQ = 256
DEPTH = 2
DEC_BATCH = 4
DEC_SEQ = 2048
PAST_LEN = 512
GRID_W = 64
GRID_ROWS = DEC_SEQ // GRID_W
HEAD_DIM = 64
NA_HEADS = 8
NA_WIN_H = 8
NA_WIN_W = 16
GQA_Q_HEADS = 8
GQA_KV_HEADS = 2
GQA_GROUP = GQA_Q_HEADS // GQA_KV_HEADS
SWA_WINDOW = 128
SWA_BLOCK = 128
ROPE_THETA = 10000.0
S5_CH = 512
S5_GROUP_CH = 16
S5_GROUPS = S5_CH // S5_GROUP_CH
S5_STATE = 64
N_BRANCH = 3
NA_WIDTH = NA_HEADS * HEAD_DIM
GQA_Q_WIDTH = GQA_Q_HEADS * HEAD_DIM
GQA_KV_WIDTH = GQA_KV_HEADS * HEAD_DIM
MOE_GROUPS = 4
MOE_EXPERTS_PER_GROUP = 4
MOE_EXPERTS = MOE_GROUPS * MOE_EXPERTS_PER_GROUP
EXPERT_FF = 512
EPS = 1e-6
NEG_INF = -1e30

N_CTX_TOK = BATCH * SEQ
N_LAT_TOK = DEC_BATCH * DEC_SEQ
N_TOK = N_CTX_TOK + N_LAT_TOK
N_COND = 8

COL_NA_Q = 0
COL_NA_K = 512
COL_NA_V = 1024
COL_G_Q = 1536
COL_G_K = 2048
COL_G_V = 2176
COL_U = 2560
COL_GATES = 3072
Z_COLS = 6144

S5_T = 16
S5_PAIRS = S5_GROUPS // 2
S5_OCT_PAIRS = 4
S5_BLOCK_TOK = 4096
S5_ROWS = S5_BLOCK_TOK // S5_T
S5_NPOW = 7

MOE_ROW_WORDS = D_MODEL // 2 + 128
MOE_TILE = 512
MOE_ROWS = N_TOK + MOE_GROUPS * MOE_TILE
MOE_TOK_BLOCK = 1024
COMBINE_PARTS = 4
MOE_PAIR_MEMBERS = ((0, 1), (0, 2), (0, 3), (1, 3), (1, 2), (2, 3))
MOE_PAIRS = len(MOE_PAIR_MEMBERS)

CTX_HEADS_PER_DOT = 4
NA_HEADS_PER_DOT = 2
NA_QROWS = 4
NA_KROWS = 12

VMEM_LIMIT = 56 * 1024 * 1024


def _dot(a, b):
    return jnp.dot(a, b, preferred_element_type=F32)


def _dot_nt(a, b):
    return lax.dot_general(a, b, (((1,), (1,)), ((), ())), preferred_element_type=F32)


def _split_bf16(x):
    hi = x.astype(BF16)
    lo = (x - hi.astype(F32)).astype(BF16)
    return hi, lo


def _pack_pairs(x):
    w = x.shape[1] // 2
    xb = x.astype(BF16).astype(F32)
    return (pltpu.bitcast(xb[:, :w], jnp.uint32) & jnp.uint32(0xFFFF0000)) | (
        pltpu.bitcast(xb[:, w:], jnp.uint32) >> 16)


def _unpack_pairs(word):
    return jnp.concatenate([pltpu.bitcast(word & jnp.uint32(0xFFFF0000), F32),
                            pltpu.bitcast(word << 16, F32)], axis=1)


def _rms(x, g):
    return x * lax.rsqrt(jnp.mean(x * x, axis=-1, keepdims=True) + EPS) * g


def _cond_index(i, tm):
    nctx = N_CTX_TOK // tm
    return jnp.where(i < nctx, 0, 1 + ((i - nctx) * tm) // DEC_SEQ)


def _ada_kernel(c_ref, w_ref, b_ref, o_ref):
    c = c_ref[...]
    s = c * jax.nn.sigmoid(c)
    s_hi, s_lo = _split_bf16(s)
    w_hi, w_lo = _split_bf16(w_ref[...])
    o_ref[...] = _dot(s_hi, w_hi) + _dot(s_lo, w_hi) + _dot(s_hi, w_lo) + b_ref[...]


def _ada_mod(cond, w_ada, b_ada):
    tn = 1536
    n = 6 * D_MODEL
    out = pl.pallas_call(
        _ada_kernel,
        out_shape=jax.ShapeDtypeStruct((DEPTH, N_COND, n), F32),
        grid=(DEPTH, n // tn),
        in_specs=[
            pl.BlockSpec((N_COND, D_MODEL), lambda l, j: (0, 0)),
            pl.BlockSpec((None, D_MODEL, tn), lambda l, j: (l, 0, j)),
            pl.BlockSpec((None, 1, tn), lambda l, j: (l, 0, j)),
        ],
        out_specs=pl.BlockSpec((None, N_COND, tn), lambda l, j: (l, 0, j)),
        compiler_params=pltpu.CompilerParams(
            dimension_semantics=("parallel", "parallel"), vmem_limit_bytes=VMEM_LIMIT),
        name="ada_mod",
    )(cond, w_ada, b_ada.reshape(DEPTH, 1, n))
    return out.reshape(DEPTH, N_COND, 6, D_MODEL)


INPROJ_TM = 1024
INPROJ_TN = 1536


def _inproj_kernel(xc_ref, xl_ref, g_ref, mod_ref, w_ref, z_ref, u_ref, cna_ref, cgq_ref, h_sc):
    i, j = pl.program_id(0), pl.program_id(1)
    is_ctx = i < N_CTX_TOK // INPROJ_TM

    def norm_mod(x_ref):
        h = _rms(x_ref[...], g_ref[...]) * (1.0 + mod_ref[1:2, :]) + mod_ref[0:1, :]
        h_sc[...] = h.astype(BF16)

    @pl.when((j == 0) & is_ctx)
    def _():
        norm_mod(xc_ref)

    @pl.when((j == 0) & jnp.logical_not(is_ctx))
    def _():
        norm_mod(xl_ref)

    acc = _dot(h_sc[...], w_ref[...])
    z_ref[...] = acc.astype(BF16)

    def rows(b):
        return slice(b * SEQ, (b + 1) * SEQ)

    def cols(c, width):
        return slice(c % INPROJ_TN, c % INPROJ_TN + width)

    @pl.when(is_ctx & (j == COL_NA_K // INPROJ_TN))
    def _():
        for b in range(INPROJ_TM // SEQ):
            cna_ref[b, 0] = acc[rows(b), cols(COL_NA_K, NA_WIDTH)]
            cna_ref[b, 1] = acc[rows(b), cols(COL_NA_V, NA_WIDTH)]

    @pl.when(is_ctx & (j == COL_G_K // INPROJ_TN))
    def _():
        for b in range(INPROJ_TM // SEQ):
            cgq_ref[b, 0] = acc[rows(b), cols(COL_G_K, GQA_KV_WIDTH)]
            cgq_ref[b, 1] = acc[rows(b), cols(COL_G_V, GQA_KV_WIDTH)]

    @pl.when(j == COL_U // INPROJ_TN)
    def _():
        u_ref[...] = acc[:, cols(COL_U, S5_CH)]


def _token_arrays(x):
    return (x, x, N_CTX_TOK) if not isinstance(x, tuple) else (x[0], x[1], 0)


def _inproj(x, g, mod, w, l):
    tm, tn = INPROJ_TM, INPROJ_TN
    nb = tm // SEQ
    last_ctx = N_CTX_TOK // tm - 1
    x_ctx, x_lat, lat0 = _token_arrays(x)
    return pl.pallas_call(
        _inproj_kernel,
        out_shape=(jax.ShapeDtypeStruct((N_TOK, Z_COLS), BF16),
                   jax.ShapeDtypeStruct((N_TOK, S5_CH), F32),
                   jax.ShapeDtypeStruct((BATCH, 2, SEQ, NA_WIDTH), F32),
                   jax.ShapeDtypeStruct((BATCH, 2, SEQ, GQA_KV_WIDTH), F32)),
        grid=(N_TOK // tm, Z_COLS // tn),
        in_specs=[
            pl.BlockSpec((tm, D_MODEL), lambda i, j: (jnp.minimum(i, last_ctx), 0)),
            pl.BlockSpec((tm, D_MODEL), lambda i, j: (jnp.maximum(i - last_ctx - 1, 0) + lat0 // tm, 0)),
            pl.BlockSpec((1, D_MODEL), lambda i, j: (0, 0)),
            pl.BlockSpec((None, None, 6, D_MODEL), lambda i, j: (l, _cond_index(i, tm), 0, 0)),
            pl.BlockSpec((None, D_MODEL, tn), lambda i, j: (l, 0, j)),
        ],
        out_specs=(pl.BlockSpec((tm, tn), lambda i, j: (i, j)),
                   pl.BlockSpec((tm, S5_CH), lambda i, j: (i, 0)),
                   pl.BlockSpec((nb, 2, SEQ, NA_WIDTH), lambda i, j: (jnp.minimum(i, last_ctx), 0, 0, 0)),
                   pl.BlockSpec((nb, 2, SEQ, GQA_KV_WIDTH), lambda i, j: (jnp.minimum(i, last_ctx), 0, 0, 0))),
        scratch_shapes=[pltpu.VMEM((tm, D_MODEL), BF16)],
        compiler_params=pltpu.CompilerParams(
            dimension_semantics=("arbitrary", "arbitrary"), vmem_limit_bytes=VMEM_LIMIT),
        name="inproj",
    )(x_ctx, x_lat, g, mod, w)


def _block_diag_heads(x, nrep):
    t = x.shape[0]
    rows = lax.broadcasted_iota(jnp.int32, (nrep * t, nrep * HEAD_DIM), 0) // t
    lanes = lax.broadcasted_iota(jnp.int32, (nrep * t, nrep * HEAD_DIM), 1) // HEAD_DIM
    return jnp.where(rows == lanes, jnp.concatenate([x] * nrep, axis=0), jnp.zeros((), x.dtype))


def _per_head(cols, shape):
    head = lax.broadcasted_iota(jnp.int32, shape, 1) // HEAD_DIM
    out = cols[-1]
    for h in range(len(cols) - 2, -1, -1):
        out = jnp.where(head == h, cols[h], out)
    return out


def _attend_heads(q, k, v, sinks):
    nh = q.shape[1] // HEAD_DIM
    t = k.shape[0]
    s = _dot_nt(q * HEAD_DIM ** -0.5, _block_diag_heads(k, nh))
    ps, ds = [], []
    for h in range(nh):
        sh = s[:, h * t:(h + 1) * t]
        m = jnp.max(sh, axis=-1, keepdims=True)
        if sinks is not None:
            m = jnp.maximum(m, sinks[h])
        p = jnp.exp(sh - m)
        d = jnp.sum(p, axis=-1, keepdims=True)
        if sinks is not None:
            d = d + jnp.exp(sinks[h] - m)
        ps.append(p.astype(BF16))
        ds.append(d)
    o = _dot(jnp.concatenate(ps, axis=1), _block_diag_heads(v, nh))
    return o / _per_head(ds, o.shape)


def _ctx_attn_kernel(sink_ref, na_ref, gq_ref, gkv_ref, oa_ref, ob_ref):
    hb = CTX_HEADS_PER_DOT
    w = hb * HEAD_DIM
    na = na_ref[...]
    gq = gq_ref[...]
    gkv = gkv_ref[...]
    for g in range(NA_HEADS // hb):
        o = _attend_heads(na[:, COL_NA_Q + g * w:COL_NA_Q + (g + 1) * w],
                          na[:, COL_NA_K + g * w:COL_NA_K + (g + 1) * w],
                          na[:, COL_NA_V + g * w:COL_NA_V + (g + 1) * w], None)
        oa_ref[:, g * w:(g + 1) * w] = o.astype(BF16)
    for g in range(GQA_Q_HEADS // hb):
        kv_heads = [(g * hb + h) // GQA_GROUP for h in range(hb)]
        k = jnp.concatenate([gkv[:, kh * HEAD_DIM:(kh + 1) * HEAD_DIM] for kh in kv_heads], axis=1)
        v = jnp.concatenate([gkv[:, GQA_KV_WIDTH + kh * HEAD_DIM:GQA_KV_WIDTH + (kh + 1) * HEAD_DIM]
                             for kh in kv_heads], axis=1)
        o = _attend_heads(gq[:, g * w:(g + 1) * w], k, v, [sink_ref[g * hb + h] for h in range(hb)])
        ob_ref[:, g * w:(g + 1) * w] = o.astype(BF16)


def _ctx_attn(z, sink):
    out = jax.ShapeDtypeStruct((N_CTX_TOK, NA_WIDTH), BF16)
    return pl.pallas_call(
        _ctx_attn_kernel,
        out_shape=(out, out),
        grid=(BATCH,),
        in_specs=[
            pl.BlockSpec(memory_space=pltpu.SMEM),
            pl.BlockSpec((SEQ, 3 * NA_WIDTH), lambda b: (b, 0)),
            pl.BlockSpec((SEQ, GQA_Q_WIDTH), lambda b: (b, COL_G_Q // GQA_Q_WIDTH)),
            pl.BlockSpec((SEQ, 2 * GQA_KV_WIDTH), lambda b: (b, COL_G_K // (2 * GQA_KV_WIDTH))),
        ],
        out_specs=(pl.BlockSpec((SEQ, NA_WIDTH), lambda b: (b, 0)),
                   pl.BlockSpec((SEQ, GQA_Q_WIDTH), lambda b: (b, 0))),
        compiler_params=pltpu.CompilerParams(
            dimension_semantics=("parallel",), vmem_limit_bytes=VMEM_LIMIT),
        name="ctx_attn",
    )(sink, z, z, z)


def _na_bias_tables(rpb):
    n_dc = 2 * NA_WIN_W - 1
    qc = np.arange(GRID_W)
    kc = np.arange(GRID_W)
    cstart = np.clip(qc - NA_WIN_W // 2, 0, GRID_W - NA_WIN_W)
    col_ok = (kc[None, :] >= cstart[:, None]) & (kc[None, :] < cstart[:, None] + NA_WIN_W)
    dc = np.clip(kc[None, :] - qc[:, None] + (NA_WIN_W - 1), 0, n_dc - 1)
    onehot = ((np.arange(n_dc)[:, None, None] == dc[None]) & col_ok[None]).astype(np.float32)
    band = jnp.einsum("hrd,dqk->hrqk", rpb.astype(F32), onehot, precision=lax.Precision.HIGHEST)
    band = jnp.where(col_ok, band, NEG_INF)
    band = jnp.pad(band, ((0, 0), (1, 1), (0, 0), (0, 0)), constant_values=NEG_INF)
    return jnp.concatenate([band[:, :-1], band[:, 1:]], axis=-1)


def _na_lat_kernel(q_ref, k_ref, v_ref, ck_ref, cv_ref, bias_ref, rowmask_ref, o_ref):
    i = pl.program_id(1)
    scale = HEAD_DIM ** -0.5
    nk = NA_KROWS * GRID_W
    ks = jnp.clip(i * NA_QROWS - NA_WIN_H // 2, 0, GRID_ROWS - NA_KROWS)
    start = pl.multiple_of(ks * GRID_W, GRID_W)
    q = q_ref[...].astype(BF16)
    kw = k_ref[pl.ds(start, nk), :].astype(BF16)
    vw = v_ref[pl.ds(start, nk), :].astype(BF16)
    ck = ck_ref[...].astype(BF16)
    cv = cv_ref[...].astype(BF16)
    planes = []
    for qr in range(NA_QROWS):
        r = i * NA_QROWS + qr
        planes.append([jnp.clip(ks + 2 * m - r + NA_WIN_H, 0, 2 * NA_WIN_H - 1) for m in range(NA_KROWS // 2)])
    hb = NA_HEADS_PER_DOT
    w = hb * HEAD_DIM
    qrow = lax.broadcasted_iota(jnp.int32, (NA_QROWS * GRID_W, 128), 0) // GRID_W
    qsel = (lax.broadcasted_iota(jnp.int32, (NA_QROWS * GRID_W, 128), 1) == qrow).astype(BF16)
    rowmask = rowmask_ref[...]
    for g in range(NA_HEADS // hb):
        lanes = slice(g * w, (g + 1) * w)
        qg = jnp.concatenate([q[:, lanes] * scale, qsel], axis=1)
        kg = jnp.concatenate([_block_diag_heads(kw[:, lanes], hb), rowmask], axis=1)
        s_loc = _dot_nt(qg, kg)
        s_ctx = _dot_nt(q[:, lanes] * scale, _block_diag_heads(ck[:, lanes], hb))
        p_locs, p_ctxs, ds = [], [], []
        for hh in range(hb):
            bias = jnp.concatenate(
                [jnp.concatenate([bias_ref[g * hb + hh, plane] for plane in row], axis=1) for row in planes], axis=0)
            sl = s_loc[:, hh * nk:(hh + 1) * nk] + bias
            sc = s_ctx[:, hh * PAST_LEN:(hh + 1) * PAST_LEN]
            m = jnp.maximum(jnp.max(sl, axis=-1, keepdims=True), jnp.max(sc, axis=-1, keepdims=True))
            p_loc = jnp.exp(sl - m)
            p_ctx = jnp.exp(sc - m)
            ds.append(jnp.sum(p_loc, axis=-1, keepdims=True) + jnp.sum(p_ctx, axis=-1, keepdims=True))
            p_locs.append(p_loc.astype(BF16))
            p_ctxs.append(p_ctx.astype(BF16))
        o = (_dot(jnp.concatenate(p_locs, axis=1), _block_diag_heads(vw[:, lanes], hb))
             + _dot(jnp.concatenate(p_ctxs, axis=1), _block_diag_heads(cv[:, lanes], hb)))
        o_ref[:, lanes] = (o / _per_head(ds, o.shape)).astype(BF16)


def _na_row_masks():
    kh = min(NA_WIN_H, GRID_ROWS)
    nsteps = GRID_ROWS // NA_QROWS
    out = np.zeros((3, NA_HEADS_PER_DOT, NA_KROWS, GRID_W, 128), np.float32)
    for pat, i in enumerate((0, 1, nsteps - 1)):
        ks = min(max(i * NA_QROWS - NA_WIN_H // 2, 0), GRID_ROWS - NA_KROWS)
        for qr in range(NA_QROWS):
            r = i * NA_QROWS + qr
            st = min(max(r - kh // 2, 0), GRID_ROWS - kh)
            for kr in range(NA_KROWS):
                if not st <= ks + kr < st + kh:
                    out[pat, :, kr, :, qr] = NEG_INF
    return jnp.asarray(out.reshape(3, NA_HEADS_PER_DOT * NA_KROWS * GRID_W, 128), BF16)


def _na_lat(z, cache_na, bias, l):
    tq = NA_QROWS * GRID_W
    nsteps = GRID_ROWS // NA_QROWS
    lat_blk = N_CTX_TOK // DEC_SEQ
    return pl.pallas_call(
        _na_lat_kernel,
        out_shape=jax.ShapeDtypeStruct((N_LAT_TOK, NA_WIDTH), BF16),
        grid=(DEC_BATCH, nsteps),
        in_specs=[
            pl.BlockSpec((tq, NA_WIDTH), lambda b, i: (N_CTX_TOK // tq + b * nsteps + i, COL_NA_Q // NA_WIDTH)),
            pl.BlockSpec((DEC_SEQ, NA_WIDTH), lambda b, i: (lat_blk + b, COL_NA_K // NA_WIDTH)),
            pl.BlockSpec((DEC_SEQ, NA_WIDTH), lambda b, i: (lat_blk + b, COL_NA_V // NA_WIDTH)),
            pl.BlockSpec((None, None, None, PAST_LEN, NA_WIDTH), lambda b, i: (b, l, 0, 0, 0)),
            pl.BlockSpec((None, None, None, PAST_LEN, NA_WIDTH), lambda b, i: (b, l, 1, 0, 0)),
            pl.BlockSpec((NA_HEADS, 2 * NA_WIN_H, GRID_W, 2 * GRID_W), lambda b, i: (0, 0, 0, 0)),
            pl.BlockSpec((None, NA_HEADS_PER_DOT * NA_KROWS * GRID_W, 128),
                         lambda b, i: (jnp.where(i == 0, 0, jnp.where(i == nsteps - 1, 2, 1)), 0, 0)),
        ],
        out_specs=pl.BlockSpec((tq, NA_WIDTH), lambda b, i: (b * nsteps + i, 0)),
        compiler_params=pltpu.CompilerParams(
            dimension_semantics=("parallel", "arbitrary"), vmem_limit_bytes=VMEM_LIMIT),
        name="na_lat",
    )(z, z, z, cache_na, cache_na, bias, _na_row_masks())


def _rope_tables():
    nf = HEAD_DIM // 4
    t = jnp.arange(DEC_SEQ)
    pos = jnp.stack([t // GRID_W, t % GRID_W], axis=-1).astype(F32)
    inv = ROPE_THETA ** (-jnp.arange(nf, dtype=F32) / nf)
    ang = pos[:, :, None] * inv
    cos = jnp.cos(ang)
    sin = jnp.sin(ang)
    cos_d = jnp.stack([cos, cos], axis=2).reshape(DEC_SEQ, HEAD_DIM)
    sin_d = jnp.stack([-sin, sin], axis=2).reshape(DEC_SEQ, HEAD_DIM)
    return cos_d, sin_d


def _rope(x, cos, sin_signed):
    n = x.shape[-1]
    nf = HEAD_DIM // 4
    lane = lax.broadcasted_iota(jnp.int32, x.shape, 1)
    first_half = (lane // nf) % 2 == 0
    partner = jnp.where(first_half, pltpu.roll(x, n - nf, 1), pltpu.roll(x, nf, 1))
    return x * cos + partner * sin_signed


def _swa_lat_kernel(sink_ref, q_ref, kv_ref, cq_ref, sq_ref, ckt_ref, skt_ref, ck_ref, cv_ref, o_ref, k_sc, v_sc):
    n = pl.program_id(1)
    scale = HEAD_DIM ** -0.5
    nwin = 3 * SWA_BLOCK

    @pl.when(n == 0)
    def _():
        kv = kv_ref[...]
        k_sc[...] = _rope(kv[:, :GQA_KV_WIDTH].astype(F32), ckt_ref[...], skt_ref[...]).astype(BF16)
        v_sc[...] = kv[:, GQA_KV_WIDTH:].astype(BF16)

    q = (_rope(q_ref[...].astype(F32), cq_ref[...], sq_ref[...]) * scale).astype(BF16)
    start = pl.multiple_of(jnp.clip((n - 1) * SWA_BLOCK, 0, DEC_SEQ - nwin), SWA_BLOCK)
    kw = k_sc[pl.ds(start, nwin), :]
    vw = v_sc[pl.ds(start, nwin), :]
    ck = ck_ref[...].astype(BF16)
    cv = cv_ref[...].astype(BF16)
    rows = GQA_GROUP * SWA_BLOCK
    row = lax.broadcasted_iota(jnp.int32, (rows, nwin), 0)
    col = lax.broadcasted_iota(jnp.int32, (rows, nwin), 1)
    qpos = n * SWA_BLOCK + row % SWA_BLOCK
    kpos = start + col
    ok = jnp.abs(qpos - kpos) <= SWA_WINDOW
    grp = lax.broadcasted_iota(jnp.int32, (rows, 1), 0) // SWA_BLOCK
    for kh in range(GQA_KV_HEADS):
        sl = slice(kh * HEAD_DIM, (kh + 1) * HEAD_DIM)
        q4 = jnp.concatenate(
            [q[:, (kh * GQA_GROUP + g) * HEAD_DIM:(kh * GQA_GROUP + g + 1) * HEAD_DIM] for g in range(GQA_GROUP)],
            axis=0)
        sink = jnp.zeros((rows, 1), F32)
        for g in range(GQA_GROUP):
            sink = jnp.where(grp == g, sink_ref[kh * GQA_GROUP + g], sink)
        s_loc = jnp.where(ok, _dot_nt(q4, kw[:, sl]), NEG_INF)
        s_ctx = _dot_nt(q4, ck[:, sl])
        m = jnp.maximum(jnp.maximum(jnp.max(s_loc, axis=-1, keepdims=True),
                                    jnp.max(s_ctx, axis=-1, keepdims=True)), sink)
        p_loc = jnp.exp(s_loc - m)
        p_ctx = jnp.exp(s_ctx - m)
        d = (jnp.sum(p_loc, axis=-1, keepdims=True) + jnp.sum(p_ctx, axis=-1, keepdims=True)
             + jnp.exp(sink - m))
        o4 = (_dot(p_loc.astype(BF16), vw[:, sl]) + _dot(p_ctx.astype(BF16), cv[:, sl])) / d
        for g in range(GQA_GROUP):
            h = kh * GQA_GROUP + g
            o_ref[:, h * HEAD_DIM:(h + 1) * HEAD_DIM] = o4[g * SWA_BLOCK:(g + 1) * SWA_BLOCK].astype(BF16)


def _swa_lat(z, cache_gqa, sink, cos_q, sin_q, cos_k, sin_k, l):
    nb = DEC_SEQ // SWA_BLOCK
    lat_blk = N_CTX_TOK // DEC_SEQ
    return pl.pallas_call(
        _swa_lat_kernel,
        out_shape=jax.ShapeDtypeStruct((N_LAT_TOK, GQA_Q_WIDTH), BF16),
        grid=(DEC_BATCH, nb),
        in_specs=[
            pl.BlockSpec(memory_space=pltpu.SMEM),
            pl.BlockSpec((SWA_BLOCK, GQA_Q_WIDTH),
                         lambda b, n: (N_CTX_TOK // SWA_BLOCK + b * nb + n, COL_G_Q // GQA_Q_WIDTH)),
            pl.BlockSpec((DEC_SEQ, 2 * GQA_KV_WIDTH), lambda b, n: (lat_blk + b, COL_G_K // (2 * GQA_KV_WIDTH))),
            pl.BlockSpec((SWA_BLOCK, GQA_Q_WIDTH), lambda b, n: (n, 0)),
            pl.BlockSpec((SWA_BLOCK, GQA_Q_WIDTH), lambda b, n: (n, 0)),
            pl.BlockSpec((DEC_SEQ, GQA_KV_WIDTH), lambda b, n: (0, 0)),
            pl.BlockSpec((DEC_SEQ, GQA_KV_WIDTH), lambda b, n: (0, 0)),
            pl.BlockSpec((None, None, None, PAST_LEN, GQA_KV_WIDTH), lambda b, n: (b, l, 0, 0, 0)),
            pl.BlockSpec((None, None, None, PAST_LEN, GQA_KV_WIDTH), lambda b, n: (b, l, 1, 0, 0)),
        ],
        out_specs=pl.BlockSpec((SWA_BLOCK, GQA_Q_WIDTH), lambda b, n: (b * nb + n, 0)),
        scratch_shapes=[pltpu.VMEM((DEC_SEQ, GQA_KV_WIDTH), BF16), pltpu.VMEM((DEC_SEQ, GQA_KV_WIDTH), BF16)],
        compiler_params=pltpu.CompilerParams(
            dimension_semantics=("parallel", "arbitrary"), vmem_limit_bytes=VMEM_LIMIT),
        name="swa_lat",
    )(sink, z, z, cos_q, sin_q, cos_k, sin_k, cache_gqa, cache_gqa)


S5_OP_KEYS = ("kt", "bt_re", "bt_im", "ct_re", "ct_im", "pin_re", "pin_im", "po_re", "po_im", "apr", "api")


def _s5_operators(a_re, a_im, log_dt, b_re, b_im, c_re, c_im):
    T, G, N, C = S5_T, S5_GROUPS, S5_STATE, S5_GROUP_CH
    tau = jnp.arange(T + 1, dtype=F32)
    out = {k: [] for k in S5_OP_KEYS}
    for d in range(2):
        A = lax.complex(a_re[d].astype(F32), a_im[d].astype(F32))
        dt = jnp.exp(log_dt[d].astype(F32))[:, None]
        a_bar = jnp.exp(A * dt)
        pw = jnp.exp((A * dt)[None] * tau[:, None, None])
        b_bar = ((a_bar - 1.0) / A)[..., None] * lax.complex(b_re[d].astype(F32), b_im[d].astype(F32))
        c_mat = lax.complex(c_re[d].astype(F32), c_im[d].astype(F32))
        kern = jnp.einsum("gon,tgn,gni->gtoi", c_mat, pw[:T], b_bar, precision=lax.Precision.HIGHEST).real
        kern = lax.optimization_barrier(kern).transpose(0, 3, 1, 2)
        if d == 0:
            p_in = pw[:T][::-1]
            p_out = pw[1:T + 1]
        else:
            p_in = pw[:T]
            p_out = pw[1:T + 1][::-1]
            kern = kern[:, :, ::-1]
        bt, ct, po = b_bar.transpose(0, 2, 1), c_mat.transpose(0, 2, 1), p_out.transpose(1, 2, 0)
        pw2 = jnp.exp((A * dt)[None] * (T * 2.0 ** jnp.arange(S5_NPOW, dtype=F32))[:, None, None])
        vals = (kern.reshape(G, C, T * C), bt.real, bt.imag, ct.real, ct.imag,
                p_in.real.reshape(T, S5_PAIRS, 1, 2 * N), p_in.imag.reshape(T, S5_PAIRS, 1, 2 * N), po.real, po.imag,
                pw2.real.reshape(S5_NPOW, G * N), pw2.imag.reshape(S5_NPOW, G * N))
        for k, v in zip(S5_OP_KEYS, vals):
            out[k].append(v)
    return {k: jnp.stack(v) for k, v in out.items()}


def _s5_expand(kt_ref, btr_ref, bti_ref, ctr_ref, cti_ref, pir_ref, pii_ref, por_ref, poi_ref,
               bs_sc, m_sc, cre_sc, cim_sc):
    T, C, N = S5_T, S5_GROUP_CH, S5_STATE
    pc, half, width = 2 * C, 2 * N, 2 * T * C
    lane = lax.broadcasted_iota(jnp.int32, (pc, width), 1)
    zc, zn, zo = jnp.zeros((C, C), F32), jnp.zeros((C, N), F32), jnp.zeros((N, C), F32)

    def own_half(x, z, g):
        return [x, z] if g == 0 else [z, x]

    def expand_pair(p, d):
        k0, bx_re, bx_im, co_re, co_im = [], [], [], [], []
        for g in range(2):
            gl = 2 * p + g
            ktg = kt_ref[d, gl]
            k0.append(jnp.concatenate(
                [blk for lag in range(T) for blk in own_half(ktg[:, lag * C:(lag + 1) * C], zc, g)], axis=1))
            bx_re.append(jnp.concatenate(own_half(btr_ref[d, gl], zn, g), axis=1))
            bx_im.append(jnp.concatenate(own_half(bti_ref[d, gl], zn, g), axis=1))
            cr, ci = ctr_ref[d, gl], cti_ref[d, gl]
            pr_all, pi_all = por_ref[d, gl], poi_ref[d, gl]
            re_p, im_p = [], []
            for j in range(T):
                pr, pi = pr_all[:, j:j + 1], pi_all[:, j:j + 1]
                re_p += own_half(cr * pr - ci * pi, zo, g)
                im_p += own_half(cr * pi + ci * pr, zo, g)
            co_re.append(jnp.concatenate(re_p, axis=1))
            co_im.append(jnp.concatenate(im_p, axis=1))
        k0 = jnp.concatenate(k0, axis=0)
        bx_re, bx_im = jnp.concatenate(bx_re, axis=0), jnp.concatenate(bx_im, axis=0)
        for s in range(T):
            if d == 0:
                sh = pc * s
                blk = jnp.where(lane >= sh, pltpu.roll(k0, sh, 1), 0.0) if sh else k0
            else:
                sh = pc * (T - 1 - s)
                blk = jnp.where(lane < width - sh, pltpu.roll(k0, width - sh, 1), 0.0) if sh else k0
            m_sc[d, p, s * pc:(s + 1) * pc, :] = blk.astype(BF16)
            pr, pi = pir_ref[d, s, p], pii_ref[d, s, p]
            bs_sc[d, p, s * pc:(s + 1) * pc, :] = jnp.concatenate(
                [pr * bx_re - pi * bx_im, pr * bx_im + pi * bx_re], axis=1).astype(BF16)
        cre_sc[d, p] = jnp.concatenate(co_re, axis=0).astype(BF16)
        cim_sc[d, p] = (-jnp.concatenate(co_im, axis=0)).astype(BF16)

    for d in range(2):
        lax.fori_loop(0, S5_OCT_PAIRS, lambda p, c, d=d: (expand_pair(p, d), c)[1], 0)


def _s5_kernel(u_ref, kt_ref, btr_ref, bti_ref, ctr_ref, cti_ref, pir_ref, pii_ref, por_ref, poi_ref, apr_ref,
               api_ref, h0_ref, y_ref, fin_ref, fin_sc, bs_sc, m_sc, cre_sc, cim_sc):
    t = pl.program_id(1)
    nctx = N_CTX_TOK // S5_BLOCK_TOK

    @pl.when(t == 0)
    def _():
        _s5_expand(kt_ref, btr_ref, bti_ref, ctr_ref, cti_ref, pir_ref, pii_ref, por_ref, poi_ref,
                   bs_sc, m_sc, cre_sc, cim_sc)

    refs = (u_ref, bs_sc, m_sc, cre_sc, cim_sc, apr_ref, api_ref, h0_ref, y_ref, fin_ref, fin_sc)

    @pl.when(t < nctx)
    def _():
        _s5_block(*refs, t, kseq=SEQ // S5_T, has_h0=False)

    @pl.when(t >= nctx)
    def _():
        _s5_block(*refs, t - nctx, kseq=DEC_SEQ // S5_T, has_h0=True)


def _s5_block(u_ref, bs_ref, m_ref, cre_ref, cim_ref, apr_ref, api_ref, h0_ref, y_ref, fin_ref, fin_sc, t, *,
              kseq, has_h0):
    rows, half, pc = S5_ROWS, 2 * S5_STATE, 2 * S5_GROUP_CH
    k = lax.broadcasted_iota(jnp.int32, (rows, half), 0) % kseq
    shifts = [1 << i for i in range(kseq.bit_length() - 1)]
    xs = [u_ref[pl.ds(s, rows, stride=S5_T), :].astype(BF16) for s in range(S5_T)]
    ys = []
    for p in range(S5_OCT_PAIRS):
        lanes = slice(p * half, (p + 1) * half)
        xp = jnp.concatenate([x[:, p * pc:(p + 1) * pc] for x in xs], axis=1)
        acc = None
        for d in range(2):
            inc = _dot(xp, bs_ref[d, p])
            sr, si = inc[:, :half], inc[:, half:]
            if has_h0:
                nseq = rows // kseq
                h0r, h0i = h0_ref[d, 0, t * nseq, :, lanes], h0_ref[d, 1, t * nseq, :, lanes]
                seq = lax.broadcasted_iota(jnp.int32, (rows, half), 0) // kseq
                for j in range(1, nseq):
                    h0r = jnp.where(seq == j, h0_ref[d, 0, t * nseq + j, :, lanes], h0r)
                    h0i = jnp.where(seq == j, h0_ref[d, 1, t * nseq + j, :, lanes], h0i)
                ar, ai = apr_ref[d, 0:1, lanes], api_ref[d, 0:1, lanes]
                first = (k == 0) if d == 0 else (k == kseq - 1)
                sr = sr + jnp.where(first, ar * h0r - ai * h0i, 0.0)
                si = si + jnp.where(first, ar * h0i + ai * h0r, 0.0)
            else:
                h0r = h0i = 0.0
            for i, sh in enumerate(shifts):
                ar, ai = apr_ref[d, i:i + 1, lanes], api_ref[d, i:i + 1, lanes]
                ok = (k >= sh) if d == 0 else (k < kseq - sh)
                amt = sh if d == 0 else rows - sh
                rr = jnp.where(ok, pltpu.roll(sr, amt, 0), 0.0)
                ri = jnp.where(ok, pltpu.roll(si, amt, 0), 0.0)
                sr, si = sr + ar * rr - ai * ri, si + ar * ri + ai * rr
            inner = (k >= 1) if d == 0 else (k < kseq - 1)
            amt = 1 if d == 0 else rows - 1
            hr = jnp.where(inner, pltpu.roll(sr, amt, 0), h0r)
            hi = jnp.where(inner, pltpu.roll(si, amt, 0), h0i)
            yd = (_dot(xp, m_ref[d, p]) + _dot(hr.astype(BF16), cre_ref[d, p])
                  + _dot(hi.astype(BF16), cim_ref[d, p]))
            acc = yd if acc is None else acc + yd
            if not has_h0:
                last = kseq - 1 if d == 0 else 0
                fin_sc[0] = sr
                fin_sc[1] = si
                fin_ref[d, 0, :, lanes] = fin_sc[0, pl.ds(last, rows // kseq, stride=kseq), :]
                fin_ref[d, 1, :, lanes] = fin_sc[1, pl.ds(last, rows // kseq, stride=kseq), :]
        ys.append(acc)
    for j in range(S5_T):
        y_ref[pl.ds(j, rows, stride=S5_T), :] = jnp.concatenate([y[:, j * pc:(j + 1) * pc] for y in ys], axis=1)


def _s5_scan(u, ops, h0, l):
    nblk, nctx = N_TOK // S5_BLOCK_TOK, N_CTX_TOK // S5_BLOCK_TOK
    noct = S5_PAIRS // S5_OCT_PAIRS
    half, gn = 2 * S5_STATE, S5_GROUPS * S5_STATE
    pw = 2 * S5_T * S5_GROUP_CH
    nseq = S5_BLOCK_TOK // SEQ
    ngrp = 2 * S5_OCT_PAIRS
    C, N, T = S5_GROUP_CH, S5_STATE, S5_T
    gspec = lambda r, c: pl.BlockSpec((None, 2, ngrp, r, c), lambda q, t: (l, 0, q, 0, 0))
    lspec = lambda r: pl.BlockSpec((None, 2, r, S5_OCT_PAIRS * half), lambda q, t: (l, 0, 0, q))
    pspec = pl.BlockSpec((None, 2, T, S5_OCT_PAIRS, 1, half), lambda q, t: (l, 0, 0, q, 0, 0))
    return pl.pallas_call(
        _s5_kernel,
        grid=(noct, nblk),
        in_specs=[
            pl.BlockSpec((S5_BLOCK_TOK, 128), lambda q, t: (t, q)),
            gspec(C, T * C), gspec(C, N), gspec(C, N), gspec(N, C), gspec(N, C), pspec, pspec,
            gspec(N, T), gspec(N, T), lspec(S5_NPOW), lspec(S5_NPOW),
            pl.BlockSpec((2, 2, DEC_BATCH, 1, S5_OCT_PAIRS * half), lambda q, t: (0, 0, 0, 0, q)),
        ],
        out_shape=(jax.ShapeDtypeStruct((N_TOK, S5_CH), F32), jax.ShapeDtypeStruct((2, 2, BATCH, gn), F32)),
        out_specs=(pl.BlockSpec((S5_BLOCK_TOK, 128), lambda q, t: (t, q)),
                   pl.BlockSpec((2, 2, nseq, S5_OCT_PAIRS * half),
                                lambda q, t: (0, 0, jnp.minimum(t, nctx - 1), q))),
        scratch_shapes=[pltpu.VMEM((2, S5_ROWS, half), F32),
                        pltpu.VMEM((2, S5_OCT_PAIRS, pw, 2 * half), BF16), pltpu.VMEM((2, S5_OCT_PAIRS, pw, pw), BF16),
                        pltpu.VMEM((2, S5_OCT_PAIRS, half, pw), BF16), pltpu.VMEM((2, S5_OCT_PAIRS, half, pw), BF16)],
        compiler_params=pltpu.CompilerParams(
            dimension_semantics=("parallel", "arbitrary"), vmem_limit_bytes=VMEM_LIMIT),
        name="s5_scan",
    )(u, *[ops[k] for k in S5_OP_KEYS], h0.reshape(2, 2, DEC_BATCH, 1, gn))


def _route(logits):
    lane = lax.broadcasted_iota(jnp.int32, logits.shape, 1)
    big = jnp.int32(1 << 20)
    is_g = (lane >= MOE_EXPERTS) & (lane < MOE_EXPERTS + MOE_GROUPS)
    lg = jnp.where(is_g, logits, NEG_INF)
    gmax = jnp.max(lg, axis=-1, keepdims=True)
    gsel = jnp.min(jnp.where(is_g & (lg == gmax), lane, big), axis=-1, keepdims=True) - MOE_EXPERTS
    p_group = 1.0 / jnp.sum(jnp.where(is_g, jnp.exp(lg - gmax), 0.0), axis=-1, keepdims=True)
    in_grp = (lane < MOE_EXPERTS) & (lane // MOE_EXPERTS_PER_GROUP == gsel)
    le = jnp.where(in_grp, logits, NEG_INF)
    v1 = jnp.max(le, axis=-1, keepdims=True)
    i1 = jnp.min(jnp.where(in_grp & (le == v1), lane, big), axis=-1, keepdims=True)
    rest = in_grp & (lane != i1)
    le2 = jnp.where(rest, logits, NEG_INF)
    v2 = jnp.max(le2, axis=-1, keepdims=True)
    i2 = jnp.min(jnp.where(rest & (le2 == v2), lane, big), axis=-1, keepdims=True)
    e2 = jnp.exp(v2 - v1)
    w1 = 1.0 / (1.0 + e2)
    w2 = e2 / (1.0 + e2)
    comb = jnp.where(lane == i1, w1 * p_group, 0.0) + jnp.where(lane == i2, w2 * p_group, 0.0)
    a = jnp.minimum(i1, i2) - gsel * MOE_EXPERTS_PER_GROUP
    b = jnp.maximum(i1, i2) - gsel * MOE_EXPERTS_PER_GROUP
    pair = jnp.where(a == 0, b - 1, jnp.where(a == 1, jnp.where(b == 3, 3, 4), 5))
    cls = gsel * MOE_PAIRS + pair
    return jnp.where(lane == MOE_EXPERTS, cls.astype(F32), comb)


def _merge_kernel(xc_ref, xl_ref, oac_ref, oal_ref, obc_ref, obl_ref, u_ref, yc_ref, gates_ref, mod_ref, d_ref,
                  wglu_ref, wa_ref, wb_ref, wc_ref, wout_ref, g2_ref, wrh_ref, wrl_ref, br_ref, x1_ref, hx_ref,
                  *, tm):
    is_ctx = pl.program_id(0) < N_CTX_TOK // tm
    x = jnp.where(is_ctx, xc_ref[...], xl_ref[...])
    oa = jnp.where(is_ctx, oac_ref[...], oal_ref[...])
    ob = jnp.where(is_ctx, obc_ref[...], obl_ref[...])
    y = u_ref[...] * d_ref[...] + yc_ref[...]
    y = y * (0.5 * (1.0 + jnp.tanh(math.sqrt(2.0 / math.pi) * (y + 0.044715 * (y * y * y)))))
    oc = y * jax.nn.sigmoid(_dot(y.astype(BF16), wglu_ref[...]))
    gate = jax.nn.sigmoid(gates_ref[...].astype(F32))
    merged = (gate[:, :D_MODEL] * _dot(oa, wa_ref[...])
              + gate[:, D_MODEL:2 * D_MODEL] * _dot(ob, wb_ref[...])
              + gate[:, 2 * D_MODEL:] * _dot(oc.astype(BF16), wc_ref[...]))
    x1 = x + mod_ref[2:3, :] * _dot(merged.astype(BF16), wout_ref[...])
    x1_ref[...] = x1
    h2 = _rms(x1, g2_ref[...]) * (1.0 + mod_ref[4:5, :]) + mod_ref[3:4, :]
    h_hi, h_lo = _split_bf16(h2)
    logits = _dot(h_hi, wrh_ref[...]) + _dot(h_lo, wrh_ref[...]) + _dot(h_hi, wrl_ref[...]) + br_ref[...]
    half = D_MODEL // 2
    hx_ref[:, :half] = _pack_pairs(h2)
    hx_ref[:, half:] = pltpu.bitcast(_route(logits), jnp.uint32)


def _merge(x, oa_c, oa_l, ob_c, ob_l, u, yc, z, mod, d_s5, wglu, wa, wb, wc, wout, g2, wr_hi, wr_lo, br, l):
    tm = 512
    nctx = N_CTX_TOK // tm
    full = lambda r, c: pl.BlockSpec((r, c), lambda i: (0, 0))
    layer = lambda r, c: pl.BlockSpec((None, r, c), lambda i: (l, 0, 0))
    ctx_blk = lambda w: pl.BlockSpec((tm, w), lambda i: (jnp.minimum(i, nctx - 1), 0))
    lat_blk = lambda w, first=0: pl.BlockSpec((tm, w), lambda i: (jnp.maximum(i - nctx, 0) + first // tm, 0))
    x_ctx, x_lat, lat0 = _token_arrays(x)
    return pl.pallas_call(
        functools.partial(_merge_kernel, tm=tm),
        out_shape=(jax.ShapeDtypeStruct((N_TOK, D_MODEL), F32),
                   jax.ShapeDtypeStruct((N_TOK, MOE_ROW_WORDS), jnp.uint32)),
        grid=(N_TOK // tm,),
        in_specs=[
            ctx_blk(D_MODEL), lat_blk(D_MODEL, lat0),
            ctx_blk(NA_WIDTH), lat_blk(NA_WIDTH), ctx_blk(GQA_Q_WIDTH), lat_blk(GQA_Q_WIDTH),
            pl.BlockSpec((tm, S5_CH), lambda i: (i, 0)),
            pl.BlockSpec((tm, S5_CH), lambda i: (i, 0)),
            pl.BlockSpec((tm, N_BRANCH * D_MODEL), lambda i: (i, COL_GATES // (N_BRANCH * D_MODEL))),
            pl.BlockSpec((None, None, 6, D_MODEL), lambda i: (l, _cond_index(i, tm), 0, 0)),
            full(1, S5_CH), layer(S5_CH, S5_CH), layer(NA_WIDTH, D_MODEL), layer(GQA_Q_WIDTH, D_MODEL),
            layer(S5_CH, D_MODEL), layer(D_MODEL, D_MODEL), full(1, D_MODEL),
            full(D_MODEL, 128), full(D_MODEL, 128), full(1, 128),
        ],
        out_specs=(pl.BlockSpec((tm, D_MODEL), lambda i: (i, 0)),
                   pl.BlockSpec((tm, MOE_ROW_WORDS), lambda i: (i, 0))),
        compiler_params=pltpu.CompilerParams(
            dimension_semantics=("parallel",), vmem_limit_bytes=VMEM_LIMIT),
        name="merge",
    )(x_ctx, x_lat, oa_c, oa_l, ob_c, ob_l, u, yc, z, mod, d_s5, wglu, wa, wb, wc, wout, g2, wr_hi, wr_lo, br)


def _moe_plan(cls):
    ncls = MOE_GROUPS * MOE_PAIRS
    ntiles = MOE_ROWS // MOE_TILE
    onehot = (cls[:, None] == jnp.arange(ncls)[None, :]).astype(jnp.int32)
    rank = jnp.sum((jnp.cumsum(onehot, axis=0) - onehot) * onehot, axis=1)
    ccount = jnp.sum(onehot, axis=0).reshape(MOE_GROUPS, MOE_PAIRS)
    gcount = jnp.sum(ccount, axis=1)
    gpadded = (gcount + MOE_TILE - 1) // MOE_TILE * MOE_TILE
    gend = jnp.cumsum(gpadded)
    cstart = ((gend - gpadded)[:, None] + jnp.cumsum(ccount, axis=1) - ccount).reshape(ncls)
    cend = cstart + ccount.reshape(ncls)
    pos = jnp.sum(onehot * cstart[None, :], axis=1) + rank
    tile_row = jnp.arange(ntiles) * MOE_TILE
    tile_group = jnp.minimum(jnp.sum((tile_row[:, None] >= gend[None, :]).astype(jnp.int32), axis=1),
                             MOE_GROUPS - 1)
    member = np.zeros((ncls, MOE_GROUPS, MOE_EXPERTS_PER_GROUP), np.float32)
    for g in range(MOE_GROUPS):
        for p, pair in enumerate(MOE_PAIR_MEMBERS):
            member[g * MOE_PAIRS + p, g, list(pair)] = 1.0
    overlap = ((cstart[None, :] < tile_row[:, None] + MOE_TILE) & (cend[None, :] > tile_row[:, None])
               & (cend > cstart)[None, :]).astype(F32)
    need = jnp.einsum("tc,cge->tge", overlap, member)
    need = jnp.sum(need * (tile_group[:, None, None] == jnp.arange(MOE_GROUPS)[None, :, None]), axis=1)
    return (pos.astype(jnp.int32), tile_group.astype(jnp.int32), (need > 0).astype(jnp.int32).reshape(-1),
            (gend[-1:] // MOE_TILE).astype(jnp.int32))


def _dispatch_kernel(pos_ref, hx_ref, init_ref, out_ref, sem):
    del init_ref
    base = pl.program_id(0) * MOE_TOK_BLOCK

    def row_copy(r):
        return pltpu.make_async_copy(hx_ref.at[pl.ds(r, 1), :], out_ref.at[pl.ds(pos_ref[base + r], 1), :], sem)

    for r in range(MOE_TOK_BLOCK):
        row_copy(r).start(priority=r % 2)
    for r in range(MOE_TOK_BLOCK):
        row_copy(r).wait()


def _dispatch(hx, pos):
    return pl.pallas_call(
        _dispatch_kernel,
        grid_spec=pltpu.PrefetchScalarGridSpec(
            num_scalar_prefetch=1,
            grid=(N_TOK // MOE_TOK_BLOCK,),
            in_specs=[pl.BlockSpec((MOE_TOK_BLOCK, MOE_ROW_WORDS), lambda i, pos: (i, 0)),
                      pl.BlockSpec(memory_space=pl.ANY)],
            out_specs=pl.BlockSpec(memory_space=pl.ANY),
            scratch_shapes=[pltpu.SemaphoreType.DMA]),
        out_shape=jax.ShapeDtypeStruct((MOE_ROWS, MOE_ROW_WORDS), jnp.uint32),
        input_output_aliases={2: 0},
        compiler_params=pltpu.CompilerParams(dimension_semantics=("arbitrary",)),
        name="moe_dispatch",
    )(pos, hx, jnp.zeros((MOE_ROWS, MOE_ROW_WORDS), jnp.uint32))


def _experts_kernel(tg_ref, need_ref, nu_ref, hx_ref, wg_ref, wu_ref, wd_ref, y_ref, acc_sc):
    t = pl.program_id(0)
    acc_sc[...] = jnp.zeros_like(acc_sc)
    half = D_MODEL // 2
    for e in range(MOE_EXPERTS_PER_GROUP):
        @pl.when((t < nu_ref[0]) & (need_ref[t * MOE_EXPERTS_PER_GROUP + e] > 0))
        def _():
            h = _unpack_pairs(hx_ref[:, :half]).astype(BF16)
            comb = pltpu.bitcast(hx_ref[:, half:], F32)
            lane = lax.broadcasted_iota(jnp.int32, comb.shape, 1)
            ce = jnp.sum(jnp.where(lane == tg_ref[t] * MOE_EXPERTS_PER_GROUP + e, comb, 0.0), axis=-1,
                         keepdims=True)
            a = _dot(h, wg_ref[e])
            b = _dot(h, wu_ref[e])
            act = (a * jax.nn.sigmoid(a)) * b * ce
            acc_sc[...] += _dot(act.astype(BF16), wd_ref[e])

    y_ref[...] = _pack_pairs(acc_sc[...])


def _experts(hs, tile_group, need, n_used, wg, wu, wd, l):
    wspec = lambda r, c: pl.BlockSpec((None, None, MOE_EXPERTS_PER_GROUP, r, c),
                                      lambda t, tg, need, nu: (l, tg[t], 0, 0, 0))
    return pl.pallas_call(
        _experts_kernel,
        grid_spec=pltpu.PrefetchScalarGridSpec(
            num_scalar_prefetch=3,
            grid=(MOE_ROWS // MOE_TILE,),
            in_specs=[pl.BlockSpec((MOE_TILE, MOE_ROW_WORDS), lambda t, tg, need, nu: (t, 0)),
                      wspec(D_MODEL, EXPERT_FF), wspec(D_MODEL, EXPERT_FF), wspec(EXPERT_FF, D_MODEL)],
            out_specs=pl.BlockSpec((MOE_TILE, D_MODEL // 2), lambda t, tg, need, nu: (t, 0)),
            scratch_shapes=[pltpu.VMEM((MOE_TILE, D_MODEL), F32)]),
        out_shape=jax.ShapeDtypeStruct((MOE_ROWS, D_MODEL // 2), jnp.uint32),
        compiler_params=pltpu.CompilerParams(
            dimension_semantics=("arbitrary",), vmem_limit_bytes=VMEM_LIMIT),
        name="moe_experts",
    )(tile_group, need, n_used, hs, wg, wu, wd)


def _combine_kernel(pos_ref, y_ref, x_ref, mod_ref, fg_ref, *rest, final):
    buf, sems = rest[-2:]
    i = pl.program_id(0)
    base = i * MOE_TOK_BLOCK
    part = MOE_TOK_BLOCK // COMBINE_PARTS
    is_ctx = i < N_CTX_TOK // MOE_TOK_BLOCK

    def row_copy(r):
        return pltpu.make_async_copy(y_ref.at[pl.ds(pos_ref[base + r], 1), :], buf.at[pl.ds(r, 1), :],
                                     sems.at[r // part])

    for r in range(MOE_TOK_BLOCK):
        row_copy(r).start(priority=r % 2)
    for q in range(COMBINE_PARTS):
        rows = slice(q * part, (q + 1) * part)
        for r in range(q * part, (q + 1) * part):
            row_copy(r).wait()
        x2 = x_ref[rows, :] + mod_ref[5:6, :] * _unpack_pairs(buf[rows, :])
        if not final:
            rest[0][rows, :] = x2
        else:
            y = _rms(x2, fg_ref[...])

            @pl.when(is_ctx)
            def _():
                rest[0][rows, :] = y

            @pl.when(jnp.logical_not(is_ctx))
            def _():
                rest[1][rows, :] = y


def _combine(ys, pos, x1, mod, fg, l, final):
    tm = MOE_TOK_BLOCK
    nctx = N_CTX_TOK // tm
    tok = pl.BlockSpec((tm, D_MODEL), lambda i, pos: (i, 0))
    out = jax.ShapeDtypeStruct((N_TOK, D_MODEL), F32)
    if final:
        out = (jax.ShapeDtypeStruct((N_CTX_TOK, D_MODEL), F32), jax.ShapeDtypeStruct((N_LAT_TOK, D_MODEL), F32))
        out_specs = (pl.BlockSpec((tm, D_MODEL), lambda i, pos: (jnp.minimum(i, nctx - 1), 0)),
                     pl.BlockSpec((tm, D_MODEL), lambda i, pos: (jnp.maximum(i - nctx, 0), 0)))
    else:
        out_specs = tok
    return pl.pallas_call(
        functools.partial(_combine_kernel, final=final),
        grid_spec=pltpu.PrefetchScalarGridSpec(
            num_scalar_prefetch=1,
            grid=(N_TOK // tm,),
            in_specs=[pl.BlockSpec(memory_space=pl.ANY), tok,
                      pl.BlockSpec((None, None, 6, D_MODEL), lambda i, pos: (l, _cond_index(i, tm), 0, 0)),
                      pl.BlockSpec((1, D_MODEL), lambda i, pos: (0, 0))],
            out_specs=out_specs,
            scratch_shapes=[pltpu.VMEM((tm, D_MODEL // 2), jnp.uint32),
                            pltpu.SemaphoreType.DMA((COMBINE_PARTS,))]),
        out_shape=out,
        compiler_params=pltpu.CompilerParams(
            dimension_semantics=("arbitrary",), vmem_limit_bytes=VMEM_LIMIT),
        name="moe_combine",
    )(pos, ys, x1, mod, fg)


def _pack_w_in(w):
    qkv = w[..., :COL_G_V + GQA_KV_WIDTH]
    u = w[..., COL_G_V + GQA_KV_WIDTH:COL_G_V + GQA_KV_WIDTH + S5_CH]
    gates = w[..., COL_G_V + GQA_KV_WIDTH + S5_CH:]
    pad = jnp.zeros(w.shape[:-1] + (COL_U - (COL_G_V + GQA_KV_WIDTH),), w.dtype)
    return jnp.concatenate([qkv, pad, u, gates], axis=-1).astype(BF16)


def kernel(x_prompt, x_sample, cache_na_kv, cache_gqa_kv, state_ssm, c, c_ctx, norm_g, w_ada, b_ada, w_in, na_rpb, gqa_sink, s5_a_re, s5_a_im, s5_log_dt, s5_b_re, s5_b_im, s5_c_re, s5_c_im, s5_d, s5_w_glu, w_branch_a, w_branch_b, w_branch_c, w_out, moe_w_group, moe_b_group, moe_w_expert, moe_b_expert, moe_w_gate, moe_w_up, moe_w_down, final_g):
    cond = jnp.zeros((N_COND, D_MODEL), F32).at[0].set(c_ctx.astype(F32)).at[1:1 + DEC_BATCH].set(c.astype(F32))
    mod = _ada_mod(cond, w_ada.astype(F32), b_ada.astype(F32))

    x = (x_prompt.astype(F32).reshape(N_CTX_TOK, D_MODEL), x_sample.astype(F32).reshape(N_LAT_TOK, D_MODEL))
    cache_na = cache_na_kv.reshape(DEC_BATCH, DEPTH, 2, PAST_LEN, NA_WIDTH)
    cache_gqa = cache_gqa_kv.reshape(DEC_BATCH, DEPTH, 2, PAST_LEN, GQA_KV_WIDTH)
    cos_d, sin_d = _rope_tables()
    cos_q, sin_q = jnp.tile(cos_d, (1, GQA_Q_HEADS)), jnp.tile(sin_d, (1, GQA_Q_HEADS))
    cos_k, sin_k = jnp.tile(cos_d, (1, GQA_KV_HEADS)), jnp.tile(sin_d, (1, GQA_KV_HEADS))
    gn = S5_GROUPS * S5_STATE
    fg = final_g.astype(F32).reshape(1, D_MODEL)

    w_in_packed = _pack_w_in(w_in)
    bf16_weights = [w.astype(BF16) for w in (s5_w_glu, w_branch_a, w_branch_b, w_branch_c, w_out)]
    grouped = lambda w: w.astype(BF16).reshape((DEPTH, MOE_GROUPS, MOE_EXPERTS_PER_GROUP) + w.shape[2:])
    moe_weights = [grouped(w) for w in (moe_w_gate, moe_w_up, moe_w_down)]
    ops = jax.vmap(_s5_operators)(s5_a_re, s5_a_im, s5_log_dt, s5_b_re, s5_b_im, s5_c_re, s5_c_im)

    na_list, gqa_list, ssm_list = [], [], []
    y = None
    for l in range(DEPTH):
        z, u, new_na, new_gqa = _inproj(x, norm_g[l, 0].astype(F32).reshape(1, D_MODEL), mod, w_in_packed, l)
        na_list.append(new_na.reshape(BATCH, 2, SEQ, NA_HEADS, HEAD_DIM))
        gqa_list.append(new_gqa.reshape(BATCH, 2, SEQ, GQA_KV_HEADS, HEAD_DIM))

        sink = gqa_sink[l].astype(F32)
        oa_c, ob_c = _ctx_attn(z, sink)
        oa_l = _na_lat(z, cache_na, _na_bias_tables(na_rpb[l]), l)
        ob_l = _swa_lat(z, cache_gqa, sink, cos_q, sin_q, cos_k, sin_k, l)

        h0 = state_ssm[:, l].astype(F32).reshape(DEC_BATCH, 2, 2, gn).transpose(1, 2, 0, 3)
        yc, fin = _s5_scan(u, ops, h0, l)
        ssm_list.append(fin.transpose(2, 0, 1, 3).reshape(BATCH, 2, 2, S5_GROUPS, S5_STATE).astype(x_prompt.dtype))

        wr = jnp.zeros((D_MODEL, 128), F32)
        wr = wr.at[:, :MOE_EXPERTS].set(moe_w_expert[l].astype(F32))
        wr = wr.at[:, MOE_EXPERTS:MOE_EXPERTS + MOE_GROUPS].set(moe_w_group[l].astype(F32))
        br = jnp.zeros((1, 128), F32)
        br = br.at[0, :MOE_EXPERTS].set(moe_b_expert[l].astype(F32))
        br = br.at[0, MOE_EXPERTS:MOE_EXPERTS + MOE_GROUPS].set(moe_b_group[l].astype(F32))
        wr_hi, wr_lo = _split_bf16(wr)
        x1, hx = _merge(
            x, oa_c, oa_l, ob_c, ob_l, u, yc, z, mod, s5_d[l].astype(F32).reshape(1, S5_CH), *bf16_weights,
            norm_g[l, 1].astype(F32).reshape(1, D_MODEL), wr_hi, wr_lo, br, l)
        cls = lax.bitcast_convert_type(hx[:, D_MODEL // 2 + MOE_EXPERTS], F32).astype(jnp.int32)
        pos, tile_group, need, n_used = _moe_plan(cls)
        ys = _experts(_dispatch(hx, pos), tile_group, need, n_used, *moe_weights, l)
        if l < DEPTH - 1:
            x = _combine(ys, pos, x1, mod, fg, l, False)
        else:
            y_ctx, y_lat = _combine(ys, pos, x1, mod, fg, l, True)

    return (y_ctx.reshape(BATCH, SEQ, D_MODEL), y_lat.reshape(DEC_BATCH, DEC_SEQ, D_MODEL),
            jnp.stack(na_list, axis=1), jnp.stack(gqa_list, axis=1), jnp.stack(ssm_list, axis=1))
```

```python
import functools
import math

import numpy as np
import jax
import jax.numpy as jnp
from jax import lax
from jax.experimental import pallas as pl
from jax.experimental.pallas import tpu as pltpu

F32 = jnp.float32
BF16 = jnp.bfloat16

D_MODEL = 1024
BATCH = 32
SEQ = 256
DEPTH = 2
DEC_BATCH = 4
DEC_SEQ = 2048
PAST_LEN = 512
GRID_W = 64
GRID_ROWS = DEC_SEQ // GRID_W
HEAD_DIM = 64
NA_HEADS = 8
NA_WIN_H = 8
NA_WIN_W = 16
GQA_Q_HEADS = 8
GQA_KV_HEADS = 2
GQA_GROUP = GQA_Q_HEADS // GQA_KV_HEADS
SWA_WINDOW = 128
SWA_BLOCK = 128
ROPE_THETA = 10000.0
S5_CH = 512
S5_GROUP_CH = 16
S5_GROUPS = S5_CH // S5_GROUP_CH
S5_STATE = 64
N_BRANCH = 3
NA_WIDTH = NA_HEADS * HEAD_DIM
GQA_Q_WIDTH = GQA_Q_HEADS * HEAD_DIM
GQA_KV_WIDTH = GQA_KV_HEADS * HEAD_DIM
MOE_GROUPS = 4
MOE_EXPERTS_PER_GROUP = 4
MOE_EXPERTS = MOE_GROUPS * MOE_EXPERTS_PER_GROUP
EXPERT_FF = 512
EPS = 1e-6
NEG_INF = -1e30

N_CTX_TOK = BATCH * SEQ
N_LAT_TOK = DEC_BATCH * DEC_SEQ
N_TOK = N_CTX_TOK + N_LAT_TOK
N_COND = 8

COL_NA_Q = 0
COL_NA_K = 512
COL_NA_V = 1024
COL_G_Q = 1536
COL_G_K = 2048
COL_G_V = 2176
COL_U = 2560
COL_GATES = 3072
Z_COLS = 6144

S5_T = 16
S5_PAIRS = S5_GROUPS // 2
S5_OCT_PAIRS = 4
S5_BLOCK_TOK = 4096
S5_ROWS = S5_BLOCK_TOK // S5_T
S5_NPOW = 7

MOE_ROW_WORDS = D_MODEL // 2 + 128
MOE_TILE = 512
MOE_ROWS = N_TOK + MOE_GROUPS * MOE_TILE
MOE_TOK_BLOCK = 1024
COMBINE_PARTS = 4
MOE_PAIR_MEMBERS = ((0, 1), (0, 2), (0, 3), (1, 3), (1, 2), (2, 3))
MOE_PAIRS = len(MOE_PAIR_MEMBERS)

CTX_HEADS_PER_DOT = 4
CTX_SEQS_PER_STEP = 2
NA_HEADS_PER_DOT = 2
NA_QROWS = 4
NA_KROWS = 12

VMEM_LIMIT = 56 * 1024 * 1024


def _dot(a, b):
    return jnp.dot(a, b, preferred_element_type=F32)


def _dot_nt(a, b):
    return lax.dot_general(a, b, (((1,), (1,)), ((), ())), preferred_element_type=F32)


def _split_bf16(x):
    hi = x.astype(BF16)
    lo = (x - hi.astype(F32)).astype(BF16)
    return hi, lo


def _pack_pairs(x):
    w = x.shape[1] // 2
    xb = x.astype(BF16).astype(F32)
    return (pltpu.bitcast(xb[:, :w], jnp.uint32) & jnp.uint32(0xFFFF0000)) | (
        pltpu.bitcast(xb[:, w:], jnp.uint32) >> 16)


def _unpack_pairs(word):
    return jnp.concatenate([pltpu.bitcast(word & jnp.uint32(0xFFFF0000), F32),
                            pltpu.bitcast(word << 16, F32)], axis=1)


def _rms(x, g):
    return x * lax.rsqrt(jnp.mean(x * x, axis=-1, keepdims=True) + EPS) * g


def _cond_index(i, tm):
    nctx = N_CTX_TOK // tm
    return jnp.where(i < nctx, 0, 1 + ((i - nctx) * tm) // DEC_SEQ)


def _ada_kernel(c_ref, w_ref, b_ref, o_ref):
    c = c_ref[...]
    s = c * jax.nn.sigmoid(c)
    s_hi, s_lo = _split_bf16(s)
    w_hi, w_lo = _split_bf16(w_ref[...])
    o_ref[...] = _dot(s_hi, w_hi) + _dot(s_lo, w_hi) + _dot(s_hi, w_lo) + b_ref[...]


def _ada_mod(cond, w_ada, b_ada):
    tn = 1536
    n = 6 * D_MODEL
    out = pl.pallas_call(
        _ada_kernel,
        out_shape=jax.ShapeDtypeStruct((DEPTH, N_COND, n), F32),
        grid=(DEPTH, n // tn),
        in_specs=[
            pl.BlockSpec((N_COND, D_MODEL), lambda l, j: (0, 0)),
            pl.BlockSpec((None, D_MODEL, tn), lambda l, j: (l, 0, j)),
            pl.BlockSpec((None, 1, tn), lambda l, j: (l, 0, j)),
        ],
        out_specs=pl.BlockSpec((None, N_COND, tn), lambda l, j: (l, 0, j)),
        compiler_params=pltpu.CompilerParams(
            dimension_semantics=("parallel", "parallel"), vmem_limit_bytes=VMEM_LIMIT),
        name="ada_mod",
    )(cond, w_ada, b_ada.reshape(DEPTH, 1, n))
    return out.reshape(DEPTH, N_COND, 6, D_MODEL)


INPROJ_TM = 1024
INPROJ_TN = 1536


def _inproj_kernel(xc_ref, xl_ref, g_ref, mod_ref, w_ref, z_ref, u_ref, cna_ref, cgq_ref, h_sc):
    i, j = pl.program_id(0), pl.program_id(1)
    is_ctx = i < N_CTX_TOK // INPROJ_TM

    def norm_mod(x_ref):
        h = _rms(x_ref[...], g_ref[...]) * (1.0 + mod_ref[1:2, :]) + mod_ref[0:1, :]
        h_sc[...] = h.astype(BF16)

    @pl.when((j == 0) & is_ctx)
    def _():
        norm_mod(xc_ref)

    @pl.when((j == 0) & jnp.logical_not(is_ctx))
    def _():
        norm_mod(xl_ref)

    acc = _dot(h_sc[...], w_ref[...])
    z_ref[...] = acc.astype(BF16)

    def rows(b):
        return slice(b * SEQ, (b + 1) * SEQ)

    def cols(c, width):
        return slice(c % INPROJ_TN, c % INPROJ_TN + width)

    @pl.when(is_ctx & (j == COL_NA_K // INPROJ_TN))
    def _():
        for b in range(INPROJ_TM // SEQ):
            cna_ref[b, 0] = acc[rows(b), cols(COL_NA_K, NA_WIDTH)]
            cna_ref[b, 1] = acc[rows(b), cols(COL_NA_V, NA_WIDTH)]

    @pl.when(is_ctx & (j == COL_G_K // INPROJ_TN))
    def _():
        for b in range(INPROJ_TM // SEQ):
            cgq_ref[b, 0] = acc[rows(b), cols(COL_G_K, GQA_KV_WIDTH)]
            cgq_ref[b, 1] = acc[rows(b), cols(COL_G_V, GQA_KV_WIDTH)]

    @pl.when(j == COL_U // INPROJ_TN)
    def _():
        u_ref[...] = acc[:, cols(COL_U, S5_CH)]


def _token_arrays(x):
    return (x, x, N_CTX_TOK) if not isinstance(x, tuple) else (x[0], x[1], 0)


def _inproj(x, g, mod, w, l):
    tm, tn = INPROJ_TM, INPROJ_TN
    nb = tm // SEQ
    last_ctx = N_CTX_TOK // tm - 1
    x_ctx, x_lat, lat0 = _token_arrays(x)
    return pl.pallas_call(
        _inproj_kernel,
        out_shape=(jax.ShapeDtypeStruct((N_TOK, Z_COLS), BF16),
                   jax.ShapeDtypeStruct((N_TOK, S5_CH), F32),
                   jax.ShapeDtypeStruct((BATCH, 2, SEQ, NA_WIDTH), F32),
                   jax.ShapeDtypeStruct((BATCH, 2, SEQ, GQA_KV_WIDTH), F32)),
        grid=(N_TOK // tm, Z_COLS // tn),
        in_specs=[
            pl.BlockSpec((tm, D_MODEL), lambda i, j: (jnp.minimum(i, last_ctx), 0)),
            pl.BlockSpec((tm, D_MODEL), lambda i, j: (jnp.maximum(i - last_ctx - 1, 0) + lat0 // tm, 0)),
            pl.BlockSpec((1, D_MODEL), lambda i, j: (0, 0)),
            pl.BlockSpec((None, None, 6, D_MODEL), lambda i, j: (l, _cond_index(i, tm), 0, 0)),
            pl.BlockSpec((None, D_MODEL, tn), lambda i, j: (l, 0, j)),
        ],
        out_specs=(pl.BlockSpec((tm, tn), lambda i, j: (i, j)),
                   pl.BlockSpec((tm, S5_CH), lambda i, j: (i, 0)),
                   pl.BlockSpec((nb, 2, SEQ, NA_WIDTH), lambda i, j: (jnp.minimum(i, last_ctx), 0, 0, 0)),
                   pl.BlockSpec((nb, 2, SEQ, GQA_KV_WIDTH), lambda i, j: (jnp.minimum(i, last_ctx), 0, 0, 0))),
        scratch_shapes=[pltpu.VMEM((tm, D_MODEL), BF16)],
        compiler_params=pltpu.CompilerParams(
            dimension_semantics=("arbitrary", "arbitrary"), vmem_limit_bytes=VMEM_LIMIT),
        name="inproj",
    )(x_ctx, x_lat, g, mod, w)


def _block_diag_heads(x, nrep):
    t = x.shape[0]
    rows = lax.broadcasted_iota(jnp.int32, (nrep * t, nrep * HEAD_DIM), 0) // t
    lanes = lax.broadcasted_iota(jnp.int32, (nrep * t, nrep * HEAD_DIM), 1) // HEAD_DIM
    return jnp.where(rows == lanes, jnp.concatenate([x] * nrep, axis=0), jnp.zeros((), x.dtype))


def _per_head(cols, shape):
    head = lax.broadcasted_iota(jnp.int32, shape, 1) // HEAD_DIM
    out = cols[-1]
    for h in range(len(cols) - 2, -1, -1):
        out = jnp.where(head == h, cols[h], out)
    return out


def _attend_heads(q, k, v, sinks):
    nh = q.shape[1] // HEAD_DIM
    t = k.shape[0]
    s = _dot_nt(q * HEAD_DIM ** -0.5, _block_diag_heads(k, nh))
    ps, ds = [], []
    for h in range(nh):
        sh = s[:, h * t:(h + 1) * t]
        m = jnp.max(sh, axis=-1, keepdims=True)
        if sinks is not None:
            m = jnp.maximum(m, sinks[h])
        p = jnp.exp(sh - m)
        d = jnp.sum(p, axis=-1, keepdims=True)
        if sinks is not None:
            d = d + jnp.exp(sinks[h] - m)
        ps.append(p.astype(BF16))
        ds.append(d)
    o = _dot(jnp.concatenate(ps, axis=1), _block_diag_heads(v, nh))
    return o / _per_head(ds, o.shape)


def _ctx_attn_kernel(sink_ref, na_ref, gq_ref, gkv_ref, oa_ref, ob_ref):
    hb = CTX_HEADS_PER_DOT
    w = hb * HEAD_DIM
    for sq in range(CTX_SEQS_PER_STEP):
        rows = slice(sq * SEQ, (sq + 1) * SEQ)
        na = na_ref[rows, :]
        gq = gq_ref[rows, :]
        gkv = gkv_ref[rows, :]
        for g in range(NA_HEADS // hb):
            o = _attend_heads(na[:, COL_NA_Q + g * w:COL_NA_Q + (g + 1) * w],
                              na[:, COL_NA_K + g * w:COL_NA_K + (g + 1) * w],
                              na[:, COL_NA_V + g * w:COL_NA_V + (g + 1) * w], None)
            oa_ref[rows, g * w:(g + 1) * w] = o.astype(BF16)
        for g in range(GQA_Q_HEADS // hb):
            kv_heads = [(g * hb + h) // GQA_GROUP for h in range(hb)]
            k = jnp.concatenate([gkv[:, kh * HEAD_DIM:(kh + 1) * HEAD_DIM] for kh in kv_heads], axis=1)
            v = jnp.concatenate([gkv[:, GQA_KV_WIDTH + kh * HEAD_DIM:GQA_KV_WIDTH + (kh + 1) * HEAD_DIM]
                                 for kh in kv_heads], axis=1)
            o = _attend_heads(gq[:, g * w:(g + 1) * w], k, v, [sink_ref[g * hb + h] for h in range(hb)])
            ob_ref[rows, g * w:(g + 1) * w] = o.astype(BF16)


def _ctx_attn(z, sink):
    out = jax.ShapeDtypeStruct((N_CTX_TOK, NA_WIDTH), BF16)
    tq = CTX_SEQS_PER_STEP * SEQ
    return pl.pallas_call(
        _ctx_attn_kernel,
        out_shape=(out, out),
        grid=(N_CTX_TOK // tq,),
        in_specs=[
            pl.BlockSpec(memory_space=pltpu.SMEM),
            pl.BlockSpec((tq, 3 * NA_WIDTH), lambda b: (b, 0)),
            pl.BlockSpec((tq, GQA_Q_WIDTH), lambda b: (b, COL_G_Q // GQA_Q_WIDTH)),
            pl.BlockSpec((tq, 2 * GQA_KV_WIDTH), lambda b: (b, COL_G_K // (2 * GQA_KV_WIDTH))),
        ],
        out_specs=(pl.BlockSpec((tq, NA_WIDTH), lambda b: (b, 0)),
                   pl.BlockSpec((tq, GQA_Q_WIDTH), lambda b: (b, 0))),
        compiler_params=pltpu.CompilerParams(
            dimension_semantics=("parallel",), vmem_limit_bytes=VMEM_LIMIT),
        name="ctx_attn",
    )(sink, z, z, z)


def _na_bias_tables(rpb):
    n_dc = 2 * NA_WIN_W - 1
    qc = np.arange(GRID_W)
    kc = np.arange(GRID_W)
    cstart = np.clip(qc - NA_WIN_W // 2, 0, GRID_W - NA_WIN_W)
    col_ok = (kc[None, :] >= cstart[:, None]) & (kc[None, :] < cstart[:, None] + NA_WIN_W)
    dc = np.clip(kc[None, :] - qc[:, None] + (NA_WIN_W - 1), 0, n_dc - 1)
    onehot = ((np.arange(n_dc)[:, None, None] == dc[None]) & col_ok[None]).astype(np.float32)
    band = jnp.einsum("hrd,dqk->hrqk", rpb.astype(F32), onehot, precision=lax.Precision.HIGHEST)
    band = jnp.where(col_ok, band, NEG_INF)
    band = jnp.pad(band, ((0, 0), (1, 1), (0, 0), (0, 0)), constant_values=NEG_INF)
    return jnp.concatenate([band[:, :-1], band[:, 1:]], axis=-1)


def _na_lat_kernel(q_ref, k_ref, v_ref, ck_ref, cv_ref, bias_ref, rowmask_ref, o_ref):
    i = pl.program_id(1)
    scale = HEAD_DIM ** -0.5
    nk = NA_KROWS * GRID_W
    ks = jnp.clip(i * NA_QROWS - NA_WIN_H // 2, 0, GRID_ROWS - NA_KROWS)
    start = pl.multiple_of(ks * GRID_W, GRID_W)
    q = q_ref[...].astype(BF16)
    kw = k_ref[pl.ds(start, nk), :].astype(BF16)
    vw = v_ref[pl.ds(start, nk), :].astype(BF16)
    ck = ck_ref[...].astype(BF16)
    cv = cv_ref[...].astype(BF16)
    planes = []
    for qr in range(NA_QROWS):
        r = i * NA_QROWS + qr
        planes.append([jnp.clip(ks + 2 * m - r + NA_WIN_H, 0, 2 * NA_WIN_H - 1) for m in range(NA_KROWS // 2)])
    hb = NA_HEADS_PER_DOT
    w = hb * HEAD_DIM
    qrow = lax.broadcasted_iota(jnp.int32, (NA_QROWS * GRID_W, 128), 0) // GRID_W
    qsel = (lax.broadcasted_iota(jnp.int32, (NA_QROWS * GRID_W, 128), 1) == qrow).astype(BF16)
    rowmask = rowmask_ref[...]
    for g in range(NA_HEADS // hb):
        lanes = slice(g * w, (g + 1) * w)
        qg = jnp.concatenate([q[:, lanes] * scale, qsel], axis=1)
        kg = jnp.concatenate([_block_diag_heads(kw[:, lanes], hb), rowmask], axis=1)
        s_loc = _dot_nt(qg, kg)
        s_ctx = _dot_nt(q[:, lanes] * scale, _block_diag_heads(ck[:, lanes], hb))
        p_locs, p_ctxs, ds = [], [], []
        for hh in range(hb):
            bias = jnp.concatenate(
                [jnp.concatenate([bias_ref[g * hb + hh, plane] for plane in row], axis=1) for row in planes], axis=0)
            sl = s_loc[:, hh * nk:(hh + 1) * nk] + bias
            sc = s_ctx[:, hh * PAST_LEN:(hh + 1) * PAST_LEN]
            m = jnp.maximum(jnp.max(sl, axis=-1, keepdims=True), jnp.max(sc, axis=-1, keepdims=True))
            p_loc = jnp.exp(sl - m)
            p_ctx = jnp.exp(sc - m)
            ds.append(jnp.sum(p_loc, axis=-1, keepdims=True) + jnp.sum(p_ctx, axis=-1, keepdims=True))
            p_locs.append(p_loc.astype(BF16))
            p_ctxs.append(p_ctx.astype(BF16))
        o = (_dot(jnp.concatenate(p_locs, axis=1), _block_diag_heads(vw[:, lanes], hb))
             + _dot(jnp.concatenate(p_ctxs, axis=1), _block_diag_heads(cv[:, lanes], hb)))
        o_ref[:, lanes] = (o / _per_head(ds, o.shape)).astype(BF16)


def _na_row_masks():
    kh = min(NA_WIN_H, GRID_ROWS)
    nsteps = GRID_ROWS // NA_QROWS
    out = np.zeros((3, NA_HEADS_PER_DOT, NA_KROWS, GRID_W, 128), np.float32)
    for pat, i in enumerate((0, 1, nsteps - 1)):
        ks = min(max(i * NA_QROWS - NA_WIN_H // 2, 0), GRID_ROWS - NA_KROWS)
        for qr in range(NA_QROWS):
            r = i * NA_QROWS + qr
            st = min(max(r - kh // 2, 0), GRID_ROWS - kh)
            for kr in range(NA_KROWS):
                if not st <= ks + kr < st + kh:
                    out[pat, :, kr, :, qr] = NEG_INF
    return jnp.asarray(out.reshape(3, NA_HEADS_PER_DOT * NA_KROWS * GRID_W, 128), BF16)


def _na_lat(z, cache_na, bias, l):
    tq = NA_QROWS * GRID_W
    nsteps = GRID_ROWS // NA_QROWS
    lat_blk = N_CTX_TOK // DEC_SEQ
    return pl.pallas_call(
        _na_lat_kernel,
        out_shape=jax.ShapeDtypeStruct((N_LAT_TOK, NA_WIDTH), BF16),
        grid=(DEC_BATCH, nsteps),
        in_specs=[
            pl.BlockSpec((tq, NA_WIDTH), lambda b, i: (N_CTX_TOK // tq + b * nsteps + i, COL_NA_Q // NA_WIDTH)),
            pl.BlockSpec((DEC_SEQ, NA_WIDTH), lambda b, i: (lat_blk + b, COL_NA_K // NA_WIDTH)),
            pl.BlockSpec((DEC_SEQ, NA_WIDTH), lambda b, i: (lat_blk + b, COL_NA_V // NA_WIDTH)),
            pl.BlockSpec((None, None, None, PAST_LEN, NA_WIDTH), lambda b, i: (b, l, 0, 0, 0)),
            pl.BlockSpec((None, None, None, PAST_LEN, NA_WIDTH), lambda b, i: (b, l, 1, 0, 0)),
            pl.BlockSpec((NA_HEADS, 2 * NA_WIN_H, GRID_W, 2 * GRID_W), lambda b, i: (0, 0, 0, 0)),
            pl.BlockSpec((None, NA_HEADS_PER_DOT * NA_KROWS * GRID_W, 128),
                         lambda b, i: (jnp.where(i == 0, 0, jnp.where(i == nsteps - 1, 2, 1)), 0, 0)),
        ],
        out_specs=pl.BlockSpec((tq, NA_WIDTH), lambda b, i: (b * nsteps + i, 0)),
        compiler_params=pltpu.CompilerParams(
            dimension_semantics=("parallel", "arbitrary"), vmem_limit_bytes=VMEM_LIMIT),
        name="na_lat",
    )(z, z, z, cache_na, cache_na, bias, _na_row_masks())


def _rope_tables():
    nf = HEAD_DIM // 4
    t = jnp.arange(DEC_SEQ)
    pos = jnp.stack([t // GRID_W, t % GRID_W], axis=-1).astype(F32)
    inv = ROPE_THETA ** (-jnp.arange(nf, dtype=F32) / nf)
    ang = pos[:, :, None] * inv
    cos = jnp.cos(ang)
    sin = jnp.sin(ang)
    cos_d = jnp.stack([cos, cos], axis=2).reshape(DEC_SEQ, HEAD_DIM)
    sin_d = jnp.stack([-sin, sin], axis=2).reshape(DEC_SEQ, HEAD_DIM)
    return cos_d, sin_d


def _rope(x, cos, sin_signed):
    n = x.shape[-1]
    nf = HEAD_DIM // 4
    lane = lax.broadcasted_iota(jnp.int32, x.shape, 1)
    first_half = (lane // nf) % 2 == 0
    partner = jnp.where(first_half, pltpu.roll(x, n - nf, 1), pltpu.roll(x, nf, 1))
    return x * cos + partner * sin_signed


def _swa_lat_kernel(sink_ref, q_ref, kv_ref, cq_ref, sq_ref, ckt_ref, skt_ref, ck_ref, cv_ref, o_ref, k_sc, v_sc):
    n = pl.program_id(1)
    scale = HEAD_DIM ** -0.5
    nwin = 3 * SWA_BLOCK

    @pl.when(n == 0)
    def _():
        kv = kv_ref[...]
        k_sc[...] = _rope(kv[:, :GQA_KV_WIDTH].astype(F32), ckt_ref[...], skt_ref[...]).astype(BF16)
        v_sc[...] = kv[:, GQA_KV_WIDTH:].astype(BF16)

    q = (_rope(q_ref[...].astype(F32), cq_ref[...], sq_ref[...]) * scale).astype(BF16)
    start = pl.multiple_of(jnp.clip((n - 1) * SWA_BLOCK, 0, DEC_SEQ - nwin), SWA_BLOCK)
    kw = k_sc[pl.ds(start, nwin), :]
    vw = v_sc[pl.ds(start, nwin), :]
    ck = ck_ref[...].astype(BF16)
    cv = cv_ref[...].astype(BF16)
    rows = GQA_GROUP * SWA_BLOCK
    row = lax.broadcasted_iota(jnp.int32, (rows, nwin), 0)
    col = lax.broadcasted_iota(jnp.int32, (rows, nwin), 1)
    qpos = n * SWA_BLOCK + row % SWA_BLOCK
    kpos = start + col
    ok = jnp.abs(qpos - kpos) <= SWA_WINDOW
    grp = lax.broadcasted_iota(jnp.int32, (rows, 1), 0) // SWA_BLOCK
    for kh in range(GQA_KV_HEADS):
        sl = slice(kh * HEAD_DIM, (kh + 1) * HEAD_DIM)
        q4 = jnp.concatenate(
            [q[:, (kh * GQA_GROUP + g) * HEAD_DIM:(kh * GQA_GROUP + g + 1) * HEAD_DIM] for g in range(GQA_GROUP)],
            axis=0)
        sink = jnp.zeros((rows, 1), F32)
        for g in range(GQA_GROUP):
            sink = jnp.where(grp == g, sink_ref[kh * GQA_GROUP + g], sink)
        s_loc = jnp.where(ok, _dot_nt(q4, kw[:, sl]), NEG_INF)
        s_ctx = _dot_nt(q4, ck[:, sl])
        m = jnp.maximum(jnp.maximum(jnp.max(s_loc, axis=-1, keepdims=True),
                                    jnp.max(s_ctx, axis=-1, keepdims=True)), sink)
        p_loc = jnp.exp(s_loc - m)
        p_ctx = jnp.exp(s_ctx - m)
        d = (jnp.sum(p_loc, axis=-1, keepdims=True) + jnp.sum(p_ctx, axis=-1, keepdims=True)
             + jnp.exp(sink - m))
        o4 = (_dot(p_loc.astype(BF16), vw[:, sl]) + _dot(p_ctx.astype(BF16), cv[:, sl])) / d
        for g in range(GQA_GROUP):
            h = kh * GQA_GROUP + g
            o_ref[:, h * HEAD_DIM:(h + 1) * HEAD_DIM] = o4[g * SWA_BLOCK:(g + 1) * SWA_BLOCK].astype(BF16)


def _swa_lat(z, cache_gqa, sink, cos_q, sin_q, cos_k, sin_k, l):
    nb = DEC_SEQ // SWA_BLOCK
    lat_blk = N_CTX_TOK // DEC_SEQ
    return pl.pallas_call(
        _swa_lat_kernel,
        out_shape=jax.ShapeDtypeStruct((N_LAT_TOK, GQA_Q_WIDTH), BF16),
        grid=(DEC_BATCH, nb),
        in_specs=[
            pl.BlockSpec(memory_space=pltpu.SMEM),
            pl.BlockSpec((SWA_BLOCK, GQA_Q_WIDTH),
                         lambda b, n: (N_CTX_TOK // SWA_BLOCK + b * nb + n, COL_G_Q // GQA_Q_WIDTH)),
            pl.BlockSpec((DEC_SEQ, 2 * GQA_KV_WIDTH), lambda b, n: (lat_blk + b, COL_G_K // (2 * GQA_KV_WIDTH))),
            pl.BlockSpec((SWA_BLOCK, GQA_Q_WIDTH), lambda b, n: (n, 0)),
            pl.BlockSpec((SWA_BLOCK, GQA_Q_WIDTH), lambda b, n: (n, 0)),
            pl.BlockSpec((DEC_SEQ, GQA_KV_WIDTH), lambda b, n: (0, 0)),
            pl.BlockSpec((DEC_SEQ, GQA_KV_WIDTH), lambda b, n: (0, 0)),
            pl.BlockSpec((None, None, None, PAST_LEN, GQA_KV_WIDTH), lambda b, n: (b, l, 0, 0, 0)),
            pl.BlockSpec((None, None, None, PAST_LEN, GQA_KV_WIDTH), lambda b, n: (b, l, 1, 0, 0)),
        ],
        out_specs=pl.BlockSpec((SWA_BLOCK, GQA_Q_WIDTH), lambda b, n: (b * nb + n, 0)),
        scratch_shapes=[pltpu.VMEM((DEC_SEQ, GQA_KV_WIDTH), BF16), pltpu.VMEM((DEC_SEQ, GQA_KV_WIDTH), BF16)],
        compiler_params=pltpu.CompilerParams(
            dimension_semantics=("parallel", "arbitrary"), vmem_limit_bytes=VMEM_LIMIT),
        name="swa_lat",
    )(sink, z, z, cos_q, sin_q, cos_k, sin_k, cache_gqa, cache_gqa)


S5_OP_KEYS = ("kt", "bt_re", "bt_im", "ct_re", "ct_im", "pin_re", "pin_im", "po_re", "po_im", "apr", "api")


def _s5_operators(a_re, a_im, log_dt, b_re, b_im, c_re, c_im):
    T, G, N, C = S5_T, S5_GROUPS, S5_STATE, S5_GROUP_CH
    tau = jnp.arange(T + 1, dtype=F32)
    out = {k: [] for k in S5_OP_KEYS}
    for d in range(2):
        A = lax.complex(a_re[d].astype(F32), a_im[d].astype(F32))
        dt = jnp.exp(log_dt[d].astype(F32))[:, None]
        a_bar = jnp.exp(A * dt)
        pw = jnp.exp((A * dt)[None] * tau[:, None, None])
        b_bar = ((a_bar - 1.0) / A)[..., None] * lax.complex(b_re[d].astype(F32), b_im[d].astype(F32))
        c_mat = lax.complex(c_re[d].astype(F32), c_im[d].astype(F32))
        kern = jnp.einsum("gon,tgn,gni->gtoi", c_mat, pw[:T], b_bar, precision=lax.Precision.HIGHEST).real
        kern = lax.optimization_barrier(kern).transpose(0, 3, 1, 2)
        if d == 0:
            p_in = pw[:T][::-1]
            p_out = pw[1:T + 1]
        else:
            p_in = pw[:T]
            p_out = pw[1:T + 1][::-1]
            kern = kern[:, :, ::-1]
        bt, ct, po = b_bar.transpose(0, 2, 1), c_mat.transpose(0, 2, 1), p_out.transpose(1, 2, 0)
        pw2 = jnp.exp((A * dt)[None] * (T * 2.0 ** jnp.arange(S5_NPOW, dtype=F32))[:, None, None])
        vals = (kern.reshape(G, C, T * C), bt.real, bt.imag, ct.real, ct.imag,
                p_in.real.reshape(T, S5_PAIRS, 1, 2 * N), p_in.imag.reshape(T, S5_PAIRS, 1, 2 * N), po.real, po.imag,
                pw2.real.reshape(S5_NPOW, G * N), pw2.imag.reshape(S5_NPOW, G * N))
        for k, v in zip(S5_OP_KEYS, vals):
            out[k].append(v)
    return {k: jnp.stack(v) for k, v in out.items()}


def _s5_expand(kt_ref, btr_ref, bti_ref, ctr_ref, cti_ref, pir_ref, pii_ref, por_ref, poi_ref,
               bs_sc, m_sc, cre_sc, cim_sc):
    T, C, N = S5_T, S5_GROUP_CH, S5_STATE
    pc, half, width = 2 * C, 2 * N, 2 * T * C
    lane = lax.broadcasted_iota(jnp.int32, (pc, width), 1)
    zc, zn, zo = jnp.zeros((C, C), F32), jnp.zeros((C, N), F32), jnp.zeros((N, C), F32)

    def own_half(x, z, g):
        return [x, z] if g == 0 else [z, x]

    def expand_pair(p, d):
        k0, bx_re, bx_im, co_re, co_im = [], [], [], [], []
        for g in range(2):
            gl = 2 * p + g
            ktg = kt_ref[d, gl]
            k0.append(jnp.concatenate(
                [blk for lag in range(T) for blk in own_half(ktg[:, lag * C:(lag + 1) * C], zc, g)], axis=1))
            bx_re.append(jnp.concatenate(own_half(btr_ref[d, gl], zn, g), axis=1))
            bx_im.append(jnp.concatenate(own_half(bti_ref[d, gl], zn, g), axis=1))
            cr, ci = ctr_ref[d, gl], cti_ref[d, gl]
            pr_all, pi_all = por_ref[d, gl], poi_ref[d, gl]
            re_p, im_p = [], []
            for j in range(T):
                pr, pi = pr_all[:, j:j + 1], pi_all[:, j:j + 1]
                re_p += own_half(cr * pr - ci * pi, zo, g)
                im_p += own_half(cr * pi + ci * pr, zo, g)
            co_re.append(jnp.concatenate(re_p, axis=1))
            co_im.append(jnp.concatenate(im_p, axis=1))
        k0 = jnp.concatenate(k0, axis=0)
        bx_re, bx_im = jnp.concatenate(bx_re, axis=0), jnp.concatenate(bx_im, axis=0)
        for s in range(T):
            if d == 0:
                sh = pc * s
                blk = jnp.where(lane >= sh, pltpu.roll(k0, sh, 1), 0.0) if sh else k0
            else:
                sh = pc * (T - 1 - s)
                blk = jnp.where(lane < width - sh, pltpu.roll(k0, width - sh, 1), 0.0) if sh else k0
            m_sc[d, p, s * pc:(s + 1) * pc, :] = blk.astype(BF16)
            pr, pi = pir_ref[d, s, p], pii_ref[d, s, p]
            bs_sc[d, p, s * pc:(s + 1) * pc, :] = jnp.concatenate(
                [pr * bx_re - pi * bx_im, pr * bx_im + pi * bx_re], axis=1).astype(BF16)
        cre_sc[d, p] = jnp.concatenate(co_re, axis=0).astype(BF16)
        cim_sc[d, p] = (-jnp.concatenate(co_im, axis=0)).astype(BF16)

    for d in range(2):
        lax.fori_loop(0, S5_OCT_PAIRS, lambda p, c, d=d: (expand_pair(p, d), c)[1], 0)


def _s5_kernel(u_ref, kt_ref, btr_ref, bti_ref, ctr_ref, cti_ref, pir_ref, pii_ref, por_ref, poi_ref, apr_ref,
               api_ref, h0_ref, y_ref, fin_ref, fin_sc, bs_sc, m_sc, cre_sc, cim_sc):
    t = pl.program_id(1)
    nctx = N_CTX_TOK // S5_BLOCK_TOK

    @pl.when(t == 0)
    def _():
        _s5_expand(kt_ref, btr_ref, bti_ref, ctr_ref, cti_ref, pir_ref, pii_ref, por_ref, poi_ref,
                   bs_sc, m_sc, cre_sc, cim_sc)

    refs = (u_ref, bs_sc, m_sc, cre_sc, cim_sc, apr_ref, api_ref, h0_ref, y_ref, fin_ref, fin_sc)

    @pl.when(t < nctx)
    def _():
        _s5_block(*refs, t, kseq=SEQ // S5_T, has_h0=False)

    @pl.when(t >= nctx)
    def _():
        _s5_block(*refs, t - nctx, kseq=DEC_SEQ // S5_T, has_h0=True)


def _s5_block(u_ref, bs_ref, m_ref, cre_ref, cim_ref, apr_ref, api_ref, h0_ref, y_ref, fin_ref, fin_sc, t, *,
              kseq, has_h0):
    rows, half, pc = S5_ROWS, 2 * S5_STATE, 2 * S5_GROUP_CH
    k = lax.broadcasted_iota(jnp.int32, (rows, half), 0) % kseq
    shifts = [1 << i for i in range(kseq.bit_length() - 1)]
    xs = [u_ref[pl.ds(s, rows, stride=S5_T), :].astype(BF16) for s in range(S5_T)]
    ys = []
    for p in range(S5_OCT_PAIRS):
        lanes = slice(p * half, (p + 1) * half)
        xp = jnp.concatenate([x[:, p * pc:(p + 1) * pc] for x in xs], axis=1)
        acc = None
        for d in range(2):
            inc = _dot(xp, bs_ref[d, p])
            sr, si = inc[:, :half], inc[:, half:]
            if has_h0:
                nseq = rows // kseq
                h0r, h0i = h0_ref[d, 0, t * nseq, :, lanes], h0_ref[d, 1, t * nseq, :, lanes]
                seq = lax.broadcasted_iota(jnp.int32, (rows, half), 0) // kseq
                for j in range(1, nseq):
                    h0r = jnp.where(seq == j, h0_ref[d, 0, t * nseq + j, :, lanes], h0r)
                    h0i = jnp.where(seq == j, h0_ref[d, 1, t * nseq + j, :, lanes], h0i)
                ar, ai = apr_ref[d, 0:1, lanes], api_ref[d, 0:1, lanes]
                first = (k == 0) if d == 0 else (k == kseq - 1)
                sr = sr + jnp.where(first, ar * h0r - ai * h0i, 0.0)
                si = si + jnp.where(first, ar * h0i + ai * h0r, 0.0)
            else:
                h0r = h0i = 0.0
            for i, sh in enumerate(shifts):
                ar, ai = apr_ref[d, i:i + 1, lanes], api_ref[d, i:i + 1, lanes]
                ok = (k >= sh) if d == 0 else (k < kseq - sh)
                amt = sh if d == 0 else rows - sh
                rr = jnp.where(ok, pltpu.roll(sr, amt, 0), 0.0)
                ri = jnp.where(ok, pltpu.roll(si, amt, 0), 0.0)
                sr, si = sr + ar * rr - ai * ri, si + ar * ri + ai * rr
            inner = (k >= 1) if d == 0 else (k < kseq - 1)
            amt = 1 if d == 0 else rows - 1
            hr = jnp.where(inner, pltpu.roll(sr, amt, 0), h0r)
            hi = jnp.where(inner, pltpu.roll(si, amt, 0), h0i)
            yd = (_dot(xp, m_ref[d, p]) + _dot(hr.astype(BF16), cre_ref[d, p])
                  + _dot(hi.astype(BF16), cim_ref[d, p]))
            acc = yd if acc is None else acc + yd
            if not has_h0:
                last = kseq - 1 if d == 0 else 0
                fin_sc[0] = sr
                fin_sc[1] = si
                fin_ref[d, 0, :, lanes] = fin_sc[0, pl.ds(last, rows // kseq, stride=kseq), :]
                fin_ref[d, 1, :, lanes] = fin_sc[1, pl.ds(last, rows // kseq, stride=kseq), :]
        ys.append(acc)
    for j in range(S5_T):
        y_ref[pl.ds(j, rows, stride=S5_T), :] = jnp.concatenate([y[:, j * pc:(j + 1) * pc] for y in ys], axis=1)


def _s5_scan(u, ops, h0, l):
    nblk, nctx = N_TOK // S5_BLOCK_TOK, N_CTX_TOK // S5_BLOCK_TOK
    noct = S5_PAIRS // S5_OCT_PAIRS
    half, gn = 2 * S5_STATE, S5_GROUPS * S5_STATE
    pw = 2 * S5_T * S5_GROUP_CH
    nseq = S5_BLOCK_TOK // SEQ
    ngrp = 2 * S5_OCT_PAIRS
    C, N, T = S5_GROUP_CH, S5_STATE, S5_T
    gspec = lambda r, c: pl.BlockSpec((None, 2, ngrp, r, c), lambda q, t: (l, 0, q, 0, 0))
    lspec = lambda r: pl.BlockSpec((None, 2, r, S5_OCT_PAIRS * half), lambda q, t: (l, 0, 0, q))
    pspec = pl.BlockSpec((None, 2, T, S5_OCT_PAIRS, 1, half), lambda q, t: (l, 0, 0, q, 0, 0))
    return pl.pallas_call(
        _s5_kernel,
        grid=(noct, nblk),
        in_specs=[
            pl.BlockSpec((S5_BLOCK_TOK, 128), lambda q, t: (t, q)),
            gspec(C, T * C), gspec(C, N), gspec(C, N), gspec(N, C), gspec(N, C), pspec, pspec,
            gspec(N, T), gspec(N, T), lspec(S5_NPOW), lspec(S5_NPOW),
            pl.BlockSpec((2, 2, DEC_BATCH, 1, S5_OCT_PAIRS * half), lambda q, t: (0, 0, 0, 0, q)),
        ],
        out_shape=(jax.ShapeDtypeStruct((N_TOK, S5_CH), F32), jax.ShapeDtypeStruct((2, 2, BATCH, gn), F32)),
        out_specs=(pl.BlockSpec((S5_BLOCK_TOK, 128), lambda q, t: (t, q)),
                   pl.BlockSpec((2, 2, nseq, S5_OCT_PAIRS * half),
                                lambda q, t: (0, 0, jnp.minimum(t, nctx - 1), q))),
        scratch_shapes=[pltpu.VMEM((2, S5_ROWS, half), F32),
                        pltpu.VMEM((2, S5_OCT_PAIRS, pw, 2 * half), BF16), pltpu.VMEM((2, S5_OCT_PAIRS, pw, pw), BF16),
                        pltpu.VMEM((2, S5_OCT_PAIRS, half, pw), BF16), pltpu.VMEM((2, S5_OCT_PAIRS, half, pw), BF16)],
        compiler_params=pltpu.CompilerParams(
            dimension_semantics=("parallel", "arbitrary"), vmem_limit_bytes=VMEM_LIMIT),
        name="s5_scan",
    )(u, *[ops[k] for k in S5_OP_KEYS], h0.reshape(2, 2, DEC_BATCH, 1, gn))


def _route(logits):
    lane = lax.broadcasted_iota(jnp.int32, logits.shape, 1)
    big = jnp.int32(1 << 20)
    is_g = (lane >= MOE_EXPERTS) & (lane < MOE_EXPERTS + MOE_GROUPS)
    lg = jnp.where(is_g, logits, NEG_INF)
    gmax = jnp.max(lg, axis=-1, keepdims=True)
    gsel = jnp.min(jnp.where(is_g & (lg == gmax), lane, big), axis=-1, keepdims=True) - MOE_EXPERTS
    p_group = 1.0 / jnp.sum(jnp.where(is_g, jnp.exp(lg - gmax), 0.0), axis=-1, keepdims=True)
    in_grp = (lane < MOE_EXPERTS) & (lane // MOE_EXPERTS_PER_GROUP == gsel)
    le = jnp.where(in_grp, logits, NEG_INF)
    v1 = jnp.max(le, axis=-1, keepdims=True)
    i1 = jnp.min(jnp.where(in_grp & (le == v1), lane, big), axis=-1, keepdims=True)
    rest = in_grp & (lane != i1)
    le2 = jnp.where(rest, logits, NEG_INF)
    v2 = jnp.max(le2, axis=-1, keepdims=True)
    i2 = jnp.min(jnp.where(rest & (le2 == v2), lane, big), axis=-1, keepdims=True)
    e2 = jnp.exp(v2 - v1)
    w1 = 1.0 / (1.0 + e2)
    w2 = e2 / (1.0 + e2)
    comb = jnp.where(lane == i1, w1 * p_group, 0.0) + jnp.where(lane == i2, w2 * p_group, 0.0)
    a = jnp.minimum(i1, i2) - gsel * MOE_EXPERTS_PER_GROUP
    b = jnp.maximum(i1, i2) - gsel * MOE_EXPERTS_PER_GROUP
    pair = jnp.where(a == 0, b - 1, jnp.where(a == 1, jnp.where(b == 3, 3, 4), 5))
    cls = gsel * MOE_PAIRS + pair
    return jnp.where(lane == MOE_EXPERTS, cls.astype(F32), comb)


def _merge_kernel(xc_ref, xl_ref, oac_ref, oal_ref, obc_ref, obl_ref, u_ref, yc_ref, gates_ref, mod_ref, d_ref,
                  wglu_ref, wa_ref, wb_ref, wc_ref, wout_ref, g2_ref, wrh_ref, wrl_ref, br_ref, x1_ref, hx_ref,
                  *, tm):
    is_ctx = pl.program_id(0) < N_CTX_TOK // tm
    x = jnp.where(is_ctx, xc_ref[...], xl_ref[...])
    oa = jnp.where(is_ctx, oac_ref[...], oal_ref[...])
    ob = jnp.where(is_ctx, obc_ref[...], obl_ref[...])
    y = u_ref[...] * d_ref[...] + yc_ref[...]
    y = y * (0.5 * (1.0 + jnp.tanh(math.sqrt(2.0 / math.pi) * (y + 0.044715 * (y * y * y)))))
    oc = y * jax.nn.sigmoid(_dot(y.astype(BF16), wglu_ref[...]))
    gate = jax.nn.sigmoid(gates_ref[...].astype(F32))
    merged = (gate[:, :D_MODEL] * _dot(oa, wa_ref[...])
              + gate[:, D_MODEL:2 * D_MODEL] * _dot(ob, wb_ref[...])
              + gate[:, 2 * D_MODEL:] * _dot(oc.astype(BF16), wc_ref[...]))
    x1 = x + mod_ref[2:3, :] * _dot(merged.astype(BF16), wout_ref[...])
    x1_ref[...] = x1
    h2 = _rms(x1, g2_ref[...]) * (1.0 + mod_ref[4:5, :]) + mod_ref[3:4, :]
    h_hi, h_lo = _split_bf16(h2)
    logits = _dot(h_hi, wrh_ref[...]) + _dot(h_lo, wrh_ref[...]) + _dot(h_hi, wrl_ref[...]) + br_ref[...]
    half = D_MODEL // 2
    hx_ref[:, :half] = _pack_pairs(h2)
    hx_ref[:, half:] = pltpu.bitcast(_route(logits), jnp.uint32)


def _merge(x, oa_c, oa_l, ob_c, ob_l, u, yc, z, mod, d_s5, wglu, wa, wb, wc, wout, g2, wr_hi, wr_lo, br, l):
    tm = 512
    nctx = N_CTX_TOK // tm
    full = lambda r, c: pl.BlockSpec((r, c), lambda i: (0, 0))
    layer = lambda r, c: pl.BlockSpec((None, r, c), lambda i: (l, 0, 0))
    ctx_blk = lambda w: pl.BlockSpec((tm, w), lambda i: (jnp.minimum(i, nctx - 1), 0))
    lat_blk = lambda w, first=0: pl.BlockSpec((tm, w), lambda i: (jnp.maximum(i - nctx, 0) + first // tm, 0))
    x_ctx, x_lat, lat0 = _token_arrays(x)
    return pl.pallas_call(
        functools.partial(_merge_kernel, tm=tm),
        out_shape=(jax.ShapeDtypeStruct((N_TOK, D_MODEL), F32),
                   jax.ShapeDtypeStruct((N_TOK, MOE_ROW_WORDS), jnp.uint32)),
        grid=(N_TOK // tm,),
        in_specs=[
            ctx_blk(D_MODEL), lat_blk(D_MODEL, lat0),
            ctx_blk(NA_WIDTH), lat_blk(NA_WIDTH), ctx_blk(GQA_Q_WIDTH), lat_blk(GQA_Q_WIDTH),
            pl.BlockSpec((tm, S5_CH), lambda i: (i, 0)),
            pl.BlockSpec((tm, S5_CH), lambda i: (i, 0)),
            pl.BlockSpec((tm, N_BRANCH * D_MODEL), lambda i: (i, COL_GATES // (N_BRANCH * D_MODEL))),
            pl.BlockSpec((None, None, 6, D_MODEL), lambda i: (l, _cond_index(i, tm), 0, 0)),
            full(1, S5_CH), layer(S5_CH, S5_CH), layer(NA_WIDTH, D_MODEL), layer(GQA_Q_WIDTH, D_MODEL),
            layer(S5_CH, D_MODEL), layer(D_MODEL, D_MODEL), full(1, D_MODEL),
            full(D_MODEL, 128), full(D_MODEL, 128), full(1, 128),
        ],
        out_specs=(pl.BlockSpec((tm, D_MODEL), lambda i: (i, 0)),
                   pl.BlockSpec((tm, MOE_ROW_WORDS), lambda i: (i, 0))),
        compiler_params=pltpu.CompilerParams(
            dimension_semantics=("parallel",), vmem_limit_bytes=VMEM_LIMIT),
        name="merge",
    )(x_ctx, x_lat, oa_c, oa_l, ob_c, ob_l, u, yc, z, mod, d_s5, wglu, wa, wb, wc, wout, g2, wr_hi, wr_lo, br)


def _moe_plan(cls):
    ncls = MOE_GROUPS * MOE_PAIRS
    ntiles = MOE_ROWS // MOE_TILE
    onehot = (cls[:, None] == jnp.arange(ncls)[None, :]).astype(jnp.int32)
    rank = jnp.sum((jnp.cumsum(onehot, axis=0) - onehot) * onehot, axis=1)
    ccount = jnp.sum(onehot, axis=0).reshape(MOE_GROUPS, MOE_PAIRS)
    gcount = jnp.sum(ccount, axis=1)
    gpadded = (gcount + MOE_TILE - 1) // MOE_TILE * MOE_TILE
    gend = jnp.cumsum(gpadded)
    cstart = ((gend - gpadded)[:, None] + jnp.cumsum(ccount, axis=1) - ccount).reshape(ncls)
    cend = cstart + ccount.reshape(ncls)
    pos = jnp.sum(onehot * cstart[None, :], axis=1) + rank
    tile_row = jnp.arange(ntiles) * MOE_TILE
    tile_group = jnp.minimum(jnp.sum((tile_row[:, None] >= gend[None, :]).astype(jnp.int32), axis=1),
                             MOE_GROUPS - 1)
    member = np.zeros((ncls, MOE_GROUPS, MOE_EXPERTS_PER_GROUP), np.float32)
    for g in range(MOE_GROUPS):
        for p, pair in enumerate(MOE_PAIR_MEMBERS):
            member[g * MOE_PAIRS + p, g, list(pair)] = 1.0
    overlap = ((cstart[None, :] < tile_row[:, None] + MOE_TILE) & (cend[None, :] > tile_row[:, None])
               & (cend > cstart)[None, :]).astype(F32)
    need = jnp.einsum("tc,cge->tge", overlap, member)
    need = jnp.sum(need * (tile_group[:, None, None] == jnp.arange(MOE_GROUPS)[None, :, None]), axis=1)
    return (pos.astype(jnp.int32), tile_group.astype(jnp.int32), (need > 0).astype(jnp.int32).reshape(-1),
            (gend[-1:] // MOE_TILE).astype(jnp.int32))


def _dispatch_kernel(pos_ref, hx_ref, init_ref, out_ref, sem):
    del init_ref
    base = pl.program_id(0) * MOE_TOK_BLOCK

    def row_copy(r):
        return pltpu.make_async_copy(hx_ref.at[pl.ds(r, 1), :], out_ref.at[pl.ds(pos_ref[base + r], 1), :], sem)

    for r in range(MOE_TOK_BLOCK):
        row_copy(r).start(priority=r % 2)
    for r in range(MOE_TOK_BLOCK):
        row_copy(r).wait()


def _dispatch(hx, pos):
    return pl.pallas_call(
        _dispatch_kernel,
        grid_spec=pltpu.PrefetchScalarGridSpec(
            num_scalar_prefetch=1,
            grid=(N_TOK // MOE_TOK_BLOCK,),
            in_specs=[pl.BlockSpec((MOE_TOK_BLOCK, MOE_ROW_WORDS), lambda i, pos: (i, 0)),
                      pl.BlockSpec(memory_space=pl.ANY)],
            out_specs=pl.BlockSpec(memory_space=pl.ANY),
            scratch_shapes=[pltpu.SemaphoreType.DMA]),
        out_shape=jax.ShapeDtypeStruct((MOE_ROWS, MOE_ROW_WORDS), jnp.uint32),
        input_output_aliases={2: 0},
        compiler_params=pltpu.CompilerParams(dimension_semantics=("arbitrary",)),
        name="moe_dispatch",
    )(pos, hx, jnp.zeros((MOE_ROWS, MOE_ROW_WORDS), jnp.uint32))


def _experts_kernel(tg_ref, need_ref, nu_ref, hx_ref, wg_ref, wu_ref, wd_ref, y_ref, acc_sc, h_sc):
    t = pl.program_id(0)
    acc_sc[...] = jnp.zeros_like(acc_sc)
    half = D_MODEL // 2
    h_sc[...] = _unpack_pairs(hx_ref[:, :half]).astype(BF16)
    for e in range(MOE_EXPERTS_PER_GROUP):
        @pl.when((t < nu_ref[0]) & (need_ref[t * MOE_EXPERTS_PER_GROUP + e] > 0))
        def _():
            h = h_sc[...]
            comb = pltpu.bitcast(hx_ref[:, half:], F32)
            lane = lax.broadcasted_iota(jnp.int32, comb.shape, 1)
            ce = jnp.sum(jnp.where(lane == tg_ref[t] * MOE_EXPERTS_PER_GROUP + e, comb, 0.0), axis=-1,
                         keepdims=True)
            a = _dot(h, wg_ref[e])
            b = _dot(h, wu_ref[e])
            act = (a * jax.nn.sigmoid(a)) * b * ce
            acc_sc[...] += _dot(act.astype(BF16), wd_ref[e])

    y_ref[...] = _pack_pairs(acc_sc[...])


def _experts(hs, tile_group, need, n_used, wg, wu, wd, l):
    wspec = lambda r, c: pl.BlockSpec((None, None, MOE_EXPERTS_PER_GROUP, r, c),
                                      lambda t, tg, need, nu: (l, tg[t], 0, 0, 0))
    return pl.pallas_call(
        _experts_kernel,
        grid_spec=pltpu.PrefetchScalarGridSpec(
            num_scalar_prefetch=3,
            grid=(MOE_ROWS // MOE_TILE,),
            in_specs=[pl.BlockSpec((MOE_TILE, MOE_ROW_WORDS), lambda t, tg, need, nu: (t, 0)),
                      wspec(D_MODEL, EXPERT_FF), wspec(D_MODEL, EXPERT_FF), wspec(EXPERT_FF, D_MODEL)],
            out_specs=pl.BlockSpec((MOE_TILE, D_MODEL // 2), lambda t, tg, need, nu: (t, 0)),
            scratch_shapes=[pltpu.VMEM((MOE_TILE, D_MODEL), F32), pltpu.VMEM((MOE_TILE, D_MODEL), BF16)]),
        out_shape=jax.ShapeDtypeStruct((MOE_ROWS, D_MODEL // 2), jnp.uint32),
        compiler_params=pltpu.CompilerParams(
            dimension_semantics=("arbitrary",), vmem_limit_bytes=VMEM_LIMIT),
        name="moe_experts",
    )(tile_group, need, n_used, hs, wg, wu, wd)


def _combine_kernel(pos_ref, y_ref, x_ref, mod_ref, fg_ref, *rest, final):
    buf, sems = rest[-2:]
    i = pl.program_id(0)
    base = i * MOE_TOK_BLOCK
    part = MOE_TOK_BLOCK // COMBINE_PARTS
    is_ctx = i < N_CTX_TOK // MOE_TOK_BLOCK

    def row_copy(r):
        return pltpu.make_async_copy(y_ref.at[pl.ds(pos_ref[base + r], 1), :], buf.at[pl.ds(r, 1), :],
                                     sems.at[r // part])

    for r in range(MOE_TOK_BLOCK):
        row_copy(r).start(priority=r % 2)
    for q in range(COMBINE_PARTS):
        rows = slice(q * part, (q + 1) * part)
        for r in range(q * part, (q + 1) * part):
            row_copy(r).wait()
        x2 = x_ref[rows, :] + mod_ref[5:6, :] * _unpack_pairs(buf[rows, :])
        if not final:
            rest[0][rows, :] = x2
        else:
            y = _rms(x2, fg_ref[...])

            @pl.when(is_ctx)
            def _():
                rest[0][rows, :] = y

            @pl.when(jnp.logical_not(is_ctx))
            def _():
                rest[1][rows, :] = y


def _combine(ys, pos, x1, mod, fg, l, final):
    tm = MOE_TOK_BLOCK
    nctx = N_CTX_TOK // tm
    tok = pl.BlockSpec((tm, D_MODEL), lambda i, pos: (i, 0))
    out = jax.ShapeDtypeStruct((N_TOK, D_MODEL), F32)
    if final:
        out = (jax.ShapeDtypeStruct((N_CTX_TOK, D_MODEL), F32), jax.ShapeDtypeStruct((N_LAT_TOK, D_MODEL), F32))
        out_specs = (pl.BlockSpec((tm, D_MODEL), lambda i, pos: (jnp.minimum(i, nctx - 1), 0)),
                     pl.BlockSpec((tm, D_MODEL), lambda i, pos: (jnp.maximum(i - nctx, 0), 0)))
    else:
        out_specs = tok
    return pl.pallas_call(
        functools.partial(_combine_kernel, final=final),
        grid_spec=pltpu.PrefetchScalarGridSpec(
            num_scalar_prefetch=1,
            grid=(N_TOK // tm,),
            in_specs=[pl.BlockSpec(memory_space=pl.ANY), tok,
                      pl.BlockSpec((None, None, 6, D_MODEL), lambda i, pos: (l, _cond_index(i, tm), 0, 0)),
                      pl.BlockSpec((1, D_MODEL), lambda i, pos: (0, 0))],
            out_specs=out_specs,
            scratch_shapes=[pltpu.VMEM((tm, D_MODEL // 2), jnp.uint32),
                            pltpu.SemaphoreType.DMA((COMBINE_PARTS,))]),
        out_shape=out,
        compiler_params=pltpu.CompilerParams(
            dimension_semantics=("arbitrary",), vmem_limit_bytes=VMEM_LIMIT),
        name="moe_combine",
    )(pos, ys, x1, mod, fg)


def _pack_w_in(w):
    qkv = w[..., :COL_G_V + GQA_KV_WIDTH]
    u = w[..., COL_G_V + GQA_KV_WIDTH:COL_G_V + GQA_KV_WIDTH + S5_CH]
    gates = w[..., COL_G_V + GQA_KV_WIDTH + S5_CH:]
    pad = jnp.zeros(w.shape[:-1] + (COL_U - (COL_G_V + GQA_KV_WIDTH),), w.dtype)
    return jnp.concatenate([qkv, pad, u, gates], axis=-1).astype(BF16)


def kernel(x_prompt, x_sample, cache_na_kv, cache_gqa_kv, state_ssm, c, c_ctx, norm_g, w_ada, b_ada, w_in, na_rpb, gqa_sink, s5_a_re, s5_a_im, s5_log_dt, s5_b_re, s5_b_im, s5_c_re, s5_c_im, s5_d, s5_w_glu, w_branch_a, w_branch_b, w_branch_c, w_out, moe_w_group, moe_b_group, moe_w_expert, moe_b_expert, moe_w_gate, moe_w_up, moe_w_down, final_g):
    cond = jnp.zeros((N_COND, D_MODEL), F32).at[0].set(c_ctx.astype(F32)).at[1:1 + DEC_BATCH].set(c.astype(F32))
    mod = _ada_mod(cond, w_ada.astype(F32), b_ada.astype(F32))

    x = (x_prompt.astype(F32).reshape(N_CTX_TOK, D_MODEL), x_sample.astype(F32).reshape(N_LAT_TOK, D_MODEL))
    cache_na = cache_na_kv.reshape(DEC_BATCH, DEPTH, 2, PAST_LEN, NA_WIDTH)
    cache_gqa = cache_gqa_kv.reshape(DEC_BATCH, DEPTH, 2, PAST_LEN, GQA_KV_WIDTH)
    cos_d, sin_d = _rope_tables()
    cos_q, sin_q = jnp.tile(cos_d, (1, GQA_Q_HEADS)), jnp.tile(sin_d, (1, GQA_Q_HEADS))
    cos_k, sin_k = jnp.tile(cos_d, (1, GQA_KV_HEADS)), jnp.tile(sin_d, (1, GQA_KV_HEADS))
    gn = S5_GROUPS * S5_STATE
    fg = final_g.astype(F32).reshape(1, D_MODEL)

    w_in_packed = _pack_w_in(w_in)
    bf16_weights = [w.astype(BF16) for w in (s5_w_glu, w_branch_a, w_branch_b, w_branch_c, w_out)]
    grouped = lambda w: w.astype(BF16).reshape((DEPTH, MOE_GROUPS, MOE_EXPERTS_PER_GROUP) + w.shape[2:])
    moe_weights = [grouped(w) for w in (moe_w_gate, moe_w_up, moe_w_down)]
    ops = jax.vmap(_s5_operators)(s5_a_re, s5_a_im, s5_log_dt, s5_b_re, s5_b_im, s5_c_re, s5_c_im)

    na_list, gqa_list, ssm_list = [], [], []
    y = None
    for l in range(DEPTH):
        z, u, new_na, new_gqa = _inproj(x, norm_g[l, 0].astype(F32).reshape(1, D_MODEL), mod, w_in_packed, l)
        na_list.append(new_na.reshape(BATCH, 2, SEQ, NA_HEADS, HEAD_DIM))
        gqa_list.append(new_gqa.reshape(BATCH, 2, SEQ, GQA_KV_HEADS, HEAD_DIM))

        sink = gqa_sink[l].astype(F32)
        oa_c, ob_c = _ctx_attn(z, sink)
        oa_l = _na_lat(z, cache_na, _na_bias_tables(na_rpb[l]), l)
        ob_l = _swa_lat(z, cache_gqa, sink, cos_q, sin_q, cos_k, sin_k, l)

        h0 = state_ssm[:, l].astype(F32).reshape(DEC_BATCH, 2, 2, gn).transpose(1, 2, 0, 3)
        yc, fin = _s5_scan(u, ops, h0, l)
        ssm_list.append(fin.transpose(2, 0, 1, 3).reshape(BATCH, 2, 2, S5_GROUPS, S5_STATE).astype(x_prompt.dtype))

        wr = jnp.zeros((D_MODEL, 128), F32)
        wr = wr.at[:, :MOE_EXPERTS].set(moe_w_expert[l].astype(F32))
        wr = wr.at[:, MOE_EXPERTS:MOE_EXPERTS + MOE_GROUPS].set(moe_w_group[l].astype(F32))
        br = jnp.zeros((1, 128), F32)
        br = br.at[0, :MOE_EXPERTS].set(moe_b_expert[l].astype(F32))
        br = br.at[0, MOE_EXPERTS:MOE_EXPERTS + MOE_GROUPS].set(moe_b_group[l].astype(F32))
        wr_hi, wr_lo = _split_bf16(wr)
        x1, hx = _merge(
            x, oa_c, oa_l, ob_c, ob_l, u, yc, z, mod, s5_d[l].astype(F32).reshape(1, S5_CH), *bf16_weights,
            norm_g[l, 1].astype(F32).reshape(1, D_MODEL), wr_hi, wr_lo, br, l)
        cls = lax.bitcast_convert_type(hx[:, D_MODEL // 2 + MOE_EXPERTS], F32).astype(jnp.int32)
        pos, tile_group, need, n_used = _moe_plan(cls)
        ys = _experts(_dispatch(hx, pos), tile_group, need, n_used, *moe_weights, l)
        if l < DEPTH - 1:
            x = _combine(ys, pos, x1, mod, fg, l, False)
        else:
            y_ctx, y_lat = _combine(ys, pos, x1, mod, fg, l, True)

    return (y_ctx.reshape(BATCH, SEQ, D_MODEL), y_lat.reshape(DEC_BATCH, DEC_SEQ, D_MODEL),
            jnp.stack(na_list, axis=1), jnp.stack(gqa_list, axis=1), jnp.stack(ssm_list, axis=1))
```
